```python
import math, functools
import jax, jax.numpy as jnp
from jax import lax
import numpy as np

D_MODEL = 2048
BATCH = 8
SEQ = 4096
DEPTH = 1

CHUNK = 64
D_LRU = 1024
LRU_HEADS = 16
LRU_HEAD_DIM = D_LRU // LRU_HEADS
LRU_CONV = 4
LRU_C = 8.0
D_SC = 1024
SC_CONV = 3
D_FF = 5632
FFN_CONV = 3
N_BRANCH = 2
EPS = 1e-6
IN_COLS = 2 * D_LRU + 3 * D_SC + N_BRANCH * D_MODEL

kernel_name = "hybrid_rglru_shortconv_convffn_block"


def rmsnorm(x, g):
    xf = x.astype(jnp.float32)
    y = xf * lax.rsqrt(jnp.mean(xf * xf, axis=-1, keepdims=True) + EPS)
    return (y * g.astype(jnp.float32)).astype(x.dtype)


def causal_dwconv(x, w):
    k_w = w.shape[0]
    s = x.shape[1]
    xp = jnp.pad(x, ((0, 0), (k_w - 1, 0), (0, 0)))
    y = xp[:, 0:s] * w[0]
    for k in range(1, k_w):
        y = y + xp[:, k:k + s] * w[k]
    return y


def _lin_combine(left, right):
    a1, b1 = left
    a2, b2 = right
    return a1 * a2, a2 * b1 + b2


def rg_lru(x, w_a, b_a, w_x, b_x, lam):
    bsz, s, d = x.shape
    xf = x.astype(jnp.float32)
    xh = xf.reshape(bsz, s, LRU_HEADS, LRU_HEAD_DIM)
    r = jax.nn.sigmoid(jnp.einsum('bshi,hij->bshj', xh, w_a.astype(jnp.float32)).reshape(bsz, s, d) + b_a.astype(jnp.float32))
    i = jax.nn.sigmoid(jnp.einsum('bshi,hij->bshj', xh, w_x.astype(jnp.float32)).reshape(bsz, s, d) + b_x.astype(jnp.float32))
    log_a = -LRU_C * jax.nn.softplus(-lam.astype(jnp.float32)) * r
    a = jnp.exp(log_a)
    u = jnp.sqrt(-jnp.expm1(2.0 * log_a)) * (i * xf)
    n_chunks = s // CHUNK
    a_c = a.reshape(bsz, n_chunks, CHUNK, d).transpose(1, 0, 2, 3)
    u_c = u.reshape(bsz, n_chunks, CHUNK, d).transpose(1, 0, 2, 3)

    def step(h0, inp):
        ac, uc = inp
        a_cum, b_cum = lax.associative_scan(_lin_combine, (ac, uc), axis=1)
        h = a_cum * h0[:, None, :] + b_cum
        return h[:, -1], h

    h_init = jnp.zeros((bsz, d), jnp.float32)
    _, hs = lax.scan(step, h_init, (a_c, u_c))
    return hs.transpose(1, 0, 2, 3).reshape(bsz, s, d).astype(x.dtype)


def _fwd_setup_inputs(seed: int = 0) -> dict:
    key = jax.random.key(seed)
    ks = jax.random.split(key, 20)
    f32 = jnp.float32
    nrm = lambda k, shp, fan: jax.random.normal(k, shp, f32) * (fan ** -0.5)
    a0 = jax.random.uniform(ks[9], (D_LRU,), f32, 0.9, 0.999)
    return {
        "x": jax.random.normal(ks[0], (BATCH, SEQ, D_MODEL), f32),
        "g_mix": 1.0 + 0.02 * jax.random.normal(ks[1], (D_MODEL,), f32),
        "w_in": nrm(ks[2], (D_MODEL, IN_COLS), D_MODEL),
        "lru_conv_w": nrm(ks[3], (LRU_CONV, D_LRU), LRU_CONV),
        "lru_conv_b": 0.01 * jax.random.normal(ks[4], (D_LRU,), f32),
        "lru_wa": nrm(ks[5], (LRU_HEADS, LRU_HEAD_DIM, LRU_HEAD_DIM), LRU_HEAD_DIM),
        "lru_ba": 0.01 * jax.random.normal(ks[6], (D_LRU,), f32),
        "lru_wx": nrm(ks[7], (LRU_HEADS, LRU_HEAD_DIM, LRU_HEAD_DIM), LRU_HEAD_DIM),
        "lru_bx": 0.01 * jax.random.normal(ks[8], (D_LRU,), f32),
        "lru_lambda": jnp.log(a0) - jnp.log1p(-a0),
        "lru_w_out": nrm(ks[10], (D_LRU, D_MODEL), D_LRU),
        "sc_conv_w": nrm(ks[11], (SC_CONV, D_SC), SC_CONV),
        "sc_w_out": nrm(ks[12], (D_SC, D_MODEL), D_SC),
        "w_o": nrm(ks[13], (D_MODEL, D_MODEL), D_MODEL),
        "g_ffn": 1.0 + 0.02 * jax.random.normal(ks[14], (D_MODEL,), f32),
        "ffn_w_up": nrm(ks[15], (D_MODEL, 2 * D_FF), D_MODEL),
        "ffn_conv_w": nrm(ks[16], (FFN_CONV, 2 * D_FF), FFN_CONV),
        "ffn_w_down": nrm(ks[17], (D_FF, D_MODEL), D_FF),
        "g_final": 1.0 + 0.02 * jax.random.normal(ks[18], (D_MODEL,), f32),
    }


def _fwd_reference(x, g_mix, w_in, lru_conv_w, lru_conv_b, lru_wa, lru_ba, lru_wx, lru_bx,
              lru_lambda, lru_w_out, sc_conv_w, sc_w_out, w_o, g_ffn, ffn_w_up,
              ffn_conv_w, ffn_w_down, g_final):
    for _ in range(DEPTH):
        h = rmsnorm(x, g_mix)
        p = h @ w_in
        o = 0
        lru_x = p[..., o:o + D_LRU]; o += D_LRU
        lru_gate = p[..., o:o + D_LRU]; o += D_LRU
        sc_b = p[..., o:o + D_SC]; o += D_SC
        sc_c = p[..., o:o + D_SC]; o += D_SC
        sc_v = p[..., o:o + D_SC]; o += D_SC
        gate_lru = p[..., o:o + D_MODEL]; o += D_MODEL
        gate_sc = p[..., o:o + D_MODEL]

        xc = causal_dwconv(lru_x, lru_conv_w) + lru_conv_b
        y_lru = rg_lru(xc, lru_wa, lru_ba, lru_wx, lru_bx, lru_lambda)
        y_lru = (jax.nn.gelu(lru_gate) * y_lru) @ lru_w_out

        y_sc = (sc_b * causal_dwconv(sc_c * sc_v, sc_conv_w)) @ sc_w_out

        merged = jax.nn.sigmoid(gate_lru) * y_lru + jax.nn.sigmoid(gate_sc) * y_sc
        x = x + merged @ w_o

        h = rmsnorm(x, g_ffn)
        u = causal_dwconv(h @ ffn_w_up, ffn_conv_w)
        ff_gate, ff_val = u[..., :D_FF], u[..., D_FF:]
        x = x + (jax.nn.silu(ff_gate) * ff_val) @ ffn_w_down
    return rmsnorm(x, g_final)


import jax as _jax
import jax.numpy as _jnp

TWIN_FORMAT = 'train_step'
FWD_PARAMS = ['x', 'g_mix', 'w_in', 'lru_conv_w', 'lru_conv_b', 'lru_wa', 'lru_ba', 'lru_wx', 'lru_bx', 'lru_lambda', 'lru_w_out', 'sc_conv_w', 'sc_w_out', 'w_o', 'g_ffn', 'ffn_w_up', 'ffn_conv_w', 'ffn_w_down', 'g_final']
TWIN_WEIGHTS = ['g_mix', 'w_in', 'lru_conv_w', 'lru_conv_b', 'lru_wa', 'lru_ba', 'lru_wx', 'lru_bx', 'lru_lambda', 'lru_w_out', 'sc_conv_w', 'sc_w_out', 'w_o', 'g_ffn', 'ffn_w_up', 'ffn_conv_w', 'ffn_w_down', 'g_final']
TWIN_DIFF_INPUT = 'x'
TWIN_INPUTS = ['x', 'g_mix', 'w_in', 'lru_conv_w', 'lru_conv_b', 'lru_wa', 'lru_ba', 'lru_wx', 'lru_bx', 'lru_lambda', 'lru_w_out', 'sc_conv_w', 'sc_w_out', 'w_o', 'g_ffn', 'ffn_w_up', 'ffn_conv_w', 'ffn_w_down', 'g_final', 'loss_target', 'm_g_mix', 'm_w_in', 'm_lru_conv_w', 'm_lru_conv_b', 'm_lru_wa', 'm_lru_ba', 'm_lru_wx', 'm_lru_bx', 'm_lru_lambda', 'm_lru_w_out', 'm_sc_conv_w', 'm_sc_w_out', 'm_w_o', 'm_g_ffn', 'm_ffn_w_up', 'm_ffn_conv_w', 'm_ffn_w_down', 'm_g_final', 'v_g_mix', 'v_w_in', 'v_lru_conv_w', 'v_lru_conv_b', 'v_lru_wa', 'v_lru_ba', 'v_lru_wx', 'v_lru_bx', 'v_lru_lambda', 'v_lru_w_out', 'v_sc_conv_w', 'v_sc_w_out', 'v_w_o', 'v_g_ffn', 'v_ffn_w_up', 'v_ffn_conv_w', 'v_ffn_w_down', 'v_g_final']
TWIN_OUTPUTS = ['loss', 'grad_x', 'grad_g_mix', 'grad_w_in', 'grad_lru_conv_w', 'grad_lru_conv_b', 'grad_lru_wa', 'grad_lru_ba', 'grad_lru_wx', 'grad_lru_bx', 'grad_lru_lambda', 'grad_lru_w_out', 'grad_sc_conv_w', 'grad_sc_w_out', 'grad_w_o', 'grad_g_ffn', 'grad_ffn_w_up', 'grad_ffn_conv_w', 'grad_ffn_w_down', 'grad_g_final', 'delta_g_mix', 'delta_w_in', 'delta_lru_conv_w', 'delta_lru_conv_b', 'delta_lru_wa', 'delta_lru_ba', 'delta_lru_wx', 'delta_lru_bx', 'delta_lru_lambda', 'delta_lru_w_out', 'delta_sc_conv_w', 'delta_sc_w_out', 'delta_w_o', 'delta_g_ffn', 'delta_ffn_w_up', 'delta_ffn_conv_w', 'delta_ffn_w_down', 'delta_g_final', 'new_m_g_mix', 'new_m_w_in', 'new_m_lru_conv_w', 'new_m_lru_conv_b', 'new_m_lru_wa', 'new_m_lru_ba', 'new_m_lru_wx', 'new_m_lru_bx', 'new_m_lru_lambda', 'new_m_lru_w_out', 'new_m_sc_conv_w', 'new_m_sc_w_out', 'new_m_w_o', 'new_m_g_ffn', 'new_m_ffn_w_up', 'new_m_ffn_conv_w', 'new_m_ffn_w_down', 'new_m_g_final', 'new_v_g_mix', 'new_v_w_in', 'new_v_lru_conv_w', 'new_v_lru_conv_b', 'new_v_lru_wa', 'new_v_lru_ba', 'new_v_lru_wx', 'new_v_lru_bx', 'new_v_lru_lambda', 'new_v_lru_w_out', 'new_v_sc_conv_w', 'new_v_sc_w_out', 'new_v_w_o', 'new_v_g_ffn', 'new_v_ffn_w_up', 'new_v_ffn_conv_w', 'new_v_ffn_w_down', 'new_v_g_final']
TWIN_LEAF_KINDS = {'loss': 'loss', 'grad_x': 'grad_x', 'grad_g_mix': 'grad_w', 'grad_w_in': 'grad_w', 'grad_lru_conv_w': 'grad_w', 'grad_lru_conv_b': 'grad_w', 'grad_lru_wa': 'grad_w', 'grad_lru_ba': 'grad_w', 'grad_lru_wx': 'grad_w', 'grad_lru_bx': 'grad_w', 'grad_lru_lambda': 'grad_w', 'grad_lru_w_out': 'grad_w', 'grad_sc_conv_w': 'grad_w', 'grad_sc_w_out': 'grad_w', 'grad_w_o': 'grad_w', 'grad_g_ffn': 'grad_w', 'grad_ffn_w_up': 'grad_w', 'grad_ffn_conv_w': 'grad_w', 'grad_ffn_w_down': 'grad_w', 'grad_g_final': 'grad_w', 'delta_g_mix': 'delta_w', 'delta_w_in': 'delta_w', 'delta_lru_conv_w': 'delta_w', 'delta_lru_conv_b': 'delta_w', 'delta_lru_wa': 'delta_w', 'delta_lru_ba': 'delta_w', 'delta_lru_wx': 'delta_w', 'delta_lru_bx': 'delta_w', 'delta_lru_lambda': 'delta_w', 'delta_lru_w_out': 'delta_w', 'delta_sc_conv_w': 'delta_w', 'delta_sc_w_out': 'delta_w', 'delta_w_o': 'delta_w', 'delta_g_ffn': 'delta_w', 'delta_ffn_w_up': 'delta_w', 'delta_ffn_conv_w': 'delta_w', 'delta_ffn_w_down': 'delta_w', 'delta_g_final': 'delta_w', 'new_m_g_mix': 'new_m', 'new_m_w_in': 'new_m', 'new_m_lru_conv_w': 'new_m', 'new_m_lru_conv_b': 'new_m', 'new_m_lru_wa': 'new_m', 'new_m_lru_ba': 'new_m', 'new_m_lru_wx': 'new_m', 'new_m_lru_bx': 'new_m', 'new_m_lru_lambda': 'new_m', 'new_m_lru_w_out': 'new_m', 'new_m_sc_conv_w': 'new_m', 'new_m_sc_w_out': 'new_m', 'new_m_w_o': 'new_m', 'new_m_g_ffn': 'new_m', 'new_m_ffn_w_up': 'new_m', 'new_m_ffn_conv_w': 'new_m', 'new_m_ffn_w_down': 'new_m', 'new_m_g_final': 'new_m', 'new_v_g_mix': 'new_v', 'new_v_w_in': 'new_v', 'new_v_lru_conv_w': 'new_v', 'new_v_lru_conv_b': 'new_v', 'new_v_lru_wa': 'new_v', 'new_v_lru_ba': 'new_v', 'new_v_lru_wx': 'new_v', 'new_v_lru_bx': 'new_v', 'new_v_lru_lambda': 'new_v', 'new_v_lru_w_out': 'new_v', 'new_v_sc_conv_w': 'new_v', 'new_v_sc_w_out': 'new_v', 'new_v_w_o': 'new_v', 'new_v_g_ffn': 'new_v', 'new_v_ffn_w_up': 'new_v', 'new_v_ffn_conv_w': 'new_v', 'new_v_ffn_w_down': 'new_v', 'new_v_g_final': 'new_v'}


def _forward(args):
    return _fwd_reference(*[args[k] for k in FWD_PARAMS])


def _output_shape():
    def fwd():
        inp = _fwd_setup_inputs(0)
        return _fwd_reference(*[inp[k] for k in FWD_PARAMS])
    out = _jax.eval_shape(fwd)
    return out.shape, out.dtype

N_MICROBATCH = 1
ADAM_LR = 0.001
ADAM_B1 = 0.9
ADAM_B2 = 0.999
ADAM_EPS = 1e-08
ADAM_WD = 0.01
ADAM_STEP = 10
PER_EXAMPLE_BATCH_AXIS = {'x': 0, 'loss_target': 0}
SHARED_INPUTS = []
_WEIGHT_DTYPES = {'g_mix': _jnp.float32, 'w_in': _jnp.float32, 'lru_conv_w': _jnp.float32, 'lru_conv_b': _jnp.float32, 'lru_wa': _jnp.float32, 'lru_ba': _jnp.float32, 'lru_wx': _jnp.float32, 'lru_bx': _jnp.float32, 'lru_lambda': _jnp.float32, 'lru_w_out': _jnp.float32, 'sc_conv_w': _jnp.float32, 'sc_w_out': _jnp.float32, 'w_o': _jnp.float32, 'g_ffn': _jnp.float32, 'ffn_w_up': _jnp.float32, 'ffn_conv_w': _jnp.float32, 'ffn_w_down': _jnp.float32, 'g_final': _jnp.float32}
MOMENT_SCALE = {'g_mix': 8.807240e-02, 'w_in': 4.013606e-02, 'lru_conv_w': 2.663255e-02, 'lru_conv_b': 1.228471e-01, 'lru_wa': 6.239434e-03, 'lru_ba': 5.547240e-03, 'lru_wx': 1.077991e-02, 'lru_bx': 8.532089e-03, 'lru_lambda': 1.121268e-02, 'lru_w_out': 1.698484e-02, 'sc_conv_w': 6.656722e-02, 'sc_w_out': 4.570205e-02, 'w_o': 4.862959e-02, 'g_ffn': 6.034882e-02, 'ffn_w_up': 2.561453e-02, 'ffn_conv_w': 2.547617e-02, 'ffn_w_down': 4.177259e-02, 'g_final': 1.600678e+01}


def _to_microbatches(a, axis):
    t = _jnp.moveaxis(a, axis, 0)
    t = t.reshape((N_MICROBATCH, t.shape[0] // N_MICROBATCH) + t.shape[1:])
    return _jnp.moveaxis(t, 1, axis + 1)


def setup_inputs(seed: int = 0) -> dict:
    inp = _fwd_setup_inputs(seed)
    key = _jax.random.fold_in(_jax.random.key(seed), 7919)
    shape, _ = _output_shape()
    out = dict(inp)
    out["loss_target"] = _jax.random.normal(_jax.random.fold_in(key, 0), shape, _jnp.float32)
    for i, name in enumerate(TWIN_WEIGHTS):
        w = inp[name].astype(_jnp.float32)
        if MOMENT_SCALE is None:
            s = _jnp.sqrt(_jnp.mean(_jnp.square(w)) + 1e-30)
        else:
            s = MOMENT_SCALE[name]
        km, kv = _jax.random.split(_jax.random.fold_in(key, i + 1))
        out[name] = w
        out["m_" + name] = s * _jax.random.normal(km, w.shape, _jnp.float32)
        out["v_" + name] = (s * s) * _jax.random.uniform(kv, w.shape, _jnp.float32, 0.5, 1.5)
    if N_MICROBATCH > 1:
        for name, axis in PER_EXAMPLE_BATCH_AXIS.items():
            out[name] = _to_microbatches(out[name], axis)
    return {'x': out['x'], 'g_mix': out['g_mix'], 'w_in': out['w_in'], 'lru_conv_w': out['lru_conv_w'], 'lru_conv_b': out['lru_conv_b'], 'lru_wa': out['lru_wa'], 'lru_ba': out['lru_ba'], 'lru_wx': out['lru_wx'], 'lru_bx': out['lru_bx'], 'lru_lambda': out['lru_lambda'], 'lru_w_out': out['lru_w_out'], 'sc_conv_w': out['sc_conv_w'], 'sc_w_out': out['sc_w_out'], 'w_o': out['w_o'], 'g_ffn': out['g_ffn'], 'ffn_w_up': out['ffn_w_up'], 'ffn_conv_w': out['ffn_conv_w'], 'ffn_w_down': out['ffn_w_down'], 'g_final': out['g_final'], 'loss_target': out['loss_target'], 'm_g_mix': out['m_g_mix'], 'm_w_in': out['m_w_in'], 'm_lru_conv_w': out['m_lru_conv_w'], 'm_lru_conv_b': out['m_lru_conv_b'], 'm_lru_wa': out['m_lru_wa'], 'm_lru_ba': out['m_lru_ba'], 'm_lru_wx': out['m_lru_wx'], 'm_lru_bx': out['m_lru_bx'], 'm_lru_lambda': out['m_lru_lambda'], 'm_lru_w_out': out['m_lru_w_out'], 'm_sc_conv_w': out['m_sc_conv_w'], 'm_sc_w_out': out['m_sc_w_out'], 'm_w_o': out['m_w_o'], 'm_g_ffn': out['m_g_ffn'], 'm_ffn_w_up': out['m_ffn_w_up'], 'm_ffn_conv_w': out['m_ffn_conv_w'], 'm_ffn_w_down': out['m_ffn_w_down'], 'm_g_final': out['m_g_final'], 'v_g_mix': out['v_g_mix'], 'v_w_in': out['v_w_in'], 'v_lru_conv_w': out['v_lru_conv_w'], 'v_lru_conv_b': out['v_lru_conv_b'], 'v_lru_wa': out['v_lru_wa'], 'v_lru_ba': out['v_lru_ba'], 'v_lru_wx': out['v_lru_wx'], 'v_lru_bx': out['v_lru_bx'], 'v_lru_lambda': out['v_lru_lambda'], 'v_lru_w_out': out['v_lru_w_out'], 'v_sc_conv_w': out['v_sc_conv_w'], 'v_sc_w_out': out['v_sc_w_out'], 'v_w_o': out['v_w_o'], 'v_g_ffn': out['v_g_ffn'], 'v_ffn_w_up': out['v_ffn_w_up'], 'v_ffn_conv_w': out['v_ffn_conv_w'], 'v_ffn_w_down': out['v_ffn_w_down'], 'v_g_final': out['v_g_final']}


def _loss(weights, diff, rest, loss_target):
    with _jax.named_scope("forward"):
        args = {**rest, TWIN_DIFF_INPUT: diff, **{k: w.astype(_WEIGHT_DTYPES[k]) for k, w in weights.items()}}
        y = _forward(args)
    with _jax.named_scope("loss_head"):
        err = _jnp.square(y.astype(_jnp.float32) - loss_target)
        return 0.5 * _jnp.sum(_jnp.mean(err, axis=-1)) if err.ndim else 0.5 * err


def _adamw(w, g, m, v):
    m = ADAM_B1 * m + (1.0 - ADAM_B1) * g
    v = ADAM_B2 * v + (1.0 - ADAM_B2) * _jnp.square(g)
    m_hat = m / (1.0 - ADAM_B1 ** ADAM_STEP)
    v_hat = v / (1.0 - ADAM_B2 ** ADAM_STEP)
    delta = -ADAM_LR * (m_hat / (_jnp.sqrt(v_hat) + ADAM_EPS) + ADAM_WD * w)
    return delta, m, v


def reference(x, g_mix, w_in, lru_conv_w, lru_conv_b, lru_wa, lru_ba, lru_wx, lru_bx, lru_lambda, lru_w_out, sc_conv_w, sc_w_out, w_o, g_ffn, ffn_w_up, ffn_conv_w, ffn_w_down, g_final, loss_target, m_g_mix, m_w_in, m_lru_conv_w, m_lru_conv_b, m_lru_wa, m_lru_ba, m_lru_wx, m_lru_bx, m_lru_lambda, m_lru_w_out, m_sc_conv_w, m_sc_w_out, m_w_o, m_g_ffn, m_ffn_w_up, m_ffn_conv_w, m_ffn_w_down, m_g_final, v_g_mix, v_w_in, v_lru_conv_w, v_lru_conv_b, v_lru_wa, v_lru_ba, v_lru_wx, v_lru_bx, v_lru_lambda, v_lru_w_out, v_sc_conv_w, v_sc_w_out, v_w_o, v_g_ffn, v_ffn_w_up, v_ffn_conv_w, v_ffn_w_down, v_g_final):
    given = dict(x=x, g_mix=g_mix, w_in=w_in, lru_conv_w=lru_conv_w, lru_conv_b=lru_conv_b, lru_wa=lru_wa, lru_ba=lru_ba, lru_wx=lru_wx, lru_bx=lru_bx, lru_lambda=lru_lambda, lru_w_out=lru_w_out, sc_conv_w=sc_conv_w, sc_w_out=sc_w_out, w_o=w_o, g_ffn=g_ffn, ffn_w_up=ffn_w_up, ffn_conv_w=ffn_conv_w, ffn_w_down=ffn_w_down, g_final=g_final, loss_target=loss_target, m_g_mix=m_g_mix, m_w_in=m_w_in, m_lru_conv_w=m_lru_conv_w, m_lru_conv_b=m_lru_conv_b, m_lru_wa=m_lru_wa, m_lru_ba=m_lru_ba, m_lru_wx=m_lru_wx, m_lru_bx=m_lru_bx, m_lru_lambda=m_lru_lambda, m_lru_w_out=m_lru_w_out, m_sc_conv_w=m_sc_conv_w, m_sc_w_out=m_sc_w_out, m_w_o=m_w_o, m_g_ffn=m_g_ffn, m_ffn_w_up=m_ffn_w_up, m_ffn_conv_w=m_ffn_conv_w, m_ffn_w_down=m_ffn_w_down, m_g_final=m_g_final, v_g_mix=v_g_mix, v_w_in=v_w_in, v_lru_conv_w=v_lru_conv_w, v_lru_conv_b=v_lru_conv_b, v_lru_wa=v_lru_wa, v_lru_ba=v_lru_ba, v_lru_wx=v_lru_wx, v_lru_bx=v_lru_bx, v_lru_lambda=v_lru_lambda, v_lru_w_out=v_lru_w_out, v_sc_conv_w=v_sc_conv_w, v_sc_w_out=v_sc_w_out, v_w_o=v_w_o, v_g_ffn=v_g_ffn, v_ffn_w_up=v_ffn_w_up, v_ffn_conv_w=v_ffn_conv_w, v_ffn_w_down=v_ffn_w_down, v_g_final=v_g_final)
    weights = {n: given[n] for n in TWIN_WEIGHTS}
    shared = {n: given[n] for n in SHARED_INPUTS}
    per_example = {n: given[n] for n in ['x']}
    grad_fn = _jax.value_and_grad(_loss, argnums=(0, 1))

    def one_microbatch(ex, loss_target):
        ex = dict(ex)
        diff = ex.pop(TWIN_DIFF_INPUT)
        return grad_fn(weights, diff, {**shared, **ex}, loss_target)

    if N_MICROBATCH == 1:
        loss, (grad_w, grad_x) = one_microbatch(per_example, given["loss_target"])
    else:
        def body(carry, xs):
            loss_sum, grad_sum = carry
            l_k, (gw_k, gx_k) = one_microbatch(xs[0], xs[1])
            with _jax.named_scope("update"):
                return (loss_sum + l_k, _jax.tree.map(_jnp.add, grad_sum, gw_k)), gx_k

        init = (_jnp.zeros((), _jnp.float32), _jax.tree.map(_jnp.zeros_like, weights))
        (loss, grad_w), grad_x = _jax.lax.scan(body, init, (per_example, given["loss_target"]))
    with _jax.named_scope("update"):
        delta_w, new_m, new_v = {}, {}, {}
        for n in TWIN_WEIGHTS:
            delta_w[n], new_m[n], new_v[n] = _adamw(weights[n], grad_w[n], given["m_" + n], given["v_" + n])
    return (loss, grad_x, *[grad_w[n] for n in TWIN_WEIGHTS], *[delta_w[n] for n in TWIN_WEIGHTS],
            *[new_m[n] for n in TWIN_WEIGHTS], *[new_v[n] for n in TWIN_WEIGHTS])
```

```python
import functools
import math

import jax
import jax.numpy as jnp
from jax import lax
from jax.experimental import pallas as pl
from jax.experimental.pallas import tpu as pltpu

F32, BF16 = jnp.float32, jnp.bfloat16
MESH = pl.DeviceIdType.MESH
N_DEV = 8
N_CHIP = 4
AXES = ("x", "y", "c")

EPS = 1e-6
LRU_C = 8.0
HEAD_DIM = 64
ADAM_LR, ADAM_B1, ADAM_B2, ADAM_EPS, ADAM_WD, ADAM_STEP = 0.001, 0.9, 0.999, 1e-08, 0.01, 10

VMEM_LIMIT = 48 * 1024 * 1024
LANES = 128
SUB = 8
HALO = 16
TB = 512
C_LRU = 256
C_EW = 512
TM, TN, TK = 512, 1536, 2048


def _tile(n, pref, align=LANES):
    best = None
    for d in range(align, min(n, pref) + 1, align):
        if n % d == 0:
            best = d
    return best or n


def _cparams(sem=None, vmem=VMEM_LIMIT):
    kw = dict(vmem_limit_bytes=vmem)
    if sem is not None:
        kw["dimension_semantics"] = sem
    return pltpu.CompilerParams(**kw)


def _S(shape, dtype):
    return jax.ShapeDtypeStruct(shape, dtype)


ANY = pl.BlockSpec(memory_space=pl.ANY)
VMEM_SPEC = pl.BlockSpec(memory_space=pltpu.VMEM)


def _mm_nn(a, w3, *, out_dtype, name, residual=None, tm=TM, tn=TN, tk=TK):
    M, K = a.shape
    G, _, n = w3.shape
    tm, tn, tk = _tile(M, tm, SUB), _tile(n, tn), _tile(K, tk)
    nj, nk = n // tn, K // tk

    def body(*refs):
        if residual is None:
            a_ref, w_ref, o_ref, acc = refs
            r_ref = None
        else:
            a_ref, w_ref, r_ref, o_ref, acc = refs
        k = pl.program_id(3)

        @pl.when(k == 0)
        def _():
            acc[...] = jnp.zeros_like(acc)

        acc[...] += jnp.dot(a_ref[...], w_ref[...], preferred_element_type=F32)

        @pl.when(k == nk - 1)
        def _():
            r = acc[...]
            if r_ref is not None:
                r = r + r_ref[...]
            o_ref[...] = r.astype(o_ref.dtype)

    in_specs = [pl.BlockSpec((tm, tk), lambda g, j, i, k: (i, k)),
                pl.BlockSpec((None, tk, tn), lambda g, j, i, k: (g, k, j))]
    args = [a, w3]
    if residual is not None:
        in_specs.append(pl.BlockSpec((tm, tn), lambda g, j, i, k: (i, g * nj + j)))
        args.append(residual)
    return pl.pallas_call(
        body, name=name, out_shape=_S((M, G * n), out_dtype),
        grid=(G, nj, M // tm, nk), in_specs=in_specs,
        out_specs=pl.BlockSpec((tm, tn), lambda g, j, i, k: (i, g * nj + j)),
        scratch_shapes=[pltpu.VMEM((tm, tn), F32)],
        compiler_params=_cparams(("parallel", "parallel", "parallel", "arbitrary")),
    )(*args)


def _mm_nt(dy, w3, *, out_dtype, name, tm=1024, tko=1024, tn=TN):
    M, _ = dy.shape
    G, K, n = w3.shape
    tm, tko, tn = _tile(M, tm, SUB), _tile(K, tko), _tile(n, tn)
    nj = n // tn
    nr = G * nj

    def body(dy_ref, w_ref, o_ref, acc):
        r = pl.program_id(2)

        @pl.when(r == 0)
        def _():
            acc[...] = jnp.zeros_like(acc)

        acc[...] += lax.dot_general(dy_ref[...], w_ref[...], (((1,), (1,)), ((), ())),
                                    preferred_element_type=F32)

        @pl.when(r == nr - 1)
        def _():
            o_ref[...] = acc[...].astype(o_ref.dtype)

    return pl.pallas_call(
        body, name=name, out_shape=_S((M, K), out_dtype),
        grid=(K // tko, M // tm, nr),
        in_specs=[pl.BlockSpec((tm, tn), lambda ko, i, r: (i, r)),
                  pl.BlockSpec((None, tko, tn), lambda ko, i, r: (r // nj, ko, r % nj))],
        out_specs=pl.BlockSpec((tm, tko), lambda ko, i, r: (i, ko)),
        scratch_shapes=[pltpu.VMEM((tm, tko), F32)],
        compiler_params=_cparams(("parallel", "parallel", "arbitrary")),
    )(dy, w3)


def _mm_tn(a, dy, G, *, out_dtype, name, tk=1024, tn=TN, tt=512):
    M, K = a.shape
    n = dy.shape[1] // G
    tk, tn, tt = _tile(K, tk), _tile(n, tn), _tile(M, tt, SUB)
    nj, nt = n // tn, M // tt

    def body(a_ref, dy_ref, o_ref, acc):
        t = pl.program_id(3)

        @pl.when(t == 0)
        def _():
            acc[...] = jnp.zeros_like(acc)

        acc[...] += lax.dot_general(a_ref[...], dy_ref[...], (((0,), (0,)), ((), ())),
                                    preferred_element_type=F32)

        @pl.when(t == nt - 1)
        def _():
            o_ref[...] = acc[...].astype(o_ref.dtype)

    return pl.pallas_call(
        body, name=name, out_shape=_S((G, K, n), out_dtype),
        grid=(G, nj, K // tk, nt),
        in_specs=[pl.BlockSpec((tt, tk), lambda g, j, k, t: (t, k)),
                  pl.BlockSpec((tt, tn), lambda g, j, k, t: (t, g * nj + j))],
        out_specs=pl.BlockSpec((None, tk, tn), lambda g, j, k, t: (g, k, j)),
        scratch_shapes=[pltpu.VMEM((tk, tn), F32)],
        compiler_params=_cparams(("parallel", "parallel", "parallel", "arbitrary")),
    )(a, dy)


def _cast_bf16(w, name):
    R, C = w.shape
    tr = _tile(R, 512, SUB)

    def body(w_ref, o_ref):
        o_ref[...] = w_ref[...].astype(BF16)

    return pl.pallas_call(
        body, name=name, out_shape=_S((R, C), BF16), grid=(R // tr,),
        in_specs=[pl.BlockSpec((tr, C), lambda i: (i, 0))],
        out_specs=pl.BlockSpec((tr, C), lambda i: (i, 0)),
        compiler_params=_cparams(("parallel",)),
    )(w)


def _down(cur, prev8, j):
    return pltpu.roll(jnp.concatenate([prev8, cur], axis=0), j, 0)[SUB:, :]


def _up(cur, next8, j):
    n = cur.shape[0] + SUB
    return pltpu.roll(jnp.concatenate([cur, next8], axis=0), n - j, 0)[:cur.shape[0], :]


def _conv(x, prev8, w_ref):
    kw = w_ref.shape[0]
    y = x * w_ref[pl.ds(kw - 1, 1), :]
    for k in range(kw - 1):
        y = y + _down(x, prev8, kw - 1 - k) * w_ref[pl.ds(k, 1), :]
    return y


def _conv_t(dy, next8, w_ref):
    kw = w_ref.shape[0]
    dx = dy * w_ref[pl.ds(kw - 1, 1), :]
    for k in range(kw - 1):
        dx = dx + _up(dy, next8, kw - 1 - k) * w_ref[pl.ds(k, 1), :]
    return dx


def _conv_dw(dw_ref, dy, x, prev8, first):
    kw = dw_ref.shape[0]

    @pl.when(first)
    def _():
        dw_ref[...] = jnp.zeros_like(dw_ref)

    for k in range(kw):
        xs = x if k == kw - 1 else _down(x, prev8, kw - 1 - k)
        dw_ref[pl.ds(k, 1), :] += jnp.sum(dy * xs, axis=0, keepdims=True)


def _acc(ref, val, first):
    @pl.when(first)
    def _():
        ref[...] = jnp.zeros_like(ref)

    ref[...] += val


def _acc_row(ref, val, first):
    _acc(ref, jnp.sum(val, axis=0, keepdims=True), first)


def _prev8(h_ref, t):
    return jnp.where(t > 0, h_ref[...].astype(F32)[HALO - SUB:, :], 0.0)


def _next8(h_ref, is_last):
    return jnp.where(is_last, 0.0, h_ref[...].astype(F32)[:SUB, :])


_GELU_K0 = math.sqrt(2.0 / math.pi)
_GELU_K1 = 0.044715


def _gelu_and_grad(x):
    x2 = x * x
    th = jnp.tanh(_GELU_K0 * x * (1.0 + _GELU_K1 * x2))
    g = 0.5 * x * (1.0 + th)
    dg = 0.5 * (1.0 + th) + 0.5 * x * (1.0 - th * th) * (_GELU_K0 * (1.0 + 3.0 * _GELU_K1 * x2))
    return g, dg


def _neg_expm1(z):
    series = -z * (1.0 + z * (0.5 + z * (1.0 / 6.0 + z * (1.0 / 24.0))))
    return jnp.where(z > -0.03, series, 1.0 - jnp.exp(z))


def _store_block(stage_ref, dst_hbm, sem, row0, col0):
    tb, c = stage_ref.shape
    return pltpu.make_async_copy(stage_ref, dst_hbm.at[pl.ds(row0, tb), pl.ds(col0, c)], sem)


def _halo_prev_map(hb, col_fn):
    return lambda c, t: (jnp.maximum(t * hb - 1, 0), col_fn(c))


def _rms_fwd(x, g, name):
    T, D = x.shape
    tb = _tile(T, TB, SUB)

    def body(x_ref, g_ref, o_ref):
        xv = x_ref[...]
        rstd = lax.rsqrt(jnp.mean(xv * xv, axis=-1, keepdims=True) + EPS)
        o_ref[...] = (xv * rstd * g_ref[...]).astype(BF16)

    return pl.pallas_call(
        body, name=name, out_shape=_S((T, D), BF16), grid=(T // tb,),
        in_specs=[pl.BlockSpec((tb, D), lambda i: (i, 0)), pl.BlockSpec((1, D), lambda i: (0, 0))],
        out_specs=pl.BlockSpec((tb, D), lambda i: (i, 0)),
        compiler_params=_cparams(("parallel",)),
    )(x, g.reshape(1, D))


def _rms_bwd(x, g, dh, dres, name):
    T, D = x.shape
    tb = _tile(T, 256, SUB)

    def body(x_ref, g_ref, dh_ref, dr_ref, dx_ref, dxb_ref, dg_ref):
        i = pl.program_id(0)
        xv = x_ref[...]
        rstd = lax.rsqrt(jnp.mean(xv * xv, axis=-1, keepdims=True) + EPS)
        xn = xv * rstd
        dhv = dh_ref[...].astype(F32)
        _acc_row(dg_ref, dhv * xn, i == 0)
        dxn = dhv * g_ref[...]
        dx = dr_ref[...] + rstd * (dxn - xn * jnp.mean(dxn * xn, axis=-1, keepdims=True))
        dx_ref[...] = dx
        dxb_ref[...] = dx.astype(BF16)

    blk = pl.BlockSpec((tb, D), lambda i: (i, 0))
    vec = pl.BlockSpec((1, D), lambda i: (0, 0))
    return pl.pallas_call(
        body, name=name, out_shape=(_S((T, D), F32), _S((T, D), BF16), _S((1, D), F32)),
        grid=(T // tb,), in_specs=[blk, vec, blk, blk], out_specs=(blk, blk, vec),
        compiler_params=_cparams(("arbitrary",)),
    )(x, g.reshape(1, D), dh, dres)


def _loss_head(x2, g, target, name):
    T, D = x2.shape
    tb = _tile(T, 256, SUB)

    def body(x_ref, g_ref, t_ref, dx_ref, dxb_ref, loss_ref, dg_ref):
        i = pl.program_id(0)
        xv = x_ref[...]
        rstd = lax.rsqrt(jnp.mean(xv * xv, axis=-1, keepdims=True) + EPS)
        xn = xv * rstd
        err = xn * g_ref[...] - t_ref[...]
        part = 0.5 * jnp.sum(jnp.mean(err * err, axis=-1, keepdims=True), axis=0, keepdims=True)
        part = jnp.broadcast_to(part, (1, LANES))
        _acc(loss_ref, part, i == 0)
        dy = err * (1.0 / D)
        _acc_row(dg_ref, dy * xn, i == 0)
        dxn = dy * g_ref[...]
        dx = rstd * (dxn - xn * jnp.mean(dxn * xn, axis=-1, keepdims=True))
        dx_ref[...] = dx
        dxb_ref[...] = dx.astype(BF16)

    blk = pl.BlockSpec((tb, D), lambda i: (i, 0))
    vec = pl.BlockSpec((1, D), lambda i: (0, 0))
    return pl.pallas_call(
        body, name=name,
        out_shape=(_S((T, D), F32), _S((T, D), BF16), _S((1, LANES), F32), _S((1, D), F32)),
        grid=(T // tb,), in_specs=[blk, vec, blk],
        out_specs=(blk, blk, pl.BlockSpec((1, LANES), lambda i: (0, 0)), vec),
        compiler_params=_cparams(("arbitrary",)),
    )(x2, g.reshape(1, D), target)


def _lru_gates(xc, wa_ref, ba_ref, wx_ref, bx_ref, lam_ref):
    xcb = xc.astype(BF16)
    r = jax.nn.sigmoid(jnp.dot(xcb, wa_ref[...], preferred_element_type=F32) + ba_ref[...])
    i = jax.nn.sigmoid(jnp.dot(xcb, wx_ref[...], preferred_element_type=F32) + bx_ref[...])
    sp = jax.nn.softplus(-lam_ref[...])
    log_a = (-LRU_C * sp) * r
    a = jnp.exp(log_a)
    s = jnp.sqrt(_neg_expm1(2.0 * log_a))
    return xcb, r, i, a, s


def _lru_fwd(p, conv_w, conv_b, wa_bd, ba, wx_bd, bx, lam, *, name):
    T = p.shape[0]
    d = lam.shape[-1]
    C = _tile(d, C_LRU)
    nC = d // C
    tb = _tile(T, TB, HALO)
    nT, hb, nt = T // tb, tb // HALO, tb // SUB

    def body(x_ref, xh_ref, g_ref, cw_ref, cb_ref, wa_ref, ba_ref, wx_ref, bx_ref, lam_ref,
             hs_ref, y_ref, a_s, u_s, h_s):
        t = pl.program_id(1)

        @pl.when(t == 0)
        def _():
            h_s[...] = jnp.zeros_like(h_s)

        x = x_ref[...].astype(F32)
        xc = _conv(x, _prev8(xh_ref, t), cw_ref) + cb_ref[...]
        _, r, i, a, s = _lru_gates(xc, wa_ref, ba_ref, wx_ref, bx_ref, lam_ref)
        a_s[...] = a
        u_s[...] = s * (i * xc)
        row = lax.broadcasted_iota(jnp.int32, (SUB, C), 0)

        def step(k, h):
            o = pl.multiple_of(k * SUB, SUB)
            A = a_s[pl.ds(o, SUB), :]
            B = u_s[pl.ds(o, SUB), :]
            for sh in (1, 2, 4):
                m = row >= sh
                Ap = pltpu.roll(A, sh, 0)
                Bp = pltpu.roll(B, sh, 0)
                B = jnp.where(m, A * Bp + B, B)
                A = jnp.where(m, A * Ap, A)
            hs = A * h + B
            hs_ref[pl.ds(o, SUB), :] = hs
            return jnp.broadcast_to(hs[SUB - 1:SUB, :], (SUB, C))

        h_s[...] = lax.fori_loop(0, nt, step, h_s[...])
        gel, _ = _gelu_and_grad(g_ref[...].astype(F32))
        y_ref[...] = (gel * hs_ref[...]).astype(BF16)

    vec = pl.BlockSpec((1, C), lambda c, t: (0, c))
    sq = pl.BlockSpec((None, C, C), lambda c, t: (c, 0, 0))
    return pl.pallas_call(
        body, name=name, out_shape=(_S((T, d), F32), _S((T, d), BF16)),
        grid=(nC, nT),
        in_specs=[pl.BlockSpec((tb, C), lambda c, t: (t, c)),
                  pl.BlockSpec((HALO, C), _halo_prev_map(hb, lambda c: c)),
                  pl.BlockSpec((tb, C), lambda c, t: (t, nC + c)),
                  pl.BlockSpec((conv_w.shape[0], C), lambda c, t: (0, c)),
                  vec, sq, vec, sq, vec, vec],
        out_specs=(pl.BlockSpec((tb, C), lambda c, t: (t, c)), pl.BlockSpec((tb, C), lambda c, t: (t, c))),
        scratch_shapes=[pltpu.VMEM((tb, C), F32), pltpu.VMEM((tb, C), F32), pltpu.VMEM((SUB, C), F32)],
        compiler_params=_cparams(("parallel", "arbitrary")),
    )(p, p, p, conv_w, conv_b, wa_bd, ba, wx_bd, bx, lam)


def _lru_bwd(p, hs, dyl, dp, conv_w, conv_b, wa_bd, ba, wx_bd, bx, lam, *, name):
    T = p.shape[0]
    d = lam.shape[-1]
    C = _tile(d, C_LRU)
    nC = d // C
    tb = _tile(T, TB, HALO)
    nT, hb, nt = T // tb, tb // HALO, tb // SUB
    kw = conv_w.shape[0]

    def body(x_ref, xh_ref, g_ref, hs_ref, hh_ref, dy_ref, cw_ref, cb_ref, wa_ref, ba_ref, wx_ref, bx_ref,
             lam_ref, dp_in, dp_ref, dcw_ref, dcb_ref, dwa_ref, dba_ref, dwx_ref, dbx_ref, dlam_ref,
             b_s, g_s, dh_s, an_s, dhn_s, dxn_s, st_x, st_g, sems):
        del dp_in
        c = pl.program_id(0)
        tr = pl.program_id(1)
        t = nT - 1 - tr
        first = tr == 0

        @pl.when(first)
        def _():
            an_s[...] = jnp.zeros_like(an_s)
            dhn_s[...] = jnp.zeros_like(dhn_s)
            dxn_s[...] = jnp.zeros_like(dxn_s)

        x = x_ref[...].astype(F32)
        xprev = _prev8(xh_ref, t)
        xc = _conv(x, xprev, cw_ref) + cb_ref[...]
        xcb, r, i, a, s = _lru_gates(xc, wa_ref, ba_ref, wx_ref, bx_ref, lam_ref)
        hsv = hs_ref[...]
        dy = dy_ref[...].astype(F32)
        gel, dgel = _gelu_and_grad(g_ref[...].astype(F32))
        st_g[...] = (dy * hsv * dgel).astype(BF16)

        b_s[...] = _up(a, an_s[...], 1)
        g_s[...] = dy * gel
        row = lax.broadcasted_iota(jnp.int32, (SUB, C), 0)

        def step(k, carry):
            o = pl.multiple_of((nt - 1 - k) * SUB, SUB)
            B = b_s[pl.ds(o, SUB), :]
            G = g_s[pl.ds(o, SUB), :]
            for sh in (1, 2, 4):
                m = row < SUB - sh
                Bn = pltpu.roll(B, SUB - sh, 0)
                Gn = pltpu.roll(G, SUB - sh, 0)
                G = jnp.where(m, B * Gn + G, G)
                B = jnp.where(m, B * Bn, B)
            dh = B * carry + G
            dh_s[pl.ds(o, SUB), :] = dh
            return jnp.broadcast_to(dh[0:1, :], (SUB, C))

        dhn_s[...] = lax.fori_loop(0, nt, step, dhn_s[...])
        an_s[...] = a[:SUB, :]
        dh = dh_s[...]

        hprev = _down(hsv, jnp.where(t > 0, hh_ref[...][HALO - SUB:, :], 0.0), 1)
        d_a = dh * hprev
        ixc = i * xc
        d_s = dh * ixc
        d_i = dh * s * xc
        d_xc = dh * s * i
        d_l = d_a * a - d_s * (a * a) / s
        sp = jax.nn.softplus(-lam_ref[...])
        _acc_row(dlam_ref, d_l * r * (LRU_C * jax.nn.sigmoid(-lam_ref[...])), first)
        d_zr = (d_l * (-LRU_C * sp)) * r * (1.0 - r)
        d_zi = d_i * i * (1.0 - i)
        _acc_row(dba_ref, d_zr, first)
        _acc_row(dbx_ref, d_zi, first)
        d_zrb = d_zr.astype(BF16)
        d_zib = d_zi.astype(BF16)
        tn_dims = (((0,), (0,)), ((), ()))
        nt_dims = (((1,), (1,)), ((), ()))
        gwa = lax.dot_general(xcb, d_zrb, tn_dims, preferred_element_type=F32)
        gwx = lax.dot_general(xcb, d_zib, tn_dims, preferred_element_type=F32)
        _acc(dwa_ref, gwa, first)
        _acc(dwx_ref, gwx, first)
        d_xc = (d_xc + lax.dot_general(d_zrb, wa_ref[...], nt_dims, preferred_element_type=F32)
                + lax.dot_general(d_zib, wx_ref[...], nt_dims, preferred_element_type=F32))
        _acc_row(dcb_ref, d_xc, first)
        _conv_dw(dcw_ref, d_xc, x, xprev, first)
        st_x[...] = _conv_t(d_xc, dxn_s[...], cw_ref).astype(BF16)
        dxn_s[...] = d_xc[:SUB, :]

        cx = _store_block(st_x, dp_ref, sems.at[0], t * tb, c * C)
        cg = _store_block(st_g, dp_ref, sems.at[1], t * tb, d + c * C)
        cx.start()
        cg.start()
        cx.wait()
        cg.wait()

    rev = lambda c, tr: (nT - 1 - tr, c)
    vec = pl.BlockSpec((1, C), lambda c, tr: (0, c))
    sq = pl.BlockSpec((None, C, C), lambda c, tr: (c, 0, 0))
    cwb = pl.BlockSpec((kw, C), lambda c, tr: (0, c))
    halo_prev = lambda c, tr: (jnp.maximum((nT - 1 - tr) * hb - 1, 0), c)
    return pl.pallas_call(
        body, name=name,
        out_shape=(_S(dp.shape, dp.dtype), _S((kw, d), F32), _S((1, d), F32), _S((nC, C, C), F32), _S((1, d), F32),
                   _S((nC, C, C), F32), _S((1, d), F32), _S((1, d), F32)),
        grid=(nC, nT),
        in_specs=[pl.BlockSpec((tb, C), rev),
                  pl.BlockSpec((HALO, C), halo_prev),
                  pl.BlockSpec((tb, C), lambda c, tr: (nT - 1 - tr, nC + c)),
                  pl.BlockSpec((tb, C), rev),
                  pl.BlockSpec((HALO, C), halo_prev),
                  pl.BlockSpec((tb, C), rev),
                  cwb, vec, sq, vec, sq, vec, vec, ANY],
        out_specs=(ANY, cwb, vec, sq, vec, sq, vec, vec),
        scratch_shapes=[pltpu.VMEM((tb, C), F32), pltpu.VMEM((tb, C), F32), pltpu.VMEM((tb, C), F32),
                        pltpu.VMEM((SUB, C), F32), pltpu.VMEM((SUB, C), F32), pltpu.VMEM((SUB, C), F32),
                        pltpu.VMEM((tb, C), BF16), pltpu.VMEM((tb, C), BF16), pltpu.SemaphoreType.DMA((2,))],
        input_output_aliases={13: 0},
        compiler_params=_cparams(("arbitrary", "arbitrary")),
    )(p, p, p, hs, hs, dyl, conv_w, conv_b, wa_bd, ba, wx_bd, bx, lam, dp)


def _sc_fwd(p, conv_w, *, d, name):
    T = p.shape[0]
    C = _tile(d, C_EW)
    nC = d // C
    tb = _tile(T, TB, HALO)
    nT, hb = T // tb, tb // HALO

    def body(b_ref, c_ref, ch_ref, v_ref, vh_ref, w_ref, y_ref):
        t = pl.program_id(1)
        cv = c_ref[...].astype(F32) * v_ref[...].astype(F32)
        cvp = _prev8(ch_ref, t) * _prev8(vh_ref, t)
        y_ref[...] = (b_ref[...].astype(F32) * _conv(cv, cvp, w_ref)).astype(BF16)

    seg = lambda k: pl.BlockSpec((tb, C), lambda c, t: (t, k * nC + c))
    hseg = lambda k: pl.BlockSpec((HALO, C), _halo_prev_map(hb, lambda c: k * nC + c))
    return pl.pallas_call(
        body, name=name, out_shape=_S((T, d), BF16), grid=(nC, nT),
        in_specs=[seg(2), seg(3), hseg(3), seg(4), hseg(4), pl.BlockSpec((conv_w.shape[0], C), lambda c, t: (0, c))],
        out_specs=pl.BlockSpec((tb, C), lambda c, t: (t, c)),
        compiler_params=_cparams(("parallel", "parallel")),
    )(p, p, p, p, p, conv_w)


def _sc_bwd(p, dys, dp, conv_w, *, d, name):
    T = p.shape[0]
    C = _tile(d, C_EW)
    nC = d // C
    tb = _tile(T, TB, HALO)
    nT, hb = T // tb, tb // HALO
    kw = conv_w.shape[0]

    def body(b_ref, bn_ref, c_ref, ch_ref, v_ref, vh_ref, dy_ref, dyn_ref, w_ref, dp_in, dp_ref, dw_ref,
             st_b, st_c, st_v, sems):
        del dp_in
        c = pl.program_id(0)
        t = pl.program_id(1)
        last = t == nT - 1
        bv = b_ref[...].astype(F32)
        cvv = c_ref[...].astype(F32)
        vv = v_ref[...].astype(F32)
        dy = dy_ref[...].astype(F32)
        cv = cvv * vv
        cvp = _prev8(ch_ref, t) * _prev8(vh_ref, t)
        st_b[...] = (dy * _conv(cv, cvp, w_ref)).astype(BF16)
        dz = dy * bv
        dzn = _next8(dyn_ref, last) * _next8(bn_ref, last)
        _conv_dw(dw_ref, dz, cv, cvp, t == 0)
        dcv = _conv_t(dz, dzn, w_ref)
        st_c[...] = (dcv * vv).astype(BF16)
        st_v[...] = (dcv * cvv).astype(BF16)
        cps = [_store_block(st, dp_ref, sems.at[k], t * tb, (2 + k) * d + c * C)
               for k, st in enumerate((st_b, st_c, st_v))]
        for cp in cps:
            cp.start()
        for cp in cps:
            cp.wait()

    seg = lambda k: pl.BlockSpec((tb, C), lambda c, t: (t, k * nC + c))
    hseg = lambda k: pl.BlockSpec((HALO, C), _halo_prev_map(hb, lambda c: k * nC + c))
    last_h = T // HALO - 1
    nseg = lambda k: pl.BlockSpec((HALO, C), lambda c, t: (jnp.minimum((t + 1) * hb, last_h), k * nC + c))
    return pl.pallas_call(
        body, name=name, out_shape=(_S(dp.shape, dp.dtype), _S((kw, d), F32)), grid=(nC, nT),
        in_specs=[seg(2), nseg(2), seg(3), hseg(3), seg(4), hseg(4),
                  pl.BlockSpec((tb, C), lambda c, t: (t, c)), nseg(0),
                  pl.BlockSpec((kw, C), lambda c, t: (0, c)), ANY],
        out_specs=(ANY, pl.BlockSpec((kw, C), lambda c, t: (0, c))),
        scratch_shapes=[pltpu.VMEM((tb, C), BF16)] * 3 + [pltpu.SemaphoreType.DMA((3,))],
        input_output_aliases={9: 0},
        compiler_params=_cparams(("arbitrary", "arbitrary")),
    )(p, p, p, p, p, p, dys, dys, conv_w, dp)


def _merge_fwd(p, y_lru, y_sc, *, col0, name):
    T, D = y_lru.shape
    C = _tile(math.gcd(D, col0), 1024)
    nC = D // C
    k0 = col0 // C
    tb = _tile(T, 256, HALO)

    def body(gl_ref, gs_ref, yl_ref, ys_ref, o_ref):
        o_ref[...] = (jax.nn.sigmoid(gl_ref[...].astype(F32)) * yl_ref[...].astype(F32)
                      + jax.nn.sigmoid(gs_ref[...].astype(F32)) * ys_ref[...].astype(F32)).astype(BF16)

    blk = pl.BlockSpec((tb, C), lambda c, t: (t, c))
    return pl.pallas_call(
        body, name=name, out_shape=_S((T, D), BF16), grid=(nC, T // tb),
        in_specs=[pl.BlockSpec((tb, C), lambda c, t: (t, k0 + c)),
                  pl.BlockSpec((tb, C), lambda c, t: (t, k0 + nC + c)), blk, blk],
        out_specs=blk, compiler_params=_cparams(("parallel", "parallel")),
    )(p, p, y_lru, y_sc)


def _merge_bwd(p, y_lru, y_sc, dm, *, col0, name):
    T, D = y_lru.shape
    C = _tile(math.gcd(D, col0), 1024)
    nC = D // C
    k0 = col0 // C
    tb = _tile(T, 256, HALO)

    def body(gl_ref, gs_ref, yl_ref, ys_ref, dm_ref, dp_ref, dyl_ref, dys_ref, st_l, st_s, sems):
        c = pl.program_id(0)
        t = pl.program_id(1)
        dmv = dm_ref[...].astype(F32)
        sl = jax.nn.sigmoid(gl_ref[...].astype(F32))
        ss = jax.nn.sigmoid(gs_ref[...].astype(F32))
        dyl_ref[...] = (dmv * sl).astype(BF16)
        dys_ref[...] = (dmv * ss).astype(BF16)
        st_l[...] = (dmv * yl_ref[...].astype(F32) * sl * (1.0 - sl)).astype(BF16)
        st_s[...] = (dmv * ys_ref[...].astype(F32) * ss * (1.0 - ss)).astype(BF16)
        cl = _store_block(st_l, dp_ref, sems.at[0], t * tb, col0 + c * C)
        cs = _store_block(st_s, dp_ref, sems.at[1], t * tb, col0 + D + c * C)
        cl.start()
        cs.start()
        cl.wait()
        cs.wait()

    blk = pl.BlockSpec((tb, C), lambda c, t: (t, c))
    return pl.pallas_call(
        body, name=name, out_shape=(_S(p.shape, BF16), _S((T, D), BF16), _S((T, D), BF16)),
        grid=(nC, T // tb),
        in_specs=[pl.BlockSpec((tb, C), lambda c, t: (t, k0 + c)),
                  pl.BlockSpec((tb, C), lambda c, t: (t, k0 + nC + c)), blk, blk, blk],
        out_specs=(ANY, blk, blk),
        scratch_shapes=[pltpu.VMEM((tb, C), BF16), pltpu.VMEM((tb, C), BF16), pltpu.SemaphoreType.DMA((2,))],
        compiler_params=_cparams(("arbitrary", "arbitrary")),
    )(p, p, y_lru, y_sc, dm)


def _ffn_act_fwd(uu, conv_w, *, name):
    T = uu.shape[0]
    F = uu.shape[1] // 2
    C = _tile(F, C_EW)
    nC = F // C
    tb = _tile(T, 256, HALO)
    nT, hb = T // tb, tb // HALO
    kw = conv_w.shape[0]

    def body(g_ref, gh_ref, v_ref, vh_ref, wg_ref, wv_ref, o_ref):
        t = pl.program_id(1)
        cg = _conv(g_ref[...].astype(F32), _prev8(gh_ref, t), wg_ref)
        cv = _conv(v_ref[...].astype(F32), _prev8(vh_ref, t), wv_ref)
        o_ref[...] = (cg * jax.nn.sigmoid(cg) * cv).astype(BF16)

    seg = lambda k: pl.BlockSpec((tb, C), lambda c, t: (t, k * nC + c))
    hseg = lambda k: pl.BlockSpec((HALO, C), _halo_prev_map(hb, lambda c: k * nC + c))
    wseg = lambda k: pl.BlockSpec((kw, C), lambda c, t: (0, k * nC + c))
    return pl.pallas_call(
        body, name=name, out_shape=_S((T, F), BF16), grid=(nC, nT),
        in_specs=[seg(0), hseg(0), seg(1), hseg(1), wseg(0), wseg(1)],
        out_specs=pl.BlockSpec((tb, C), lambda c, t: (t, c)),
        compiler_params=_cparams(("parallel", "parallel")),
    )(uu, uu, uu, uu, conv_w, conv_w)


def _ffn_act_bwd(uu, dact, conv_w, *, name):
    T = uu.shape[0]
    F = uu.shape[1] // 2
    C = _tile(F, C_EW)
    nC = F // C
    tb = _tile(T, 256, HALO)
    nT, hb = T // tb, tb // HALO
    kw = conv_w.shape[0]

    def body(g_ref, gh_ref, v_ref, vh_ref, da_ref, wg_ref, wv_ref, du_ref, dwg_ref, dwv_ref,
             gn_s, vn_s, st_g, st_v, sems):
        c = pl.program_id(0)
        tr = pl.program_id(1)
        t = nT - 1 - tr
        first = tr == 0

        @pl.when(first)
        def _():
            gn_s[...] = jnp.zeros_like(gn_s)
            vn_s[...] = jnp.zeros_like(vn_s)

        ug = g_ref[...].astype(F32)
        uv = v_ref[...].astype(F32)
        ugp = _prev8(gh_ref, t)
        uvp = _prev8(vh_ref, t)
        cg = _conv(ug, ugp, wg_ref)
        cv = _conv(uv, uvp, wv_ref)
        da = da_ref[...].astype(F32)
        sg = jax.nn.sigmoid(cg)
        d_cg = da * cv * (sg * (1.0 + cg * (1.0 - sg)))
        d_cv = da * (cg * sg)
        _conv_dw(dwg_ref, d_cg, ug, ugp, first)
        _conv_dw(dwv_ref, d_cv, uv, uvp, first)
        st_g[...] = _conv_t(d_cg, gn_s[...], wg_ref).astype(BF16)
        st_v[...] = _conv_t(d_cv, vn_s[...], wv_ref).astype(BF16)
        gn_s[...] = d_cg[:SUB, :]
        vn_s[...] = d_cv[:SUB, :]
        cpg = _store_block(st_g, du_ref, sems.at[0], t * tb, c * C)
        cpv = _store_block(st_v, du_ref, sems.at[1], t * tb, F + c * C)
        cpg.start()
        cpv.start()
        cpg.wait()
        cpv.wait()

    seg = lambda k: pl.BlockSpec((tb, C), lambda c, tr: (nT - 1 - tr, k * nC + c))
    hseg = lambda k: pl.BlockSpec((HALO, C), lambda c, tr: (jnp.maximum((nT - 1 - tr) * hb - 1, 0), k * nC + c))
    wseg = lambda k: pl.BlockSpec((kw, C), lambda c, tr: (0, k * nC + c))
    dwb = pl.BlockSpec((kw, C), lambda c, tr: (0, c))
    return pl.pallas_call(
        body, name=name, out_shape=(_S(uu.shape, BF16), _S((kw, F), F32), _S((kw, F), F32)), grid=(nC, nT),
        in_specs=[seg(0), hseg(0), seg(1), hseg(1), pl.BlockSpec((tb, C), lambda c, tr: (nT - 1 - tr, c)),
                  wseg(0), wseg(1)],
        out_specs=(ANY, dwb, dwb),
        scratch_shapes=[pltpu.VMEM((SUB, C), F32), pltpu.VMEM((SUB, C), F32),
                        pltpu.VMEM((tb, C), BF16), pltpu.VMEM((tb, C), BF16), pltpu.SemaphoreType.DMA((2,))],
        compiler_params=_cparams(("arbitrary", "arbitrary")),
    )(uu, uu, uu, uu, dact, conv_w, conv_w)


def _place():
    x, y, c = lax.axis_index("x"), lax.axis_index("y"), lax.axis_index("c")
    return x, y, c


def _all_gather(shards, name):
    n = len(shards)

    def body(*refs):
        ins, outs = refs[:n], refs[n:2 * n]
        send_sems, recv_sems, local_sems = refs[2 * n:]
        x, y, c = _place()
        me, sibling = (x, y, c), (x, y, 1 - c)
        chips = [(1 - x, y), (x, 1 - y), (1 - x, 1 - y)]

        def idx(px, py, pc):
            return 4 * px + 2 * py + pc

        def copy(a, k, block, to, src=None):
            dst = outs[a].at[idx(*block)]
            return pltpu.make_async_remote_copy(
                src_ref=dst if src is None else src, dst_ref=dst,
                send_sem=send_sems.at[a, k], recv_sem=recv_sems.at[a, k],
                device_id=to, device_id_type=MESH)

        mine = [pltpu.make_async_copy(ins[a], outs[a].at[idx(*me)], local_sems.at[a]) for a in range(n)]
        for cp in mine:
            cp.start()
        first = []
        for a in range(n):
            first.append(copy(a, 0, me, sibling, src=ins[a]))
            first += [copy(a, 1 + j, me, (*chip, c), src=ins[a]) for j, chip in enumerate(chips)]
        for cp in first:
            cp.start()
        passed = []
        for a in range(n):
            for j, chip in enumerate(chips):
                copy(a, 1 + j, (*chip, c), me).wait_recv()
                cp = copy(a, 4 + j, (*chip, c), sibling)
                cp.start()
                passed.append(cp)
        for a in range(n):
            copy(a, 0, sibling, me).wait_recv()
            for j, chip in enumerate(chips):
                copy(a, 4 + j, (*chip, 1 - c), me).wait_recv()
        for cp in first + passed:
            cp.wait_send()
        for cp in mine:
            cp.wait()

    return pl.pallas_call(
        body, name=name,
        out_shape=tuple(_S((N_DEV,) + s.shape, s.dtype) for s in shards),
        in_specs=[ANY] * n, out_specs=tuple([ANY] * n),
        scratch_shapes=[pltpu.SemaphoreType.DMA((n, 7)), pltpu.SemaphoreType.DMA((n, 7)),
                        pltpu.SemaphoreType.DMA((n,))],
    )(*shards)


def _swap_halves(grads, name):
    n = len(grads)
    g4 = [g.reshape((N_CHIP, 2) + g.shape[1:]) for g in grads]

    def body(*refs):
        ins, outs = refs[:n], refs[n:2 * n]
        send_sems, recv_sems = refs[2 * n:]
        x, y, c = _place()
        cps = [pltpu.make_async_remote_copy(
            src_ref=ins[a].at[:, 1 - c], dst_ref=outs[a],
            send_sem=send_sems.at[a], recv_sem=recv_sems.at[a],
            device_id=(x, y, 1 - c), device_id_type=MESH) for a in range(n)]
        for cp in cps:
            cp.start()
        for cp in cps:
            cp.wait()

    return pl.pallas_call(
        body, name=name,
        out_shape=tuple(_S((N_CHIP,) + g.shape[1:], g.dtype) for g in grads),
        in_specs=[ANY] * n, out_specs=tuple([ANY] * n),
        scratch_shapes=[pltpu.SemaphoreType.DMA((n,)), pltpu.SemaphoreType.DMA((n,))],
    )(*g4)


def _add_halves(g, landed, c_idx, name):
    _, r, cc = g.shape
    g4 = g.reshape(N_CHIP, 2, r, cc)
    tr = _tile(r, 512, HALO)

    def body(c_ref, g_ref, l_ref, o_ref):
        del c_ref
        o_ref[...] = (g_ref[...].astype(F32) + l_ref[...].astype(F32)).astype(BF16)

    return pl.pallas_call(
        body, name=name, out_shape=_S((N_CHIP, r, cc), BF16),
        grid_spec=pltpu.PrefetchScalarGridSpec(
            num_scalar_prefetch=1, grid=(N_CHIP, r // tr),
            in_specs=[pl.BlockSpec((None, None, tr, cc), lambda q, i, c_ref: (q, c_ref[0], i, 0)),
                      pl.BlockSpec((None, tr, cc), lambda q, i, c_ref: (q, i, 0))],
            out_specs=pl.BlockSpec((None, tr, cc), lambda q, i, c_ref: (q, i, 0))),
        compiler_params=_cparams(("parallel", "parallel")),
    )(c_idx, g4, landed)


def _exchange_chips(parts, name):
    n = len(parts)

    def body(*refs):
        ins, outs = refs[:n], refs[n:2 * n]
        send_sems, recv_sems, local_sems = refs[2 * n:]
        x, y, c = _place()
        myq = 2 * x + y
        chips = [(1 - x, y), (x, 1 - y), (1 - x, 1 - y)]
        mine = [pltpu.make_async_copy(ins[a].at[myq], outs[a].at[myq], local_sems.at[a]) for a in range(n)]
        for cp in mine:
            cp.start()
        cps = []
        for a in range(n):
            for k, (px, py) in enumerate(chips):
                cps.append(pltpu.make_async_remote_copy(
                    src_ref=ins[a].at[2 * px + py], dst_ref=outs[a].at[myq],
                    send_sem=send_sems.at[a, k], recv_sem=recv_sems.at[a, k],
                    device_id=(px, py, c), device_id_type=MESH))
        for cp in cps:
            cp.start()
        for a in range(n):
            for k, (px, py) in enumerate(chips):
                pltpu.make_async_remote_copy(
                    src_ref=ins[a].at[myq], dst_ref=outs[a].at[2 * px + py],
                    send_sem=send_sems.at[a, k], recv_sem=recv_sems.at[a, k],
                    device_id=(px, py, c), device_id_type=MESH).wait_recv()
        for cp in cps:
            cp.wait_send()
        for cp in mine:
            cp.wait()

    return pl.pallas_call(
        body, name=name,
        out_shape=tuple(_S(p.shape, p.dtype) for p in parts),
        in_specs=[ANY] * n, out_specs=tuple([ANY] * n),
        scratch_shapes=[pltpu.SemaphoreType.DMA((n, 3)), pltpu.SemaphoreType.DMA((n, 3)),
                        pltpu.SemaphoreType.DMA((n,))],
    )(*parts)


def _all_reduce_small(pack, name):
    R = pack.shape[0]

    def body(p_ref, o_ref, buf, send_sems, recv_sems):
        x, y, c = _place()
        me = 4 * x + 2 * y + c
        buf[me] = p_ref[...]
        cps = []
        for k in range(N_DEV - 1):
            m = k + 1
            peer = (x ^ (m >> 2), y ^ ((m >> 1) & 1), c ^ (m & 1))
            cps.append(pltpu.make_async_remote_copy(
                src_ref=p_ref, dst_ref=buf.at[me], send_sem=send_sems.at[k], recv_sem=recv_sems.at[k],
                device_id=peer, device_id_type=MESH))
        for cp in cps:
            cp.start()
        for k in range(N_DEV - 1):
            m = k + 1
            peer_idx = 4 * (x ^ (m >> 2)) + 2 * (y ^ ((m >> 1) & 1)) + (c ^ (m & 1))
            pltpu.make_async_remote_copy(
                src_ref=p_ref, dst_ref=buf.at[peer_idx], send_sem=send_sems.at[k], recv_sem=recv_sems.at[k],
                device_id=(x, y, c), device_id_type=MESH).wait_recv()
        for cp in cps:
            cp.wait_send()
        acc = buf[0]
        for k in range(1, N_DEV):
            acc = acc + buf[k]
        o_ref[...] = acc

    return pl.pallas_call(
        body, name=name, out_shape=_S((R, LANES), F32),
        in_specs=[VMEM_SPEC], out_specs=VMEM_SPEC,
        scratch_shapes=[pltpu.VMEM((N_DEV, R, LANES), F32), pltpu.SemaphoreType.DMA((N_DEV - 1,)),
                        pltpu.SemaphoreType.DMA((N_DEV - 1,))],
        compiler_params=_cparams(),
    )(pack)


def _adamw_math(w, g, m, v):
    m = ADAM_B1 * m + (1.0 - ADAM_B1) * g
    v = ADAM_B2 * v + (1.0 - ADAM_B2) * (g * g)
    m_hat = m / (1.0 - ADAM_B1 ** ADAM_STEP)
    v_hat = v / (1.0 - ADAM_B2 ** ADAM_STEP)
    delta = -ADAM_LR * (m_hat / (jnp.sqrt(v_hat) + ADAM_EPS) + ADAM_WD * w)
    return delta, m, v


def _adamw_big(parts, w, m, v, name):
    r, cc = w.shape
    tr = _tile(r, 128, HALO)

    def body(p_ref, w_ref, m_ref, v_ref, g_ref, d_ref, nm_ref, nv_ref):
        g = p_ref[0].astype(F32)
        for q in range(1, N_CHIP):
            g = g + p_ref[q].astype(F32)
        g_ref[...] = g
        d_ref[...], nm_ref[...], nv_ref[...] = _adamw_math(w_ref[...], g, m_ref[...], v_ref[...])

    blk = pl.BlockSpec((tr, cc), lambda i: (i, 0))
    return pl.pallas_call(
        body, name=name, out_shape=tuple(_S((r, cc), F32) for _ in range(4)), grid=(r // tr,),
        in_specs=[pl.BlockSpec((N_CHIP, tr, cc), lambda i: (0, i, 0)), blk, blk, blk],
        out_specs=(blk, blk, blk, blk), compiler_params=_cparams(("parallel",)),
    )(parts, w, m, v)


def _adamw_small(ws, gs, ms, vs, name):
    n = len(ws)

    def body(*refs):
        w_r, g_r, m_r, v_r = refs[:n], refs[n:2 * n], refs[2 * n:3 * n], refs[3 * n:4 * n]
        d_r, nm_r, nv_r = refs[4 * n:5 * n], refs[5 * n:6 * n], refs[6 * n:7 * n]
        for k in range(n):
            d_r[k][...], nm_r[k][...], nv_r[k][...] = _adamw_math(w_r[k][...], g_r[k][...], m_r[k][...], v_r[k][...])

    shapes = tuple(_S(w.shape, F32) for w in ws)
    outs = pl.pallas_call(
        body, name=name, out_shape=shapes * 3,
        in_specs=[VMEM_SPEC] * (4 * n), out_specs=tuple([VMEM_SPEC] * (3 * n)),
        compiler_params=_cparams(),
    )(*ws, *gs, *ms, *vs)
    return outs[:n], outs[n:2 * n], outs[2 * n:]


def _block_diag(w, heads_per_block):
    H, hd, _ = w.shape
    nb = H // heads_per_block
    eye = jnp.eye(heads_per_block, dtype=w.dtype)
    w4 = w.reshape(nb, heads_per_block, hd, hd)
    return jnp.einsum("nhab,hg->nhagb", w4, eye).reshape(nb, heads_per_block * hd, heads_per_block * hd)


def _diag_blocks(bd, heads_per_block, hd):
    nb = bd.shape[0]
    b5 = bd.reshape(nb, heads_per_block, hd, heads_per_block, hd)
    return jnp.stack([b5[:, h, :, h, :] for h in range(heads_per_block)], axis=1).reshape(nb * heads_per_block, hd, hd)


def _as_rows(a):
    if a.ndim == 1:
        return a.reshape(-1, LANES) if a.shape[0] % LANES == 0 else a.reshape(1, -1)
    if a.ndim == 3:
        return a.reshape(-1, LANES) if (a.size % LANES == 0) else a.reshape(a.shape[0] * a.shape[1], a.shape[2])
    return a


def kernel(x, g_mix, w_in, lru_conv_w, lru_conv_b, lru_wa, lru_ba, lru_wx, lru_bx, lru_lambda, lru_w_out, sc_conv_w, sc_w_out, w_o, g_ffn, ffn_w_up, ffn_conv_w, ffn_w_down, g_final, loss_target, m_g_mix, m_w_in, m_lru_conv_w, m_lru_conv_b, m_lru_wa, m_lru_ba, m_lru_wx, m_lru_bx, m_lru_lambda, m_lru_w_out, m_sc_conv_w, m_sc_w_out, m_w_o, m_g_ffn, m_ffn_w_up, m_ffn_conv_w, m_ffn_w_down, m_g_final, v_g_mix, v_w_in, v_lru_conv_w, v_lru_conv_b, v_lru_wa, v_lru_ba, v_lru_wx, v_lru_bx, v_lru_lambda, v_lru_w_out, v_sc_conv_w, v_sc_w_out, v_w_o, v_g_ffn, v_ffn_w_up, v_ffn_conv_w, v_ffn_w_down, v_g_final):
    T, D = x.shape[1], x.shape[2]
    d_lru = lru_lambda.shape[0]
    d_sc = sc_conv_w.shape[1] * N_DEV
    F = ffn_w_down.shape[0] * N_DEV
    H = lru_wa.shape[0]
    assert d_lru == d_sc and H * HEAD_DIM == d_lru
    xs = x.reshape(T, D)
    tgt = loss_target.reshape(T, D)
    my_x, my_y, my_c = _place()
    me = 4 * my_x + 2 * my_y + my_c

    big = [w_in, lru_w_out, sc_w_out, w_o, ffn_w_up, ffn_w_down]
    big_names = ["w_in", "lru_w_out", "sc_w_out", "w_o", "ffn_w_up", "ffn_w_down"]
    big_bf = [_cast_bf16(w, "cast_" + nm) for w, nm in zip(big, big_names)]
    pad_rows = lambda a: jnp.pad(a, ((0, SUB - a.shape[0]), (0, 0)))
    gathered = _all_gather(big_bf + [pad_rows(lru_conv_w), pad_rows(sc_conv_w), pad_rows(ffn_conv_w)],
                           "all_gather_weights")
    W_in, W_lo, W_so, W_o8, W_up, W_dn8 = gathered[:6]
    W_o = W_o8.reshape(1, D, D)
    W_dn = W_dn8.reshape(1, F, D)
    full_cols = lambda g, kw: g[:, :kw, :].transpose(1, 0, 2).reshape(kw, -1)
    cw_lru = full_cols(gathered[6], lru_conv_w.shape[0])
    cw_sc = full_cols(gathered[7], sc_conv_w.shape[0])
    cw_ffn = full_cols(gathered[8], ffn_conv_w.shape[0])

    C = _tile(d_lru, C_LRU)
    hpb = C // HEAD_DIM
    wa_bd = _block_diag(lru_wa, hpb).astype(BF16)
    wx_bd = _block_diag(lru_wx, hpb).astype(BF16)
    cb, ba, bx, lam = (a.reshape(1, d_lru) for a in (lru_conv_b, lru_ba, lru_bx, lru_lambda))

    h1 = _rms_fwd(xs, g_mix, "rms_mix")
    p = _mm_nn(h1, W_in, out_dtype=BF16, name="mm_in")
    hs, yl_pre = _lru_fwd(p, cw_lru, cb, wa_bd, ba, wx_bd, bx, lam, name="lru_fwd")
    ys_pre = _sc_fwd(p, cw_sc, d=d_sc, name="sc_fwd")
    y_lru = _mm_nn(yl_pre, W_lo, out_dtype=BF16, name="mm_lru_out")
    y_sc = _mm_nn(ys_pre, W_so, out_dtype=BF16, name="mm_sc_out")
    gate0 = 2 * d_lru + 3 * d_sc
    merged = _merge_fwd(p, y_lru, y_sc, col0=gate0, name="merge_fwd")
    x1 = _mm_nn(merged, W_o, out_dtype=F32, residual=xs, name="mm_o")
    h2 = _rms_fwd(x1, g_ffn, "rms_ffn")
    uu = _mm_nn(h2, W_up, out_dtype=BF16, name="mm_up")
    act = _ffn_act_fwd(uu, cw_ffn, name="ffn_act_fwd")
    x2 = _mm_nn(act, W_dn, out_dtype=F32, residual=x1, name="mm_down")
    dx2, dx2b, loss_part, dg_final = _loss_head(x2, g_final, tgt, "loss_head")

    dact = _mm_nt(dx2b, W_dn, out_dtype=BF16, name="mm_down_dx")
    gW_dn = _mm_tn(act, dx2b, 1, out_dtype=BF16, name="mm_down_dw").reshape(N_DEV, F // N_DEV, D)
    duu, dcw_ffn_g, dcw_ffn_v = _ffn_act_bwd(uu, dact, cw_ffn, name="ffn_act_bwd")
    dh2 = _mm_nt(duu, W_up, out_dtype=BF16, name="mm_up_dx")
    gW_up = _mm_tn(h2, duu, N_DEV, out_dtype=BF16, name="mm_up_dw")
    dx1, dx1b, dg_ffn = _rms_bwd(x1, g_ffn, dh2, dx2, "rms_ffn_bwd")
    dmerged = _mm_nt(dx1b, W_o, out_dtype=BF16, name="mm_o_dx")
    gW_o = _mm_tn(merged, dx1b, 1, out_dtype=BF16, name="mm_o_dw").reshape(N_DEV, D // N_DEV, D)
    dp, dy_lru, dy_sc = _merge_bwd(p, y_lru, y_sc, dmerged, col0=gate0, name="merge_bwd")
    dyl_pre = _mm_nt(dy_lru, W_lo, out_dtype=BF16, name="mm_lru_out_dx")
    gW_lo = _mm_tn(yl_pre, dy_lru, N_DEV, out_dtype=BF16, name="mm_lru_out_dw")
    dys_pre = _mm_nt(dy_sc, W_so, out_dtype=BF16, name="mm_sc_out_dx")
    gW_so = _mm_tn(ys_pre, dy_sc, N_DEV, out_dtype=BF16, name="mm_sc_out_dw")
    dp, dcw_sc = _sc_bwd(p, dys_pre, dp, cw_sc, d=d_sc, name="sc_bwd")
    dp, dcw_lru, dcb, dwa_bd, dba, dwx_bd, dbx, dlam = _lru_bwd(
        p, hs, dyl_pre, dp, cw_lru, cb, wa_bd, ba, wx_bd, bx, lam, name="lru_bwd")
    dh1 = _mm_nt(dp, W_in, out_dtype=BF16, name="mm_in_dx")
    gW_in = _mm_tn(h1, dp, N_DEV, out_dtype=BF16, name="mm_in_dw")
    grad_x, _, dg_mix = _rms_bwd(xs, g_mix, dh1, dx1, "rms_mix_bwd")

    gbig = [gW_in, gW_lo, gW_so, gW_o, gW_up, gW_dn]
    landed = _swap_halves(gbig, "rs_swap_cores")
    c_idx = jnp.reshape(my_c, (1,)).astype(jnp.int32)
    chip_parts = [_add_halves(g, l, c_idx, "rs_add_" + nm) for g, l, nm in zip(gbig, landed, big_names)]
    mine = _exchange_chips(chip_parts, "rs_exchange_chips")
    big_m = [m_w_in, m_lru_w_out, m_sc_w_out, m_w_o, m_ffn_w_up, m_ffn_w_down]
    big_v = [v_w_in, v_lru_w_out, v_sc_w_out, v_w_o, v_ffn_w_up, v_ffn_w_down]
    big_out = {nm: _adamw_big(pt, w, m, v, "adamw_" + nm)
               for nm, pt, w, m, v in zip(big_names, mine, big, big_m, big_v)}

    dwa = _diag_blocks(dwa_bd, hpb, HEAD_DIM)
    dwx = _diag_blocks(dwx_bd, hpb, HEAD_DIM)
    dcw_ffn = jnp.concatenate([dcw_ffn_g, dcw_ffn_v], axis=1)
    small_full = [dg_mix, dcw_lru, dcb, dwa, dba, dwx, dbx, dlam, dcw_sc, dg_ffn, dcw_ffn, dg_final]
    flat = jnp.concatenate([a.reshape(-1) for a in small_full])
    n_flat = flat.shape[0]
    rows = -(-n_flat // (SUB * LANES)) * SUB
    pack = jnp.pad(flat, (0, rows * LANES - n_flat)).reshape(rows, LANES)
    total = _all_reduce_small(pack, "all_reduce_small").reshape(-1)
    sums, o = [], 0
    for a in small_full:
        sums.append(total[o:o + a.size].reshape(a.shape))
        o += a.size
    (sg_mix, scw_lru, scb, swa, sba, swx, sbx, slam, scw_sc, sg_ffn, scw_ffn, sg_final) = sums

    def my_cols(a):
        n = a.shape[1] // N_DEV
        return lax.dynamic_slice_in_dim(a, me * n, n, axis=1)

    small_names = ["g_mix", "lru_conv_w", "lru_conv_b", "lru_wa", "lru_ba", "lru_wx", "lru_bx", "lru_lambda",
                   "sc_conv_w", "g_ffn", "ffn_conv_w", "g_final"]
    small_w = [g_mix, lru_conv_w, lru_conv_b, lru_wa, lru_ba, lru_wx, lru_bx, lru_lambda, sc_conv_w, g_ffn,
               ffn_conv_w, g_final]
    small_m = [m_g_mix, m_lru_conv_w, m_lru_conv_b, m_lru_wa, m_lru_ba, m_lru_wx, m_lru_bx, m_lru_lambda,
               m_sc_conv_w, m_g_ffn, m_ffn_conv_w, m_g_final]
    small_v = [v_g_mix, v_lru_conv_w, v_lru_conv_b, v_lru_wa, v_lru_ba, v_lru_wx, v_lru_bx, v_lru_lambda,
               v_sc_conv_w, v_g_ffn, v_ffn_conv_w, v_g_final]
    small_g = [sg_mix.reshape(D), my_cols(scw_lru), scb.reshape(d_lru), swa, sba.reshape(d_lru), swx,
               sbx.reshape(d_lru), slam.reshape(d_lru), my_cols(scw_sc), sg_ffn.reshape(D), my_cols(scw_ffn),
               sg_final.reshape(D)]
    sd, snm, snv = _adamw_small([_as_rows(a) for a in small_w], [_as_rows(a) for a in small_g],
                                [_as_rows(a) for a in small_m], [_as_rows(a) for a in small_v], "adamw_small")
    small_out = {nm: (g, d.reshape(w.shape), nm_.reshape(w.shape), nv_.reshape(w.shape))
                 for nm, w, g, d, nm_, nv_ in zip(small_names, small_w, small_g, sd, snm, snv)}

    loss = lax.psum(loss_part[0, 0], AXES)
    order = ["g_mix", "w_in", "lru_conv_w", "lru_conv_b", "lru_wa", "lru_ba", "lru_wx", "lru_bx", "lru_lambda",
             "lru_w_out", "sc_conv_w", "sc_w_out", "w_o", "g_ffn", "ffn_w_up", "ffn_conv_w", "ffn_w_down", "g_final"]
    res = {**big_out, **small_out}
    return (loss, grad_x.reshape(x.shape),
            *[res[nm][0] for nm in order], *[res[nm][1] for nm in order],
            *[res[nm][2] for nm in order], *[res[nm][3] for nm in order])
```

```python
import functools
import math

import jax
import jax.numpy as jnp
from jax import lax
from jax.experimental import pallas as pl
from jax.experimental.pallas import tpu as pltpu

F32, BF16 = jnp.float32, jnp.bfloat16
MESH = pl.DeviceIdType.MESH
N_DEV = 8
N_CHIP = 4
AXES = ("x", "y", "c")

EPS = 1e-6
LRU_C = 8.0
HEAD_DIM = 64
ADAM_LR, ADAM_B1, ADAM_B2, ADAM_EPS, ADAM_WD, ADAM_STEP = 0.001, 0.9, 0.999, 1e-08, 0.01, 10

VMEM_LIMIT = 48 * 1024 * 1024
LANES = 128
SUB = 8
HALO = 16
TB = 512
C_LRU = 256
C_EW = 512
TM, TN, TK = 512, 1536, 2048


def _tile(n, pref, align=LANES):
    best = None
    for d in range(align, min(n, pref) + 1, align):
        if n % d == 0:
            best = d
    return best or n


def _cparams(sem=None, vmem=VMEM_LIMIT):
    kw = dict(vmem_limit_bytes=vmem)
    if sem is not None:
        kw["dimension_semantics"] = sem
    return pltpu.CompilerParams(**kw)


def _S(shape, dtype):
    return jax.ShapeDtypeStruct(shape, dtype)


ANY = pl.BlockSpec(memory_space=pl.ANY)
VMEM_SPEC = pl.BlockSpec(memory_space=pltpu.VMEM)


class _Task:
    def __init__(self, arrays, aliased, start, wait):
        self.arrays, self.aliased, self.start, self.wait = arrays, aliased, start, wait


def _call(name, grid, compute, in_specs, args, out_shape, out_specs, scratch, tasks=()):
    n_in, n_out, n_scr = len(args), len(out_shape), len(scratch)
    x_in, x_out, aliases, where = [], [], {}, []
    for t in tasks:
        places = []
        for k, arr in enumerate(t.arrays):
            if k in t.aliased:
                aliases[n_in + len(x_in)] = n_out + len(x_out)
                places.append(("out", len(x_out)))
                x_out.append(_S(arr.shape, arr.dtype))
            else:
                places.append(("in", len(x_in)))
            x_in.append(arr)
        where.append(places)
    n_xi, n_xo = len(x_in), len(x_out)

    def body(*refs):
        ins, xi = refs[:n_in], refs[n_in:n_in + n_xi]
        o0 = n_in + n_xi
        outs, xo = refs[o0:o0 + n_out], refs[o0 + n_out:o0 + n_out + n_xo]
        s0 = o0 + n_out + n_xo
        scr, sems = refs[s0:s0 + n_scr], refs[s0 + n_scr:]
        ids = [pl.program_id(a) for a in range(len(grid))]

        def task_refs(ti):
            return [xo[i] if kind == "out" else xi[i] for kind, i in where[ti]]

        if tasks:
            first = functools.reduce(jnp.logical_and, [i == 0 for i in ids])

            @pl.when(first)
            def _():
                for ti, t in enumerate(tasks):
                    t.start(task_refs(ti), *sems[3 * ti:3 * ti + 3])

        compute(*ins, *outs, *scr)
        if tasks:
            last = functools.reduce(jnp.logical_and, [i == g - 1 for i, g in zip(ids, grid)])

            @pl.when(last)
            def _():
                for ti, t in enumerate(tasks):
                    t.wait(task_refs(ti), *sems[3 * ti:3 * ti + 3])

    sem_shapes = []
    for _ in tasks:
        sem_shapes += [pltpu.SemaphoreType.DMA((3,)), pltpu.SemaphoreType.DMA((3,)), pltpu.SemaphoreType.DMA((1,))]
    res = pl.pallas_call(
        body, name=name, grid=grid,
        in_specs=list(in_specs) + [ANY] * n_xi,
        out_specs=tuple(out_specs) + (ANY,) * n_xo,
        out_shape=tuple(out_shape) + tuple(x_out),
        scratch_shapes=list(scratch) + sem_shapes,
        input_output_aliases=aliases,
        compiler_params=_cparams(("arbitrary",) * len(grid)),
    )(*args, *x_in)
    outs, passed, o = res[:n_out], [], n_out
    for places in where:
        k = sum(1 for kind, _ in places if kind == "out")
        passed.append(list(res[o:o + k]))
        o += k
    return outs, passed


def _mm_nn(a, w3, *, out_dtype, name, residual=None, tm=TM, tn=TN, tk=TK, tasks=()):
    M, K = a.shape
    G, _, n = w3.shape
    tm, tn, tk = _tile(M, tm, SUB), _tile(n, tn), _tile(K, tk)
    nj, nk = n // tn, K // tk

    def compute(*refs):
        if residual is None:
            a_ref, w_ref, o_ref = refs[:3]
            r_ref = None
        else:
            a_ref, w_ref, r_ref, o_ref = refs[:4]

        def finish(r):
            if r_ref is not None:
                r = r + r_ref[...]
            o_ref[...] = r.astype(o_ref.dtype)

        if nk == 1:
            finish(jnp.dot(a_ref[...], w_ref[...], preferred_element_type=F32))
            return
        acc = refs[-1]
        k = pl.program_id(3)

        @pl.when(k == 0)
        def _():
            acc[...] = jnp.zeros_like(acc)

        acc[...] += jnp.dot(a_ref[...], w_ref[...], preferred_element_type=F32)

        @pl.when(k == nk - 1)
        def _():
            finish(acc[...])

    in_specs = [pl.BlockSpec((tm, tk), lambda g, j, i, k: (i, k)),
                pl.BlockSpec((None, tk, tn), lambda g, j, i, k: (g, k, j))]
    args = [a, w3]
    if residual is not None:
        in_specs.append(pl.BlockSpec((tm, tn), lambda g, j, i, k: (i, g * nj + j)))
        args.append(residual)
    outs, passed = _call(
        name, (G, nj, M // tm, nk), compute, in_specs, args, [_S((M, G * n), out_dtype)],
        [pl.BlockSpec((tm, tn), lambda g, j, i, k: (i, g * nj + j))],
        [] if nk == 1 else [pltpu.VMEM((tm, tn), F32)], tasks)
    return (outs[0], passed) if tasks else outs[0]


def _mm_nt(dy, w3, *, out_dtype, name, tm=1024, tko=1024, tn=TN, tasks=()):
    M, _ = dy.shape
    G, K, n = w3.shape
    tm, tko, tn = _tile(M, tm, SUB), _tile(K, tko), _tile(n, tn)
    nj = n // tn
    nr = G * nj

    def compute(dy_ref, w_ref, o_ref, acc):
        r = pl.program_id(2)

        @pl.when(r == 0)
        def _():
            acc[...] = jnp.zeros_like(acc)

        acc[...] += lax.dot_general(dy_ref[...], w_ref[...], (((1,), (1,)), ((), ())),
                                    preferred_element_type=F32)

        @pl.when(r == nr - 1)
        def _():
            o_ref[...] = acc[...].astype(o_ref.dtype)

    outs, passed = _call(
        name, (K // tko, M // tm, nr), compute,
        [pl.BlockSpec((tm, tn), lambda ko, i, r: (i, r)),
         pl.BlockSpec((None, tko, tn), lambda ko, i, r: (r // nj, ko, r % nj))],
        [dy, w3], [_S((M, K), out_dtype)], [pl.BlockSpec((tm, tko), lambda ko, i, r: (i, ko))],
        [pltpu.VMEM((tm, tko), F32)], tasks)
    return (outs[0], passed) if tasks else outs[0]


def _mm_tn(a, dy, G, *, out_dtype, name, tk=1024, tn=TN, tt=1024, tasks=()):
    M, K = a.shape
    n = dy.shape[1] // G
    tk, tn, tt = _tile(K, tk), _tile(n, tn), _tile(M, tt, SUB)
    nj, nt = n // tn, M // tt

    def compute(a_ref, dy_ref, o_ref, acc):
        t = pl.program_id(3)

        @pl.when(t == 0)
        def _():
            acc[...] = jnp.zeros_like(acc)

        acc[...] += lax.dot_general(a_ref[...], dy_ref[...], (((0,), (0,)), ((), ())),
                                    preferred_element_type=F32)

        @pl.when(t == nt - 1)
        def _():
            o_ref[...] = acc[...].astype(o_ref.dtype)

    outs, passed = _call(
        name, (G, nj, K // tk, nt), compute,
        [pl.BlockSpec((tt, tk), lambda g, j, k, t: (t, k)),
         pl.BlockSpec((tt, tn), lambda g, j, k, t: (t, g * nj + j))],
        [a, dy], [_S((G, K, n), out_dtype)], [pl.BlockSpec((None, tk, tn), lambda g, j, k, t: (g, k, j))],
        [pltpu.VMEM((tk, tn), F32)], tasks)
    return (outs[0], passed) if tasks else outs[0]


def _cast_bf16(w, name):
    R, C = w.shape
    tr = _tile(R, 512, SUB)

    def body(w_ref, o_ref):
        o_ref[...] = w_ref[...].astype(BF16)

    return pl.pallas_call(
        body, name=name, out_shape=_S((R, C), BF16), grid=(R // tr,),
        in_specs=[pl.BlockSpec((tr, C), lambda i: (i, 0))],
        out_specs=pl.BlockSpec((tr, C), lambda i: (i, 0)),
        compiler_params=_cparams(("parallel",)),
    )(w)


def _down(cur, prev8, j):
    return pltpu.roll(jnp.concatenate([prev8, cur], axis=0), j, 0)[SUB:, :]


def _up(cur, next8, j):
    n = cur.shape[0] + SUB
    return pltpu.roll(jnp.concatenate([cur, next8], axis=0), n - j, 0)[:cur.shape[0], :]


def _conv(x, prev8, w_ref):
    kw = w_ref.shape[0]
    y = x * w_ref[pl.ds(kw - 1, 1), :]
    for k in range(kw - 1):
        y = y + _down(x, prev8, kw - 1 - k) * w_ref[pl.ds(k, 1), :]
    return y


def _conv_t(dy, next8, w_ref):
    kw = w_ref.shape[0]
    dx = dy * w_ref[pl.ds(kw - 1, 1), :]
    for k in range(kw - 1):
        dx = dx + _up(dy, next8, kw - 1 - k) * w_ref[pl.ds(k, 1), :]
    return dx


def _conv_dw(dw_ref, dy, x, prev8, first):
    kw = dw_ref.shape[0]

    @pl.when(first)
    def _():
        dw_ref[...] = jnp.zeros_like(dw_ref)

    for k in range(kw):
        xs = x if k == kw - 1 else _down(x, prev8, kw - 1 - k)
        dw_ref[pl.ds(k, 1), :] += jnp.sum(dy * xs, axis=0, keepdims=True)


def _acc(ref, val, first):
    @pl.when(first)
    def _():
        ref[...] = jnp.zeros_like(ref)

    ref[...] += val


def _acc_row(ref, val, first):
    _acc(ref, jnp.sum(val, axis=0, keepdims=True), first)


def _prev8(h_ref, t):
    return jnp.where(t > 0, h_ref[...].astype(F32)[HALO - SUB:, :], 0.0)


def _next8(h_ref, is_last):
    return jnp.where(is_last, 0.0, h_ref[...].astype(F32)[:SUB, :])


_GELU_K0 = math.sqrt(2.0 / math.pi)
_GELU_K1 = 0.044715


def _gelu_and_grad(x):
    x2 = x * x
    th = jnp.tanh(_GELU_K0 * x * (1.0 + _GELU_K1 * x2))
    g = 0.5 * x * (1.0 + th)
    dg = 0.5 * (1.0 + th) + 0.5 * x * (1.0 - th * th) * (_GELU_K0 * (1.0 + 3.0 * _GELU_K1 * x2))
    return g, dg


def _neg_expm1(z):
    series = -z * (1.0 + z * (0.5 + z * (1.0 / 6.0 + z * (1.0 / 24.0))))
    return jnp.where(z > -0.03, series, 1.0 - jnp.exp(z))


def _store_block(stage_ref, dst_hbm, sem, row0, col0):
    tb, c = stage_ref.shape
    return pltpu.make_async_copy(stage_ref, dst_hbm.at[pl.ds(row0, tb), pl.ds(col0, c)], sem)


def _halo_prev_map(hb, col_fn):
    return lambda c, t: (jnp.maximum(t * hb - 1, 0), col_fn(c))


def _rms_fwd(x, g, name):
    T, D = x.shape
    tb = _tile(T, TB, SUB)

    def body(x_ref, g_ref, o_ref):
        xv = x_ref[...]
        rstd = lax.rsqrt(jnp.mean(xv * xv, axis=-1, keepdims=True) + EPS)
        o_ref[...] = (xv * rstd * g_ref[...]).astype(BF16)

    return pl.pallas_call(
        body, name=name, out_shape=_S((T, D), BF16), grid=(T // tb,),
        in_specs=[pl.BlockSpec((tb, D), lambda i: (i, 0)), pl.BlockSpec((1, D), lambda i: (0, 0))],
        out_specs=pl.BlockSpec((tb, D), lambda i: (i, 0)),
        compiler_params=_cparams(("parallel",)),
    )(x, g.reshape(1, D))


def _rms_bwd(x, g, dh, dres, name):
    T, D = x.shape
    tb = _tile(T, 256, SUB)

    def body(x_ref, g_ref, dh_ref, dr_ref, dx_ref, dxb_ref, dg_ref):
        i = pl.program_id(0)
        xv = x_ref[...]
        rstd = lax.rsqrt(jnp.mean(xv * xv, axis=-1, keepdims=True) + EPS)
        xn = xv * rstd
        dhv = dh_ref[...].astype(F32)
        _acc_row(dg_ref, dhv * xn, i == 0)
        dxn = dhv * g_ref[...]
        dx = dr_ref[...] + rstd * (dxn - xn * jnp.mean(dxn * xn, axis=-1, keepdims=True))
        dx_ref[...] = dx
        dxb_ref[...] = dx.astype(BF16)

    blk = pl.BlockSpec((tb, D), lambda i: (i, 0))
    vec = pl.BlockSpec((1, D), lambda i: (0, 0))
    return pl.pallas_call(
        body, name=name, out_shape=(_S((T, D), F32), _S((T, D), BF16), _S((1, D), F32)),
        grid=(T // tb,), in_specs=[blk, vec, blk, blk], out_specs=(blk, blk, vec),
        compiler_params=_cparams(("arbitrary",)),
    )(x, g.reshape(1, D), dh, dres)


def _loss_head(x2, g, target, name):
    T, D = x2.shape
    tb = _tile(T, 256, SUB)

    def body(x_ref, g_ref, t_ref, dx_ref, dxb_ref, loss_ref, dg_ref):
        i = pl.program_id(0)
        xv = x_ref[...]
        rstd = lax.rsqrt(jnp.mean(xv * xv, axis=-1, keepdims=True) + EPS)
        xn = xv * rstd
        err = xn * g_ref[...] - t_ref[...]
        part = 0.5 * jnp.sum(jnp.mean(err * err, axis=-1, keepdims=True), axis=0, keepdims=True)
        part = jnp.broadcast_to(part, (1, LANES))
        _acc(loss_ref, part, i == 0)
        dy = err * (1.0 / D)
        _acc_row(dg_ref, dy * xn, i == 0)
        dxn = dy * g_ref[...]
        dx = rstd * (dxn - xn * jnp.mean(dxn * xn, axis=-1, keepdims=True))
        dx_ref[...] = dx
        dxb_ref[...] = dx.astype(BF16)

    blk = pl.BlockSpec((tb, D), lambda i: (i, 0))
    vec = pl.BlockSpec((1, D), lambda i: (0, 0))
    return pl.pallas_call(
        body, name=name,
        out_shape=(_S((T, D), F32), _S((T, D), BF16), _S((1, LANES), F32), _S((1, D), F32)),
        grid=(T // tb,), in_specs=[blk, vec, blk],
        out_specs=(blk, blk, pl.BlockSpec((1, LANES), lambda i: (0, 0)), vec),
        compiler_params=_cparams(("arbitrary",)),
    )(x2, g.reshape(1, D), target)


def _lru_gates(xc, wa_ref, ba_ref, wx_ref, bx_ref, lam_ref):
    xcb = xc.astype(BF16)
    r = jax.nn.sigmoid(jnp.dot(xcb, wa_ref[...], preferred_element_type=F32) + ba_ref[...])
    i = jax.nn.sigmoid(jnp.dot(xcb, wx_ref[...], preferred_element_type=F32) + bx_ref[...])
    sp = jax.nn.softplus(-lam_ref[...])
    log_a = (-LRU_C * sp) * r
    a = jnp.exp(log_a)
    s = jnp.sqrt(_neg_expm1(2.0 * log_a))
    return xcb, r, i, a, s


def _lru_fwd(p, conv_w, conv_b, wa_bd, ba, wx_bd, bx, lam, *, name):
    T = p.shape[0]
    d = lam.shape[-1]
    C = _tile(d, C_LRU)
    nC = d // C
    tb = _tile(T, TB, HALO)
    nT, hb, nt = T // tb, tb // HALO, tb // SUB

    def body(x_ref, xh_ref, g_ref, cw_ref, cb_ref, wa_ref, ba_ref, wx_ref, bx_ref, lam_ref,
             hs_ref, y_ref, a_s, u_s, h_s):
        t = pl.program_id(1)

        @pl.when(t == 0)
        def _():
            h_s[...] = jnp.zeros_like(h_s)

        x = x_ref[...].astype(F32)
        xc = _conv(x, _prev8(xh_ref, t), cw_ref) + cb_ref[...]
        _, r, i, a, s = _lru_gates(xc, wa_ref, ba_ref, wx_ref, bx_ref, lam_ref)
        a_s[...] = a
        u_s[...] = s * (i * xc)
        row = lax.broadcasted_iota(jnp.int32, (SUB, C), 0)

        def step(k, h):
            o = pl.multiple_of(k * SUB, SUB)
            A = a_s[pl.ds(o, SUB), :]
            B = u_s[pl.ds(o, SUB), :]
            for sh in (1, 2, 4):
                m = row >= sh
                Ap = pltpu.roll(A, sh, 0)
                Bp = pltpu.roll(B, sh, 0)
                B = jnp.where(m, A * Bp + B, B)
                A = jnp.where(m, A * Ap, A)
            hs = A * h + B
            hs_ref[pl.ds(o, SUB), :] = hs
            return jnp.broadcast_to(hs[SUB - 1:SUB, :], (SUB, C))

        h_s[...] = lax.fori_loop(0, nt, step, h_s[...])
        gel, _ = _gelu_and_grad(g_ref[...].astype(F32))
        y_ref[...] = (gel * hs_ref[...]).astype(BF16)

    vec = pl.BlockSpec((1, C), lambda c, t: (0, c))
    sq = pl.BlockSpec((None, C, C), lambda c, t: (c, 0, 0))
    return pl.pallas_call(
        body, name=name, out_shape=(_S((T, d), F32), _S((T, d), BF16)),
        grid=(nC, nT),
        in_specs=[pl.BlockSpec((tb, C), lambda c, t: (t, c)),
                  pl.BlockSpec((HALO, C), _halo_prev_map(hb, lambda c: c)),
                  pl.BlockSpec((tb, C), lambda c, t: (t, nC + c)),
                  pl.BlockSpec((conv_w.shape[0], C), lambda c, t: (0, c)),
                  vec, sq, vec, sq, vec, vec],
        out_specs=(pl.BlockSpec((tb, C), lambda c, t: (t, c)), pl.BlockSpec((tb, C), lambda c, t: (t, c))),
        scratch_shapes=[pltpu.VMEM((tb, C), F32), pltpu.VMEM((tb, C), F32), pltpu.VMEM((SUB, C), F32)],
        compiler_params=_cparams(("parallel", "arbitrary")),
    )(p, p, p, conv_w, conv_b, wa_bd, ba, wx_bd, bx, lam)


def _lru_bwd(p, hs, dyl, dp, conv_w, conv_b, wa_bd, ba, wx_bd, bx, lam, *, name):
    T = p.shape[0]
    d = lam.shape[-1]
    C = _tile(d, C_LRU)
    nC = d // C
    tb = _tile(T, TB, HALO)
    nT, hb, nt = T // tb, tb // HALO, tb // SUB
    kw = conv_w.shape[0]

    def body(x_ref, xh_ref, g_ref, hs_ref, hh_ref, dy_ref, cw_ref, cb_ref, wa_ref, ba_ref, wx_ref, bx_ref,
             lam_ref, dp_in, dp_ref, dcw_ref, dcb_ref, dwa_ref, dba_ref, dwx_ref, dbx_ref, dlam_ref,
             b_s, g_s, dh_s, an_s, dhn_s, dxn_s, st_x, st_g, sems):
        del dp_in
        c = pl.program_id(0)
        tr = pl.program_id(1)
        t = nT - 1 - tr
        first = tr == 0

        @pl.when(first)
        def _():
            an_s[...] = jnp.zeros_like(an_s)
            dhn_s[...] = jnp.zeros_like(dhn_s)
            dxn_s[...] = jnp.zeros_like(dxn_s)

        x = x_ref[...].astype(F32)
        xprev = _prev8(xh_ref, t)
        xc = _conv(x, xprev, cw_ref) + cb_ref[...]
        xcb, r, i, a, s = _lru_gates(xc, wa_ref, ba_ref, wx_ref, bx_ref, lam_ref)
        hsv = hs_ref[...]
        dy = dy_ref[...].astype(F32)
        gel, dgel = _gelu_and_grad(g_ref[...].astype(F32))
        st_g[...] = (dy * hsv * dgel).astype(BF16)

        b_s[...] = _up(a, an_s[...], 1)
        g_s[...] = dy * gel
        row = lax.broadcasted_iota(jnp.int32, (SUB, C), 0)

        def step(k, carry):
            o = pl.multiple_of((nt - 1 - k) * SUB, SUB)
            B = b_s[pl.ds(o, SUB), :]
            G = g_s[pl.ds(o, SUB), :]
            for sh in (1, 2, 4):
                m = row < SUB - sh
                Bn = pltpu.roll(B, SUB - sh, 0)
                Gn = pltpu.roll(G, SUB - sh, 0)
                G = jnp.where(m, B * Gn + G, G)
                B = jnp.where(m, B * Bn, B)
            dh = B * carry + G
            dh_s[pl.ds(o, SUB), :] = dh
            return jnp.broadcast_to(dh[0:1, :], (SUB, C))

        dhn_s[...] = lax.fori_loop(0, nt, step, dhn_s[...])
        an_s[...] = a[:SUB, :]
        dh = dh_s[...]

        hprev = _down(hsv, jnp.where(t > 0, hh_ref[...][HALO - SUB:, :], 0.0), 1)
        d_a = dh * hprev
        ixc = i * xc
        d_s = dh * ixc
        d_i = dh * s * xc
        d_xc = dh * s * i
        d_l = d_a * a - d_s * (a * a) / s
        sp = jax.nn.softplus(-lam_ref[...])
        _acc_row(dlam_ref, d_l * r * (LRU_C * jax.nn.sigmoid(-lam_ref[...])), first)
        d_zr = (d_l * (-LRU_C * sp)) * r * (1.0 - r)
        d_zi = d_i * i * (1.0 - i)
        _acc_row(dba_ref, d_zr, first)
        _acc_row(dbx_ref, d_zi, first)
        d_zrb = d_zr.astype(BF16)
        d_zib = d_zi.astype(BF16)
        tn_dims = (((0,), (0,)), ((), ()))
        nt_dims = (((1,), (1,)), ((), ()))
        gwa = lax.dot_general(xcb, d_zrb, tn_dims, preferred_element_type=F32)
        gwx = lax.dot_general(xcb, d_zib, tn_dims, preferred_element_type=F32)
        _acc(dwa_ref, gwa, first)
        _acc(dwx_ref, gwx, first)
        d_xc = (d_xc + lax.dot_general(d_zrb, wa_ref[...], nt_dims, preferred_element_type=F32)
                + lax.dot_general(d_zib, wx_ref[...], nt_dims, preferred_element_type=F32))
        _acc_row(dcb_ref, d_xc, first)
        _conv_dw(dcw_ref, d_xc, x, xprev, first)
        st_x[...] = _conv_t(d_xc, dxn_s[...], cw_ref).astype(BF16)
        dxn_s[...] = d_xc[:SUB, :]

        cx = _store_block(st_x, dp_ref, sems.at[0], t * tb, c * C)
        cg = _store_block(st_g, dp_ref, sems.at[1], t * tb, d + c * C)
        cx.start()
        cg.start()
        cx.wait()
        cg.wait()

    rev = lambda c, tr: (nT - 1 - tr, c)
    vec = pl.BlockSpec((1, C), lambda c, tr: (0, c))
    sq = pl.BlockSpec((None, C, C), lambda c, tr: (c, 0, 0))
    cwb = pl.BlockSpec((kw, C), lambda c, tr: (0, c))
    halo_prev = lambda c, tr: (jnp.maximum((nT - 1 - tr) * hb - 1, 0), c)
    return pl.pallas_call(
        body, name=name,
        out_shape=(_S(dp.shape, dp.dtype), _S((kw, d), F32), _S((1, d), F32), _S((nC, C, C), F32), _S((1, d), F32),
                   _S((nC, C, C), F32), _S((1, d), F32), _S((1, d), F32)),
        grid=(nC, nT),
        in_specs=[pl.BlockSpec((tb, C), rev),
                  pl.BlockSpec((HALO, C), halo_prev),
                  pl.BlockSpec((tb, C), lambda c, tr: (nT - 1 - tr, nC + c)),
                  pl.BlockSpec((tb, C), rev),
                  pl.BlockSpec((HALO, C), halo_prev),
                  pl.BlockSpec((tb, C), rev),
                  cwb, vec, sq, vec, sq, vec, vec, ANY],
        out_specs=(ANY, cwb, vec, sq, vec, sq, vec, vec),
        scratch_shapes=[pltpu.VMEM((tb, C), F32), pltpu.VMEM((tb, C), F32), pltpu.VMEM((tb, C), F32),
                        pltpu.VMEM((SUB, C), F32), pltpu.VMEM((SUB, C), F32), pltpu.VMEM((SUB, C), F32),
                        pltpu.VMEM((tb, C), BF16), pltpu.VMEM((tb, C), BF16), pltpu.SemaphoreType.DMA((2,))],
        input_output_aliases={13: 0},
        compiler_params=_cparams(("arbitrary", "arbitrary")),
    )(p, p, p, hs, hs, dyl, conv_w, conv_b, wa_bd, ba, wx_bd, bx, lam, dp)


def _sc_fwd(p, conv_w, *, d, name):
    T = p.shape[0]
    C = _tile(d, C_EW)
    nC = d // C
    tb = _tile(T, TB, HALO)
    nT, hb = T // tb, tb // HALO

    def body(b_ref, c_ref, ch_ref, v_ref, vh_ref, w_ref, y_ref):
        t = pl.program_id(1)
        cv = c_ref[...].astype(F32) * v_ref[...].astype(F32)
        cvp = _prev8(ch_ref, t) * _prev8(vh_ref, t)
        y_ref[...] = (b_ref[...].astype(F32) * _conv(cv, cvp, w_ref)).astype(BF16)

    seg = lambda k: pl.BlockSpec((tb, C), lambda c, t: (t, k * nC + c))
    hseg = lambda k: pl.BlockSpec((HALO, C), _halo_prev_map(hb, lambda c: k * nC + c))
    return pl.pallas_call(
        body, name=name, out_shape=_S((T, d), BF16), grid=(nC, nT),
        in_specs=[seg(2), seg(3), hseg(3), seg(4), hseg(4), pl.BlockSpec((conv_w.shape[0], C), lambda c, t: (0, c))],
        out_specs=pl.BlockSpec((tb, C), lambda c, t: (t, c)),
        compiler_params=_cparams(("parallel", "parallel")),
    )(p, p, p, p, p, conv_w)


def _sc_bwd(p, dys, dp, conv_w, *, d, name):
    T = p.shape[0]
    C = _tile(d, C_EW)
    nC = d // C
    tb = _tile(T, TB, HALO)
    nT, hb = T // tb, tb // HALO
    kw = conv_w.shape[0]

    def body(b_ref, bn_ref, c_ref, ch_ref, v_ref, vh_ref, dy_ref, dyn_ref, w_ref, dp_in, dp_ref, dw_ref,
             st_b, st_c, st_v, sems):
        del dp_in
        c = pl.program_id(0)
        t = pl.program_id(1)
        last = t == nT - 1
        bv = b_ref[...].astype(F32)
        cvv = c_ref[...].astype(F32)
        vv = v_ref[...].astype(F32)
        dy = dy_ref[...].astype(F32)
        cv = cvv * vv
        cvp = _prev8(ch_ref, t) * _prev8(vh_ref, t)
        st_b[...] = (dy * _conv(cv, cvp, w_ref)).astype(BF16)
        dz = dy * bv
        dzn = _next8(dyn_ref, last) * _next8(bn_ref, last)
        _conv_dw(dw_ref, dz, cv, cvp, t == 0)
        dcv = _conv_t(dz, dzn, w_ref)
        st_c[...] = (dcv * vv).astype(BF16)
        st_v[...] = (dcv * cvv).astype(BF16)
        cps = [_store_block(st, dp_ref, sems.at[k], t * tb, (2 + k) * d + c * C)
               for k, st in enumerate((st_b, st_c, st_v))]
        for cp in cps:
            cp.start()
        for cp in cps:
            cp.wait()

    seg = lambda k: pl.BlockSpec((tb, C), lambda c, t: (t, k * nC + c))
    hseg = lambda k: pl.BlockSpec((HALO, C), _halo_prev_map(hb, lambda c: k * nC + c))
    last_h = T // HALO - 1
    nseg = lambda k: pl.BlockSpec((HALO, C), lambda c, t: (jnp.minimum((t + 1) * hb, last_h), k * nC + c))
    return pl.pallas_call(
        body, name=name, out_shape=(_S(dp.shape, dp.dtype), _S((kw, d), F32)), grid=(nC, nT),
        in_specs=[seg(2), nseg(2), seg(3), hseg(3), seg(4), hseg(4),
                  pl.BlockSpec((tb, C), lambda c, t: (t, c)), nseg(0),
                  pl.BlockSpec((kw, C), lambda c, t: (0, c)), ANY],
        out_specs=(ANY, pl.BlockSpec((kw, C), lambda c, t: (0, c))),
        scratch_shapes=[pltpu.VMEM((tb, C), BF16)] * 3 + [pltpu.SemaphoreType.DMA((3,))],
        input_output_aliases={9: 0},
        compiler_params=_cparams(("arbitrary", "arbitrary")),
    )(p, p, p, p, p, p, dys, dys, conv_w, dp)


def _merge_fwd(p, y_lru, y_sc, *, col0, name):
    T, D = y_lru.shape
    C = _tile(math.gcd(D, col0), 1024)
    nC = D // C
    k0 = col0 // C
    tb = _tile(T, 256, HALO)

    def body(gl_ref, gs_ref, yl_ref, ys_ref, o_ref):
        o_ref[...] = (jax.nn.sigmoid(gl_ref[...].astype(F32)) * yl_ref[...].astype(F32)
                      + jax.nn.sigmoid(gs_ref[...].astype(F32)) * ys_ref[...].astype(F32)).astype(BF16)

    blk = pl.BlockSpec((tb, C), lambda c, t: (t, c))
    return pl.pallas_call(
        body, name=name, out_shape=_S((T, D), BF16), grid=(nC, T // tb),
        in_specs=[pl.BlockSpec((tb, C), lambda c, t: (t, k0 + c)),
                  pl.BlockSpec((tb, C), lambda c, t: (t, k0 + nC + c)), blk, blk],
        out_specs=blk, compiler_params=_cparams(("parallel", "parallel")),
    )(p, p, y_lru, y_sc)


def _merge_bwd(p, y_lru, y_sc, dm, *, col0, name):
    T, D = y_lru.shape
    C = _tile(math.gcd(D, col0), 1024)
    nC = D // C
    k0 = col0 // C
    tb = _tile(T, 256, HALO)

    def body(gl_ref, gs_ref, yl_ref, ys_ref, dm_ref, dp_ref, dyl_ref, dys_ref, st_l, st_s, sems):
        c = pl.program_id(0)
        t = pl.program_id(1)
        dmv = dm_ref[...].astype(F32)
        sl = jax.nn.sigmoid(gl_ref[...].astype(F32))
        ss = jax.nn.sigmoid(gs_ref[...].astype(F32))
        dyl_ref[...] = (dmv * sl).astype(BF16)
        dys_ref[...] = (dmv * ss).astype(BF16)
        st_l[...] = (dmv * yl_ref[...].astype(F32) * sl * (1.0 - sl)).astype(BF16)
        st_s[...] = (dmv * ys_ref[...].astype(F32) * ss * (1.0 - ss)).astype(BF16)
        cl = _store_block(st_l, dp_ref, sems.at[0], t * tb, col0 + c * C)
        cs = _store_block(st_s, dp_ref, sems.at[1], t * tb, col0 + D + c * C)
        cl.start()
        cs.start()
        cl.wait()
        cs.wait()

    blk = pl.BlockSpec((tb, C), lambda c, t: (t, c))
    return pl.pallas_call(
        body, name=name, out_shape=(_S(p.shape, BF16), _S((T, D), BF16), _S((T, D), BF16)),
        grid=(nC, T // tb),
        in_specs=[pl.BlockSpec((tb, C), lambda c, t: (t, k0 + c)),
                  pl.BlockSpec((tb, C), lambda c, t: (t, k0 + nC + c)), blk, blk, blk],
        out_specs=(ANY, blk, blk),
        scratch_shapes=[pltpu.VMEM((tb, C), BF16), pltpu.VMEM((tb, C), BF16), pltpu.SemaphoreType.DMA((2,))],
        compiler_params=_cparams(("arbitrary", "arbitrary")),
    )(p, p, y_lru, y_sc, dm)


def _ffn_act_fwd(uu, conv_w, *, name):
    T = uu.shape[0]
    F = uu.shape[1] // 2
    C = _tile(F, C_EW)
    nC = F // C
    tb = _tile(T, 256, HALO)
    nT, hb = T // tb, tb // HALO
    kw = conv_w.shape[0]

    def body(g_ref, gh_ref, v_ref, vh_ref, wg_ref, wv_ref, o_ref):
        t = pl.program_id(1)
        cg = _conv(g_ref[...].astype(F32), _prev8(gh_ref, t), wg_ref)
        cv = _conv(v_ref[...].astype(F32), _prev8(vh_ref, t), wv_ref)
        o_ref[...] = (cg * jax.nn.sigmoid(cg) * cv).astype(BF16)

    seg = lambda k: pl.BlockSpec((tb, C), lambda c, t: (t, k * nC + c))
    hseg = lambda k: pl.BlockSpec((HALO, C), _halo_prev_map(hb, lambda c: k * nC + c))
    wseg = lambda k: pl.BlockSpec((kw, C), lambda c, t: (0, k * nC + c))
    return pl.pallas_call(
        body, name=name, out_shape=_S((T, F), BF16), grid=(nC, nT),
        in_specs=[seg(0), hseg(0), seg(1), hseg(1), wseg(0), wseg(1)],
        out_specs=pl.BlockSpec((tb, C), lambda c, t: (t, c)),
        compiler_params=_cparams(("parallel", "parallel")),
    )(uu, uu, uu, uu, conv_w, conv_w)


def _ffn_act_bwd(uu, dact, conv_w, *, name):
    T = uu.shape[0]
    F = uu.shape[1] // 2
    C = _tile(F, C_EW)
    nC = F // C
    tb = _tile(T, 256, HALO)
    nT, hb = T // tb, tb // HALO
    kw = conv_w.shape[0]

    def body(g_ref, gh_ref, v_ref, vh_ref, da_ref, wg_ref, wv_ref, du_ref, dwg_ref, dwv_ref,
             gn_s, vn_s, st_g, st_v, sems):
        c = pl.program_id(0)
        tr = pl.program_id(1)
        t = nT - 1 - tr
        first = tr == 0

        @pl.when(first)
        def _():
            gn_s[...] = jnp.zeros_like(gn_s)
            vn_s[...] = jnp.zeros_like(vn_s)

        ug = g_ref[...].astype(F32)
        uv = v_ref[...].astype(F32)
        ugp = _prev8(gh_ref, t)
        uvp = _prev8(vh_ref, t)
        cg = _conv(ug, ugp, wg_ref)
        cv = _conv(uv, uvp, wv_ref)
        da = da_ref[...].astype(F32)
        sg = jax.nn.sigmoid(cg)
        d_cg = da * cv * (sg * (1.0 + cg * (1.0 - sg)))
        d_cv = da * (cg * sg)
        _conv_dw(dwg_ref, d_cg, ug, ugp, first)
        _conv_dw(dwv_ref, d_cv, uv, uvp, first)
        st_g[...] = _conv_t(d_cg, gn_s[...], wg_ref).astype(BF16)
        st_v[...] = _conv_t(d_cv, vn_s[...], wv_ref).astype(BF16)
        gn_s[...] = d_cg[:SUB, :]
        vn_s[...] = d_cv[:SUB, :]
        cpg = _store_block(st_g, du_ref, sems.at[0], t * tb, c * C)
        cpv = _store_block(st_v, du_ref, sems.at[1], t * tb, F + c * C)
        cpg.start()
        cpv.start()
        cpg.wait()
        cpv.wait()

    seg = lambda k: pl.BlockSpec((tb, C), lambda c, tr: (nT - 1 - tr, k * nC + c))
    hseg = lambda k: pl.BlockSpec((HALO, C), lambda c, tr: (jnp.maximum((nT - 1 - tr) * hb - 1, 0), k * nC + c))
    wseg = lambda k: pl.BlockSpec((kw, C), lambda c, tr: (0, k * nC + c))
    dwb = pl.BlockSpec((kw, C), lambda c, tr: (0, c))
    return pl.pallas_call(
        body, name=name, out_shape=(_S(uu.shape, BF16), _S((kw, F), F32), _S((kw, F), F32)), grid=(nC, nT),
        in_specs=[seg(0), hseg(0), seg(1), hseg(1), pl.BlockSpec((tb, C), lambda c, tr: (nT - 1 - tr, c)),
                  wseg(0), wseg(1)],
        out_specs=(ANY, dwb, dwb),
        scratch_shapes=[pltpu.VMEM((SUB, C), F32), pltpu.VMEM((SUB, C), F32),
                        pltpu.VMEM((tb, C), BF16), pltpu.VMEM((tb, C), BF16), pltpu.SemaphoreType.DMA((2,))],
        compiler_params=_cparams(("arbitrary", "arbitrary")),
    )(uu, uu, uu, uu, dact, conv_w, conv_w)


def _place():
    x, y, c = lax.axis_index("x"), lax.axis_index("y"), lax.axis_index("c")
    return x, y, c


def _chips(x, y):
    return [(1 - x, y), (x, 1 - y), (1 - x, 1 - y)]


def _all_gather(shards, over_ici, name):
    n = len(shards)

    def body(*refs):
        ins, outs = refs[:n], refs[n:2 * n]
        send_sems, recv_sems, local_sems = refs[2 * n:]
        x, y, c = _place()
        me, sibling = (x, y, c), (x, y, 1 - c)
        chips = _chips(x, y)
        full = [a for a in range(n) if over_ici[a]]

        def idx(px, py, pc):
            return 4 * px + 2 * py + pc

        def copy(a, k, block, to, src=None):
            dst = outs[a].at[idx(*block)]
            return pltpu.make_async_remote_copy(
                src_ref=dst if src is None else src, dst_ref=dst,
                send_sem=send_sems.at[a, k], recv_sem=recv_sems.at[a, k],
                device_id=to, device_id_type=MESH)

        mine = [pltpu.make_async_copy(ins[a], outs[a].at[idx(*me)], local_sems.at[a]) for a in range(n)]
        for cp in mine:
            cp.start()
        first = []
        for a in full:
            first += [copy(a, 1 + j, me, (*chip, c), src=ins[a]) for j, chip in enumerate(chips)]
        for a in range(n):
            first.append(copy(a, 0, me, sibling, src=ins[a]))
        for cp in first:
            cp.start()
        passed = []
        for a in full:
            for j, chip in enumerate(chips):
                copy(a, 1 + j, (*chip, c), me).wait_recv()
                cp = copy(a, 4 + j, (*chip, c), sibling)
                cp.start()
                passed.append(cp)
        for a in range(n):
            copy(a, 0, sibling, me).wait_recv()
        for a in full:
            for j, chip in enumerate(chips):
                copy(a, 4 + j, (*chip, 1 - c), me).wait_recv()
        for cp in first + passed:
            cp.wait_send()
        for cp in mine:
            cp.wait()

    return pl.pallas_call(
        body, name=name,
        out_shape=tuple(_S((N_DEV,) + s.shape, s.dtype) for s in shards),
        in_specs=[ANY] * n, out_specs=tuple([ANY] * n),
        scratch_shapes=[pltpu.SemaphoreType.DMA((n, 7)), pltpu.SemaphoreType.DMA((n, 7)),
                        pltpu.SemaphoreType.DMA((n,))],
    )(*shards)


def _rows_of(ref, blk, rows):
    v = ref.at[blk]
    return v if rows is None else v.at[pl.ds(rows[0], rows[1])]


def _gather_task(buf, rows=None):
    def copies(refs, ss, rs):
        x, y, c = _place()
        me = 4 * x + 2 * y + c
        return [pltpu.make_async_remote_copy(
            src_ref=_rows_of(refs[0], me, rows), dst_ref=_rows_of(refs[0], me, rows),
            send_sem=ss.at[j], recv_sem=rs.at[j], device_id=(px, py, c), device_id_type=MESH)
            for j, (px, py) in enumerate(_chips(x, y))]

    def start(refs, ss, rs, ls):
        for cp in copies(refs, ss, rs):
            cp.start()

    def wait(refs, ss, rs, ls):
        x, y, c = _place()
        for j, (px, py) in enumerate(_chips(x, y)):
            pltpu.make_async_remote_copy(
                src_ref=_rows_of(refs[0], 4 * px + 2 * py + c, rows), dst_ref=_rows_of(refs[0], 4 * px + 2 * py + c, rows),
                send_sem=ss.at[j], recv_sem=rs.at[j], device_id=(px, py, c), device_id_type=MESH).wait_recv()
        for cp in copies(refs, ss, rs):
            cp.wait_send()

    return _Task([buf], [0], start, wait)


def _d2d_forward(bufs, rows, name):
    n = len(bufs)

    def body(*refs):
        outs = refs[n:2 * n]
        send_sems, recv_sems = refs[2 * n:]
        x, y, c = _place()
        cps = []
        for a in range(n):
            for j, (px, py) in enumerate(_chips(x, y)):
                blk = _rows_of(outs[a], 4 * px + 2 * py + c, rows[a])
                cps.append(pltpu.make_async_remote_copy(
                    src_ref=blk, dst_ref=blk, send_sem=send_sems.at[a, j], recv_sem=recv_sems.at[a, j],
                    device_id=(x, y, 1 - c), device_id_type=MESH))
        for cp in cps:
            cp.start()
        for a in range(n):
            for j, (px, py) in enumerate(_chips(x, y)):
                blk = _rows_of(outs[a], 4 * px + 2 * py + 1 - c, rows[a])
                pltpu.make_async_remote_copy(
                    src_ref=blk, dst_ref=blk, send_sem=send_sems.at[a, j], recv_sem=recv_sems.at[a, j],
                    device_id=(x, y, 1 - c), device_id_type=MESH).wait_recv()
        for cp in cps:
            cp.wait_send()

    return pl.pallas_call(
        body, name=name, out_shape=tuple(_S(b.shape, b.dtype) for b in bufs),
        in_specs=[ANY] * n, out_specs=tuple([ANY] * n),
        scratch_shapes=[pltpu.SemaphoreType.DMA((n, 3)), pltpu.SemaphoreType.DMA((n, 3))],
        input_output_aliases={a: a for a in range(n)},
    )(*bufs)


def _exchange_task(parts, landing):
    def copies(refs, ss, rs):
        x, y, c = _place()
        myq = 2 * x + y
        return [pltpu.make_async_remote_copy(
            src_ref=refs[0].at[2 * px + py], dst_ref=refs[1].at[myq],
            send_sem=ss.at[k], recv_sem=rs.at[k], device_id=(px, py, c), device_id_type=MESH)
            for k, (px, py) in enumerate(_chips(x, y))]

    def local(refs, ls):
        x, y, _ = _place()
        return pltpu.make_async_copy(refs[0].at[2 * x + y], refs[1].at[2 * x + y], ls.at[0])

    def start(refs, ss, rs, ls):
        local(refs, ls).start()
        for cp in copies(refs, ss, rs):
            cp.start()

    def wait(refs, ss, rs, ls):
        x, y, c = _place()
        for k, (px, py) in enumerate(_chips(x, y)):
            pltpu.make_async_remote_copy(
                src_ref=refs[0].at[2 * x + y], dst_ref=refs[1].at[2 * px + py],
                send_sem=ss.at[k], recv_sem=rs.at[k], device_id=(px, py, c), device_id_type=MESH).wait_recv()
        for cp in copies(refs, ss, rs):
            cp.wait_send()
        local(refs, ls).wait()

    return _Task([parts, landing], [1], start, wait)


def _swap_halves(grads, name):
    n = len(grads)
    g4 = [g.reshape((N_CHIP, 2) + g.shape[1:]) for g in grads]

    def body(*refs):
        ins, outs = refs[:n], refs[n:2 * n]
        send_sems, recv_sems = refs[2 * n:]
        x, y, c = _place()
        cps = [pltpu.make_async_remote_copy(
            src_ref=ins[a].at[:, 1 - c], dst_ref=outs[a],
            send_sem=send_sems.at[a], recv_sem=recv_sems.at[a],
            device_id=(x, y, 1 - c), device_id_type=MESH) for a in range(n)]
        for cp in cps:
            cp.start()
        for cp in cps:
            cp.wait()

    return pl.pallas_call(
        body, name=name,
        out_shape=tuple(_S((N_CHIP,) + g.shape[1:], g.dtype) for g in grads),
        in_specs=[ANY] * n, out_specs=tuple([ANY] * n),
        scratch_shapes=[pltpu.SemaphoreType.DMA((n,)), pltpu.SemaphoreType.DMA((n,))],
    )(*g4)


def _add_halves(g, landed, c_idx, name):
    _, r, cc = g.shape
    g4 = g.reshape(N_CHIP, 2, r, cc)
    tr = _tile(r, 512, HALO)

    def body(c_ref, g_ref, l_ref, o_ref, land_ref):
        del c_ref, land_ref
        o_ref[...] = (g_ref[...].astype(F32) + l_ref[...].astype(F32)).astype(BF16)

    return pl.pallas_call(
        body, name=name, out_shape=(_S((N_CHIP, r, cc), BF16), _S((N_CHIP, r, cc), BF16)),
        grid_spec=pltpu.PrefetchScalarGridSpec(
            num_scalar_prefetch=1, grid=(N_CHIP, r // tr),
            in_specs=[pl.BlockSpec((None, None, tr, cc), lambda q, i, c_ref: (q, c_ref[0], i, 0)),
                      pl.BlockSpec((None, tr, cc), lambda q, i, c_ref: (q, i, 0))],
            out_specs=(pl.BlockSpec((None, tr, cc), lambda q, i, c_ref: (q, i, 0)), ANY)),
        compiler_params=_cparams(("parallel", "parallel")),
    )(c_idx, g4, landed)


def _all_reduce_small(pack, name):
    R = pack.shape[0]

    def body(p_ref, o_ref, buf, send_sems, recv_sems):
        x, y, c = _place()
        me = 4 * x + 2 * y + c
        buf[me] = p_ref[...]
        cps = []
        for k in range(N_DEV - 1):
            m = k + 1
            peer = (x ^ (m >> 2), y ^ ((m >> 1) & 1), c ^ (m & 1))
            cps.append(pltpu.make_async_remote_copy(
                src_ref=p_ref, dst_ref=buf.at[me], send_sem=send_sems.at[k], recv_sem=recv_sems.at[k],
                device_id=peer, device_id_type=MESH))
        for cp in cps:
            cp.start()
        for k in range(N_DEV - 1):
            m = k + 1
            peer_idx = 4 * (x ^ (m >> 2)) + 2 * (y ^ ((m >> 1) & 1)) + (c ^ (m & 1))
            pltpu.make_async_remote_copy(
                src_ref=p_ref, dst_ref=buf.at[peer_idx], send_sem=send_sems.at[k], recv_sem=recv_sems.at[k],
                device_id=(x, y, c), device_id_type=MESH).wait_recv()
        for cp in cps:
            cp.wait_send()
        acc = buf[0]
        for k in range(1, N_DEV):
            acc = acc + buf[k]
        o_ref[...] = acc

    return pl.pallas_call(
        body, name=name, out_shape=_S((R, LANES), F32),
        in_specs=[VMEM_SPEC], out_specs=VMEM_SPEC,
        scratch_shapes=[pltpu.VMEM((N_DEV, R, LANES), F32), pltpu.SemaphoreType.DMA((N_DEV - 1,)),
                        pltpu.SemaphoreType.DMA((N_DEV - 1,))],
        compiler_params=_cparams(),
    )(pack)


def _adamw_math(w, g, m, v):
    m = ADAM_B1 * m + (1.0 - ADAM_B1) * g
    v = ADAM_B2 * v + (1.0 - ADAM_B2) * (g * g)
    m_hat = m / (1.0 - ADAM_B1 ** ADAM_STEP)
    v_hat = v / (1.0 - ADAM_B2 ** ADAM_STEP)
    delta = -ADAM_LR * (m_hat / (jnp.sqrt(v_hat) + ADAM_EPS) + ADAM_WD * w)
    return delta, m, v


def _adamw_big(parts, w, m, v, name):
    r, cc = w.shape
    tr = _tile(r, 128, HALO)

    def body(p_ref, w_ref, m_ref, v_ref, g_ref, d_ref, nm_ref, nv_ref):
        g = p_ref[0].astype(F32)
        for q in range(1, N_CHIP):
            g = g + p_ref[q].astype(F32)
        g_ref[...] = g
        d_ref[...], nm_ref[...], nv_ref[...] = _adamw_math(w_ref[...], g, m_ref[...], v_ref[...])

    blk = pl.BlockSpec((tr, cc), lambda i: (i, 0))
    return pl.pallas_call(
        body, name=name, out_shape=tuple(_S((r, cc), F32) for _ in range(4)), grid=(r // tr,),
        in_specs=[pl.BlockSpec((N_CHIP, tr, cc), lambda i: (0, i, 0)), blk, blk, blk],
        out_specs=(blk, blk, blk, blk), compiler_params=_cparams(("parallel",)),
    )(parts, w, m, v)


def _adamw_small(ws, gs, ms, vs, name):
    n = len(ws)

    def body(*refs):
        w_r, g_r, m_r, v_r = refs[:n], refs[n:2 * n], refs[2 * n:3 * n], refs[3 * n:4 * n]
        d_r, nm_r, nv_r = refs[4 * n:5 * n], refs[5 * n:6 * n], refs[6 * n:7 * n]
        for k in range(n):
            d_r[k][...], nm_r[k][...], nv_r[k][...] = _adamw_math(w_r[k][...], g_r[k][...], m_r[k][...], v_r[k][...])

    shapes = tuple(_S(w.shape, F32) for w in ws)
    outs = pl.pallas_call(
        body, name=name, out_shape=shapes * 3,
        in_specs=[VMEM_SPEC] * (4 * n), out_specs=tuple([VMEM_SPEC] * (3 * n)),
        compiler_params=_cparams(),
    )(*ws, *gs, *ms, *vs)
    return outs[:n], outs[n:2 * n], outs[2 * n:]


def _block_diag(w, heads_per_block):
    H, hd, _ = w.shape
    nb = H // heads_per_block
    eye = jnp.eye(heads_per_block, dtype=w.dtype)
    w4 = w.reshape(nb, heads_per_block, hd, hd)
    return jnp.einsum("nhab,hg->nhagb", w4, eye).reshape(nb, heads_per_block * hd, heads_per_block * hd)


def _diag_blocks(bd, heads_per_block, hd):
    nb = bd.shape[0]
    b5 = bd.reshape(nb, heads_per_block, hd, heads_per_block, hd)
    return jnp.stack([b5[:, h, :, h, :] for h in range(heads_per_block)], axis=1).reshape(nb * heads_per_block, hd, hd)


def _as_rows(a):
    if a.ndim == 1:
        return a.reshape(-1, LANES) if a.shape[0] % LANES == 0 else a.reshape(1, -1)
    if a.ndim == 3:
        return a.reshape(-1, LANES) if (a.size % LANES == 0) else a.reshape(a.shape[0] * a.shape[1], a.shape[2])
    return a


def kernel(x, g_mix, w_in, lru_conv_w, lru_conv_b, lru_wa, lru_ba, lru_wx, lru_bx, lru_lambda, lru_w_out, sc_conv_w, sc_w_out, w_o, g_ffn, ffn_w_up, ffn_conv_w, ffn_w_down, g_final, loss_target, m_g_mix, m_w_in, m_lru_conv_w, m_lru_conv_b, m_lru_wa, m_lru_ba, m_lru_wx, m_lru_bx, m_lru_lambda, m_lru_w_out, m_sc_conv_w, m_sc_w_out, m_w_o, m_g_ffn, m_ffn_w_up, m_ffn_conv_w, m_ffn_w_down, m_g_final, v_g_mix, v_w_in, v_lru_conv_w, v_lru_conv_b, v_lru_wa, v_lru_ba, v_lru_wx, v_lru_bx, v_lru_lambda, v_lru_w_out, v_sc_conv_w, v_sc_w_out, v_w_o, v_g_ffn, v_ffn_w_up, v_ffn_conv_w, v_ffn_w_down, v_g_final):
    T, D = x.shape[1], x.shape[2]
    d_lru = lru_lambda.shape[0]
    d_sc = sc_conv_w.shape[1] * N_DEV
    F = ffn_w_down.shape[0] * N_DEV
    H = lru_wa.shape[0]
    assert d_lru == d_sc and H * HEAD_DIM == d_lru
    xs = x.reshape(T, D)
    tgt = loss_target.reshape(T, D)
    my_x, my_y, my_c = _place()
    me = 4 * my_x + 2 * my_y + my_c

    big = [w_in, lru_w_out, sc_w_out, w_o, ffn_w_up, ffn_w_down]
    big_names = ["w_in", "lru_w_out", "sc_w_out", "w_o", "ffn_w_up", "ffn_w_down"]
    big_bf = [_cast_bf16(w, "cast_" + nm) for w, nm in zip(big, big_names)]
    pad_rows = lambda a: jnp.pad(a, ((0, SUB - a.shape[0]), (0, 0)))
    gathered = _all_gather(big_bf + [pad_rows(lru_conv_w), pad_rows(sc_conv_w), pad_rows(ffn_conv_w)],
                           [True, False, False, False, False, False, True, True, True], "all_gather_first")
    W_in, W_lo, W_so, W_o8, W_up, W_dn8 = gathered[:6]
    full_cols = lambda g, kw: g[:, :kw, :].transpose(1, 0, 2).reshape(kw, -1)
    cw_lru = full_cols(gathered[6], lru_conv_w.shape[0])
    cw_sc = full_cols(gathered[7], sc_conv_w.shape[0])
    cw_ffn = full_cols(gathered[8], ffn_conv_w.shape[0])

    C = _tile(d_lru, C_LRU)
    hpb = C // HEAD_DIM
    wa_bd = _block_diag(lru_wa, hpb).astype(BF16)
    wx_bd = _block_diag(lru_wx, hpb).astype(BF16)
    cb, ba, bx, lam = (a.reshape(1, d_lru) for a in (lru_conv_b, lru_ba, lru_bx, lru_lambda))

    h1 = _rms_fwd(xs, g_mix, "rms_mix")
    kq = W_up.shape[1] // 4
    p, ((W_lo,), (W_so,), (W_o8,), (W_up,)) = _mm_nn(
        h1, W_in, out_dtype=BF16, name="mm_in",
        tasks=[_gather_task(W_lo), _gather_task(W_so), _gather_task(W_o8), _gather_task(W_up, (0, 2 * kq))])
    W_lo, W_so, W_o8, W_up = _d2d_forward([W_lo, W_so, W_o8, W_up], [None, None, None, (0, 2 * kq)], "ag_forward_1")
    hs, yl_pre = _lru_fwd(p, cw_lru, cb, wa_bd, ba, wx_bd, bx, lam, name="lru_fwd")
    ys_pre = _sc_fwd(p, cw_sc, d=d_sc, name="sc_fwd")
    y_lru, ((W_up,),) = _mm_nn(yl_pre, W_lo, out_dtype=BF16, name="mm_lru_out", tm=2048,
                               tasks=[_gather_task(W_up, (2 * kq, kq))])
    y_sc, ((W_up,),) = _mm_nn(ys_pre, W_so, out_dtype=BF16, name="mm_sc_out", tm=2048,
                              tasks=[_gather_task(W_up, (3 * kq, kq))])
    gate0 = 2 * d_lru + 3 * d_sc
    merged = _merge_fwd(p, y_lru, y_sc, col0=gate0, name="merge_fwd")
    W_o = W_o8.reshape(1, D, D)
    x1 = _mm_nn(merged, W_o, out_dtype=F32, residual=xs, name="mm_o")
    (W_up,) = _d2d_forward([W_up], [(2 * kq, 2 * kq)], "ag_forward_2")
    h2 = _rms_fwd(x1, g_ffn, "rms_ffn")
    uu, ((W_dn8,),) = _mm_nn(h2, W_up, out_dtype=BF16, name="mm_up", tasks=[_gather_task(W_dn8)])
    act = _ffn_act_fwd(uu, cw_ffn, name="ffn_act_fwd")
    (W_dn8,) = _d2d_forward([W_dn8], [None], "ag_forward_3")
    W_dn = W_dn8.reshape(1, F, D)
    x2 = _mm_nn(act, W_dn, out_dtype=F32, residual=x1, name="mm_down", tn=512, tk=F)
    dx2, dx2b, loss_part, dg_final = _loss_head(x2, g_final, tgt, "loss_head")

    c_idx = jnp.reshape(my_c, (1,)).astype(jnp.int32)

    def _reduce_cores(g, nm):
        (landed,) = _swap_halves([g], "rs_swap_" + nm)
        return _add_halves(g, landed, c_idx, "rs_add_" + nm)

    dact = _mm_nt(dx2b, W_dn, out_dtype=BF16, name="mm_down_dx", tko=1408)
    gW_dn = _mm_tn(act, dx2b, 1, out_dtype=BF16, name="mm_down_dw").reshape(N_DEV, F // N_DEV, D)
    parts_dn = _reduce_cores(gW_dn, "ffn_w_down")
    duu, dcw_ffn_g, dcw_ffn_v = _ffn_act_bwd(uu, dact, cw_ffn, name="ffn_act_bwd")
    dh2, ((mine_dn,),) = _mm_nt(duu, W_up, out_dtype=BF16, name="mm_up_dx", tasks=[_exchange_task(*parts_dn)])
    gW_up = _mm_tn(h2, duu, N_DEV, out_dtype=BF16, name="mm_up_dw")
    parts_up = _reduce_cores(gW_up, "ffn_w_up")
    dx1, dx1b, dg_ffn = _rms_bwd(x1, g_ffn, dh2, dx2, "rms_ffn_bwd")
    dmerged = _mm_nt(dx1b, W_o, out_dtype=BF16, name="mm_o_dx")
    gW_o = _mm_tn(merged, dx1b, 1, out_dtype=BF16, name="mm_o_dw").reshape(N_DEV, D // N_DEV, D)
    parts_o = _reduce_cores(gW_o, "w_o")
    dp, dy_lru, dy_sc = _merge_bwd(p, y_lru, y_sc, dmerged, col0=gate0, name="merge_bwd")
    dyl_pre = _mm_nt(dy_lru, W_lo, out_dtype=BF16, name="mm_lru_out_dx")
    gW_lo, ((mine_o,),) = _mm_tn(yl_pre, dy_lru, N_DEV, out_dtype=BF16, name="mm_lru_out_dw",
                                 tasks=[_exchange_task(*parts_o)])
    parts_lo = _reduce_cores(gW_lo, "lru_w_out")
    dys_pre = _mm_nt(dy_sc, W_so, out_dtype=BF16, name="mm_sc_out_dx")
    gW_so, ((mine_lo,),) = _mm_tn(ys_pre, dy_sc, N_DEV, out_dtype=BF16, name="mm_sc_out_dw",
                                  tasks=[_exchange_task(*parts_lo)])
    parts_so = _reduce_cores(gW_so, "sc_w_out")
    dp, dcw_sc = _sc_bwd(p, dys_pre, dp, cw_sc, d=d_sc, name="sc_bwd")
    dp, dcw_lru, dcb, dwa_bd, dba, dwx_bd, dbx, dlam = _lru_bwd(
        p, hs, dyl_pre, dp, cw_lru, cb, wa_bd, ba, wx_bd, bx, lam, name="lru_bwd")
    gW_in, ((mine_up,), (mine_so,)) = _mm_tn(h1, dp, N_DEV, out_dtype=BF16, name="mm_in_dw",
                                             tasks=[_exchange_task(*parts_up), _exchange_task(*parts_so)])
    parts_in = _reduce_cores(gW_in, "w_in")
    dh1, ((mine_in,),) = _mm_nt(dp, W_in, out_dtype=BF16, name="mm_in_dx", tasks=[_exchange_task(*parts_in)])
    grad_x, _, dg_mix = _rms_bwd(xs, g_mix, dh1, dx1, "rms_mix_bwd")

    mine = [mine_in, mine_lo, mine_so, mine_o, mine_up, mine_dn]
    big_m = [m_w_in, m_lru_w_out, m_sc_w_out, m_w_o, m_ffn_w_up, m_ffn_w_down]
    big_v = [v_w_in, v_lru_w_out, v_sc_w_out, v_w_o, v_ffn_w_up, v_ffn_w_down]
    big_out = {nm: _adamw_big(pt, w, m, v, "adamw_" + nm)
               for nm, pt, w, m, v in zip(big_names, mine, big, big_m, big_v)}

    dwa = _diag_blocks(dwa_bd, hpb, HEAD_DIM)
    dwx = _diag_blocks(dwx_bd, hpb, HEAD_DIM)
    dcw_ffn = jnp.concatenate([dcw_ffn_g, dcw_ffn_v], axis=1)
    small_full = [dg_mix, dcw_lru, dcb, dwa, dba, dwx, dbx, dlam, dcw_sc, dg_ffn, dcw_ffn, dg_final]
    flat = jnp.concatenate([a.reshape(-1) for a in small_full])
    n_flat = flat.shape[0]
    rows = -(-n_flat // (SUB * LANES)) * SUB
    pack = jnp.pad(flat, (0, rows * LANES - n_flat)).reshape(rows, LANES)
    total = _all_reduce_small(pack, "all_reduce_small").reshape(-1)
    sums, o = [], 0
    for a in small_full:
        sums.append(total[o:o + a.size].reshape(a.shape))
        o += a.size
    (sg_mix, scw_lru, scb, swa, sba, swx, sbx, slam, scw_sc, sg_ffn, scw_ffn, sg_final) = sums

    def my_cols(a):
        n = a.shape[1] // N_DEV
        return lax.dynamic_slice_in_dim(a, me * n, n, axis=1)

    small_names = ["g_mix", "lru_conv_w", "lru_conv_b", "lru_wa", "lru_ba", "lru_wx", "lru_bx", "lru_lambda",
                   "sc_conv_w", "g_ffn", "ffn_conv_w", "g_final"]
    small_w = [g_mix, lru_conv_w, lru_conv_b, lru_wa, lru_ba, lru_wx, lru_bx, lru_lambda, sc_conv_w, g_ffn,
               ffn_conv_w, g_final]
    small_m = [m_g_mix, m_lru_conv_w, m_lru_conv_b, m_lru_wa, m_lru_ba, m_lru_wx, m_lru_bx, m_lru_lambda,
               m_sc_conv_w, m_g_ffn, m_ffn_conv_w, m_g_final]
    small_v = [v_g_mix, v_lru_conv_w, v_lru_conv_b, v_lru_wa, v_lru_ba, v_lru_wx, v_lru_bx, v_lru_lambda,
               v_sc_conv_w, v_g_ffn, v_ffn_conv_w, v_g_final]
    small_g = [sg_mix.reshape(D), my_cols(scw_lru), scb.reshape(d_lru), swa, sba.reshape(d_lru), swx,
               sbx.reshape(d_lru), slam.reshape(d_lru), my_cols(scw_sc), sg_ffn.reshape(D), my_cols(scw_ffn),
               sg_final.reshape(D)]
    sd, snm, snv = _adamw_small([_as_rows(a) for a in small_w], [_as_rows(a) for a in small_g],
                                [_as_rows(a) for a in small_m], [_as_rows(a) for a in small_v], "adamw_small")
    small_out = {nm: (g, d.reshape(w.shape), nm_.reshape(w.shape), nv_.reshape(w.shape))
                 for nm, w, g, d, nm_, nv_ in zip(small_names, small_w, small_g, sd, snm, snv)}

    loss = lax.psum(loss_part[0, 0], AXES)
    order = ["g_mix", "w_in", "lru_conv_w", "lru_conv_b", "lru_wa", "lru_ba", "lru_wx", "lru_bx", "lru_lambda",
             "lru_w_out", "sc_conv_w", "sc_w_out", "w_o", "g_ffn", "ffn_w_up", "ffn_conv_w", "ffn_w_down", "g_final"]
    res = {**big_out, **small_out}
    return (loss, grad_x.reshape(x.shape),
            *[res[nm][0] for nm in order], *[res[nm][1] for nm in order],
            *[res[nm][2] for nm in order], *[res[nm][3] for nm in order])
```

```python
import functools
import math

import jax
import jax.numpy as jnp
from jax import lax
from jax.experimental import pallas as pl
from jax.experimental.pallas import tpu as pltpu

F32, BF16 = jnp.float32, jnp.bfloat16
MESH = pl.DeviceIdType.MESH
N_DEV = 8
N_CHIP = 4
AXES = ("x", "y", "c")

EPS = 1e-6
LRU_C = 8.0
HEAD_DIM = 64
ADAM_LR, ADAM_B1, ADAM_B2, ADAM_EPS, ADAM_WD, ADAM_STEP = 0.001, 0.9, 0.999, 1e-08, 0.01, 10

VMEM_LIMIT = 48 * 1024 * 1024
LANES = 128
SUB = 8
HALO = 16
TB = 512
C_LRU = 256
C_EW = 512
TM, TN, TK = 512, 1536, 2048


def _tile(n, pref, align=LANES):
    best = None
    for d in range(align, min(n, pref) + 1, align):
        if n % d == 0:
            best = d
    return best or n


def _cparams(sem=None, vmem=VMEM_LIMIT):
    kw = dict(vmem_limit_bytes=vmem)
    if sem is not None:
        kw["dimension_semantics"] = sem
    return pltpu.CompilerParams(**kw)


def _S(shape, dtype):
    return jax.ShapeDtypeStruct(shape, dtype)


ANY = pl.BlockSpec(memory_space=pl.ANY)
VMEM_SPEC = pl.BlockSpec(memory_space=pltpu.VMEM)


class _Task:
    def __init__(self, arrays, aliased, start, wait):
        self.arrays, self.aliased, self.start, self.wait = arrays, aliased, start, wait


def _call(name, grid, compute, in_specs, args, out_shape, out_specs, scratch, tasks=()):
    n_in, n_out, n_scr = len(args), len(out_shape), len(scratch)
    x_in, x_out, aliases, where = [], [], {}, []
    for t in tasks:
        places = []
        for k, arr in enumerate(t.arrays):
            if k in t.aliased:
                aliases[n_in + len(x_in)] = n_out + len(x_out)
                places.append(("out", len(x_out)))
                x_out.append(_S(arr.shape, arr.dtype))
            else:
                places.append(("in", len(x_in)))
            x_in.append(arr)
        where.append(places)
    n_xi, n_xo = len(x_in), len(x_out)

    def body(*refs):
        ins, xi = refs[:n_in], refs[n_in:n_in + n_xi]
        o0 = n_in + n_xi
        outs, xo = refs[o0:o0 + n_out], refs[o0 + n_out:o0 + n_out + n_xo]
        s0 = o0 + n_out + n_xo
        scr, sems = refs[s0:s0 + n_scr], refs[s0 + n_scr:]
        ids = [pl.program_id(a) for a in range(len(grid))]

        def task_refs(ti):
            return [xo[i] if kind == "out" else xi[i] for kind, i in where[ti]]

        if tasks:
            first = functools.reduce(jnp.logical_and, [i == 0 for i in ids])

            @pl.when(first)
            def _():
                for ti, t in enumerate(tasks):
                    t.start(task_refs(ti), *sems[3 * ti:3 * ti + 3])

        compute(*ins, *outs, *scr)
        if tasks:
            last = functools.reduce(jnp.logical_and, [i == g - 1 for i, g in zip(ids, grid)])

            @pl.when(last)
            def _():
                for ti, t in enumerate(tasks):
                    t.wait(task_refs(ti), *sems[3 * ti:3 * ti + 3])

    sem_shapes = []
    for _ in tasks:
        sem_shapes += [pltpu.SemaphoreType.DMA((3,)), pltpu.SemaphoreType.DMA((3,)), pltpu.SemaphoreType.DMA((1,))]
    res = pl.pallas_call(
        body, name=name, grid=grid,
        in_specs=list(in_specs) + [ANY] * n_xi,
        out_specs=tuple(out_specs) + (ANY,) * n_xo,
        out_shape=tuple(out_shape) + tuple(x_out),
        scratch_shapes=list(scratch) + sem_shapes,
        input_output_aliases=aliases,
        compiler_params=_cparams(("arbitrary",) * len(grid)),
    )(*args, *x_in)
    outs, passed, o = res[:n_out], [], n_out
    for places in where:
        k = sum(1 for kind, _ in places if kind == "out")
        passed.append(list(res[o:o + k]))
        o += k
    return outs, passed


def _mm_nn(a, w3, *, out_dtype, name, residual=None, tm=TM, tn=TN, tk=TK, tasks=()):
    M, K = a.shape
    G, _, n = w3.shape
    tm, tn, tk = _tile(M, tm, SUB), _tile(n, tn), _tile(K, tk)
    nj, nk = n // tn, K // tk

    def compute(*refs):
        if residual is None:
            a_ref, w_ref, o_ref = refs[:3]
            r_ref = None
        else:
            a_ref, w_ref, r_ref, o_ref = refs[:4]

        def finish(r):
            if r_ref is not None:
                r = r + r_ref[...]
            o_ref[...] = r.astype(o_ref.dtype)

        if nk == 1:
            finish(jnp.dot(a_ref[...], w_ref[...], preferred_element_type=F32))
            return
        acc = refs[-1]
        k = pl.program_id(3)

        @pl.when(k == 0)
        def _():
            acc[...] = jnp.zeros_like(acc)

        acc[...] += jnp.dot(a_ref[...], w_ref[...], preferred_element_type=F32)

        @pl.when(k == nk - 1)
        def _():
            finish(acc[...])

    in_specs = [pl.BlockSpec((tm, tk), lambda g, j, i, k: (i, k)),
                pl.BlockSpec((None, tk, tn), lambda g, j, i, k: (g, k, j))]
    args = [a, w3]
    if residual is not None:
        in_specs.append(pl.BlockSpec((tm, tn), lambda g, j, i, k: (i, g * nj + j)))
        args.append(residual)
    outs, passed = _call(
        name, (G, nj, M // tm, nk), compute, in_specs, args, [_S((M, G * n), out_dtype)],
        [pl.BlockSpec((tm, tn), lambda g, j, i, k: (i, g * nj + j))],
        [] if nk == 1 else [pltpu.VMEM((tm, tn), F32)], tasks)
    return (outs[0], passed) if tasks else outs[0]


def _mm_nt(dy, w3, *, out_dtype, name, tm=1024, tko=1024, tn=TN, tasks=()):
    M, _ = dy.shape
    G, K, n = w3.shape
    tm, tko, tn = _tile(M, tm, SUB), _tile(K, tko), _tile(n, tn)
    nj = n // tn
    nr = G * nj

    def compute(dy_ref, w_ref, o_ref, acc):
        r = pl.program_id(2)

        @pl.when(r == 0)
        def _():
            acc[...] = jnp.zeros_like(acc)

        acc[...] += lax.dot_general(dy_ref[...], w_ref[...], (((1,), (1,)), ((), ())),
                                    preferred_element_type=F32)

        @pl.when(r == nr - 1)
        def _():
            o_ref[...] = acc[...].astype(o_ref.dtype)

    outs, passed = _call(
        name, (K // tko, M // tm, nr), compute,
        [pl.BlockSpec((tm, tn), lambda ko, i, r: (i, r)),
         pl.BlockSpec((None, tko, tn), lambda ko, i, r: (r // nj, ko, r % nj))],
        [dy, w3], [_S((M, K), out_dtype)], [pl.BlockSpec((tm, tko), lambda ko, i, r: (i, ko))],
        [pltpu.VMEM((tm, tko), F32)], tasks)
    return (outs[0], passed) if tasks else outs[0]


def _mm_tn(a, dy, G, *, out_dtype, name, tk=1024, tn=TN, tt=1024, tasks=()):
    M, K = a.shape
    n = dy.shape[1] // G
    tk, tn, tt = _tile(K, tk), _tile(n, tn), _tile(M, tt, SUB)
    nj, nt = n // tn, M // tt

    def compute(a_ref, dy_ref, o_ref, acc):
        t = pl.program_id(3)

        @pl.when(t == 0)
        def _():
            acc[...] = jnp.zeros_like(acc)

        acc[...] += lax.dot_general(a_ref[...], dy_ref[...], (((0,), (0,)), ((), ())),
                                    preferred_element_type=F32)

        @pl.when(t == nt - 1)
        def _():
            o_ref[...] = acc[...].astype(o_ref.dtype)

    outs, passed = _call(
        name, (G, nj, K // tk, nt), compute,
        [pl.BlockSpec((tt, tk), lambda g, j, k, t: (t, k)),
         pl.BlockSpec((tt, tn), lambda g, j, k, t: (t, g * nj + j))],
        [a, dy], [_S((G, K, n), out_dtype)], [pl.BlockSpec((None, tk, tn), lambda g, j, k, t: (g, k, j))],
        [pltpu.VMEM((tk, tn), F32)], tasks)
    return (outs[0], passed) if tasks else outs[0]


def _cast_into_slot(w, me_idx, name):
    R, C = w.shape
    tr = _tile(R, 512, HALO)

    def body(me_ref, w_ref, o_ref):
        del me_ref
        o_ref[...] = w_ref[...].astype(BF16)

    return pl.pallas_call(
        body, name=name, out_shape=_S((N_DEV, R, C), BF16),
        grid_spec=pltpu.PrefetchScalarGridSpec(
            num_scalar_prefetch=1, grid=(R // tr,),
            in_specs=[pl.BlockSpec((tr, C), lambda i, me_ref: (i, 0))],
            out_specs=pl.BlockSpec((None, tr, C), lambda i, me_ref: (me_ref[0], i, 0))),
        compiler_params=_cparams(("parallel",)),
    )(me_idx, w)


def _down(cur, prev8, j):
    return pltpu.roll(jnp.concatenate([prev8, cur], axis=0), j, 0)[SUB:, :]


def _up(cur, next8, j):
    n = cur.shape[0] + SUB
    return pltpu.roll(jnp.concatenate([cur, next8], axis=0), n - j, 0)[:cur.shape[0], :]


def _conv(x, prev8, w_ref):
    kw = w_ref.shape[0]
    y = x * w_ref[pl.ds(kw - 1, 1), :]
    for k in range(kw - 1):
        y = y + _down(x, prev8, kw - 1 - k) * w_ref[pl.ds(k, 1), :]
    return y


def _conv_t(dy, next8, w_ref):
    kw = w_ref.shape[0]
    dx = dy * w_ref[pl.ds(kw - 1, 1), :]
    for k in range(kw - 1):
        dx = dx + _up(dy, next8, kw - 1 - k) * w_ref[pl.ds(k, 1), :]
    return dx


def _conv_dw(dw_ref, dy, x, prev8, first):
    kw = dw_ref.shape[0]

    @pl.when(first)
    def _():
        dw_ref[...] = jnp.zeros_like(dw_ref)

    for k in range(kw):
        xs = x if k == kw - 1 else _down(x, prev8, kw - 1 - k)
        dw_ref[pl.ds(k, 1), :] += jnp.sum(dy * xs, axis=0, keepdims=True)


def _acc(ref, val, first):
    @pl.when(first)
    def _():
        ref[...] = jnp.zeros_like(ref)

    ref[...] += val


def _acc_row(ref, val, first):
    _acc(ref, jnp.sum(val, axis=0, keepdims=True), first)


def _prev8(h_ref, t):
    return jnp.where(t > 0, h_ref[...].astype(F32)[HALO - SUB:, :], 0.0)


def _next8(h_ref, is_last):
    return jnp.where(is_last, 0.0, h_ref[...].astype(F32)[:SUB, :])


_GELU_K0 = math.sqrt(2.0 / math.pi)
_GELU_K1 = 0.044715


def _gelu_and_grad(x):
    x2 = x * x
    th = jnp.tanh(_GELU_K0 * x * (1.0 + _GELU_K1 * x2))
    g = 0.5 * x * (1.0 + th)
    dg = 0.5 * (1.0 + th) + 0.5 * x * (1.0 - th * th) * (_GELU_K0 * (1.0 + 3.0 * _GELU_K1 * x2))
    return g, dg


def _neg_expm1(z):
    series = -z * (1.0 + z * (0.5 + z * (1.0 / 6.0 + z * (1.0 / 24.0))))
    return jnp.where(z > -0.03, series, 1.0 - jnp.exp(z))


def _store_block(stage_ref, dst_hbm, sem, row0, col0):
    tb, c = stage_ref.shape
    return pltpu.make_async_copy(stage_ref, dst_hbm.at[pl.ds(row0, tb), pl.ds(col0, c)], sem)


def _halo_prev_map(hb, col_fn):
    return lambda c, t: (jnp.maximum(t * hb - 1, 0), col_fn(c))


def _rms_fwd(x, g, name):
    T, D = x.shape
    tb = _tile(T, TB, SUB)

    def body(x_ref, g_ref, o_ref):
        xv = x_ref[...]
        rstd = lax.rsqrt(jnp.mean(xv * xv, axis=-1, keepdims=True) + EPS)
        o_ref[...] = (xv * rstd * g_ref[...]).astype(BF16)

    return pl.pallas_call(
        body, name=name, out_shape=_S((T, D), BF16), grid=(T // tb,),
        in_specs=[pl.BlockSpec((tb, D), lambda i: (i, 0)), pl.BlockSpec((1, D), lambda i: (0, 0))],
        out_specs=pl.BlockSpec((tb, D), lambda i: (i, 0)),
        compiler_params=_cparams(("parallel",)),
    )(x, g.reshape(1, D))


def _rms_bwd(x, g, dh, dres, name):
    T, D = x.shape
    tb = _tile(T, 256, SUB)

    def body(x_ref, g_ref, dh_ref, dr_ref, dx_ref, dxb_ref, dg_ref):
        i = pl.program_id(0)
        xv = x_ref[...]
        rstd = lax.rsqrt(jnp.mean(xv * xv, axis=-1, keepdims=True) + EPS)
        xn = xv * rstd
        dhv = dh_ref[...].astype(F32)
        _acc_row(dg_ref, dhv * xn, i == 0)
        dxn = dhv * g_ref[...]
        dx = dr_ref[...] + rstd * (dxn - xn * jnp.mean(dxn * xn, axis=-1, keepdims=True))
        dx_ref[...] = dx
        dxb_ref[...] = dx.astype(BF16)

    blk = pl.BlockSpec((tb, D), lambda i: (i, 0))
    vec = pl.BlockSpec((1, D), lambda i: (0, 0))
    return pl.pallas_call(
        body, name=name, out_shape=(_S((T, D), F32), _S((T, D), BF16), _S((1, D), F32)),
        grid=(T // tb,), in_specs=[blk, vec, blk, blk], out_specs=(blk, blk, vec),
        compiler_params=_cparams(("arbitrary",)),
    )(x, g.reshape(1, D), dh, dres)


def _loss_head(x2, g, target, name):
    T, D = x2.shape
    tb = _tile(T, 256, SUB)

    def body(x_ref, g_ref, t_ref, dx_ref, dxb_ref, loss_ref, dg_ref):
        i = pl.program_id(0)
        xv = x_ref[...]
        rstd = lax.rsqrt(jnp.mean(xv * xv, axis=-1, keepdims=True) + EPS)
        xn = xv * rstd
        err = xn * g_ref[...] - t_ref[...]
        part = 0.5 * jnp.sum(jnp.mean(err * err, axis=-1, keepdims=True), axis=0, keepdims=True)
        part = jnp.broadcast_to(part, (1, LANES))
        _acc(loss_ref, part, i == 0)
        dy = err * (1.0 / D)
        _acc_row(dg_ref, dy * xn, i == 0)
        dxn = dy * g_ref[...]
        dx = rstd * (dxn - xn * jnp.mean(dxn * xn, axis=-1, keepdims=True))
        dx_ref[...] = dx
        dxb_ref[...] = dx.astype(BF16)

    blk = pl.BlockSpec((tb, D), lambda i: (i, 0))
    vec = pl.BlockSpec((1, D), lambda i: (0, 0))
    return pl.pallas_call(
        body, name=name,
        out_shape=(_S((T, D), F32), _S((T, D), BF16), _S((1, LANES), F32), _S((1, D), F32)),
        grid=(T // tb,), in_specs=[blk, vec, blk],
        out_specs=(blk, blk, pl.BlockSpec((1, LANES), lambda i: (0, 0)), vec),
        compiler_params=_cparams(("arbitrary",)),
    )(x2, g.reshape(1, D), target)


def _lru_gates(xc, wa_ref, ba_ref, wx_ref, bx_ref, lam_ref):
    xcb = xc.astype(BF16)
    r = jax.nn.sigmoid(jnp.dot(xcb, wa_ref[...], preferred_element_type=F32) + ba_ref[...])
    i = jax.nn.sigmoid(jnp.dot(xcb, wx_ref[...], preferred_element_type=F32) + bx_ref[...])
    sp = jax.nn.softplus(-lam_ref[...])
    log_a = (-LRU_C * sp) * r
    a = jnp.exp(log_a)
    s = jnp.sqrt(_neg_expm1(2.0 * log_a))
    return xcb, r, i, a, s


def _lru_fwd(p, conv_w, conv_b, wa_bd, ba, wx_bd, bx, lam, *, name):
    T = p.shape[0]
    d = lam.shape[-1]
    C = _tile(d, C_LRU)
    nC = d // C
    tb = _tile(T, TB, HALO)
    nT, hb, nt = T // tb, tb // HALO, tb // SUB

    def body(x_ref, xh_ref, g_ref, cw_ref, cb_ref, wa_ref, ba_ref, wx_ref, bx_ref, lam_ref,
             hs_ref, y_ref, a_s, u_s, h_s):
        t = pl.program_id(1)

        @pl.when(t == 0)
        def _():
            h_s[...] = jnp.zeros_like(h_s)

        x = x_ref[...].astype(F32)
        xc = _conv(x, _prev8(xh_ref, t), cw_ref) + cb_ref[...]
        _, r, i, a, s = _lru_gates(xc, wa_ref, ba_ref, wx_ref, bx_ref, lam_ref)
        a_s[...] = a
        u_s[...] = s * (i * xc)
        row = lax.broadcasted_iota(jnp.int32, (SUB, C), 0)

        def step(k, h):
            o = pl.multiple_of(k * SUB, SUB)
            A = a_s[pl.ds(o, SUB), :]
            B = u_s[pl.ds(o, SUB), :]
            for sh in (1, 2, 4):
                m = row >= sh
                Ap = pltpu.roll(A, sh, 0)
                Bp = pltpu.roll(B, sh, 0)
                B = jnp.where(m, A * Bp + B, B)
                A = jnp.where(m, A * Ap, A)
            hs = A * h + B
            hs_ref[pl.ds(o, SUB), :] = hs
            return jnp.broadcast_to(hs[SUB - 1:SUB, :], (SUB, C))

        h_s[...] = lax.fori_loop(0, nt, step, h_s[...])
        gel, _ = _gelu_and_grad(g_ref[...].astype(F32))
        y_ref[...] = (gel * hs_ref[...]).astype(BF16)

    vec = pl.BlockSpec((1, C), lambda c, t: (0, c))
    sq = pl.BlockSpec((None, C, C), lambda c, t: (c, 0, 0))
    return pl.pallas_call(
        body, name=name, out_shape=(_S((T, d), F32), _S((T, d), BF16)),
        grid=(nC, nT),
        in_specs=[pl.BlockSpec((tb, C), lambda c, t: (t, c)),
                  pl.BlockSpec((HALO, C), _halo_prev_map(hb, lambda c: c)),
                  pl.BlockSpec((tb, C), lambda c, t: (t, nC + c)),
                  pl.BlockSpec((conv_w.shape[0], C), lambda c, t: (0, c)),
                  vec, sq, vec, sq, vec, vec],
        out_specs=(pl.BlockSpec((tb, C), lambda c, t: (t, c)), pl.BlockSpec((tb, C), lambda c, t: (t, c))),
        scratch_shapes=[pltpu.VMEM((tb, C), F32), pltpu.VMEM((tb, C), F32), pltpu.VMEM((SUB, C), F32)],
        compiler_params=_cparams(("parallel", "arbitrary")),
    )(p, p, p, conv_w, conv_b, wa_bd, ba, wx_bd, bx, lam)


def _lru_bwd(p, hs, dyl, dp, conv_w, conv_b, wa_bd, ba, wx_bd, bx, lam, *, name):
    T = p.shape[0]
    d = lam.shape[-1]
    C = _tile(d, C_LRU)
    nC = d // C
    tb = _tile(T, TB, HALO)
    nT, hb, nt = T // tb, tb // HALO, tb // SUB
    kw = conv_w.shape[0]

    def body(x_ref, xh_ref, g_ref, hs_ref, hh_ref, dy_ref, cw_ref, cb_ref, wa_ref, ba_ref, wx_ref, bx_ref,
             lam_ref, dp_in, dp_ref, dcw_ref, dcb_ref, dwa_ref, dba_ref, dwx_ref, dbx_ref, dlam_ref,
             b_s, g_s, dh_s, an_s, dhn_s, dxn_s, st_x, st_g, sems):
        del dp_in
        c = pl.program_id(0)
        tr = pl.program_id(1)
        t = nT - 1 - tr
        first = tr == 0

        @pl.when(first)
        def _():
            an_s[...] = jnp.zeros_like(an_s)
            dhn_s[...] = jnp.zeros_like(dhn_s)
            dxn_s[...] = jnp.zeros_like(dxn_s)

        x = x_ref[...].astype(F32)
        xprev = _prev8(xh_ref, t)
        xc = _conv(x, xprev, cw_ref) + cb_ref[...]
        xcb, r, i, a, s = _lru_gates(xc, wa_ref, ba_ref, wx_ref, bx_ref, lam_ref)
        hsv = hs_ref[...]
        dy = dy_ref[...].astype(F32)
        gel, dgel = _gelu_and_grad(g_ref[...].astype(F32))
        st_g[...] = (dy * hsv * dgel).astype(BF16)

        b_s[...] = _up(a, an_s[...], 1)
        g_s[...] = dy * gel
        row = lax.broadcasted_iota(jnp.int32, (SUB, C), 0)

        def step(k, carry):
            o = pl.multiple_of((nt - 1 - k) * SUB, SUB)
            B = b_s[pl.ds(o, SUB), :]
            G = g_s[pl.ds(o, SUB), :]
            for sh in (1, 2, 4):
                m = row < SUB - sh
                Bn = pltpu.roll(B, SUB - sh, 0)
                Gn = pltpu.roll(G, SUB - sh, 0)
                G = jnp.where(m, B * Gn + G, G)
                B = jnp.where(m, B * Bn, B)
            dh = B * carry + G
            dh_s[pl.ds(o, SUB), :] = dh
            return jnp.broadcast_to(dh[0:1, :], (SUB, C))

        dhn_s[...] = lax.fori_loop(0, nt, step, dhn_s[...])
        an_s[...] = a[:SUB, :]
        dh = dh_s[...]

        hprev = _down(hsv, jnp.where(t > 0, hh_ref[...][HALO - SUB:, :], 0.0), 1)
        d_a = dh * hprev
        ixc = i * xc
        d_s = dh * ixc
        d_i = dh * s * xc
        d_xc = dh * s * i
        d_l = d_a * a - d_s * (a * a) / s
        sp = jax.nn.softplus(-lam_ref[...])
        _acc_row(dlam_ref, d_l * r * (LRU_C * jax.nn.sigmoid(-lam_ref[...])), first)
        d_zr = (d_l * (-LRU_C * sp)) * r * (1.0 - r)
        d_zi = d_i * i * (1.0 - i)
        _acc_row(dba_ref, d_zr, first)
        _acc_row(dbx_ref, d_zi, first)
        d_zrb = d_zr.astype(BF16)
        d_zib = d_zi.astype(BF16)
        tn_dims = (((0,), (0,)), ((), ()))
        nt_dims = (((1,), (1,)), ((), ()))
        gwa = lax.dot_general(xcb, d_zrb, tn_dims, preferred_element_type=F32)
        gwx = lax.dot_general(xcb, d_zib, tn_dims, preferred_element_type=F32)
        _acc(dwa_ref, gwa, first)
        _acc(dwx_ref, gwx, first)
        d_xc = (d_xc + lax.dot_general(d_zrb, wa_ref[...], nt_dims, preferred_element_type=F32)
                + lax.dot_general(d_zib, wx_ref[...], nt_dims, preferred_element_type=F32))
        _acc_row(dcb_ref, d_xc, first)
        _conv_dw(dcw_ref, d_xc, x, xprev, first)
        st_x[...] = _conv_t(d_xc, dxn_s[...], cw_ref).astype(BF16)
        dxn_s[...] = d_xc[:SUB, :]

        cx = _store_block(st_x, dp_ref, sems.at[0], t * tb, c * C)
        cg = _store_block(st_g, dp_ref, sems.at[1], t * tb, d + c * C)
        cx.start()
        cg.start()
        cx.wait()
        cg.wait()

    rev = lambda c, tr: (nT - 1 - tr, c)
    vec = pl.BlockSpec((1, C), lambda c, tr: (0, c))
    sq = pl.BlockSpec((None, C, C), lambda c, tr: (c, 0, 0))
    cwb = pl.BlockSpec((kw, C), lambda c, tr: (0, c))
    halo_prev = lambda c, tr: (jnp.maximum((nT - 1 - tr) * hb - 1, 0), c)
    return pl.pallas_call(
        body, name=name,
        out_shape=(_S(dp.shape, dp.dtype), _S((kw, d), F32), _S((1, d), F32), _S((nC, C, C), F32), _S((1, d), F32),
                   _S((nC, C, C), F32), _S((1, d), F32), _S((1, d), F32)),
        grid=(nC, nT),
        in_specs=[pl.BlockSpec((tb, C), rev),
                  pl.BlockSpec((HALO, C), halo_prev),
                  pl.BlockSpec((tb, C), lambda c, tr: (nT - 1 - tr, nC + c)),
                  pl.BlockSpec((tb, C), rev),
                  pl.BlockSpec((HALO, C), halo_prev),
                  pl.BlockSpec((tb, C), rev),
                  cwb, vec, sq, vec, sq, vec, vec, ANY],
        out_specs=(ANY, cwb, vec, sq, vec, sq, vec, vec),
        scratch_shapes=[pltpu.VMEM((tb, C), F32), pltpu.VMEM((tb, C), F32), pltpu.VMEM((tb, C), F32),
                        pltpu.VMEM((SUB, C), F32), pltpu.VMEM((SUB, C), F32), pltpu.VMEM((SUB, C), F32),
                        pltpu.VMEM((tb, C), BF16), pltpu.VMEM((tb, C), BF16), pltpu.SemaphoreType.DMA((2,))],
        input_output_aliases={13: 0},
        compiler_params=_cparams(("arbitrary", "arbitrary")),
    )(p, p, p, hs, hs, dyl, conv_w, conv_b, wa_bd, ba, wx_bd, bx, lam, dp)


def _sc_fwd(p, conv_w, *, d, name):
    T = p.shape[0]
    C = _tile(d, C_EW)
    nC = d // C
    tb = _tile(T, TB, HALO)
    nT, hb = T // tb, tb // HALO

    def body(b_ref, c_ref, ch_ref, v_ref, vh_ref, w_ref, y_ref):
        t = pl.program_id(1)
        cv = c_ref[...].astype(F32) * v_ref[...].astype(F32)
        cvp = _prev8(ch_ref, t) * _prev8(vh_ref, t)
        y_ref[...] = (b_ref[...].astype(F32) * _conv(cv, cvp, w_ref)).astype(BF16)

    seg = lambda k: pl.BlockSpec((tb, C), lambda c, t: (t, k * nC + c))
    hseg = lambda k: pl.BlockSpec((HALO, C), _halo_prev_map(hb, lambda c: k * nC + c))
    return pl.pallas_call(
        body, name=name, out_shape=_S((T, d), BF16), grid=(nC, nT),
        in_specs=[seg(2), seg(3), hseg(3), seg(4), hseg(4), pl.BlockSpec((conv_w.shape[0], C), lambda c, t: (0, c))],
        out_specs=pl.BlockSpec((tb, C), lambda c, t: (t, c)),
        compiler_params=_cparams(("parallel", "parallel")),
    )(p, p, p, p, p, conv_w)


def _sc_bwd(p, dys, dp, conv_w, *, d, name):
    T = p.shape[0]
    C = _tile(d, C_EW)
    nC = d // C
    tb = _tile(T, TB, HALO)
    nT, hb = T // tb, tb // HALO
    kw = conv_w.shape[0]

    def body(b_ref, bn_ref, c_ref, ch_ref, v_ref, vh_ref, dy_ref, dyn_ref, w_ref, dp_in, dp_ref, dw_ref,
             st_b, st_c, st_v, sems):
        del dp_in
        c = pl.program_id(0)
        t = pl.program_id(1)
        last = t == nT - 1
        bv = b_ref[...].astype(F32)
        cvv = c_ref[...].astype(F32)
        vv = v_ref[...].astype(F32)
        dy = dy_ref[...].astype(F32)
        cv = cvv * vv
        cvp = _prev8(ch_ref, t) * _prev8(vh_ref, t)
        st_b[...] = (dy * _conv(cv, cvp, w_ref)).astype(BF16)
        dz = dy * bv
        dzn = _next8(dyn_ref, last) * _next8(bn_ref, last)
        _conv_dw(dw_ref, dz, cv, cvp, t == 0)
        dcv = _conv_t(dz, dzn, w_ref)
        st_c[...] = (dcv * vv).astype(BF16)
        st_v[...] = (dcv * cvv).astype(BF16)
        cps = [_store_block(st, dp_ref, sems.at[k], t * tb, (2 + k) * d + c * C)
               for k, st in enumerate((st_b, st_c, st_v))]
        for cp in cps:
            cp.start()
        for cp in cps:
            cp.wait()

    seg = lambda k: pl.BlockSpec((tb, C), lambda c, t: (t, k * nC + c))
    hseg = lambda k: pl.BlockSpec((HALO, C), _halo_prev_map(hb, lambda c: k * nC + c))
    last_h = T // HALO - 1
    nseg = lambda k: pl.BlockSpec((HALO, C), lambda c, t: (jnp.minimum((t + 1) * hb, last_h), k * nC + c))
    return pl.pallas_call(
        body, name=name, out_shape=(_S(dp.shape, dp.dtype), _S((kw, d), F32)), grid=(nC, nT),
        in_specs=[seg(2), nseg(2), seg(3), hseg(3), seg(4), hseg(4),
                  pl.BlockSpec((tb, C), lambda c, t: (t, c)), nseg(0),
                  pl.BlockSpec((kw, C), lambda c, t: (0, c)), ANY],
        out_specs=(ANY, pl.BlockSpec((kw, C), lambda c, t: (0, c))),
        scratch_shapes=[pltpu.VMEM((tb, C), BF16)] * 3 + [pltpu.SemaphoreType.DMA((3,))],
        input_output_aliases={9: 0},
        compiler_params=_cparams(("arbitrary", "arbitrary")),
    )(p, p, p, p, p, p, dys, dys, conv_w, dp)


def _merge_fwd(p, y_lru, y_sc, *, col0, name):
    T, D = y_lru.shape
    C = _tile(math.gcd(D, col0), 1024)
    nC = D // C
    k0 = col0 // C
    tb = _tile(T, 256, HALO)

    def body(gl_ref, gs_ref, yl_ref, ys_ref, o_ref):
        o_ref[...] = (jax.nn.sigmoid(gl_ref[...].astype(F32)) * yl_ref[...].astype(F32)
                      + jax.nn.sigmoid(gs_ref[...].astype(F32)) * ys_ref[...].astype(F32)).astype(BF16)

    blk = pl.BlockSpec((tb, C), lambda c, t: (t, c))
    return pl.pallas_call(
        body, name=name, out_shape=_S((T, D), BF16), grid=(nC, T // tb),
        in_specs=[pl.BlockSpec((tb, C), lambda c, t: (t, k0 + c)),
                  pl.BlockSpec((tb, C), lambda c, t: (t, k0 + nC + c)), blk, blk],
        out_specs=blk, compiler_params=_cparams(("parallel", "parallel")),
    )(p, p, y_lru, y_sc)


def _merge_bwd(p, y_lru, y_sc, dm, *, col0, name):
    T, D = y_lru.shape
    C = _tile(math.gcd(D, col0), 1024)
    nC = D // C
    k0 = col0 // C
    tb = _tile(T, 256, HALO)

    def body(gl_ref, gs_ref, yl_ref, ys_ref, dm_ref, dp_ref, dyl_ref, dys_ref, st_l, st_s, sems):
        c = pl.program_id(0)
        t = pl.program_id(1)
        dmv = dm_ref[...].astype(F32)
        sl = jax.nn.sigmoid(gl_ref[...].astype(F32))
        ss = jax.nn.sigmoid(gs_ref[...].astype(F32))
        dyl_ref[...] = (dmv * sl).astype(BF16)
        dys_ref[...] = (dmv * ss).astype(BF16)
        st_l[...] = (dmv * yl_ref[...].astype(F32) * sl * (1.0 - sl)).astype(BF16)
        st_s[...] = (dmv * ys_ref[...].astype(F32) * ss * (1.0 - ss)).astype(BF16)
        cl = _store_block(st_l, dp_ref, sems.at[0], t * tb, col0 + c * C)
        cs = _store_block(st_s, dp_ref, sems.at[1], t * tb, col0 + D + c * C)
        cl.start()
        cs.start()
        cl.wait()
        cs.wait()

    blk = pl.BlockSpec((tb, C), lambda c, t: (t, c))
    return pl.pallas_call(
        body, name=name, out_shape=(_S(p.shape, BF16), _S((T, D), BF16), _S((T, D), BF16)),
        grid=(nC, T // tb),
        in_specs=[pl.BlockSpec((tb, C), lambda c, t: (t, k0 + c)),
                  pl.BlockSpec((tb, C), lambda c, t: (t, k0 + nC + c)), blk, blk, blk],
        out_specs=(ANY, blk, blk),
        scratch_shapes=[pltpu.VMEM((tb, C), BF16), pltpu.VMEM((tb, C), BF16), pltpu.SemaphoreType.DMA((2,))],
        compiler_params=_cparams(("arbitrary", "arbitrary")),
    )(p, p, y_lru, y_sc, dm)


def _ffn_act_fwd(uu, conv_w, *, name):
    T = uu.shape[0]
    F = uu.shape[1] // 2
    C = _tile(F, C_EW)
    nC = F // C
    tb = _tile(T, 256, HALO)
    nT, hb = T // tb, tb // HALO
    kw = conv_w.shape[0]

    def body(g_ref, gh_ref, v_ref, vh_ref, wg_ref, wv_ref, o_ref):
        t = pl.program_id(1)
        cg = _conv(g_ref[...].astype(F32), _prev8(gh_ref, t), wg_ref)
        cv = _conv(v_ref[...].astype(F32), _prev8(vh_ref, t), wv_ref)
        o_ref[...] = (cg * jax.nn.sigmoid(cg) * cv).astype(BF16)

    seg = lambda k: pl.BlockSpec((tb, C), lambda c, t: (t, k * nC + c))
    hseg = lambda k: pl.BlockSpec((HALO, C), _halo_prev_map(hb, lambda c: k * nC + c))
    wseg = lambda k: pl.BlockSpec((kw, C), lambda c, t: (0, k * nC + c))
    return pl.pallas_call(
        body, name=name, out_shape=_S((T, F), BF16), grid=(nC, nT),
        in_specs=[seg(0), hseg(0), seg(1), hseg(1), wseg(0), wseg(1)],
        out_specs=pl.BlockSpec((tb, C), lambda c, t: (t, c)),
        compiler_params=_cparams(("parallel", "parallel")),
    )(uu, uu, uu, uu, conv_w, conv_w)


def _ffn_act_bwd(uu, dact, conv_w, *, name):
    T = uu.shape[0]
    F = uu.shape[1] // 2
    C = _tile(F, C_EW)
    nC = F // C
    tb = _tile(T, 256, HALO)
    nT, hb = T // tb, tb // HALO
    kw = conv_w.shape[0]

    def body(g_ref, gh_ref, v_ref, vh_ref, da_ref, wg_ref, wv_ref, du_ref, dwg_ref, dwv_ref,
             gn_s, vn_s, st_g, st_v, sems):
        c = pl.program_id(0)
        tr = pl.program_id(1)
        t = nT - 1 - tr
        first = tr == 0

        @pl.when(first)
        def _():
            gn_s[...] = jnp.zeros_like(gn_s)
            vn_s[...] = jnp.zeros_like(vn_s)

        ug = g_ref[...].astype(F32)
        uv = v_ref[...].astype(F32)
        ugp = _prev8(gh_ref, t)
        uvp = _prev8(vh_ref, t)
        cg = _conv(ug, ugp, wg_ref)
        cv = _conv(uv, uvp, wv_ref)
        da = da_ref[...].astype(F32)
        sg = jax.nn.sigmoid(cg)
        d_cg = da * cv * (sg * (1.0 + cg * (1.0 - sg)))
        d_cv = da * (cg * sg)
        _conv_dw(dwg_ref, d_cg, ug, ugp, first)
        _conv_dw(dwv_ref, d_cv, uv, uvp, first)
        st_g[...] = _conv_t(d_cg, gn_s[...], wg_ref).astype(BF16)
        st_v[...] = _conv_t(d_cv, vn_s[...], wv_ref).astype(BF16)
        gn_s[...] = d_cg[:SUB, :]
        vn_s[...] = d_cv[:SUB, :]
        cpg = _store_block(st_g, du_ref, sems.at[0], t * tb, c * C)
        cpv = _store_block(st_v, du_ref, sems.at[1], t * tb, F + c * C)
        cpg.start()
        cpv.start()
        cpg.wait()
        cpv.wait()

    seg = lambda k: pl.BlockSpec((tb, C), lambda c, tr: (nT - 1 - tr, k * nC + c))
    hseg = lambda k: pl.BlockSpec((HALO, C), lambda c, tr: (jnp.maximum((nT - 1 - tr) * hb - 1, 0), k * nC + c))
    wseg = lambda k: pl.BlockSpec((kw, C), lambda c, tr: (0, k * nC + c))
    dwb = pl.BlockSpec((kw, C), lambda c, tr: (0, c))
    return pl.pallas_call(
        body, name=name, out_shape=(_S(uu.shape, BF16), _S((kw, F), F32), _S((kw, F), F32)), grid=(nC, nT),
        in_specs=[seg(0), hseg(0), seg(1), hseg(1), pl.BlockSpec((tb, C), lambda c, tr: (nT - 1 - tr, c)),
                  wseg(0), wseg(1)],
        out_specs=(ANY, dwb, dwb),
        scratch_shapes=[pltpu.VMEM((SUB, C), F32), pltpu.VMEM((SUB, C), F32),
                        pltpu.VMEM((tb, C), BF16), pltpu.VMEM((tb, C), BF16), pltpu.SemaphoreType.DMA((2,))],
        compiler_params=_cparams(("arbitrary", "arbitrary")),
    )(uu, uu, uu, uu, dact, conv_w, conv_w)


def _place():
    x, y, c = lax.axis_index("x"), lax.axis_index("y"), lax.axis_index("c")
    return x, y, c


def _chips(x, y):
    return [(1 - x, y), (x, 1 - y), (1 - x, 1 - y)]


def _all_gather(arrays, placed, over_ici, name):
    n = len(arrays)

    def body(*refs):
        ins, outs = refs[:n], refs[n:2 * n]
        send_sems, recv_sems, local_sems = refs[2 * n:]
        x, y, c = _place()
        me, sibling = (x, y, c), (x, y, 1 - c)
        chips = _chips(x, y)
        full = [a for a in range(n) if over_ici[a]]

        def idx(px, py, pc):
            return 4 * px + 2 * py + pc

        def copy(a, k, block, to):
            dst = outs[a].at[idx(*block)]
            src = ins[a] if (block is me and not placed[a]) else dst
            return pltpu.make_async_remote_copy(
                src_ref=src, dst_ref=dst, send_sem=send_sems.at[a, k], recv_sem=recv_sems.at[a, k],
                device_id=to, device_id_type=MESH)

        mine = [pltpu.make_async_copy(ins[a], outs[a].at[idx(*me)], local_sems.at[a])
                for a in range(n) if not placed[a]]
        for cp in mine:
            cp.start()
        first = []
        for a in full:
            first += [copy(a, 1 + j, me, (*chip, c)) for j, chip in enumerate(chips)]
        for a in range(n):
            first.append(copy(a, 0, me, sibling))
        for cp in first:
            cp.start()
        passed = []
        for a in full:
            for j, chip in enumerate(chips):
                copy(a, 1 + j, (*chip, c), me).wait_recv()
                cp = copy(a, 4 + j, (*chip, c), sibling)
                cp.start()
                passed.append(cp)
        for a in range(n):
            copy(a, 0, sibling, me).wait_recv()
        for a in full:
            for j, chip in enumerate(chips):
                copy(a, 4 + j, (*chip, 1 - c), me).wait_recv()
        for cp in first + passed:
            cp.wait_send()
        for cp in mine:
            cp.wait()

    return pl.pallas_call(
        body, name=name,
        out_shape=tuple(_S(s.shape if placed[a] else (N_DEV,) + s.shape, s.dtype) for a, s in enumerate(arrays)),
        in_specs=[ANY] * n, out_specs=tuple([ANY] * n),
        scratch_shapes=[pltpu.SemaphoreType.DMA((n, 7)), pltpu.SemaphoreType.DMA((n, 7)),
                        pltpu.SemaphoreType.DMA((n,))],
        input_output_aliases={a: a for a in range(n) if placed[a]},
    )(*arrays)


def _rows_of(ref, blk, rows):
    v = ref.at[blk]
    return v if rows is None else v.at[pl.ds(rows[0], rows[1])]


def _gather_task(buf, rows=None):
    def copies(refs, ss, rs):
        x, y, c = _place()
        me = 4 * x + 2 * y + c
        return [pltpu.make_async_remote_copy(
            src_ref=_rows_of(refs[0], me, rows), dst_ref=_rows_of(refs[0], me, rows),
            send_sem=ss.at[j], recv_sem=rs.at[j], device_id=(px, py, c), device_id_type=MESH)
            for j, (px, py) in enumerate(_chips(x, y))]

    def start(refs, ss, rs, ls):
        for cp in copies(refs, ss, rs):
            cp.start()

    def wait(refs, ss, rs, ls):
        x, y, c = _place()
        for j, (px, py) in enumerate(_chips(x, y)):
            pltpu.make_async_remote_copy(
                src_ref=_rows_of(refs[0], 4 * px + 2 * py + c, rows), dst_ref=_rows_of(refs[0], 4 * px + 2 * py + c, rows),
                send_sem=ss.at[j], recv_sem=rs.at[j], device_id=(px, py, c), device_id_type=MESH).wait_recv()
        for cp in copies(refs, ss, rs):
            cp.wait_send()

    return _Task([buf], [0], start, wait)


def _d2d_forward(bufs, rows, name):
    n = len(bufs)

    def body(*refs):
        outs = refs[n:2 * n]
        send_sems, recv_sems = refs[2 * n:]
        x, y, c = _place()
        cps = []
        for a in range(n):
            for j, (px, py) in enumerate(_chips(x, y)):
                blk = _rows_of(outs[a], 4 * px + 2 * py + c, rows[a])
                cps.append(pltpu.make_async_remote_copy(
                    src_ref=blk, dst_ref=blk, send_sem=send_sems.at[a, j], recv_sem=recv_sems.at[a, j],
                    device_id=(x, y, 1 - c), device_id_type=MESH))
        for cp in cps:
            cp.start()
        for a in range(n):
            for j, (px, py) in enumerate(_chips(x, y)):
                blk = _rows_of(outs[a], 4 * px + 2 * py + 1 - c, rows[a])
                pltpu.make_async_remote_copy(
                    src_ref=blk, dst_ref=blk, send_sem=send_sems.at[a, j], recv_sem=recv_sems.at[a, j],
                    device_id=(x, y, 1 - c), device_id_type=MESH).wait_recv()
        for cp in cps:
            cp.wait_send()

    return pl.pallas_call(
        body, name=name, out_shape=tuple(_S(b.shape, b.dtype) for b in bufs),
        in_specs=[ANY] * n, out_specs=tuple([ANY] * n),
        scratch_shapes=[pltpu.SemaphoreType.DMA((n, 3)), pltpu.SemaphoreType.DMA((n, 3))],
        input_output_aliases={a: a for a in range(n)},
    )(*bufs)


def _exchange_task(parts, landing):
    def copies(refs, ss, rs):
        x, y, c = _place()
        myq = 2 * x + y
        return [pltpu.make_async_remote_copy(
            src_ref=refs[0].at[2 * px + py], dst_ref=refs[1].at[myq],
            send_sem=ss.at[k], recv_sem=rs.at[k], device_id=(px, py, c), device_id_type=MESH)
            for k, (px, py) in enumerate(_chips(x, y))]

    def start(refs, ss, rs, ls):
        for cp in copies(refs, ss, rs):
            cp.start()

    def wait(refs, ss, rs, ls):
        x, y, c = _place()
        for k, (px, py) in enumerate(_chips(x, y)):
            pltpu.make_async_remote_copy(
                src_ref=refs[0].at[2 * x + y], dst_ref=refs[1].at[2 * px + py],
                send_sem=ss.at[k], recv_sem=rs.at[k], device_id=(px, py, c), device_id_type=MESH).wait_recv()
        for cp in copies(refs, ss, rs):
            cp.wait_send()

    return _Task([parts, landing], [1], start, wait)


def _swap_halves(grads, name):
    n = len(grads)
    g4 = [g.reshape((N_CHIP, 2) + g.shape[1:]) for g in grads]

    def body(*refs):
        ins, outs = refs[:n], refs[n:2 * n]
        send_sems, recv_sems = refs[2 * n:]
        x, y, c = _place()
        cps = [pltpu.make_async_remote_copy(
            src_ref=ins[a].at[:, 1 - c], dst_ref=outs[a],
            send_sem=send_sems.at[a], recv_sem=recv_sems.at[a],
            device_id=(x, y, 1 - c), device_id_type=MESH) for a in range(n)]
        for cp in cps:
            cp.start()
        for cp in cps:
            cp.wait()

    return pl.pallas_call(
        body, name=name,
        out_shape=tuple(_S((N_CHIP,) + g.shape[1:], g.dtype) for g in grads),
        in_specs=[ANY] * n, out_specs=tuple([ANY] * n),
        scratch_shapes=[pltpu.SemaphoreType.DMA((n,)), pltpu.SemaphoreType.DMA((n,))],
    )(*g4)


def _add_halves(g, landed, place, name):
    _, r, cc = g.shape
    g4 = g.reshape(N_CHIP, 2, r, cc)
    tr = _tile(r, 512, HALO)

    def body(s_ref, g_ref, l_ref, o_ref, land_ref):
        q = pl.program_id(1)
        v = (g_ref[...].astype(F32) + l_ref[...].astype(F32)).astype(BF16)
        o_ref[...] = v

        @pl.when(q == s_ref[1])
        def _():
            land_ref[...] = v

    return pl.pallas_call(
        body, name=name, out_shape=(_S((N_CHIP, r, cc), BF16), _S((N_CHIP, r, cc), BF16)),
        grid_spec=pltpu.PrefetchScalarGridSpec(
            num_scalar_prefetch=1, grid=(r // tr, N_CHIP),
            in_specs=[pl.BlockSpec((None, None, tr, cc), lambda i, q, s: (q, s[0], i, 0)),
                      pl.BlockSpec((None, tr, cc), lambda i, q, s: (q, i, 0))],
            out_specs=(pl.BlockSpec((None, tr, cc), lambda i, q, s: (q, i, 0)),
                       pl.BlockSpec((None, tr, cc), lambda i, q, s: (s[1], i, 0)))),
        compiler_params=_cparams(("arbitrary", "arbitrary")),
    )(place, g4, landed)


def _all_reduce_small(pack, name):
    R = pack.shape[0]

    def body(p_ref, o_ref, buf, send_sems, recv_sems):
        x, y, c = _place()
        me = 4 * x + 2 * y + c
        buf[me] = p_ref[...]
        cps = []
        for k in range(N_DEV - 1):
            m = k + 1
            peer = (x ^ (m >> 2), y ^ ((m >> 1) & 1), c ^ (m & 1))
            cps.append(pltpu.make_async_remote_copy(
                src_ref=p_ref, dst_ref=buf.at[me], send_sem=send_sems.at[k], recv_sem=recv_sems.at[k],
                device_id=peer, device_id_type=MESH))
        for cp in cps:
            cp.start()
        for k in range(N_DEV - 1):
            m = k + 1
            peer_idx = 4 * (x ^ (m >> 2)) + 2 * (y ^ ((m >> 1) & 1)) + (c ^ (m & 1))
            pltpu.make_async_remote_copy(
                src_ref=p_ref, dst_ref=buf.at[peer_idx], send_sem=send_sems.at[k], recv_sem=recv_sems.at[k],
                device_id=(x, y, c), device_id_type=MESH).wait_recv()
        for cp in cps:
            cp.wait_send()
        acc = buf[0]
        for k in range(1, N_DEV):
            acc = acc + buf[k]
        o_ref[...] = acc

    return pl.pallas_call(
        body, name=name, out_shape=_S((R, LANES), F32),
        in_specs=[VMEM_SPEC], out_specs=VMEM_SPEC,
        scratch_shapes=[pltpu.VMEM((N_DEV, R, LANES), F32), pltpu.SemaphoreType.DMA((N_DEV - 1,)),
                        pltpu.SemaphoreType.DMA((N_DEV - 1,))],
        compiler_params=_cparams(),
    )(pack)


def _adamw_math(w, g, m, v):
    m = ADAM_B1 * m + (1.0 - ADAM_B1) * g
    v = ADAM_B2 * v + (1.0 - ADAM_B2) * (g * g)
    m_hat = m / (1.0 - ADAM_B1 ** ADAM_STEP)
    v_hat = v / (1.0 - ADAM_B2 ** ADAM_STEP)
    delta = -ADAM_LR * (m_hat / (jnp.sqrt(v_hat) + ADAM_EPS) + ADAM_WD * w)
    return delta, m, v


def _adamw_big(parts, w, m, v, name):
    r, cc = w.shape
    tr = _tile(r, 128, HALO)

    def body(p_ref, w_ref, m_ref, v_ref, g_ref, d_ref, nm_ref, nv_ref):
        g = p_ref[0].astype(F32)
        for q in range(1, N_CHIP):
            g = g + p_ref[q].astype(F32)
        g_ref[...] = g
        d_ref[...], nm_ref[...], nv_ref[...] = _adamw_math(w_ref[...], g, m_ref[...], v_ref[...])

    blk = pl.BlockSpec((tr, cc), lambda i: (i, 0))
    return pl.pallas_call(
        body, name=name, out_shape=tuple(_S((r, cc), F32) for _ in range(4)), grid=(r // tr,),
        in_specs=[pl.BlockSpec((N_CHIP, tr, cc), lambda i: (0, i, 0)), blk, blk, blk],
        out_specs=(blk, blk, blk, blk), compiler_params=_cparams(("parallel",)),
    )(parts, w, m, v)


def _adamw_small(ws, gs, ms, vs, name):
    n = len(ws)

    def body(*refs):
        w_r, g_r, m_r, v_r = refs[:n], refs[n:2 * n], refs[2 * n:3 * n], refs[3 * n:4 * n]
        d_r, nm_r, nv_r = refs[4 * n:5 * n], refs[5 * n:6 * n], refs[6 * n:7 * n]
        for k in range(n):
            d_r[k][...], nm_r[k][...], nv_r[k][...] = _adamw_math(w_r[k][...], g_r[k][...], m_r[k][...], v_r[k][...])

    shapes = tuple(_S(w.shape, F32) for w in ws)
    outs = pl.pallas_call(
        body, name=name, out_shape=shapes * 3,
        in_specs=[VMEM_SPEC] * (4 * n), out_specs=tuple([VMEM_SPEC] * (3 * n)),
        compiler_params=_cparams(),
    )(*ws, *gs, *ms, *vs)
    return outs[:n], outs[n:2 * n], outs[2 * n:]


def _block_diag(w, heads_per_block):
    H, hd, _ = w.shape
    nb = H // heads_per_block
    eye = jnp.eye(heads_per_block, dtype=w.dtype)
    w4 = w.reshape(nb, heads_per_block, hd, hd)
    return jnp.einsum("nhab,hg->nhagb", w4, eye).reshape(nb, heads_per_block * hd, heads_per_block * hd)


def _diag_blocks(bd, heads_per_block, hd):
    nb = bd.shape[0]
    b5 = bd.reshape(nb, heads_per_block, hd, heads_per_block, hd)
    return jnp.stack([b5[:, h, :, h, :] for h in range(heads_per_block)], axis=1).reshape(nb * heads_per_block, hd, hd)


def _as_rows(a):
    if a.ndim == 1:
        return a.reshape(-1, LANES) if a.shape[0] % LANES == 0 else a.reshape(1, -1)
    if a.ndim == 3:
        return a.reshape(-1, LANES) if (a.size % LANES == 0) else a.reshape(a.shape[0] * a.shape[1], a.shape[2])
    return a


def kernel(x, g_mix, w_in, lru_conv_w, lru_conv_b, lru_wa, lru_ba, lru_wx, lru_bx, lru_lambda, lru_w_out, sc_conv_w, sc_w_out, w_o, g_ffn, ffn_w_up, ffn_conv_w, ffn_w_down, g_final, loss_target, m_g_mix, m_w_in, m_lru_conv_w, m_lru_conv_b, m_lru_wa, m_lru_ba, m_lru_wx, m_lru_bx, m_lru_lambda, m_lru_w_out, m_sc_conv_w, m_sc_w_out, m_w_o, m_g_ffn, m_ffn_w_up, m_ffn_conv_w, m_ffn_w_down, m_g_final, v_g_mix, v_w_in, v_lru_conv_w, v_lru_conv_b, v_lru_wa, v_lru_ba, v_lru_wx, v_lru_bx, v_lru_lambda, v_lru_w_out, v_sc_conv_w, v_sc_w_out, v_w_o, v_g_ffn, v_ffn_w_up, v_ffn_conv_w, v_ffn_w_down, v_g_final):
    T, D = x.shape[1], x.shape[2]
    d_lru = lru_lambda.shape[0]
    d_sc = sc_conv_w.shape[1] * N_DEV
    F = ffn_w_down.shape[0] * N_DEV
    H = lru_wa.shape[0]
    assert d_lru == d_sc and H * HEAD_DIM == d_lru
    xs = x.reshape(T, D)
    tgt = loss_target.reshape(T, D)
    my_x, my_y, my_c = _place()
    me = 4 * my_x + 2 * my_y + my_c

    big = [w_in, lru_w_out, sc_w_out, w_o, ffn_w_up, ffn_w_down]
    big_names = ["w_in", "lru_w_out", "sc_w_out", "w_o", "ffn_w_up", "ffn_w_down"]
    me_idx = jnp.reshape(me, (1,)).astype(jnp.int32)
    big_bf = [_cast_into_slot(w, me_idx, "cast_" + nm) for w, nm in zip(big, big_names)]
    pad_rows = lambda a: jnp.pad(a, ((0, SUB - a.shape[0]), (0, 0)))
    gathered = _all_gather(big_bf + [pad_rows(lru_conv_w), pad_rows(sc_conv_w), pad_rows(ffn_conv_w)],
                           [True] * 6 + [False] * 3,
                           [True, False, False, False, False, False, True, True, True], "all_gather_first")
    W_in, W_lo, W_so, W_o8, W_up, W_dn8 = gathered[:6]
    full_cols = lambda g, kw: g[:, :kw, :].transpose(1, 0, 2).reshape(kw, -1)
    cw_lru = full_cols(gathered[6], lru_conv_w.shape[0])
    cw_sc = full_cols(gathered[7], sc_conv_w.shape[0])
    cw_ffn = full_cols(gathered[8], ffn_conv_w.shape[0])

    C = _tile(d_lru, C_LRU)
    hpb = C // HEAD_DIM
    wa_bd = _block_diag(lru_wa, hpb).astype(BF16)
    wx_bd = _block_diag(lru_wx, hpb).astype(BF16)
    cb, ba, bx, lam = (a.reshape(1, d_lru) for a in (lru_conv_b, lru_ba, lru_bx, lru_lambda))

    h1 = _rms_fwd(xs, g_mix, "rms_mix")
    kq = W_up.shape[1] // 4
    p, ((W_lo,), (W_so,), (W_o8,), (W_up,)) = _mm_nn(
        h1, W_in, out_dtype=BF16, name="mm_in",
        tasks=[_gather_task(W_lo), _gather_task(W_so), _gather_task(W_o8), _gather_task(W_up, (0, 2 * kq))])
    W_lo, W_so, W_o8, W_up = _d2d_forward([W_lo, W_so, W_o8, W_up], [None, None, None, (0, 2 * kq)], "ag_forward_1")
    hs, yl_pre = _lru_fwd(p, cw_lru, cb, wa_bd, ba, wx_bd, bx, lam, name="lru_fwd")
    ys_pre = _sc_fwd(p, cw_sc, d=d_sc, name="sc_fwd")
    y_lru, ((W_up,),) = _mm_nn(yl_pre, W_lo, out_dtype=BF16, name="mm_lru_out", tm=2048,
                               tasks=[_gather_task(W_up, (2 * kq, kq))])
    y_sc, ((W_up,),) = _mm_nn(ys_pre, W_so, out_dtype=BF16, name="mm_sc_out", tm=2048,
                              tasks=[_gather_task(W_up, (3 * kq, kq))])
    gate0 = 2 * d_lru + 3 * d_sc
    merged = _merge_fwd(p, y_lru, y_sc, col0=gate0, name="merge_fwd")
    W_o = W_o8.reshape(1, D, D)
    x1 = _mm_nn(merged, W_o, out_dtype=F32, residual=xs, name="mm_o")
    (W_up,) = _d2d_forward([W_up], [(2 * kq, 2 * kq)], "ag_forward_2")
    h2 = _rms_fwd(x1, g_ffn, "rms_ffn")
    uu, ((W_dn8,),) = _mm_nn(h2, W_up, out_dtype=BF16, name="mm_up", tasks=[_gather_task(W_dn8)])
    act = _ffn_act_fwd(uu, cw_ffn, name="ffn_act_fwd")
    (W_dn8,) = _d2d_forward([W_dn8], [None], "ag_forward_3")
    W_dn = W_dn8.reshape(1, F, D)
    x2 = _mm_nn(act, W_dn, out_dtype=F32, residual=x1, name="mm_down", tn=512, tk=F)
    dx2, dx2b, loss_part, dg_final = _loss_head(x2, g_final, tgt, "loss_head")

    place = jnp.stack([my_c, 2 * my_x + my_y]).astype(jnp.int32)

    def _reduce_cores(g, nm):
        (landed,) = _swap_halves([g], "rs_swap_" + nm)
        return _add_halves(g, landed, place, "rs_add_" + nm)

    dact = _mm_nt(dx2b, W_dn, out_dtype=BF16, name="mm_down_dx", tko=1408)
    gW_dn = _mm_tn(act, dx2b, 1, out_dtype=BF16, name="mm_down_dw").reshape(N_DEV, F // N_DEV, D)
    parts_dn = _reduce_cores(gW_dn, "ffn_w_down")
    duu, dcw_ffn_g, dcw_ffn_v = _ffn_act_bwd(uu, dact, cw_ffn, name="ffn_act_bwd")
    dh2, ((mine_dn,),) = _mm_nt(duu, W_up, out_dtype=BF16, name="mm_up_dx", tasks=[_exchange_task(*parts_dn)])
    gW_up = _mm_tn(h2, duu, N_DEV, out_dtype=BF16, name="mm_up_dw")
    parts_up = _reduce_cores(gW_up, "ffn_w_up")
    dx1, dx1b, dg_ffn = _rms_bwd(x1, g_ffn, dh2, dx2, "rms_ffn_bwd")
    dmerged = _mm_nt(dx1b, W_o, out_dtype=BF16, name="mm_o_dx")
    gW_o = _mm_tn(merged, dx1b, 1, out_dtype=BF16, name="mm_o_dw").reshape(N_DEV, D // N_DEV, D)
    parts_o = _reduce_cores(gW_o, "w_o")
    dp, dy_lru, dy_sc = _merge_bwd(p, y_lru, y_sc, dmerged, col0=gate0, name="merge_bwd")
    dyl_pre = _mm_nt(dy_lru, W_lo, out_dtype=BF16, name="mm_lru_out_dx")
    gW_lo, ((mine_o,),) = _mm_tn(yl_pre, dy_lru, N_DEV, out_dtype=BF16, name="mm_lru_out_dw",
                                 tasks=[_exchange_task(*parts_o)])
    parts_lo = _reduce_cores(gW_lo, "lru_w_out")
    dys_pre = _mm_nt(dy_sc, W_so, out_dtype=BF16, name="mm_sc_out_dx")
    gW_so, ((mine_lo,),) = _mm_tn(ys_pre, dy_sc, N_DEV, out_dtype=BF16, name="mm_sc_out_dw",
                                  tasks=[_exchange_task(*parts_lo)])
    parts_so = _reduce_cores(gW_so, "sc_w_out")
    dp, dcw_sc = _sc_bwd(p, dys_pre, dp, cw_sc, d=d_sc, name="sc_bwd")
    dp, dcw_lru, dcb, dwa_bd, dba, dwx_bd, dbx, dlam = _lru_bwd(
        p, hs, dyl_pre, dp, cw_lru, cb, wa_bd, ba, wx_bd, bx, lam, name="lru_bwd")
    gW_in, ((mine_up,), (mine_so,)) = _mm_tn(h1, dp, N_DEV, out_dtype=BF16, name="mm_in_dw",
                                             tasks=[_exchange_task(*parts_up), _exchange_task(*parts_so)])
    parts_in = _reduce_cores(gW_in, "w_in")
    dh1, ((mine_in,),) = _mm_nt(dp, W_in, out_dtype=BF16, name="mm_in_dx", tasks=[_exchange_task(*parts_in)])
    grad_x, _, dg_mix = _rms_bwd(xs, g_mix, dh1, dx1, "rms_mix_bwd")

    mine = [mine_in, mine_lo, mine_so, mine_o, mine_up, mine_dn]
    big_m = [m_w_in, m_lru_w_out, m_sc_w_out, m_w_o, m_ffn_w_up, m_ffn_w_down]
    big_v = [v_w_in, v_lru_w_out, v_sc_w_out, v_w_o, v_ffn_w_up, v_ffn_w_down]
    big_out = {nm: _adamw_big(pt, w, m, v, "adamw_" + nm)
               for nm, pt, w, m, v in zip(big_names, mine, big, big_m, big_v)}

    dwa = _diag_blocks(dwa_bd, hpb, HEAD_DIM)
    dwx = _diag_blocks(dwx_bd, hpb, HEAD_DIM)
    dcw_ffn = jnp.concatenate([dcw_ffn_g, dcw_ffn_v], axis=1)
    small_full = [dg_mix, dcw_lru, dcb, dwa, dba, dwx, dbx, dlam, dcw_sc, dg_ffn, dcw_ffn, dg_final]
    flat = jnp.concatenate([a.reshape(-1) for a in small_full])
    n_flat = flat.shape[0]
    rows = -(-n_flat // (SUB * LANES)) * SUB
    pack = jnp.pad(flat, (0, rows * LANES - n_flat)).reshape(rows, LANES)
    total = _all_reduce_small(pack, "all_reduce_small").reshape(-1)
    sums, o = [], 0
    for a in small_full:
        sums.append(total[o:o + a.size].reshape(a.shape))
        o += a.size
    (sg_mix, scw_lru, scb, swa, sba, swx, sbx, slam, scw_sc, sg_ffn, scw_ffn, sg_final) = sums

    def my_cols(a):
        n = a.shape[1] // N_DEV
        return lax.dynamic_slice_in_dim(a, me * n, n, axis=1)

    small_names = ["g_mix", "lru_conv_w", "lru_conv_b", "lru_wa", "lru_ba", "lru_wx", "lru_bx", "lru_lambda",
                   "sc_conv_w", "g_ffn", "ffn_conv_w", "g_final"]
    small_w = [g_mix, lru_conv_w, lru_conv_b, lru_wa, lru_ba, lru_wx, lru_bx, lru_lambda, sc_conv_w, g_ffn,
               ffn_conv_w, g_final]
    small_m = [m_g_mix, m_lru_conv_w, m_lru_conv_b, m_lru_wa, m_lru_ba, m_lru_wx, m_lru_bx, m_lru_lambda,
               m_sc_conv_w, m_g_ffn, m_ffn_conv_w, m_g_final]
    small_v = [v_g_mix, v_lru_conv_w, v_lru_conv_b, v_lru_wa, v_lru_ba, v_lru_wx, v_lru_bx, v_lru_lambda,
               v_sc_conv_w, v_g_ffn, v_ffn_conv_w, v_g_final]
    small_g = [sg_mix.reshape(D), my_cols(scw_lru), scb.reshape(d_lru), swa, sba.reshape(d_lru), swx,
               sbx.reshape(d_lru), slam.reshape(d_lru), my_cols(scw_sc), sg_ffn.reshape(D), my_cols(scw_ffn),
               sg_final.reshape(D)]
    sd, snm, snv = _adamw_small([_as_rows(a) for a in small_w], [_as_rows(a) for a in small_g],
                                [_as_rows(a) for a in small_m], [_as_rows(a) for a in small_v], "adamw_small")
    small_out = {nm: (g, d.reshape(w.shape), nm_.reshape(w.shape), nv_.reshape(w.shape))
                 for nm, w, g, d, nm_, nv_ in zip(small_names, small_w, small_g, sd, snm, snv)}

    loss = lax.psum(loss_part[0, 0], AXES)
    order = ["g_mix", "w_in", "lru_conv_w", "lru_conv_b", "lru_wa", "lru_ba", "lru_wx", "lru_bx", "lru_lambda",
             "lru_w_out", "sc_conv_w", "sc_w_out", "w_o", "g_ffn", "ffn_w_up", "ffn_conv_w", "ffn_w_down", "g_final"]
    res = {**big_out, **small_out}
    return (loss, grad_x.reshape(x.shape),
            *[res[nm][0] for nm in order], *[res[nm][1] for nm in order],
            *[res[nm][2] for nm in order], *[res[nm][3] for nm in order])
```

```python
import functools
import math

import jax
import jax.numpy as jnp
from jax import lax
from jax.experimental import pallas as pl
from jax.experimental.pallas import tpu as pltpu

F32, BF16 = jnp.float32, jnp.bfloat16
MESH = pl.DeviceIdType.MESH
N_DEV = 8
N_CHIP = 4
AXES = ("x", "y", "c")

EPS = 1e-6
LRU_C = 8.0
HEAD_DIM = 64
ADAM_LR, ADAM_B1, ADAM_B2, ADAM_EPS, ADAM_WD, ADAM_STEP = 0.001, 0.9, 0.999, 1e-08, 0.01, 10

VMEM_LIMIT = 48 * 1024 * 1024
LANES = 128
SUB = 8
HALO = 16
TB = 512
C_LRU = 256
C_EW = 512
TM, TN, TK = 512, 1536, 2048


def _tile(n, pref, align=LANES):
    best = None
    for d in range(align, min(n, pref) + 1, align):
        if n % d == 0:
            best = d
    return best or n


def _cparams(sem=None, vmem=VMEM_LIMIT):
    kw = dict(vmem_limit_bytes=vmem)
    if sem is not None:
        kw["dimension_semantics"] = sem
    return pltpu.CompilerParams(**kw)


def _S(shape, dtype):
    return jax.ShapeDtypeStruct(shape, dtype)


ANY = pl.BlockSpec(memory_space=pl.ANY)
VMEM_SPEC = pl.BlockSpec(memory_space=pltpu.VMEM)


class _Task:
    def __init__(self, arrays, aliased, start, wait, fresh=(), nsem=3):
        self.arrays, self.aliased, self.start, self.wait = arrays, aliased, start, wait
        self.fresh, self.nsem = list(fresh), nsem


def _call(name, grid, compute, in_specs, args, out_shape, out_specs, scratch, tasks=(), own_aliases=None):
    n_in, n_out, n_scr = len(args), len(out_shape), len(scratch)
    x_in, x_out, aliases, where = [], [], dict(own_aliases or {}), []
    for t in tasks:
        places = []
        for k, arr in enumerate(t.arrays):
            if k in t.aliased:
                aliases[n_in + len(x_in)] = n_out + len(x_out)
                places.append(("out", len(x_out)))
                x_out.append(_S(arr.shape, arr.dtype))
            else:
                places.append(("in", len(x_in)))
            x_in.append(arr)
        for shp in t.fresh:
            places.append(("out", len(x_out)))
            x_out.append(shp)
        where.append(places)
    n_xi, n_xo = len(x_in), len(x_out)

    def body(*refs):
        ins, xi = refs[:n_in], refs[n_in:n_in + n_xi]
        o0 = n_in + n_xi
        outs, xo = refs[o0:o0 + n_out], refs[o0 + n_out:o0 + n_out + n_xo]
        s0 = o0 + n_out + n_xo
        scr, sems = refs[s0:s0 + n_scr], refs[s0 + n_scr:]
        ids = [pl.program_id(a) for a in range(len(grid))]

        def task_refs(ti):
            return [xo[i] if kind == "out" else xi[i] for kind, i in where[ti]]

        if tasks:
            first = functools.reduce(jnp.logical_and, [i == 0 for i in ids])

            @pl.when(first)
            def _():
                for ti, t in enumerate(tasks):
                    t.start(task_refs(ti), *sems[3 * ti:3 * ti + 3])

        compute(*ins, *outs, *scr)
        if tasks:
            last = functools.reduce(jnp.logical_and, [i == g - 1 for i, g in zip(ids, grid)])

            @pl.when(last)
            def _():
                for ti, t in enumerate(tasks):
                    t.wait(task_refs(ti), *sems[3 * ti:3 * ti + 3])

    sem_shapes = []
    for t in tasks:
        sem_shapes += [pltpu.SemaphoreType.DMA((t.nsem,)), pltpu.SemaphoreType.DMA((t.nsem,)),
                       pltpu.SemaphoreType.DMA((1,))]
    res = pl.pallas_call(
        body, name=name, grid=grid,
        in_specs=list(in_specs) + [ANY] * n_xi,
        out_specs=tuple(out_specs) + (ANY,) * n_xo,
        out_shape=tuple(out_shape) + tuple(x_out),
        scratch_shapes=list(scratch) + sem_shapes,
        input_output_aliases=aliases,
        compiler_params=_cparams(("arbitrary",) * len(grid)),
    )(*args, *x_in)
    outs, passed, o = res[:n_out], [], n_out
    for places in where:
        k = sum(1 for kind, _ in places if kind == "out")
        passed.append(list(res[o:o + k]))
        o += k
    return outs, passed


def _mm_nn(a, w3, *, out_dtype, name, residual=None, tm=TM, tn=TN, tk=TK, tasks=()):
    M, K = a.shape
    G, _, n = w3.shape
    tm, tn, tk = _tile(M, tm, SUB), _tile(n, tn), _tile(K, tk)
    nj, nk = n // tn, K // tk

    def compute(*refs):
        if residual is None:
            a_ref, w_ref, o_ref = refs[:3]
            r_ref = None
        else:
            a_ref, w_ref, r_ref, o_ref = refs[:4]

        def finish(r):
            if r_ref is not None:
                r = r + r_ref[...]
            o_ref[...] = r.astype(o_ref.dtype)

        if nk == 1:
            finish(jnp.dot(a_ref[...], w_ref[...], preferred_element_type=F32))
            return
        acc = refs[-1]
        k = pl.program_id(3)

        @pl.when(k == 0)
        def _():
            acc[...] = jnp.zeros_like(acc)

        acc[...] += jnp.dot(a_ref[...], w_ref[...], preferred_element_type=F32)

        @pl.when(k == nk - 1)
        def _():
            finish(acc[...])

    in_specs = [pl.BlockSpec((tm, tk), lambda g, j, i, k: (i, k)),
                pl.BlockSpec((None, tk, tn), lambda g, j, i, k: (g, k, j))]
    args = [a, w3]
    if residual is not None:
        in_specs.append(pl.BlockSpec((tm, tn), lambda g, j, i, k: (i, g * nj + j)))
        args.append(residual)
    outs, passed = _call(
        name, (G, nj, M // tm, nk), compute, in_specs, args, [_S((M, G * n), out_dtype)],
        [pl.BlockSpec((tm, tn), lambda g, j, i, k: (i, g * nj + j))],
        [] if nk == 1 else [pltpu.VMEM((tm, tn), F32)], tasks)
    return (outs[0], passed) if tasks else outs[0]


def _mm_nt(dy, w3, *, out_dtype, name, tm=1024, tko=1024, tn=TN, tasks=()):
    M, _ = dy.shape
    G, K, n = w3.shape
    tm, tko, tn = _tile(M, tm, SUB), _tile(K, tko), _tile(n, tn)
    nj = n // tn
    nr = G * nj

    def compute(dy_ref, w_ref, o_ref, acc):
        r = pl.program_id(2)

        @pl.when(r == 0)
        def _():
            acc[...] = jnp.zeros_like(acc)

        acc[...] += lax.dot_general(dy_ref[...], w_ref[...], (((1,), (1,)), ((), ())),
                                    preferred_element_type=F32)

        @pl.when(r == nr - 1)
        def _():
            o_ref[...] = acc[...].astype(o_ref.dtype)

    outs, passed = _call(
        name, (K // tko, M // tm, nr), compute,
        [pl.BlockSpec((tm, tn), lambda ko, i, r: (i, r)),
         pl.BlockSpec((None, tko, tn), lambda ko, i, r: (r // nj, ko, r % nj))],
        [dy, w3], [_S((M, K), out_dtype)], [pl.BlockSpec((tm, tko), lambda ko, i, r: (i, ko))],
        [pltpu.VMEM((tm, tko), F32)], tasks)
    return (outs[0], passed) if tasks else outs[0]


def _mm_tn(a, dy, G, *, out_dtype, name, tk=1024, tn=TN, tt=1024, tasks=()):
    M, K = a.shape
    n = dy.shape[1] // G
    tk, tn, tt = _tile(K, tk), _tile(n, tn), _tile(M, tt, SUB)
    nj, nt = n // tn, M // tt

    def compute(a_ref, dy_ref, o_ref, acc):
        t = pl.program_id(3)

        @pl.when(t == 0)
        def _():
            acc[...] = jnp.zeros_like(acc)

        acc[...] += lax.dot_general(a_ref[...], dy_ref[...], (((0,), (0,)), ((), ())),
                                    preferred_element_type=F32)

        @pl.when(t == nt - 1)
        def _():
            o_ref[...] = acc[...].astype(o_ref.dtype)

    outs, passed = _call(
        name, (G, nj, K // tk, nt), compute,
        [pl.BlockSpec((tt, tk), lambda g, j, k, t: (t, k)),
         pl.BlockSpec((tt, tn), lambda g, j, k, t: (t, g * nj + j))],
        [a, dy], [_S((G, K, n), out_dtype)], [pl.BlockSpec((None, tk, tn), lambda g, j, k, t: (g, k, j))],
        [pltpu.VMEM((tk, tn), F32)], tasks)
    return (outs[0], passed) if tasks else outs[0]


def _cast_into_slot(w, me_idx, name):
    R, C = w.shape
    tr = _tile(R, 512, HALO)

    def body(me_ref, w_ref, o_ref):
        del me_ref
        o_ref[...] = w_ref[...].astype(BF16)

    return pl.pallas_call(
        body, name=name, out_shape=_S((N_DEV, R, C), BF16),
        grid_spec=pltpu.PrefetchScalarGridSpec(
            num_scalar_prefetch=1, grid=(R // tr,),
            in_specs=[pl.BlockSpec((tr, C), lambda i, me_ref: (i, 0))],
            out_specs=pl.BlockSpec((None, tr, C), lambda i, me_ref: (me_ref[0], i, 0))),
        compiler_params=_cparams(("parallel",)),
    )(me_idx, w)


def _down(cur, prev8, j):
    return pltpu.roll(jnp.concatenate([prev8, cur], axis=0), j, 0)[SUB:, :]


def _up(cur, next8, j):
    n = cur.shape[0] + SUB
    return pltpu.roll(jnp.concatenate([cur, next8], axis=0), n - j, 0)[:cur.shape[0], :]


def _conv(x, prev8, w_ref):
    kw = w_ref.shape[0]
    y = x * w_ref[pl.ds(kw - 1, 1), :]
    for k in range(kw - 1):
        y = y + _down(x, prev8, kw - 1 - k) * w_ref[pl.ds(k, 1), :]
    return y


def _conv_t(dy, next8, w_ref):
    kw = w_ref.shape[0]
    dx = dy * w_ref[pl.ds(kw - 1, 1), :]
    for k in range(kw - 1):
        dx = dx + _up(dy, next8, kw - 1 - k) * w_ref[pl.ds(k, 1), :]
    return dx


def _conv_dw(dw_ref, dy, x, prev8, first):
    kw = dw_ref.shape[0]

    @pl.when(first)
    def _():
        dw_ref[...] = jnp.zeros_like(dw_ref)

    for k in range(kw):
        xs = x if k == kw - 1 else _down(x, prev8, kw - 1 - k)
        dw_ref[pl.ds(k, 1), :] += jnp.sum(dy * xs, axis=0, keepdims=True)


def _acc(ref, val, first):
    @pl.when(first)
    def _():
        ref[...] = jnp.zeros_like(ref)

    ref[...] += val


def _acc_row(ref, val, first):
    _acc(ref, jnp.sum(val, axis=0, keepdims=True), first)


def _prev8(h_ref, t):
    return jnp.where(t > 0, h_ref[...].astype(F32)[HALO - SUB:, :], 0.0)


def _next8(h_ref, is_last):
    return jnp.where(is_last, 0.0, h_ref[...].astype(F32)[:SUB, :])


_GELU_K0 = math.sqrt(2.0 / math.pi)
_GELU_K1 = 0.044715


def _gelu_and_grad(x):
    x2 = x * x
    th = jnp.tanh(_GELU_K0 * x * (1.0 + _GELU_K1 * x2))
    g = 0.5 * x * (1.0 + th)
    dg = 0.5 * (1.0 + th) + 0.5 * x * (1.0 - th * th) * (_GELU_K0 * (1.0 + 3.0 * _GELU_K1 * x2))
    return g, dg


def _neg_expm1(z):
    series = -z * (1.0 + z * (0.5 + z * (1.0 / 6.0 + z * (1.0 / 24.0))))
    return jnp.where(z > -0.03, series, 1.0 - jnp.exp(z))


def _store_block(stage_ref, dst_hbm, sem, row0, col0):
    tb, c = stage_ref.shape
    return pltpu.make_async_copy(stage_ref, dst_hbm.at[pl.ds(row0, tb), pl.ds(col0, c)], sem)


def _halo_prev_map(hb, col_fn):
    return lambda c, t: (jnp.maximum(t * hb - 1, 0), col_fn(c))


def _rms_fwd(x, g, name):
    T, D = x.shape
    tb = _tile(T, TB, SUB)

    def body(x_ref, g_ref, o_ref):
        xv = x_ref[...]
        rstd = lax.rsqrt(jnp.mean(xv * xv, axis=-1, keepdims=True) + EPS)
        o_ref[...] = (xv * rstd * g_ref[...]).astype(BF16)

    return pl.pallas_call(
        body, name=name, out_shape=_S((T, D), BF16), grid=(T // tb,),
        in_specs=[pl.BlockSpec((tb, D), lambda i: (i, 0)), pl.BlockSpec((1, D), lambda i: (0, 0))],
        out_specs=pl.BlockSpec((tb, D), lambda i: (i, 0)),
        compiler_params=_cparams(("parallel",)),
    )(x, g.reshape(1, D))


def _rms_bwd(x, g, dh, dres, name):
    T, D = x.shape
    tb = _tile(T, 256, SUB)

    def body(x_ref, g_ref, dh_ref, dr_ref, dx_ref, dxb_ref, dg_ref):
        i = pl.program_id(0)
        xv = x_ref[...]
        rstd = lax.rsqrt(jnp.mean(xv * xv, axis=-1, keepdims=True) + EPS)
        xn = xv * rstd
        dhv = dh_ref[...].astype(F32)
        _acc_row(dg_ref, dhv * xn, i == 0)
        dxn = dhv * g_ref[...]
        dx = dr_ref[...] + rstd * (dxn - xn * jnp.mean(dxn * xn, axis=-1, keepdims=True))
        dx_ref[...] = dx
        dxb_ref[...] = dx.astype(BF16)

    blk = pl.BlockSpec((tb, D), lambda i: (i, 0))
    vec = pl.BlockSpec((1, D), lambda i: (0, 0))
    return pl.pallas_call(
        body, name=name, out_shape=(_S((T, D), F32), _S((T, D), BF16), _S((1, D), F32)),
        grid=(T // tb,), in_specs=[blk, vec, blk, blk], out_specs=(blk, blk, vec),
        compiler_params=_cparams(("arbitrary",)),
    )(x, g.reshape(1, D), dh, dres)


def _loss_head(x2, g, target, name):
    T, D = x2.shape
    tb = _tile(T, 256, SUB)

    def body(x_ref, g_ref, t_ref, dx_ref, dxb_ref, loss_ref, dg_ref):
        i = pl.program_id(0)
        xv = x_ref[...]
        rstd = lax.rsqrt(jnp.mean(xv * xv, axis=-1, keepdims=True) + EPS)
        xn = xv * rstd
        err = xn * g_ref[...] - t_ref[...]
        part = 0.5 * jnp.sum(jnp.mean(err * err, axis=-1, keepdims=True), axis=0, keepdims=True)
        part = jnp.broadcast_to(part, (1, LANES))
        _acc(loss_ref, part, i == 0)
        dy = err * (1.0 / D)
        _acc_row(dg_ref, dy * xn, i == 0)
        dxn = dy * g_ref[...]
        dx = rstd * (dxn - xn * jnp.mean(dxn * xn, axis=-1, keepdims=True))
        dx_ref[...] = dx
        dxb_ref[...] = dx.astype(BF16)

    blk = pl.BlockSpec((tb, D), lambda i: (i, 0))
    vec = pl.BlockSpec((1, D), lambda i: (0, 0))
    return pl.pallas_call(
        body, name=name,
        out_shape=(_S((T, D), F32), _S((T, D), BF16), _S((1, LANES), F32), _S((1, D), F32)),
        grid=(T // tb,), in_specs=[blk, vec, blk],
        out_specs=(blk, blk, pl.BlockSpec((1, LANES), lambda i: (0, 0)), vec),
        compiler_params=_cparams(("arbitrary",)),
    )(x2, g.reshape(1, D), target)


def _lru_gates(xc, wa_ref, ba_ref, wx_ref, bx_ref, lam_ref):
    xcb = xc.astype(BF16)
    r = jax.nn.sigmoid(jnp.dot(xcb, wa_ref[...], preferred_element_type=F32) + ba_ref[...])
    i = jax.nn.sigmoid(jnp.dot(xcb, wx_ref[...], preferred_element_type=F32) + bx_ref[...])
    sp = jax.nn.softplus(-lam_ref[...])
    log_a = (-LRU_C * sp) * r
    a = jnp.exp(log_a)
    s = jnp.sqrt(_neg_expm1(2.0 * log_a))
    return xcb, r, i, a, s


def _lru_fwd(p, conv_w, conv_b, wa_bd, ba, wx_bd, bx, lam, *, name):
    T = p.shape[0]
    d = lam.shape[-1]
    C = _tile(d, C_LRU)
    nC = d // C
    tb = _tile(T, TB, HALO)
    nT, hb, nt = T // tb, tb // HALO, tb // SUB

    def body(x_ref, xh_ref, g_ref, cw_ref, cb_ref, wa_ref, ba_ref, wx_ref, bx_ref, lam_ref,
             hs_ref, y_ref, a_s, u_s, h_s):
        t = pl.program_id(1)

        @pl.when(t == 0)
        def _():
            h_s[...] = jnp.zeros_like(h_s)

        x = x_ref[...].astype(F32)
        xc = _conv(x, _prev8(xh_ref, t), cw_ref) + cb_ref[...]
        _, r, i, a, s = _lru_gates(xc, wa_ref, ba_ref, wx_ref, bx_ref, lam_ref)
        a_s[...] = a
        u_s[...] = s * (i * xc)
        row = lax.broadcasted_iota(jnp.int32, (SUB, C), 0)

        def step(k, h):
            o = pl.multiple_of(k * SUB, SUB)
            A = a_s[pl.ds(o, SUB), :]
            B = u_s[pl.ds(o, SUB), :]
            for sh in (1, 2, 4):
                m = row >= sh
                Ap = pltpu.roll(A, sh, 0)
                Bp = pltpu.roll(B, sh, 0)
                B = jnp.where(m, A * Bp + B, B)
                A = jnp.where(m, A * Ap, A)
            hs = A * h + B
            hs_ref[pl.ds(o, SUB), :] = hs
            return jnp.broadcast_to(hs[SUB - 1:SUB, :], (SUB, C))

        h_s[...] = lax.fori_loop(0, nt, step, h_s[...])
        gel, _ = _gelu_and_grad(g_ref[...].astype(F32))
        y_ref[...] = (gel * hs_ref[...]).astype(BF16)

    vec = pl.BlockSpec((1, C), lambda c, t: (0, c))
    sq = pl.BlockSpec((None, C, C), lambda c, t: (c, 0, 0))
    return pl.pallas_call(
        body, name=name, out_shape=(_S((T, d), F32), _S((T, d), BF16)),
        grid=(nC, nT),
        in_specs=[pl.BlockSpec((tb, C), lambda c, t: (t, c)),
                  pl.BlockSpec((HALO, C), _halo_prev_map(hb, lambda c: c)),
                  pl.BlockSpec((tb, C), lambda c, t: (t, nC + c)),
                  pl.BlockSpec((conv_w.shape[0], C), lambda c, t: (0, c)),
                  vec, sq, vec, sq, vec, vec],
        out_specs=(pl.BlockSpec((tb, C), lambda c, t: (t, c)), pl.BlockSpec((tb, C), lambda c, t: (t, c))),
        scratch_shapes=[pltpu.VMEM((tb, C), F32), pltpu.VMEM((tb, C), F32), pltpu.VMEM((SUB, C), F32)],
        compiler_params=_cparams(("parallel", "arbitrary")),
    )(p, p, p, conv_w, conv_b, wa_bd, ba, wx_bd, bx, lam)


def _lru_bwd(p, hs, dyl, dp, conv_w, conv_b, wa_bd, ba, wx_bd, bx, lam, *, name, tasks=()):
    T = p.shape[0]
    d = lam.shape[-1]
    C = _tile(d, C_LRU)
    nC = d // C
    tb = _tile(T, TB, HALO)
    nT, hb, nt = T // tb, tb // HALO, tb // SUB
    kw = conv_w.shape[0]

    def body(x_ref, xh_ref, g_ref, hs_ref, hh_ref, dy_ref, cw_ref, cb_ref, wa_ref, ba_ref, wx_ref, bx_ref,
             lam_ref, dp_in, dp_ref, dcw_ref, dcb_ref, dwa_ref, dba_ref, dwx_ref, dbx_ref, dlam_ref,
             b_s, g_s, dh_s, an_s, dhn_s, dxn_s, st_x, st_g, sems):
        del dp_in
        c = pl.program_id(0)
        tr = pl.program_id(1)
        t = nT - 1 - tr
        first = tr == 0

        @pl.when(first)
        def _():
            an_s[...] = jnp.zeros_like(an_s)
            dhn_s[...] = jnp.zeros_like(dhn_s)
            dxn_s[...] = jnp.zeros_like(dxn_s)

        x = x_ref[...].astype(F32)
        xprev = _prev8(xh_ref, t)
        xc = _conv(x, xprev, cw_ref) + cb_ref[...]
        xcb, r, i, a, s = _lru_gates(xc, wa_ref, ba_ref, wx_ref, bx_ref, lam_ref)
        hsv = hs_ref[...]
        dy = dy_ref[...].astype(F32)
        gel, dgel = _gelu_and_grad(g_ref[...].astype(F32))
        st_g[...] = (dy * hsv * dgel).astype(BF16)

        b_s[...] = _up(a, an_s[...], 1)
        g_s[...] = dy * gel
        row = lax.broadcasted_iota(jnp.int32, (SUB, C), 0)

        def step(k, carry):
            o = pl.multiple_of((nt - 1 - k) * SUB, SUB)
            B = b_s[pl.ds(o, SUB), :]
            G = g_s[pl.ds(o, SUB), :]
            for sh in (1, 2, 4):
                m = row < SUB - sh
                Bn = pltpu.roll(B, SUB - sh, 0)
                Gn = pltpu.roll(G, SUB - sh, 0)
                G = jnp.where(m, B * Gn + G, G)
                B = jnp.where(m, B * Bn, B)
            dh = B * carry + G
            dh_s[pl.ds(o, SUB), :] = dh
            return jnp.broadcast_to(dh[0:1, :], (SUB, C))

        dhn_s[...] = lax.fori_loop(0, nt, step, dhn_s[...])
        an_s[...] = a[:SUB, :]
        dh = dh_s[...]

        hprev = _down(hsv, jnp.where(t > 0, hh_ref[...][HALO - SUB:, :], 0.0), 1)
        d_a = dh * hprev
        ixc = i * xc
        d_s = dh * ixc
        d_i = dh * s * xc
        d_xc = dh * s * i
        d_l = d_a * a - d_s * (a * a) / s
        sp = jax.nn.softplus(-lam_ref[...])
        _acc_row(dlam_ref, d_l * r * (LRU_C * jax.nn.sigmoid(-lam_ref[...])), first)
        d_zr = (d_l * (-LRU_C * sp)) * r * (1.0 - r)
        d_zi = d_i * i * (1.0 - i)
        _acc_row(dba_ref, d_zr, first)
        _acc_row(dbx_ref, d_zi, first)
        d_zrb = d_zr.astype(BF16)
        d_zib = d_zi.astype(BF16)
        tn_dims = (((0,), (0,)), ((), ()))
        nt_dims = (((1,), (1,)), ((), ()))
        gwa = lax.dot_general(xcb, d_zrb, tn_dims, preferred_element_type=F32)
        gwx = lax.dot_general(xcb, d_zib, tn_dims, preferred_element_type=F32)
        _acc(dwa_ref, gwa, first)
        _acc(dwx_ref, gwx, first)
        d_xc = (d_xc + lax.dot_general(d_zrb, wa_ref[...], nt_dims, preferred_element_type=F32)
                + lax.dot_general(d_zib, wx_ref[...], nt_dims, preferred_element_type=F32))
        _acc_row(dcb_ref, d_xc, first)
        _conv_dw(dcw_ref, d_xc, x, xprev, first)
        st_x[...] = _conv_t(d_xc, dxn_s[...], cw_ref).astype(BF16)
        dxn_s[...] = d_xc[:SUB, :]

        cx = _store_block(st_x, dp_ref, sems.at[0], t * tb, c * C)
        cg = _store_block(st_g, dp_ref, sems.at[1], t * tb, d + c * C)
        cx.start()
        cg.start()
        cx.wait()
        cg.wait()

    rev = lambda c, tr: (nT - 1 - tr, c)
    vec = pl.BlockSpec((1, C), lambda c, tr: (0, c))
    sq = pl.BlockSpec((None, C, C), lambda c, tr: (c, 0, 0))
    cwb = pl.BlockSpec((kw, C), lambda c, tr: (0, c))
    halo_prev = lambda c, tr: (jnp.maximum((nT - 1 - tr) * hb - 1, 0), c)
    outs, passed = _call(
        name, (nC, nT), body,
        [pl.BlockSpec((tb, C), rev),
         pl.BlockSpec((HALO, C), halo_prev),
         pl.BlockSpec((tb, C), lambda c, tr: (nT - 1 - tr, nC + c)),
         pl.BlockSpec((tb, C), rev),
         pl.BlockSpec((HALO, C), halo_prev),
         pl.BlockSpec((tb, C), rev),
         cwb, vec, sq, vec, sq, vec, vec, ANY],
        [p, p, p, hs, hs, dyl, conv_w, conv_b, wa_bd, ba, wx_bd, bx, lam, dp],
        [_S(dp.shape, dp.dtype), _S((kw, d), F32), _S((1, d), F32), _S((nC, C, C), F32), _S((1, d), F32),
         _S((nC, C, C), F32), _S((1, d), F32), _S((1, d), F32)],
        [ANY, cwb, vec, sq, vec, sq, vec, vec],
        [pltpu.VMEM((tb, C), F32), pltpu.VMEM((tb, C), F32), pltpu.VMEM((tb, C), F32),
         pltpu.VMEM((SUB, C), F32), pltpu.VMEM((SUB, C), F32), pltpu.VMEM((SUB, C), F32),
         pltpu.VMEM((tb, C), BF16), pltpu.VMEM((tb, C), BF16), pltpu.SemaphoreType.DMA((2,))],
        tasks, own_aliases={13: 0})
    return (*outs, passed) if tasks else outs


def _sc_fwd(p, conv_w, *, d, name):
    T = p.shape[0]
    C = _tile(d, C_EW)
    nC = d // C
    tb = _tile(T, TB, HALO)
    nT, hb = T // tb, tb // HALO

    def body(b_ref, c_ref, ch_ref, v_ref, vh_ref, w_ref, y_ref):
        t = pl.program_id(1)
        cv = c_ref[...].astype(F32) * v_ref[...].astype(F32)
        cvp = _prev8(ch_ref, t) * _prev8(vh_ref, t)
        y_ref[...] = (b_ref[...].astype(F32) * _conv(cv, cvp, w_ref)).astype(BF16)

    seg = lambda k: pl.BlockSpec((tb, C), lambda c, t: (t, k * nC + c))
    hseg = lambda k: pl.BlockSpec((HALO, C), _halo_prev_map(hb, lambda c: k * nC + c))
    return pl.pallas_call(
        body, name=name, out_shape=_S((T, d), BF16), grid=(nC, nT),
        in_specs=[seg(2), seg(3), hseg(3), seg(4), hseg(4), pl.BlockSpec((conv_w.shape[0], C), lambda c, t: (0, c))],
        out_specs=pl.BlockSpec((tb, C), lambda c, t: (t, c)),
        compiler_params=_cparams(("parallel", "parallel")),
    )(p, p, p, p, p, conv_w)


def _sc_bwd(p, dys, dp, conv_w, *, d, name):
    T = p.shape[0]
    C = _tile(d, C_EW)
    nC = d // C
    tb = _tile(T, TB, HALO)
    nT, hb = T // tb, tb // HALO
    kw = conv_w.shape[0]

    def body(b_ref, bn_ref, c_ref, ch_ref, v_ref, vh_ref, dy_ref, dyn_ref, w_ref, dp_in, dp_ref, dw_ref,
             st_b, st_c, st_v, sems):
        del dp_in
        c = pl.program_id(0)
        t = pl.program_id(1)
        last = t == nT - 1
        bv = b_ref[...].astype(F32)
        cvv = c_ref[...].astype(F32)
        vv = v_ref[...].astype(F32)
        dy = dy_ref[...].astype(F32)
        cv = cvv * vv
        cvp = _prev8(ch_ref, t) * _prev8(vh_ref, t)
        st_b[...] = (dy * _conv(cv, cvp, w_ref)).astype(BF16)
        dz = dy * bv
        dzn = _next8(dyn_ref, last) * _next8(bn_ref, last)
        _conv_dw(dw_ref, dz, cv, cvp, t == 0)
        dcv = _conv_t(dz, dzn, w_ref)
        st_c[...] = (dcv * vv).astype(BF16)
        st_v[...] = (dcv * cvv).astype(BF16)
        cps = [_store_block(st, dp_ref, sems.at[k], t * tb, (2 + k) * d + c * C)
               for k, st in enumerate((st_b, st_c, st_v))]
        for cp in cps:
            cp.start()
        for cp in cps:
            cp.wait()

    seg = lambda k: pl.BlockSpec((tb, C), lambda c, t: (t, k * nC + c))
    hseg = lambda k: pl.BlockSpec((HALO, C), _halo_prev_map(hb, lambda c: k * nC + c))
    last_h = T // HALO - 1
    nseg = lambda k: pl.BlockSpec((HALO, C), lambda c, t: (jnp.minimum((t + 1) * hb, last_h), k * nC + c))
    return pl.pallas_call(
        body, name=name, out_shape=(_S(dp.shape, dp.dtype), _S((kw, d), F32)), grid=(nC, nT),
        in_specs=[seg(2), nseg(2), seg(3), hseg(3), seg(4), hseg(4),
                  pl.BlockSpec((tb, C), lambda c, t: (t, c)), nseg(0),
                  pl.BlockSpec((kw, C), lambda c, t: (0, c)), ANY],
        out_specs=(ANY, pl.BlockSpec((kw, C), lambda c, t: (0, c))),
        scratch_shapes=[pltpu.VMEM((tb, C), BF16)] * 3 + [pltpu.SemaphoreType.DMA((3,))],
        input_output_aliases={9: 0},
        compiler_params=_cparams(("arbitrary", "arbitrary")),
    )(p, p, p, p, p, p, dys, dys, conv_w, dp)


def _merge_fwd(p, y_lru, y_sc, *, col0, name):
    T, D = y_lru.shape
    C = _tile(math.gcd(D, col0), 1024)
    nC = D // C
    k0 = col0 // C
    tb = _tile(T, 256, HALO)

    def body(gl_ref, gs_ref, yl_ref, ys_ref, o_ref):
        o_ref[...] = (jax.nn.sigmoid(gl_ref[...].astype(F32)) * yl_ref[...].astype(F32)
                      + jax.nn.sigmoid(gs_ref[...].astype(F32)) * ys_ref[...].astype(F32)).astype(BF16)

    blk = pl.BlockSpec((tb, C), lambda c, t: (t, c))
    return pl.pallas_call(
        body, name=name, out_shape=_S((T, D), BF16), grid=(nC, T // tb),
        in_specs=[pl.BlockSpec((tb, C), lambda c, t: (t, k0 + c)),
                  pl.BlockSpec((tb, C), lambda c, t: (t, k0 + nC + c)), blk, blk],
        out_specs=blk, compiler_params=_cparams(("parallel", "parallel")),
    )(p, p, y_lru, y_sc)


def _merge_bwd(p, y_lru, y_sc, dm, *, col0, name):
    T, D = y_lru.shape
    C = _tile(math.gcd(D, col0), 1024)
    nC = D // C
    k0 = col0 // C
    tb = _tile(T, 256, HALO)

    def body(gl_ref, gs_ref, yl_ref, ys_ref, dm_ref, dp_ref, dyl_ref, dys_ref, st_l, st_s, sems):
        c = pl.program_id(0)
        t = pl.program_id(1)
        dmv = dm_ref[...].astype(F32)
        sl = jax.nn.sigmoid(gl_ref[...].astype(F32))
        ss = jax.nn.sigmoid(gs_ref[...].astype(F32))
        dyl_ref[...] = (dmv * sl).astype(BF16)
        dys_ref[...] = (dmv * ss).astype(BF16)
        st_l[...] = (dmv * yl_ref[...].astype(F32) * sl * (1.0 - sl)).astype(BF16)
        st_s[...] = (dmv * ys_ref[...].astype(F32) * ss * (1.0 - ss)).astype(BF16)
        cl = _store_block(st_l, dp_ref, sems.at[0], t * tb, col0 + c * C)
        cs = _store_block(st_s, dp_ref, sems.at[1], t * tb, col0 + D + c * C)
        cl.start()
        cs.start()
        cl.wait()
        cs.wait()

    blk = pl.BlockSpec((tb, C), lambda c, t: (t, c))
    return pl.pallas_call(
        body, name=name, out_shape=(_S(p.shape, BF16), _S((T, D), BF16), _S((T, D), BF16)),
        grid=(nC, T // tb),
        in_specs=[pl.BlockSpec((tb, C), lambda c, t: (t, k0 + c)),
                  pl.BlockSpec((tb, C), lambda c, t: (t, k0 + nC + c)), blk, blk, blk],
        out_specs=(ANY, blk, blk),
        scratch_shapes=[pltpu.VMEM((tb, C), BF16), pltpu.VMEM((tb, C), BF16), pltpu.SemaphoreType.DMA((2,))],
        compiler_params=_cparams(("arbitrary", "arbitrary")),
    )(p, p, y_lru, y_sc, dm)


def _ffn_act_fwd(uu, conv_w, *, name, tasks=()):
    T = uu.shape[0]
    F = uu.shape[1] // 2
    C = _tile(F, C_EW)
    nC = F // C
    tb = _tile(T, 256, HALO)
    nT, hb = T // tb, tb // HALO
    kw = conv_w.shape[0]

    def body(g_ref, gh_ref, v_ref, vh_ref, wg_ref, wv_ref, o_ref):
        t = pl.program_id(1)
        cg = _conv(g_ref[...].astype(F32), _prev8(gh_ref, t), wg_ref)
        cv = _conv(v_ref[...].astype(F32), _prev8(vh_ref, t), wv_ref)
        o_ref[...] = (cg * jax.nn.sigmoid(cg) * cv).astype(BF16)

    seg = lambda k: pl.BlockSpec((tb, C), lambda c, t: (t, k * nC + c))
    hseg = lambda k: pl.BlockSpec((HALO, C), _halo_prev_map(hb, lambda c: k * nC + c))
    wseg = lambda k: pl.BlockSpec((kw, C), lambda c, t: (0, k * nC + c))
    outs, passed = _call(
        name, (nC, nT), body, [seg(0), hseg(0), seg(1), hseg(1), wseg(0), wseg(1)],
        [uu, uu, uu, uu, conv_w, conv_w], [_S((T, F), BF16)], [pl.BlockSpec((tb, C), lambda c, t: (t, c))], [], tasks)
    return (outs[0], passed) if tasks else outs[0]


def _ffn_act_bwd(uu, dact, conv_w, *, name):
    T = uu.shape[0]
    F = uu.shape[1] // 2
    C = _tile(F, C_EW)
    nC = F // C
    tb = _tile(T, 256, HALO)
    nT, hb = T // tb, tb // HALO
    kw = conv_w.shape[0]

    def body(g_ref, gh_ref, v_ref, vh_ref, da_ref, wg_ref, wv_ref, du_ref, dwg_ref, dwv_ref,
             gn_s, vn_s, st_g, st_v, sems):
        c = pl.program_id(0)
        tr = pl.program_id(1)
        t = nT - 1 - tr
        first = tr == 0

        @pl.when(first)
        def _():
            gn_s[...] = jnp.zeros_like(gn_s)
            vn_s[...] = jnp.zeros_like(vn_s)

        ug = g_ref[...].astype(F32)
        uv = v_ref[...].astype(F32)
        ugp = _prev8(gh_ref, t)
        uvp = _prev8(vh_ref, t)
        cg = _conv(ug, ugp, wg_ref)
        cv = _conv(uv, uvp, wv_ref)
        da = da_ref[...].astype(F32)
        sg = jax.nn.sigmoid(cg)
        d_cg = da * cv * (sg * (1.0 + cg * (1.0 - sg)))
        d_cv = da * (cg * sg)
        _conv_dw(dwg_ref, d_cg, ug, ugp, first)
        _conv_dw(dwv_ref, d_cv, uv, uvp, first)
        st_g[...] = _conv_t(d_cg, gn_s[...], wg_ref).astype(BF16)
        st_v[...] = _conv_t(d_cv, vn_s[...], wv_ref).astype(BF16)
        gn_s[...] = d_cg[:SUB, :]
        vn_s[...] = d_cv[:SUB, :]
        cpg = _store_block(st_g, du_ref, sems.at[0], t * tb, c * C)
        cpv = _store_block(st_v, du_ref, sems.at[1], t * tb, F + c * C)
        cpg.start()
        cpv.start()
        cpg.wait()
        cpv.wait()

    seg = lambda k: pl.BlockSpec((tb, C), lambda c, tr: (nT - 1 - tr, k * nC + c))
    hseg = lambda k: pl.BlockSpec((HALO, C), lambda c, tr: (jnp.maximum((nT - 1 - tr) * hb - 1, 0), k * nC + c))
    wseg = lambda k: pl.BlockSpec((kw, C), lambda c, tr: (0, k * nC + c))
    dwb = pl.BlockSpec((kw, C), lambda c, tr: (0, c))
    return pl.pallas_call(
        body, name=name, out_shape=(_S(uu.shape, BF16), _S((kw, F), F32), _S((kw, F), F32)), grid=(nC, nT),
        in_specs=[seg(0), hseg(0), seg(1), hseg(1), pl.BlockSpec((tb, C), lambda c, tr: (nT - 1 - tr, c)),
                  wseg(0), wseg(1)],
        out_specs=(ANY, dwb, dwb),
        scratch_shapes=[pltpu.VMEM((SUB, C), F32), pltpu.VMEM((SUB, C), F32),
                        pltpu.VMEM((tb, C), BF16), pltpu.VMEM((tb, C), BF16), pltpu.SemaphoreType.DMA((2,))],
        compiler_params=_cparams(("arbitrary", "arbitrary")),
    )(uu, uu, uu, uu, dact, conv_w, conv_w)


def _place():
    x, y, c = lax.axis_index("x"), lax.axis_index("y"), lax.axis_index("c")
    return x, y, c


def _chips(x, y):
    return [(1 - x, y), (x, 1 - y), (1 - x, 1 - y)]


def _all_gather(arrays, placed, over_ici, name):
    n = len(arrays)

    def body(*refs):
        ins, outs = refs[:n], refs[n:2 * n]
        send_sems, recv_sems, local_sems = refs[2 * n:]
        x, y, c = _place()
        me, sibling = (x, y, c), (x, y, 1 - c)
        chips = _chips(x, y)
        full = [a for a in range(n) if over_ici[a]]

        def idx(px, py, pc):
            return 4 * px + 2 * py + pc

        def copy(a, k, block, to):
            dst = outs[a].at[idx(*block)]
            src = ins[a] if (block is me and not placed[a]) else dst
            return pltpu.make_async_remote_copy(
                src_ref=src, dst_ref=dst, send_sem=send_sems.at[a, k], recv_sem=recv_sems.at[a, k],
                device_id=to, device_id_type=MESH)

        mine = [pltpu.make_async_copy(ins[a], outs[a].at[idx(*me)], local_sems.at[a])
                for a in range(n) if not placed[a]]
        for cp in mine:
            cp.start()
        first = []
        for a in full:
            first += [copy(a, 1 + j, me, (*chip, c)) for j, chip in enumerate(chips)]
        for a in range(n):
            first.append(copy(a, 0, me, sibling))
        for cp in first:
            cp.start()
        passed = []
        for a in full:
            for j, chip in enumerate(chips):
                copy(a, 1 + j, (*chip, c), me).wait_recv()
                cp = copy(a, 4 + j, (*chip, c), sibling)
                cp.start()
                passed.append(cp)
        for a in range(n):
            copy(a, 0, sibling, me).wait_recv()
        for a in full:
            for j, chip in enumerate(chips):
                copy(a, 4 + j, (*chip, 1 - c), me).wait_recv()
        for cp in first + passed:
            cp.wait_send()
        for cp in mine:
            cp.wait()

    return pl.pallas_call(
        body, name=name,
        out_shape=tuple(_S(s.shape if placed[a] else (N_DEV,) + s.shape, s.dtype) for a, s in enumerate(arrays)),
        in_specs=[ANY] * n, out_specs=tuple([ANY] * n),
        scratch_shapes=[pltpu.SemaphoreType.DMA((n, 7)), pltpu.SemaphoreType.DMA((n, 7)),
                        pltpu.SemaphoreType.DMA((n,))],
        input_output_aliases={a: a for a in range(n) if placed[a]},
    )(*arrays)


def _rows_of(ref, blk, rows):
    v = ref.at[blk]
    return v if rows is None else v.at[pl.ds(rows[0], rows[1])]


ALL_ROWS = "all"


def _gather_task(buf, ici=None, fwd=None):
    rows = lambda r: None if r == ALL_ROWS else r

    def copies(refs, ss, rs):
        x, y, c = _place()
        me = 4 * x + 2 * y + c
        cps = []
        for j, (px, py) in enumerate(_chips(x, y)):
            if ici is not None:
                blk = _rows_of(refs[0], me, rows(ici))
                cps.append(pltpu.make_async_remote_copy(
                    src_ref=blk, dst_ref=blk, send_sem=ss.at[j], recv_sem=rs.at[j],
                    device_id=(px, py, c), device_id_type=MESH))
            if fwd is not None:
                blk = _rows_of(refs[0], 4 * px + 2 * py + c, rows(fwd))
                cps.append(pltpu.make_async_remote_copy(
                    src_ref=blk, dst_ref=blk, send_sem=ss.at[3 + j], recv_sem=rs.at[3 + j],
                    device_id=(x, y, 1 - c), device_id_type=MESH))
        return cps

    def start(refs, ss, rs, ls):
        for cp in copies(refs, ss, rs):
            cp.start()

    def wait(refs, ss, rs, ls):
        x, y, c = _place()
        for j, (px, py) in enumerate(_chips(x, y)):
            if ici is not None:
                blk = _rows_of(refs[0], 4 * px + 2 * py + c, rows(ici))
                pltpu.make_async_remote_copy(
                    src_ref=blk, dst_ref=blk, send_sem=ss.at[j], recv_sem=rs.at[j],
                    device_id=(px, py, c), device_id_type=MESH).wait_recv()
            if fwd is not None:
                blk = _rows_of(refs[0], 4 * px + 2 * py + 1 - c, rows(fwd))
                pltpu.make_async_remote_copy(
                    src_ref=blk, dst_ref=blk, send_sem=ss.at[3 + j], recv_sem=rs.at[3 + j],
                    device_id=(x, y, 1 - c), device_id_type=MESH).wait_recv()
        for cp in copies(refs, ss, rs):
            cp.wait_send()

    return _Task([buf], [0], start, wait, nsem=6)


def _exchange_task(parts, landing, rows=None):
    def copies(refs, ss, rs):
        x, y, c = _place()
        myq = 2 * x + y
        return [pltpu.make_async_remote_copy(
            src_ref=_rows_of(refs[0], 2 * px + py, rows), dst_ref=_rows_of(refs[1], myq, rows),
            send_sem=ss.at[k], recv_sem=rs.at[k], device_id=(px, py, c), device_id_type=MESH)
            for k, (px, py) in enumerate(_chips(x, y))]

    def start(refs, ss, rs, ls):
        for cp in copies(refs, ss, rs):
            cp.start()

    def wait(refs, ss, rs, ls):
        x, y, c = _place()
        for k, (px, py) in enumerate(_chips(x, y)):
            pltpu.make_async_remote_copy(
                src_ref=_rows_of(refs[0], 2 * x + y, rows), dst_ref=_rows_of(refs[1], 2 * px + py, rows),
                send_sem=ss.at[k], recv_sem=rs.at[k], device_id=(px, py, c), device_id_type=MESH).wait_recv()
        for cp in copies(refs, ss, rs):
            cp.wait_send()

    return _Task([parts, landing], [1], start, wait)


def _swap_task(g):
    g4 = g.reshape((N_CHIP, 2) + g.shape[1:])

    def copy(refs, ss, rs):
        x, y, c = _place()
        return pltpu.make_async_remote_copy(
            src_ref=refs[0].at[:, 1 - c], dst_ref=refs[1], send_sem=ss.at[0], recv_sem=rs.at[0],
            device_id=(x, y, 1 - c), device_id_type=MESH)

    def start(refs, ss, rs, ls):
        copy(refs, ss, rs).start()

    def wait(refs, ss, rs, ls):
        copy(refs, ss, rs).wait()

    return _Task([g4], [], start, wait, fresh=[_S((N_CHIP,) + g.shape[1:], g.dtype)], nsem=1)


def _peer(x, y, c, m):
    return x ^ (m >> 2), y ^ ((m >> 1) & 1), c ^ (m & 1)


def _bcast_task(pack):
    def copies(refs, ss, rs):
        x, y, c = _place()
        me = 4 * x + 2 * y + c
        return [pltpu.make_async_remote_copy(
            src_ref=refs[0], dst_ref=refs[1].at[me], send_sem=ss.at[m - 1], recv_sem=rs.at[m - 1],
            device_id=_peer(x, y, c, m), device_id_type=MESH) for m in range(1, N_DEV)]

    def local(refs, ls):
        x, y, c = _place()
        return pltpu.make_async_copy(refs[0], refs[1].at[4 * x + 2 * y + c], ls.at[0])

    def start(refs, ss, rs, ls):
        local(refs, ls).start()
        for cp in copies(refs, ss, rs):
            cp.start()

    def wait(refs, ss, rs, ls):
        x, y, c = _place()
        for m in range(1, N_DEV):
            px, py, pc = _peer(x, y, c, m)
            pltpu.make_async_remote_copy(
                src_ref=refs[0], dst_ref=refs[1].at[4 * px + 2 * py + pc], send_sem=ss.at[m - 1],
                recv_sem=rs.at[m - 1], device_id=(px, py, pc), device_id_type=MESH).wait_recv()
        for cp in copies(refs, ss, rs):
            cp.wait_send()
        local(refs, ls).wait()

    return _Task([pack], [], start, wait, fresh=[_S((N_DEV,) + pack.shape, pack.dtype)], nsem=N_DEV - 1)


def _sum_packs(packs, name):
    _, R, L = packs.shape

    def body(p_ref, o_ref):
        acc = p_ref[0]
        for k in range(1, N_DEV):
            acc = acc + p_ref[k]
        o_ref[...] = acc

    return pl.pallas_call(body, name=name, out_shape=_S((R, L), packs.dtype), in_specs=[VMEM_SPEC],
                          out_specs=VMEM_SPEC, compiler_params=_cparams())(packs)


def _swap_halves(grads, name):
    n = len(grads)
    g4 = [g.reshape((N_CHIP, 2) + g.shape[1:]) for g in grads]

    def body(*refs):
        ins, outs = refs[:n], refs[n:2 * n]
        send_sems, recv_sems = refs[2 * n:]
        x, y, c = _place()
        cps = [pltpu.make_async_remote_copy(
            src_ref=ins[a].at[:, 1 - c], dst_ref=outs[a],
            send_sem=send_sems.at[a], recv_sem=recv_sems.at[a],
            device_id=(x, y, 1 - c), device_id_type=MESH) for a in range(n)]
        for cp in cps:
            cp.start()
        for cp in cps:
            cp.wait()

    return pl.pallas_call(
        body, name=name,
        out_shape=tuple(_S((N_CHIP,) + g.shape[1:], g.dtype) for g in grads),
        in_specs=[ANY] * n, out_specs=tuple([ANY] * n),
        scratch_shapes=[pltpu.SemaphoreType.DMA((n,)), pltpu.SemaphoreType.DMA((n,))],
    )(*g4)


def _add_halves(g, landed, place, name):
    _, r, cc = g.shape
    g4 = g.reshape(N_CHIP, 2, r, cc)
    tr = _tile(r, 512, HALO)

    def body(s_ref, g_ref, l_ref, o_ref, land_ref):
        q = pl.program_id(1)
        v = (g_ref[...].astype(F32) + l_ref[...].astype(F32)).astype(BF16)
        o_ref[...] = v

        @pl.when(q == s_ref[1])
        def _():
            land_ref[...] = v

    return pl.pallas_call(
        body, name=name, out_shape=(_S((N_CHIP, r, cc), BF16), _S((N_CHIP, r, cc), BF16)),
        grid_spec=pltpu.PrefetchScalarGridSpec(
            num_scalar_prefetch=1, grid=(r // tr, N_CHIP),
            in_specs=[pl.BlockSpec((None, None, tr, cc), lambda i, q, s: (q, s[0], i, 0)),
                      pl.BlockSpec((None, tr, cc), lambda i, q, s: (q, i, 0))],
            out_specs=(pl.BlockSpec((None, tr, cc), lambda i, q, s: (q, i, 0)),
                       pl.BlockSpec((None, tr, cc), lambda i, q, s: (s[1], i, 0)))),
        compiler_params=_cparams(("arbitrary", "arbitrary")),
    )(place, g4, landed)


def _all_reduce_small(pack, name):
    R = pack.shape[0]

    def body(p_ref, o_ref, buf, send_sems, recv_sems):
        x, y, c = _place()
        me = 4 * x + 2 * y + c
        buf[me] = p_ref[...]
        cps = []
        for k in range(N_DEV - 1):
            m = k + 1
            peer = (x ^ (m >> 2), y ^ ((m >> 1) & 1), c ^ (m & 1))
            cps.append(pltpu.make_async_remote_copy(
                src_ref=p_ref, dst_ref=buf.at[me], send_sem=send_sems.at[k], recv_sem=recv_sems.at[k],
                device_id=peer, device_id_type=MESH))
        for cp in cps:
            cp.start()
        for k in range(N_DEV - 1):
            m = k + 1
            peer_idx = 4 * (x ^ (m >> 2)) + 2 * (y ^ ((m >> 1) & 1)) + (c ^ (m & 1))
            pltpu.make_async_remote_copy(
                src_ref=p_ref, dst_ref=buf.at[peer_idx], send_sem=send_sems.at[k], recv_sem=recv_sems.at[k],
                device_id=(x, y, c), device_id_type=MESH).wait_recv()
        for cp in cps:
            cp.wait_send()
        acc = buf[0]
        for k in range(1, N_DEV):
            acc = acc + buf[k]
        o_ref[...] = acc

    return pl.pallas_call(
        body, name=name, out_shape=_S((R, LANES), F32),
        in_specs=[VMEM_SPEC], out_specs=VMEM_SPEC,
        scratch_shapes=[pltpu.VMEM((N_DEV, R, LANES), F32), pltpu.SemaphoreType.DMA((N_DEV - 1,)),
                        pltpu.SemaphoreType.DMA((N_DEV - 1,))],
        compiler_params=_cparams(),
    )(pack)


def _adamw_math(w, g, m, v):
    m = ADAM_B1 * m + (1.0 - ADAM_B1) * g
    v = ADAM_B2 * v + (1.0 - ADAM_B2) * (g * g)
    m_hat = m / (1.0 - ADAM_B1 ** ADAM_STEP)
    v_hat = v / (1.0 - ADAM_B2 ** ADAM_STEP)
    delta = -ADAM_LR * (m_hat / (jnp.sqrt(v_hat) + ADAM_EPS) + ADAM_WD * w)
    return delta, m, v


def _adamw_big(parts, w, m, v, name):
    r, cc = w.shape
    tr = _tile(r, 128, HALO)

    def body(p_ref, w_ref, m_ref, v_ref, g_ref, d_ref, nm_ref, nv_ref):
        g = p_ref[0].astype(F32)
        for q in range(1, N_CHIP):
            g = g + p_ref[q].astype(F32)
        g_ref[...] = g
        d_ref[...], nm_ref[...], nv_ref[...] = _adamw_math(w_ref[...], g, m_ref[...], v_ref[...])

    blk = pl.BlockSpec((tr, cc), lambda i: (i, 0))
    return pl.pallas_call(
        body, name=name, out_shape=tuple(_S((r, cc), F32) for _ in range(4)), grid=(r // tr,),
        in_specs=[pl.BlockSpec((N_CHIP, tr, cc), lambda i: (0, i, 0)), blk, blk, blk],
        out_specs=(blk, blk, blk, blk), compiler_params=_cparams(("parallel",)),
    )(parts, w, m, v)


def _adamw_small(ws, gs, ms, vs, name):
    n = len(ws)

    def body(*refs):
        w_r, g_r, m_r, v_r = refs[:n], refs[n:2 * n], refs[2 * n:3 * n], refs[3 * n:4 * n]
        d_r, nm_r, nv_r = refs[4 * n:5 * n], refs[5 * n:6 * n], refs[6 * n:7 * n]
        for k in range(n):
            d_r[k][...], nm_r[k][...], nv_r[k][...] = _adamw_math(w_r[k][...], g_r[k][...], m_r[k][...], v_r[k][...])

    shapes = tuple(_S(w.shape, F32) for w in ws)
    outs = pl.pallas_call(
        body, name=name, out_shape=shapes * 3,
        in_specs=[VMEM_SPEC] * (4 * n), out_specs=tuple([VMEM_SPEC] * (3 * n)),
        compiler_params=_cparams(),
    )(*ws, *gs, *ms, *vs)
    return outs[:n], outs[n:2 * n], outs[2 * n:]


def _block_diag(w, heads_per_block):
    H, hd, _ = w.shape
    nb = H // heads_per_block
    eye = jnp.eye(heads_per_block, dtype=w.dtype)
    w4 = w.reshape(nb, heads_per_block, hd, hd)
    return jnp.einsum("nhab,hg->nhagb", w4, eye).reshape(nb, heads_per_block * hd, heads_per_block * hd)


def _diag_blocks(bd, heads_per_block, hd):
    nb = bd.shape[0]
    b5 = bd.reshape(nb, heads_per_block, hd, heads_per_block, hd)
    return jnp.stack([b5[:, h, :, h, :] for h in range(heads_per_block)], axis=1).reshape(nb * heads_per_block, hd, hd)


def _as_rows(a):
    if a.ndim == 1:
        return a.reshape(-1, LANES) if a.shape[0] % LANES == 0 else a.reshape(1, -1)
    if a.ndim == 3:
        return a.reshape(-1, LANES) if (a.size % LANES == 0) else a.reshape(a.shape[0] * a.shape[1], a.shape[2])
    return a


def kernel(x, g_mix, w_in, lru_conv_w, lru_conv_b, lru_wa, lru_ba, lru_wx, lru_bx, lru_lambda, lru_w_out, sc_conv_w, sc_w_out, w_o, g_ffn, ffn_w_up, ffn_conv_w, ffn_w_down, g_final, loss_target, m_g_mix, m_w_in, m_lru_conv_w, m_lru_conv_b, m_lru_wa, m_lru_ba, m_lru_wx, m_lru_bx, m_lru_lambda, m_lru_w_out, m_sc_conv_w, m_sc_w_out, m_w_o, m_g_ffn, m_ffn_w_up, m_ffn_conv_w, m_ffn_w_down, m_g_final, v_g_mix, v_w_in, v_lru_conv_w, v_lru_conv_b, v_lru_wa, v_lru_ba, v_lru_wx, v_lru_bx, v_lru_lambda, v_lru_w_out, v_sc_conv_w, v_sc_w_out, v_w_o, v_g_ffn, v_ffn_w_up, v_ffn_conv_w, v_ffn_w_down, v_g_final):
    T, D = x.shape[1], x.shape[2]
    d_lru = lru_lambda.shape[0]
    d_sc = sc_conv_w.shape[1] * N_DEV
    F = ffn_w_down.shape[0] * N_DEV
    H = lru_wa.shape[0]
    assert d_lru == d_sc and H * HEAD_DIM == d_lru
    xs = x.reshape(T, D)
    tgt = loss_target.reshape(T, D)
    my_x, my_y, my_c = _place()
    me = 4 * my_x + 2 * my_y + my_c

    big = [w_in, lru_w_out, sc_w_out, w_o, ffn_w_up, ffn_w_down]
    big_names = ["w_in", "lru_w_out", "sc_w_out", "w_o", "ffn_w_up", "ffn_w_down"]
    me_idx = jnp.reshape(me, (1,)).astype(jnp.int32)
    big_bf = [_cast_into_slot(w, me_idx, "cast_" + nm) for w, nm in zip(big, big_names)]
    pad_rows = lambda a: jnp.pad(a, ((0, SUB - a.shape[0]), (0, 0)))
    gathered = _all_gather(big_bf + [pad_rows(lru_conv_w), pad_rows(sc_conv_w), pad_rows(ffn_conv_w)],
                           [True] * 6 + [False] * 3,
                           [True, True, True, False, False, False, True, True, True], "all_gather_first")
    W_in, W_lo, W_so, W_o8, W_up, W_dn8 = gathered[:6]
    full_cols = lambda g, kw: g[:, :kw, :].transpose(1, 0, 2).reshape(kw, -1)
    cw_lru = full_cols(gathered[6], lru_conv_w.shape[0])
    cw_sc = full_cols(gathered[7], sc_conv_w.shape[0])
    cw_ffn = full_cols(gathered[8], ffn_conv_w.shape[0])

    C = _tile(d_lru, C_LRU)
    hpb = C // HEAD_DIM
    wa_bd = _block_diag(lru_wa, hpb).astype(BF16)
    wx_bd = _block_diag(lru_wx, hpb).astype(BF16)
    cb, ba, bx, lam = (a.reshape(1, d_lru) for a in (lru_conv_b, lru_ba, lru_bx, lru_lambda))

    h1 = _rms_fwd(xs, g_mix, "rms_mix")
    kq = W_up.shape[1] // 4
    p, ((W_o8,), (W_up,)) = _mm_nn(
        h1, W_in, out_dtype=BF16, name="mm_in",
        tasks=[_gather_task(W_o8, ici=ALL_ROWS), _gather_task(W_up, ici=(0, 2 * kq))])
    hs, yl_pre = _lru_fwd(p, cw_lru, cb, wa_bd, ba, wx_bd, bx, lam, name="lru_fwd")
    ys_pre = _sc_fwd(p, cw_sc, d=d_sc, name="sc_fwd")
    y_lru, ((W_o8,), (W_up,)) = _mm_nn(
        yl_pre, W_lo, out_dtype=BF16, name="mm_lru_out", tm=2048,
        tasks=[_gather_task(W_o8, fwd=ALL_ROWS), _gather_task(W_up, ici=(2 * kq, kq), fwd=(0, 2 * kq))])
    y_sc, ((W_up,),) = _mm_nn(ys_pre, W_so, out_dtype=BF16, name="mm_sc_out", tm=2048,
                              tasks=[_gather_task(W_up, ici=(3 * kq, kq))])
    gate0 = 2 * d_lru + 3 * d_sc
    merged = _merge_fwd(p, y_lru, y_sc, col0=gate0, name="merge_fwd")
    W_o = W_o8.reshape(1, D, D)
    x1, ((W_up,),) = _mm_nn(merged, W_o, out_dtype=F32, residual=xs, name="mm_o",
                            tasks=[_gather_task(W_up, fwd=(2 * kq, 2 * kq))])
    h2 = _rms_fwd(x1, g_ffn, "rms_ffn")
    uu, ((W_dn8,),) = _mm_nn(h2, W_up, out_dtype=BF16, name="mm_up", tasks=[_gather_task(W_dn8, ici=ALL_ROWS)])
    act, ((W_dn8,),) = _ffn_act_fwd(uu, cw_ffn, name="ffn_act_fwd", tasks=[_gather_task(W_dn8, fwd=ALL_ROWS)])
    W_dn = W_dn8.reshape(1, F, D)
    x2 = _mm_nn(act, W_dn, out_dtype=F32, residual=x1, name="mm_down", tn=512, tk=F)
    dx2, dx2b, loss_part, dg_final = _loss_head(x2, g_final, tgt, "loss_head")

    place = jnp.stack([my_c, 2 * my_x + my_y]).astype(jnp.int32)

    def pack_rows(arrs):
        flat = jnp.concatenate([a.reshape(-1) for a in arrs])
        rows = -(-flat.shape[0] // (SUB * LANES)) * SUB
        return jnp.pad(flat, (0, rows * LANES - flat.shape[0])).reshape(rows, LANES)

    def unpack_rows(pack, arrs):
        flat, out, o = pack.reshape(-1), [], 0
        for a in arrs:
            out.append(flat[o:o + a.size].reshape(a.shape))
            o += a.size
        return out

    dact = _mm_nt(dx2b, W_dn, out_dtype=BF16, name="mm_down_dx", tko=1408)
    gW_dn = _mm_tn(act, dx2b, 1, out_dtype=BF16, name="mm_down_dw").reshape(N_DEV, F // N_DEV, D)
    duu, dcw_ffn_g, dcw_ffn_v = _ffn_act_bwd(uu, dact, cw_ffn, name="ffn_act_bwd")
    dh2, ((land_dn,),) = _mm_nt(duu, W_up, out_dtype=BF16, name="mm_up_dx", tasks=[_swap_task(gW_dn)])
    parts_dn = _add_halves(gW_dn, land_dn, place, "rs_add_ffn_w_down")
    gW_up, ((mine_dn,),) = _mm_tn(h2, duu, N_DEV, out_dtype=BF16, name="mm_up_dw", tasks=[_exchange_task(*parts_dn)])
    dx1, dx1b, dg_ffn = _rms_bwd(x1, g_ffn, dh2, dx2, "rms_ffn_bwd")
    dmerged, ((land_up,),) = _mm_nt(dx1b, W_o, out_dtype=BF16, name="mm_o_dx", tasks=[_swap_task(gW_up)])
    parts_up = _add_halves(gW_up, land_up, place, "rs_add_ffn_w_up")
    gW_o = _mm_tn(merged, dx1b, 1, out_dtype=BF16, name="mm_o_dw").reshape(N_DEV, D // N_DEV, D)
    dp, dy_lru, dy_sc = _merge_bwd(p, y_lru, y_sc, dmerged, col0=gate0, name="merge_bwd")
    dyl_pre, ((land_o,),) = _mm_nt(dy_lru, W_lo, out_dtype=BF16, name="mm_lru_out_dx", tasks=[_swap_task(gW_o)])
    parts_o = _add_halves(gW_o, land_o, place, "rs_add_w_o")
    gW_lo, ((mine_o,),) = _mm_tn(yl_pre, dy_lru, N_DEV, out_dtype=BF16, name="mm_lru_out_dw",
                                 tasks=[_exchange_task(*parts_o)])
    dys_pre, ((land_lo,),) = _mm_nt(dy_sc, W_so, out_dtype=BF16, name="mm_sc_out_dx", tasks=[_swap_task(gW_lo)])
    parts_lo = _add_halves(gW_lo, land_lo, place, "rs_add_lru_w_out")
    gW_so, ((mine_lo,),) = _mm_tn(ys_pre, dy_sc, N_DEV, out_dtype=BF16, name="mm_sc_out_dw",
                                  tasks=[_exchange_task(*parts_lo)])
    dp, dcw_sc = _sc_bwd(p, dys_pre, dp, cw_sc, d=d_sc, name="sc_bwd")
    r_up = parts_up[0].shape[1] // 2
    dp, dcw_lru, dcb, dwa_bd, dba, dwx_bd, dbx, dlam, ((land_up,),) = _lru_bwd(
        p, hs, dyl_pre, dp, cw_lru, cb, wa_bd, ba, wx_bd, bx, lam, name="lru_bwd",
        tasks=[_exchange_task(*parts_up, rows=(0, r_up))])

    dwa = _diag_blocks(dwa_bd, hpb, HEAD_DIM)
    dwx = _diag_blocks(dwx_bd, hpb, HEAD_DIM)
    dcw_ffn = jnp.concatenate([dcw_ffn_g, dcw_ffn_v], axis=1)
    small_full = [dcw_lru, dcb, dwa, dba, dwx, dbx, dlam, dcw_sc, dg_ffn, dcw_ffn, dg_final]
    gW_in, ((mine_up,), (land_so,), (packs,)) = _mm_tn(
        h1, dp, N_DEV, out_dtype=BF16, name="mm_in_dw",
        tasks=[_exchange_task(parts_up[0], land_up, rows=(r_up, r_up)), _swap_task(gW_so),
               _bcast_task(pack_rows(small_full))])
    parts_so = _add_halves(gW_so, land_so, place, "rs_add_sc_w_out")
    (land_in,) = _swap_halves([gW_in], "rs_swap_w_in")
    parts_in = _add_halves(gW_in, land_in, place, "rs_add_w_in")
    dh1, ((mine_in,), (mine_so,)) = _mm_nt(dp, W_in, out_dtype=BF16, name="mm_in_dx",
                                           tasks=[_exchange_task(*parts_in), _exchange_task(*parts_so)])
    grad_x, _, dg_mix = _rms_bwd(xs, g_mix, dh1, dx1, "rms_mix_bwd")

    mine = [mine_in, mine_lo, mine_so, mine_o, mine_up, mine_dn]
    big_m = [m_w_in, m_lru_w_out, m_sc_w_out, m_w_o, m_ffn_w_up, m_ffn_w_down]
    big_v = [v_w_in, v_lru_w_out, v_sc_w_out, v_w_o, v_ffn_w_up, v_ffn_w_down]
    big_out = {nm: _adamw_big(pt, w, m, v, "adamw_" + nm)
               for nm, pt, w, m, v in zip(big_names, mine, big, big_m, big_v)}

    (scw_lru, scb, swa, sba, swx, sbx, slam, scw_sc, sg_ffn, scw_ffn, sg_final) = unpack_rows(
        _sum_packs(packs, "sum_small"), small_full)
    (sg_mix,) = unpack_rows(_all_reduce_small(pack_rows([dg_mix]), "all_reduce_g_mix"), [dg_mix])

    def my_cols(a):
        n = a.shape[1] // N_DEV
        return lax.dynamic_slice_in_dim(a, me * n, n, axis=1)

    small_names = ["g_mix", "lru_conv_w", "lru_conv_b", "lru_wa", "lru_ba", "lru_wx", "lru_bx", "lru_lambda",
                   "sc_conv_w", "g_ffn", "ffn_conv_w", "g_final"]
    small_w = [g_mix, lru_conv_w, lru_conv_b, lru_wa, lru_ba, lru_wx, lru_bx, lru_lambda, sc_conv_w, g_ffn,
               ffn_conv_w, g_final]
    small_m = [m_g_mix, m_lru_conv_w, m_lru_conv_b, m_lru_wa, m_lru_ba, m_lru_wx, m_lru_bx, m_lru_lambda,
               m_sc_conv_w, m_g_ffn, m_ffn_conv_w, m_g_final]
    small_v = [v_g_mix, v_lru_conv_w, v_lru_conv_b, v_lru_wa, v_lru_ba, v_lru_wx, v_lru_bx, v_lru_lambda,
               v_sc_conv_w, v_g_ffn, v_ffn_conv_w, v_g_final]
    small_g = [sg_mix.reshape(D), my_cols(scw_lru), scb.reshape(d_lru), swa, sba.reshape(d_lru), swx,
               sbx.reshape(d_lru), slam.reshape(d_lru), my_cols(scw_sc), sg_ffn.reshape(D), my_cols(scw_ffn),
               sg_final.reshape(D)]
    sd, snm, snv = _adamw_small([_as_rows(a) for a in small_w], [_as_rows(a) for a in small_g],
                                [_as_rows(a) for a in small_m], [_as_rows(a) for a in small_v], "adamw_small")
    small_out = {nm: (g, d.reshape(w.shape), nm_.reshape(w.shape), nv_.reshape(w.shape))
                 for nm, w, g, d, nm_, nv_ in zip(small_names, small_w, small_g, sd, snm, snv)}

    loss = lax.psum(loss_part[0, 0], AXES)
    order = ["g_mix", "w_in", "lru_conv_w", "lru_conv_b", "lru_wa", "lru_ba", "lru_wx", "lru_bx", "lru_lambda",
             "lru_w_out", "sc_conv_w", "sc_w_out", "w_o", "g_ffn", "ffn_w_up", "ffn_conv_w", "ffn_w_down", "g_final"]
    res = {**big_out, **small_out}
    return (loss, grad_x.reshape(x.shape),
            *[res[nm][0] for nm in order], *[res[nm][1] for nm in order],
            *[res[nm][2] for nm in order], *[res[nm][3] for nm in order])
```

```python
import functools
import math

import jax
import jax.numpy as jnp
from jax import lax
from jax.experimental import pallas as pl
from jax.experimental.pallas import tpu as pltpu

F32, BF16 = jnp.float32, jnp.bfloat16
MESH = pl.DeviceIdType.MESH
N_DEV = 8
N_CHIP = 4
AXES = ("x", "y", "c")

EPS = 1e-6
LRU_C = 8.0
HEAD_DIM = 64
ADAM_LR, ADAM_B1, ADAM_B2, ADAM_EPS, ADAM_WD, ADAM_STEP = 0.001, 0.9, 0.999, 1e-08, 0.01, 10

VMEM_LIMIT = 48 * 1024 * 1024
LANES = 128
SUB = 8
HALO = 16
TB = 512
C_LRU = 256
C_EW = 512
TM, TN, TK = 512, 1536, 2048


def _tile(n, pref, align=LANES):
    best = None
    for d in range(align, min(n, pref) + 1, align):
        if n % d == 0:
            best = d
    return best or n


def _cparams(sem=None, vmem=VMEM_LIMIT):
    kw = dict(vmem_limit_bytes=vmem)
    if sem is not None:
        kw["dimension_semantics"] = sem
    return pltpu.CompilerParams(**kw)


def _S(shape, dtype):
    return jax.ShapeDtypeStruct(shape, dtype)


ANY = pl.BlockSpec(memory_space=pl.ANY)
VMEM_SPEC = pl.BlockSpec(memory_space=pltpu.VMEM)


class _Task:
    def __init__(self, arrays, aliased, start, wait, fresh=(), nsem=3):
        self.arrays, self.aliased, self.start, self.wait = arrays, aliased, start, wait
        self.fresh, self.nsem = list(fresh), nsem


def _call(name, grid, compute, in_specs, args, out_shape, out_specs, scratch, tasks=(), own_aliases=None):
    n_in, n_out, n_scr = len(args), len(out_shape), len(scratch)
    x_in, x_out, aliases, where = [], [], dict(own_aliases or {}), []
    for t in tasks:
        places = []
        for k, arr in enumerate(t.arrays):
            if k in t.aliased:
                aliases[n_in + len(x_in)] = n_out + len(x_out)
                places.append(("out", len(x_out)))
                x_out.append(_S(arr.shape, arr.dtype))
            else:
                places.append(("in", len(x_in)))
            x_in.append(arr)
        for shp in t.fresh:
            places.append(("out", len(x_out)))
            x_out.append(shp)
        where.append(places)
    n_xi, n_xo = len(x_in), len(x_out)

    def body(*refs):
        ins, xi = refs[:n_in], refs[n_in:n_in + n_xi]
        o0 = n_in + n_xi
        outs, xo = refs[o0:o0 + n_out], refs[o0 + n_out:o0 + n_out + n_xo]
        s0 = o0 + n_out + n_xo
        scr, sems = refs[s0:s0 + n_scr], refs[s0 + n_scr:]
        ids = [pl.program_id(a) for a in range(len(grid))]

        def task_refs(ti):
            return [xo[i] if kind == "out" else xi[i] for kind, i in where[ti]]

        if tasks:
            first = functools.reduce(jnp.logical_and, [i == 0 for i in ids])

            @pl.when(first)
            def _():
                for ti, t in enumerate(tasks):
                    t.start(task_refs(ti), *sems[3 * ti:3 * ti + 3])

        compute(*ins, *outs, *scr)
        if tasks:
            last = functools.reduce(jnp.logical_and, [i == g - 1 for i, g in zip(ids, grid)])

            @pl.when(last)
            def _():
                for ti, t in enumerate(tasks):
                    t.wait(task_refs(ti), *sems[3 * ti:3 * ti + 3])

    sem_shapes = []
    for t in tasks:
        sem_shapes += [pltpu.SemaphoreType.DMA((t.nsem,)), pltpu.SemaphoreType.DMA((t.nsem,)),
                       pltpu.SemaphoreType.DMA((1,))]
    res = pl.pallas_call(
        body, name=name, grid=grid,
        in_specs=list(in_specs) + [ANY] * n_xi,
        out_specs=tuple(out_specs) + (ANY,) * n_xo,
        out_shape=tuple(out_shape) + tuple(x_out),
        scratch_shapes=list(scratch) + sem_shapes,
        input_output_aliases=aliases,
        compiler_params=_cparams(("arbitrary",) * len(grid)),
    )(*args, *x_in)
    outs, passed, o = res[:n_out], [], n_out
    for places in where:
        k = sum(1 for kind, _ in places if kind == "out")
        passed.append(list(res[o:o + k]))
        o += k
    return outs, passed


def _mm_nn(a, w3, *, out_dtype, name, residual=None, tm=TM, tn=TN, tk=TK, tasks=()):
    M, K = a.shape
    G, _, n = w3.shape
    tm, tn, tk = _tile(M, tm, SUB), _tile(n, tn), _tile(K, tk)
    nj, nk = n // tn, K // tk

    def compute(*refs):
        if residual is None:
            a_ref, w_ref, o_ref = refs[:3]
            r_ref = None
        else:
            a_ref, w_ref, r_ref, o_ref = refs[:4]

        def finish(r):
            if r_ref is not None:
                r = r + r_ref[...]
            o_ref[...] = r.astype(o_ref.dtype)

        if nk == 1:
            finish(jnp.dot(a_ref[...], w_ref[...], preferred_element_type=F32))
            return
        acc = refs[-1]
        k = pl.program_id(3)

        @pl.when(k == 0)
        def _():
            acc[...] = jnp.zeros_like(acc)

        acc[...] += jnp.dot(a_ref[...], w_ref[...], preferred_element_type=F32)

        @pl.when(k == nk - 1)
        def _():
            finish(acc[...])

    in_specs = [pl.BlockSpec((tm, tk), lambda g, j, i, k: (i, k)),
                pl.BlockSpec((None, tk, tn), lambda g, j, i, k: (g, k, j))]
    args = [a, w3]
    if residual is not None:
        in_specs.append(pl.BlockSpec((tm, tn), lambda g, j, i, k: (i, g * nj + j)))
        args.append(residual)
    outs, passed = _call(
        name, (G, nj, M // tm, nk), compute, in_specs, args, [_S((M, G * n), out_dtype)],
        [pl.BlockSpec((tm, tn), lambda g, j, i, k: (i, g * nj + j))],
        [] if nk == 1 else [pltpu.VMEM((tm, tn), F32)], tasks)
    return (outs[0], passed) if tasks else outs[0]


def _mm_nt(dy, w3, *, out_dtype, name, tm=1024, tko=1024, tn=TN, tasks=()):
    M, _ = dy.shape
    G, K, n = w3.shape
    tm, tko, tn = _tile(M, tm, SUB), _tile(K, tko), _tile(n, tn)
    nj = n // tn
    nr = G * nj

    def compute(dy_ref, w_ref, o_ref, acc):
        r = pl.program_id(2)

        @pl.when(r == 0)
        def _():
            acc[...] = jnp.zeros_like(acc)

        acc[...] += lax.dot_general(dy_ref[...], w_ref[...], (((1,), (1,)), ((), ())),
                                    preferred_element_type=F32)

        @pl.when(r == nr - 1)
        def _():
            o_ref[...] = acc[...].astype(o_ref.dtype)

    outs, passed = _call(
        name, (K // tko, M // tm, nr), compute,
        [pl.BlockSpec((tm, tn), lambda ko, i, r: (i, r)),
         pl.BlockSpec((None, tko, tn), lambda ko, i, r: (r // nj, ko, r % nj))],
        [dy, w3], [_S((M, K), out_dtype)], [pl.BlockSpec((tm, tko), lambda ko, i, r: (i, ko))],
        [pltpu.VMEM((tm, tko), F32)], tasks)
    return (outs[0], passed) if tasks else outs[0]


def _mm_tn(a, dy, G, *, out_dtype, name, tk=1024, tn=TN, tt=1024, tasks=()):
    M, K = a.shape
    n = dy.shape[1] // G
    tk, tn, tt = _tile(K, tk), _tile(n, tn), _tile(M, tt, SUB)
    nj, nt = n // tn, M // tt

    def compute(a_ref, dy_ref, o_ref, acc):
        t = pl.program_id(3)

        @pl.when(t == 0)
        def _():
            acc[...] = jnp.zeros_like(acc)

        acc[...] += lax.dot_general(a_ref[...], dy_ref[...], (((0,), (0,)), ((), ())),
                                    preferred_element_type=F32)

        @pl.when(t == nt - 1)
        def _():
            o_ref[...] = acc[...].astype(o_ref.dtype)

    outs, passed = _call(
        name, (G, nj, K // tk, nt), compute,
        [pl.BlockSpec((tt, tk), lambda g, j, k, t: (t, k)),
         pl.BlockSpec((tt, tn), lambda g, j, k, t: (t, g * nj + j))],
        [a, dy], [_S((G, K, n), out_dtype)], [pl.BlockSpec((None, tk, tn), lambda g, j, k, t: (g, k, j))],
        [pltpu.VMEM((tk, tn), F32)], tasks)
    return (outs[0], passed) if tasks else outs[0]


def _cast_into_slot(w, me_idx, name):
    R, C = w.shape
    tr = _tile(R, 512, HALO)

    def body(me_ref, w_ref, o_ref):
        del me_ref
        o_ref[...] = w_ref[...].astype(BF16)

    return pl.pallas_call(
        body, name=name, out_shape=_S((N_DEV, R, C), BF16),
        grid_spec=pltpu.PrefetchScalarGridSpec(
            num_scalar_prefetch=1, grid=(R // tr,),
            in_specs=[pl.BlockSpec((tr, C), lambda i, me_ref: (i, 0))],
            out_specs=pl.BlockSpec((None, tr, C), lambda i, me_ref: (me_ref[0], i, 0))),
        compiler_params=_cparams(("parallel",)),
    )(me_idx, w)


def _down(cur, prev8, j):
    return pltpu.roll(jnp.concatenate([prev8, cur], axis=0), j, 0)[SUB:, :]


def _up(cur, next8, j):
    n = cur.shape[0] + SUB
    return pltpu.roll(jnp.concatenate([cur, next8], axis=0), n - j, 0)[:cur.shape[0], :]


def _conv(x, prev8, w_ref):
    kw = w_ref.shape[0]
    y = x * w_ref[pl.ds(kw - 1, 1), :]
    for k in range(kw - 1):
        y = y + _down(x, prev8, kw - 1 - k) * w_ref[pl.ds(k, 1), :]
    return y


def _conv_t(dy, next8, w_ref):
    kw = w_ref.shape[0]
    dx = dy * w_ref[pl.ds(kw - 1, 1), :]
    for k in range(kw - 1):
        dx = dx + _up(dy, next8, kw - 1 - k) * w_ref[pl.ds(k, 1), :]
    return dx


def _conv_dw(dw_ref, dy, x, prev8, first):
    kw = dw_ref.shape[0]

    @pl.when(first)
    def _():
        dw_ref[...] = jnp.zeros_like(dw_ref)

    for k in range(kw):
        xs = x if k == kw - 1 else _down(x, prev8, kw - 1 - k)
        dw_ref[pl.ds(k, 1), :] += jnp.sum(dy * xs, axis=0, keepdims=True)


def _acc(ref, val, first):
    @pl.when(first)
    def _():
        ref[...] = jnp.zeros_like(ref)

    ref[...] += val


def _acc_row(ref, val, first):
    _acc(ref, jnp.sum(val, axis=0, keepdims=True), first)


def _prev8(h_ref, t):
    return jnp.where(t > 0, h_ref[...].astype(F32)[HALO - SUB:, :], 0.0)


def _next8(h_ref, is_last):
    return jnp.where(is_last, 0.0, h_ref[...].astype(F32)[:SUB, :])


_GELU_K0 = math.sqrt(2.0 / math.pi)
_GELU_K1 = 0.044715


def _gelu_and_grad(x):
    x2 = x * x
    th = jnp.tanh(_GELU_K0 * x * (1.0 + _GELU_K1 * x2))
    g = 0.5 * x * (1.0 + th)
    dg = 0.5 * (1.0 + th) + 0.5 * x * (1.0 - th * th) * (_GELU_K0 * (1.0 + 3.0 * _GELU_K1 * x2))
    return g, dg


def _neg_expm1(z):
    series = -z * (1.0 + z * (0.5 + z * (1.0 / 6.0 + z * (1.0 / 24.0))))
    return jnp.where(z > -0.03, series, 1.0 - jnp.exp(z))


def _store_block(stage_ref, dst_hbm, sem, row0, col0):
    tb, c = stage_ref.shape
    return pltpu.make_async_copy(stage_ref, dst_hbm.at[pl.ds(row0, tb), pl.ds(col0, c)], sem)


def _halo_prev_map(hb, col_fn):
    return lambda c, t: (jnp.maximum(t * hb - 1, 0), col_fn(c))


def _rms_fwd(x, g, name):
    T, D = x.shape
    tb = _tile(T, TB, SUB)

    def body(x_ref, g_ref, o_ref):
        xv = x_ref[...]
        rstd = lax.rsqrt(jnp.mean(xv * xv, axis=-1, keepdims=True) + EPS)
        o_ref[...] = (xv * rstd * g_ref[...]).astype(BF16)

    return pl.pallas_call(
        body, name=name, out_shape=_S((T, D), BF16), grid=(T // tb,),
        in_specs=[pl.BlockSpec((tb, D), lambda i: (i, 0)), pl.BlockSpec((1, D), lambda i: (0, 0))],
        out_specs=pl.BlockSpec((tb, D), lambda i: (i, 0)),
        compiler_params=_cparams(("parallel",)),
    )(x, g.reshape(1, D))


def _rms_bwd(x, g, dh, dres, name):
    T, D = x.shape
    tb = _tile(T, 256, SUB)

    def body(x_ref, g_ref, dh_ref, dr_ref, dx_ref, dxb_ref, dg_ref):
        i = pl.program_id(0)
        xv = x_ref[...]
        rstd = lax.rsqrt(jnp.mean(xv * xv, axis=-1, keepdims=True) + EPS)
        xn = xv * rstd
        dhv = dh_ref[...].astype(F32)
        _acc_row(dg_ref, dhv * xn, i == 0)
        dxn = dhv * g_ref[...]
        dx = dr_ref[...] + rstd * (dxn - xn * jnp.mean(dxn * xn, axis=-1, keepdims=True))
        dx_ref[...] = dx
        dxb_ref[...] = dx.astype(BF16)

    blk = pl.BlockSpec((tb, D), lambda i: (i, 0))
    vec = pl.BlockSpec((1, D), lambda i: (0, 0))
    return pl.pallas_call(
        body, name=name, out_shape=(_S((T, D), F32), _S((T, D), BF16), _S((1, D), F32)),
        grid=(T // tb,), in_specs=[blk, vec, blk, blk], out_specs=(blk, blk, vec),
        compiler_params=_cparams(("arbitrary",)),
    )(x, g.reshape(1, D), dh, dres)


def _loss_head(x2, g, target, name):
    T, D = x2.shape
    tb = _tile(T, 256, SUB)

    def body(x_ref, g_ref, t_ref, dx_ref, dxb_ref, loss_ref, dg_ref):
        i = pl.program_id(0)
        xv = x_ref[...]
        rstd = lax.rsqrt(jnp.mean(xv * xv, axis=-1, keepdims=True) + EPS)
        xn = xv * rstd
        err = xn * g_ref[...] - t_ref[...]
        part = 0.5 * jnp.sum(jnp.mean(err * err, axis=-1, keepdims=True), axis=0, keepdims=True)
        part = jnp.broadcast_to(part, (1, LANES))
        _acc(loss_ref, part, i == 0)
        dy = err * (1.0 / D)
        _acc_row(dg_ref, dy * xn, i == 0)
        dxn = dy * g_ref[...]
        dx = rstd * (dxn - xn * jnp.mean(dxn * xn, axis=-1, keepdims=True))
        dx_ref[...] = dx
        dxb_ref[...] = dx.astype(BF16)

    blk = pl.BlockSpec((tb, D), lambda i: (i, 0))
    vec = pl.BlockSpec((1, D), lambda i: (0, 0))
    return pl.pallas_call(
        body, name=name,
        out_shape=(_S((T, D), F32), _S((T, D), BF16), _S((1, LANES), F32), _S((1, D), F32)),
        grid=(T // tb,), in_specs=[blk, vec, blk],
        out_specs=(blk, blk, pl.BlockSpec((1, LANES), lambda i: (0, 0)), vec),
        compiler_params=_cparams(("arbitrary",)),
    )(x2, g.reshape(1, D), target)


def _lru_gates(xc, wa_ref, ba_ref, wx_ref, bx_ref, lam_ref):
    xcb = xc.astype(BF16)
    r = jax.nn.sigmoid(jnp.dot(xcb, wa_ref[...], preferred_element_type=F32) + ba_ref[...])
    i = jax.nn.sigmoid(jnp.dot(xcb, wx_ref[...], preferred_element_type=F32) + bx_ref[...])
    sp = jax.nn.softplus(-lam_ref[...])
    log_a = (-LRU_C * sp) * r
    a = jnp.exp(log_a)
    s = jnp.sqrt(_neg_expm1(2.0 * log_a))
    return xcb, r, i, a, s


def _lru_fwd(p, conv_w, conv_b, wa_bd, ba, wx_bd, bx, lam, *, name, tasks=()):
    T = p.shape[0]
    d = lam.shape[-1]
    C = _tile(d, C_LRU)
    nC = d // C
    tb = _tile(T, TB, HALO)
    nT, hb, nt = T // tb, tb // HALO, tb // SUB

    def body(x_ref, xh_ref, g_ref, cw_ref, cb_ref, wa_ref, ba_ref, wx_ref, bx_ref, lam_ref,
             hs_ref, y_ref, a_s, u_s, h_s):
        t = pl.program_id(1)

        @pl.when(t == 0)
        def _():
            h_s[...] = jnp.zeros_like(h_s)

        x = x_ref[...].astype(F32)
        xc = _conv(x, _prev8(xh_ref, t), cw_ref) + cb_ref[...]
        _, r, i, a, s = _lru_gates(xc, wa_ref, ba_ref, wx_ref, bx_ref, lam_ref)
        a_s[...] = a
        u_s[...] = s * (i * xc)
        row = lax.broadcasted_iota(jnp.int32, (SUB, C), 0)

        def step(k, h):
            o = pl.multiple_of(k * SUB, SUB)
            A = a_s[pl.ds(o, SUB), :]
            B = u_s[pl.ds(o, SUB), :]
            for sh in (1, 2, 4):
                m = row >= sh
                Ap = pltpu.roll(A, sh, 0)
                Bp = pltpu.roll(B, sh, 0)
                B = jnp.where(m, A * Bp + B, B)
                A = jnp.where(m, A * Ap, A)
            hs = A * h + B
            hs_ref[pl.ds(o, SUB), :] = hs
            return jnp.broadcast_to(hs[SUB - 1:SUB, :], (SUB, C))

        h_s[...] = lax.fori_loop(0, nt, step, h_s[...])
        gel, _ = _gelu_and_grad(g_ref[...].astype(F32))
        y_ref[...] = (gel * hs_ref[...]).astype(BF16)

    vec = pl.BlockSpec((1, C), lambda c, t: (0, c))
    sq = pl.BlockSpec((None, C, C), lambda c, t: (c, 0, 0))
    outs, passed = _call(
        name, (nC, nT), body,
        [pl.BlockSpec((tb, C), lambda c, t: (t, c)),
         pl.BlockSpec((HALO, C), _halo_prev_map(hb, lambda c: c)),
         pl.BlockSpec((tb, C), lambda c, t: (t, nC + c)),
         pl.BlockSpec((conv_w.shape[0], C), lambda c, t: (0, c)),
         vec, sq, vec, sq, vec, vec],
        [p, p, p, conv_w, conv_b, wa_bd, ba, wx_bd, bx, lam],
        [_S((T, d), F32), _S((T, d), BF16)],
        [pl.BlockSpec((tb, C), lambda c, t: (t, c)), pl.BlockSpec((tb, C), lambda c, t: (t, c))],
        [pltpu.VMEM((tb, C), F32), pltpu.VMEM((tb, C), F32), pltpu.VMEM((SUB, C), F32)], tasks)
    return (*outs, passed) if tasks else outs


def _lru_bwd(p, hs, dyl, dp, conv_w, conv_b, wa_bd, ba, wx_bd, bx, lam, *, name, tasks=()):
    T = p.shape[0]
    d = lam.shape[-1]
    C = _tile(d, C_LRU)
    nC = d // C
    tb = _tile(T, TB, HALO)
    nT, hb, nt = T // tb, tb // HALO, tb // SUB
    kw = conv_w.shape[0]

    def body(x_ref, xh_ref, g_ref, hs_ref, hh_ref, dy_ref, cw_ref, cb_ref, wa_ref, ba_ref, wx_ref, bx_ref,
             lam_ref, dp_in, dp_ref, dcw_ref, dcb_ref, dwa_ref, dba_ref, dwx_ref, dbx_ref, dlam_ref,
             b_s, g_s, dh_s, an_s, dhn_s, dxn_s, st_x, st_g, sems):
        del dp_in
        c = pl.program_id(0)
        tr = pl.program_id(1)
        t = nT - 1 - tr
        first = tr == 0

        @pl.when(first)
        def _():
            an_s[...] = jnp.zeros_like(an_s)
            dhn_s[...] = jnp.zeros_like(dhn_s)
            dxn_s[...] = jnp.zeros_like(dxn_s)

        x = x_ref[...].astype(F32)
        xprev = _prev8(xh_ref, t)
        xc = _conv(x, xprev, cw_ref) + cb_ref[...]
        xcb, r, i, a, s = _lru_gates(xc, wa_ref, ba_ref, wx_ref, bx_ref, lam_ref)
        hsv = hs_ref[...]
        dy = dy_ref[...].astype(F32)
        gel, dgel = _gelu_and_grad(g_ref[...].astype(F32))
        st_g[...] = (dy * hsv * dgel).astype(BF16)

        b_s[...] = _up(a, an_s[...], 1)
        g_s[...] = dy * gel
        row = lax.broadcasted_iota(jnp.int32, (SUB, C), 0)

        def step(k, carry):
            o = pl.multiple_of((nt - 1 - k) * SUB, SUB)
            B = b_s[pl.ds(o, SUB), :]
            G = g_s[pl.ds(o, SUB), :]
            for sh in (1, 2, 4):
                m = row < SUB - sh
                Bn = pltpu.roll(B, SUB - sh, 0)
                Gn = pltpu.roll(G, SUB - sh, 0)
                G = jnp.where(m, B * Gn + G, G)
                B = jnp.where(m, B * Bn, B)
            dh = B * carry + G
            dh_s[pl.ds(o, SUB), :] = dh
            return jnp.broadcast_to(dh[0:1, :], (SUB, C))

        dhn_s[...] = lax.fori_loop(0, nt, step, dhn_s[...])
        an_s[...] = a[:SUB, :]
        dh = dh_s[...]

        hprev = _down(hsv, jnp.where(t > 0, hh_ref[...][HALO - SUB:, :], 0.0), 1)
        d_a = dh * hprev
        ixc = i * xc
        d_s = dh * ixc
        d_i = dh * s * xc
        d_xc = dh * s * i
        d_l = d_a * a - d_s * (a * a) / s
        sp = jax.nn.softplus(-lam_ref[...])
        _acc_row(dlam_ref, d_l * r * (LRU_C * jax.nn.sigmoid(-lam_ref[...])), first)
        d_zr = (d_l * (-LRU_C * sp)) * r * (1.0 - r)
        d_zi = d_i * i * (1.0 - i)
        _acc_row(dba_ref, d_zr, first)
        _acc_row(dbx_ref, d_zi, first)
        d_zrb = d_zr.astype(BF16)
        d_zib = d_zi.astype(BF16)
        tn_dims = (((0,), (0,)), ((), ()))
        nt_dims = (((1,), (1,)), ((), ()))
        gwa = lax.dot_general(xcb, d_zrb, tn_dims, preferred_element_type=F32)
        gwx = lax.dot_general(xcb, d_zib, tn_dims, preferred_element_type=F32)
        _acc(dwa_ref, gwa, first)
        _acc(dwx_ref, gwx, first)
        d_xc = (d_xc + lax.dot_general(d_zrb, wa_ref[...], nt_dims, preferred_element_type=F32)
                + lax.dot_general(d_zib, wx_ref[...], nt_dims, preferred_element_type=F32))
        _acc_row(dcb_ref, d_xc, first)
        _conv_dw(dcw_ref, d_xc, x, xprev, first)
        st_x[...] = _conv_t(d_xc, dxn_s[...], cw_ref).astype(BF16)
        dxn_s[...] = d_xc[:SUB, :]

        cx = _store_block(st_x, dp_ref, sems.at[0], t * tb, c * C)
        cg = _store_block(st_g, dp_ref, sems.at[1], t * tb, d + c * C)
        cx.start()
        cg.start()
        cx.wait()
        cg.wait()

    rev = lambda c, tr: (nT - 1 - tr, c)
    vec = pl.BlockSpec((1, C), lambda c, tr: (0, c))
    sq = pl.BlockSpec((None, C, C), lambda c, tr: (c, 0, 0))
    cwb = pl.BlockSpec((kw, C), lambda c, tr: (0, c))
    halo_prev = lambda c, tr: (jnp.maximum((nT - 1 - tr) * hb - 1, 0), c)
    outs, passed = _call(
        name, (nC, nT), body,
        [pl.BlockSpec((tb, C), rev),
         pl.BlockSpec((HALO, C), halo_prev),
         pl.BlockSpec((tb, C), lambda c, tr: (nT - 1 - tr, nC + c)),
         pl.BlockSpec((tb, C), rev),
         pl.BlockSpec((HALO, C), halo_prev),
         pl.BlockSpec((tb, C), rev),
         cwb, vec, sq, vec, sq, vec, vec, ANY],
        [p, p, p, hs, hs, dyl, conv_w, conv_b, wa_bd, ba, wx_bd, bx, lam, dp],
        [_S(dp.shape, dp.dtype), _S((kw, d), F32), _S((1, d), F32), _S((nC, C, C), F32), _S((1, d), F32),
         _S((nC, C, C), F32), _S((1, d), F32), _S((1, d), F32)],
        [ANY, cwb, vec, sq, vec, sq, vec, vec],
        [pltpu.VMEM((tb, C), F32), pltpu.VMEM((tb, C), F32), pltpu.VMEM((tb, C), F32),
         pltpu.VMEM((SUB, C), F32), pltpu.VMEM((SUB, C), F32), pltpu.VMEM((SUB, C), F32),
         pltpu.VMEM((tb, C), BF16), pltpu.VMEM((tb, C), BF16), pltpu.SemaphoreType.DMA((2,))],
        tasks, own_aliases={13: 0})
    return (*outs, passed) if tasks else outs


def _sc_fwd(p, conv_w, *, d, name):
    T = p.shape[0]
    C = _tile(d, C_EW)
    nC = d // C
    tb = _tile(T, TB, HALO)
    nT, hb = T // tb, tb // HALO

    def body(b_ref, c_ref, ch_ref, v_ref, vh_ref, w_ref, y_ref):
        t = pl.program_id(1)
        cv = c_ref[...].astype(F32) * v_ref[...].astype(F32)
        cvp = _prev8(ch_ref, t) * _prev8(vh_ref, t)
        y_ref[...] = (b_ref[...].astype(F32) * _conv(cv, cvp, w_ref)).astype(BF16)

    seg = lambda k: pl.BlockSpec((tb, C), lambda c, t: (t, k * nC + c))
    hseg = lambda k: pl.BlockSpec((HALO, C), _halo_prev_map(hb, lambda c: k * nC + c))
    return pl.pallas_call(
        body, name=name, out_shape=_S((T, d), BF16), grid=(nC, nT),
        in_specs=[seg(2), seg(3), hseg(3), seg(4), hseg(4), pl.BlockSpec((conv_w.shape[0], C), lambda c, t: (0, c))],
        out_specs=pl.BlockSpec((tb, C), lambda c, t: (t, c)),
        compiler_params=_cparams(("parallel", "parallel")),
    )(p, p, p, p, p, conv_w)


def _sc_bwd(p, dys, dp, conv_w, *, d, name):
    T = p.shape[0]
    C = _tile(d, C_EW)
    nC = d // C
    tb = _tile(T, TB, HALO)
    nT, hb = T // tb, tb // HALO
    kw = conv_w.shape[0]

    def body(b_ref, bn_ref, c_ref, ch_ref, v_ref, vh_ref, dy_ref, dyn_ref, w_ref, dp_in, dp_ref, dw_ref,
             st_b, st_c, st_v, sems):
        del dp_in
        c = pl.program_id(0)
        t = pl.program_id(1)
        last = t == nT - 1
        bv = b_ref[...].astype(F32)
        cvv = c_ref[...].astype(F32)
        vv = v_ref[...].astype(F32)
        dy = dy_ref[...].astype(F32)
        cv = cvv * vv
        cvp = _prev8(ch_ref, t) * _prev8(vh_ref, t)
        st_b[...] = (dy * _conv(cv, cvp, w_ref)).astype(BF16)
        dz = dy * bv
        dzn = _next8(dyn_ref, last) * _next8(bn_ref, last)
        _conv_dw(dw_ref, dz, cv, cvp, t == 0)
        dcv = _conv_t(dz, dzn, w_ref)
        st_c[...] = (dcv * vv).astype(BF16)
        st_v[...] = (dcv * cvv).astype(BF16)
        cps = [_store_block(st, dp_ref, sems.at[k], t * tb, (2 + k) * d + c * C)
               for k, st in enumerate((st_b, st_c, st_v))]
        for cp in cps:
            cp.start()
        for cp in cps:
            cp.wait()

    seg = lambda k: pl.BlockSpec((tb, C), lambda c, t: (t, k * nC + c))
    hseg = lambda k: pl.BlockSpec((HALO, C), _halo_prev_map(hb, lambda c: k * nC + c))
    last_h = T // HALO - 1
    nseg = lambda k: pl.BlockSpec((HALO, C), lambda c, t: (jnp.minimum((t + 1) * hb, last_h), k * nC + c))
    return pl.pallas_call(
        body, name=name, out_shape=(_S(dp.shape, dp.dtype), _S((kw, d), F32)), grid=(nC, nT),
        in_specs=[seg(2), nseg(2), seg(3), hseg(3), seg(4), hseg(4),
                  pl.BlockSpec((tb, C), lambda c, t: (t, c)), nseg(0),
                  pl.BlockSpec((kw, C), lambda c, t: (0, c)), ANY],
        out_specs=(ANY, pl.BlockSpec((kw, C), lambda c, t: (0, c))),
        scratch_shapes=[pltpu.VMEM((tb, C), BF16)] * 3 + [pltpu.SemaphoreType.DMA((3,))],
        input_output_aliases={9: 0},
        compiler_params=_cparams(("arbitrary", "arbitrary")),
    )(p, p, p, p, p, p, dys, dys, conv_w, dp)


def _merge_fwd(p, y_lru, y_sc, *, col0, name):
    T, D = y_lru.shape
    C = _tile(math.gcd(D, col0), 1024)
    nC = D // C
    k0 = col0 // C
    tb = _tile(T, 256, HALO)

    def body(gl_ref, gs_ref, yl_ref, ys_ref, o_ref):
        o_ref[...] = (jax.nn.sigmoid(gl_ref[...].astype(F32)) * yl_ref[...].astype(F32)
                      + jax.nn.sigmoid(gs_ref[...].astype(F32)) * ys_ref[...].astype(F32)).astype(BF16)

    blk = pl.BlockSpec((tb, C), lambda c, t: (t, c))
    return pl.pallas_call(
        body, name=name, out_shape=_S((T, D), BF16), grid=(nC, T // tb),
        in_specs=[pl.BlockSpec((tb, C), lambda c, t: (t, k0 + c)),
                  pl.BlockSpec((tb, C), lambda c, t: (t, k0 + nC + c)), blk, blk],
        out_specs=blk, compiler_params=_cparams(("parallel", "parallel")),
    )(p, p, y_lru, y_sc)


def _merge_bwd(p, y_lru, y_sc, dm, *, col0, name):
    T, D = y_lru.shape
    C = _tile(math.gcd(D, col0), 1024)
    nC = D // C
    k0 = col0 // C
    tb = _tile(T, 256, HALO)

    def body(gl_ref, gs_ref, yl_ref, ys_ref, dm_ref, dp_ref, dyl_ref, dys_ref, st_l, st_s, sems):
        c = pl.program_id(0)
        t = pl.program_id(1)
        dmv = dm_ref[...].astype(F32)
        sl = jax.nn.sigmoid(gl_ref[...].astype(F32))
        ss = jax.nn.sigmoid(gs_ref[...].astype(F32))
        dyl_ref[...] = (dmv * sl).astype(BF16)
        dys_ref[...] = (dmv * ss).astype(BF16)
        st_l[...] = (dmv * yl_ref[...].astype(F32) * sl * (1.0 - sl)).astype(BF16)
        st_s[...] = (dmv * ys_ref[...].astype(F32) * ss * (1.0 - ss)).astype(BF16)
        cl = _store_block(st_l, dp_ref, sems.at[0], t * tb, col0 + c * C)
        cs = _store_block(st_s, dp_ref, sems.at[1], t * tb, col0 + D + c * C)
        cl.start()
        cs.start()
        cl.wait()
        cs.wait()

    blk = pl.BlockSpec((tb, C), lambda c, t: (t, c))
    return pl.pallas_call(
        body, name=name, out_shape=(_S(p.shape, BF16), _S((T, D), BF16), _S((T, D), BF16)),
        grid=(nC, T // tb),
        in_specs=[pl.BlockSpec((tb, C), lambda c, t: (t, k0 + c)),
                  pl.BlockSpec((tb, C), lambda c, t: (t, k0 + nC + c)), blk, blk, blk],
        out_specs=(ANY, blk, blk),
        scratch_shapes=[pltpu.VMEM((tb, C), BF16), pltpu.VMEM((tb, C), BF16), pltpu.SemaphoreType.DMA((2,))],
        compiler_params=_cparams(("arbitrary", "arbitrary")),
    )(p, p, y_lru, y_sc, dm)


def _ffn_act_fwd(uu, conv_w, *, name, tasks=()):
    T = uu.shape[0]
    F = uu.shape[1] // 2
    C = _tile(F, C_EW)
    nC = F // C
    tb = _tile(T, 256, HALO)
    nT, hb = T // tb, tb // HALO
    kw = conv_w.shape[0]

    def body(g_ref, gh_ref, v_ref, vh_ref, wg_ref, wv_ref, o_ref):
        t = pl.program_id(1)
        cg = _conv(g_ref[...].astype(F32), _prev8(gh_ref, t), wg_ref)
        cv = _conv(v_ref[...].astype(F32), _prev8(vh_ref, t), wv_ref)
        o_ref[...] = (cg * jax.nn.sigmoid(cg) * cv).astype(BF16)

    seg = lambda k: pl.BlockSpec((tb, C), lambda c, t: (t, k * nC + c))
    hseg = lambda k: pl.BlockSpec((HALO, C), _halo_prev_map(hb, lambda c: k * nC + c))
    wseg = lambda k: pl.BlockSpec((kw, C), lambda c, t: (0, k * nC + c))
    outs, passed = _call(
        name, (nC, nT), body, [seg(0), hseg(0), seg(1), hseg(1), wseg(0), wseg(1)],
        [uu, uu, uu, uu, conv_w, conv_w], [_S((T, F), BF16)], [pl.BlockSpec((tb, C), lambda c, t: (t, c))], [], tasks)
    return (outs[0], passed) if tasks else outs[0]


def _ffn_act_bwd(uu, dact, conv_w, *, name):
    T = uu.shape[0]
    F = uu.shape[1] // 2
    C = _tile(F, C_EW)
    nC = F // C
    tb = _tile(T, 256, HALO)
    nT, hb = T // tb, tb // HALO
    kw = conv_w.shape[0]

    def body(g_ref, gh_ref, v_ref, vh_ref, da_ref, wg_ref, wv_ref, du_ref, dwg_ref, dwv_ref,
             gn_s, vn_s, st_g, st_v, sems):
        c = pl.program_id(0)
        tr = pl.program_id(1)
        t = nT - 1 - tr
        first = tr == 0

        @pl.when(first)
        def _():
            gn_s[...] = jnp.zeros_like(gn_s)
            vn_s[...] = jnp.zeros_like(vn_s)

        ug = g_ref[...].astype(F32)
        uv = v_ref[...].astype(F32)
        ugp = _prev8(gh_ref, t)
        uvp = _prev8(vh_ref, t)
        cg = _conv(ug, ugp, wg_ref)
        cv = _conv(uv, uvp, wv_ref)
        da = da_ref[...].astype(F32)
        sg = jax.nn.sigmoid(cg)
        d_cg = da * cv * (sg * (1.0 + cg * (1.0 - sg)))
        d_cv = da * (cg * sg)
        _conv_dw(dwg_ref, d_cg, ug, ugp, first)
        _conv_dw(dwv_ref, d_cv, uv, uvp, first)
        st_g[...] = _conv_t(d_cg, gn_s[...], wg_ref).astype(BF16)
        st_v[...] = _conv_t(d_cv, vn_s[...], wv_ref).astype(BF16)
        gn_s[...] = d_cg[:SUB, :]
        vn_s[...] = d_cv[:SUB, :]
        cpg = _store_block(st_g, du_ref, sems.at[0], t * tb, c * C)
        cpv = _store_block(st_v, du_ref, sems.at[1], t * tb, F + c * C)
        cpg.start()
        cpv.start()
        cpg.wait()
        cpv.wait()

    seg = lambda k: pl.BlockSpec((tb, C), lambda c, tr: (nT - 1 - tr, k * nC + c))
    hseg = lambda k: pl.BlockSpec((HALO, C), lambda c, tr: (jnp.maximum((nT - 1 - tr) * hb - 1, 0), k * nC + c))
    wseg = lambda k: pl.BlockSpec((kw, C), lambda c, tr: (0, k * nC + c))
    dwb = pl.BlockSpec((kw, C), lambda c, tr: (0, c))
    return pl.pallas_call(
        body, name=name, out_shape=(_S(uu.shape, BF16), _S((kw, F), F32), _S((kw, F), F32)), grid=(nC, nT),
        in_specs=[seg(0), hseg(0), seg(1), hseg(1), pl.BlockSpec((tb, C), lambda c, tr: (nT - 1 - tr, c)),
                  wseg(0), wseg(1)],
        out_specs=(ANY, dwb, dwb),
        scratch_shapes=[pltpu.VMEM((SUB, C), F32), pltpu.VMEM((SUB, C), F32),
                        pltpu.VMEM((tb, C), BF16), pltpu.VMEM((tb, C), BF16), pltpu.SemaphoreType.DMA((2,))],
        compiler_params=_cparams(("arbitrary", "arbitrary")),
    )(uu, uu, uu, uu, dact, conv_w, conv_w)


def _place():
    x, y, c = lax.axis_index("x"), lax.axis_index("y"), lax.axis_index("c")
    return x, y, c


def _chips(x, y):
    return [(1 - x, y), (x, 1 - y), (1 - x, 1 - y)]


def _all_gather(arrays, placed, over_ici, name):
    n = len(arrays)

    def body(*refs):
        ins, outs = refs[:n], refs[n:2 * n]
        send_sems, recv_sems, local_sems = refs[2 * n:]
        x, y, c = _place()
        me, sibling = (x, y, c), (x, y, 1 - c)
        chips = _chips(x, y)
        full = [a for a in range(n) if over_ici[a]]

        def idx(px, py, pc):
            return 4 * px + 2 * py + pc

        def copy(a, k, block, to):
            dst = outs[a].at[idx(*block)]
            src = ins[a] if (block is me and not placed[a]) else dst
            return pltpu.make_async_remote_copy(
                src_ref=src, dst_ref=dst, send_sem=send_sems.at[a, k], recv_sem=recv_sems.at[a, k],
                device_id=to, device_id_type=MESH)

        mine = [pltpu.make_async_copy(ins[a], outs[a].at[idx(*me)], local_sems.at[a])
                for a in range(n) if not placed[a]]
        for cp in mine:
            cp.start()
        first = []
        for a in full:
            first += [copy(a, 1 + j, me, (*chip, c)) for j, chip in enumerate(chips)]
        for a in range(n):
            first.append(copy(a, 0, me, sibling))
        for cp in first:
            cp.start()
        passed = []
        for a in full:
            for j, chip in enumerate(chips):
                copy(a, 1 + j, (*chip, c), me).wait_recv()
                cp = copy(a, 4 + j, (*chip, c), sibling)
                cp.start()
                passed.append(cp)
        for a in range(n):
            copy(a, 0, sibling, me).wait_recv()
        for a in full:
            for j, chip in enumerate(chips):
                copy(a, 4 + j, (*chip, 1 - c), me).wait_recv()
        for cp in first + passed:
            cp.wait_send()
        for cp in mine:
            cp.wait()

    return pl.pallas_call(
        body, name=name,
        out_shape=tuple(_S(s.shape if placed[a] else (N_DEV,) + s.shape, s.dtype) for a, s in enumerate(arrays)),
        in_specs=[ANY] * n, out_specs=tuple([ANY] * n),
        scratch_shapes=[pltpu.SemaphoreType.DMA((n, 7)), pltpu.SemaphoreType.DMA((n, 7)),
                        pltpu.SemaphoreType.DMA((n,))],
        input_output_aliases={a: a for a in range(n) if placed[a]},
    )(*arrays)


def _rows_of(ref, blk, rows):
    v = ref.at[blk]
    return v if rows is None else v.at[pl.ds(rows[0], rows[1])]


ALL_ROWS = "all"


def _gather_task(buf, ici=None, fwd=None):
    rows = lambda r: None if r == ALL_ROWS else r

    def copies(refs, ss, rs):
        x, y, c = _place()
        me = 4 * x + 2 * y + c
        cps = []
        for j, (px, py) in enumerate(_chips(x, y)):
            if ici is not None:
                blk = _rows_of(refs[0], me, rows(ici))
                cps.append(pltpu.make_async_remote_copy(
                    src_ref=blk, dst_ref=blk, send_sem=ss.at[j], recv_sem=rs.at[j],
                    device_id=(px, py, c), device_id_type=MESH))
            if fwd is not None:
                blk = _rows_of(refs[0], 4 * px + 2 * py + c, rows(fwd))
                cps.append(pltpu.make_async_remote_copy(
                    src_ref=blk, dst_ref=blk, send_sem=ss.at[3 + j], recv_sem=rs.at[3 + j],
                    device_id=(x, y, 1 - c), device_id_type=MESH))
        return cps

    def start(refs, ss, rs, ls):
        for cp in copies(refs, ss, rs):
            cp.start()

    def wait(refs, ss, rs, ls):
        x, y, c = _place()
        for j, (px, py) in enumerate(_chips(x, y)):
            if ici is not None:
                blk = _rows_of(refs[0], 4 * px + 2 * py + c, rows(ici))
                pltpu.make_async_remote_copy(
                    src_ref=blk, dst_ref=blk, send_sem=ss.at[j], recv_sem=rs.at[j],
                    device_id=(px, py, c), device_id_type=MESH).wait_recv()
            if fwd is not None:
                blk = _rows_of(refs[0], 4 * px + 2 * py + 1 - c, rows(fwd))
                pltpu.make_async_remote_copy(
                    src_ref=blk, dst_ref=blk, send_sem=ss.at[3 + j], recv_sem=rs.at[3 + j],
                    device_id=(x, y, 1 - c), device_id_type=MESH).wait_recv()
        for cp in copies(refs, ss, rs):
            cp.wait_send()

    return _Task([buf], [0], start, wait, nsem=6)


def _exchange_task(parts, landing, rows=None):
    def copies(refs, ss, rs):
        x, y, c = _place()
        myq = 2 * x + y
        return [pltpu.make_async_remote_copy(
            src_ref=_rows_of(refs[0], 2 * px + py, rows), dst_ref=_rows_of(refs[1], myq, rows),
            send_sem=ss.at[k], recv_sem=rs.at[k], device_id=(px, py, c), device_id_type=MESH)
            for k, (px, py) in enumerate(_chips(x, y))]

    def start(refs, ss, rs, ls):
        for cp in copies(refs, ss, rs):
            cp.start()

    def wait(refs, ss, rs, ls):
        x, y, c = _place()
        for k, (px, py) in enumerate(_chips(x, y)):
            pltpu.make_async_remote_copy(
                src_ref=_rows_of(refs[0], 2 * x + y, rows), dst_ref=_rows_of(refs[1], 2 * px + py, rows),
                send_sem=ss.at[k], recv_sem=rs.at[k], device_id=(px, py, c), device_id_type=MESH).wait_recv()
        for cp in copies(refs, ss, rs):
            cp.wait_send()

    return _Task([parts, landing], [1], start, wait)


def _swap_task(g):
    g4 = g.reshape((N_CHIP, 2) + g.shape[1:])

    def copy(refs, ss, rs):
        x, y, c = _place()
        return pltpu.make_async_remote_copy(
            src_ref=refs[0].at[:, 1 - c], dst_ref=refs[1], send_sem=ss.at[0], recv_sem=rs.at[0],
            device_id=(x, y, 1 - c), device_id_type=MESH)

    def start(refs, ss, rs, ls):
        copy(refs, ss, rs).start()

    def wait(refs, ss, rs, ls):
        copy(refs, ss, rs).wait()

    return _Task([g4], [], start, wait, fresh=[_S((N_CHIP,) + g.shape[1:], g.dtype)], nsem=1)


def _peer(x, y, c, m):
    return x ^ (m >> 2), y ^ ((m >> 1) & 1), c ^ (m & 1)


def _bcast_task(pack):
    def copies(refs, ss, rs):
        x, y, c = _place()
        me = 4 * x + 2 * y + c
        return [pltpu.make_async_remote_copy(
            src_ref=refs[0], dst_ref=refs[1].at[me], send_sem=ss.at[m - 1], recv_sem=rs.at[m - 1],
            device_id=_peer(x, y, c, m), device_id_type=MESH) for m in range(1, N_DEV)]

    def local(refs, ls):
        x, y, c = _place()
        return pltpu.make_async_copy(refs[0], refs[1].at[4 * x + 2 * y + c], ls.at[0])

    def start(refs, ss, rs, ls):
        local(refs, ls).start()
        for cp in copies(refs, ss, rs):
            cp.start()

    def wait(refs, ss, rs, ls):
        x, y, c = _place()
        for m in range(1, N_DEV):
            px, py, pc = _peer(x, y, c, m)
            pltpu.make_async_remote_copy(
                src_ref=refs[0], dst_ref=refs[1].at[4 * px + 2 * py + pc], send_sem=ss.at[m - 1],
                recv_sem=rs.at[m - 1], device_id=(px, py, pc), device_id_type=MESH).wait_recv()
        for cp in copies(refs, ss, rs):
            cp.wait_send()
        local(refs, ls).wait()

    return _Task([pack], [], start, wait, fresh=[_S((N_DEV,) + pack.shape, pack.dtype)], nsem=N_DEV - 1)


def _sum_packs(packs, name):
    _, R, L = packs.shape

    def body(p_ref, o_ref):
        acc = p_ref[0]
        for k in range(1, N_DEV):
            acc = acc + p_ref[k]
        o_ref[...] = acc

    return pl.pallas_call(body, name=name, out_shape=_S((R, L), packs.dtype), in_specs=[VMEM_SPEC],
                          out_specs=VMEM_SPEC, compiler_params=_cparams())(packs)


def _swap_halves(grads, name):
    n = len(grads)
    g4 = [g.reshape((N_CHIP, 2) + g.shape[1:]) for g in grads]

    def body(*refs):
        ins, outs = refs[:n], refs[n:2 * n]
        send_sems, recv_sems = refs[2 * n:]
        x, y, c = _place()
        cps = [pltpu.make_async_remote_copy(
            src_ref=ins[a].at[:, 1 - c], dst_ref=outs[a],
            send_sem=send_sems.at[a], recv_sem=recv_sems.at[a],
            device_id=(x, y, 1 - c), device_id_type=MESH) for a in range(n)]
        for cp in cps:
            cp.start()
        for cp in cps:
            cp.wait()

    return pl.pallas_call(
        body, name=name,
        out_shape=tuple(_S((N_CHIP,) + g.shape[1:], g.dtype) for g in grads),
        in_specs=[ANY] * n, out_specs=tuple([ANY] * n),
        scratch_shapes=[pltpu.SemaphoreType.DMA((n,)), pltpu.SemaphoreType.DMA((n,))],
    )(*g4)


def _add_halves(g, landed, place, name):
    _, r, cc = g.shape
    g4 = g.reshape(N_CHIP, 2, r, cc)
    tr = _tile(r, 512, HALO)

    def body(s_ref, g_ref, l_ref, o_ref, land_ref):
        q = pl.program_id(1)
        v = (g_ref[...].astype(F32) + l_ref[...].astype(F32)).astype(BF16)
        o_ref[...] = v

        @pl.when(q == s_ref[1])
        def _():
            land_ref[...] = v

    return pl.pallas_call(
        body, name=name, out_shape=(_S((N_CHIP, r, cc), BF16), _S((N_CHIP, r, cc), BF16)),
        grid_spec=pltpu.PrefetchScalarGridSpec(
            num_scalar_prefetch=1, grid=(r // tr, N_CHIP),
            in_specs=[pl.BlockSpec((None, None, tr, cc), lambda i, q, s: (q, s[0], i, 0)),
                      pl.BlockSpec((None, tr, cc), lambda i, q, s: (q, i, 0))],
            out_specs=(pl.BlockSpec((None, tr, cc), lambda i, q, s: (q, i, 0)),
                       pl.BlockSpec((None, tr, cc), lambda i, q, s: (s[1], i, 0)))),
        compiler_params=_cparams(("arbitrary", "arbitrary")),
    )(place, g4, landed)


def _all_reduce_small(pack, name):
    R = pack.shape[0]

    def body(p_ref, o_ref, buf, send_sems, recv_sems):
        x, y, c = _place()
        me = 4 * x + 2 * y + c
        buf[me] = p_ref[...]
        cps = []
        for k in range(N_DEV - 1):
            m = k + 1
            peer = (x ^ (m >> 2), y ^ ((m >> 1) & 1), c ^ (m & 1))
            cps.append(pltpu.make_async_remote_copy(
                src_ref=p_ref, dst_ref=buf.at[me], send_sem=send_sems.at[k], recv_sem=recv_sems.at[k],
                device_id=peer, device_id_type=MESH))
        for cp in cps:
            cp.start()
        for k in range(N_DEV - 1):
            m = k + 1
            peer_idx = 4 * (x ^ (m >> 2)) + 2 * (y ^ ((m >> 1) & 1)) + (c ^ (m & 1))
            pltpu.make_async_remote_copy(
                src_ref=p_ref, dst_ref=buf.at[peer_idx], send_sem=send_sems.at[k], recv_sem=recv_sems.at[k],
                device_id=(x, y, c), device_id_type=MESH).wait_recv()
        for cp in cps:
            cp.wait_send()
        acc = buf[0]
        for k in range(1, N_DEV):
            acc = acc + buf[k]
        o_ref[...] = acc

    return pl.pallas_call(
        body, name=name, out_shape=_S((R, LANES), F32),
        in_specs=[VMEM_SPEC], out_specs=VMEM_SPEC,
        scratch_shapes=[pltpu.VMEM((N_DEV, R, LANES), F32), pltpu.SemaphoreType.DMA((N_DEV - 1,)),
                        pltpu.SemaphoreType.DMA((N_DEV - 1,))],
        compiler_params=_cparams(),
    )(pack)


def _adamw_math(w, g, m, v):
    m = ADAM_B1 * m + (1.0 - ADAM_B1) * g
    v = ADAM_B2 * v + (1.0 - ADAM_B2) * (g * g)
    m_hat = m / (1.0 - ADAM_B1 ** ADAM_STEP)
    v_hat = v / (1.0 - ADAM_B2 ** ADAM_STEP)
    delta = -ADAM_LR * (m_hat / (jnp.sqrt(v_hat) + ADAM_EPS) + ADAM_WD * w)
    return delta, m, v


def _adamw_big(parts, w, m, v, name):
    r, cc = w.shape
    tr = _tile(r, 128, HALO)

    def body(p_ref, w_ref, m_ref, v_ref, g_ref, d_ref, nm_ref, nv_ref):
        g = p_ref[0].astype(F32)
        for q in range(1, N_CHIP):
            g = g + p_ref[q].astype(F32)
        g_ref[...] = g
        d_ref[...], nm_ref[...], nv_ref[...] = _adamw_math(w_ref[...], g, m_ref[...], v_ref[...])

    blk = pl.BlockSpec((tr, cc), lambda i: (i, 0))
    return pl.pallas_call(
        body, name=name, out_shape=tuple(_S((r, cc), F32) for _ in range(4)), grid=(r // tr,),
        in_specs=[pl.BlockSpec((N_CHIP, tr, cc), lambda i: (0, i, 0)), blk, blk, blk],
        out_specs=(blk, blk, blk, blk), compiler_params=_cparams(("parallel",)),
    )(parts, w, m, v)


def _adamw_small(ws, gs, ms, vs, name):
    n = len(ws)

    def body(*refs):
        w_r, g_r, m_r, v_r = refs[:n], refs[n:2 * n], refs[2 * n:3 * n], refs[3 * n:4 * n]
        d_r, nm_r, nv_r = refs[4 * n:5 * n], refs[5 * n:6 * n], refs[6 * n:7 * n]
        for k in range(n):
            d_r[k][...], nm_r[k][...], nv_r[k][...] = _adamw_math(w_r[k][...], g_r[k][...], m_r[k][...], v_r[k][...])

    shapes = tuple(_S(w.shape, F32) for w in ws)
    outs = pl.pallas_call(
        body, name=name, out_shape=shapes * 3,
        in_specs=[VMEM_SPEC] * (4 * n), out_specs=tuple([VMEM_SPEC] * (3 * n)),
        compiler_params=_cparams(),
    )(*ws, *gs, *ms, *vs)
    return outs[:n], outs[n:2 * n], outs[2 * n:]


def _block_diag(w, heads_per_block):
    H, hd, _ = w.shape
    nb = H // heads_per_block
    eye = jnp.eye(heads_per_block, dtype=w.dtype)
    w4 = w.reshape(nb, heads_per_block, hd, hd)
    return jnp.einsum("nhab,hg->nhagb", w4, eye).reshape(nb, heads_per_block * hd, heads_per_block * hd)


def _diag_blocks(bd, heads_per_block, hd):
    nb = bd.shape[0]
    b5 = bd.reshape(nb, heads_per_block, hd, heads_per_block, hd)
    return jnp.stack([b5[:, h, :, h, :] for h in range(heads_per_block)], axis=1).reshape(nb * heads_per_block, hd, hd)


def _as_rows(a):
    if a.ndim == 1:
        return a.reshape(-1, LANES) if a.shape[0] % LANES == 0 else a.reshape(1, -1)
    if a.ndim == 3:
        return a.reshape(-1, LANES) if (a.size % LANES == 0) else a.reshape(a.shape[0] * a.shape[1], a.shape[2])
    return a


def kernel(x, g_mix, w_in, lru_conv_w, lru_conv_b, lru_wa, lru_ba, lru_wx, lru_bx, lru_lambda, lru_w_out, sc_conv_w, sc_w_out, w_o, g_ffn, ffn_w_up, ffn_conv_w, ffn_w_down, g_final, loss_target, m_g_mix, m_w_in, m_lru_conv_w, m_lru_conv_b, m_lru_wa, m_lru_ba, m_lru_wx, m_lru_bx, m_lru_lambda, m_lru_w_out, m_sc_conv_w, m_sc_w_out, m_w_o, m_g_ffn, m_ffn_w_up, m_ffn_conv_w, m_ffn_w_down, m_g_final, v_g_mix, v_w_in, v_lru_conv_w, v_lru_conv_b, v_lru_wa, v_lru_ba, v_lru_wx, v_lru_bx, v_lru_lambda, v_lru_w_out, v_sc_conv_w, v_sc_w_out, v_w_o, v_g_ffn, v_ffn_w_up, v_ffn_conv_w, v_ffn_w_down, v_g_final):
    T, D = x.shape[1], x.shape[2]
    d_lru = lru_lambda.shape[0]
    d_sc = sc_conv_w.shape[1] * N_DEV
    F = ffn_w_down.shape[0] * N_DEV
    H = lru_wa.shape[0]
    assert d_lru == d_sc and H * HEAD_DIM == d_lru
    xs = x.reshape(T, D)
    tgt = loss_target.reshape(T, D)
    my_x, my_y, my_c = _place()
    me = 4 * my_x + 2 * my_y + my_c

    big = [w_in, lru_w_out, sc_w_out, w_o, ffn_w_up, ffn_w_down]
    big_names = ["w_in", "lru_w_out", "sc_w_out", "w_o", "ffn_w_up", "ffn_w_down"]
    me_idx = jnp.reshape(me, (1,)).astype(jnp.int32)
    big_bf = [_cast_into_slot(w, me_idx, "cast_" + nm) for w, nm in zip(big, big_names)]
    pad_rows = lambda a: jnp.pad(a, ((0, SUB - a.shape[0]), (0, 0)))
    gathered = _all_gather(big_bf + [pad_rows(lru_conv_w), pad_rows(sc_conv_w), pad_rows(ffn_conv_w)],
                           [True] * 6 + [False] * 3,
                           [True, True, True, False, False, False, True, True, True], "all_gather_first")
    W_in, W_lo, W_so, W_o8, W_up, W_dn8 = gathered[:6]
    full_cols = lambda g, kw: g[:, :kw, :].transpose(1, 0, 2).reshape(kw, -1)
    cw_lru = full_cols(gathered[6], lru_conv_w.shape[0])
    cw_sc = full_cols(gathered[7], sc_conv_w.shape[0])
    cw_ffn = full_cols(gathered[8], ffn_conv_w.shape[0])

    C = _tile(d_lru, C_LRU)
    hpb = C // HEAD_DIM
    wa_bd = _block_diag(lru_wa, hpb).astype(BF16)
    wx_bd = _block_diag(lru_wx, hpb).astype(BF16)
    cb, ba, bx, lam = (a.reshape(1, d_lru) for a in (lru_conv_b, lru_ba, lru_bx, lru_lambda))

    h1 = _rms_fwd(xs, g_mix, "rms_mix")
    k8 = W_up.shape[1] // 8
    p, ((W_o8,), (W_up,)) = _mm_nn(
        h1, W_in, out_dtype=BF16, name="mm_in",
        tasks=[_gather_task(W_o8, ici=ALL_ROWS), _gather_task(W_up, ici=(0, 5 * k8))])
    hs, yl_pre, ((W_o8,), (W_up,)) = _lru_fwd(
        p, cw_lru, cb, wa_bd, ba, wx_bd, bx, lam, name="lru_fwd",
        tasks=[_gather_task(W_o8, fwd=ALL_ROWS), _gather_task(W_up, ici=(5 * k8, 3 * k8), fwd=(0, 5 * k8))])
    ys_pre = _sc_fwd(p, cw_sc, d=d_sc, name="sc_fwd")
    y_lru, ((W_up,),) = _mm_nn(yl_pre, W_lo, out_dtype=BF16, name="mm_lru_out", tm=2048,
                               tasks=[_gather_task(W_up, fwd=(5 * k8, 3 * k8))])
    y_sc = _mm_nn(ys_pre, W_so, out_dtype=BF16, name="mm_sc_out", tm=2048)
    gate0 = 2 * d_lru + 3 * d_sc
    merged = _merge_fwd(p, y_lru, y_sc, col0=gate0, name="merge_fwd")
    W_o = W_o8.reshape(1, D, D)
    x1 = _mm_nn(merged, W_o, out_dtype=F32, residual=xs, name="mm_o")
    h2 = _rms_fwd(x1, g_ffn, "rms_ffn")
    uu, ((W_dn8,),) = _mm_nn(h2, W_up, out_dtype=BF16, name="mm_up", tasks=[_gather_task(W_dn8, ici=ALL_ROWS)])
    act, ((W_dn8,),) = _ffn_act_fwd(uu, cw_ffn, name="ffn_act_fwd", tasks=[_gather_task(W_dn8, fwd=ALL_ROWS)])
    W_dn = W_dn8.reshape(1, F, D)
    x2 = _mm_nn(act, W_dn, out_dtype=F32, residual=x1, name="mm_down", tn=512, tk=F)
    dx2, dx2b, loss_part, dg_final = _loss_head(x2, g_final, tgt, "loss_head")

    place = jnp.stack([my_c, 2 * my_x + my_y]).astype(jnp.int32)

    def pack_rows(arrs):
        flat = jnp.concatenate([a.reshape(-1) for a in arrs])
        rows = -(-flat.shape[0] // (SUB * LANES)) * SUB
        return jnp.pad(flat, (0, rows * LANES - flat.shape[0])).reshape(rows, LANES)

    def unpack_rows(pack, arrs):
        flat, out, o = pack.reshape(-1), [], 0
        for a in arrs:
            out.append(flat[o:o + a.size].reshape(a.shape))
            o += a.size
        return out

    dact = _mm_nt(dx2b, W_dn, out_dtype=BF16, name="mm_down_dx", tko=1408)
    gW_dn = _mm_tn(act, dx2b, 1, out_dtype=BF16, name="mm_down_dw", tk=1408).reshape(N_DEV, F // N_DEV, D)
    duu, dcw_ffn_g, dcw_ffn_v = _ffn_act_bwd(uu, dact, cw_ffn, name="ffn_act_bwd")
    dh2, ((land_dn,),) = _mm_nt(duu, W_up, out_dtype=BF16, name="mm_up_dx", tasks=[_swap_task(gW_dn)])
    parts_dn = _add_halves(gW_dn, land_dn, place, "rs_add_ffn_w_down")
    gW_up, ((mine_dn,),) = _mm_tn(h2, duu, N_DEV, out_dtype=BF16, name="mm_up_dw", tk=2048,
                                  tasks=[_exchange_task(*parts_dn)])
    dx1, dx1b, dg_ffn = _rms_bwd(x1, g_ffn, dh2, dx2, "rms_ffn_bwd")
    dmerged, ((land_up,),) = _mm_nt(dx1b, W_o, out_dtype=BF16, name="mm_o_dx", tasks=[_swap_task(gW_up)])
    parts_up = _add_halves(gW_up, land_up, place, "rs_add_ffn_w_up")
    gW_o = _mm_tn(merged, dx1b, 1, out_dtype=BF16, name="mm_o_dw").reshape(N_DEV, D // N_DEV, D)
    dp, dy_lru, dy_sc = _merge_bwd(p, y_lru, y_sc, dmerged, col0=gate0, name="merge_bwd")
    dyl_pre, ((land_o,),) = _mm_nt(dy_lru, W_lo, out_dtype=BF16, name="mm_lru_out_dx", tasks=[_swap_task(gW_o)])
    parts_o = _add_halves(gW_o, land_o, place, "rs_add_w_o")
    gW_lo = _mm_tn(yl_pre, dy_lru, N_DEV, out_dtype=BF16, name="mm_lru_out_dw")
    dys_pre, ((land_lo,),) = _mm_nt(dy_sc, W_so, out_dtype=BF16, name="mm_sc_out_dx", tasks=[_swap_task(gW_lo)])
    parts_lo = _add_halves(gW_lo, land_lo, place, "rs_add_lru_w_out")
    gW_so = _mm_tn(ys_pre, dy_sc, N_DEV, out_dtype=BF16, name="mm_sc_out_dw")
    dp, dcw_sc = _sc_bwd(p, dys_pre, dp, cw_sc, d=d_sc, name="sc_bwd")
    r_up = parts_up[0].shape[1] // 2
    dp, dcw_lru, dcb, dwa_bd, dba, dwx_bd, dbx, dlam, ((land_up,), (mine_o,), (mine_lo,)) = _lru_bwd(
        p, hs, dyl_pre, dp, cw_lru, cb, wa_bd, ba, wx_bd, bx, lam, name="lru_bwd",
        tasks=[_exchange_task(*parts_up, rows=(0, r_up)), _exchange_task(*parts_o), _exchange_task(*parts_lo)])

    dwa = _diag_blocks(dwa_bd, hpb, HEAD_DIM)
    dwx = _diag_blocks(dwx_bd, hpb, HEAD_DIM)
    dcw_ffn = jnp.concatenate([dcw_ffn_g, dcw_ffn_v], axis=1)
    small_full = [dcw_lru, dcb, dwa, dba, dwx, dbx, dlam, dcw_sc, dg_ffn, dcw_ffn, dg_final]
    gW_in, ((mine_up,), (land_so,), (packs,)) = _mm_tn(
        h1, dp, N_DEV, out_dtype=BF16, name="mm_in_dw", tk=2048,
        tasks=[_exchange_task(parts_up[0], land_up, rows=(r_up, r_up)), _swap_task(gW_so),
               _bcast_task(pack_rows(small_full))])
    parts_so = _add_halves(gW_so, land_so, place, "rs_add_sc_w_out")
    (land_in,) = _swap_halves([gW_in], "rs_swap_w_in")
    parts_in = _add_halves(gW_in, land_in, place, "rs_add_w_in")
    dh1, ((mine_in,), (mine_so,)) = _mm_nt(dp, W_in, out_dtype=BF16, name="mm_in_dx",
                                           tasks=[_exchange_task(*parts_in), _exchange_task(*parts_so)])
    grad_x, _, dg_mix = _rms_bwd(xs, g_mix, dh1, dx1, "rms_mix_bwd")

    mine = [mine_in, mine_lo, mine_so, mine_o, mine_up, mine_dn]
    big_m = [m_w_in, m_lru_w_out, m_sc_w_out, m_w_o, m_ffn_w_up, m_ffn_w_down]
    big_v = [v_w_in, v_lru_w_out, v_sc_w_out, v_w_o, v_ffn_w_up, v_ffn_w_down]
    big_out = {nm: _adamw_big(pt, w, m, v, "adamw_" + nm)
               for nm, pt, w, m, v in zip(big_names, mine, big, big_m, big_v)}

    (scw_lru, scb, swa, sba, swx, sbx, slam, scw_sc, sg_ffn, scw_ffn, sg_final) = unpack_rows(
        _sum_packs(packs, "sum_small"), small_full)
    (sg_mix,) = unpack_rows(_all_reduce_small(pack_rows([dg_mix]), "all_reduce_g_mix"), [dg_mix])

    def my_cols(a):
        n = a.shape[1] // N_DEV
        return lax.dynamic_slice_in_dim(a, me * n, n, axis=1)

    small_names = ["g_mix", "lru_conv_w", "lru_conv_b", "lru_wa", "lru_ba", "lru_wx", "lru_bx", "lru_lambda",
                   "sc_conv_w", "g_ffn", "ffn_conv_w", "g_final"]
    small_w = [g_mix, lru_conv_w, lru_conv_b, lru_wa, lru_ba, lru_wx, lru_bx, lru_lambda, sc_conv_w, g_ffn,
               ffn_conv_w, g_final]
    small_m = [m_g_mix, m_lru_conv_w, m_lru_conv_b, m_lru_wa, m_lru_ba, m_lru_wx, m_lru_bx, m_lru_lambda,
               m_sc_conv_w, m_g_ffn, m_ffn_conv_w, m_g_final]
    small_v = [v_g_mix, v_lru_conv_w, v_lru_conv_b, v_lru_wa, v_lru_ba, v_lru_wx, v_lru_bx, v_lru_lambda,
               v_sc_conv_w, v_g_ffn, v_ffn_conv_w, v_g_final]
    small_g = [sg_mix.reshape(D), my_cols(scw_lru), scb.reshape(d_lru), swa, sba.reshape(d_lru), swx,
               sbx.reshape(d_lru), slam.reshape(d_lru), my_cols(scw_sc), sg_ffn.reshape(D), my_cols(scw_ffn),
               sg_final.reshape(D)]
    sd, snm, snv = _adamw_small([_as_rows(a) for a in small_w], [_as_rows(a) for a in small_g],
                                [_as_rows(a) for a in small_m], [_as_rows(a) for a in small_v], "adamw_small")
    small_out = {nm: (g, d.reshape(w.shape), nm_.reshape(w.shape), nv_.reshape(w.shape))
                 for nm, w, g, d, nm_, nv_ in zip(small_names, small_w, small_g, sd, snm, snv)}

    loss = lax.psum(loss_part[0, 0], AXES)
    order = ["g_mix", "w_in", "lru_conv_w", "lru_conv_b", "lru_wa", "lru_ba", "lru_wx", "lru_bx", "lru_lambda",
             "lru_w_out", "sc_conv_w", "sc_w_out", "w_o", "g_ffn", "ffn_w_up", "ffn_conv_w", "ffn_w_down", "g_final"]
    res = {**big_out, **small_out}
    return (loss, grad_x.reshape(x.shape),
            *[res[nm][0] for nm in order], *[res[nm][1] for nm in order],
            *[res[nm][2] for nm in order], *[res[nm][3] for nm in order])
```

```python
import functools
import math

import jax
import jax.numpy as jnp
from jax import lax
from jax.experimental import pallas as pl
from jax.experimental.pallas import tpu as pltpu

F32, BF16 = jnp.float32, jnp.bfloat16
MESH = pl.DeviceIdType.MESH
N_DEV = 8
N_CHIP = 4
AXES = ("x", "y", "c")

EPS = 1e-6
LRU_C = 8.0
HEAD_DIM = 64
ADAM_LR, ADAM_B1, ADAM_B2, ADAM_EPS, ADAM_WD, ADAM_STEP = 0.001, 0.9, 0.999, 1e-08, 0.01, 10

VMEM_LIMIT = 48 * 1024 * 1024
LANES = 128
SUB = 8
HALO = 16
TB = 512
C_LRU = 256
C_EW = 512
TM, TN, TK = 512, 1536, 2048


def _tile(n, pref, align=LANES):
    best = None
    for d in range(align, min(n, pref) + 1, align):
        if n % d == 0:
            best = d
    return best or n


def _cparams(sem=None, vmem=VMEM_LIMIT):
    kw = dict(vmem_limit_bytes=vmem)
    if sem is not None:
        kw["dimension_semantics"] = sem
    return pltpu.CompilerParams(**kw)


def _S(shape, dtype):
    return jax.ShapeDtypeStruct(shape, dtype)


ANY = pl.BlockSpec(memory_space=pl.ANY)
VMEM_SPEC = pl.BlockSpec(memory_space=pltpu.VMEM)


class _Task:
    def __init__(self, arrays, aliased, start, wait, fresh=(), nsem=3):
        self.arrays, self.aliased, self.start, self.wait = arrays, aliased, start, wait
        self.fresh, self.nsem = list(fresh), nsem


def _call(name, grid, compute, in_specs, args, out_shape, out_specs, scratch, tasks=(), own_aliases=None):
    n_in, n_out, n_scr = len(args), len(out_shape), len(scratch)
    x_in, x_out, aliases, where = [], [], dict(own_aliases or {}), []
    for t in tasks:
        places = []
        for k, arr in enumerate(t.arrays):
            if k in t.aliased:
                aliases[n_in + len(x_in)] = n_out + len(x_out)
                places.append(("out", len(x_out)))
                x_out.append(_S(arr.shape, arr.dtype))
            else:
                places.append(("in", len(x_in)))
            x_in.append(arr)
        for shp in t.fresh:
            places.append(("out", len(x_out)))
            x_out.append(shp)
        where.append(places)
    n_xi, n_xo = len(x_in), len(x_out)

    def body(*refs):
        ins, xi = refs[:n_in], refs[n_in:n_in + n_xi]
        o0 = n_in + n_xi
        outs, xo = refs[o0:o0 + n_out], refs[o0 + n_out:o0 + n_out + n_xo]
        s0 = o0 + n_out + n_xo
        scr, sems = refs[s0:s0 + n_scr], refs[s0 + n_scr:]
        ids = [pl.program_id(a) for a in range(len(grid))]

        def task_refs(ti):
            return [xo[i] if kind == "out" else xi[i] for kind, i in where[ti]]

        if tasks:
            first = functools.reduce(jnp.logical_and, [i == 0 for i in ids])

            @pl.when(first)
            def _():
                for ti, t in enumerate(tasks):
                    t.start(task_refs(ti), *sems[3 * ti:3 * ti + 3])

        compute(*ins, *outs, *scr)
        if tasks:
            last = functools.reduce(jnp.logical_and, [i == g - 1 for i, g in zip(ids, grid)])

            @pl.when(last)
            def _():
                for ti, t in enumerate(tasks):
                    t.wait(task_refs(ti), *sems[3 * ti:3 * ti + 3])

    sem_shapes = []
    for t in tasks:
        sem_shapes += [pltpu.SemaphoreType.DMA((t.nsem,)), pltpu.SemaphoreType.DMA((t.nsem,)),
                       pltpu.SemaphoreType.DMA((1,))]
    res = pl.pallas_call(
        body, name=name, grid=grid,
        in_specs=list(in_specs) + [ANY] * n_xi,
        out_specs=tuple(out_specs) + (ANY,) * n_xo,
        out_shape=tuple(out_shape) + tuple(x_out),
        scratch_shapes=list(scratch) + sem_shapes,
        input_output_aliases=aliases,
        compiler_params=_cparams(("arbitrary",) * len(grid)),
    )(*args, *x_in)
    outs, passed, o = res[:n_out], [], n_out
    for places in where:
        k = sum(1 for kind, _ in places if kind == "out")
        passed.append(list(res[o:o + k]))
        o += k
    return outs, passed


def _mm_nn(a, w3, *, out_dtype, name, residual=None, tm=TM, tn=TN, tk=TK, tasks=()):
    M, K = a.shape
    G, _, n = w3.shape
    tm, tn, tk = _tile(M, tm, SUB), _tile(n, tn), _tile(K, tk)
    nj, nk = n // tn, K // tk

    def compute(*refs):
        if residual is None:
            a_ref, w_ref, o_ref = refs[:3]
            r_ref = None
        else:
            a_ref, w_ref, r_ref, o_ref = refs[:4]

        def finish(r):
            if r_ref is not None:
                r = r + r_ref[...]
            o_ref[...] = r.astype(o_ref.dtype)

        if nk == 1:
            finish(jnp.dot(a_ref[...], w_ref[...], preferred_element_type=F32))
            return
        acc = refs[-1]
        k = pl.program_id(3)

        @pl.when(k == 0)
        def _():
            acc[...] = jnp.zeros_like(acc)

        acc[...] += jnp.dot(a_ref[...], w_ref[...], preferred_element_type=F32)

        @pl.when(k == nk - 1)
        def _():
            finish(acc[...])

    in_specs = [pl.BlockSpec((tm, tk), lambda g, j, i, k: (i, k)),
                pl.BlockSpec((None, tk, tn), lambda g, j, i, k: (g, k, j))]
    args = [a, w3]
    if residual is not None:
        in_specs.append(pl.BlockSpec((tm, tn), lambda g, j, i, k: (i, g * nj + j)))
        args.append(residual)
    outs, passed = _call(
        name, (G, nj, M // tm, nk), compute, in_specs, args, [_S((M, G * n), out_dtype)],
        [pl.BlockSpec((tm, tn), lambda g, j, i, k: (i, g * nj + j))],
        [] if nk == 1 else [pltpu.VMEM((tm, tn), F32)], tasks)
    return (outs[0], passed) if tasks else outs[0]


def _mm_nt(dy, w3, *, out_dtype, name, tm=1024, tko=1024, tn=TN, tasks=()):
    M, _ = dy.shape
    G, K, n = w3.shape
    tm, tko, tn = _tile(M, tm, SUB), _tile(K, tko), _tile(n, tn)
    nj = n // tn
    nr = G * nj

    def compute(dy_ref, w_ref, o_ref, acc):
        r = pl.program_id(2)

        @pl.when(r == 0)
        def _():
            acc[...] = jnp.zeros_like(acc)

        acc[...] += lax.dot_general(dy_ref[...], w_ref[...], (((1,), (1,)), ((), ())),
                                    preferred_element_type=F32)

        @pl.when(r == nr - 1)
        def _():
            o_ref[...] = acc[...].astype(o_ref.dtype)

    outs, passed = _call(
        name, (K // tko, M // tm, nr), compute,
        [pl.BlockSpec((tm, tn), lambda ko, i, r: (i, r)),
         pl.BlockSpec((None, tko, tn), lambda ko, i, r: (r // nj, ko, r % nj))],
        [dy, w3], [_S((M, K), out_dtype)], [pl.BlockSpec((tm, tko), lambda ko, i, r: (i, ko))],
        [pltpu.VMEM((tm, tko), F32)], tasks)
    return (outs[0], passed) if tasks else outs[0]


def _mm_tn(a, dy, G, *, out_dtype, name, tk=1024, tn=TN, tt=1024, tasks=()):
    M, K = a.shape
    n = dy.shape[1] // G
    tk, tn, tt = _tile(K, tk), _tile(n, tn), _tile(M, tt, SUB)
    nj, nt = n // tn, M // tt

    def compute(a_ref, dy_ref, o_ref, acc):
        t = pl.program_id(3)

        @pl.when(t == 0)
        def _():
            acc[...] = jnp.zeros_like(acc)

        acc[...] += lax.dot_general(a_ref[...], dy_ref[...], (((0,), (0,)), ((), ())),
                                    preferred_element_type=F32)

        @pl.when(t == nt - 1)
        def _():
            o_ref[...] = acc[...].astype(o_ref.dtype)

    outs, passed = _call(
        name, (G, nj, K // tk, nt), compute,
        [pl.BlockSpec((tt, tk), lambda g, j, k, t: (t, k)),
         pl.BlockSpec((tt, tn), lambda g, j, k, t: (t, g * nj + j))],
        [a, dy], [_S((G, K, n), out_dtype)], [pl.BlockSpec((None, tk, tn), lambda g, j, k, t: (g, k, j))],
        [pltpu.VMEM((tk, tn), F32)], tasks)
    return (outs[0], passed) if tasks else outs[0]


def _mm_small(kind, a, b, w3, *, name, tm=1024, tasks=()):
    G, K, n = w3.shape
    M = (a if a is not None else b).shape[0]
    tm = _tile(M, tm, HALO)
    nt = M // tm
    w_spec = pl.BlockSpec((G, K, n), lambda i: (0, 0, 0))
    a_spec = pl.BlockSpec((tm, K), lambda i: (i, 0))
    b_spec = pl.BlockSpec((tm, G * n), lambda i: (i, 0))
    cols = lambda g: slice(g * n, (g + 1) * n)
    if kind == "nn":
        def compute(a_ref, w_ref, o_ref):
            av = a_ref[...]
            for g in range(G):
                o_ref[:, cols(g)] = jnp.dot(av, w_ref[g], preferred_element_type=F32).astype(o_ref.dtype)

        outs, passed = _call(name, (nt,), compute, [a_spec, w_spec], [a, w3], [_S((M, G * n), BF16)], [b_spec], [], tasks)
    elif kind == "nt":
        def compute(b_ref, w_ref, o_ref):
            acc = None
            for g in range(G):
                part = lax.dot_general(b_ref[:, cols(g)], w_ref[g], (((1,), (1,)), ((), ())),
                                       preferred_element_type=F32)
                acc = part if acc is None else acc + part
            o_ref[...] = acc.astype(o_ref.dtype)

        outs, passed = _call(name, (nt,), compute, [b_spec, w_spec], [b, w3], [_S((M, K), BF16)], [a_spec], [], tasks)
    else:
        def compute(a_ref, b_ref, o_ref, acc):
            i = pl.program_id(0)

            @pl.when(i == 0)
            def _():
                acc[...] = jnp.zeros_like(acc)

            at = a_ref[...].T
            for g in range(G):
                acc[g] += jnp.dot(at, b_ref[:, cols(g)], preferred_element_type=F32)

            @pl.when(i == nt - 1)
            def _():
                o_ref[...] = acc[...].astype(o_ref.dtype)

        outs, passed = _call(name, (nt,), compute, [a_spec, b_spec], [a, b], [_S((G, K, n), BF16)], [w_spec],
                             [pltpu.VMEM((G, K, n), F32)], tasks)
    return (outs[0], passed) if tasks else outs[0]


def _cast_into_slot(w, me_idx, name):
    R, C = w.shape
    tr = _tile(R, 512, HALO)

    def body(me_ref, w_ref, o_ref):
        del me_ref
        o_ref[...] = w_ref[...].astype(BF16)

    return pl.pallas_call(
        body, name=name, out_shape=_S((N_DEV, R, C), BF16),
        grid_spec=pltpu.PrefetchScalarGridSpec(
            num_scalar_prefetch=1, grid=(R // tr,),
            in_specs=[pl.BlockSpec((tr, C), lambda i, me_ref: (i, 0))],
            out_specs=pl.BlockSpec((None, tr, C), lambda i, me_ref: (me_ref[0], i, 0))),
        compiler_params=_cparams(("parallel",)),
    )(me_idx, w)


def _down(cur, prev8, j):
    return pltpu.roll(jnp.concatenate([prev8, cur], axis=0), j, 0)[SUB:, :]


def _up(cur, next8, j):
    n = cur.shape[0] + SUB
    return pltpu.roll(jnp.concatenate([cur, next8], axis=0), n - j, 0)[:cur.shape[0], :]


def _shifted_down(x, prev8, n):
    full = jnp.concatenate([prev8, x], axis=0)
    return [x] + [pltpu.roll(full, s, 0)[SUB:, :] for s in range(1, n)]


def _shifted_up(x, next8, n):
    m = x.shape[0] + SUB
    full = jnp.concatenate([x, next8], axis=0)
    return [x] + [pltpu.roll(full, m - s, 0)[:x.shape[0], :] for s in range(1, n)]


def _taps(sh, w_ref):
    kw = w_ref.shape[0]
    y = sh[0] * w_ref[pl.ds(kw - 1, 1), :]
    for k in range(kw - 1):
        y = y + sh[kw - 1 - k] * w_ref[pl.ds(k, 1), :]
    return y


def _conv(x, prev8, w_ref):
    return _taps(_shifted_down(x, prev8, w_ref.shape[0]), w_ref)


def _conv_t(dy, next8, w_ref):
    return _taps(_shifted_up(dy, next8, w_ref.shape[0]), w_ref)


def _conv_dw(dw_ref, dy, x, prev8, first):
    kw = dw_ref.shape[0]

    @pl.when(first)
    def _():
        dw_ref[...] = jnp.zeros_like(dw_ref)

    for k in range(kw):
        xs = x if k == kw - 1 else _down(x, prev8, kw - 1 - k)
        dw_ref[pl.ds(k, 1), :] += jnp.sum(dy * xs, axis=0, keepdims=True)


def _acc(ref, val, first):
    @pl.when(first)
    def _():
        ref[...] = jnp.zeros_like(ref)

    ref[...] += val


def _acc_row(ref, val, first):
    _acc(ref, jnp.sum(val, axis=0, keepdims=True), first)


def _prev8(h_ref, t):
    return jnp.where(t > 0, h_ref[...].astype(F32)[HALO - SUB:, :], 0.0)


def _next8(h_ref, is_last):
    return jnp.where(is_last, 0.0, h_ref[...].astype(F32)[:SUB, :])


_GELU_K0 = math.sqrt(2.0 / math.pi)
_GELU_K1 = 0.044715


def _gelu_and_grad(x):
    x2 = x * x
    th = jnp.tanh(_GELU_K0 * x * (1.0 + _GELU_K1 * x2))
    g = 0.5 * x * (1.0 + th)
    dg = 0.5 * (1.0 + th) + 0.5 * x * (1.0 - th * th) * (_GELU_K0 * (1.0 + 3.0 * _GELU_K1 * x2))
    return g, dg


def _neg_expm1(z):
    series = -z * (1.0 + z * (0.5 + z * (1.0 / 6.0 + z * (1.0 / 24.0))))
    return jnp.where(z > -0.03, series, 1.0 - jnp.exp(z))


def _store_block(stage_ref, dst_hbm, sem, row0, col0):
    tb, c = stage_ref.shape
    return pltpu.make_async_copy(stage_ref, dst_hbm.at[pl.ds(row0, tb), pl.ds(col0, c)], sem)


def _halo_prev_map(hb, col_fn):
    return lambda c, t: (jnp.maximum(t * hb - 1, 0), col_fn(c))


def _rms_fwd(x, g, name):
    T, D = x.shape
    tb = _tile(T, TB, SUB)

    def body(x_ref, g_ref, o_ref):
        xv = x_ref[...]
        rstd = lax.rsqrt(jnp.mean(xv * xv, axis=-1, keepdims=True) + EPS)
        o_ref[...] = (xv * rstd * g_ref[...]).astype(BF16)

    return pl.pallas_call(
        body, name=name, out_shape=_S((T, D), BF16), grid=(T // tb,),
        in_specs=[pl.BlockSpec((tb, D), lambda i: (i, 0)), pl.BlockSpec((1, D), lambda i: (0, 0))],
        out_specs=pl.BlockSpec((tb, D), lambda i: (i, 0)),
        compiler_params=_cparams(("parallel",)),
    )(x, g.reshape(1, D))


def _rms_bwd(x, g, dh, dres, name):
    T, D = x.shape
    tb = _tile(T, 256, SUB)

    def body(x_ref, g_ref, dh_ref, dr_ref, dx_ref, dxb_ref, dg_ref):
        i = pl.program_id(0)
        xv = x_ref[...]
        rstd = lax.rsqrt(jnp.mean(xv * xv, axis=-1, keepdims=True) + EPS)
        xn = xv * rstd
        dhv = dh_ref[...].astype(F32)
        _acc_row(dg_ref, dhv * xn, i == 0)
        dxn = dhv * g_ref[...]
        dx = dr_ref[...] + rstd * (dxn - xn * jnp.mean(dxn * xn, axis=-1, keepdims=True))
        dx_ref[...] = dx
        dxb_ref[...] = dx.astype(BF16)

    blk = pl.BlockSpec((tb, D), lambda i: (i, 0))
    vec = pl.BlockSpec((1, D), lambda i: (0, 0))
    return pl.pallas_call(
        body, name=name, out_shape=(_S((T, D), F32), _S((T, D), BF16), _S((1, D), F32)),
        grid=(T // tb,), in_specs=[blk, vec, blk, blk], out_specs=(blk, blk, vec),
        compiler_params=_cparams(("arbitrary",)),
    )(x, g.reshape(1, D), dh, dres)


def _loss_head(x2, g, target, name):
    T, D = x2.shape
    tb = _tile(T, 256, SUB)

    def body(x_ref, g_ref, t_ref, dx_ref, dxb_ref, loss_ref, dg_ref):
        i = pl.program_id(0)
        xv = x_ref[...]
        rstd = lax.rsqrt(jnp.mean(xv * xv, axis=-1, keepdims=True) + EPS)
        xn = xv * rstd
        err = xn * g_ref[...] - t_ref[...]
        part = 0.5 * jnp.sum(jnp.mean(err * err, axis=-1, keepdims=True), axis=0, keepdims=True)
        part = jnp.broadcast_to(part, (1, LANES))
        _acc(loss_ref, part, i == 0)
        dy = err * (1.0 / D)
        _acc_row(dg_ref, dy * xn, i == 0)
        dxn = dy * g_ref[...]
        dx = rstd * (dxn - xn * jnp.mean(dxn * xn, axis=-1, keepdims=True))
        dx_ref[...] = dx
        dxb_ref[...] = dx.astype(BF16)

    blk = pl.BlockSpec((tb, D), lambda i: (i, 0))
    vec = pl.BlockSpec((1, D), lambda i: (0, 0))
    return pl.pallas_call(
        body, name=name,
        out_shape=(_S((T, D), F32), _S((T, D), BF16), _S((1, LANES), F32), _S((1, D), F32)),
        grid=(T // tb,), in_specs=[blk, vec, blk],
        out_specs=(blk, blk, pl.BlockSpec((1, LANES), lambda i: (0, 0)), vec),
        compiler_params=_cparams(("arbitrary",)),
    )(x2, g.reshape(1, D), target)


def _lru_gates(xc, wa_ref, ba_ref, wx_ref, bx_ref, lam_ref):
    xcb = xc.astype(BF16)
    r = jax.nn.sigmoid(jnp.dot(xcb, wa_ref[...], preferred_element_type=F32) + ba_ref[...])
    i = jax.nn.sigmoid(jnp.dot(xcb, wx_ref[...], preferred_element_type=F32) + bx_ref[...])
    sp = jax.nn.softplus(-lam_ref[...])
    log_a = (-LRU_C * sp) * r
    a = jnp.exp(log_a)
    s = jnp.sqrt(_neg_expm1(2.0 * log_a))
    return xcb, r, i, a, s


def _lru_fwd(p, conv_w, conv_b, wa_bd, ba, wx_bd, bx, lam, *, name, tasks=()):
    T = p.shape[0]
    d = lam.shape[-1]
    C = _tile(d, C_LRU)
    nC = d // C
    tb = _tile(T, TB, HALO)
    nT, hb, nt = T // tb, tb // HALO, tb // SUB

    def body(x_ref, xh_ref, g_ref, cw_ref, cb_ref, wa_ref, ba_ref, wx_ref, bx_ref, lam_ref,
             hs_ref, y_ref, a_s, u_s, h_s):
        t = pl.program_id(1)

        @pl.when(t == 0)
        def _():
            h_s[...] = jnp.zeros_like(h_s)

        x = x_ref[...].astype(F32)
        xc = _conv(x, _prev8(xh_ref, t), cw_ref) + cb_ref[...]
        _, r, i, a, s = _lru_gates(xc, wa_ref, ba_ref, wx_ref, bx_ref, lam_ref)
        a_s[...] = a
        u_s[...] = s * (i * xc)
        row = lax.broadcasted_iota(jnp.int32, (SUB, C), 0)

        def step(k, h):
            o = pl.multiple_of(k * SUB, SUB)
            A = a_s[pl.ds(o, SUB), :]
            B = u_s[pl.ds(o, SUB), :]
            for sh in (1, 2, 4):
                m = row >= sh
                Ap = pltpu.roll(A, sh, 0)
                Bp = pltpu.roll(B, sh, 0)
                B = jnp.where(m, A * Bp + B, B)
                A = jnp.where(m, A * Ap, A)
            hs = A * h + B
            hs_ref[pl.ds(o, SUB), :] = hs
            return jnp.broadcast_to(hs[SUB - 1:SUB, :], (SUB, C))

        h_s[...] = lax.fori_loop(0, nt, step, h_s[...])
        gel, _ = _gelu_and_grad(g_ref[...].astype(F32))
        y_ref[...] = (gel * hs_ref[...]).astype(BF16)

    vec = pl.BlockSpec((1, C), lambda c, t: (0, c))
    sq = pl.BlockSpec((None, C, C), lambda c, t: (c, 0, 0))
    outs, passed = _call(
        name, (nC, nT), body,
        [pl.BlockSpec((tb, C), lambda c, t: (t, c)),
         pl.BlockSpec((HALO, C), _halo_prev_map(hb, lambda c: c)),
         pl.BlockSpec((tb, C), lambda c, t: (t, nC + c)),
         pl.BlockSpec((conv_w.shape[0], C), lambda c, t: (0, c)),
         vec, sq, vec, sq, vec, vec],
        [p, p, p, conv_w, conv_b, wa_bd, ba, wx_bd, bx, lam],
        [_S((T, d), F32), _S((T, d), BF16)],
        [pl.BlockSpec((tb, C), lambda c, t: (t, c)), pl.BlockSpec((tb, C), lambda c, t: (t, c))],
        [pltpu.VMEM((tb, C), F32), pltpu.VMEM((tb, C), F32), pltpu.VMEM((SUB, C), F32)], tasks)
    return (*outs, passed) if tasks else outs


def _lru_bwd(p, hs, dyl, dp, conv_w, conv_b, wa_bd, ba, wx_bd, bx, lam, *, name, tasks=()):
    T = p.shape[0]
    d = lam.shape[-1]
    C = _tile(d, C_LRU)
    nC = d // C
    tb = _tile(T, TB, HALO)
    nT, hb, nt = T // tb, tb // HALO, tb // SUB
    kw = conv_w.shape[0]

    def body(x_ref, xh_ref, g_ref, hs_ref, hh_ref, dy_ref, cw_ref, cb_ref, wa_ref, ba_ref, wx_ref, bx_ref,
             lam_ref, dp_in, dp_ref, dcw_ref, dcb_ref, dwa_ref, dba_ref, dwx_ref, dbx_ref, dlam_ref,
             b_s, g_s, dh_s, an_s, dhn_s, dxn_s, st_x, st_g, sems):
        del dp_in
        c = pl.program_id(0)
        tr = pl.program_id(1)
        t = nT - 1 - tr
        first = tr == 0

        @pl.when(first)
        def _():
            an_s[...] = jnp.zeros_like(an_s)
            dhn_s[...] = jnp.zeros_like(dhn_s)
            dxn_s[...] = jnp.zeros_like(dxn_s)

        x = x_ref[...].astype(F32)
        xprev = _prev8(xh_ref, t)
        xc = _conv(x, xprev, cw_ref) + cb_ref[...]
        xcb, r, i, a, s = _lru_gates(xc, wa_ref, ba_ref, wx_ref, bx_ref, lam_ref)
        hsv = hs_ref[...]
        dy = dy_ref[...].astype(F32)
        gel, dgel = _gelu_and_grad(g_ref[...].astype(F32))
        st_g[...] = (dy * hsv * dgel).astype(BF16)

        b_s[...] = _up(a, an_s[...], 1)
        g_s[...] = dy * gel
        row = lax.broadcasted_iota(jnp.int32, (SUB, C), 0)

        def step(k, carry):
            o = pl.multiple_of((nt - 1 - k) * SUB, SUB)
            B = b_s[pl.ds(o, SUB), :]
            G = g_s[pl.ds(o, SUB), :]
            for sh in (1, 2, 4):
                m = row < SUB - sh
                Bn = pltpu.roll(B, SUB - sh, 0)
                Gn = pltpu.roll(G, SUB - sh, 0)
                G = jnp.where(m, B * Gn + G, G)
                B = jnp.where(m, B * Bn, B)
            dh = B * carry + G
            dh_s[pl.ds(o, SUB), :] = dh
            return jnp.broadcast_to(dh[0:1, :], (SUB, C))

        dhn_s[...] = lax.fori_loop(0, nt, step, dhn_s[...])
        an_s[...] = a[:SUB, :]
        dh = dh_s[...]

        hprev = _down(hsv, jnp.where(t > 0, hh_ref[...][HALO - SUB:, :], 0.0), 1)
        d_a = dh * hprev
        ixc = i * xc
        d_s = dh * ixc
        d_i = dh * s * xc
        d_xc = dh * s * i
        d_l = d_a * a - d_s * (a * a) / s
        sp = jax.nn.softplus(-lam_ref[...])
        _acc_row(dlam_ref, d_l * r * (LRU_C * jax.nn.sigmoid(-lam_ref[...])), first)
        d_zr = (d_l * (-LRU_C * sp)) * r * (1.0 - r)
        d_zi = d_i * i * (1.0 - i)
        _acc_row(dba_ref, d_zr, first)
        _acc_row(dbx_ref, d_zi, first)
        d_zrb = d_zr.astype(BF16)
        d_zib = d_zi.astype(BF16)
        tn_dims = (((0,), (0,)), ((), ()))
        nt_dims = (((1,), (1,)), ((), ()))
        gwa = lax.dot_general(xcb, d_zrb, tn_dims, preferred_element_type=F32)
        gwx = lax.dot_general(xcb, d_zib, tn_dims, preferred_element_type=F32)
        _acc(dwa_ref, gwa, first)
        _acc(dwx_ref, gwx, first)
        d_xc = (d_xc + lax.dot_general(d_zrb, wa_ref[...], nt_dims, preferred_element_type=F32)
                + lax.dot_general(d_zib, wx_ref[...], nt_dims, preferred_element_type=F32))
        _acc_row(dcb_ref, d_xc, first)
        _conv_dw(dcw_ref, d_xc, x, xprev, first)
        st_x[...] = _conv_t(d_xc, dxn_s[...], cw_ref).astype(BF16)
        dxn_s[...] = d_xc[:SUB, :]

        cx = _store_block(st_x, dp_ref, sems.at[0], t * tb, c * C)
        cg = _store_block(st_g, dp_ref, sems.at[1], t * tb, d + c * C)
        cx.start()
        cg.start()
        cx.wait()
        cg.wait()

    rev = lambda c, tr: (nT - 1 - tr, c)
    vec = pl.BlockSpec((1, C), lambda c, tr: (0, c))
    sq = pl.BlockSpec((None, C, C), lambda c, tr: (c, 0, 0))
    cwb = pl.BlockSpec((kw, C), lambda c, tr: (0, c))
    halo_prev = lambda c, tr: (jnp.maximum((nT - 1 - tr) * hb - 1, 0), c)
    outs, passed = _call(
        name, (nC, nT), body,
        [pl.BlockSpec((tb, C), rev),
         pl.BlockSpec((HALO, C), halo_prev),
         pl.BlockSpec((tb, C), lambda c, tr: (nT - 1 - tr, nC + c)),
         pl.BlockSpec((tb, C), rev),
         pl.BlockSpec((HALO, C), halo_prev),
         pl.BlockSpec((tb, C), rev),
         cwb, vec, sq, vec, sq, vec, vec, ANY],
        [p, p, p, hs, hs, dyl, conv_w, conv_b, wa_bd, ba, wx_bd, bx, lam, dp],
        [_S(dp.shape, dp.dtype), _S((kw, d), F32), _S((1, d), F32), _S((nC, C, C), F32), _S((1, d), F32),
         _S((nC, C, C), F32), _S((1, d), F32), _S((1, d), F32)],
        [ANY, cwb, vec, sq, vec, sq, vec, vec],
        [pltpu.VMEM((tb, C), F32), pltpu.VMEM((tb, C), F32), pltpu.VMEM((tb, C), F32),
         pltpu.VMEM((SUB, C), F32), pltpu.VMEM((SUB, C), F32), pltpu.VMEM((SUB, C), F32),
         pltpu.VMEM((tb, C), BF16), pltpu.VMEM((tb, C), BF16), pltpu.SemaphoreType.DMA((2,))],
        tasks, own_aliases={13: 0})
    return (*outs, passed) if tasks else outs


def _sc_fwd(p, conv_w, *, d, name):
    T = p.shape[0]
    C = _tile(d, C_EW)
    nC = d // C
    tb = _tile(T, TB, HALO)
    nT, hb = T // tb, tb // HALO

    def body(b_ref, c_ref, ch_ref, v_ref, vh_ref, w_ref, y_ref):
        t = pl.program_id(1)
        cv = c_ref[...].astype(F32) * v_ref[...].astype(F32)
        cvp = _prev8(ch_ref, t) * _prev8(vh_ref, t)
        y_ref[...] = (b_ref[...].astype(F32) * _conv(cv, cvp, w_ref)).astype(BF16)

    seg = lambda k: pl.BlockSpec((tb, C), lambda c, t: (t, k * nC + c))
    hseg = lambda k: pl.BlockSpec((HALO, C), _halo_prev_map(hb, lambda c: k * nC + c))
    return pl.pallas_call(
        body, name=name, out_shape=_S((T, d), BF16), grid=(nC, nT),
        in_specs=[seg(2), seg(3), hseg(3), seg(4), hseg(4), pl.BlockSpec((conv_w.shape[0], C), lambda c, t: (0, c))],
        out_specs=pl.BlockSpec((tb, C), lambda c, t: (t, c)),
        compiler_params=_cparams(("parallel", "parallel")),
    )(p, p, p, p, p, conv_w)


def _sc_bwd(p, dys, dp, conv_w, *, d, name):
    T = p.shape[0]
    C = _tile(d, C_EW)
    nC = d // C
    tb = _tile(T, TB, HALO)
    nT, hb = T // tb, tb // HALO
    kw = conv_w.shape[0]

    def body(b_ref, bn_ref, c_ref, ch_ref, v_ref, vh_ref, dy_ref, dyn_ref, w_ref, dp_in, dp_ref, dw_ref,
             st_b, st_c, st_v, sems):
        del dp_in
        c = pl.program_id(0)
        t = pl.program_id(1)
        last = t == nT - 1
        bv = b_ref[...].astype(F32)
        cvv = c_ref[...].astype(F32)
        vv = v_ref[...].astype(F32)
        dy = dy_ref[...].astype(F32)
        cv = cvv * vv
        cvp = _prev8(ch_ref, t) * _prev8(vh_ref, t)
        st_b[...] = (dy * _conv(cv, cvp, w_ref)).astype(BF16)
        dz = dy * bv
        dzn = _next8(dyn_ref, last) * _next8(bn_ref, last)
        _conv_dw(dw_ref, dz, cv, cvp, t == 0)
        dcv = _conv_t(dz, dzn, w_ref)
        st_c[...] = (dcv * vv).astype(BF16)
        st_v[...] = (dcv * cvv).astype(BF16)
        cps = [_store_block(st, dp_ref, sems.at[k], t * tb, (2 + k) * d + c * C)
               for k, st in enumerate((st_b, st_c, st_v))]
        for cp in cps:
            cp.start()
        for cp in cps:
            cp.wait()

    seg = lambda k: pl.BlockSpec((tb, C), lambda c, t: (t, k * nC + c))
    hseg = lambda k: pl.BlockSpec((HALO, C), _halo_prev_map(hb, lambda c: k * nC + c))
    last_h = T // HALO - 1
    nseg = lambda k: pl.BlockSpec((HALO, C), lambda c, t: (jnp.minimum((t + 1) * hb, last_h), k * nC + c))
    return pl.pallas_call(
        body, name=name, out_shape=(_S(dp.shape, dp.dtype), _S((kw, d), F32)), grid=(nC, nT),
        in_specs=[seg(2), nseg(2), seg(3), hseg(3), seg(4), hseg(4),
                  pl.BlockSpec((tb, C), lambda c, t: (t, c)), nseg(0),
                  pl.BlockSpec((kw, C), lambda c, t: (0, c)), ANY],
        out_specs=(ANY, pl.BlockSpec((kw, C), lambda c, t: (0, c))),
        scratch_shapes=[pltpu.VMEM((tb, C), BF16)] * 3 + [pltpu.SemaphoreType.DMA((3,))],
        input_output_aliases={9: 0},
        compiler_params=_cparams(("arbitrary", "arbitrary")),
    )(p, p, p, p, p, p, dys, dys, conv_w, dp)


def _merge_fwd(p, y_lru, y_sc, *, col0, name):
    T, D = y_lru.shape
    C = _tile(math.gcd(D, col0), 1024)
    nC = D // C
    k0 = col0 // C
    tb = _tile(T, 256, HALO)

    def body(gl_ref, gs_ref, yl_ref, ys_ref, o_ref):
        @pl.loop(0, tb // HALO)
        def _(k):
            rows = pl.ds(pl.multiple_of(k * HALO, HALO), HALO)
            for l0 in range(0, C, min(C, C_EW)):
                at = (rows, pl.ds(l0, min(C, C_EW)))
                o_ref[at] = (jax.nn.sigmoid(gl_ref[at].astype(F32)) * yl_ref[at].astype(F32)
                             + jax.nn.sigmoid(gs_ref[at].astype(F32)) * ys_ref[at].astype(F32)).astype(BF16)

    blk = pl.BlockSpec((tb, C), lambda c, t: (t, c))
    return pl.pallas_call(
        body, name=name, out_shape=_S((T, D), BF16), grid=(nC, T // tb),
        in_specs=[pl.BlockSpec((tb, C), lambda c, t: (t, k0 + c)),
                  pl.BlockSpec((tb, C), lambda c, t: (t, k0 + nC + c)), blk, blk],
        out_specs=blk, compiler_params=_cparams(("parallel", "parallel")),
    )(p, p, y_lru, y_sc)


def _merge_bwd(p, y_lru, y_sc, dm, *, col0, name):
    T, D = y_lru.shape
    C = _tile(math.gcd(D, col0), 1024)
    nC = D // C
    k0 = col0 // C
    tb = _tile(T, 256, HALO)

    def body(gl_ref, gs_ref, yl_ref, ys_ref, dm_ref, dp_ref, dyl_ref, dys_ref, st_l, st_s, sems):
        c = pl.program_id(0)
        t = pl.program_id(1)
        @pl.loop(0, tb // HALO)
        def _(k):
            rows = pl.ds(pl.multiple_of(k * HALO, HALO), HALO)
            for l0 in range(0, C, min(C, C_EW)):
                at = (rows, pl.ds(l0, min(C, C_EW)))
                dmv = dm_ref[at].astype(F32)
                sl = jax.nn.sigmoid(gl_ref[at].astype(F32))
                ss = jax.nn.sigmoid(gs_ref[at].astype(F32))
                dyl_ref[at] = (dmv * sl).astype(BF16)
                dys_ref[at] = (dmv * ss).astype(BF16)
                st_l[at] = (dmv * yl_ref[at].astype(F32) * sl * (1.0 - sl)).astype(BF16)
                st_s[at] = (dmv * ys_ref[at].astype(F32) * ss * (1.0 - ss)).astype(BF16)

        cl = _store_block(st_l, dp_ref, sems.at[0], t * tb, col0 + c * C)
        cs = _store_block(st_s, dp_ref, sems.at[1], t * tb, col0 + D + c * C)
        cl.start()
        cs.start()
        cl.wait()
        cs.wait()

    blk = pl.BlockSpec((tb, C), lambda c, t: (t, c))
    return pl.pallas_call(
        body, name=name, out_shape=(_S(p.shape, BF16), _S((T, D), BF16), _S((T, D), BF16)),
        grid=(nC, T // tb),
        in_specs=[pl.BlockSpec((tb, C), lambda c, t: (t, k0 + c)),
                  pl.BlockSpec((tb, C), lambda c, t: (t, k0 + nC + c)), blk, blk, blk],
        out_specs=(ANY, blk, blk),
        scratch_shapes=[pltpu.VMEM((tb, C), BF16), pltpu.VMEM((tb, C), BF16), pltpu.SemaphoreType.DMA((2,))],
        compiler_params=_cparams(("arbitrary", "arbitrary")),
    )(p, p, y_lru, y_sc, dm)


def _ffn_act_fwd(uu, conv_w, *, name, tasks=()):
    T = uu.shape[0]
    F = uu.shape[1] // 2
    C = _tile(F, C_EW)
    nC = F // C
    tb = _tile(T, TB, HALO)
    nT, hb = T // tb, tb // HALO
    kw = conv_w.shape[0]
    R = HALO

    def body(g_ref, gh_ref, v_ref, vh_ref, wg_ref, wv_ref, o_ref):
        t = pl.program_id(1)

        def chunk(k, carry):
            gp, vp = carry
            r0 = pl.multiple_of(k * R, R)
            ug = g_ref[pl.ds(r0, R), :].astype(F32)
            uv = v_ref[pl.ds(r0, R), :].astype(F32)
            cg = _conv(ug, gp, wg_ref)
            cv = _conv(uv, vp, wv_ref)
            o_ref[pl.ds(r0, R), :] = (cg * jax.nn.sigmoid(cg) * cv).astype(BF16)
            return ug[R - SUB:, :], uv[R - SUB:, :]

        lax.fori_loop(0, tb // R, chunk, (_prev8(gh_ref, t), _prev8(vh_ref, t)))

    seg = lambda k: pl.BlockSpec((tb, C), lambda c, t: (t, k * nC + c))
    hseg = lambda k: pl.BlockSpec((HALO, C), _halo_prev_map(hb, lambda c: k * nC + c))
    wseg = lambda k: pl.BlockSpec((kw, C), lambda c, t: (0, k * nC + c))
    outs, passed = _call(
        name, (nC, nT), body, [seg(0), hseg(0), seg(1), hseg(1), wseg(0), wseg(1)],
        [uu, uu, uu, uu, conv_w, conv_w], [_S((T, F), BF16)], [pl.BlockSpec((tb, C), lambda c, t: (t, c))], [], tasks)
    return (outs[0], passed) if tasks else outs[0]


def _ffn_act_bwd(uu, dact, conv_w, *, name):
    T = uu.shape[0]
    F = uu.shape[1] // 2
    C = _tile(F, C_EW)
    nC = F // C
    tb = _tile(T, TB, HALO)
    nT, hb = T // tb, tb // HALO
    kw = conv_w.shape[0]
    R = HALO
    nk = tb // R

    def body(g_ref, gh_ref, v_ref, vh_ref, da_ref, wg_ref, wv_ref, du_ref, dwg_ref, dwv_ref,
             gn_s, vn_s, accg_s, accv_s, st_g, st_v, sems):
        c = pl.program_id(0)
        tr = pl.program_id(1)
        t = nT - 1 - tr
        first = tr == 0

        @pl.when(first)
        def _():
            gn_s[...] = jnp.zeros_like(gn_s)
            vn_s[...] = jnp.zeros_like(vn_s)
            dwg_ref[...] = jnp.zeros_like(dwg_ref)
            dwv_ref[...] = jnp.zeros_like(dwv_ref)

        accg_s[...] = jnp.zeros_like(accg_s)
        accv_s[...] = jnp.zeros_like(accv_s)

        def chunk(i, carry):
            gn, vn = carry
            k = nk - 1 - i
            r0 = pl.multiple_of(k * R, R)
            rp = pl.multiple_of(jnp.maximum(r0 - R, 0), R)
            ug = g_ref[pl.ds(r0, R), :].astype(F32)
            uv = v_ref[pl.ds(r0, R), :].astype(F32)
            gp = jnp.where(k > 0, g_ref[pl.ds(rp, R), :].astype(F32)[R - SUB:, :], _prev8(gh_ref, t))
            vp = jnp.where(k > 0, v_ref[pl.ds(rp, R), :].astype(F32)[R - SUB:, :], _prev8(vh_ref, t))
            sh_g = _shifted_down(ug, gp, kw)
            sh_v = _shifted_down(uv, vp, kw)
            cg = _taps(sh_g, wg_ref)
            cv = _taps(sh_v, wv_ref)
            da = da_ref[pl.ds(r0, R), :].astype(F32)
            sg = jax.nn.sigmoid(cg)
            d_cg = da * cv * (sg * (1.0 + cg * (1.0 - sg)))
            d_cv = da * (cg * sg)
            for j in range(kw):
                accg_s[j] += d_cg * sh_g[kw - 1 - j]
                accv_s[j] += d_cv * sh_v[kw - 1 - j]
            st_g[pl.ds(r0, R), :] = _conv_t(d_cg, gn, wg_ref).astype(BF16)
            st_v[pl.ds(r0, R), :] = _conv_t(d_cv, vn, wv_ref).astype(BF16)
            return d_cg[:SUB, :], d_cv[:SUB, :]

        gn, vn = lax.fori_loop(0, nk, chunk, (gn_s[...], vn_s[...]))
        gn_s[...] = gn
        vn_s[...] = vn
        for j in range(kw):
            dwg_ref[pl.ds(j, 1), :] += jnp.sum(accg_s[j], axis=0, keepdims=True)
            dwv_ref[pl.ds(j, 1), :] += jnp.sum(accv_s[j], axis=0, keepdims=True)
        cpg = _store_block(st_g, du_ref, sems.at[0], t * tb, c * C)
        cpv = _store_block(st_v, du_ref, sems.at[1], t * tb, F + c * C)
        cpg.start()
        cpv.start()
        cpg.wait()
        cpv.wait()

    seg = lambda k: pl.BlockSpec((tb, C), lambda c, tr: (nT - 1 - tr, k * nC + c))
    hseg = lambda k: pl.BlockSpec((HALO, C), lambda c, tr: (jnp.maximum((nT - 1 - tr) * hb - 1, 0), k * nC + c))
    wseg = lambda k: pl.BlockSpec((kw, C), lambda c, tr: (0, k * nC + c))
    dwb = pl.BlockSpec((kw, C), lambda c, tr: (0, c))
    return pl.pallas_call(
        body, name=name, out_shape=(_S(uu.shape, BF16), _S((kw, F), F32), _S((kw, F), F32)), grid=(nC, nT),
        in_specs=[seg(0), hseg(0), seg(1), hseg(1), pl.BlockSpec((tb, C), lambda c, tr: (nT - 1 - tr, c)),
                  wseg(0), wseg(1)],
        out_specs=(ANY, dwb, dwb),
        scratch_shapes=[pltpu.VMEM((SUB, C), F32), pltpu.VMEM((SUB, C), F32),
                        pltpu.VMEM((kw, R, C), F32), pltpu.VMEM((kw, R, C), F32),
                        pltpu.VMEM((tb, C), BF16), pltpu.VMEM((tb, C), BF16), pltpu.SemaphoreType.DMA((2,))],
        compiler_params=_cparams(("arbitrary", "arbitrary")),
    )(uu, uu, uu, uu, dact, conv_w, conv_w)


def _place():
    x, y, c = lax.axis_index("x"), lax.axis_index("y"), lax.axis_index("c")
    return x, y, c


def _chips(x, y):
    return [(1 - x, y), (x, 1 - y), (1 - x, 1 - y)]


def _all_gather(arrays, placed, over_ici, name):
    n = len(arrays)

    def body(*refs):
        ins, outs = refs[:n], refs[n:2 * n]
        send_sems, recv_sems, local_sems = refs[2 * n:]
        x, y, c = _place()
        me, sibling = (x, y, c), (x, y, 1 - c)
        chips = _chips(x, y)
        full = [a for a in range(n) if over_ici[a]]

        def idx(px, py, pc):
            return 4 * px + 2 * py + pc

        def copy(a, k, block, to):
            dst = outs[a].at[idx(*block)]
            src = ins[a] if (block is me and not placed[a]) else dst
            return pltpu.make_async_remote_copy(
                src_ref=src, dst_ref=dst, send_sem=send_sems.at[a, k], recv_sem=recv_sems.at[a, k],
                device_id=to, device_id_type=MESH)

        mine = [pltpu.make_async_copy(ins[a], outs[a].at[idx(*me)], local_sems.at[a])
                for a in range(n) if not placed[a]]
        for cp in mine:
            cp.start()
        first = []
        for a in full:
            first += [copy(a, 1 + j, me, (*chip, c)) for j, chip in enumerate(chips)]
        for a in range(n):
            first.append(copy(a, 0, me, sibling))
        for cp in first:
            cp.start()
        passed = []
        for a in full:
            for j, chip in enumerate(chips):
                copy(a, 1 + j, (*chip, c), me).wait_recv()
                cp = copy(a, 4 + j, (*chip, c), sibling)
                cp.start()
                passed.append(cp)
        for a in range(n):
            copy(a, 0, sibling, me).wait_recv()
        for a in full:
            for j, chip in enumerate(chips):
                copy(a, 4 + j, (*chip, 1 - c), me).wait_recv()
        for cp in first + passed:
            cp.wait_send()
        for cp in mine:
            cp.wait()

    return pl.pallas_call(
        body, name=name,
        out_shape=tuple(_S(s.shape if placed[a] else (N_DEV,) + s.shape, s.dtype) for a, s in enumerate(arrays)),
        in_specs=[ANY] * n, out_specs=tuple([ANY] * n),
        scratch_shapes=[pltpu.SemaphoreType.DMA((n, 7)), pltpu.SemaphoreType.DMA((n, 7)),
                        pltpu.SemaphoreType.DMA((n,))],
        input_output_aliases={a: a for a in range(n) if placed[a]},
    )(*arrays)


def _rows_of(ref, blk, rows):
    v = ref.at[blk]
    return v if rows is None else v.at[pl.ds(rows[0], rows[1])]


ALL_ROWS = "all"


def _gather_task(buf, ici=None, fwd=None):
    rows = lambda r: None if r == ALL_ROWS else r

    def copies(refs, ss, rs):
        x, y, c = _place()
        me = 4 * x + 2 * y + c
        cps = []
        for j, (px, py) in enumerate(_chips(x, y)):
            if ici is not None:
                blk = _rows_of(refs[0], me, rows(ici))
                cps.append(pltpu.make_async_remote_copy(
                    src_ref=blk, dst_ref=blk, send_sem=ss.at[j], recv_sem=rs.at[j],
                    device_id=(px, py, c), device_id_type=MESH))
            if fwd is not None:
                blk = _rows_of(refs[0], 4 * px + 2 * py + c, rows(fwd))
                cps.append(pltpu.make_async_remote_copy(
                    src_ref=blk, dst_ref=blk, send_sem=ss.at[3 + j], recv_sem=rs.at[3 + j],
                    device_id=(x, y, 1 - c), device_id_type=MESH))
        return cps

    def start(refs, ss, rs, ls):
        for cp in copies(refs, ss, rs):
            cp.start()

    def wait(refs, ss, rs, ls):
        x, y, c = _place()
        for j, (px, py) in enumerate(_chips(x, y)):
            if ici is not None:
                blk = _rows_of(refs[0], 4 * px + 2 * py + c, rows(ici))
                pltpu.make_async_remote_copy(
                    src_ref=blk, dst_ref=blk, send_sem=ss.at[j], recv_sem=rs.at[j],
                    device_id=(px, py, c), device_id_type=MESH).wait_recv()
            if fwd is not None:
                blk = _rows_of(refs[0], 4 * px + 2 * py + 1 - c, rows(fwd))
                pltpu.make_async_remote_copy(
                    src_ref=blk, dst_ref=blk, send_sem=ss.at[3 + j], recv_sem=rs.at[3 + j],
                    device_id=(x, y, 1 - c), device_id_type=MESH).wait_recv()
        for cp in copies(refs, ss, rs):
            cp.wait_send()

    return _Task([buf], [0], start, wait, nsem=6)


def _exchange_task(parts, landing, rows=None):
    def copies(refs, ss, rs):
        x, y, c = _place()
        myq = 2 * x + y
        return [pltpu.make_async_remote_copy(
            src_ref=_rows_of(refs[0], 2 * px + py, rows), dst_ref=_rows_of(refs[1], myq, rows),
            send_sem=ss.at[k], recv_sem=rs.at[k], device_id=(px, py, c), device_id_type=MESH)
            for k, (px, py) in enumerate(_chips(x, y))]

    def start(refs, ss, rs, ls):
        for cp in copies(refs, ss, rs):
            cp.start()

    def wait(refs, ss, rs, ls):
        x, y, c = _place()
        for k, (px, py) in enumerate(_chips(x, y)):
            pltpu.make_async_remote_copy(
                src_ref=_rows_of(refs[0], 2 * x + y, rows), dst_ref=_rows_of(refs[1], 2 * px + py, rows),
                send_sem=ss.at[k], recv_sem=rs.at[k], device_id=(px, py, c), device_id_type=MESH).wait_recv()
        for cp in copies(refs, ss, rs):
            cp.wait_send()

    return _Task([parts, landing], [1], start, wait)


def _swap_task(g):
    g4 = g.reshape((N_CHIP, 2) + g.shape[1:])

    def copy(refs, ss, rs):
        x, y, c = _place()
        return pltpu.make_async_remote_copy(
            src_ref=refs[0].at[:, 1 - c], dst_ref=refs[1], send_sem=ss.at[0], recv_sem=rs.at[0],
            device_id=(x, y, 1 - c), device_id_type=MESH)

    def start(refs, ss, rs, ls):
        copy(refs, ss, rs).start()

    def wait(refs, ss, rs, ls):
        copy(refs, ss, rs).wait()

    return _Task([g4], [], start, wait, fresh=[_S((N_CHIP,) + g.shape[1:], g.dtype)], nsem=1)


def _peer(x, y, c, m):
    return x ^ (m >> 2), y ^ ((m >> 1) & 1), c ^ (m & 1)


def _bcast_task(pack):
    def copies(refs, ss, rs):
        x, y, c = _place()
        me = 4 * x + 2 * y + c
        return [pltpu.make_async_remote_copy(
            src_ref=refs[0], dst_ref=refs[1].at[me], send_sem=ss.at[m - 1], recv_sem=rs.at[m - 1],
            device_id=_peer(x, y, c, m), device_id_type=MESH) for m in range(1, N_DEV)]

    def local(refs, ls):
        x, y, c = _place()
        return pltpu.make_async_copy(refs[0], refs[1].at[4 * x + 2 * y + c], ls.at[0])

    def start(refs, ss, rs, ls):
        local(refs, ls).start()
        for cp in copies(refs, ss, rs):
            cp.start()

    def wait(refs, ss, rs, ls):
        x, y, c = _place()
        for m in range(1, N_DEV):
            px, py, pc = _peer(x, y, c, m)
            pltpu.make_async_remote_copy(
                src_ref=refs[0], dst_ref=refs[1].at[4 * px + 2 * py + pc], send_sem=ss.at[m - 1],
                recv_sem=rs.at[m - 1], device_id=(px, py, pc), device_id_type=MESH).wait_recv()
        for cp in copies(refs, ss, rs):
            cp.wait_send()
        local(refs, ls).wait()

    return _Task([pack], [], start, wait, fresh=[_S((N_DEV,) + pack.shape, pack.dtype)], nsem=N_DEV - 1)


def _sum_packs(packs, name):
    _, R, L = packs.shape

    def body(p_ref, o_ref):
        acc = p_ref[0]
        for k in range(1, N_DEV):
            acc = acc + p_ref[k]
        o_ref[...] = acc

    return pl.pallas_call(body, name=name, out_shape=_S((R, L), packs.dtype), in_specs=[VMEM_SPEC],
                          out_specs=VMEM_SPEC, compiler_params=_cparams())(packs)


def _swap_halves(grads, name):
    n = len(grads)
    g4 = [g.reshape((N_CHIP, 2) + g.shape[1:]) for g in grads]

    def body(*refs):
        ins, outs = refs[:n], refs[n:2 * n]
        send_sems, recv_sems = refs[2 * n:]
        x, y, c = _place()
        cps = [pltpu.make_async_remote_copy(
            src_ref=ins[a].at[:, 1 - c], dst_ref=outs[a],
            send_sem=send_sems.at[a], recv_sem=recv_sems.at[a],
            device_id=(x, y, 1 - c), device_id_type=MESH) for a in range(n)]
        for cp in cps:
            cp.start()
        for cp in cps:
            cp.wait()

    return pl.pallas_call(
        body, name=name,
        out_shape=tuple(_S((N_CHIP,) + g.shape[1:], g.dtype) for g in grads),
        in_specs=[ANY] * n, out_specs=tuple([ANY] * n),
        scratch_shapes=[pltpu.SemaphoreType.DMA((n,)), pltpu.SemaphoreType.DMA((n,))],
    )(*g4)


def _add_halves(g, landed, place, name):
    _, r, cc = g.shape
    g4 = g.reshape(N_CHIP, 2, r, cc)
    tr = _tile(r, 512, HALO)

    def body(s_ref, g_ref, l_ref, o_ref, land_ref):
        q = pl.program_id(1)
        v = (g_ref[...].astype(F32) + l_ref[...].astype(F32)).astype(BF16)
        o_ref[...] = v

        @pl.when(q == s_ref[1])
        def _():
            land_ref[...] = v

    return pl.pallas_call(
        body, name=name, out_shape=(_S((N_CHIP, r, cc), BF16), _S((N_CHIP, r, cc), BF16)),
        grid_spec=pltpu.PrefetchScalarGridSpec(
            num_scalar_prefetch=1, grid=(r // tr, N_CHIP),
            in_specs=[pl.BlockSpec((None, None, tr, cc), lambda i, q, s: (q, s[0], i, 0)),
                      pl.BlockSpec((None, tr, cc), lambda i, q, s: (q, i, 0))],
            out_specs=(pl.BlockSpec((None, tr, cc), lambda i, q, s: (q, i, 0)),
                       pl.BlockSpec((None, tr, cc), lambda i, q, s: (s[1], i, 0)))),
        compiler_params=_cparams(("arbitrary", "arbitrary")),
    )(place, g4, landed)


def _all_reduce_small(pack, name):
    R = pack.shape[0]

    def body(p_ref, o_ref, buf, send_sems, recv_sems):
        x, y, c = _place()
        me = 4 * x + 2 * y + c
        buf[me] = p_ref[...]
        cps = []
        for k in range(N_DEV - 1):
            m = k + 1
            peer = (x ^ (m >> 2), y ^ ((m >> 1) & 1), c ^ (m & 1))
            cps.append(pltpu.make_async_remote_copy(
                src_ref=p_ref, dst_ref=buf.at[me], send_sem=send_sems.at[k], recv_sem=recv_sems.at[k],
                device_id=peer, device_id_type=MESH))
        for cp in cps:
            cp.start()
        for k in range(N_DEV - 1):
            m = k + 1
            peer_idx = 4 * (x ^ (m >> 2)) + 2 * (y ^ ((m >> 1) & 1)) + (c ^ (m & 1))
            pltpu.make_async_remote_copy(
                src_ref=p_ref, dst_ref=buf.at[peer_idx], send_sem=send_sems.at[k], recv_sem=recv_sems.at[k],
                device_id=(x, y, c), device_id_type=MESH).wait_recv()
        for cp in cps:
            cp.wait_send()
        acc = buf[0]
        for k in range(1, N_DEV):
            acc = acc + buf[k]
        o_ref[...] = acc

    return pl.pallas_call(
        body, name=name, out_shape=_S((R, LANES), F32),
        in_specs=[VMEM_SPEC], out_specs=VMEM_SPEC,
        scratch_shapes=[pltpu.VMEM((N_DEV, R, LANES), F32), pltpu.SemaphoreType.DMA((N_DEV - 1,)),
                        pltpu.SemaphoreType.DMA((N_DEV - 1,))],
        compiler_params=_cparams(),
    )(pack)


def _adamw_math(w, g, m, v):
    m = ADAM_B1 * m + (1.0 - ADAM_B1) * g
    v = ADAM_B2 * v + (1.0 - ADAM_B2) * (g * g)
    m_hat = m / (1.0 - ADAM_B1 ** ADAM_STEP)
    v_hat = v / (1.0 - ADAM_B2 ** ADAM_STEP)
    delta = -ADAM_LR * (m_hat / (jnp.sqrt(v_hat) + ADAM_EPS) + ADAM_WD * w)
    return delta, m, v


def _adamw_big(parts, w, m, v, name):
    r, cc = w.shape
    tr = _tile(r, 128, HALO)

    def body(p_ref, w_ref, m_ref, v_ref, g_ref, d_ref, nm_ref, nv_ref):
        g = p_ref[0].astype(F32)
        for q in range(1, N_CHIP):
            g = g + p_ref[q].astype(F32)
        g_ref[...] = g
        d_ref[...], nm_ref[...], nv_ref[...] = _adamw_math(w_ref[...], g, m_ref[...], v_ref[...])

    blk = pl.BlockSpec((tr, cc), lambda i: (i, 0))
    return pl.pallas_call(
        body, name=name, out_shape=tuple(_S((r, cc), F32) for _ in range(4)), grid=(r // tr,),
        in_specs=[pl.BlockSpec((N_CHIP, tr, cc), lambda i: (0, i, 0)), blk, blk, blk],
        out_specs=(blk, blk, blk, blk), compiler_params=_cparams(("parallel",)),
    )(parts, w, m, v)


def _adamw_small(ws, gs, ms, vs, name):
    n = len(ws)

    def body(*refs):
        w_r, g_r, m_r, v_r = refs[:n], refs[n:2 * n], refs[2 * n:3 * n], refs[3 * n:4 * n]
        d_r, nm_r, nv_r = refs[4 * n:5 * n], refs[5 * n:6 * n], refs[6 * n:7 * n]
        for k in range(n):
            d_r[k][...], nm_r[k][...], nv_r[k][...] = _adamw_math(w_r[k][...], g_r[k][...], m_r[k][...], v_r[k][...])

    shapes = tuple(_S(w.shape, F32) for w in ws)
    outs = pl.pallas_call(
        body, name=name, out_shape=shapes * 3,
        in_specs=[VMEM_SPEC] * (4 * n), out_specs=tuple([VMEM_SPEC] * (3 * n)),
        compiler_params=_cparams(),
    )(*ws, *gs, *ms, *vs)
    return outs[:n], outs[n:2 * n], outs[2 * n:]


def _block_diag(w, heads_per_block):
    H, hd, _ = w.shape
    nb = H // heads_per_block
    eye = jnp.eye(heads_per_block, dtype=w.dtype)
    w4 = w.reshape(nb, heads_per_block, hd, hd)
    return jnp.einsum("nhab,hg->nhagb", w4, eye).reshape(nb, heads_per_block * hd, heads_per_block * hd)


def _diag_blocks(bd, heads_per_block, hd):
    nb = bd.shape[0]
    b5 = bd.reshape(nb, heads_per_block, hd, heads_per_block, hd)
    return jnp.stack([b5[:, h, :, h, :] for h in range(heads_per_block)], axis=1).reshape(nb * heads_per_block, hd, hd)


def _as_rows(a):
    if a.ndim == 1:
        return a.reshape(-1, LANES) if a.shape[0] % LANES == 0 else a.reshape(1, -1)
    if a.ndim == 3:
        return a.reshape(-1, LANES) if (a.size % LANES == 0) else a.reshape(a.shape[0] * a.shape[1], a.shape[2])
    return a


def kernel(x, g_mix, w_in, lru_conv_w, lru_conv_b, lru_wa, lru_ba, lru_wx, lru_bx, lru_lambda, lru_w_out, sc_conv_w, sc_w_out, w_o, g_ffn, ffn_w_up, ffn_conv_w, ffn_w_down, g_final, loss_target, m_g_mix, m_w_in, m_lru_conv_w, m_lru_conv_b, m_lru_wa, m_lru_ba, m_lru_wx, m_lru_bx, m_lru_lambda, m_lru_w_out, m_sc_conv_w, m_sc_w_out, m_w_o, m_g_ffn, m_ffn_w_up, m_ffn_conv_w, m_ffn_w_down, m_g_final, v_g_mix, v_w_in, v_lru_conv_w, v_lru_conv_b, v_lru_wa, v_lru_ba, v_lru_wx, v_lru_bx, v_lru_lambda, v_lru_w_out, v_sc_conv_w, v_sc_w_out, v_w_o, v_g_ffn, v_ffn_w_up, v_ffn_conv_w, v_ffn_w_down, v_g_final):
    T, D = x.shape[1], x.shape[2]
    d_lru = lru_lambda.shape[0]
    d_sc = sc_conv_w.shape[1] * N_DEV
    F = ffn_w_down.shape[0] * N_DEV
    H = lru_wa.shape[0]
    assert d_lru == d_sc and H * HEAD_DIM == d_lru
    xs = x.reshape(T, D)
    tgt = loss_target.reshape(T, D)
    my_x, my_y, my_c = _place()
    me = 4 * my_x + 2 * my_y + my_c

    big = [w_in, lru_w_out, sc_w_out, w_o, ffn_w_up, ffn_w_down]
    big_names = ["w_in", "lru_w_out", "sc_w_out", "w_o", "ffn_w_up", "ffn_w_down"]
    me_idx = jnp.reshape(me, (1,)).astype(jnp.int32)
    big_bf = [_cast_into_slot(w, me_idx, "cast_" + nm) for w, nm in zip(big, big_names)]
    pad_rows = lambda a: jnp.pad(a, ((0, SUB - a.shape[0]), (0, 0)))
    gathered = _all_gather(big_bf + [pad_rows(lru_conv_w), pad_rows(sc_conv_w), pad_rows(ffn_conv_w)],
                           [True] * 6 + [False] * 3,
                           [True, True, True, False, False, False, True, True, True], "all_gather_first")
    W_in, W_lo, W_so, W_o8, W_up, W_dn8 = gathered[:6]
    full_cols = lambda g, kw: g[:, :kw, :].transpose(1, 0, 2).reshape(kw, -1)
    cw_lru = full_cols(gathered[6], lru_conv_w.shape[0])
    cw_sc = full_cols(gathered[7], sc_conv_w.shape[0])
    cw_ffn = full_cols(gathered[8], ffn_conv_w.shape[0])

    C = _tile(d_lru, C_LRU)
    hpb = C // HEAD_DIM
    wa_bd = _block_diag(lru_wa, hpb).astype(BF16)
    wx_bd = _block_diag(lru_wx, hpb).astype(BF16)
    cb, ba, bx, lam = (a.reshape(1, d_lru) for a in (lru_conv_b, lru_ba, lru_bx, lru_lambda))

    h1 = _rms_fwd(xs, g_mix, "rms_mix")
    k8 = W_up.shape[1] // 8
    p, ((W_o8,), (W_up,)) = _mm_nn(
        h1, W_in, out_dtype=BF16, name="mm_in",
        tasks=[_gather_task(W_o8, ici=ALL_ROWS), _gather_task(W_up, ici=(0, 5 * k8))])
    hs, yl_pre, ((W_o8,), (W_up,)) = _lru_fwd(
        p, cw_lru, cb, wa_bd, ba, wx_bd, bx, lam, name="lru_fwd",
        tasks=[_gather_task(W_o8, fwd=ALL_ROWS), _gather_task(W_up, ici=(5 * k8, 3 * k8), fwd=(0, 5 * k8))])
    ys_pre = _sc_fwd(p, cw_sc, d=d_sc, name="sc_fwd")
    y_lru, ((W_up,),) = _mm_small("nn", yl_pre, None, W_lo, name="mm_lru_out",
                                  tasks=[_gather_task(W_up, fwd=(5 * k8, 3 * k8))])
    y_sc = _mm_small("nn", ys_pre, None, W_so, name="mm_sc_out")
    gate0 = 2 * d_lru + 3 * d_sc
    merged = _merge_fwd(p, y_lru, y_sc, col0=gate0, name="merge_fwd")
    W_o = W_o8.reshape(1, D, D)
    x1 = _mm_nn(merged, W_o, out_dtype=F32, residual=xs, name="mm_o")
    h2 = _rms_fwd(x1, g_ffn, "rms_ffn")
    uu, ((W_dn8,),) = _mm_nn(h2, W_up, out_dtype=BF16, name="mm_up", tasks=[_gather_task(W_dn8, ici=ALL_ROWS)])
    act, ((W_dn8,),) = _ffn_act_fwd(uu, cw_ffn, name="ffn_act_fwd", tasks=[_gather_task(W_dn8, fwd=ALL_ROWS)])
    W_dn = W_dn8.reshape(1, F, D)
    x2 = _mm_nn(act, W_dn, out_dtype=F32, residual=x1, name="mm_down", tn=512, tk=F)
    dx2, dx2b, loss_part, dg_final = _loss_head(x2, g_final, tgt, "loss_head")

    place = jnp.stack([my_c, 2 * my_x + my_y]).astype(jnp.int32)

    def pack_rows(arrs):
        flat = jnp.concatenate([a.reshape(-1) for a in arrs])
        rows = -(-flat.shape[0] // (SUB * LANES)) * SUB
        return jnp.pad(flat, (0, rows * LANES - flat.shape[0])).reshape(rows, LANES)

    def unpack_rows(pack, arrs):
        flat, out, o = pack.reshape(-1), [], 0
        for a in arrs:
            out.append(flat[o:o + a.size].reshape(a.shape))
            o += a.size
        return out

    dact = _mm_nt(dx2b, W_dn, out_dtype=BF16, name="mm_down_dx", tko=1408)
    gW_dn = _mm_tn(act, dx2b, 1, out_dtype=BF16, name="mm_down_dw", tk=1408).reshape(N_DEV, F // N_DEV, D)
    duu, dcw_ffn_g, dcw_ffn_v = _ffn_act_bwd(uu, dact, cw_ffn, name="ffn_act_bwd")
    dh2, ((land_dn,),) = _mm_nt(duu, W_up, out_dtype=BF16, name="mm_up_dx", tasks=[_swap_task(gW_dn)])
    parts_dn = _add_halves(gW_dn, land_dn, place, "rs_add_ffn_w_down")
    gW_up, ((mine_dn,),) = _mm_tn(h2, duu, N_DEV, out_dtype=BF16, name="mm_up_dw", tk=2048,
                                  tasks=[_exchange_task(*parts_dn)])
    dx1, dx1b, dg_ffn = _rms_bwd(x1, g_ffn, dh2, dx2, "rms_ffn_bwd")
    dmerged, ((land_up,),) = _mm_nt(dx1b, W_o, out_dtype=BF16, name="mm_o_dx", tasks=[_swap_task(gW_up)])
    parts_up = _add_halves(gW_up, land_up, place, "rs_add_ffn_w_up")
    gW_o = _mm_tn(merged, dx1b, 1, out_dtype=BF16, name="mm_o_dw").reshape(N_DEV, D // N_DEV, D)
    dp, dy_lru, dy_sc = _merge_bwd(p, y_lru, y_sc, dmerged, col0=gate0, name="merge_bwd")
    dyl_pre, ((land_o,),) = _mm_small("nt", None, dy_lru, W_lo, name="mm_lru_out_dx", tasks=[_swap_task(gW_o)])
    parts_o = _add_halves(gW_o, land_o, place, "rs_add_w_o")
    gW_lo = _mm_small("tn", yl_pre, dy_lru, W_lo, name="mm_lru_out_dw")
    dys_pre, ((land_lo,),) = _mm_small("nt", None, dy_sc, W_so, name="mm_sc_out_dx", tasks=[_swap_task(gW_lo)])
    parts_lo = _add_halves(gW_lo, land_lo, place, "rs_add_lru_w_out")
    gW_so = _mm_small("tn", ys_pre, dy_sc, W_so, name="mm_sc_out_dw")
    dp, dcw_sc = _sc_bwd(p, dys_pre, dp, cw_sc, d=d_sc, name="sc_bwd")
    r_up = parts_up[0].shape[1] // 2
    dp, dcw_lru, dcb, dwa_bd, dba, dwx_bd, dbx, dlam, ((land_up,), (mine_o,), (mine_lo,)) = _lru_bwd(
        p, hs, dyl_pre, dp, cw_lru, cb, wa_bd, ba, wx_bd, bx, lam, name="lru_bwd",
        tasks=[_exchange_task(*parts_up, rows=(0, r_up)), _exchange_task(*parts_o), _exchange_task(*parts_lo)])

    dwa = _diag_blocks(dwa_bd, hpb, HEAD_DIM)
    dwx = _diag_blocks(dwx_bd, hpb, HEAD_DIM)
    dcw_ffn = jnp.concatenate([dcw_ffn_g, dcw_ffn_v], axis=1)
    small_full = [dcw_lru, dcb, dwa, dba, dwx, dbx, dlam, dcw_sc, dg_ffn, dcw_ffn, dg_final]
    gW_in, ((mine_up,), (land_so,), (packs,)) = _mm_tn(
        h1, dp, N_DEV, out_dtype=BF16, name="mm_in_dw", tk=2048,
        tasks=[_exchange_task(parts_up[0], land_up, rows=(r_up, r_up)), _swap_task(gW_so),
               _bcast_task(pack_rows(small_full))])
    parts_so = _add_halves(gW_so, land_so, place, "rs_add_sc_w_out")
    (land_in,) = _swap_halves([gW_in], "rs_swap_w_in")
    parts_in = _add_halves(gW_in, land_in, place, "rs_add_w_in")
    dh1, ((mine_in,), (mine_so,)) = _mm_nt(dp, W_in, out_dtype=BF16, name="mm_in_dx",
                                           tasks=[_exchange_task(*parts_in), _exchange_task(*parts_so)])
    grad_x, _, dg_mix = _rms_bwd(xs, g_mix, dh1, dx1, "rms_mix_bwd")

    mine = [mine_in, mine_lo, mine_so, mine_o, mine_up, mine_dn]
    big_m = [m_w_in, m_lru_w_out, m_sc_w_out, m_w_o, m_ffn_w_up, m_ffn_w_down]
    big_v = [v_w_in, v_lru_w_out, v_sc_w_out, v_w_o, v_ffn_w_up, v_ffn_w_down]
    big_out = {nm: _adamw_big(pt, w, m, v, "adamw_" + nm)
               for nm, pt, w, m, v in zip(big_names, mine, big, big_m, big_v)}

    (scw_lru, scb, swa, sba, swx, sbx, slam, scw_sc, sg_ffn, scw_ffn, sg_final) = unpack_rows(
        _sum_packs(packs, "sum_small"), small_full)
    (sg_mix,) = unpack_rows(_all_reduce_small(pack_rows([dg_mix]), "all_reduce_g_mix"), [dg_mix])

    def my_cols(a):
        n = a.shape[1] // N_DEV
        return lax.dynamic_slice_in_dim(a, me * n, n, axis=1)

    small_names = ["g_mix", "lru_conv_w", "lru_conv_b", "lru_wa", "lru_ba", "lru_wx", "lru_bx", "lru_lambda",
                   "sc_conv_w", "g_ffn", "ffn_conv_w", "g_final"]
    small_w = [g_mix, lru_conv_w, lru_conv_b, lru_wa, lru_ba, lru_wx, lru_bx, lru_lambda, sc_conv_w, g_ffn,
               ffn_conv_w, g_final]
    small_m = [m_g_mix, m_lru_conv_w, m_lru_conv_b, m_lru_wa, m_lru_ba, m_lru_wx, m_lru_bx, m_lru_lambda,
               m_sc_conv_w, m_g_ffn, m_ffn_conv_w, m_g_final]
    small_v = [v_g_mix, v_lru_conv_w, v_lru_conv_b, v_lru_wa, v_lru_ba, v_lru_wx, v_lru_bx, v_lru_lambda,
               v_sc_conv_w, v_g_ffn, v_ffn_conv_w, v_g_final]
    small_g = [sg_mix.reshape(D), my_cols(scw_lru), scb.reshape(d_lru), swa, sba.reshape(d_lru), swx,
               sbx.reshape(d_lru), slam.reshape(d_lru), my_cols(scw_sc), sg_ffn.reshape(D), my_cols(scw_ffn),
               sg_final.reshape(D)]
    sd, snm, snv = _adamw_small([_as_rows(a) for a in small_w], [_as_rows(a) for a in small_g],
                                [_as_rows(a) for a in small_m], [_as_rows(a) for a in small_v], "adamw_small")
    small_out = {nm: (g, d.reshape(w.shape), nm_.reshape(w.shape), nv_.reshape(w.shape))
                 for nm, w, g, d, nm_, nv_ in zip(small_names, small_w, small_g, sd, snm, snv)}

    loss = lax.psum(loss_part[0, 0], AXES)
    order = ["g_mix", "w_in", "lru_conv_w", "lru_conv_b", "lru_wa", "lru_ba", "lru_wx", "lru_bx", "lru_lambda",
             "lru_w_out", "sc_conv_w", "sc_w_out", "w_o", "g_ffn", "ffn_w_up", "ffn_conv_w", "ffn_w_down", "g_final"]
    res = {**big_out, **small_out}
    return (loss, grad_x.reshape(x.shape),
            *[res[nm][0] for nm in order], *[res[nm][1] for nm in order],
            *[res[nm][2] for nm in order], *[res[nm][3] for nm in order])
```

```python
import functools
import math

import jax
import jax.numpy as jnp
from jax import lax
from jax.experimental import pallas as pl
from jax.experimental.pallas import tpu as pltpu

F32, BF16 = jnp.float32, jnp.bfloat16
MESH = pl.DeviceIdType.MESH
N_DEV = 8
N_CHIP = 4
AXES = ("x", "y", "c")

EPS = 1e-6
LRU_C = 8.0
HEAD_DIM = 64
ADAM_LR, ADAM_B1, ADAM_B2, ADAM_EPS, ADAM_WD, ADAM_STEP = 0.001, 0.9, 0.999, 1e-08, 0.01, 10

VMEM_LIMIT = 48 * 1024 * 1024
LANES = 128
SUB = 8
HALO = 16
TB = 512
C_LRU = 256
C_EW = 512
TM, TN, TK = 512, 1536, 2048


def _tile(n, pref, align=LANES):
    best = None
    for d in range(align, min(n, pref) + 1, align):
        if n % d == 0:
            best = d
    return best or n


def _cparams(sem=None, vmem=VMEM_LIMIT):
    kw = dict(vmem_limit_bytes=vmem)
    if sem is not None:
        kw["dimension_semantics"] = sem
    return pltpu.CompilerParams(**kw)


def _S(shape, dtype):
    return jax.ShapeDtypeStruct(shape, dtype)


ANY = pl.BlockSpec(memory_space=pl.ANY)
VMEM_SPEC = pl.BlockSpec(memory_space=pltpu.VMEM)


class _Task:
    def __init__(self, arrays, aliased, start, wait, fresh=(), nsem=3):
        self.arrays, self.aliased, self.start, self.wait = arrays, aliased, start, wait
        self.fresh, self.nsem = list(fresh), nsem


def _call(name, grid, compute, in_specs, args, out_shape, out_specs, scratch, tasks=(), own_aliases=None):
    n_in, n_out, n_scr = len(args), len(out_shape), len(scratch)
    x_in, x_out, aliases, where = [], [], dict(own_aliases or {}), []
    for t in tasks:
        places = []
        for k, arr in enumerate(t.arrays):
            if k in t.aliased:
                aliases[n_in + len(x_in)] = n_out + len(x_out)
                places.append(("out", len(x_out)))
                x_out.append(_S(arr.shape, arr.dtype))
            else:
                places.append(("in", len(x_in)))
            x_in.append(arr)
        for shp in t.fresh:
            places.append(("out", len(x_out)))
            x_out.append(shp)
        where.append(places)
    n_xi, n_xo = len(x_in), len(x_out)

    def body(*refs):
        ins, xi = refs[:n_in], refs[n_in:n_in + n_xi]
        o0 = n_in + n_xi
        outs, xo = refs[o0:o0 + n_out], refs[o0 + n_out:o0 + n_out + n_xo]
        s0 = o0 + n_out + n_xo
        scr, sems = refs[s0:s0 + n_scr], refs[s0 + n_scr:]
        ids = [pl.program_id(a) for a in range(len(grid))]

        def task_refs(ti):
            return [xo[i] if kind == "out" else xi[i] for kind, i in where[ti]]

        if tasks:
            first = functools.reduce(jnp.logical_and, [i == 0 for i in ids])

            @pl.when(first)
            def _():
                for ti, t in enumerate(tasks):
                    t.start(task_refs(ti), *sems[3 * ti:3 * ti + 3])

        compute(*ins, *outs, *scr)
        if tasks:
            last = functools.reduce(jnp.logical_and, [i == g - 1 for i, g in zip(ids, grid)])

            @pl.when(last)
            def _():
                for ti, t in enumerate(tasks):
                    t.wait(task_refs(ti), *sems[3 * ti:3 * ti + 3])

    sem_shapes = []
    for t in tasks:
        sem_shapes += [pltpu.SemaphoreType.DMA((t.nsem,)), pltpu.SemaphoreType.DMA((t.nsem,)),
                       pltpu.SemaphoreType.DMA((1,))]
    res = pl.pallas_call(
        body, name=name, grid=grid,
        in_specs=list(in_specs) + [ANY] * n_xi,
        out_specs=tuple(out_specs) + (ANY,) * n_xo,
        out_shape=tuple(out_shape) + tuple(x_out),
        scratch_shapes=list(scratch) + sem_shapes,
        input_output_aliases=aliases,
        compiler_params=_cparams(("arbitrary",) * len(grid)),
    )(*args, *x_in)
    outs, passed, o = res[:n_out], [], n_out
    for places in where:
        k = sum(1 for kind, _ in places if kind == "out")
        passed.append(list(res[o:o + k]))
        o += k
    return outs, passed


def _mm_nn(a, w3, *, out_dtype, name, residual=None, tm=TM, tn=TN, tk=TK, tasks=()):
    M, K = a.shape
    G, _, n = w3.shape
    tm, tn, tk = _tile(M, tm, SUB), _tile(n, tn), _tile(K, tk)
    nj, nk = n // tn, K // tk

    def compute(*refs):
        if residual is None:
            a_ref, w_ref, o_ref = refs[:3]
            r_ref = None
        else:
            a_ref, w_ref, r_ref, o_ref = refs[:4]

        def finish(r):
            if r_ref is not None:
                r = r + r_ref[...]
            o_ref[...] = r.astype(o_ref.dtype)

        if nk == 1:
            finish(jnp.dot(a_ref[...], w_ref[...], preferred_element_type=F32))
            return
        acc = refs[-1]
        k = pl.program_id(3)

        @pl.when(k == 0)
        def _():
            acc[...] = jnp.zeros_like(acc)

        acc[...] += jnp.dot(a_ref[...], w_ref[...], preferred_element_type=F32)

        @pl.when(k == nk - 1)
        def _():
            finish(acc[...])

    in_specs = [pl.BlockSpec((tm, tk), lambda g, j, i, k: (i, k)),
                pl.BlockSpec((None, tk, tn), lambda g, j, i, k: (g, k, j))]
    args = [a, w3]
    if residual is not None:
        in_specs.append(pl.BlockSpec((tm, tn), lambda g, j, i, k: (i, g * nj + j)))
        args.append(residual)
    outs, passed = _call(
        name, (G, nj, M // tm, nk), compute, in_specs, args, [_S((M, G * n), out_dtype)],
        [pl.BlockSpec((tm, tn), lambda g, j, i, k: (i, g * nj + j))],
        [] if nk == 1 else [pltpu.VMEM((tm, tn), F32)], tasks)
    return (outs[0], passed) if tasks else outs[0]


def _mm_nt(dy, w3, *, out_dtype, name, tm=1024, tko=1024, tn=TN, tasks=()):
    M, _ = dy.shape
    G, K, n = w3.shape
    tm, tko, tn = _tile(M, tm, SUB), _tile(K, tko), _tile(n, tn)
    nj = n // tn
    nr = G * nj

    def compute(dy_ref, w_ref, o_ref, acc):
        r = pl.program_id(2)

        @pl.when(r == 0)
        def _():
            acc[...] = jnp.zeros_like(acc)

        acc[...] += lax.dot_general(dy_ref[...], w_ref[...], (((1,), (1,)), ((), ())),
                                    preferred_element_type=F32)

        @pl.when(r == nr - 1)
        def _():
            o_ref[...] = acc[...].astype(o_ref.dtype)

    outs, passed = _call(
        name, (K // tko, M // tm, nr), compute,
        [pl.BlockSpec((tm, tn), lambda ko, i, r: (i, r)),
         pl.BlockSpec((None, tko, tn), lambda ko, i, r: (r // nj, ko, r % nj))],
        [dy, w3], [_S((M, K), out_dtype)], [pl.BlockSpec((tm, tko), lambda ko, i, r: (i, ko))],
        [pltpu.VMEM((tm, tko), F32)], tasks)
    return (outs[0], passed) if tasks else outs[0]


def _mm_tn(a, dy, G, *, out_dtype, name, tk=1024, tn=TN, tt=1024, tasks=()):
    M, K = a.shape
    n = dy.shape[1] // G
    tk, tn, tt = _tile(K, tk), _tile(n, tn), _tile(M, tt, SUB)
    nj, nt = n // tn, M // tt

    def compute(a_ref, dy_ref, o_ref, acc):
        t = pl.program_id(3)

        @pl.when(t == 0)
        def _():
            acc[...] = jnp.zeros_like(acc)

        acc[...] += lax.dot_general(a_ref[...], dy_ref[...], (((0,), (0,)), ((), ())),
                                    preferred_element_type=F32)

        @pl.when(t == nt - 1)
        def _():
            o_ref[...] = acc[...].astype(o_ref.dtype)

    outs, passed = _call(
        name, (G, nj, K // tk, nt), compute,
        [pl.BlockSpec((tt, tk), lambda g, j, k, t: (t, k)),
         pl.BlockSpec((tt, tn), lambda g, j, k, t: (t, g * nj + j))],
        [a, dy], [_S((G, K, n), out_dtype)], [pl.BlockSpec((None, tk, tn), lambda g, j, k, t: (g, k, j))],
        [pltpu.VMEM((tk, tn), F32)], tasks)
    return (outs[0], passed) if tasks else outs[0]


def _mm_small(kind, a, b, w3, *, name, tm=1024, tasks=()):
    G, K, n = w3.shape
    M = (a if a is not None else b).shape[0]
    tm = _tile(M, tm, HALO)
    nt = M // tm
    w_spec = pl.BlockSpec((G, K, n), lambda i: (0, 0, 0))
    a_spec = pl.BlockSpec((tm, K), lambda i: (i, 0))
    b_spec = pl.BlockSpec((tm, G * n), lambda i: (i, 0))
    cols = lambda g: slice(g * n, (g + 1) * n)
    if kind == "nn":
        def compute(a_ref, w_ref, o_ref):
            av = a_ref[...]
            for g in range(G):
                o_ref[:, cols(g)] = jnp.dot(av, w_ref[g], preferred_element_type=F32).astype(o_ref.dtype)

        outs, passed = _call(name, (nt,), compute, [a_spec, w_spec], [a, w3], [_S((M, G * n), BF16)], [b_spec], [], tasks)
    elif kind == "nt":
        def compute(b_ref, w_ref, o_ref):
            acc = None
            for g in range(G):
                part = lax.dot_general(b_ref[:, cols(g)], w_ref[g], (((1,), (1,)), ((), ())),
                                       preferred_element_type=F32)
                acc = part if acc is None else acc + part
            o_ref[...] = acc.astype(o_ref.dtype)

        outs, passed = _call(name, (nt,), compute, [b_spec, w_spec], [b, w3], [_S((M, K), BF16)], [a_spec], [], tasks)
    else:
        def compute(a_ref, b_ref, o_ref, acc):
            i = pl.program_id(0)

            @pl.when(i == 0)
            def _():
                acc[...] = jnp.zeros_like(acc)

            at = a_ref[...].T
            for g in range(G):
                acc[g] += jnp.dot(at, b_ref[:, cols(g)], preferred_element_type=F32)

            @pl.when(i == nt - 1)
            def _():
                o_ref[...] = acc[...].astype(o_ref.dtype)

        outs, passed = _call(name, (nt,), compute, [a_spec, b_spec], [a, b], [_S((G, K, n), BF16)], [w_spec],
                             [pltpu.VMEM((G, K, n), F32)], tasks)
    return (outs[0], passed) if tasks else outs[0]


def _cast_into_slot(w, me_idx, name):
    R, C = w.shape
    tr = _tile(R, 512, HALO)

    def body(me_ref, w_ref, o_ref):
        del me_ref
        o_ref[...] = w_ref[...].astype(BF16)

    return pl.pallas_call(
        body, name=name, out_shape=_S((N_DEV, R, C), BF16),
        grid_spec=pltpu.PrefetchScalarGridSpec(
            num_scalar_prefetch=1, grid=(R // tr,),
            in_specs=[pl.BlockSpec((tr, C), lambda i, me_ref: (i, 0))],
            out_specs=pl.BlockSpec((None, tr, C), lambda i, me_ref: (me_ref[0], i, 0))),
        compiler_params=_cparams(("parallel",)),
    )(me_idx, w)


def _down(cur, prev8, j):
    return pltpu.roll(jnp.concatenate([prev8, cur], axis=0), j, 0)[SUB:, :]


def _up(cur, next8, j):
    n = cur.shape[0] + SUB
    return pltpu.roll(jnp.concatenate([cur, next8], axis=0), n - j, 0)[:cur.shape[0], :]


def _shifted_down(x, prev8, n):
    full = jnp.concatenate([prev8, x], axis=0)
    return [x] + [pltpu.roll(full, s, 0)[SUB:, :] for s in range(1, n)]


def _shifted_up(x, next8, n):
    m = x.shape[0] + SUB
    full = jnp.concatenate([x, next8], axis=0)
    return [x] + [pltpu.roll(full, m - s, 0)[:x.shape[0], :] for s in range(1, n)]


def _taps(sh, w_ref):
    kw = w_ref.shape[0]
    y = sh[0] * w_ref[pl.ds(kw - 1, 1), :]
    for k in range(kw - 1):
        y = y + sh[kw - 1 - k] * w_ref[pl.ds(k, 1), :]
    return y


def _conv(x, prev8, w_ref):
    return _taps(_shifted_down(x, prev8, w_ref.shape[0]), w_ref)


def _conv_t(dy, next8, w_ref):
    return _taps(_shifted_up(dy, next8, w_ref.shape[0]), w_ref)


def _conv_dw(dw_ref, dy, x, prev8, first):
    kw = dw_ref.shape[0]

    @pl.when(first)
    def _():
        dw_ref[...] = jnp.zeros_like(dw_ref)

    for k in range(kw):
        xs = x if k == kw - 1 else _down(x, prev8, kw - 1 - k)
        dw_ref[pl.ds(k, 1), :] += jnp.sum(dy * xs, axis=0, keepdims=True)


def _acc(ref, val, first):
    @pl.when(first)
    def _():
        ref[...] = jnp.zeros_like(ref)

    ref[...] += val


def _acc_row(ref, val, first):
    _acc(ref, jnp.sum(val, axis=0, keepdims=True), first)


def _prev8(h_ref, t):
    return jnp.where(t > 0, h_ref[...].astype(F32)[HALO - SUB:, :], 0.0)


def _next8(h_ref, is_last):
    return jnp.where(is_last, 0.0, h_ref[...].astype(F32)[:SUB, :])


_GELU_K0 = math.sqrt(2.0 / math.pi)
_GELU_K1 = 0.044715


def _gelu_and_grad(x):
    x2 = x * x
    th = jnp.tanh(_GELU_K0 * x * (1.0 + _GELU_K1 * x2))
    g = 0.5 * x * (1.0 + th)
    dg = 0.5 * (1.0 + th) + 0.5 * x * (1.0 - th * th) * (_GELU_K0 * (1.0 + 3.0 * _GELU_K1 * x2))
    return g, dg


def _neg_expm1(z):
    series = -z * (1.0 + z * (0.5 + z * (1.0 / 6.0 + z * (1.0 / 24.0))))
    return jnp.where(z > -0.03, series, 1.0 - jnp.exp(z))


def _store_staged(stages, dst_hbm, sems, step, n_steps, where):
    def copies(s, slot):
        return [pltpu.make_async_copy(
            st.at[slot], dst_hbm.at[pl.ds(r0, st.shape[1]), pl.ds(c0, st.shape[2])], sems.at[slot, k])
            for k, (st, (r0, c0)) in enumerate(zip(stages, where(s)))]

    slot = step % 2

    @pl.when(step > 0)
    def _():
        for cp in copies(step - 1, 1 - slot):
            cp.wait()

    for cp in copies(step, slot):
        cp.start()

    @pl.when(step == n_steps - 1)
    def _():
        for cp in copies(step, slot):
            cp.wait()


def _halo_prev_map(hb, col_fn):
    return lambda c, t: (jnp.maximum(t * hb - 1, 0), col_fn(c))


def _rms_fwd(x, g, name):
    T, D = x.shape
    tb = _tile(T, TB, SUB)

    def body(x_ref, g_ref, o_ref):
        xv = x_ref[...]
        rstd = lax.rsqrt(jnp.mean(xv * xv, axis=-1, keepdims=True) + EPS)
        o_ref[...] = (xv * rstd * g_ref[...]).astype(BF16)

    return pl.pallas_call(
        body, name=name, out_shape=_S((T, D), BF16), grid=(T // tb,),
        in_specs=[pl.BlockSpec((tb, D), lambda i: (i, 0)), pl.BlockSpec((1, D), lambda i: (0, 0))],
        out_specs=pl.BlockSpec((tb, D), lambda i: (i, 0)),
        compiler_params=_cparams(("parallel",)),
    )(x, g.reshape(1, D))


def _rms_bwd(x, g, dh, dres, name):
    T, D = x.shape
    tb = _tile(T, 256, SUB)

    def body(x_ref, g_ref, dh_ref, dr_ref, dx_ref, dxb_ref, dg_ref):
        i = pl.program_id(0)
        xv = x_ref[...]
        rstd = lax.rsqrt(jnp.mean(xv * xv, axis=-1, keepdims=True) + EPS)
        xn = xv * rstd
        dhv = dh_ref[...].astype(F32)
        _acc_row(dg_ref, dhv * xn, i == 0)
        dxn = dhv * g_ref[...]
        dx = dr_ref[...] + rstd * (dxn - xn * jnp.mean(dxn * xn, axis=-1, keepdims=True))
        dx_ref[...] = dx
        dxb_ref[...] = dx.astype(BF16)

    blk = pl.BlockSpec((tb, D), lambda i: (i, 0))
    vec = pl.BlockSpec((1, D), lambda i: (0, 0))
    return pl.pallas_call(
        body, name=name, out_shape=(_S((T, D), F32), _S((T, D), BF16), _S((1, D), F32)),
        grid=(T // tb,), in_specs=[blk, vec, blk, blk], out_specs=(blk, blk, vec),
        compiler_params=_cparams(("arbitrary",)),
    )(x, g.reshape(1, D), dh, dres)


def _loss_head(x2, g, target, name):
    T, D = x2.shape
    tb = _tile(T, 256, SUB)

    def body(x_ref, g_ref, t_ref, dx_ref, dxb_ref, loss_ref, dg_ref):
        i = pl.program_id(0)
        xv = x_ref[...]
        rstd = lax.rsqrt(jnp.mean(xv * xv, axis=-1, keepdims=True) + EPS)
        xn = xv * rstd
        err = xn * g_ref[...] - t_ref[...]
        part = 0.5 * jnp.sum(jnp.mean(err * err, axis=-1, keepdims=True), axis=0, keepdims=True)
        part = jnp.broadcast_to(part, (1, LANES))
        _acc(loss_ref, part, i == 0)
        dy = err * (1.0 / D)
        _acc_row(dg_ref, dy * xn, i == 0)
        dxn = dy * g_ref[...]
        dx = rstd * (dxn - xn * jnp.mean(dxn * xn, axis=-1, keepdims=True))
        dx_ref[...] = dx
        dxb_ref[...] = dx.astype(BF16)

    blk = pl.BlockSpec((tb, D), lambda i: (i, 0))
    vec = pl.BlockSpec((1, D), lambda i: (0, 0))
    return pl.pallas_call(
        body, name=name,
        out_shape=(_S((T, D), F32), _S((T, D), BF16), _S((1, LANES), F32), _S((1, D), F32)),
        grid=(T // tb,), in_specs=[blk, vec, blk],
        out_specs=(blk, blk, pl.BlockSpec((1, LANES), lambda i: (0, 0)), vec),
        compiler_params=_cparams(("arbitrary",)),
    )(x2, g.reshape(1, D), target)


def _lru_gates(xc, wa_ref, ba_ref, wx_ref, bx_ref, lam_ref):
    xcb = xc.astype(BF16)
    r = jax.nn.sigmoid(jnp.dot(xcb, wa_ref[...], preferred_element_type=F32) + ba_ref[...])
    i = jax.nn.sigmoid(jnp.dot(xcb, wx_ref[...], preferred_element_type=F32) + bx_ref[...])
    sp = jax.nn.softplus(-lam_ref[...])
    log_a = (-LRU_C * sp) * r
    a = jnp.exp(log_a)
    s = jnp.sqrt(_neg_expm1(2.0 * log_a))
    return xcb, r, i, a, s


def _lru_fwd(p, conv_w, conv_b, wa_bd, ba, wx_bd, bx, lam, *, name, tasks=()):
    T = p.shape[0]
    d = lam.shape[-1]
    C = _tile(d, C_LRU)
    nC = d // C
    tb = _tile(T, TB, HALO)
    nT, hb, nt = T // tb, tb // HALO, tb // SUB

    def body(x_ref, xh_ref, g_ref, cw_ref, cb_ref, wa_ref, ba_ref, wx_ref, bx_ref, lam_ref,
             hs_ref, y_ref, a_s, u_s, h_s):
        t = pl.program_id(1)

        @pl.when(t == 0)
        def _():
            h_s[...] = jnp.zeros_like(h_s)

        x = x_ref[...].astype(F32)
        xc = _conv(x, _prev8(xh_ref, t), cw_ref) + cb_ref[...]
        _, r, i, a, s = _lru_gates(xc, wa_ref, ba_ref, wx_ref, bx_ref, lam_ref)
        a_s[...] = a
        u_s[...] = s * (i * xc)
        row = lax.broadcasted_iota(jnp.int32, (SUB, C), 0)

        def step(k, h):
            o = pl.multiple_of(k * SUB, SUB)
            A = a_s[pl.ds(o, SUB), :]
            B = u_s[pl.ds(o, SUB), :]
            for sh in (1, 2, 4):
                m = row >= sh
                Ap = pltpu.roll(A, sh, 0)
                Bp = pltpu.roll(B, sh, 0)
                B = jnp.where(m, A * Bp + B, B)
                A = jnp.where(m, A * Ap, A)
            hs = A * h + B
            hs_ref[pl.ds(o, SUB), :] = hs
            return jnp.broadcast_to(hs[SUB - 1:SUB, :], (SUB, C))

        h_s[...] = lax.fori_loop(0, nt, step, h_s[...])
        gel, _ = _gelu_and_grad(g_ref[...].astype(F32))
        y_ref[...] = (gel * hs_ref[...]).astype(BF16)

    vec = pl.BlockSpec((1, C), lambda c, t: (0, c))
    sq = pl.BlockSpec((None, C, C), lambda c, t: (c, 0, 0))
    outs, passed = _call(
        name, (nC, nT), body,
        [pl.BlockSpec((tb, C), lambda c, t: (t, c)),
         pl.BlockSpec((HALO, C), _halo_prev_map(hb, lambda c: c)),
         pl.BlockSpec((tb, C), lambda c, t: (t, nC + c)),
         pl.BlockSpec((conv_w.shape[0], C), lambda c, t: (0, c)),
         vec, sq, vec, sq, vec, vec],
        [p, p, p, conv_w, conv_b, wa_bd, ba, wx_bd, bx, lam],
        [_S((T, d), F32), _S((T, d), BF16)],
        [pl.BlockSpec((tb, C), lambda c, t: (t, c)), pl.BlockSpec((tb, C), lambda c, t: (t, c))],
        [pltpu.VMEM((tb, C), F32), pltpu.VMEM((tb, C), F32), pltpu.VMEM((SUB, C), F32)], tasks)
    return (*outs, passed) if tasks else outs


def _lru_bwd(p, hs, dyl, dp, conv_w, conv_b, wa_bd, ba, wx_bd, bx, lam, *, name, tasks=()):
    T = p.shape[0]
    d = lam.shape[-1]
    C = _tile(d, C_LRU)
    nC = d // C
    tb = _tile(T, TB, HALO)
    nT, hb, nt = T // tb, tb // HALO, tb // SUB
    kw = conv_w.shape[0]

    def body(x_ref, xh_ref, g_ref, hs_ref, hh_ref, dy_ref, cw_ref, cb_ref, wa_ref, ba_ref, wx_ref, bx_ref,
             lam_ref, dp_in, dp_ref, dcw_ref, dcb_ref, dwa_ref, dba_ref, dwx_ref, dbx_ref, dlam_ref,
             b_s, g_s, dh_s, an_s, dhn_s, dxn_s, st_x, st_g, sems):
        del dp_in
        c = pl.program_id(0)
        tr = pl.program_id(1)
        t = nT - 1 - tr
        first = tr == 0

        @pl.when(first)
        def _():
            an_s[...] = jnp.zeros_like(an_s)
            dhn_s[...] = jnp.zeros_like(dhn_s)
            dxn_s[...] = jnp.zeros_like(dxn_s)

        x = x_ref[...].astype(F32)
        xprev = _prev8(xh_ref, t)
        xc = _conv(x, xprev, cw_ref) + cb_ref[...]
        xcb, r, i, a, s = _lru_gates(xc, wa_ref, ba_ref, wx_ref, bx_ref, lam_ref)
        hsv = hs_ref[...]
        dy = dy_ref[...].astype(F32)
        gel, dgel = _gelu_and_grad(g_ref[...].astype(F32))
        step_no = c * nT + tr
        slot = step_no % 2
        st_g[slot] = (dy * hsv * dgel).astype(BF16)

        b_s[...] = _up(a, an_s[...], 1)
        g_s[...] = dy * gel
        row = lax.broadcasted_iota(jnp.int32, (SUB, C), 0)

        def step(k, carry):
            o = pl.multiple_of((nt - 1 - k) * SUB, SUB)
            B = b_s[pl.ds(o, SUB), :]
            G = g_s[pl.ds(o, SUB), :]
            for sh in (1, 2, 4):
                m = row < SUB - sh
                Bn = pltpu.roll(B, SUB - sh, 0)
                Gn = pltpu.roll(G, SUB - sh, 0)
                G = jnp.where(m, B * Gn + G, G)
                B = jnp.where(m, B * Bn, B)
            dh = B * carry + G
            dh_s[pl.ds(o, SUB), :] = dh
            return jnp.broadcast_to(dh[0:1, :], (SUB, C))

        dhn_s[...] = lax.fori_loop(0, nt, step, dhn_s[...])
        an_s[...] = a[:SUB, :]
        dh = dh_s[...]

        hprev = _down(hsv, jnp.where(t > 0, hh_ref[...][HALO - SUB:, :], 0.0), 1)
        d_a = dh * hprev
        ixc = i * xc
        d_s = dh * ixc
        d_i = dh * s * xc
        d_xc = dh * s * i
        d_l = d_a * a - d_s * (a * a) / s
        sp = jax.nn.softplus(-lam_ref[...])
        _acc_row(dlam_ref, d_l * r * (LRU_C * jax.nn.sigmoid(-lam_ref[...])), first)
        d_zr = (d_l * (-LRU_C * sp)) * r * (1.0 - r)
        d_zi = d_i * i * (1.0 - i)
        _acc_row(dba_ref, d_zr, first)
        _acc_row(dbx_ref, d_zi, first)
        d_zrb = d_zr.astype(BF16)
        d_zib = d_zi.astype(BF16)
        tn_dims = (((0,), (0,)), ((), ()))
        nt_dims = (((1,), (1,)), ((), ()))
        gwa = lax.dot_general(xcb, d_zrb, tn_dims, preferred_element_type=F32)
        gwx = lax.dot_general(xcb, d_zib, tn_dims, preferred_element_type=F32)
        _acc(dwa_ref, gwa, first)
        _acc(dwx_ref, gwx, first)
        d_xc = (d_xc + lax.dot_general(d_zrb, wa_ref[...], nt_dims, preferred_element_type=F32)
                + lax.dot_general(d_zib, wx_ref[...], nt_dims, preferred_element_type=F32))
        _acc_row(dcb_ref, d_xc, first)
        _conv_dw(dcw_ref, d_xc, x, xprev, first)
        st_x[slot] = _conv_t(d_xc, dxn_s[...], cw_ref).astype(BF16)
        dxn_s[...] = d_xc[:SUB, :]

        def where(s):
            row0, col0 = (nT - 1 - s % nT) * tb, (s // nT) * C
            return [(row0, col0), (row0, d + col0)]

        _store_staged([st_x, st_g], dp_ref, sems, step_no, nC * nT, where)

    rev = lambda c, tr: (nT - 1 - tr, c)
    vec = pl.BlockSpec((1, C), lambda c, tr: (0, c))
    sq = pl.BlockSpec((None, C, C), lambda c, tr: (c, 0, 0))
    cwb = pl.BlockSpec((kw, C), lambda c, tr: (0, c))
    halo_prev = lambda c, tr: (jnp.maximum((nT - 1 - tr) * hb - 1, 0), c)
    outs, passed = _call(
        name, (nC, nT), body,
        [pl.BlockSpec((tb, C), rev),
         pl.BlockSpec((HALO, C), halo_prev),
         pl.BlockSpec((tb, C), lambda c, tr: (nT - 1 - tr, nC + c)),
         pl.BlockSpec((tb, C), rev),
         pl.BlockSpec((HALO, C), halo_prev),
         pl.BlockSpec((tb, C), rev),
         cwb, vec, sq, vec, sq, vec, vec, ANY],
        [p, p, p, hs, hs, dyl, conv_w, conv_b, wa_bd, ba, wx_bd, bx, lam, dp],
        [_S(dp.shape, dp.dtype), _S((kw, d), F32), _S((1, d), F32), _S((nC, C, C), F32), _S((1, d), F32),
         _S((nC, C, C), F32), _S((1, d), F32), _S((1, d), F32)],
        [ANY, cwb, vec, sq, vec, sq, vec, vec],
        [pltpu.VMEM((tb, C), F32), pltpu.VMEM((tb, C), F32), pltpu.VMEM((tb, C), F32),
         pltpu.VMEM((SUB, C), F32), pltpu.VMEM((SUB, C), F32), pltpu.VMEM((SUB, C), F32),
         pltpu.VMEM((2, tb, C), BF16), pltpu.VMEM((2, tb, C), BF16), pltpu.SemaphoreType.DMA((2, 2))],
        tasks, own_aliases={13: 0})
    return (*outs, passed) if tasks else outs


def _sc_fwd(p, conv_w, *, d, name):
    T = p.shape[0]
    C = _tile(d, C_EW)
    nC = d // C
    tb = _tile(T, TB, HALO)
    nT, hb = T // tb, tb // HALO

    def body(b_ref, c_ref, ch_ref, v_ref, vh_ref, w_ref, y_ref):
        t = pl.program_id(1)
        cv = c_ref[...].astype(F32) * v_ref[...].astype(F32)
        cvp = _prev8(ch_ref, t) * _prev8(vh_ref, t)
        y_ref[...] = (b_ref[...].astype(F32) * _conv(cv, cvp, w_ref)).astype(BF16)

    seg = lambda k: pl.BlockSpec((tb, C), lambda c, t: (t, k * nC + c))
    hseg = lambda k: pl.BlockSpec((HALO, C), _halo_prev_map(hb, lambda c: k * nC + c))
    return pl.pallas_call(
        body, name=name, out_shape=_S((T, d), BF16), grid=(nC, nT),
        in_specs=[seg(2), seg(3), hseg(3), seg(4), hseg(4), pl.BlockSpec((conv_w.shape[0], C), lambda c, t: (0, c))],
        out_specs=pl.BlockSpec((tb, C), lambda c, t: (t, c)),
        compiler_params=_cparams(("parallel", "parallel")),
    )(p, p, p, p, p, conv_w)


def _sc_bwd(p, dys, dp, conv_w, *, d, name):
    T = p.shape[0]
    C = _tile(d, C_EW)
    nC = d // C
    tb = _tile(T, TB, HALO)
    nT, hb = T // tb, tb // HALO
    kw = conv_w.shape[0]

    def body(b_ref, bn_ref, c_ref, ch_ref, v_ref, vh_ref, dy_ref, dyn_ref, w_ref, dp_in, dp_ref, dw_ref,
             st_b, st_c, st_v, sems):
        del dp_in
        c = pl.program_id(0)
        t = pl.program_id(1)
        last = t == nT - 1
        bv = b_ref[...].astype(F32)
        cvv = c_ref[...].astype(F32)
        vv = v_ref[...].astype(F32)
        dy = dy_ref[...].astype(F32)
        cv = cvv * vv
        cvp = _prev8(ch_ref, t) * _prev8(vh_ref, t)
        step_no = c * nT + t
        slot = step_no % 2
        st_b[slot] = (dy * _conv(cv, cvp, w_ref)).astype(BF16)
        dz = dy * bv
        dzn = _next8(dyn_ref, last) * _next8(bn_ref, last)
        _conv_dw(dw_ref, dz, cv, cvp, t == 0)
        dcv = _conv_t(dz, dzn, w_ref)
        st_c[slot] = (dcv * vv).astype(BF16)
        st_v[slot] = (dcv * cvv).astype(BF16)

        def where(s):
            return [((s % nT) * tb, (2 + k) * d + (s // nT) * C) for k in range(3)]

        _store_staged([st_b, st_c, st_v], dp_ref, sems, step_no, nC * nT, where)

    seg = lambda k: pl.BlockSpec((tb, C), lambda c, t: (t, k * nC + c))
    hseg = lambda k: pl.BlockSpec((HALO, C), _halo_prev_map(hb, lambda c: k * nC + c))
    last_h = T // HALO - 1
    nseg = lambda k: pl.BlockSpec((HALO, C), lambda c, t: (jnp.minimum((t + 1) * hb, last_h), k * nC + c))
    return pl.pallas_call(
        body, name=name, out_shape=(_S(dp.shape, dp.dtype), _S((kw, d), F32)), grid=(nC, nT),
        in_specs=[seg(2), nseg(2), seg(3), hseg(3), seg(4), hseg(4),
                  pl.BlockSpec((tb, C), lambda c, t: (t, c)), nseg(0),
                  pl.BlockSpec((kw, C), lambda c, t: (0, c)), ANY],
        out_specs=(ANY, pl.BlockSpec((kw, C), lambda c, t: (0, c))),
        scratch_shapes=[pltpu.VMEM((2, tb, C), BF16)] * 3 + [pltpu.SemaphoreType.DMA((2, 3))],
        input_output_aliases={9: 0},
        compiler_params=_cparams(("arbitrary", "arbitrary")),
    )(p, p, p, p, p, p, dys, dys, conv_w, dp)


def _merge_fwd(p, y_lru, y_sc, *, col0, name):
    T, D = y_lru.shape
    C = _tile(math.gcd(D, col0), 1024)
    nC = D // C
    k0 = col0 // C
    tb = _tile(T, 256, HALO)

    def body(gl_ref, gs_ref, yl_ref, ys_ref, o_ref):
        @pl.loop(0, tb // HALO)
        def _(k):
            rows = pl.ds(pl.multiple_of(k * HALO, HALO), HALO)
            for l0 in range(0, C, min(C, C_EW)):
                at = (rows, pl.ds(l0, min(C, C_EW)))
                o_ref[at] = (jax.nn.sigmoid(gl_ref[at].astype(F32)) * yl_ref[at].astype(F32)
                             + jax.nn.sigmoid(gs_ref[at].astype(F32)) * ys_ref[at].astype(F32)).astype(BF16)

    blk = pl.BlockSpec((tb, C), lambda c, t: (t, c))
    return pl.pallas_call(
        body, name=name, out_shape=_S((T, D), BF16), grid=(nC, T // tb),
        in_specs=[pl.BlockSpec((tb, C), lambda c, t: (t, k0 + c)),
                  pl.BlockSpec((tb, C), lambda c, t: (t, k0 + nC + c)), blk, blk],
        out_specs=blk, compiler_params=_cparams(("parallel", "parallel")),
    )(p, p, y_lru, y_sc)


def _merge_bwd(p, y_lru, y_sc, dm, *, col0, name):
    T, D = y_lru.shape
    C = _tile(math.gcd(D, col0), 1024)
    nC = D // C
    k0 = col0 // C
    tb = _tile(T, 256, HALO)
    nT = T // tb

    def body(gl_ref, gs_ref, yl_ref, ys_ref, dm_ref, dp_ref, dyl_ref, dys_ref, st_l, st_s, sems):
        step_no = pl.program_id(0) * nT + pl.program_id(1)
        slot = step_no % 2

        @pl.loop(0, tb // HALO)
        def _(k):
            rows = pl.ds(pl.multiple_of(k * HALO, HALO), HALO)
            for l0 in range(0, C, min(C, C_EW)):
                at = (rows, pl.ds(l0, min(C, C_EW)))
                dmv = dm_ref[at].astype(F32)
                sl = jax.nn.sigmoid(gl_ref[at].astype(F32))
                ss = jax.nn.sigmoid(gs_ref[at].astype(F32))
                dyl_ref[at] = (dmv * sl).astype(BF16)
                dys_ref[at] = (dmv * ss).astype(BF16)
                st_l[(slot,) + at] = (dmv * yl_ref[at].astype(F32) * sl * (1.0 - sl)).astype(BF16)
                st_s[(slot,) + at] = (dmv * ys_ref[at].astype(F32) * ss * (1.0 - ss)).astype(BF16)

        def where(s):
            row0, colc = (s % nT) * tb, (s // nT) * C
            return [(row0, col0 + colc), (row0, col0 + D + colc)]

        _store_staged([st_l, st_s], dp_ref, sems, step_no, nC * nT, where)

    blk = pl.BlockSpec((tb, C), lambda c, t: (t, c))
    return pl.pallas_call(
        body, name=name, out_shape=(_S(p.shape, BF16), _S((T, D), BF16), _S((T, D), BF16)),
        grid=(nC, nT),
        in_specs=[pl.BlockSpec((tb, C), lambda c, t: (t, k0 + c)),
                  pl.BlockSpec((tb, C), lambda c, t: (t, k0 + nC + c)), blk, blk, blk],
        out_specs=(ANY, blk, blk),
        scratch_shapes=[pltpu.VMEM((2, tb, C), BF16), pltpu.VMEM((2, tb, C), BF16), pltpu.SemaphoreType.DMA((2, 2))],
        compiler_params=_cparams(("arbitrary", "arbitrary")),
    )(p, p, y_lru, y_sc, dm)


def _ffn_act_fwd(uu, conv_w, *, name, tasks=()):
    T = uu.shape[0]
    F = uu.shape[1] // 2
    C = _tile(F, C_EW)
    nC = F // C
    tb = _tile(T, TB, HALO)
    nT, hb = T // tb, tb // HALO
    kw = conv_w.shape[0]
    R = HALO

    def body(g_ref, gh_ref, v_ref, vh_ref, wg_ref, wv_ref, o_ref):
        t = pl.program_id(1)

        def chunk(k, carry):
            gp, vp = carry
            r0 = pl.multiple_of(k * R, R)
            ug = g_ref[pl.ds(r0, R), :].astype(F32)
            uv = v_ref[pl.ds(r0, R), :].astype(F32)
            cg = _conv(ug, gp, wg_ref)
            cv = _conv(uv, vp, wv_ref)
            o_ref[pl.ds(r0, R), :] = (cg * jax.nn.sigmoid(cg) * cv).astype(BF16)
            return ug[R - SUB:, :], uv[R - SUB:, :]

        lax.fori_loop(0, tb // R, chunk, (_prev8(gh_ref, t), _prev8(vh_ref, t)))

    seg = lambda k: pl.BlockSpec((tb, C), lambda c, t: (t, k * nC + c))
    hseg = lambda k: pl.BlockSpec((HALO, C), _halo_prev_map(hb, lambda c: k * nC + c))
    wseg = lambda k: pl.BlockSpec((kw, C), lambda c, t: (0, k * nC + c))
    outs, passed = _call(
        name, (nC, nT), body, [seg(0), hseg(0), seg(1), hseg(1), wseg(0), wseg(1)],
        [uu, uu, uu, uu, conv_w, conv_w], [_S((T, F), BF16)], [pl.BlockSpec((tb, C), lambda c, t: (t, c))], [], tasks)
    return (outs[0], passed) if tasks else outs[0]


def _ffn_act_bwd(uu, dact, conv_w, *, name):
    T = uu.shape[0]
    F = uu.shape[1] // 2
    C = _tile(F, C_EW)
    nC = F // C
    tb = _tile(T, TB, HALO)
    nT, hb = T // tb, tb // HALO
    kw = conv_w.shape[0]
    R = HALO
    nk = tb // R

    def body(g_ref, gh_ref, v_ref, vh_ref, da_ref, wg_ref, wv_ref, du_ref, dwg_ref, dwv_ref,
             gn_s, vn_s, accg_s, accv_s, st_g, st_v, sems):
        c = pl.program_id(0)
        tr = pl.program_id(1)
        t = nT - 1 - tr
        first = tr == 0

        @pl.when(first)
        def _():
            gn_s[...] = jnp.zeros_like(gn_s)
            vn_s[...] = jnp.zeros_like(vn_s)
            dwg_ref[...] = jnp.zeros_like(dwg_ref)
            dwv_ref[...] = jnp.zeros_like(dwv_ref)

        accg_s[...] = jnp.zeros_like(accg_s)
        accv_s[...] = jnp.zeros_like(accv_s)
        step_no = c * nT + tr
        slot = step_no % 2

        def chunk(i, carry):
            gn, vn = carry
            k = nk - 1 - i
            r0 = pl.multiple_of(k * R, R)
            rp = pl.multiple_of(jnp.maximum(r0 - R, 0), R)
            ug = g_ref[pl.ds(r0, R), :].astype(F32)
            uv = v_ref[pl.ds(r0, R), :].astype(F32)
            gp = jnp.where(k > 0, g_ref[pl.ds(rp, R), :].astype(F32)[R - SUB:, :], _prev8(gh_ref, t))
            vp = jnp.where(k > 0, v_ref[pl.ds(rp, R), :].astype(F32)[R - SUB:, :], _prev8(vh_ref, t))
            sh_g = _shifted_down(ug, gp, kw)
            sh_v = _shifted_down(uv, vp, kw)
            cg = _taps(sh_g, wg_ref)
            cv = _taps(sh_v, wv_ref)
            da = da_ref[pl.ds(r0, R), :].astype(F32)
            sg = jax.nn.sigmoid(cg)
            d_cg = da * cv * (sg * (1.0 + cg * (1.0 - sg)))
            d_cv = da * (cg * sg)
            for j in range(kw):
                accg_s[j] += d_cg * sh_g[kw - 1 - j]
                accv_s[j] += d_cv * sh_v[kw - 1 - j]
            st_g[slot, pl.ds(r0, R), :] = _conv_t(d_cg, gn, wg_ref).astype(BF16)
            st_v[slot, pl.ds(r0, R), :] = _conv_t(d_cv, vn, wv_ref).astype(BF16)
            return d_cg[:SUB, :], d_cv[:SUB, :]

        gn, vn = lax.fori_loop(0, nk, chunk, (gn_s[...], vn_s[...]))
        gn_s[...] = gn
        vn_s[...] = vn
        for j in range(kw):
            dwg_ref[pl.ds(j, 1), :] += jnp.sum(accg_s[j], axis=0, keepdims=True)
            dwv_ref[pl.ds(j, 1), :] += jnp.sum(accv_s[j], axis=0, keepdims=True)
        def where(s):
            row0, col0 = (nT - 1 - s % nT) * tb, (s // nT) * C
            return [(row0, col0), (row0, F + col0)]

        _store_staged([st_g, st_v], du_ref, sems, step_no, nC * nT, where)

    seg = lambda k: pl.BlockSpec((tb, C), lambda c, tr: (nT - 1 - tr, k * nC + c))
    hseg = lambda k: pl.BlockSpec((HALO, C), lambda c, tr: (jnp.maximum((nT - 1 - tr) * hb - 1, 0), k * nC + c))
    wseg = lambda k: pl.BlockSpec((kw, C), lambda c, tr: (0, k * nC + c))
    dwb = pl.BlockSpec((kw, C), lambda c, tr: (0, c))
    return pl.pallas_call(
        body, name=name, out_shape=(_S(uu.shape, BF16), _S((kw, F), F32), _S((kw, F), F32)), grid=(nC, nT),
        in_specs=[seg(0), hseg(0), seg(1), hseg(1), pl.BlockSpec((tb, C), lambda c, tr: (nT - 1 - tr, c)),
                  wseg(0), wseg(1)],
        out_specs=(ANY, dwb, dwb),
        scratch_shapes=[pltpu.VMEM((SUB, C), F32), pltpu.VMEM((SUB, C), F32),
                        pltpu.VMEM((kw, R, C), F32), pltpu.VMEM((kw, R, C), F32),
                        pltpu.VMEM((2, tb, C), BF16), pltpu.VMEM((2, tb, C), BF16), pltpu.SemaphoreType.DMA((2, 2))],
        compiler_params=_cparams(("arbitrary", "arbitrary")),
    )(uu, uu, uu, uu, dact, conv_w, conv_w)


def _place():
    x, y, c = lax.axis_index("x"), lax.axis_index("y"), lax.axis_index("c")
    return x, y, c


def _chips(x, y):
    return [(1 - x, y), (x, 1 - y), (1 - x, 1 - y)]


def _all_gather(arrays, placed, over_ici, name):
    n = len(arrays)

    def body(*refs):
        ins, outs = refs[:n], refs[n:2 * n]
        send_sems, recv_sems, local_sems = refs[2 * n:]
        x, y, c = _place()
        me, sibling = (x, y, c), (x, y, 1 - c)
        chips = _chips(x, y)
        full = [a for a in range(n) if over_ici[a]]

        def idx(px, py, pc):
            return 4 * px + 2 * py + pc

        def copy(a, k, block, to):
            dst = outs[a].at[idx(*block)]
            src = ins[a] if (block is me and not placed[a]) else dst
            return pltpu.make_async_remote_copy(
                src_ref=src, dst_ref=dst, send_sem=send_sems.at[a, k], recv_sem=recv_sems.at[a, k],
                device_id=to, device_id_type=MESH)

        mine = [pltpu.make_async_copy(ins[a], outs[a].at[idx(*me)], local_sems.at[a])
                for a in range(n) if not placed[a]]
        for cp in mine:
            cp.start()
        first = []
        for a in full:
            first += [copy(a, 1 + j, me, (*chip, c)) for j, chip in enumerate(chips)]
        for a in range(n):
            first.append(copy(a, 0, me, sibling))
        for cp in first:
            cp.start()
        passed = []
        for a in full:
            for j, chip in enumerate(chips):
                copy(a, 1 + j, (*chip, c), me).wait_recv()
                cp = copy(a, 4 + j, (*chip, c), sibling)
                cp.start()
                passed.append(cp)
        for a in range(n):
            copy(a, 0, sibling, me).wait_recv()
        for a in full:
            for j, chip in enumerate(chips):
                copy(a, 4 + j, (*chip, 1 - c), me).wait_recv()
        for cp in first + passed:
            cp.wait_send()
        for cp in mine:
            cp.wait()

    return pl.pallas_call(
        body, name=name,
        out_shape=tuple(_S(s.shape if placed[a] else (N_DEV,) + s.shape, s.dtype) for a, s in enumerate(arrays)),
        in_specs=[ANY] * n, out_specs=tuple([ANY] * n),
        scratch_shapes=[pltpu.SemaphoreType.DMA((n, 7)), pltpu.SemaphoreType.DMA((n, 7)),
                        pltpu.SemaphoreType.DMA((n,))],
        input_output_aliases={a: a for a in range(n) if placed[a]},
    )(*arrays)


def _rows_of(ref, blk, rows):
    v = ref.at[blk]
    return v if rows is None else v.at[pl.ds(rows[0], rows[1])]


ALL_ROWS = "all"


def _gather_task(buf, ici=None, fwd=None):
    rows = lambda r: None if r == ALL_ROWS else r

    def copies(refs, ss, rs):
        x, y, c = _place()
        me = 4 * x + 2 * y + c
        cps = []
        for j, (px, py) in enumerate(_chips(x, y)):
            if ici is not None:
                blk = _rows_of(refs[0], me, rows(ici))
                cps.append(pltpu.make_async_remote_copy(
                    src_ref=blk, dst_ref=blk, send_sem=ss.at[j], recv_sem=rs.at[j],
                    device_id=(px, py, c), device_id_type=MESH))
            if fwd is not None:
                blk = _rows_of(refs[0], 4 * px + 2 * py + c, rows(fwd))
                cps.append(pltpu.make_async_remote_copy(
                    src_ref=blk, dst_ref=blk, send_sem=ss.at[3 + j], recv_sem=rs.at[3 + j],
                    device_id=(x, y, 1 - c), device_id_type=MESH))
        return cps

    def start(refs, ss, rs, ls):
        for cp in copies(refs, ss, rs):
            cp.start()

    def wait(refs, ss, rs, ls):
        x, y, c = _place()
        for j, (px, py) in enumerate(_chips(x, y)):
            if ici is not None:
                blk = _rows_of(refs[0], 4 * px + 2 * py + c, rows(ici))
                pltpu.make_async_remote_copy(
                    src_ref=blk, dst_ref=blk, send_sem=ss.at[j], recv_sem=rs.at[j],
                    device_id=(px, py, c), device_id_type=MESH).wait_recv()
            if fwd is not None:
                blk = _rows_of(refs[0], 4 * px + 2 * py + 1 - c, rows(fwd))
                pltpu.make_async_remote_copy(
                    src_ref=blk, dst_ref=blk, send_sem=ss.at[3 + j], recv_sem=rs.at[3 + j],
                    device_id=(x, y, 1 - c), device_id_type=MESH).wait_recv()
        for cp in copies(refs, ss, rs):
            cp.wait_send()

    return _Task([buf], [0], start, wait, nsem=6)


def _exchange_task(parts, landing, rows=None):
    def copies(refs, ss, rs):
        x, y, c = _place()
        myq = 2 * x + y
        return [pltpu.make_async_remote_copy(
            src_ref=_rows_of(refs[0], 2 * px + py, rows), dst_ref=_rows_of(refs[1], myq, rows),
            send_sem=ss.at[k], recv_sem=rs.at[k], device_id=(px, py, c), device_id_type=MESH)
            for k, (px, py) in enumerate(_chips(x, y))]

    def start(refs, ss, rs, ls):
        for cp in copies(refs, ss, rs):
            cp.start()

    def wait(refs, ss, rs, ls):
        x, y, c = _place()
        for k, (px, py) in enumerate(_chips(x, y)):
            pltpu.make_async_remote_copy(
                src_ref=_rows_of(refs[0], 2 * x + y, rows), dst_ref=_rows_of(refs[1], 2 * px + py, rows),
                send_sem=ss.at[k], recv_sem=rs.at[k], device_id=(px, py, c), device_id_type=MESH).wait_recv()
        for cp in copies(refs, ss, rs):
            cp.wait_send()

    return _Task([parts, landing], [1], start, wait)


def _swap_task(g):
    g4 = g.reshape((N_CHIP, 2) + g.shape[1:])

    def copy(refs, ss, rs):
        x, y, c = _place()
        return pltpu.make_async_remote_copy(
            src_ref=refs[0].at[:, 1 - c], dst_ref=refs[1], send_sem=ss.at[0], recv_sem=rs.at[0],
            device_id=(x, y, 1 - c), device_id_type=MESH)

    def start(refs, ss, rs, ls):
        copy(refs, ss, rs).start()

    def wait(refs, ss, rs, ls):
        copy(refs, ss, rs).wait()

    return _Task([g4], [], start, wait, fresh=[_S((N_CHIP,) + g.shape[1:], g.dtype)], nsem=1)


def _peer(x, y, c, m):
    return x ^ (m >> 2), y ^ ((m >> 1) & 1), c ^ (m & 1)


def _bcast_task(pack):
    def copies(refs, ss, rs):
        x, y, c = _place()
        me = 4 * x + 2 * y + c
        return [pltpu.make_async_remote_copy(
            src_ref=refs[0], dst_ref=refs[1].at[me], send_sem=ss.at[m - 1], recv_sem=rs.at[m - 1],
            device_id=_peer(x, y, c, m), device_id_type=MESH) for m in range(1, N_DEV)]

    def local(refs, ls):
        x, y, c = _place()
        return pltpu.make_async_copy(refs[0], refs[1].at[4 * x + 2 * y + c], ls.at[0])

    def start(refs, ss, rs, ls):
        local(refs, ls).start()
        for cp in copies(refs, ss, rs):
            cp.start()

    def wait(refs, ss, rs, ls):
        x, y, c = _place()
        for m in range(1, N_DEV):
            px, py, pc = _peer(x, y, c, m)
            pltpu.make_async_remote_copy(
                src_ref=refs[0], dst_ref=refs[1].at[4 * px + 2 * py + pc], send_sem=ss.at[m - 1],
                recv_sem=rs.at[m - 1], device_id=(px, py, pc), device_id_type=MESH).wait_recv()
        for cp in copies(refs, ss, rs):
            cp.wait_send()
        local(refs, ls).wait()

    return _Task([pack], [], start, wait, fresh=[_S((N_DEV,) + pack.shape, pack.dtype)], nsem=N_DEV - 1)


def _sum_packs(packs, name):
    _, R, L = packs.shape

    def body(p_ref, o_ref):
        acc = p_ref[0]
        for k in range(1, N_DEV):
            acc = acc + p_ref[k]
        o_ref[...] = acc

    return pl.pallas_call(body, name=name, out_shape=_S((R, L), packs.dtype), in_specs=[VMEM_SPEC],
                          out_specs=VMEM_SPEC, compiler_params=_cparams())(packs)


def _swap_halves(grads, name):
    n = len(grads)
    g4 = [g.reshape((N_CHIP, 2) + g.shape[1:]) for g in grads]

    def body(*refs):
        ins, outs = refs[:n], refs[n:2 * n]
        send_sems, recv_sems = refs[2 * n:]
        x, y, c = _place()
        cps = [pltpu.make_async_remote_copy(
            src_ref=ins[a].at[:, 1 - c], dst_ref=outs[a],
            send_sem=send_sems.at[a], recv_sem=recv_sems.at[a],
            device_id=(x, y, 1 - c), device_id_type=MESH) for a in range(n)]
        for cp in cps:
            cp.start()
        for cp in cps:
            cp.wait()

    return pl.pallas_call(
        body, name=name,
        out_shape=tuple(_S((N_CHIP,) + g.shape[1:], g.dtype) for g in grads),
        in_specs=[ANY] * n, out_specs=tuple([ANY] * n),
        scratch_shapes=[pltpu.SemaphoreType.DMA((n,)), pltpu.SemaphoreType.DMA((n,))],
    )(*g4)


def _add_halves(g, landed, place, name):
    _, r, cc = g.shape
    g4 = g.reshape(N_CHIP, 2, r, cc)
    tr = _tile(r, 512, HALO)

    def body(s_ref, g_ref, l_ref, o_ref, land_ref):
        q = pl.program_id(1)
        v = (g_ref[...].astype(F32) + l_ref[...].astype(F32)).astype(BF16)
        o_ref[...] = v

        @pl.when(q == s_ref[1])
        def _():
            land_ref[...] = v

    return pl.pallas_call(
        body, name=name, out_shape=(_S((N_CHIP, r, cc), BF16), _S((N_CHIP, r, cc), BF16)),
        grid_spec=pltpu.PrefetchScalarGridSpec(
            num_scalar_prefetch=1, grid=(r // tr, N_CHIP),
            in_specs=[pl.BlockSpec((None, None, tr, cc), lambda i, q, s: (q, s[0], i, 0)),
                      pl.BlockSpec((None, tr, cc), lambda i, q, s: (q, i, 0))],
            out_specs=(pl.BlockSpec((None, tr, cc), lambda i, q, s: (q, i, 0)),
                       pl.BlockSpec((None, tr, cc), lambda i, q, s: (s[1], i, 0)))),
        compiler_params=_cparams(("arbitrary", "arbitrary")),
    )(place, g4, landed)


def _all_reduce_small(pack, name):
    R = pack.shape[0]

    def body(p_ref, o_ref, buf, send_sems, recv_sems):
        x, y, c = _place()
        me = 4 * x + 2 * y + c
        buf[me] = p_ref[...]
        cps = []
        for k in range(N_DEV - 1):
            m = k + 1
            peer = (x ^ (m >> 2), y ^ ((m >> 1) & 1), c ^ (m & 1))
            cps.append(pltpu.make_async_remote_copy(
                src_ref=p_ref, dst_ref=buf.at[me], send_sem=send_sems.at[k], recv_sem=recv_sems.at[k],
                device_id=peer, device_id_type=MESH))
        for cp in cps:
            cp.start()
        for k in range(N_DEV - 1):
            m = k + 1
            peer_idx = 4 * (x ^ (m >> 2)) + 2 * (y ^ ((m >> 1) & 1)) + (c ^ (m & 1))
            pltpu.make_async_remote_copy(
                src_ref=p_ref, dst_ref=buf.at[peer_idx], send_sem=send_sems.at[k], recv_sem=recv_sems.at[k],
                device_id=(x, y, c), device_id_type=MESH).wait_recv()
        for cp in cps:
            cp.wait_send()
        acc = buf[0]
        for k in range(1, N_DEV):
            acc = acc + buf[k]
        o_ref[...] = acc

    return pl.pallas_call(
        body, name=name, out_shape=_S((R, LANES), F32),
        in_specs=[VMEM_SPEC], out_specs=VMEM_SPEC,
        scratch_shapes=[pltpu.VMEM((N_DEV, R, LANES), F32), pltpu.SemaphoreType.DMA((N_DEV - 1,)),
                        pltpu.SemaphoreType.DMA((N_DEV - 1,))],
        compiler_params=_cparams(),
    )(pack)


def _adamw_math(w, g, m, v):
    m = ADAM_B1 * m + (1.0 - ADAM_B1) * g
    v = ADAM_B2 * v + (1.0 - ADAM_B2) * (g * g)
    m_hat = m / (1.0 - ADAM_B1 ** ADAM_STEP)
    v_hat = v / (1.0 - ADAM_B2 ** ADAM_STEP)
    delta = -ADAM_LR * (m_hat / (jnp.sqrt(v_hat) + ADAM_EPS) + ADAM_WD * w)
    return delta, m, v


def _adamw_big(parts, w, m, v, name):
    r, cc = w.shape
    tr = _tile(r, 128, HALO)

    def body(p_ref, w_ref, m_ref, v_ref, g_ref, d_ref, nm_ref, nv_ref):
        g = p_ref[0].astype(F32)
        for q in range(1, N_CHIP):
            g = g + p_ref[q].astype(F32)
        g_ref[...] = g
        d_ref[...], nm_ref[...], nv_ref[...] = _adamw_math(w_ref[...], g, m_ref[...], v_ref[...])

    blk = pl.BlockSpec((tr, cc), lambda i: (i, 0))
    return pl.pallas_call(
        body, name=name, out_shape=tuple(_S((r, cc), F32) for _ in range(4)), grid=(r // tr,),
        in_specs=[pl.BlockSpec((N_CHIP, tr, cc), lambda i: (0, i, 0)), blk, blk, blk],
        out_specs=(blk, blk, blk, blk), compiler_params=_cparams(("parallel",)),
    )(parts, w, m, v)


def _adamw_small(ws, gs, ms, vs, name):
    n = len(ws)

    def body(*refs):
        w_r, g_r, m_r, v_r = refs[:n], refs[n:2 * n], refs[2 * n:3 * n], refs[3 * n:4 * n]
        d_r, nm_r, nv_r = refs[4 * n:5 * n], refs[5 * n:6 * n], refs[6 * n:7 * n]
        for k in range(n):
            d_r[k][...], nm_r[k][...], nv_r[k][...] = _adamw_math(w_r[k][...], g_r[k][...], m_r[k][...], v_r[k][...])

    shapes = tuple(_S(w.shape, F32) for w in ws)
    outs = pl.pallas_call(
        body, name=name, out_shape=shapes * 3,
        in_specs=[VMEM_SPEC] * (4 * n), out_specs=tuple([VMEM_SPEC] * (3 * n)),
        compiler_params=_cparams(),
    )(*ws, *gs, *ms, *vs)
    return outs[:n], outs[n:2 * n], outs[2 * n:]


def _block_diag(w, heads_per_block):
    H, hd, _ = w.shape
    nb = H // heads_per_block
    eye = jnp.eye(heads_per_block, dtype=w.dtype)
    w4 = w.reshape(nb, heads_per_block, hd, hd)
    return jnp.einsum("nhab,hg->nhagb", w4, eye).reshape(nb, heads_per_block * hd, heads_per_block * hd)


def _diag_blocks(bd, heads_per_block, hd):
    nb = bd.shape[0]
    b5 = bd.reshape(nb, heads_per_block, hd, heads_per_block, hd)
    return jnp.stack([b5[:, h, :, h, :] for h in range(heads_per_block)], axis=1).reshape(nb * heads_per_block, hd, hd)


def _as_rows(a):
    if a.ndim == 1:
        return a.reshape(-1, LANES) if a.shape[0] % LANES == 0 else a.reshape(1, -1)
    if a.ndim == 3:
        return a.reshape(-1, LANES) if (a.size % LANES == 0) else a.reshape(a.shape[0] * a.shape[1], a.shape[2])
    return a


def kernel(x, g_mix, w_in, lru_conv_w, lru_conv_b, lru_wa, lru_ba, lru_wx, lru_bx, lru_lambda, lru_w_out, sc_conv_w, sc_w_out, w_o, g_ffn, ffn_w_up, ffn_conv_w, ffn_w_down, g_final, loss_target, m_g_mix, m_w_in, m_lru_conv_w, m_lru_conv_b, m_lru_wa, m_lru_ba, m_lru_wx, m_lru_bx, m_lru_lambda, m_lru_w_out, m_sc_conv_w, m_sc_w_out, m_w_o, m_g_ffn, m_ffn_w_up, m_ffn_conv_w, m_ffn_w_down, m_g_final, v_g_mix, v_w_in, v_lru_conv_w, v_lru_conv_b, v_lru_wa, v_lru_ba, v_lru_wx, v_lru_bx, v_lru_lambda, v_lru_w_out, v_sc_conv_w, v_sc_w_out, v_w_o, v_g_ffn, v_ffn_w_up, v_ffn_conv_w, v_ffn_w_down, v_g_final):
    T, D = x.shape[1], x.shape[2]
    d_lru = lru_lambda.shape[0]
    d_sc = sc_conv_w.shape[1] * N_DEV
    F = ffn_w_down.shape[0] * N_DEV
    H = lru_wa.shape[0]
    assert d_lru == d_sc and H * HEAD_DIM == d_lru
    xs = x.reshape(T, D)
    tgt = loss_target.reshape(T, D)
    my_x, my_y, my_c = _place()
    me = 4 * my_x + 2 * my_y + my_c

    big = [w_in, lru_w_out, sc_w_out, w_o, ffn_w_up, ffn_w_down]
    big_names = ["w_in", "lru_w_out", "sc_w_out", "w_o", "ffn_w_up", "ffn_w_down"]
    me_idx = jnp.reshape(me, (1,)).astype(jnp.int32)
    big_bf = [_cast_into_slot(w, me_idx, "cast_" + nm) for w, nm in zip(big, big_names)]
    pad_rows = lambda a: jnp.pad(a, ((0, SUB - a.shape[0]), (0, 0)))
    gathered = _all_gather(big_bf + [pad_rows(lru_conv_w), pad_rows(sc_conv_w), pad_rows(ffn_conv_w)],
                           [True] * 6 + [False] * 3,
                           [True, True, True, False, False, False, True, True, True], "all_gather_first")
    W_in, W_lo, W_so, W_o8, W_up, W_dn8 = gathered[:6]
    full_cols = lambda g, kw: g[:, :kw, :].transpose(1, 0, 2).reshape(kw, -1)
    cw_lru = full_cols(gathered[6], lru_conv_w.shape[0])
    cw_sc = full_cols(gathered[7], sc_conv_w.shape[0])
    cw_ffn = full_cols(gathered[8], ffn_conv_w.shape[0])

    C = _tile(d_lru, C_LRU)
    hpb = C // HEAD_DIM
    wa_bd = _block_diag(lru_wa, hpb).astype(BF16)
    wx_bd = _block_diag(lru_wx, hpb).astype(BF16)
    cb, ba, bx, lam = (a.reshape(1, d_lru) for a in (lru_conv_b, lru_ba, lru_bx, lru_lambda))

    h1 = _rms_fwd(xs, g_mix, "rms_mix")
    k8 = W_up.shape[1] // 8
    p, ((W_o8,), (W_up,)) = _mm_nn(
        h1, W_in, out_dtype=BF16, name="mm_in",
        tasks=[_gather_task(W_o8, ici=ALL_ROWS), _gather_task(W_up, ici=(0, 5 * k8))])
    hs, yl_pre, ((W_o8,), (W_up,)) = _lru_fwd(
        p, cw_lru, cb, wa_bd, ba, wx_bd, bx, lam, name="lru_fwd",
        tasks=[_gather_task(W_o8, fwd=ALL_ROWS), _gather_task(W_up, ici=(5 * k8, 3 * k8), fwd=(0, 5 * k8))])
    ys_pre = _sc_fwd(p, cw_sc, d=d_sc, name="sc_fwd")
    y_lru, ((W_up,),) = _mm_small("nn", yl_pre, None, W_lo, name="mm_lru_out",
                                  tasks=[_gather_task(W_up, fwd=(5 * k8, 3 * k8))])
    y_sc = _mm_small("nn", ys_pre, None, W_so, name="mm_sc_out")
    gate0 = 2 * d_lru + 3 * d_sc
    merged = _merge_fwd(p, y_lru, y_sc, col0=gate0, name="merge_fwd")
    W_o = W_o8.reshape(1, D, D)
    x1 = _mm_nn(merged, W_o, out_dtype=F32, residual=xs, name="mm_o")
    h2 = _rms_fwd(x1, g_ffn, "rms_ffn")
    uu, ((W_dn8,),) = _mm_nn(h2, W_up, out_dtype=BF16, name="mm_up", tasks=[_gather_task(W_dn8, ici=ALL_ROWS)])
    act, ((W_dn8,),) = _ffn_act_fwd(uu, cw_ffn, name="ffn_act_fwd", tasks=[_gather_task(W_dn8, fwd=ALL_ROWS)])
    W_dn = W_dn8.reshape(1, F, D)
    x2 = _mm_nn(act, W_dn, out_dtype=F32, residual=x1, name="mm_down", tn=512, tk=F)
    dx2, dx2b, loss_part, dg_final = _loss_head(x2, g_final, tgt, "loss_head")

    place = jnp.stack([my_c, 2 * my_x + my_y]).astype(jnp.int32)

    def pack_rows(arrs):
        flat = jnp.concatenate([a.reshape(-1) for a in arrs])
        rows = -(-flat.shape[0] // (SUB * LANES)) * SUB
        return jnp.pad(flat, (0, rows * LANES - flat.shape[0])).reshape(rows, LANES)

    def unpack_rows(pack, arrs):
        flat, out, o = pack.reshape(-1), [], 0
        for a in arrs:
            out.append(flat[o:o + a.size].reshape(a.shape))
            o += a.size
        return out

    dact = _mm_nt(dx2b, W_dn, out_dtype=BF16, name="mm_down_dx", tko=1408)
    gW_dn = _mm_tn(act, dx2b, 1, out_dtype=BF16, name="mm_down_dw", tk=1408).reshape(N_DEV, F // N_DEV, D)
    duu, dcw_ffn_g, dcw_ffn_v = _ffn_act_bwd(uu, dact, cw_ffn, name="ffn_act_bwd")
    dh2, ((land_dn,),) = _mm_nt(duu, W_up, out_dtype=BF16, name="mm_up_dx", tasks=[_swap_task(gW_dn)])
    parts_dn = _add_halves(gW_dn, land_dn, place, "rs_add_ffn_w_down")
    gW_up, ((mine_dn,),) = _mm_tn(h2, duu, N_DEV, out_dtype=BF16, name="mm_up_dw", tk=2048,
                                  tasks=[_exchange_task(*parts_dn)])
    dx1, dx1b, dg_ffn = _rms_bwd(x1, g_ffn, dh2, dx2, "rms_ffn_bwd")
    dmerged, ((land_up,),) = _mm_nt(dx1b, W_o, out_dtype=BF16, name="mm_o_dx", tasks=[_swap_task(gW_up)])
    parts_up = _add_halves(gW_up, land_up, place, "rs_add_ffn_w_up")
    gW_o = _mm_tn(merged, dx1b, 1, out_dtype=BF16, name="mm_o_dw").reshape(N_DEV, D // N_DEV, D)
    dp, dy_lru, dy_sc = _merge_bwd(p, y_lru, y_sc, dmerged, col0=gate0, name="merge_bwd")
    dyl_pre, ((land_o,),) = _mm_small("nt", None, dy_lru, W_lo, name="mm_lru_out_dx", tasks=[_swap_task(gW_o)])
    parts_o = _add_halves(gW_o, land_o, place, "rs_add_w_o")
    gW_lo = _mm_small("tn", yl_pre, dy_lru, W_lo, name="mm_lru_out_dw")
    dys_pre, ((land_lo,),) = _mm_small("nt", None, dy_sc, W_so, name="mm_sc_out_dx", tasks=[_swap_task(gW_lo)])
    parts_lo = _add_halves(gW_lo, land_lo, place, "rs_add_lru_w_out")
    gW_so = _mm_small("tn", ys_pre, dy_sc, W_so, name="mm_sc_out_dw")
    dp, dcw_sc = _sc_bwd(p, dys_pre, dp, cw_sc, d=d_sc, name="sc_bwd")
    r_up = parts_up[0].shape[1] // 2
    dp, dcw_lru, dcb, dwa_bd, dba, dwx_bd, dbx, dlam, ((land_up,), (mine_o,), (mine_lo,), (land_so,)) = _lru_bwd(
        p, hs, dyl_pre, dp, cw_lru, cb, wa_bd, ba, wx_bd, bx, lam, name="lru_bwd",
        tasks=[_exchange_task(*parts_up, rows=(0, r_up)), _exchange_task(*parts_o), _exchange_task(*parts_lo),
               _swap_task(gW_so)])
    parts_so = _add_halves(gW_so, land_so, place, "rs_add_sc_w_out")

    dwa = _diag_blocks(dwa_bd, hpb, HEAD_DIM)
    dwx = _diag_blocks(dwx_bd, hpb, HEAD_DIM)
    dcw_ffn = jnp.concatenate([dcw_ffn_g, dcw_ffn_v], axis=1)
    small_full = [dcw_lru, dcb, dwa, dba, dwx, dbx, dlam, dcw_sc, dg_ffn, dcw_ffn, dg_final]
    gW_in, ((mine_up,), (mine_so,), (packs,)) = _mm_tn(
        h1, dp, N_DEV, out_dtype=BF16, name="mm_in_dw", tk=2048,
        tasks=[_exchange_task(parts_up[0], land_up, rows=(r_up, r_up)), _exchange_task(*parts_so),
               _bcast_task(pack_rows(small_full))])
    (land_in,) = _swap_halves([gW_in], "rs_swap_w_in")
    parts_in = _add_halves(gW_in, land_in, place, "rs_add_w_in")
    dh1, ((mine_in,),) = _mm_nt(dp, W_in, out_dtype=BF16, name="mm_in_dx", tasks=[_exchange_task(*parts_in)])
    grad_x, _, dg_mix = _rms_bwd(xs, g_mix, dh1, dx1, "rms_mix_bwd")

    mine = [mine_in, mine_lo, mine_so, mine_o, mine_up, mine_dn]
    big_m = [m_w_in, m_lru_w_out, m_sc_w_out, m_w_o, m_ffn_w_up, m_ffn_w_down]
    big_v = [v_w_in, v_lru_w_out, v_sc_w_out, v_w_o, v_ffn_w_up, v_ffn_w_down]
    big_out = {nm: _adamw_big(pt, w, m, v, "adamw_" + nm)
               for nm, pt, w, m, v in zip(big_names, mine, big, big_m, big_v)}

    (scw_lru, scb, swa, sba, swx, sbx, slam, scw_sc, sg_ffn, scw_ffn, sg_final) = unpack_rows(
        _sum_packs(packs, "sum_small"), small_full)
    (sg_mix,) = unpack_rows(_all_reduce_small(pack_rows([dg_mix]), "all_reduce_g_mix"), [dg_mix])

    def my_cols(a):
        n = a.shape[1] // N_DEV
        return lax.dynamic_slice_in_dim(a, me * n, n, axis=1)

    small_names = ["g_mix", "lru_conv_w", "lru_conv_b", "lru_wa", "lru_ba", "lru_wx", "lru_bx", "lru_lambda",
                   "sc_conv_w", "g_ffn", "ffn_conv_w", "g_final"]
    small_w = [g_mix, lru_conv_w, lru_conv_b, lru_wa, lru_ba, lru_wx, lru_bx, lru_lambda, sc_conv_w, g_ffn,
               ffn_conv_w, g_final]
    small_m = [m_g_mix, m_lru_conv_w, m_lru_conv_b, m_lru_wa, m_lru_ba, m_lru_wx, m_lru_bx, m_lru_lambda,
               m_sc_conv_w, m_g_ffn, m_ffn_conv_w, m_g_final]
    small_v = [v_g_mix, v_lru_conv_w, v_lru_conv_b, v_lru_wa, v_lru_ba, v_lru_wx, v_lru_bx, v_lru_lambda,
               v_sc_conv_w, v_g_ffn, v_ffn_conv_w, v_g_final]
    small_g = [sg_mix.reshape(D), my_cols(scw_lru), scb.reshape(d_lru), swa, sba.reshape(d_lru), swx,
               sbx.reshape(d_lru), slam.reshape(d_lru), my_cols(scw_sc), sg_ffn.reshape(D), my_cols(scw_ffn),
               sg_final.reshape(D)]
    sd, snm, snv = _adamw_small([_as_rows(a) for a in small_w], [_as_rows(a) for a in small_g],
                                [_as_rows(a) for a in small_m], [_as_rows(a) for a in small_v], "adamw_small")
    small_out = {nm: (g, d.reshape(w.shape), nm_.reshape(w.shape), nv_.reshape(w.shape))
                 for nm, w, g, d, nm_, nv_ in zip(small_names, small_w, small_g, sd, snm, snv)}

    loss = lax.psum(loss_part[0, 0], AXES)
    order = ["g_mix", "w_in", "lru_conv_w", "lru_conv_b", "lru_wa", "lru_ba", "lru_wx", "lru_bx", "lru_lambda",
             "lru_w_out", "sc_conv_w", "sc_w_out", "w_o", "g_ffn", "ffn_w_up", "ffn_conv_w", "ffn_w_down", "g_final"]
    res = {**big_out, **small_out}
    return (loss, grad_x.reshape(x.shape),
            *[res[nm][0] for nm in order], *[res[nm][1] for nm in order],
            *[res[nm][2] for nm in order], *[res[nm][3] for nm in order])
```

```python
import functools
import math

import jax
import jax.numpy as jnp
from jax import lax
from jax.experimental import pallas as pl
from jax.experimental.pallas import tpu as pltpu

F32, BF16 = jnp.float32, jnp.bfloat16
MESH = pl.DeviceIdType.MESH
N_DEV = 8
N_CHIP = 4
AXES = ("x", "y", "c")

EPS = 1e-6
LRU_C = 8.0
HEAD_DIM = 64
ADAM_LR, ADAM_B1, ADAM_B2, ADAM_EPS, ADAM_WD, ADAM_STEP = 0.001, 0.9, 0.999, 1e-08, 0.01, 10

VMEM_LIMIT = 48 * 1024 * 1024
LANES = 128
SUB = 8
HALO = 16
TB = 512
C_LRU = 256
C_EW = 512
TM, TN, TK = 512, 1536, 2048


def _tile(n, pref, align=LANES):
    best = None
    for d in range(align, min(n, pref) + 1, align):
        if n % d == 0:
            best = d
    return best or n


def _cparams(sem=None, vmem=VMEM_LIMIT):
    kw = dict(vmem_limit_bytes=vmem)
    if sem is not None:
        kw["dimension_semantics"] = sem
    return pltpu.CompilerParams(**kw)


def _S(shape, dtype):
    return jax.ShapeDtypeStruct(shape, dtype)


ANY = pl.BlockSpec(memory_space=pl.ANY)
VMEM_SPEC = pl.BlockSpec(memory_space=pltpu.VMEM)


class _Task:
    def __init__(self, arrays, aliased, start, wait, fresh=(), nsem=3):
        self.arrays, self.aliased, self.start, self.wait = arrays, aliased, start, wait
        self.fresh, self.nsem = list(fresh), nsem


def _call(name, grid, compute, in_specs, args, out_shape, out_specs, scratch, tasks=(), own_aliases=None):
    n_in, n_out, n_scr = len(args), len(out_shape), len(scratch)
    x_in, x_out, aliases, where = [], [], dict(own_aliases or {}), []
    for t in tasks:
        places = []
        for k, arr in enumerate(t.arrays):
            if k in t.aliased:
                aliases[n_in + len(x_in)] = n_out + len(x_out)
                places.append(("out", len(x_out)))
                x_out.append(_S(arr.shape, arr.dtype))
            else:
                places.append(("in", len(x_in)))
            x_in.append(arr)
        for shp in t.fresh:
            places.append(("out", len(x_out)))
            x_out.append(shp)
        where.append(places)
    n_xi, n_xo = len(x_in), len(x_out)

    def body(*refs):
        ins, xi = refs[:n_in], refs[n_in:n_in + n_xi]
        o0 = n_in + n_xi
        outs, xo = refs[o0:o0 + n_out], refs[o0 + n_out:o0 + n_out + n_xo]
        s0 = o0 + n_out + n_xo
        scr, sems = refs[s0:s0 + n_scr], refs[s0 + n_scr:]
        ids = [pl.program_id(a) for a in range(len(grid))]

        def task_refs(ti):
            return [xo[i] if kind == "out" else xi[i] for kind, i in where[ti]]

        if tasks:
            first = functools.reduce(jnp.logical_and, [i == 0 for i in ids])

            @pl.when(first)
            def _():
                for ti, t in enumerate(tasks):
                    t.start(task_refs(ti), *sems[3 * ti:3 * ti + 3])

        compute(*ins, *outs, *scr)
        if tasks:
            last = functools.reduce(jnp.logical_and, [i == g - 1 for i, g in zip(ids, grid)])

            @pl.when(last)
            def _():
                for ti, t in enumerate(tasks):
                    t.wait(task_refs(ti), *sems[3 * ti:3 * ti + 3])

    sem_shapes = []
    for t in tasks:
        sem_shapes += [pltpu.SemaphoreType.DMA((t.nsem,)), pltpu.SemaphoreType.DMA((t.nsem,)),
                       pltpu.SemaphoreType.DMA((1,))]
    res = pl.pallas_call(
        body, name=name, grid=grid,
        in_specs=list(in_specs) + [ANY] * n_xi,
        out_specs=tuple(out_specs) + (ANY,) * n_xo,
        out_shape=tuple(out_shape) + tuple(x_out),
        scratch_shapes=list(scratch) + sem_shapes,
        input_output_aliases=aliases,
        compiler_params=_cparams(("arbitrary",) * len(grid)),
    )(*args, *x_in)
    outs, passed, o = res[:n_out], [], n_out
    for places in where:
        k = sum(1 for kind, _ in places if kind == "out")
        passed.append(list(res[o:o + k]))
        o += k
    return outs, passed


def _mm_nn(a, w3, *, out_dtype, name, residual=None, tm=TM, tn=TN, tk=TK, tasks=()):
    M, K = a.shape
    G, _, n = w3.shape
    tm, tn, tk = _tile(M, tm, SUB), _tile(n, tn), _tile(K, tk)
    nj, nk = n // tn, K // tk

    def compute(*refs):
        if residual is None:
            a_ref, w_ref, o_ref = refs[:3]
            r_ref = None
        else:
            a_ref, w_ref, r_ref, o_ref = refs[:4]

        def finish(r):
            if r_ref is not None:
                r = r + r_ref[...]
            o_ref[...] = r.astype(o_ref.dtype)

        if nk == 1:
            finish(jnp.dot(a_ref[...], w_ref[...], preferred_element_type=F32))
            return
        acc = refs[-1]
        k = pl.program_id(3)

        @pl.when(k == 0)
        def _():
            acc[...] = jnp.zeros_like(acc)

        acc[...] += jnp.dot(a_ref[...], w_ref[...], preferred_element_type=F32)

        @pl.when(k == nk - 1)
        def _():
            finish(acc[...])

    in_specs = [pl.BlockSpec((tm, tk), lambda g, j, i, k: (i, k)),
                pl.BlockSpec((None, tk, tn), lambda g, j, i, k: (g, k, j))]
    args = [a, w3]
    if residual is not None:
        in_specs.append(pl.BlockSpec((tm, tn), lambda g, j, i, k: (i, g * nj + j)))
        args.append(residual)
    outs, passed = _call(
        name, (G, nj, M // tm, nk), compute, in_specs, args, [_S((M, G * n), out_dtype)],
        [pl.BlockSpec((tm, tn), lambda g, j, i, k: (i, g * nj + j))],
        [] if nk == 1 else [pltpu.VMEM((tm, tn), F32)], tasks)
    return (outs[0], passed) if tasks else outs[0]


def _mm_nt(dy, w3, *, out_dtype, name, tm=1024, tko=1024, tn=TN, tasks=()):
    M, _ = dy.shape
    G, K, n = w3.shape
    tm, tko, tn = _tile(M, tm, SUB), _tile(K, tko), _tile(n, tn)
    nj = n // tn
    nr = G * nj

    def compute(dy_ref, w_ref, o_ref, acc):
        r = pl.program_id(2)

        @pl.when(r == 0)
        def _():
            acc[...] = jnp.zeros_like(acc)

        acc[...] += lax.dot_general(dy_ref[...], w_ref[...], (((1,), (1,)), ((), ())),
                                    preferred_element_type=F32)

        @pl.when(r == nr - 1)
        def _():
            o_ref[...] = acc[...].astype(o_ref.dtype)

    outs, passed = _call(
        name, (K // tko, M // tm, nr), compute,
        [pl.BlockSpec((tm, tn), lambda ko, i, r: (i, r)),
         pl.BlockSpec((None, tko, tn), lambda ko, i, r: (r // nj, ko, r % nj))],
        [dy, w3], [_S((M, K), out_dtype)], [pl.BlockSpec((tm, tko), lambda ko, i, r: (i, ko))],
        [pltpu.VMEM((tm, tko), F32)], tasks)
    return (outs[0], passed) if tasks else outs[0]


def _mm_tn(a, dy, G, *, out_dtype, name, tk=1024, tn=TN, tt=1024, tasks=()):
    M, K = a.shape
    n = dy.shape[1] // G
    tk, tn, tt = _tile(K, tk), _tile(n, tn), _tile(M, tt, SUB)
    nj, nt = n // tn, M // tt

    def compute(a_ref, dy_ref, o_ref, acc):
        t = pl.program_id(3)

        @pl.when(t == 0)
        def _():
            acc[...] = jnp.zeros_like(acc)

        acc[...] += lax.dot_general(a_ref[...], dy_ref[...], (((0,), (0,)), ((), ())),
                                    preferred_element_type=F32)

        @pl.when(t == nt - 1)
        def _():
            o_ref[...] = acc[...].astype(o_ref.dtype)

    outs, passed = _call(
        name, (G, nj, K // tk, nt), compute,
        [pl.BlockSpec((tt, tk), lambda g, j, k, t: (t, k)),
         pl.BlockSpec((tt, tn), lambda g, j, k, t: (t, g * nj + j))],
        [a, dy], [_S((G, K, n), out_dtype)], [pl.BlockSpec((None, tk, tn), lambda g, j, k, t: (g, k, j))],
        [pltpu.VMEM((tk, tn), F32)], tasks)
    return (outs[0], passed) if tasks else outs[0]


def _mm_small(kind, a, b, w3, *, name, tm=1024, tasks=()):
    G, K, n = w3.shape
    M = (a if a is not None else b).shape[0]
    tm = _tile(M, tm, HALO)
    nt = M // tm
    w_spec = pl.BlockSpec((G, K, n), lambda i: (0, 0, 0))
    a_spec = pl.BlockSpec((tm, K), lambda i: (i, 0))
    b_spec = pl.BlockSpec((tm, G * n), lambda i: (i, 0))
    cols = lambda g: slice(g * n, (g + 1) * n)
    if kind == "nn":
        def compute(a_ref, w_ref, o_ref):
            av = a_ref[...]
            for g in range(G):
                o_ref[:, cols(g)] = jnp.dot(av, w_ref[g], preferred_element_type=F32).astype(o_ref.dtype)

        outs, passed = _call(name, (nt,), compute, [a_spec, w_spec], [a, w3], [_S((M, G * n), BF16)], [b_spec], [], tasks)
    elif kind == "nt":
        def compute(b_ref, w_ref, o_ref):
            acc = None
            for g in range(G):
                part = lax.dot_general(b_ref[:, cols(g)], w_ref[g], (((1,), (1,)), ((), ())),
                                       preferred_element_type=F32)
                acc = part if acc is None else acc + part
            o_ref[...] = acc.astype(o_ref.dtype)

        outs, passed = _call(name, (nt,), compute, [b_spec, w_spec], [b, w3], [_S((M, K), BF16)], [a_spec], [], tasks)
    else:
        def compute(a_ref, b_ref, o_ref, acc):
            i = pl.program_id(0)

            @pl.when(i == 0)
            def _():
                acc[...] = jnp.zeros_like(acc)

            at = a_ref[...].T
            for g in range(G):
                acc[g] += jnp.dot(at, b_ref[:, cols(g)], preferred_element_type=F32)

            @pl.when(i == nt - 1)
            def _():
                o_ref[...] = acc[...].astype(o_ref.dtype)

        outs, passed = _call(name, (nt,), compute, [a_spec, b_spec], [a, b], [_S((G, K, n), BF16)], [w_spec],
                             [pltpu.VMEM((G, K, n), F32)], tasks)
    return (outs[0], passed) if tasks else outs[0]


def _cast_into_slot(w, place, name, paired=False):
    R, C = w.shape
    tr = _tile(R, 512, HALO)

    def body(s_ref, w_ref, o_ref):
        del s_ref
        o_ref[...] = w_ref[...].astype(BF16)

    if paired:
        shape, out_map = (N_CHIP, R, 2 * C), lambda i, s: (s[1], i, s[0])
    else:
        shape, out_map = (N_DEV, R, C), lambda i, s: (s[2], i, 0)
    return pl.pallas_call(
        body, name=name, out_shape=_S(shape, BF16),
        grid_spec=pltpu.PrefetchScalarGridSpec(
            num_scalar_prefetch=1, grid=(R // tr,),
            in_specs=[pl.BlockSpec((tr, C), lambda i, s: (i, 0))],
            out_specs=pl.BlockSpec((None, tr, C), out_map)),
        compiler_params=_cparams(("parallel",)),
    )(place, w)


def _down(cur, prev8, j):
    return pltpu.roll(jnp.concatenate([prev8, cur], axis=0), j, 0)[SUB:, :]


def _up(cur, next8, j):
    n = cur.shape[0] + SUB
    return pltpu.roll(jnp.concatenate([cur, next8], axis=0), n - j, 0)[:cur.shape[0], :]


def _shifted_down(x, prev8, n):
    full = jnp.concatenate([prev8, x], axis=0)
    return [x] + [pltpu.roll(full, s, 0)[SUB:, :] for s in range(1, n)]


def _shifted_up(x, next8, n):
    m = x.shape[0] + SUB
    full = jnp.concatenate([x, next8], axis=0)
    return [x] + [pltpu.roll(full, m - s, 0)[:x.shape[0], :] for s in range(1, n)]


def _taps(sh, w_ref):
    kw = w_ref.shape[0]
    y = sh[0] * w_ref[pl.ds(kw - 1, 1), :]
    for k in range(kw - 1):
        y = y + sh[kw - 1 - k] * w_ref[pl.ds(k, 1), :]
    return y


def _conv(x, prev8, w_ref):
    return _taps(_shifted_down(x, prev8, w_ref.shape[0]), w_ref)


def _conv_t(dy, next8, w_ref):
    return _taps(_shifted_up(dy, next8, w_ref.shape[0]), w_ref)


def _conv_dw(dw_ref, dy, x, prev8, first):
    kw = dw_ref.shape[0]

    @pl.when(first)
    def _():
        dw_ref[...] = jnp.zeros_like(dw_ref)

    for k in range(kw):
        xs = x if k == kw - 1 else _down(x, prev8, kw - 1 - k)
        dw_ref[pl.ds(k, 1), :] += jnp.sum(dy * xs, axis=0, keepdims=True)


def _acc(ref, val, first):
    @pl.when(first)
    def _():
        ref[...] = jnp.zeros_like(ref)

    ref[...] += val


def _acc_row(ref, val, first):
    _acc(ref, jnp.sum(val, axis=0, keepdims=True), first)


def _prev8(h_ref, t):
    return jnp.where(t > 0, h_ref[...].astype(F32)[HALO - SUB:, :], 0.0)


def _next8(h_ref, is_last):
    return jnp.where(is_last, 0.0, h_ref[...].astype(F32)[:SUB, :])


_GELU_K0 = math.sqrt(2.0 / math.pi)
_GELU_K1 = 0.044715


def _gelu_and_grad(x):
    x2 = x * x
    th = jnp.tanh(_GELU_K0 * x * (1.0 + _GELU_K1 * x2))
    g = 0.5 * x * (1.0 + th)
    dg = 0.5 * (1.0 + th) + 0.5 * x * (1.0 - th * th) * (_GELU_K0 * (1.0 + 3.0 * _GELU_K1 * x2))
    return g, dg


def _neg_expm1(z):
    series = -z * (1.0 + z * (0.5 + z * (1.0 / 6.0 + z * (1.0 / 24.0))))
    return jnp.where(z > -0.03, series, 1.0 - jnp.exp(z))


def _store_staged(stages, dst_hbm, sems, step, n_steps, where):
    def copies(s, slot):
        return [pltpu.make_async_copy(
            st.at[slot], dst_hbm.at[pl.ds(r0, st.shape[1]), pl.ds(c0, st.shape[2])], sems.at[slot, k])
            for k, (st, (r0, c0)) in enumerate(zip(stages, where(s)))]

    slot = step % 2

    @pl.when(step > 0)
    def _():
        for cp in copies(step - 1, 1 - slot):
            cp.wait()

    for cp in copies(step, slot):
        cp.start()

    @pl.when(step == n_steps - 1)
    def _():
        for cp in copies(step, slot):
            cp.wait()


def _halo_prev_map(hb, col_fn):
    return lambda c, t: (jnp.maximum(t * hb - 1, 0), col_fn(c))


def _rms_fwd(x, g, name):
    T, D = x.shape
    tb = _tile(T, TB, SUB)

    def body(x_ref, g_ref, o_ref):
        xv = x_ref[...]
        rstd = lax.rsqrt(jnp.mean(xv * xv, axis=-1, keepdims=True) + EPS)
        o_ref[...] = (xv * rstd * g_ref[...]).astype(BF16)

    return pl.pallas_call(
        body, name=name, out_shape=_S((T, D), BF16), grid=(T // tb,),
        in_specs=[pl.BlockSpec((tb, D), lambda i: (i, 0)), pl.BlockSpec((1, D), lambda i: (0, 0))],
        out_specs=pl.BlockSpec((tb, D), lambda i: (i, 0)),
        compiler_params=_cparams(("parallel",)),
    )(x, g.reshape(1, D))


def _rms_bwd(x, g, dh, dres, name):
    T, D = x.shape
    tb = _tile(T, 256, SUB)

    def body(x_ref, g_ref, dh_ref, dr_ref, dx_ref, dxb_ref, dg_ref):
        i = pl.program_id(0)
        xv = x_ref[...]
        rstd = lax.rsqrt(jnp.mean(xv * xv, axis=-1, keepdims=True) + EPS)
        xn = xv * rstd
        dhv = dh_ref[...].astype(F32)
        _acc_row(dg_ref, dhv * xn, i == 0)
        dxn = dhv * g_ref[...]
        dx = dr_ref[...] + rstd * (dxn - xn * jnp.mean(dxn * xn, axis=-1, keepdims=True))
        dx_ref[...] = dx
        dxb_ref[...] = dx.astype(BF16)

    blk = pl.BlockSpec((tb, D), lambda i: (i, 0))
    vec = pl.BlockSpec((1, D), lambda i: (0, 0))
    return pl.pallas_call(
        body, name=name, out_shape=(_S((T, D), F32), _S((T, D), BF16), _S((1, D), F32)),
        grid=(T // tb,), in_specs=[blk, vec, blk, blk], out_specs=(blk, blk, vec),
        compiler_params=_cparams(("arbitrary",)),
    )(x, g.reshape(1, D), dh, dres)


def _loss_head(x2, g, target, name):
    T, D = x2.shape
    tb = _tile(T, 256, SUB)

    def body(x_ref, g_ref, t_ref, dx_ref, dxb_ref, loss_ref, dg_ref):
        i = pl.program_id(0)
        xv = x_ref[...]
        rstd = lax.rsqrt(jnp.mean(xv * xv, axis=-1, keepdims=True) + EPS)
        xn = xv * rstd
        err = xn * g_ref[...] - t_ref[...]
        part = 0.5 * jnp.sum(jnp.mean(err * err, axis=-1, keepdims=True), axis=0, keepdims=True)
        part = jnp.broadcast_to(part, (1, LANES))
        _acc(loss_ref, part, i == 0)
        dy = err * (1.0 / D)
        _acc_row(dg_ref, dy * xn, i == 0)
        dxn = dy * g_ref[...]
        dx = rstd * (dxn - xn * jnp.mean(dxn * xn, axis=-1, keepdims=True))
        dx_ref[...] = dx
        dxb_ref[...] = dx.astype(BF16)

    blk = pl.BlockSpec((tb, D), lambda i: (i, 0))
    vec = pl.BlockSpec((1, D), lambda i: (0, 0))
    return pl.pallas_call(
        body, name=name,
        out_shape=(_S((T, D), F32), _S((T, D), BF16), _S((1, LANES), F32), _S((1, D), F32)),
        grid=(T // tb,), in_specs=[blk, vec, blk],
        out_specs=(blk, blk, pl.BlockSpec((1, LANES), lambda i: (0, 0)), vec),
        compiler_params=_cparams(("arbitrary",)),
    )(x2, g.reshape(1, D), target)


def _lru_gates(xc, wa_ref, ba_ref, wx_ref, bx_ref, lam_ref):
    xcb = xc.astype(BF16)
    r = jax.nn.sigmoid(jnp.dot(xcb, wa_ref[...], preferred_element_type=F32) + ba_ref[...])
    i = jax.nn.sigmoid(jnp.dot(xcb, wx_ref[...], preferred_element_type=F32) + bx_ref[...])
    sp = jax.nn.softplus(-lam_ref[...])
    log_a = (-LRU_C * sp) * r
    a = jnp.exp(log_a)
    s = jnp.sqrt(_neg_expm1(2.0 * log_a))
    return xcb, r, i, a, s


def _lru_fwd(p, conv_w, conv_b, wa_bd, ba, wx_bd, bx, lam, *, name, tasks=()):
    T = p.shape[0]
    d = lam.shape[-1]
    C = _tile(d, C_LRU)
    nC = d // C
    tb = _tile(T, TB, HALO)
    nT, hb, nt = T // tb, tb // HALO, tb // SUB

    def body(x_ref, xh_ref, g_ref, cw_ref, cb_ref, wa_ref, ba_ref, wx_ref, bx_ref, lam_ref,
             hs_ref, y_ref, a_s, u_s, h_s):
        t = pl.program_id(1)

        @pl.when(t == 0)
        def _():
            h_s[...] = jnp.zeros_like(h_s)

        x = x_ref[...].astype(F32)
        xc = _conv(x, _prev8(xh_ref, t), cw_ref) + cb_ref[...]
        _, r, i, a, s = _lru_gates(xc, wa_ref, ba_ref, wx_ref, bx_ref, lam_ref)
        a_s[...] = a
        u_s[...] = s * (i * xc)
        row = lax.broadcasted_iota(jnp.int32, (SUB, C), 0)

        def step(k, h):
            o = pl.multiple_of(k * SUB, SUB)
            A = a_s[pl.ds(o, SUB), :]
            B = u_s[pl.ds(o, SUB), :]
            for sh in (1, 2, 4):
                m = row >= sh
                Ap = pltpu.roll(A, sh, 0)
                Bp = pltpu.roll(B, sh, 0)
                B = jnp.where(m, A * Bp + B, B)
                A = jnp.where(m, A * Ap, A)
            hs = A * h + B
            hs_ref[pl.ds(o, SUB), :] = hs
            return jnp.broadcast_to(hs[SUB - 1:SUB, :], (SUB, C))

        h_s[...] = lax.fori_loop(0, nt, step, h_s[...])
        gel, _ = _gelu_and_grad(g_ref[...].astype(F32))
        y_ref[...] = (gel * hs_ref[...]).astype(BF16)

    vec = pl.BlockSpec((1, C), lambda c, t: (0, c))
    sq = pl.BlockSpec((None, C, C), lambda c, t: (c, 0, 0))
    outs, passed = _call(
        name, (nC, nT), body,
        [pl.BlockSpec((tb, C), lambda c, t: (t, c)),
         pl.BlockSpec((HALO, C), _halo_prev_map(hb, lambda c: c)),
         pl.BlockSpec((tb, C), lambda c, t: (t, nC + c)),
         pl.BlockSpec((conv_w.shape[0], C), lambda c, t: (0, c)),
         vec, sq, vec, sq, vec, vec],
        [p, p, p, conv_w, conv_b, wa_bd, ba, wx_bd, bx, lam],
        [_S((T, d), F32), _S((T, d), BF16)],
        [pl.BlockSpec((tb, C), lambda c, t: (t, c)), pl.BlockSpec((tb, C), lambda c, t: (t, c))],
        [pltpu.VMEM((tb, C), F32), pltpu.VMEM((tb, C), F32), pltpu.VMEM((SUB, C), F32)], tasks)
    return (*outs, passed) if tasks else outs


def _lru_bwd(p, hs, dyl, dp, conv_w, conv_b, wa_bd, ba, wx_bd, bx, lam, *, name, tasks=()):
    T = p.shape[0]
    d = lam.shape[-1]
    C = _tile(d, C_LRU)
    nC = d // C
    tb = _tile(T, TB, HALO)
    nT, hb, nt = T // tb, tb // HALO, tb // SUB
    kw = conv_w.shape[0]

    def body(x_ref, xh_ref, g_ref, hs_ref, hh_ref, dy_ref, cw_ref, cb_ref, wa_ref, ba_ref, wx_ref, bx_ref,
             lam_ref, dp_in, dp_ref, dcw_ref, dcb_ref, dwa_ref, dba_ref, dwx_ref, dbx_ref, dlam_ref,
             b_s, g_s, dh_s, an_s, dhn_s, dxn_s, st_x, st_g, sems):
        del dp_in
        c = pl.program_id(0)
        tr = pl.program_id(1)
        t = nT - 1 - tr
        first = tr == 0

        @pl.when(first)
        def _():
            an_s[...] = jnp.zeros_like(an_s)
            dhn_s[...] = jnp.zeros_like(dhn_s)
            dxn_s[...] = jnp.zeros_like(dxn_s)

        x = x_ref[...].astype(F32)
        xprev = _prev8(xh_ref, t)
        xc = _conv(x, xprev, cw_ref) + cb_ref[...]
        xcb, r, i, a, s = _lru_gates(xc, wa_ref, ba_ref, wx_ref, bx_ref, lam_ref)
        hsv = hs_ref[...]
        dy = dy_ref[...].astype(F32)
        gel, dgel = _gelu_and_grad(g_ref[...].astype(F32))
        step_no = c * nT + tr
        slot = step_no % 2
        st_g[slot] = (dy * hsv * dgel).astype(BF16)

        b_s[...] = _up(a, an_s[...], 1)
        g_s[...] = dy * gel
        row = lax.broadcasted_iota(jnp.int32, (SUB, C), 0)

        def step(k, carry):
            o = pl.multiple_of((nt - 1 - k) * SUB, SUB)
            B = b_s[pl.ds(o, SUB), :]
            G = g_s[pl.ds(o, SUB), :]
            for sh in (1, 2, 4):
                m = row < SUB - sh
                Bn = pltpu.roll(B, SUB - sh, 0)
                Gn = pltpu.roll(G, SUB - sh, 0)
                G = jnp.where(m, B * Gn + G, G)
                B = jnp.where(m, B * Bn, B)
            dh = B * carry + G
            dh_s[pl.ds(o, SUB), :] = dh
            return jnp.broadcast_to(dh[0:1, :], (SUB, C))

        dhn_s[...] = lax.fori_loop(0, nt, step, dhn_s[...])
        an_s[...] = a[:SUB, :]
        dh = dh_s[...]

        hprev = _down(hsv, jnp.where(t > 0, hh_ref[...][HALO - SUB:, :], 0.0), 1)
        d_a = dh * hprev
        ixc = i * xc
        d_s = dh * ixc
        d_i = dh * s * xc
        d_xc = dh * s * i
        d_l = d_a * a - d_s * (a * a) / s
        sp = jax.nn.softplus(-lam_ref[...])
        _acc_row(dlam_ref, d_l * r * (LRU_C * jax.nn.sigmoid(-lam_ref[...])), first)
        d_zr = (d_l * (-LRU_C * sp)) * r * (1.0 - r)
        d_zi = d_i * i * (1.0 - i)
        _acc_row(dba_ref, d_zr, first)
        _acc_row(dbx_ref, d_zi, first)
        d_zrb = d_zr.astype(BF16)
        d_zib = d_zi.astype(BF16)
        tn_dims = (((0,), (0,)), ((), ()))
        nt_dims = (((1,), (1,)), ((), ()))
        gwa = lax.dot_general(xcb, d_zrb, tn_dims, preferred_element_type=F32)
        gwx = lax.dot_general(xcb, d_zib, tn_dims, preferred_element_type=F32)
        _acc(dwa_ref, gwa, first)
        _acc(dwx_ref, gwx, first)
        d_xc = (d_xc + lax.dot_general(d_zrb, wa_ref[...], nt_dims, preferred_element_type=F32)
                + lax.dot_general(d_zib, wx_ref[...], nt_dims, preferred_element_type=F32))
        _acc_row(dcb_ref, d_xc, first)
        _conv_dw(dcw_ref, d_xc, x, xprev, first)
        st_x[slot] = _conv_t(d_xc, dxn_s[...], cw_ref).astype(BF16)
        dxn_s[...] = d_xc[:SUB, :]

        def where(s):
            row0, col0 = (nT - 1 - s % nT) * tb, (s // nT) * C
            return [(row0, col0), (row0, d + col0)]

        _store_staged([st_x, st_g], dp_ref, sems, step_no, nC * nT, where)

    rev = lambda c, tr: (nT - 1 - tr, c)
    vec = pl.BlockSpec((1, C), lambda c, tr: (0, c))
    sq = pl.BlockSpec((None, C, C), lambda c, tr: (c, 0, 0))
    cwb = pl.BlockSpec((kw, C), lambda c, tr: (0, c))
    halo_prev = lambda c, tr: (jnp.maximum((nT - 1 - tr) * hb - 1, 0), c)
    outs, passed = _call(
        name, (nC, nT), body,
        [pl.BlockSpec((tb, C), rev),
         pl.BlockSpec((HALO, C), halo_prev),
         pl.BlockSpec((tb, C), lambda c, tr: (nT - 1 - tr, nC + c)),
         pl.BlockSpec((tb, C), rev),
         pl.BlockSpec((HALO, C), halo_prev),
         pl.BlockSpec((tb, C), rev),
         cwb, vec, sq, vec, sq, vec, vec, ANY],
        [p, p, p, hs, hs, dyl, conv_w, conv_b, wa_bd, ba, wx_bd, bx, lam, dp],
        [_S(dp.shape, dp.dtype), _S((kw, d), F32), _S((1, d), F32), _S((nC, C, C), F32), _S((1, d), F32),
         _S((nC, C, C), F32), _S((1, d), F32), _S((1, d), F32)],
        [ANY, cwb, vec, sq, vec, sq, vec, vec],
        [pltpu.VMEM((tb, C), F32), pltpu.VMEM((tb, C), F32), pltpu.VMEM((tb, C), F32),
         pltpu.VMEM((SUB, C), F32), pltpu.VMEM((SUB, C), F32), pltpu.VMEM((SUB, C), F32),
         pltpu.VMEM((2, tb, C), BF16), pltpu.VMEM((2, tb, C), BF16), pltpu.SemaphoreType.DMA((2, 2))],
        tasks, own_aliases={13: 0})
    return (*outs, passed) if tasks else outs


def _sc_fwd(p, conv_w, *, d, name):
    T = p.shape[0]
    C = _tile(d, C_EW)
    nC = d // C
    tb = _tile(T, TB, HALO)
    nT, hb = T // tb, tb // HALO

    def body(b_ref, c_ref, ch_ref, v_ref, vh_ref, w_ref, y_ref):
        t = pl.program_id(1)
        cv = c_ref[...].astype(F32) * v_ref[...].astype(F32)
        cvp = _prev8(ch_ref, t) * _prev8(vh_ref, t)
        y_ref[...] = (b_ref[...].astype(F32) * _conv(cv, cvp, w_ref)).astype(BF16)

    seg = lambda k: pl.BlockSpec((tb, C), lambda c, t: (t, k * nC + c))
    hseg = lambda k: pl.BlockSpec((HALO, C), _halo_prev_map(hb, lambda c: k * nC + c))
    return pl.pallas_call(
        body, name=name, out_shape=_S((T, d), BF16), grid=(nC, nT),
        in_specs=[seg(2), seg(3), hseg(3), seg(4), hseg(4), pl.BlockSpec((conv_w.shape[0], C), lambda c, t: (0, c))],
        out_specs=pl.BlockSpec((tb, C), lambda c, t: (t, c)),
        compiler_params=_cparams(("parallel", "parallel")),
    )(p, p, p, p, p, conv_w)


def _sc_bwd(p, dys, dp, conv_w, *, d, name):
    T = p.shape[0]
    C = _tile(d, C_EW)
    nC = d // C
    tb = _tile(T, TB, HALO)
    nT, hb = T // tb, tb // HALO
    kw = conv_w.shape[0]

    def body(b_ref, bn_ref, c_ref, ch_ref, v_ref, vh_ref, dy_ref, dyn_ref, w_ref, dp_in, dp_ref, dw_ref,
             st_b, st_c, st_v, sems):
        del dp_in
        c = pl.program_id(0)
        t = pl.program_id(1)
        last = t == nT - 1
        bv = b_ref[...].astype(F32)
        cvv = c_ref[...].astype(F32)
        vv = v_ref[...].astype(F32)
        dy = dy_ref[...].astype(F32)
        cv = cvv * vv
        cvp = _prev8(ch_ref, t) * _prev8(vh_ref, t)
        step_no = c * nT + t
        slot = step_no % 2
        st_b[slot] = (dy * _conv(cv, cvp, w_ref)).astype(BF16)
        dz = dy * bv
        dzn = _next8(dyn_ref, last) * _next8(bn_ref, last)
        _conv_dw(dw_ref, dz, cv, cvp, t == 0)
        dcv = _conv_t(dz, dzn, w_ref)
        st_c[slot] = (dcv * vv).astype(BF16)
        st_v[slot] = (dcv * cvv).astype(BF16)

        def where(s):
            return [((s % nT) * tb, (2 + k) * d + (s // nT) * C) for k in range(3)]

        _store_staged([st_b, st_c, st_v], dp_ref, sems, step_no, nC * nT, where)

    seg = lambda k: pl.BlockSpec((tb, C), lambda c, t: (t, k * nC + c))
    hseg = lambda k: pl.BlockSpec((HALO, C), _halo_prev_map(hb, lambda c: k * nC + c))
    last_h = T // HALO - 1
    nseg = lambda k: pl.BlockSpec((HALO, C), lambda c, t: (jnp.minimum((t + 1) * hb, last_h), k * nC + c))
    return pl.pallas_call(
        body, name=name, out_shape=(_S(dp.shape, dp.dtype), _S((kw, d), F32)), grid=(nC, nT),
        in_specs=[seg(2), nseg(2), seg(3), hseg(3), seg(4), hseg(4),
                  pl.BlockSpec((tb, C), lambda c, t: (t, c)), nseg(0),
                  pl.BlockSpec((kw, C), lambda c, t: (0, c)), ANY],
        out_specs=(ANY, pl.BlockSpec((kw, C), lambda c, t: (0, c))),
        scratch_shapes=[pltpu.VMEM((2, tb, C), BF16)] * 3 + [pltpu.SemaphoreType.DMA((2, 3))],
        input_output_aliases={9: 0},
        compiler_params=_cparams(("arbitrary", "arbitrary")),
    )(p, p, p, p, p, p, dys, dys, conv_w, dp)


def _merge_fwd(p, y_lru, y_sc, *, col0, name):
    T, D = y_lru.shape
    C = _tile(math.gcd(D, col0), 1024)
    nC = D // C
    k0 = col0 // C
    tb = _tile(T, 256, HALO)

    def body(gl_ref, gs_ref, yl_ref, ys_ref, o_ref):
        @pl.loop(0, tb // HALO)
        def _(k):
            rows = pl.ds(pl.multiple_of(k * HALO, HALO), HALO)
            for l0 in range(0, C, min(C, C_EW)):
                at = (rows, pl.ds(l0, min(C, C_EW)))
                o_ref[at] = (jax.nn.sigmoid(gl_ref[at].astype(F32)) * yl_ref[at].astype(F32)
                             + jax.nn.sigmoid(gs_ref[at].astype(F32)) * ys_ref[at].astype(F32)).astype(BF16)

    blk = pl.BlockSpec((tb, C), lambda c, t: (t, c))
    return pl.pallas_call(
        body, name=name, out_shape=_S((T, D), BF16), grid=(nC, T // tb),
        in_specs=[pl.BlockSpec((tb, C), lambda c, t: (t, k0 + c)),
                  pl.BlockSpec((tb, C), lambda c, t: (t, k0 + nC + c)), blk, blk],
        out_specs=blk, compiler_params=_cparams(("parallel", "parallel")),
    )(p, p, y_lru, y_sc)


def _merge_bwd(p, y_lru, y_sc, dm, *, col0, name):
    T, D = y_lru.shape
    C = _tile(math.gcd(D, col0), 1024)
    nC = D // C
    k0 = col0 // C
    tb = _tile(T, 256, HALO)
    nT = T // tb

    def body(gl_ref, gs_ref, yl_ref, ys_ref, dm_ref, dp_ref, dyl_ref, dys_ref, st_l, st_s, sems):
        step_no = pl.program_id(0) * nT + pl.program_id(1)
        slot = step_no % 2

        @pl.loop(0, tb // HALO)
        def _(k):
            rows = pl.ds(pl.multiple_of(k * HALO, HALO), HALO)
            for l0 in range(0, C, min(C, C_EW)):
                at = (rows, pl.ds(l0, min(C, C_EW)))
                dmv = dm_ref[at].astype(F32)
                sl = jax.nn.sigmoid(gl_ref[at].astype(F32))
                ss = jax.nn.sigmoid(gs_ref[at].astype(F32))
                dyl_ref[at] = (dmv * sl).astype(BF16)
                dys_ref[at] = (dmv * ss).astype(BF16)
                st_l[(slot,) + at] = (dmv * yl_ref[at].astype(F32) * sl * (1.0 - sl)).astype(BF16)
                st_s[(slot,) + at] = (dmv * ys_ref[at].astype(F32) * ss * (1.0 - ss)).astype(BF16)

        def where(s):
            row0, colc = (s % nT) * tb, (s // nT) * C
            return [(row0, col0 + colc), (row0, col0 + D + colc)]

        _store_staged([st_l, st_s], dp_ref, sems, step_no, nC * nT, where)

    blk = pl.BlockSpec((tb, C), lambda c, t: (t, c))
    return pl.pallas_call(
        body, name=name, out_shape=(_S(p.shape, BF16), _S((T, D), BF16), _S((T, D), BF16)),
        grid=(nC, nT),
        in_specs=[pl.BlockSpec((tb, C), lambda c, t: (t, k0 + c)),
                  pl.BlockSpec((tb, C), lambda c, t: (t, k0 + nC + c)), blk, blk, blk],
        out_specs=(ANY, blk, blk),
        scratch_shapes=[pltpu.VMEM((2, tb, C), BF16), pltpu.VMEM((2, tb, C), BF16), pltpu.SemaphoreType.DMA((2, 2))],
        compiler_params=_cparams(("arbitrary", "arbitrary")),
    )(p, p, y_lru, y_sc, dm)


def _ffn_act_fwd(uu, conv_w, *, name, tasks=()):
    T = uu.shape[0]
    F = uu.shape[1] // 2
    C = _tile(F, C_EW)
    nC = F // C
    tb = _tile(T, TB, HALO)
    nT, hb = T // tb, tb // HALO
    kw = conv_w.shape[0]
    R = HALO

    def body(g_ref, gh_ref, v_ref, vh_ref, wg_ref, wv_ref, o_ref):
        t = pl.program_id(1)

        def chunk(k, carry):
            gp, vp = carry
            r0 = pl.multiple_of(k * R, R)
            ug = g_ref[pl.ds(r0, R), :].astype(F32)
            uv = v_ref[pl.ds(r0, R), :].astype(F32)
            cg = _conv(ug, gp, wg_ref)
            cv = _conv(uv, vp, wv_ref)
            o_ref[pl.ds(r0, R), :] = (cg * jax.nn.sigmoid(cg) * cv).astype(BF16)
            return ug[R - SUB:, :], uv[R - SUB:, :]

        lax.fori_loop(0, tb // R, chunk, (_prev8(gh_ref, t), _prev8(vh_ref, t)))

    seg = lambda k: pl.BlockSpec((tb, C), lambda c, t: (t, k * nC + c))
    hseg = lambda k: pl.BlockSpec((HALO, C), _halo_prev_map(hb, lambda c: k * nC + c))
    wseg = lambda k: pl.BlockSpec((kw, C), lambda c, t: (0, k * nC + c))
    outs, passed = _call(
        name, (nC, nT), body, [seg(0), hseg(0), seg(1), hseg(1), wseg(0), wseg(1)],
        [uu, uu, uu, uu, conv_w, conv_w], [_S((T, F), BF16)], [pl.BlockSpec((tb, C), lambda c, t: (t, c))], [], tasks)
    return (outs[0], passed) if tasks else outs[0]


def _ffn_act_bwd(uu, dact, conv_w, *, name):
    T = uu.shape[0]
    F = uu.shape[1] // 2
    C = _tile(F, C_EW)
    nC = F // C
    tb = _tile(T, TB, HALO)
    nT, hb = T // tb, tb // HALO
    kw = conv_w.shape[0]
    R = HALO
    nk = tb // R

    def body(g_ref, gh_ref, v_ref, vh_ref, da_ref, wg_ref, wv_ref, du_ref, dwg_ref, dwv_ref,
             gn_s, vn_s, accg_s, accv_s, st_g, st_v, sems):
        c = pl.program_id(0)
        tr = pl.program_id(1)
        t = nT - 1 - tr
        first = tr == 0

        @pl.when(first)
        def _():
            gn_s[...] = jnp.zeros_like(gn_s)
            vn_s[...] = jnp.zeros_like(vn_s)
            dwg_ref[...] = jnp.zeros_like(dwg_ref)
            dwv_ref[...] = jnp.zeros_like(dwv_ref)

        accg_s[...] = jnp.zeros_like(accg_s)
        accv_s[...] = jnp.zeros_like(accv_s)
        step_no = c * nT + tr
        slot = step_no % 2

        def chunk(i, carry):
            gn, vn = carry
            k = nk - 1 - i
            r0 = pl.multiple_of(k * R, R)
            rp = pl.multiple_of(jnp.maximum(r0 - R, 0), R)
            ug = g_ref[pl.ds(r0, R), :].astype(F32)
            uv = v_ref[pl.ds(r0, R), :].astype(F32)
            gp = jnp.where(k > 0, g_ref[pl.ds(rp, R), :].astype(F32)[R - SUB:, :], _prev8(gh_ref, t))
            vp = jnp.where(k > 0, v_ref[pl.ds(rp, R), :].astype(F32)[R - SUB:, :], _prev8(vh_ref, t))
            sh_g = _shifted_down(ug, gp, kw)
            sh_v = _shifted_down(uv, vp, kw)
            cg = _taps(sh_g, wg_ref)
            cv = _taps(sh_v, wv_ref)
            da = da_ref[pl.ds(r0, R), :].astype(F32)
            sg = jax.nn.sigmoid(cg)
            d_cg = da * cv * (sg * (1.0 + cg * (1.0 - sg)))
            d_cv = da * (cg * sg)
            for j in range(kw):
                accg_s[j] += d_cg * sh_g[kw - 1 - j]
                accv_s[j] += d_cv * sh_v[kw - 1 - j]
            st_g[slot, pl.ds(r0, R), :] = _conv_t(d_cg, gn, wg_ref).astype(BF16)
            st_v[slot, pl.ds(r0, R), :] = _conv_t(d_cv, vn, wv_ref).astype(BF16)
            return d_cg[:SUB, :], d_cv[:SUB, :]

        gn, vn = lax.fori_loop(0, nk, chunk, (gn_s[...], vn_s[...]))
        gn_s[...] = gn
        vn_s[...] = vn
        for j in range(kw):
            dwg_ref[pl.ds(j, 1), :] += jnp.sum(accg_s[j], axis=0, keepdims=True)
            dwv_ref[pl.ds(j, 1), :] += jnp.sum(accv_s[j], axis=0, keepdims=True)
        def where(s):
            row0, col0 = (nT - 1 - s % nT) * tb, (s // nT) * C
            return [(row0, col0), (row0, F + col0)]

        _store_staged([st_g, st_v], du_ref, sems, step_no, nC * nT, where)

    seg = lambda k: pl.BlockSpec((tb, C), lambda c, tr: (nT - 1 - tr, k * nC + c))
    hseg = lambda k: pl.BlockSpec((HALO, C), lambda c, tr: (jnp.maximum((nT - 1 - tr) * hb - 1, 0), k * nC + c))
    wseg = lambda k: pl.BlockSpec((kw, C), lambda c, tr: (0, k * nC + c))
    dwb = pl.BlockSpec((kw, C), lambda c, tr: (0, c))
    return pl.pallas_call(
        body, name=name, out_shape=(_S(uu.shape, BF16), _S((kw, F), F32), _S((kw, F), F32)), grid=(nC, nT),
        in_specs=[seg(0), hseg(0), seg(1), hseg(1), pl.BlockSpec((tb, C), lambda c, tr: (nT - 1 - tr, c)),
                  wseg(0), wseg(1)],
        out_specs=(ANY, dwb, dwb),
        scratch_shapes=[pltpu.VMEM((SUB, C), F32), pltpu.VMEM((SUB, C), F32),
                        pltpu.VMEM((kw, R, C), F32), pltpu.VMEM((kw, R, C), F32),
                        pltpu.VMEM((2, tb, C), BF16), pltpu.VMEM((2, tb, C), BF16), pltpu.SemaphoreType.DMA((2, 2))],
        compiler_params=_cparams(("arbitrary", "arbitrary")),
    )(uu, uu, uu, uu, dact, conv_w, conv_w)


def _place():
    x, y, c = lax.axis_index("x"), lax.axis_index("y"), lax.axis_index("c")
    return x, y, c


def _chips(x, y):
    return [(1 - x, y), (x, 1 - y), (1 - x, 1 - y)]


def _all_gather(arrays, placed, over_ici, pair_n, name):
    n = len(arrays)

    def body(*refs):
        ins, outs = refs[:n], refs[n:2 * n]
        send_sems, recv_sems, local_sems = refs[2 * n:]
        x, y, c = _place()
        me, sibling = (x, y, c), (x, y, 1 - c)
        chips = _chips(x, y)
        full = [a for a in range(n) if over_ici[a]]

        def idx(px, py, pc):
            return 4 * px + 2 * py + pc

        def copy(a, k, block, to):
            dst = _dev_block(outs[a], idx(*block), pair_n[a])
            src = ins[a] if (block is me and not placed[a]) else dst
            return pltpu.make_async_remote_copy(
                src_ref=src, dst_ref=dst, send_sem=send_sems.at[a, k], recv_sem=recv_sems.at[a, k],
                device_id=to, device_id_type=MESH)

        mine = [pltpu.make_async_copy(ins[a], outs[a].at[idx(*me)], local_sems.at[a])
                for a in range(n) if not placed[a]]
        for cp in mine:
            cp.start()
        first = []
        for a in full:
            first += [copy(a, 1 + j, me, (*chip, c)) for j, chip in enumerate(chips)]
        for a in range(n):
            first.append(copy(a, 0, me, sibling))
        for cp in first:
            cp.start()
        passed = []
        for a in full:
            for j, chip in enumerate(chips):
                copy(a, 1 + j, (*chip, c), me).wait_recv()
                cp = copy(a, 4 + j, (*chip, c), sibling)
                cp.start()
                passed.append(cp)
        for a in range(n):
            copy(a, 0, sibling, me).wait_recv()
        for a in full:
            for j, chip in enumerate(chips):
                copy(a, 4 + j, (*chip, 1 - c), me).wait_recv()
        for cp in first + passed:
            cp.wait_send()
        for cp in mine:
            cp.wait()

    return pl.pallas_call(
        body, name=name,
        out_shape=tuple(_S(s.shape if placed[a] else (N_DEV,) + s.shape, s.dtype) for a, s in enumerate(arrays)),
        in_specs=[ANY] * n, out_specs=tuple([ANY] * n),
        scratch_shapes=[pltpu.SemaphoreType.DMA((n, 7)), pltpu.SemaphoreType.DMA((n, 7)),
                        pltpu.SemaphoreType.DMA((n,))],
        input_output_aliases={a: a for a in range(n) if placed[a]},
    )(*arrays)


def _dev_block(ref, dev, pair_n=None):
    if pair_n is None:
        return ref.at[dev]
    return ref.at[dev // 2, :, pl.ds(pl.multiple_of((dev % 2) * pair_n, LANES), pair_n)]


def _rows_of(ref, blk, rows, pair_n=None):
    v = _dev_block(ref, blk, pair_n)
    return v if rows is None else v.at[pl.ds(rows[0], rows[1])]


ALL_ROWS = "all"


def _gather_task(buf, ici=None, fwd=None, pair_n=None):
    rows = lambda r: None if r == ALL_ROWS else r
    blk_of = functools.partial(_rows_of, pair_n=pair_n)

    def copies(refs, ss, rs):
        x, y, c = _place()
        me = 4 * x + 2 * y + c
        cps = []
        for j, (px, py) in enumerate(_chips(x, y)):
            if ici is not None:
                blk = blk_of(refs[0], me, rows(ici))
                cps.append(pltpu.make_async_remote_copy(
                    src_ref=blk, dst_ref=blk, send_sem=ss.at[j], recv_sem=rs.at[j],
                    device_id=(px, py, c), device_id_type=MESH))
            if fwd is not None:
                blk = blk_of(refs[0], 4 * px + 2 * py + c, rows(fwd))
                cps.append(pltpu.make_async_remote_copy(
                    src_ref=blk, dst_ref=blk, send_sem=ss.at[3 + j], recv_sem=rs.at[3 + j],
                    device_id=(x, y, 1 - c), device_id_type=MESH))
        return cps

    def start(refs, ss, rs, ls):
        for cp in copies(refs, ss, rs):
            cp.start()

    def wait(refs, ss, rs, ls):
        x, y, c = _place()
        for j, (px, py) in enumerate(_chips(x, y)):
            if ici is not None:
                blk = blk_of(refs[0], 4 * px + 2 * py + c, rows(ici))
                pltpu.make_async_remote_copy(
                    src_ref=blk, dst_ref=blk, send_sem=ss.at[j], recv_sem=rs.at[j],
                    device_id=(px, py, c), device_id_type=MESH).wait_recv()
            if fwd is not None:
                blk = blk_of(refs[0], 4 * px + 2 * py + 1 - c, rows(fwd))
                pltpu.make_async_remote_copy(
                    src_ref=blk, dst_ref=blk, send_sem=ss.at[3 + j], recv_sem=rs.at[3 + j],
                    device_id=(x, y, 1 - c), device_id_type=MESH).wait_recv()
        for cp in copies(refs, ss, rs):
            cp.wait_send()

    return _Task([buf], [0], start, wait, nsem=6)


def _exchange_task(parts, landing, rows=None):
    def copies(refs, ss, rs):
        x, y, c = _place()
        myq = 2 * x + y
        return [pltpu.make_async_remote_copy(
            src_ref=_rows_of(refs[0], 2 * px + py, rows), dst_ref=_rows_of(refs[1], myq, rows),
            send_sem=ss.at[k], recv_sem=rs.at[k], device_id=(px, py, c), device_id_type=MESH)
            for k, (px, py) in enumerate(_chips(x, y))]

    def start(refs, ss, rs, ls):
        for cp in copies(refs, ss, rs):
            cp.start()

    def wait(refs, ss, rs, ls):
        x, y, c = _place()
        for k, (px, py) in enumerate(_chips(x, y)):
            pltpu.make_async_remote_copy(
                src_ref=_rows_of(refs[0], 2 * x + y, rows), dst_ref=_rows_of(refs[1], 2 * px + py, rows),
                send_sem=ss.at[k], recv_sem=rs.at[k], device_id=(px, py, c), device_id_type=MESH).wait_recv()
        for cp in copies(refs, ss, rs):
            cp.wait_send()

    return _Task([parts, landing], [1], start, wait)


def _core_blocks(g, pair_n):
    if pair_n is None:
        g4 = g.reshape((N_CHIP, 2) + g.shape[1:])
        return g4, (N_CHIP,) + g.shape[1:], lambda ref, c: ref.at[:, c]
    view = lambda ref, c: ref.at[:, :, pl.ds(pl.multiple_of(c * pair_n, LANES), pair_n)]
    return g, (N_CHIP, g.shape[1], pair_n), view


def _swap_task(g, pair_n=None):
    g4, shape, view = _core_blocks(g, pair_n)

    def copy(refs, ss, rs):
        x, y, c = _place()
        return pltpu.make_async_remote_copy(
            src_ref=view(refs[0], 1 - c), dst_ref=refs[1], send_sem=ss.at[0], recv_sem=rs.at[0],
            device_id=(x, y, 1 - c), device_id_type=MESH)

    def start(refs, ss, rs, ls):
        copy(refs, ss, rs).start()

    def wait(refs, ss, rs, ls):
        copy(refs, ss, rs).wait()

    return _Task([g4], [], start, wait, fresh=[_S(shape, g.dtype)], nsem=1)


def _peer(x, y, c, m):
    return x ^ (m >> 2), y ^ ((m >> 1) & 1), c ^ (m & 1)


def _bcast_task(pack):
    def copies(refs, ss, rs):
        x, y, c = _place()
        me = 4 * x + 2 * y + c
        return [pltpu.make_async_remote_copy(
            src_ref=refs[0], dst_ref=refs[1].at[me], send_sem=ss.at[m - 1], recv_sem=rs.at[m - 1],
            device_id=_peer(x, y, c, m), device_id_type=MESH) for m in range(1, N_DEV)]

    def local(refs, ls):
        x, y, c = _place()
        return pltpu.make_async_copy(refs[0], refs[1].at[4 * x + 2 * y + c], ls.at[0])

    def start(refs, ss, rs, ls):
        local(refs, ls).start()
        for cp in copies(refs, ss, rs):
            cp.start()

    def wait(refs, ss, rs, ls):
        x, y, c = _place()
        for m in range(1, N_DEV):
            px, py, pc = _peer(x, y, c, m)
            pltpu.make_async_remote_copy(
                src_ref=refs[0], dst_ref=refs[1].at[4 * px + 2 * py + pc], send_sem=ss.at[m - 1],
                recv_sem=rs.at[m - 1], device_id=(px, py, pc), device_id_type=MESH).wait_recv()
        for cp in copies(refs, ss, rs):
            cp.wait_send()
        local(refs, ls).wait()

    return _Task([pack], [], start, wait, fresh=[_S((N_DEV,) + pack.shape, pack.dtype)], nsem=N_DEV - 1)


def _sum_packs(packs, name):
    _, R, L = packs.shape

    def body(p_ref, o_ref):
        acc = p_ref[0]
        for k in range(1, N_DEV):
            acc = acc + p_ref[k]
        o_ref[...] = acc

    return pl.pallas_call(body, name=name, out_shape=_S((R, L), packs.dtype), in_specs=[VMEM_SPEC],
                          out_specs=VMEM_SPEC, compiler_params=_cparams())(packs)


def _swap_halves(g, name, pair_n=None):
    g4, shape, view = _core_blocks(g, pair_n)

    def body(g_ref, o_ref, send_sem, recv_sem):
        x, y, c = _place()
        cp = pltpu.make_async_remote_copy(
            src_ref=view(g_ref, 1 - c), dst_ref=o_ref, send_sem=send_sem, recv_sem=recv_sem,
            device_id=(x, y, 1 - c), device_id_type=MESH)
        cp.start()
        cp.wait()

    return pl.pallas_call(
        body, name=name, out_shape=_S(shape, g.dtype), in_specs=[ANY], out_specs=ANY,
        scratch_shapes=[pltpu.SemaphoreType.DMA, pltpu.SemaphoreType.DMA],
    )(g4)


def _add_halves(g, landed, place, name, pair_n=None):
    _, r, cc = landed.shape
    tr = _tile(r, 512, HALO)
    if pair_n is None:
        g4 = g.reshape(N_CHIP, 2, r, cc)
        g_spec = pl.BlockSpec((None, None, tr, cc), lambda i, q, s: (q, s[0], i, 0))
    else:
        g4 = g
        g_spec = pl.BlockSpec((None, tr, cc), lambda i, q, s: (q, i, s[0]))

    def body(s_ref, g_ref, l_ref, o_ref, land_ref):
        q = pl.program_id(1)
        v = (g_ref[...].astype(F32) + l_ref[...].astype(F32)).astype(BF16)
        o_ref[...] = v

        @pl.when(q == s_ref[1])
        def _():
            land_ref[...] = v

    return pl.pallas_call(
        body, name=name, out_shape=(_S((N_CHIP, r, cc), BF16), _S((N_CHIP, r, cc), BF16)),
        grid_spec=pltpu.PrefetchScalarGridSpec(
            num_scalar_prefetch=1, grid=(r // tr, N_CHIP),
            in_specs=[g_spec,
                      pl.BlockSpec((None, tr, cc), lambda i, q, s: (q, i, 0))],
            out_specs=(pl.BlockSpec((None, tr, cc), lambda i, q, s: (q, i, 0)),
                       pl.BlockSpec((None, tr, cc), lambda i, q, s: (s[1], i, 0)))),
        compiler_params=_cparams(("arbitrary", "arbitrary")),
    )(place, g4, landed)


def _all_reduce_small(pack, name):
    R = pack.shape[0]

    def body(p_ref, o_ref, buf, send_sems, recv_sems):
        x, y, c = _place()
        me = 4 * x + 2 * y + c
        buf[me] = p_ref[...]
        cps = []
        for k in range(N_DEV - 1):
            m = k + 1
            peer = (x ^ (m >> 2), y ^ ((m >> 1) & 1), c ^ (m & 1))
            cps.append(pltpu.make_async_remote_copy(
                src_ref=p_ref, dst_ref=buf.at[me], send_sem=send_sems.at[k], recv_sem=recv_sems.at[k],
                device_id=peer, device_id_type=MESH))
        for cp in cps:
            cp.start()
        for k in range(N_DEV - 1):
            m = k + 1
            peer_idx = 4 * (x ^ (m >> 2)) + 2 * (y ^ ((m >> 1) & 1)) + (c ^ (m & 1))
            pltpu.make_async_remote_copy(
                src_ref=p_ref, dst_ref=buf.at[peer_idx], send_sem=send_sems.at[k], recv_sem=recv_sems.at[k],
                device_id=(x, y, c), device_id_type=MESH).wait_recv()
        for cp in cps:
            cp.wait_send()
        acc = buf[0]
        for k in range(1, N_DEV):
            acc = acc + buf[k]
        o_ref[...] = acc

    return pl.pallas_call(
        body, name=name, out_shape=_S((R, LANES), F32),
        in_specs=[VMEM_SPEC], out_specs=VMEM_SPEC,
        scratch_shapes=[pltpu.VMEM((N_DEV, R, LANES), F32), pltpu.SemaphoreType.DMA((N_DEV - 1,)),
                        pltpu.SemaphoreType.DMA((N_DEV - 1,))],
        compiler_params=_cparams(),
    )(pack)


def _adamw_math(w, g, m, v):
    m = ADAM_B1 * m + (1.0 - ADAM_B1) * g
    v = ADAM_B2 * v + (1.0 - ADAM_B2) * (g * g)
    m_hat = m / (1.0 - ADAM_B1 ** ADAM_STEP)
    v_hat = v / (1.0 - ADAM_B2 ** ADAM_STEP)
    delta = -ADAM_LR * (m_hat / (jnp.sqrt(v_hat) + ADAM_EPS) + ADAM_WD * w)
    return delta, m, v


def _adamw_big(parts, w, m, v, name):
    r, cc = w.shape
    tr = _tile(r, 128, HALO)

    def body(p_ref, w_ref, m_ref, v_ref, g_ref, d_ref, nm_ref, nv_ref):
        g = p_ref[0].astype(F32)
        for q in range(1, N_CHIP):
            g = g + p_ref[q].astype(F32)
        g_ref[...] = g
        d_ref[...], nm_ref[...], nv_ref[...] = _adamw_math(w_ref[...], g, m_ref[...], v_ref[...])

    blk = pl.BlockSpec((tr, cc), lambda i: (i, 0))
    return pl.pallas_call(
        body, name=name, out_shape=tuple(_S((r, cc), F32) for _ in range(4)), grid=(r // tr,),
        in_specs=[pl.BlockSpec((N_CHIP, tr, cc), lambda i: (0, i, 0)), blk, blk, blk],
        out_specs=(blk, blk, blk, blk), compiler_params=_cparams(("parallel",)),
    )(parts, w, m, v)


def _adamw_small(ws, gs, ms, vs, name):
    n = len(ws)

    def body(*refs):
        w_r, g_r, m_r, v_r = refs[:n], refs[n:2 * n], refs[2 * n:3 * n], refs[3 * n:4 * n]
        d_r, nm_r, nv_r = refs[4 * n:5 * n], refs[5 * n:6 * n], refs[6 * n:7 * n]
        for k in range(n):
            d_r[k][...], nm_r[k][...], nv_r[k][...] = _adamw_math(w_r[k][...], g_r[k][...], m_r[k][...], v_r[k][...])

    shapes = tuple(_S(w.shape, F32) for w in ws)
    outs = pl.pallas_call(
        body, name=name, out_shape=shapes * 3,
        in_specs=[VMEM_SPEC] * (4 * n), out_specs=tuple([VMEM_SPEC] * (3 * n)),
        compiler_params=_cparams(),
    )(*ws, *gs, *ms, *vs)
    return outs[:n], outs[n:2 * n], outs[2 * n:]


def _block_diag(w, heads_per_block):
    H, hd, _ = w.shape
    nb = H // heads_per_block
    eye = jnp.eye(heads_per_block, dtype=w.dtype)
    w4 = w.reshape(nb, heads_per_block, hd, hd)
    return jnp.einsum("nhab,hg->nhagb", w4, eye).reshape(nb, heads_per_block * hd, heads_per_block * hd)


def _diag_blocks(bd, heads_per_block, hd):
    nb = bd.shape[0]
    b5 = bd.reshape(nb, heads_per_block, hd, heads_per_block, hd)
    return jnp.stack([b5[:, h, :, h, :] for h in range(heads_per_block)], axis=1).reshape(nb * heads_per_block, hd, hd)


def _as_rows(a):
    if a.ndim == 1:
        return a.reshape(-1, LANES) if a.shape[0] % LANES == 0 else a.reshape(1, -1)
    if a.ndim == 3:
        return a.reshape(-1, LANES) if (a.size % LANES == 0) else a.reshape(a.shape[0] * a.shape[1], a.shape[2])
    return a


def kernel(x, g_mix, w_in, lru_conv_w, lru_conv_b, lru_wa, lru_ba, lru_wx, lru_bx, lru_lambda, lru_w_out, sc_conv_w, sc_w_out, w_o, g_ffn, ffn_w_up, ffn_conv_w, ffn_w_down, g_final, loss_target, m_g_mix, m_w_in, m_lru_conv_w, m_lru_conv_b, m_lru_wa, m_lru_ba, m_lru_wx, m_lru_bx, m_lru_lambda, m_lru_w_out, m_sc_conv_w, m_sc_w_out, m_w_o, m_g_ffn, m_ffn_w_up, m_ffn_conv_w, m_ffn_w_down, m_g_final, v_g_mix, v_w_in, v_lru_conv_w, v_lru_conv_b, v_lru_wa, v_lru_ba, v_lru_wx, v_lru_bx, v_lru_lambda, v_lru_w_out, v_sc_conv_w, v_sc_w_out, v_w_o, v_g_ffn, v_ffn_w_up, v_ffn_conv_w, v_ffn_w_down, v_g_final):
    T, D = x.shape[1], x.shape[2]
    d_lru = lru_lambda.shape[0]
    d_sc = sc_conv_w.shape[1] * N_DEV
    F = ffn_w_down.shape[0] * N_DEV
    H = lru_wa.shape[0]
    assert d_lru == d_sc and H * HEAD_DIM == d_lru
    xs = x.reshape(T, D)
    tgt = loss_target.reshape(T, D)
    my_x, my_y, my_c = _place()
    me = 4 * my_x + 2 * my_y + my_c

    big = [w_in, lru_w_out, sc_w_out, w_o, ffn_w_up, ffn_w_down]
    big_names = ["w_in", "lru_w_out", "sc_w_out", "w_o", "ffn_w_up", "ffn_w_down"]
    place = jnp.stack([my_c, 2 * my_x + my_y, me]).astype(jnp.int32)
    n_in, n_up = w_in.shape[1], ffn_w_up.shape[1]
    paired = [n_in, None, None, None, n_up, None]
    big_bf = [_cast_into_slot(w, place, "cast_" + nm, paired=pn is not None)
              for w, nm, pn in zip(big, big_names, paired)]
    pad_rows = lambda a: jnp.pad(a, ((0, SUB - a.shape[0]), (0, 0)))
    gathered = _all_gather(big_bf + [pad_rows(lru_conv_w), pad_rows(sc_conv_w), pad_rows(ffn_conv_w)],
                           [True] * 6 + [False] * 3,
                           [True, True, True, False, False, False, True, True, True],
                           paired + [None] * 3, "all_gather_first")
    W_in, W_lo, W_so, W_o8, W_up, W_dn8 = gathered[:6]
    full_cols = lambda g, kw: g[:, :kw, :].transpose(1, 0, 2).reshape(kw, -1)
    cw_lru = full_cols(gathered[6], lru_conv_w.shape[0])
    cw_sc = full_cols(gathered[7], sc_conv_w.shape[0])
    cw_ffn = full_cols(gathered[8], ffn_conv_w.shape[0])

    C = _tile(d_lru, C_LRU)
    hpb = C // HEAD_DIM
    wa_bd = _block_diag(lru_wa, hpb).astype(BF16)
    wx_bd = _block_diag(lru_wx, hpb).astype(BF16)
    cb, ba, bx, lam = (a.reshape(1, d_lru) for a in (lru_conv_b, lru_ba, lru_bx, lru_lambda))

    h1 = _rms_fwd(xs, g_mix, "rms_mix")
    k8 = W_up.shape[1] // 8
    wide = 2 * max(n_in, n_up)
    p, ((W_o8,), (W_up,)) = _mm_nn(
        h1, W_in, out_dtype=BF16, name="mm_in", tn=wide,
        tasks=[_gather_task(W_o8, ici=ALL_ROWS), _gather_task(W_up, ici=(0, 4 * k8), pair_n=n_up)])
    hs, yl_pre, ((W_o8,), (W_up,)) = _lru_fwd(
        p, cw_lru, cb, wa_bd, ba, wx_bd, bx, lam, name="lru_fwd",
        tasks=[_gather_task(W_o8, fwd=ALL_ROWS),
               _gather_task(W_up, ici=(4 * k8, 2 * k8), fwd=(0, 4 * k8), pair_n=n_up)])
    ys_pre = _sc_fwd(p, cw_sc, d=d_sc, name="sc_fwd")
    y_lru, ((W_up,),) = _mm_small(
        "nn", yl_pre, None, W_lo, name="mm_lru_out",
        tasks=[_gather_task(W_up, ici=(6 * k8, k8), fwd=(4 * k8, 2 * k8), pair_n=n_up)])
    y_sc, ((W_up,),) = _mm_small(
        "nn", ys_pre, None, W_so, name="mm_sc_out",
        tasks=[_gather_task(W_up, ici=(7 * k8, k8), fwd=(6 * k8, k8), pair_n=n_up)])
    gate0 = 2 * d_lru + 3 * d_sc
    merged = _merge_fwd(p, y_lru, y_sc, col0=gate0, name="merge_fwd")
    W_o = W_o8.reshape(1, D, D)
    x1, ((W_up,),) = _mm_nn(merged, W_o, out_dtype=F32, residual=xs, name="mm_o",
                            tasks=[_gather_task(W_up, fwd=(7 * k8, k8), pair_n=n_up)])
    h2 = _rms_fwd(x1, g_ffn, "rms_ffn")
    uu, ((W_dn8,),) = _mm_nn(h2, W_up, out_dtype=BF16, name="mm_up", tn=wide,
                             tasks=[_gather_task(W_dn8, ici=ALL_ROWS)])
    act, ((W_dn8,),) = _ffn_act_fwd(uu, cw_ffn, name="ffn_act_fwd", tasks=[_gather_task(W_dn8, fwd=ALL_ROWS)])
    W_dn = W_dn8.reshape(1, F, D)
    x2 = _mm_nn(act, W_dn, out_dtype=F32, residual=x1, name="mm_down", tn=512, tk=F)
    dx2, dx2b, loss_part, dg_final = _loss_head(x2, g_final, tgt, "loss_head")

    def pack_rows(arrs):
        flat = jnp.concatenate([a.reshape(-1) for a in arrs])
        rows = -(-flat.shape[0] // (SUB * LANES)) * SUB
        return jnp.pad(flat, (0, rows * LANES - flat.shape[0])).reshape(rows, LANES)

    def unpack_rows(pack, arrs):
        flat, out, o = pack.reshape(-1), [], 0
        for a in arrs:
            out.append(flat[o:o + a.size].reshape(a.shape))
            o += a.size
        return out

    dact = _mm_nt(dx2b, W_dn, out_dtype=BF16, name="mm_down_dx", tm=512, tko=F // 2)
    gW_dn = _mm_tn(act, dx2b, 1, out_dtype=BF16, name="mm_down_dw", tk=1408).reshape(N_DEV, F // N_DEV, D)
    duu, dcw_ffn_g, dcw_ffn_v = _ffn_act_bwd(uu, dact, cw_ffn, name="ffn_act_bwd")
    dh2, ((land_dn,),) = _mm_nt(duu, W_up, out_dtype=BF16, name="mm_up_dx", tn=wide, tasks=[_swap_task(gW_dn)])
    parts_dn = _add_halves(gW_dn, land_dn, place, "rs_add_ffn_w_down")
    gW_up, ((mine_dn,),) = _mm_tn(h2, duu, N_CHIP, out_dtype=BF16, name="mm_up_dw", tn=wide,
                                  tasks=[_exchange_task(*parts_dn)])
    dx1, dx1b, dg_ffn = _rms_bwd(x1, g_ffn, dh2, dx2, "rms_ffn_bwd")
    dmerged, ((land_up,),) = _mm_nt(dx1b, W_o, out_dtype=BF16, name="mm_o_dx",
                                    tasks=[_swap_task(gW_up, pair_n=n_up)])
    parts_up = _add_halves(gW_up, land_up, place, "rs_add_ffn_w_up", pair_n=n_up)
    gW_o = _mm_tn(merged, dx1b, 1, out_dtype=BF16, name="mm_o_dw").reshape(N_DEV, D // N_DEV, D)
    dp, dy_lru, dy_sc = _merge_bwd(p, y_lru, y_sc, dmerged, col0=gate0, name="merge_bwd")
    dyl_pre, ((land_o,),) = _mm_small("nt", None, dy_lru, W_lo, name="mm_lru_out_dx", tasks=[_swap_task(gW_o)])
    parts_o = _add_halves(gW_o, land_o, place, "rs_add_w_o")
    gW_lo = _mm_small("tn", yl_pre, dy_lru, W_lo, name="mm_lru_out_dw")
    dys_pre, ((land_lo,),) = _mm_small("nt", None, dy_sc, W_so, name="mm_sc_out_dx", tasks=[_swap_task(gW_lo)])
    parts_lo = _add_halves(gW_lo, land_lo, place, "rs_add_lru_w_out")
    gW_so = _mm_small("tn", ys_pre, dy_sc, W_so, name="mm_sc_out_dw")
    dp, dcw_sc = _sc_bwd(p, dys_pre, dp, cw_sc, d=d_sc, name="sc_bwd")
    r_up = parts_up[0].shape[1] // 2
    dp, dcw_lru, dcb, dwa_bd, dba, dwx_bd, dbx, dlam, ((land_up,), (mine_o,), (mine_lo,), (land_so,)) = _lru_bwd(
        p, hs, dyl_pre, dp, cw_lru, cb, wa_bd, ba, wx_bd, bx, lam, name="lru_bwd",
        tasks=[_exchange_task(*parts_up, rows=(0, r_up)), _exchange_task(*parts_o), _exchange_task(*parts_lo),
               _swap_task(gW_so)])
    parts_so = _add_halves(gW_so, land_so, place, "rs_add_sc_w_out")

    dwa = _diag_blocks(dwa_bd, hpb, HEAD_DIM)
    dwx = _diag_blocks(dwx_bd, hpb, HEAD_DIM)
    dcw_ffn = jnp.concatenate([dcw_ffn_g, dcw_ffn_v], axis=1)
    small_full = [dcw_lru, dcb, dwa, dba, dwx, dbx, dlam, dcw_sc, dg_ffn, dcw_ffn, dg_final]
    gW_in, ((mine_up,), (mine_so,), (packs,)) = _mm_tn(
        h1, dp, N_CHIP, out_dtype=BF16, name="mm_in_dw", tn=wide,
        tasks=[_exchange_task(parts_up[0], land_up, rows=(r_up, r_up)), _exchange_task(*parts_so),
               _bcast_task(pack_rows(small_full))])
    land_in = _swap_halves(gW_in, "rs_swap_w_in", pair_n=n_in)
    parts_in = _add_halves(gW_in, land_in, place, "rs_add_w_in", pair_n=n_in)
    dh1, ((mine_in,),) = _mm_nt(dp, W_in, out_dtype=BF16, name="mm_in_dx", tn=wide,
                                tasks=[_exchange_task(*parts_in)])
    grad_x, _, dg_mix = _rms_bwd(xs, g_mix, dh1, dx1, "rms_mix_bwd")

    mine = [mine_in, mine_lo, mine_so, mine_o, mine_up, mine_dn]
    big_m = [m_w_in, m_lru_w_out, m_sc_w_out, m_w_o, m_ffn_w_up, m_ffn_w_down]
    big_v = [v_w_in, v_lru_w_out, v_sc_w_out, v_w_o, v_ffn_w_up, v_ffn_w_down]
    big_out = {nm: _adamw_big(pt, w, m, v, "adamw_" + nm)
               for nm, pt, w, m, v in zip(big_names, mine, big, big_m, big_v)}

    (scw_lru, scb, swa, sba, swx, sbx, slam, scw_sc, sg_ffn, scw_ffn, sg_final) = unpack_rows(
        _sum_packs(packs, "sum_small"), small_full)
    (sg_mix,) = unpack_rows(_all_reduce_small(pack_rows([dg_mix]), "all_reduce_g_mix"), [dg_mix])

    def my_cols(a):
        n = a.shape[1] // N_DEV
        return lax.dynamic_slice_in_dim(a, me * n, n, axis=1)

    small_names = ["g_mix", "lru_conv_w", "lru_conv_b", "lru_wa", "lru_ba", "lru_wx", "lru_bx", "lru_lambda",
                   "sc_conv_w", "g_ffn", "ffn_conv_w", "g_final"]
    small_w = [g_mix, lru_conv_w, lru_conv_b, lru_wa, lru_ba, lru_wx, lru_bx, lru_lambda, sc_conv_w, g_ffn,
               ffn_conv_w, g_final]
    small_m = [m_g_mix, m_lru_conv_w, m_lru_conv_b, m_lru_wa, m_lru_ba, m_lru_wx, m_lru_bx, m_lru_lambda,
               m_sc_conv_w, m_g_ffn, m_ffn_conv_w, m_g_final]
    small_v = [v_g_mix, v_lru_conv_w, v_lru_conv_b, v_lru_wa, v_lru_ba, v_lru_wx, v_lru_bx, v_lru_lambda,
               v_sc_conv_w, v_g_ffn, v_ffn_conv_w, v_g_final]
    small_g = [sg_mix.reshape(D), my_cols(scw_lru), scb.reshape(d_lru), swa, sba.reshape(d_lru), swx,
               sbx.reshape(d_lru), slam.reshape(d_lru), my_cols(scw_sc), sg_ffn.reshape(D), my_cols(scw_ffn),
               sg_final.reshape(D)]
    sd, snm, snv = _adamw_small([_as_rows(a) for a in small_w], [_as_rows(a) for a in small_g],
                                [_as_rows(a) for a in small_m], [_as_rows(a) for a in small_v], "adamw_small")
    small_out = {nm: (g, d.reshape(w.shape), nm_.reshape(w.shape), nv_.reshape(w.shape))
                 for nm, w, g, d, nm_, nv_ in zip(small_names, small_w, small_g, sd, snm, snv)}

    loss = lax.psum(loss_part[0, 0], AXES)
    order = ["g_mix", "w_in", "lru_conv_w", "lru_conv_b", "lru_wa", "lru_ba", "lru_wx", "lru_bx", "lru_lambda",
             "lru_w_out", "sc_conv_w", "sc_w_out", "w_o", "g_ffn", "ffn_w_up", "ffn_conv_w", "ffn_w_down", "g_final"]
    res = {**big_out, **small_out}
    return (loss, grad_x.reshape(x.shape),
            *[res[nm][0] for nm in order], *[res[nm][1] for nm in order],
            *[res[nm][2] for nm in order], *[res[nm][3] for nm in order])
```

```python
import functools
import math

import jax
import jax.numpy as jnp
from jax import lax
from jax.experimental import pallas as pl
from jax.experimental.pallas import tpu as pltpu

F32, BF16 = jnp.float32, jnp.bfloat16
MESH = pl.DeviceIdType.MESH
N_DEV = 8
N_CHIP = 4
AXES = ("x", "y", "c")

EPS = 1e-6
LRU_C = 8.0
HEAD_DIM = 64
ADAM_LR, ADAM_B1, ADAM_B2, ADAM_EPS, ADAM_WD, ADAM_STEP = 0.001, 0.9, 0.999, 1e-08, 0.01, 10

VMEM_LIMIT = 48 * 1024 * 1024
LANES = 128
SUB = 8
HALO = 16
TB = 512
C_LRU = 256
C_EW = 512
TM, TN, TK = 512, 1536, 2048


def _tile(n, pref, align=LANES):
    best = None
    for d in range(align, min(n, pref) + 1, align):
        if n % d == 0:
            best = d
    return best or n


def _cparams(sem=None, vmem=VMEM_LIMIT):
    kw = dict(vmem_limit_bytes=vmem)
    if sem is not None:
        kw["dimension_semantics"] = sem
    return pltpu.CompilerParams(**kw)


def _S(shape, dtype):
    return jax.ShapeDtypeStruct(shape, dtype)


ANY = pl.BlockSpec(memory_space=pl.ANY)
VMEM_SPEC = pl.BlockSpec(memory_space=pltpu.VMEM)


class _Task:
    def __init__(self, arrays, aliased, start, wait, fresh=(), nsem=3):
        self.arrays, self.aliased, self.start, self.wait = arrays, aliased, start, wait
        self.fresh, self.nsem = list(fresh), nsem


def _call(name, grid, compute, in_specs, args, out_shape, out_specs, scratch, tasks=(), own_aliases=None):
    n_in, n_out, n_scr = len(args), len(out_shape), len(scratch)
    x_in, x_out, aliases, where = [], [], dict(own_aliases or {}), []
    for t in tasks:
        places = []
        for k, arr in enumerate(t.arrays):
            if k in t.aliased:
                aliases[n_in + len(x_in)] = n_out + len(x_out)
                places.append(("out", len(x_out)))
                x_out.append(_S(arr.shape, arr.dtype))
            else:
                places.append(("in", len(x_in)))
            x_in.append(arr)
        for shp in t.fresh:
            places.append(("out", len(x_out)))
            x_out.append(shp)
        where.append(places)
    n_xi, n_xo = len(x_in), len(x_out)

    def body(*refs):
        ins, xi = refs[:n_in], refs[n_in:n_in + n_xi]
        o0 = n_in + n_xi
        outs, xo = refs[o0:o0 + n_out], refs[o0 + n_out:o0 + n_out + n_xo]
        s0 = o0 + n_out + n_xo
        scr, sems = refs[s0:s0 + n_scr], refs[s0 + n_scr:]
        ids = [pl.program_id(a) for a in range(len(grid))]

        def task_refs(ti):
            return [xo[i] if kind == "out" else xi[i] for kind, i in where[ti]]

        if tasks:
            first = functools.reduce(jnp.logical_and, [i == 0 for i in ids])

            @pl.when(first)
            def _():
                for ti, t in enumerate(tasks):
                    t.start(task_refs(ti), *sems[3 * ti:3 * ti + 3])

        compute(*ins, *outs, *scr)
        if tasks:
            last = functools.reduce(jnp.logical_and, [i == g - 1 for i, g in zip(ids, grid)])

            @pl.when(last)
            def _():
                for ti, t in enumerate(tasks):
                    t.wait(task_refs(ti), *sems[3 * ti:3 * ti + 3])

    sem_shapes = []
    for t in tasks:
        sem_shapes += [pltpu.SemaphoreType.DMA((t.nsem,)), pltpu.SemaphoreType.DMA((t.nsem,)),
                       pltpu.SemaphoreType.DMA((1,))]
    res = pl.pallas_call(
        body, name=name, grid=grid,
        in_specs=list(in_specs) + [ANY] * n_xi,
        out_specs=tuple(out_specs) + (ANY,) * n_xo,
        out_shape=tuple(out_shape) + tuple(x_out),
        scratch_shapes=list(scratch) + sem_shapes,
        input_output_aliases=aliases,
        compiler_params=_cparams(("arbitrary",) * len(grid)),
    )(*args, *x_in)
    outs, passed, o = res[:n_out], [], n_out
    for places in where:
        k = sum(1 for kind, _ in places if kind == "out")
        passed.append(list(res[o:o + k]))
        o += k
    return outs, passed


def _mm_nn(a, w3, *, out_dtype, name, residual=None, tm=TM, tn=TN, tk=TK, tasks=()):
    M, K = a.shape
    G, _, n = w3.shape
    tm, tn, tk = _tile(M, tm, SUB), _tile(n, tn), _tile(K, tk)
    nj, nk = n // tn, K // tk

    def compute(*refs):
        if residual is None:
            a_ref, w_ref, o_ref = refs[:3]
            r_ref = None
        else:
            a_ref, w_ref, r_ref, o_ref = refs[:4]

        def finish(r):
            if r_ref is not None:
                r = r + r_ref[...]
            o_ref[...] = r.astype(o_ref.dtype)

        if nk == 1:
            finish(jnp.dot(a_ref[...], w_ref[...], preferred_element_type=F32))
            return
        acc = refs[-1]
        k = pl.program_id(3)

        @pl.when(k == 0)
        def _():
            acc[...] = jnp.zeros_like(acc)

        acc[...] += jnp.dot(a_ref[...], w_ref[...], preferred_element_type=F32)

        @pl.when(k == nk - 1)
        def _():
            finish(acc[...])

    in_specs = [pl.BlockSpec((tm, tk), lambda g, j, i, k: (i, k)),
                pl.BlockSpec((None, tk, tn), lambda g, j, i, k: (g, k, j))]
    args = [a, w3]
    if residual is not None:
        in_specs.append(pl.BlockSpec((tm, tn), lambda g, j, i, k: (i, g * nj + j)))
        args.append(residual)
    outs, passed = _call(
        name, (G, nj, M // tm, nk), compute, in_specs, args, [_S((M, G * n), out_dtype)],
        [pl.BlockSpec((tm, tn), lambda g, j, i, k: (i, g * nj + j))],
        [] if nk == 1 else [pltpu.VMEM((tm, tn), F32)], tasks)
    return (outs[0], passed) if tasks else outs[0]


def _mm_nt(dy, w3, *, out_dtype, name, tm=1024, tko=1024, tn=TN, tasks=()):
    M, _ = dy.shape
    G, K, n = w3.shape
    tm, tko, tn = _tile(M, tm, SUB), _tile(K, tko), _tile(n, tn)
    nj = n // tn
    nr = G * nj

    def compute(dy_ref, w_ref, o_ref, *scr):
        part = lax.dot_general(dy_ref[...], w_ref[...], (((1,), (1,)), ((), ())), preferred_element_type=F32)
        if nr == 1:
            o_ref[...] = part.astype(o_ref.dtype)
            return
        (acc,) = scr
        r = pl.program_id(2)

        @pl.when(r == 0)
        def _():
            acc[...] = jnp.zeros_like(acc)

        acc[...] += part

        @pl.when(r == nr - 1)
        def _():
            o_ref[...] = acc[...].astype(o_ref.dtype)

    outs, passed = _call(
        name, (K // tko, M // tm, nr), compute,
        [pl.BlockSpec((tm, tn), lambda ko, i, r: (i, r)),
         pl.BlockSpec((None, tko, tn), lambda ko, i, r: (r // nj, ko, r % nj))],
        [dy, w3], [_S((M, K), out_dtype)], [pl.BlockSpec((tm, tko), lambda ko, i, r: (i, ko))],
        [] if nr == 1 else [pltpu.VMEM((tm, tko), F32)], tasks)
    return (outs[0], passed) if tasks else outs[0]


def _mm_tn(a, dy, G, *, out_dtype, name, tk=1024, tn=TN, tt=1024, tasks=()):
    M, K = a.shape
    n = dy.shape[1] // G
    tk, tn, tt = _tile(K, tk), _tile(n, tn), _tile(M, tt, SUB)
    nj, nt = n // tn, M // tt

    def compute(a_ref, dy_ref, o_ref, acc):
        t = pl.program_id(3)

        @pl.when(t == 0)
        def _():
            acc[...] = jnp.zeros_like(acc)

        acc[...] += lax.dot_general(a_ref[...], dy_ref[...], (((0,), (0,)), ((), ())),
                                    preferred_element_type=F32)

        @pl.when(t == nt - 1)
        def _():
            o_ref[...] = acc[...].astype(o_ref.dtype)

    outs, passed = _call(
        name, (G, nj, K // tk, nt), compute,
        [pl.BlockSpec((tt, tk), lambda g, j, k, t: (t, k)),
         pl.BlockSpec((tt, tn), lambda g, j, k, t: (t, g * nj + j))],
        [a, dy], [_S((G, K, n), out_dtype)], [pl.BlockSpec((None, tk, tn), lambda g, j, k, t: (g, k, j))],
        [pltpu.VMEM((tk, tn), F32)], tasks)
    return (outs[0], passed) if tasks else outs[0]


def _mm_small(kind, a, b, w3, *, name, tm=1024, tasks=()):
    G, K, n = w3.shape
    M = (a if a is not None else b).shape[0]
    tm = _tile(M, tm, HALO)
    nt = M // tm
    w_spec = pl.BlockSpec((G, K, n), lambda i: (0, 0, 0))
    a_spec = pl.BlockSpec((tm, K), lambda i: (i, 0))
    b_spec = pl.BlockSpec((tm, G * n), lambda i: (i, 0))
    cols = lambda g: slice(g * n, (g + 1) * n)
    if kind == "nn":
        def compute(a_ref, w_ref, o_ref):
            av = a_ref[...]
            for g in range(G):
                o_ref[:, cols(g)] = jnp.dot(av, w_ref[g], preferred_element_type=F32).astype(o_ref.dtype)

        outs, passed = _call(name, (nt,), compute, [a_spec, w_spec], [a, w3], [_S((M, G * n), BF16)], [b_spec], [], tasks)
    elif kind == "nt":
        def compute(b_ref, w_ref, o_ref):
            acc = None
            for g in range(G):
                part = lax.dot_general(b_ref[:, cols(g)], w_ref[g], (((1,), (1,)), ((), ())),
                                       preferred_element_type=F32)
                acc = part if acc is None else acc + part
            o_ref[...] = acc.astype(o_ref.dtype)

        outs, passed = _call(name, (nt,), compute, [b_spec, w_spec], [b, w3], [_S((M, K), BF16)], [a_spec], [], tasks)
    else:
        def compute(a_ref, b_ref, o_ref, acc):
            i = pl.program_id(0)

            @pl.when(i == 0)
            def _():
                acc[...] = jnp.zeros_like(acc)

            at = a_ref[...].T
            for g in range(G):
                acc[g] += jnp.dot(at, b_ref[:, cols(g)], preferred_element_type=F32)

            @pl.when(i == nt - 1)
            def _():
                o_ref[...] = acc[...].astype(o_ref.dtype)

        outs, passed = _call(name, (nt,), compute, [a_spec, b_spec], [a, b], [_S((G, K, n), BF16)], [w_spec],
                             [pltpu.VMEM((G, K, n), F32)], tasks)
    return (outs[0], passed) if tasks else outs[0]


def _cast_into_slot(w, place, name, paired=False):
    R, C = w.shape
    tr = _tile(R, 512, HALO)

    def body(s_ref, w_ref, o_ref):
        del s_ref
        o_ref[...] = w_ref[...].astype(BF16)

    if paired:
        shape, out_map = (N_CHIP, R, 2 * C), lambda i, s: (s[1], i, s[0])
    else:
        shape, out_map = (N_DEV, R, C), lambda i, s: (s[2], i, 0)
    return pl.pallas_call(
        body, name=name, out_shape=_S(shape, BF16),
        grid_spec=pltpu.PrefetchScalarGridSpec(
            num_scalar_prefetch=1, grid=(R // tr,),
            in_specs=[pl.BlockSpec((tr, C), lambda i, s: (i, 0))],
            out_specs=pl.BlockSpec((None, tr, C), out_map)),
        compiler_params=_cparams(("parallel",)),
    )(place, w)


def _down(cur, prev8, j):
    return pltpu.roll(jnp.concatenate([prev8, cur], axis=0), j, 0)[SUB:, :]


def _up(cur, next8, j):
    n = cur.shape[0] + SUB
    return pltpu.roll(jnp.concatenate([cur, next8], axis=0), n - j, 0)[:cur.shape[0], :]


def _shifted_down(x, prev8, n):
    full = jnp.concatenate([prev8, x], axis=0)
    return [x] + [pltpu.roll(full, s, 0)[SUB:, :] for s in range(1, n)]


def _shifted_up(x, next8, n):
    m = x.shape[0] + SUB
    full = jnp.concatenate([x, next8], axis=0)
    return [x] + [pltpu.roll(full, m - s, 0)[:x.shape[0], :] for s in range(1, n)]


def _taps(sh, w_ref):
    kw = w_ref.shape[0]
    y = sh[0] * w_ref[pl.ds(kw - 1, 1), :]
    for k in range(kw - 1):
        y = y + sh[kw - 1 - k] * w_ref[pl.ds(k, 1), :]
    return y


def _conv(x, prev8, w_ref):
    return _taps(_shifted_down(x, prev8, w_ref.shape[0]), w_ref)


def _conv_t(dy, next8, w_ref):
    return _taps(_shifted_up(dy, next8, w_ref.shape[0]), w_ref)


def _conv_dw(dw_ref, dy, x, prev8, first):
    kw = dw_ref.shape[0]

    @pl.when(first)
    def _():
        dw_ref[...] = jnp.zeros_like(dw_ref)

    for k in range(kw):
        xs = x if k == kw - 1 else _down(x, prev8, kw - 1 - k)
        dw_ref[pl.ds(k, 1), :] += jnp.sum(dy * xs, axis=0, keepdims=True)


def _acc(ref, val, first):
    @pl.when(first)
    def _():
        ref[...] = jnp.zeros_like(ref)

    ref[...] += val


def _acc_row(ref, val, first):
    _acc(ref, jnp.sum(val, axis=0, keepdims=True), first)


def _prev8(h_ref, t):
    return jnp.where(t > 0, h_ref[...].astype(F32)[HALO - SUB:, :], 0.0)


def _next8(h_ref, is_last):
    return jnp.where(is_last, 0.0, h_ref[...].astype(F32)[:SUB, :])


_GELU_K0 = math.sqrt(2.0 / math.pi)
_GELU_K1 = 0.044715


def _gelu_and_grad(x):
    x2 = x * x
    th = jnp.tanh(_GELU_K0 * x * (1.0 + _GELU_K1 * x2))
    g = 0.5 * x * (1.0 + th)
    dg = 0.5 * (1.0 + th) + 0.5 * x * (1.0 - th * th) * (_GELU_K0 * (1.0 + 3.0 * _GELU_K1 * x2))
    return g, dg


def _neg_expm1(z):
    series = -z * (1.0 + z * (0.5 + z * (1.0 / 6.0 + z * (1.0 / 24.0))))
    return jnp.where(z > -0.03, series, 1.0 - jnp.exp(z))


def _store_staged(stages, dst_hbm, sems, step, n_steps, where):
    def copies(s, slot):
        return [pltpu.make_async_copy(
            st.at[slot], dst_hbm.at[pl.ds(r0, st.shape[1]), pl.ds(c0, st.shape[2])], sems.at[slot, k])
            for k, (st, (r0, c0)) in enumerate(zip(stages, where(s)))]

    slot = step % 2

    @pl.when(step > 0)
    def _():
        for cp in copies(step - 1, 1 - slot):
            cp.wait()

    for cp in copies(step, slot):
        cp.start()

    @pl.when(step == n_steps - 1)
    def _():
        for cp in copies(step, slot):
            cp.wait()


def _halo_prev_map(hb, col_fn):
    return lambda c, t: (jnp.maximum(t * hb - 1, 0), col_fn(c))


def _rms_fwd(x, g, name):
    T, D = x.shape
    tb = _tile(T, TB, SUB)

    def body(x_ref, g_ref, o_ref):
        xv = x_ref[...]
        rstd = lax.rsqrt(jnp.mean(xv * xv, axis=-1, keepdims=True) + EPS)
        o_ref[...] = (xv * rstd * g_ref[...]).astype(BF16)

    return pl.pallas_call(
        body, name=name, out_shape=_S((T, D), BF16), grid=(T // tb,),
        in_specs=[pl.BlockSpec((tb, D), lambda i: (i, 0)), pl.BlockSpec((1, D), lambda i: (0, 0))],
        out_specs=pl.BlockSpec((tb, D), lambda i: (i, 0)),
        compiler_params=_cparams(("parallel",)),
    )(x, g.reshape(1, D))


def _rms_bwd(x, g, dh, dres, name):
    T, D = x.shape
    tb = _tile(T, 256, SUB)

    def body(x_ref, g_ref, dh_ref, dr_ref, dx_ref, dxb_ref, dg_ref):
        i = pl.program_id(0)
        xv = x_ref[...]
        rstd = lax.rsqrt(jnp.mean(xv * xv, axis=-1, keepdims=True) + EPS)
        xn = xv * rstd
        dhv = dh_ref[...].astype(F32)
        _acc_row(dg_ref, dhv * xn, i == 0)
        dxn = dhv * g_ref[...]
        dx = dr_ref[...] + rstd * (dxn - xn * jnp.mean(dxn * xn, axis=-1, keepdims=True))
        dx_ref[...] = dx
        dxb_ref[...] = dx.astype(BF16)

    blk = pl.BlockSpec((tb, D), lambda i: (i, 0))
    vec = pl.BlockSpec((1, D), lambda i: (0, 0))
    return pl.pallas_call(
        body, name=name, out_shape=(_S((T, D), F32), _S((T, D), BF16), _S((1, D), F32)),
        grid=(T // tb,), in_specs=[blk, vec, blk, blk], out_specs=(blk, blk, vec),
        compiler_params=_cparams(("arbitrary",)),
    )(x, g.reshape(1, D), dh, dres)


def _loss_head(x2, g, target, name):
    T, D = x2.shape
    tb = _tile(T, 256, SUB)

    def body(x_ref, g_ref, t_ref, dx_ref, dxb_ref, loss_ref, dg_ref):
        i = pl.program_id(0)
        xv = x_ref[...]
        rstd = lax.rsqrt(jnp.mean(xv * xv, axis=-1, keepdims=True) + EPS)
        xn = xv * rstd
        err = xn * g_ref[...] - t_ref[...]
        part = 0.5 * jnp.sum(jnp.mean(err * err, axis=-1, keepdims=True), axis=0, keepdims=True)
        part = jnp.broadcast_to(part, (1, LANES))
        _acc(loss_ref, part, i == 0)
        dy = err * (1.0 / D)
        _acc_row(dg_ref, dy * xn, i == 0)
        dxn = dy * g_ref[...]
        dx = rstd * (dxn - xn * jnp.mean(dxn * xn, axis=-1, keepdims=True))
        dx_ref[...] = dx
        dxb_ref[...] = dx.astype(BF16)

    blk = pl.BlockSpec((tb, D), lambda i: (i, 0))
    vec = pl.BlockSpec((1, D), lambda i: (0, 0))
    return pl.pallas_call(
        body, name=name,
        out_shape=(_S((T, D), F32), _S((T, D), BF16), _S((1, LANES), F32), _S((1, D), F32)),
        grid=(T // tb,), in_specs=[blk, vec, blk],
        out_specs=(blk, blk, pl.BlockSpec((1, LANES), lambda i: (0, 0)), vec),
        compiler_params=_cparams(("arbitrary",)),
    )(x2, g.reshape(1, D), target)


def _lru_gates(xc, wa_ref, ba_ref, wx_ref, bx_ref, lam_ref):
    xcb = xc.astype(BF16)
    r = jax.nn.sigmoid(jnp.dot(xcb, wa_ref[...], preferred_element_type=F32) + ba_ref[...])
    i = jax.nn.sigmoid(jnp.dot(xcb, wx_ref[...], preferred_element_type=F32) + bx_ref[...])
    sp = jax.nn.softplus(-lam_ref[...])
    log_a = (-LRU_C * sp) * r
    a = jnp.exp(log_a)
    s = jnp.sqrt(_neg_expm1(2.0 * log_a))
    return xcb, r, i, a, s


def _lru_fwd(p, conv_w, conv_b, wa_bd, ba, wx_bd, bx, lam, *, name, tasks=()):
    T = p.shape[0]
    d = lam.shape[-1]
    C = _tile(d, C_LRU)
    nC = d // C
    tb = _tile(T, TB, HALO)
    nT, hb, nt = T // tb, tb // HALO, tb // SUB

    def body(x_ref, xh_ref, g_ref, cw_ref, cb_ref, wa_ref, ba_ref, wx_ref, bx_ref, lam_ref,
             hs_ref, y_ref, a_s, u_s, h_s):
        t = pl.program_id(1)

        @pl.when(t == 0)
        def _():
            h_s[...] = jnp.zeros_like(h_s)

        x = x_ref[...].astype(F32)
        xc = _conv(x, _prev8(xh_ref, t), cw_ref) + cb_ref[...]
        _, r, i, a, s = _lru_gates(xc, wa_ref, ba_ref, wx_ref, bx_ref, lam_ref)
        a_s[...] = a
        u_s[...] = s * (i * xc)
        row = lax.broadcasted_iota(jnp.int32, (SUB, C), 0)

        def step(k, h):
            o = pl.multiple_of(k * SUB, SUB)
            A = a_s[pl.ds(o, SUB), :]
            B = u_s[pl.ds(o, SUB), :]
            for sh in (1, 2, 4):
                m = row >= sh
                Ap = pltpu.roll(A, sh, 0)
                Bp = pltpu.roll(B, sh, 0)
                B = jnp.where(m, A * Bp + B, B)
                A = jnp.where(m, A * Ap, A)
            hs = A * h + B
            hs_ref[pl.ds(o, SUB), :] = hs
            return jnp.broadcast_to(hs[SUB - 1:SUB, :], (SUB, C))

        h_s[...] = lax.fori_loop(0, nt, step, h_s[...])
        gel, _ = _gelu_and_grad(g_ref[...].astype(F32))
        y_ref[...] = (gel * hs_ref[...]).astype(BF16)

    vec = pl.BlockSpec((1, C), lambda c, t: (0, c))
    sq = pl.BlockSpec((None, C, C), lambda c, t: (c, 0, 0))
    outs, passed = _call(
        name, (nC, nT), body,
        [pl.BlockSpec((tb, C), lambda c, t: (t, c)),
         pl.BlockSpec((HALO, C), _halo_prev_map(hb, lambda c: c)),
         pl.BlockSpec((tb, C), lambda c, t: (t, nC + c)),
         pl.BlockSpec((conv_w.shape[0], C), lambda c, t: (0, c)),
         vec, sq, vec, sq, vec, vec],
        [p, p, p, conv_w, conv_b, wa_bd, ba, wx_bd, bx, lam],
        [_S((T, d), F32), _S((T, d), BF16)],
        [pl.BlockSpec((tb, C), lambda c, t: (t, c)), pl.BlockSpec((tb, C), lambda c, t: (t, c))],
        [pltpu.VMEM((tb, C), F32), pltpu.VMEM((tb, C), F32), pltpu.VMEM((SUB, C), F32)], tasks)
    return (*outs, passed) if tasks else outs


def _lru_bwd(p, hs, dyl, dp, conv_w, conv_b, wa_bd, ba, wx_bd, bx, lam, *, name, tasks=()):
    T = p.shape[0]
    d = lam.shape[-1]
    C = _tile(d, C_LRU)
    nC = d // C
    tb = _tile(T, TB, HALO)
    nT, hb, nt = T // tb, tb // HALO, tb // SUB
    kw = conv_w.shape[0]

    def body(x_ref, xh_ref, g_ref, hs_ref, hh_ref, dy_ref, cw_ref, cb_ref, wa_ref, ba_ref, wx_ref, bx_ref,
             lam_ref, dp_in, dp_ref, dcw_ref, dcb_ref, dwa_ref, dba_ref, dwx_ref, dbx_ref, dlam_ref,
             b_s, g_s, dh_s, an_s, dhn_s, dxn_s, st_x, st_g, sems):
        del dp_in
        c = pl.program_id(0)
        tr = pl.program_id(1)
        t = nT - 1 - tr
        first = tr == 0

        @pl.when(first)
        def _():
            an_s[...] = jnp.zeros_like(an_s)
            dhn_s[...] = jnp.zeros_like(dhn_s)
            dxn_s[...] = jnp.zeros_like(dxn_s)

        x = x_ref[...].astype(F32)
        xprev = _prev8(xh_ref, t)
        xc = _conv(x, xprev, cw_ref) + cb_ref[...]
        xcb, r, i, a, s = _lru_gates(xc, wa_ref, ba_ref, wx_ref, bx_ref, lam_ref)
        hsv = hs_ref[...]
        dy = dy_ref[...].astype(F32)
        gel, dgel = _gelu_and_grad(g_ref[...].astype(F32))
        step_no = c * nT + tr
        slot = step_no % 2
        st_g[slot] = (dy * hsv * dgel).astype(BF16)

        b_s[...] = _up(a, an_s[...], 1)
        g_s[...] = dy * gel
        row = lax.broadcasted_iota(jnp.int32, (SUB, C), 0)

        def step(k, carry):
            o = pl.multiple_of((nt - 1 - k) * SUB, SUB)
            B = b_s[pl.ds(o, SUB), :]
            G = g_s[pl.ds(o, SUB), :]
            for sh in (1, 2, 4):
                m = row < SUB - sh
                Bn = pltpu.roll(B, SUB - sh, 0)
                Gn = pltpu.roll(G, SUB - sh, 0)
                G = jnp.where(m, B * Gn + G, G)
                B = jnp.where(m, B * Bn, B)
            dh = B * carry + G
            dh_s[pl.ds(o, SUB), :] = dh
            return jnp.broadcast_to(dh[0:1, :], (SUB, C))

        dhn_s[...] = lax.fori_loop(0, nt, step, dhn_s[...])
        an_s[...] = a[:SUB, :]
        dh = dh_s[...]

        hprev = _down(hsv, jnp.where(t > 0, hh_ref[...][HALO - SUB:, :], 0.0), 1)
        d_a = dh * hprev
        ixc = i * xc
        d_s = dh * ixc
        d_i = dh * s * xc
        d_xc = dh * s * i
        d_l = d_a * a - d_s * (a * a) / s
        sp = jax.nn.softplus(-lam_ref[...])
        _acc_row(dlam_ref, d_l * r * (LRU_C * jax.nn.sigmoid(-lam_ref[...])), first)
        d_zr = (d_l * (-LRU_C * sp)) * r * (1.0 - r)
        d_zi = d_i * i * (1.0 - i)
        _acc_row(dba_ref, d_zr, first)
        _acc_row(dbx_ref, d_zi, first)
        d_zrb = d_zr.astype(BF16)
        d_zib = d_zi.astype(BF16)
        tn_dims = (((0,), (0,)), ((), ()))
        nt_dims = (((1,), (1,)), ((), ()))
        gwa = lax.dot_general(xcb, d_zrb, tn_dims, preferred_element_type=F32)
        gwx = lax.dot_general(xcb, d_zib, tn_dims, preferred_element_type=F32)
        _acc(dwa_ref, gwa, first)
        _acc(dwx_ref, gwx, first)
        d_xc = (d_xc + lax.dot_general(d_zrb, wa_ref[...], nt_dims, preferred_element_type=F32)
                + lax.dot_general(d_zib, wx_ref[...], nt_dims, preferred_element_type=F32))
        _acc_row(dcb_ref, d_xc, first)
        _conv_dw(dcw_ref, d_xc, x, xprev, first)
        st_x[slot] = _conv_t(d_xc, dxn_s[...], cw_ref).astype(BF16)
        dxn_s[...] = d_xc[:SUB, :]

        def where(s):
            row0, col0 = (nT - 1 - s % nT) * tb, (s // nT) * C
            return [(row0, col0), (row0, d + col0)]

        _store_staged([st_x, st_g], dp_ref, sems, step_no, nC * nT, where)

    rev = lambda c, tr: (nT - 1 - tr, c)
    vec = pl.BlockSpec((1, C), lambda c, tr: (0, c))
    sq = pl.BlockSpec((None, C, C), lambda c, tr: (c, 0, 0))
    cwb = pl.BlockSpec((kw, C), lambda c, tr: (0, c))
    halo_prev = lambda c, tr: (jnp.maximum((nT - 1 - tr) * hb - 1, 0), c)
    outs, passed = _call(
        name, (nC, nT), body,
        [pl.BlockSpec((tb, C), rev),
         pl.BlockSpec((HALO, C), halo_prev),
         pl.BlockSpec((tb, C), lambda c, tr: (nT - 1 - tr, nC + c)),
         pl.BlockSpec((tb, C), rev),
         pl.BlockSpec((HALO, C), halo_prev),
         pl.BlockSpec((tb, C), rev),
         cwb, vec, sq, vec, sq, vec, vec, ANY],
        [p, p, p, hs, hs, dyl, conv_w, conv_b, wa_bd, ba, wx_bd, bx, lam, dp],
        [_S(dp.shape, dp.dtype), _S((kw, d), F32), _S((1, d), F32), _S((nC, C, C), F32), _S((1, d), F32),
         _S((nC, C, C), F32), _S((1, d), F32), _S((1, d), F32)],
        [ANY, cwb, vec, sq, vec, sq, vec, vec],
        [pltpu.VMEM((tb, C), F32), pltpu.VMEM((tb, C), F32), pltpu.VMEM((tb, C), F32),
         pltpu.VMEM((SUB, C), F32), pltpu.VMEM((SUB, C), F32), pltpu.VMEM((SUB, C), F32),
         pltpu.VMEM((2, tb, C), BF16), pltpu.VMEM((2, tb, C), BF16), pltpu.SemaphoreType.DMA((2, 2))],
        tasks, own_aliases={13: 0})
    return (*outs, passed) if tasks else outs


def _sc_fwd(p, conv_w, *, d, name):
    T = p.shape[0]
    C = _tile(d, C_EW)
    nC = d // C
    tb = _tile(T, TB, HALO)
    nT, hb = T // tb, tb // HALO

    def body(b_ref, c_ref, ch_ref, v_ref, vh_ref, w_ref, y_ref):
        t = pl.program_id(1)
        cv = c_ref[...].astype(F32) * v_ref[...].astype(F32)
        cvp = _prev8(ch_ref, t) * _prev8(vh_ref, t)
        y_ref[...] = (b_ref[...].astype(F32) * _conv(cv, cvp, w_ref)).astype(BF16)

    seg = lambda k: pl.BlockSpec((tb, C), lambda c, t: (t, k * nC + c))
    hseg = lambda k: pl.BlockSpec((HALO, C), _halo_prev_map(hb, lambda c: k * nC + c))
    return pl.pallas_call(
        body, name=name, out_shape=_S((T, d), BF16), grid=(nC, nT),
        in_specs=[seg(2), seg(3), hseg(3), seg(4), hseg(4), pl.BlockSpec((conv_w.shape[0], C), lambda c, t: (0, c))],
        out_specs=pl.BlockSpec((tb, C), lambda c, t: (t, c)),
        compiler_params=_cparams(("parallel", "parallel")),
    )(p, p, p, p, p, conv_w)


def _sc_bwd(p, dys, dp, conv_w, *, d, name):
    T = p.shape[0]
    C = _tile(d, C_EW)
    nC = d // C
    tb = _tile(T, TB, HALO)
    nT, hb = T // tb, tb // HALO
    kw = conv_w.shape[0]

    def body(b_ref, bn_ref, c_ref, ch_ref, v_ref, vh_ref, dy_ref, dyn_ref, w_ref, dp_in, dp_ref, dw_ref,
             st_b, st_c, st_v, sems):
        del dp_in
        c = pl.program_id(0)
        t = pl.program_id(1)
        last = t == nT - 1
        bv = b_ref[...].astype(F32)
        cvv = c_ref[...].astype(F32)
        vv = v_ref[...].astype(F32)
        dy = dy_ref[...].astype(F32)
        cv = cvv * vv
        cvp = _prev8(ch_ref, t) * _prev8(vh_ref, t)
        step_no = c * nT + t
        slot = step_no % 2
        st_b[slot] = (dy * _conv(cv, cvp, w_ref)).astype(BF16)
        dz = dy * bv
        dzn = _next8(dyn_ref, last) * _next8(bn_ref, last)
        _conv_dw(dw_ref, dz, cv, cvp, t == 0)
        dcv = _conv_t(dz, dzn, w_ref)
        st_c[slot] = (dcv * vv).astype(BF16)
        st_v[slot] = (dcv * cvv).astype(BF16)

        def where(s):
            return [((s % nT) * tb, (2 + k) * d + (s // nT) * C) for k in range(3)]

        _store_staged([st_b, st_c, st_v], dp_ref, sems, step_no, nC * nT, where)

    seg = lambda k: pl.BlockSpec((tb, C), lambda c, t: (t, k * nC + c))
    hseg = lambda k: pl.BlockSpec((HALO, C), _halo_prev_map(hb, lambda c: k * nC + c))
    last_h = T // HALO - 1
    nseg = lambda k: pl.BlockSpec((HALO, C), lambda c, t: (jnp.minimum((t + 1) * hb, last_h), k * nC + c))
    return pl.pallas_call(
        body, name=name, out_shape=(_S(dp.shape, dp.dtype), _S((kw, d), F32)), grid=(nC, nT),
        in_specs=[seg(2), nseg(2), seg(3), hseg(3), seg(4), hseg(4),
                  pl.BlockSpec((tb, C), lambda c, t: (t, c)), nseg(0),
                  pl.BlockSpec((kw, C), lambda c, t: (0, c)), ANY],
        out_specs=(ANY, pl.BlockSpec((kw, C), lambda c, t: (0, c))),
        scratch_shapes=[pltpu.VMEM((2, tb, C), BF16)] * 3 + [pltpu.SemaphoreType.DMA((2, 3))],
        input_output_aliases={9: 0},
        compiler_params=_cparams(("arbitrary", "arbitrary")),
    )(p, p, p, p, p, p, dys, dys, conv_w, dp)


def _merge_fwd(p, y_lru, y_sc, *, col0, name):
    T, D = y_lru.shape
    C = _tile(math.gcd(D, col0), 1024)
    nC = D // C
    k0 = col0 // C
    tb = _tile(T, 256, HALO)

    def body(gl_ref, gs_ref, yl_ref, ys_ref, o_ref):
        @pl.loop(0, tb // HALO)
        def _(k):
            rows = pl.ds(pl.multiple_of(k * HALO, HALO), HALO)
            for l0 in range(0, C, min(C, C_EW)):
                at = (rows, pl.ds(l0, min(C, C_EW)))
                o_ref[at] = (jax.nn.sigmoid(gl_ref[at].astype(F32)) * yl_ref[at].astype(F32)
                             + jax.nn.sigmoid(gs_ref[at].astype(F32)) * ys_ref[at].astype(F32)).astype(BF16)

    blk = pl.BlockSpec((tb, C), lambda c, t: (t, c))
    return pl.pallas_call(
        body, name=name, out_shape=_S((T, D), BF16), grid=(nC, T // tb),
        in_specs=[pl.BlockSpec((tb, C), lambda c, t: (t, k0 + c)),
                  pl.BlockSpec((tb, C), lambda c, t: (t, k0 + nC + c)), blk, blk],
        out_specs=blk, compiler_params=_cparams(("parallel", "parallel")),
    )(p, p, y_lru, y_sc)


def _merge_bwd(p, y_lru, y_sc, dm, *, col0, name):
    T, D = y_lru.shape
    C = _tile(math.gcd(D, col0), 1024)
    nC = D // C
    k0 = col0 // C
    tb = _tile(T, 256, HALO)
    nT = T // tb

    def body(gl_ref, gs_ref, yl_ref, ys_ref, dm_ref, dp_ref, dyl_ref, dys_ref, st_l, st_s, sems):
        step_no = pl.program_id(0) * nT + pl.program_id(1)
        slot = step_no % 2

        @pl.loop(0, tb // HALO)
        def _(k):
            rows = pl.ds(pl.multiple_of(k * HALO, HALO), HALO)
            for l0 in range(0, C, min(C, C_EW)):
                at = (rows, pl.ds(l0, min(C, C_EW)))
                dmv = dm_ref[at].astype(F32)
                sl = jax.nn.sigmoid(gl_ref[at].astype(F32))
                ss = jax.nn.sigmoid(gs_ref[at].astype(F32))
                dyl_ref[at] = (dmv * sl).astype(BF16)
                dys_ref[at] = (dmv * ss).astype(BF16)
                st_l[(slot,) + at] = (dmv * yl_ref[at].astype(F32) * sl * (1.0 - sl)).astype(BF16)
                st_s[(slot,) + at] = (dmv * ys_ref[at].astype(F32) * ss * (1.0 - ss)).astype(BF16)

        def where(s):
            row0, colc = (s % nT) * tb, (s // nT) * C
            return [(row0, col0 + colc), (row0, col0 + D + colc)]

        _store_staged([st_l, st_s], dp_ref, sems, step_no, nC * nT, where)

    blk = pl.BlockSpec((tb, C), lambda c, t: (t, c))
    return pl.pallas_call(
        body, name=name, out_shape=(_S(p.shape, BF16), _S((T, D), BF16), _S((T, D), BF16)),
        grid=(nC, nT),
        in_specs=[pl.BlockSpec((tb, C), lambda c, t: (t, k0 + c)),
                  pl.BlockSpec((tb, C), lambda c, t: (t, k0 + nC + c)), blk, blk, blk],
        out_specs=(ANY, blk, blk),
        scratch_shapes=[pltpu.VMEM((2, tb, C), BF16), pltpu.VMEM((2, tb, C), BF16), pltpu.SemaphoreType.DMA((2, 2))],
        compiler_params=_cparams(("arbitrary", "arbitrary")),
    )(p, p, y_lru, y_sc, dm)


def _ffn_act_fwd(uu, conv_w, *, name, tasks=()):
    T = uu.shape[0]
    F = uu.shape[1] // 2
    C = _tile(F, C_EW)
    nC = F // C
    tb = _tile(T, TB, HALO)
    nT, hb = T // tb, tb // HALO
    kw = conv_w.shape[0]
    R = HALO

    def body(g_ref, gh_ref, v_ref, vh_ref, wg_ref, wv_ref, o_ref):
        t = pl.program_id(1)

        def chunk(k, carry):
            gp, vp = carry
            r0 = pl.multiple_of(k * R, R)
            ug = g_ref[pl.ds(r0, R), :].astype(F32)
            uv = v_ref[pl.ds(r0, R), :].astype(F32)
            cg = _conv(ug, gp, wg_ref)
            cv = _conv(uv, vp, wv_ref)
            o_ref[pl.ds(r0, R), :] = (cg * jax.nn.sigmoid(cg) * cv).astype(BF16)
            return ug[R - SUB:, :], uv[R - SUB:, :]

        lax.fori_loop(0, tb // R, chunk, (_prev8(gh_ref, t), _prev8(vh_ref, t)))

    seg = lambda k: pl.BlockSpec((tb, C), lambda c, t: (t, k * nC + c))
    hseg = lambda k: pl.BlockSpec((HALO, C), _halo_prev_map(hb, lambda c: k * nC + c))
    wseg = lambda k: pl.BlockSpec((kw, C), lambda c, t: (0, k * nC + c))
    outs, passed = _call(
        name, (nC, nT), body, [seg(0), hseg(0), seg(1), hseg(1), wseg(0), wseg(1)],
        [uu, uu, uu, uu, conv_w, conv_w], [_S((T, F), BF16)], [pl.BlockSpec((tb, C), lambda c, t: (t, c))], [], tasks)
    return (outs[0], passed) if tasks else outs[0]


def _ffn_act_bwd(uu, dact, conv_w, *, name):
    T = uu.shape[0]
    F = uu.shape[1] // 2
    C = _tile(F, C_EW)
    nC = F // C
    tb = _tile(T, TB, HALO)
    nT, hb = T // tb, tb // HALO
    kw = conv_w.shape[0]
    R = HALO
    nk = tb // R

    def body(g_ref, gh_ref, v_ref, vh_ref, da_ref, wg_ref, wv_ref, du_ref, dwg_ref, dwv_ref,
             gn_s, vn_s, accg_s, accv_s, st_g, st_v, sems):
        c = pl.program_id(0)
        tr = pl.program_id(1)
        t = nT - 1 - tr
        first = tr == 0

        @pl.when(first)
        def _():
            gn_s[...] = jnp.zeros_like(gn_s)
            vn_s[...] = jnp.zeros_like(vn_s)
            dwg_ref[...] = jnp.zeros_like(dwg_ref)
            dwv_ref[...] = jnp.zeros_like(dwv_ref)

        accg_s[...] = jnp.zeros_like(accg_s)
        accv_s[...] = jnp.zeros_like(accv_s)
        step_no = c * nT + tr
        slot = step_no % 2

        def chunk(i, carry):
            gn, vn = carry
            k = nk - 1 - i
            r0 = pl.multiple_of(k * R, R)
            rp = pl.multiple_of(jnp.maximum(r0 - R, 0), R)
            ug = g_ref[pl.ds(r0, R), :].astype(F32)
            uv = v_ref[pl.ds(r0, R), :].astype(F32)
            gp = jnp.where(k > 0, g_ref[pl.ds(rp, R), :].astype(F32)[R - SUB:, :], _prev8(gh_ref, t))
            vp = jnp.where(k > 0, v_ref[pl.ds(rp, R), :].astype(F32)[R - SUB:, :], _prev8(vh_ref, t))
            sh_g = _shifted_down(ug, gp, kw)
            sh_v = _shifted_down(uv, vp, kw)
            cg = _taps(sh_g, wg_ref)
            cv = _taps(sh_v, wv_ref)
            da = da_ref[pl.ds(r0, R), :].astype(F32)
            sg = jax.nn.sigmoid(cg)
            d_cg = da * cv * (sg * (1.0 + cg * (1.0 - sg)))
            d_cv = da * (cg * sg)
            for j in range(kw):
                accg_s[j] += d_cg * sh_g[kw - 1 - j]
                accv_s[j] += d_cv * sh_v[kw - 1 - j]
            st_g[slot, pl.ds(r0, R), :] = _conv_t(d_cg, gn, wg_ref).astype(BF16)
            st_v[slot, pl.ds(r0, R), :] = _conv_t(d_cv, vn, wv_ref).astype(BF16)
            return d_cg[:SUB, :], d_cv[:SUB, :]

        gn, vn = lax.fori_loop(0, nk, chunk, (gn_s[...], vn_s[...]))
        gn_s[...] = gn
        vn_s[...] = vn
        for j in range(kw):
            dwg_ref[pl.ds(j, 1), :] += jnp.sum(accg_s[j], axis=0, keepdims=True)
            dwv_ref[pl.ds(j, 1), :] += jnp.sum(accv_s[j], axis=0, keepdims=True)
        def where(s):
            row0, col0 = (nT - 1 - s % nT) * tb, (s // nT) * C
            return [(row0, col0), (row0, F + col0)]

        _store_staged([st_g, st_v], du_ref, sems, step_no, nC * nT, where)

    seg = lambda k: pl.BlockSpec((tb, C), lambda c, tr: (nT - 1 - tr, k * nC + c))
    hseg = lambda k: pl.BlockSpec((HALO, C), lambda c, tr: (jnp.maximum((nT - 1 - tr) * hb - 1, 0), k * nC + c))
    wseg = lambda k: pl.BlockSpec((kw, C), lambda c, tr: (0, k * nC + c))
    dwb = pl.BlockSpec((kw, C), lambda c, tr: (0, c))
    return pl.pallas_call(
        body, name=name, out_shape=(_S(uu.shape, BF16), _S((kw, F), F32), _S((kw, F), F32)), grid=(nC, nT),
        in_specs=[seg(0), hseg(0), seg(1), hseg(1), pl.BlockSpec((tb, C), lambda c, tr: (nT - 1 - tr, c)),
                  wseg(0), wseg(1)],
        out_specs=(ANY, dwb, dwb),
        scratch_shapes=[pltpu.VMEM((SUB, C), F32), pltpu.VMEM((SUB, C), F32),
                        pltpu.VMEM((kw, R, C), F32), pltpu.VMEM((kw, R, C), F32),
                        pltpu.VMEM((2, tb, C), BF16), pltpu.VMEM((2, tb, C), BF16), pltpu.SemaphoreType.DMA((2, 2))],
        compiler_params=_cparams(("arbitrary", "arbitrary")),
    )(uu, uu, uu, uu, dact, conv_w, conv_w)


def _place():
    x, y, c = lax.axis_index("x"), lax.axis_index("y"), lax.axis_index("c")
    return x, y, c


def _chips(x, y):
    return [(1 - x, y), (x, 1 - y), (1 - x, 1 - y)]


def _all_gather(arrays, placed, over_ici, pair_n, name):
    n = len(arrays)

    def body(*refs):
        ins, outs = refs[:n], refs[n:2 * n]
        send_sems, recv_sems, local_sems = refs[2 * n:]
        x, y, c = _place()
        me, sibling = (x, y, c), (x, y, 1 - c)
        chips = _chips(x, y)
        full = [a for a in range(n) if over_ici[a]]

        def idx(px, py, pc):
            return 4 * px + 2 * py + pc

        def copy(a, k, block, to):
            dst = _dev_block(outs[a], idx(*block), pair_n[a])
            src = ins[a] if (block is me and not placed[a]) else dst
            return pltpu.make_async_remote_copy(
                src_ref=src, dst_ref=dst, send_sem=send_sems.at[a, k], recv_sem=recv_sems.at[a, k],
                device_id=to, device_id_type=MESH)

        mine = [pltpu.make_async_copy(ins[a], outs[a].at[idx(*me)], local_sems.at[a])
                for a in range(n) if not placed[a]]
        for cp in mine:
            cp.start()
        first = []
        for a in full:
            first += [copy(a, 1 + j, me, (*chip, c)) for j, chip in enumerate(chips)]
        for a in range(n):
            first.append(copy(a, 0, me, sibling))
        for cp in first:
            cp.start()
        passed = []
        for a in full:
            for j, chip in enumerate(chips):
                copy(a, 1 + j, (*chip, c), me).wait_recv()
                cp = copy(a, 4 + j, (*chip, c), sibling)
                cp.start()
                passed.append(cp)
        for a in range(n):
            copy(a, 0, sibling, me).wait_recv()
        for a in full:
            for j, chip in enumerate(chips):
                copy(a, 4 + j, (*chip, 1 - c), me).wait_recv()
        for cp in first + passed:
            cp.wait_send()
        for cp in mine:
            cp.wait()

    return pl.pallas_call(
        body, name=name,
        out_shape=tuple(_S(s.shape if placed[a] else (N_DEV,) + s.shape, s.dtype) for a, s in enumerate(arrays)),
        in_specs=[ANY] * n, out_specs=tuple([ANY] * n),
        scratch_shapes=[pltpu.SemaphoreType.DMA((n, 7)), pltpu.SemaphoreType.DMA((n, 7)),
                        pltpu.SemaphoreType.DMA((n,))],
        input_output_aliases={a: a for a in range(n) if placed[a]},
    )(*arrays)


def _dev_block(ref, dev, pair_n=None):
    if pair_n is None:
        return ref.at[dev]
    return ref.at[dev // 2, :, pl.ds(pl.multiple_of((dev % 2) * pair_n, LANES), pair_n)]


def _rows_of(ref, blk, rows, pair_n=None):
    v = _dev_block(ref, blk, pair_n)
    return v if rows is None else v.at[pl.ds(rows[0], rows[1])]


ALL_ROWS = "all"


def _gather_task(buf, ici=None, fwd=None, pair_n=None):
    rows = lambda r: None if r == ALL_ROWS else r
    blk_of = functools.partial(_rows_of, pair_n=pair_n)

    def copies(refs, ss, rs):
        x, y, c = _place()
        me = 4 * x + 2 * y + c
        cps = []
        for j, (px, py) in enumerate(_chips(x, y)):
            if ici is not None:
                blk = blk_of(refs[0], me, rows(ici))
                cps.append(pltpu.make_async_remote_copy(
                    src_ref=blk, dst_ref=blk, send_sem=ss.at[j], recv_sem=rs.at[j],
                    device_id=(px, py, c), device_id_type=MESH))
            if fwd is not None:
                blk = blk_of(refs[0], 4 * px + 2 * py + c, rows(fwd))
                cps.append(pltpu.make_async_remote_copy(
                    src_ref=blk, dst_ref=blk, send_sem=ss.at[3 + j], recv_sem=rs.at[3 + j],
                    device_id=(x, y, 1 - c), device_id_type=MESH))
        return cps

    def start(refs, ss, rs, ls):
        for cp in copies(refs, ss, rs):
            cp.start()

    def wait(refs, ss, rs, ls):
        x, y, c = _place()
        for j, (px, py) in enumerate(_chips(x, y)):
            if ici is not None:
                blk = blk_of(refs[0], 4 * px + 2 * py + c, rows(ici))
                pltpu.make_async_remote_copy(
                    src_ref=blk, dst_ref=blk, send_sem=ss.at[j], recv_sem=rs.at[j],
                    device_id=(px, py, c), device_id_type=MESH).wait_recv()
            if fwd is not None:
                blk = blk_of(refs[0], 4 * px + 2 * py + 1 - c, rows(fwd))
                pltpu.make_async_remote_copy(
                    src_ref=blk, dst_ref=blk, send_sem=ss.at[3 + j], recv_sem=rs.at[3 + j],
                    device_id=(x, y, 1 - c), device_id_type=MESH).wait_recv()
        for cp in copies(refs, ss, rs):
            cp.wait_send()

    return _Task([buf], [0], start, wait, nsem=6)


def _exchange_task(parts, landing, rows=None):
    def copies(refs, ss, rs):
        x, y, c = _place()
        myq = 2 * x + y
        return [pltpu.make_async_remote_copy(
            src_ref=_rows_of(refs[0], 2 * px + py, rows), dst_ref=_rows_of(refs[1], myq, rows),
            send_sem=ss.at[k], recv_sem=rs.at[k], device_id=(px, py, c), device_id_type=MESH)
            for k, (px, py) in enumerate(_chips(x, y))]

    def start(refs, ss, rs, ls):
        for cp in copies(refs, ss, rs):
            cp.start()

    def wait(refs, ss, rs, ls):
        x, y, c = _place()
        for k, (px, py) in enumerate(_chips(x, y)):
            pltpu.make_async_remote_copy(
                src_ref=_rows_of(refs[0], 2 * x + y, rows), dst_ref=_rows_of(refs[1], 2 * px + py, rows),
                send_sem=ss.at[k], recv_sem=rs.at[k], device_id=(px, py, c), device_id_type=MESH).wait_recv()
        for cp in copies(refs, ss, rs):
            cp.wait_send()

    return _Task([parts, landing], [1], start, wait)


def _core_blocks(g, pair_n):
    if pair_n is None:
        g4 = g.reshape((N_CHIP, 2) + g.shape[1:])
        return g4, (N_CHIP,) + g.shape[1:], lambda ref, c: ref.at[:, c]
    view = lambda ref, c: ref.at[:, :, pl.ds(pl.multiple_of(c * pair_n, LANES), pair_n)]
    return g, (N_CHIP, g.shape[1], pair_n), view


def _swap_task(g, pair_n=None):
    g4, shape, view = _core_blocks(g, pair_n)

    def copy(refs, ss, rs):
        x, y, c = _place()
        return pltpu.make_async_remote_copy(
            src_ref=view(refs[0], 1 - c), dst_ref=refs[1], send_sem=ss.at[0], recv_sem=rs.at[0],
            device_id=(x, y, 1 - c), device_id_type=MESH)

    def start(refs, ss, rs, ls):
        copy(refs, ss, rs).start()

    def wait(refs, ss, rs, ls):
        copy(refs, ss, rs).wait()

    return _Task([g4], [], start, wait, fresh=[_S(shape, g.dtype)], nsem=1)


def _peer(x, y, c, m):
    return x ^ (m >> 2), y ^ ((m >> 1) & 1), c ^ (m & 1)


def _bcast_task(pack):
    def copies(refs, ss, rs):
        x, y, c = _place()
        me = 4 * x + 2 * y + c
        return [pltpu.make_async_remote_copy(
            src_ref=refs[0], dst_ref=refs[1].at[me], send_sem=ss.at[m - 1], recv_sem=rs.at[m - 1],
            device_id=_peer(x, y, c, m), device_id_type=MESH) for m in range(1, N_DEV)]

    def local(refs, ls):
        x, y, c = _place()
        return pltpu.make_async_copy(refs[0], refs[1].at[4 * x + 2 * y + c], ls.at[0])

    def start(refs, ss, rs, ls):
        local(refs, ls).start()
        for cp in copies(refs, ss, rs):
            cp.start()

    def wait(refs, ss, rs, ls):
        x, y, c = _place()
        for m in range(1, N_DEV):
            px, py, pc = _peer(x, y, c, m)
            pltpu.make_async_remote_copy(
                src_ref=refs[0], dst_ref=refs[1].at[4 * px + 2 * py + pc], send_sem=ss.at[m - 1],
                recv_sem=rs.at[m - 1], device_id=(px, py, pc), device_id_type=MESH).wait_recv()
        for cp in copies(refs, ss, rs):
            cp.wait_send()
        local(refs, ls).wait()

    return _Task([pack], [], start, wait, fresh=[_S((N_DEV,) + pack.shape, pack.dtype)], nsem=N_DEV - 1)


def _sum_packs(packs, name):
    _, R, L = packs.shape

    def body(p_ref, o_ref):
        acc = p_ref[0]
        for k in range(1, N_DEV):
            acc = acc + p_ref[k]
        o_ref[...] = acc

    return pl.pallas_call(body, name=name, out_shape=_S((R, L), packs.dtype), in_specs=[VMEM_SPEC],
                          out_specs=VMEM_SPEC, compiler_params=_cparams())(packs)


def _swap_halves(g, name, pair_n=None):
    g4, shape, view = _core_blocks(g, pair_n)

    def body(g_ref, o_ref, send_sem, recv_sem):
        x, y, c = _place()
        cp = pltpu.make_async_remote_copy(
            src_ref=view(g_ref, 1 - c), dst_ref=o_ref, send_sem=send_sem, recv_sem=recv_sem,
            device_id=(x, y, 1 - c), device_id_type=MESH)
        cp.start()
        cp.wait()

    return pl.pallas_call(
        body, name=name, out_shape=_S(shape, g.dtype), in_specs=[ANY], out_specs=ANY,
        scratch_shapes=[pltpu.SemaphoreType.DMA, pltpu.SemaphoreType.DMA],
    )(g4)


def _add_halves(g, landed, place, name, pair_n=None):
    _, r, cc = landed.shape
    tr = _tile(r, 512, HALO)
    if pair_n is None:
        g4 = g.reshape(N_CHIP, 2, r, cc)
        g_spec = pl.BlockSpec((None, None, tr, cc), lambda i, q, s: (q, s[0], i, 0))
    else:
        g4 = g
        g_spec = pl.BlockSpec((None, tr, cc), lambda i, q, s: (q, i, s[0]))

    def body(s_ref, g_ref, l_ref, o_ref, land_ref):
        q = pl.program_id(1)
        v = (g_ref[...].astype(F32) + l_ref[...].astype(F32)).astype(BF16)
        o_ref[...] = v

        @pl.when(q == s_ref[1])
        def _():
            land_ref[...] = v

    return pl.pallas_call(
        body, name=name, out_shape=(_S((N_CHIP, r, cc), BF16), _S((N_CHIP, r, cc), BF16)),
        grid_spec=pltpu.PrefetchScalarGridSpec(
            num_scalar_prefetch=1, grid=(r // tr, N_CHIP),
            in_specs=[g_spec,
                      pl.BlockSpec((None, tr, cc), lambda i, q, s: (q, i, 0))],
            out_specs=(pl.BlockSpec((None, tr, cc), lambda i, q, s: (q, i, 0)),
                       pl.BlockSpec((None, tr, cc), lambda i, q, s: (s[1], i, 0)))),
        compiler_params=_cparams(("arbitrary", "arbitrary")),
    )(place, g4, landed)


def _all_reduce_small(pack, name):
    R = pack.shape[0]

    def body(p_ref, o_ref, buf, send_sems, recv_sems):
        x, y, c = _place()
        me = 4 * x + 2 * y + c
        buf[me] = p_ref[...]
        cps = []
        for k in range(N_DEV - 1):
            m = k + 1
            peer = (x ^ (m >> 2), y ^ ((m >> 1) & 1), c ^ (m & 1))
            cps.append(pltpu.make_async_remote_copy(
                src_ref=p_ref, dst_ref=buf.at[me], send_sem=send_sems.at[k], recv_sem=recv_sems.at[k],
                device_id=peer, device_id_type=MESH))
        for cp in cps:
            cp.start()
        for k in range(N_DEV - 1):
            m = k + 1
            peer_idx = 4 * (x ^ (m >> 2)) + 2 * (y ^ ((m >> 1) & 1)) + (c ^ (m & 1))
            pltpu.make_async_remote_copy(
                src_ref=p_ref, dst_ref=buf.at[peer_idx], send_sem=send_sems.at[k], recv_sem=recv_sems.at[k],
                device_id=(x, y, c), device_id_type=MESH).wait_recv()
        for cp in cps:
            cp.wait_send()
        acc = buf[0]
        for k in range(1, N_DEV):
            acc = acc + buf[k]
        o_ref[...] = acc

    return pl.pallas_call(
        body, name=name, out_shape=_S((R, LANES), F32),
        in_specs=[VMEM_SPEC], out_specs=VMEM_SPEC,
        scratch_shapes=[pltpu.VMEM((N_DEV, R, LANES), F32), pltpu.SemaphoreType.DMA((N_DEV - 1,)),
                        pltpu.SemaphoreType.DMA((N_DEV - 1,))],
        compiler_params=_cparams(),
    )(pack)


def _adamw_math(w, g, m, v):
    m = ADAM_B1 * m + (1.0 - ADAM_B1) * g
    v = ADAM_B2 * v + (1.0 - ADAM_B2) * (g * g)
    m_hat = m / (1.0 - ADAM_B1 ** ADAM_STEP)
    v_hat = v / (1.0 - ADAM_B2 ** ADAM_STEP)
    delta = -ADAM_LR * (m_hat / (jnp.sqrt(v_hat) + ADAM_EPS) + ADAM_WD * w)
    return delta, m, v


def _adamw_big(parts, w, m, v, name):
    r, cc = w.shape
    tr = _tile(r, 128, HALO)

    def body(p_ref, w_ref, m_ref, v_ref, g_ref, d_ref, nm_ref, nv_ref):
        g = p_ref[0].astype(F32)
        for q in range(1, N_CHIP):
            g = g + p_ref[q].astype(F32)
        g_ref[...] = g
        d_ref[...], nm_ref[...], nv_ref[...] = _adamw_math(w_ref[...], g, m_ref[...], v_ref[...])

    blk = pl.BlockSpec((tr, cc), lambda i: (i, 0))
    return pl.pallas_call(
        body, name=name, out_shape=tuple(_S((r, cc), F32) for _ in range(4)), grid=(r // tr,),
        in_specs=[pl.BlockSpec((N_CHIP, tr, cc), lambda i: (0, i, 0)), blk, blk, blk],
        out_specs=(blk, blk, blk, blk), compiler_params=_cparams(("parallel",)),
    )(parts, w, m, v)


def _adamw_small(ws, gs, ms, vs, name):
    n = len(ws)

    def body(*refs):
        w_r, g_r, m_r, v_r = refs[:n], refs[n:2 * n], refs[2 * n:3 * n], refs[3 * n:4 * n]
        d_r, nm_r, nv_r = refs[4 * n:5 * n], refs[5 * n:6 * n], refs[6 * n:7 * n]
        for k in range(n):
            d_r[k][...], nm_r[k][...], nv_r[k][...] = _adamw_math(w_r[k][...], g_r[k][...], m_r[k][...], v_r[k][...])

    shapes = tuple(_S(w.shape, F32) for w in ws)
    outs = pl.pallas_call(
        body, name=name, out_shape=shapes * 3,
        in_specs=[VMEM_SPEC] * (4 * n), out_specs=tuple([VMEM_SPEC] * (3 * n)),
        compiler_params=_cparams(),
    )(*ws, *gs, *ms, *vs)
    return outs[:n], outs[n:2 * n], outs[2 * n:]


def _block_diag(w, heads_per_block):
    H, hd, _ = w.shape
    nb = H // heads_per_block
    eye = jnp.eye(heads_per_block, dtype=w.dtype)
    w4 = w.reshape(nb, heads_per_block, hd, hd)
    return jnp.einsum("nhab,hg->nhagb", w4, eye).reshape(nb, heads_per_block * hd, heads_per_block * hd)


def _diag_blocks(bd, heads_per_block, hd):
    nb = bd.shape[0]
    b5 = bd.reshape(nb, heads_per_block, hd, heads_per_block, hd)
    return jnp.stack([b5[:, h, :, h, :] for h in range(heads_per_block)], axis=1).reshape(nb * heads_per_block, hd, hd)


def _as_rows(a):
    if a.ndim == 1:
        return a.reshape(-1, LANES) if a.shape[0] % LANES == 0 else a.reshape(1, -1)
    if a.ndim == 3:
        return a.reshape(-1, LANES) if (a.size % LANES == 0) else a.reshape(a.shape[0] * a.shape[1], a.shape[2])
    return a


def kernel(x, g_mix, w_in, lru_conv_w, lru_conv_b, lru_wa, lru_ba, lru_wx, lru_bx, lru_lambda, lru_w_out, sc_conv_w, sc_w_out, w_o, g_ffn, ffn_w_up, ffn_conv_w, ffn_w_down, g_final, loss_target, m_g_mix, m_w_in, m_lru_conv_w, m_lru_conv_b, m_lru_wa, m_lru_ba, m_lru_wx, m_lru_bx, m_lru_lambda, m_lru_w_out, m_sc_conv_w, m_sc_w_out, m_w_o, m_g_ffn, m_ffn_w_up, m_ffn_conv_w, m_ffn_w_down, m_g_final, v_g_mix, v_w_in, v_lru_conv_w, v_lru_conv_b, v_lru_wa, v_lru_ba, v_lru_wx, v_lru_bx, v_lru_lambda, v_lru_w_out, v_sc_conv_w, v_sc_w_out, v_w_o, v_g_ffn, v_ffn_w_up, v_ffn_conv_w, v_ffn_w_down, v_g_final):
    T, D = x.shape[1], x.shape[2]
    d_lru = lru_lambda.shape[0]
    d_sc = sc_conv_w.shape[1] * N_DEV
    F = ffn_w_down.shape[0] * N_DEV
    H = lru_wa.shape[0]
    assert d_lru == d_sc and H * HEAD_DIM == d_lru
    xs = x.reshape(T, D)
    tgt = loss_target.reshape(T, D)
    my_x, my_y, my_c = _place()
    me = 4 * my_x + 2 * my_y + my_c

    big = [w_in, lru_w_out, sc_w_out, w_o, ffn_w_up, ffn_w_down]
    big_names = ["w_in", "lru_w_out", "sc_w_out", "w_o", "ffn_w_up", "ffn_w_down"]
    place = jnp.stack([my_c, 2 * my_x + my_y, me]).astype(jnp.int32)
    n_in, n_up = w_in.shape[1], ffn_w_up.shape[1]
    paired = [n_in, None, None, None, n_up, None]
    big_bf = [_cast_into_slot(w, place, "cast_" + nm, paired=pn is not None)
              for w, nm, pn in zip(big, big_names, paired)]
    pad_rows = lambda a: jnp.pad(a, ((0, SUB - a.shape[0]), (0, 0)))
    gathered = _all_gather(big_bf + [pad_rows(lru_conv_w), pad_rows(sc_conv_w), pad_rows(ffn_conv_w)],
                           [True] * 6 + [False] * 3,
                           [True, True, True, False, False, False, True, True, True],
                           paired + [None] * 3, "all_gather_first")
    W_in, W_lo, W_so, W_o8, W_up, W_dn8 = gathered[:6]
    full_cols = lambda g, kw: g[:, :kw, :].transpose(1, 0, 2).reshape(kw, -1)
    cw_lru = full_cols(gathered[6], lru_conv_w.shape[0])
    cw_sc = full_cols(gathered[7], sc_conv_w.shape[0])
    cw_ffn = full_cols(gathered[8], ffn_conv_w.shape[0])

    C = _tile(d_lru, C_LRU)
    hpb = C // HEAD_DIM
    wa_bd = _block_diag(lru_wa, hpb).astype(BF16)
    wx_bd = _block_diag(lru_wx, hpb).astype(BF16)
    cb, ba, bx, lam = (a.reshape(1, d_lru) for a in (lru_conv_b, lru_ba, lru_bx, lru_lambda))

    h1 = _rms_fwd(xs, g_mix, "rms_mix")
    k8 = W_up.shape[1] // 8
    wide = 2 * max(n_in, n_up)
    p, ((W_o8,), (W_up,)) = _mm_nn(
        h1, W_in, out_dtype=BF16, name="mm_in", tn=wide,
        tasks=[_gather_task(W_o8, ici=ALL_ROWS), _gather_task(W_up, ici=(0, 4 * k8), pair_n=n_up)])
    hs, yl_pre, ((W_o8,), (W_up,)) = _lru_fwd(
        p, cw_lru, cb, wa_bd, ba, wx_bd, bx, lam, name="lru_fwd",
        tasks=[_gather_task(W_o8, fwd=ALL_ROWS),
               _gather_task(W_up, ici=(4 * k8, 2 * k8), fwd=(0, 4 * k8), pair_n=n_up)])
    ys_pre = _sc_fwd(p, cw_sc, d=d_sc, name="sc_fwd")
    y_lru, ((W_up,),) = _mm_small(
        "nn", yl_pre, None, W_lo, name="mm_lru_out",
        tasks=[_gather_task(W_up, ici=(6 * k8, k8), fwd=(4 * k8, 2 * k8), pair_n=n_up)])
    y_sc, ((W_up,),) = _mm_small(
        "nn", ys_pre, None, W_so, name="mm_sc_out",
        tasks=[_gather_task(W_up, ici=(7 * k8, k8), fwd=(6 * k8, k8), pair_n=n_up)])
    gate0 = 2 * d_lru + 3 * d_sc
    merged = _merge_fwd(p, y_lru, y_sc, col0=gate0, name="merge_fwd")
    W_o = W_o8.reshape(1, D, D)
    x1, ((W_up,),) = _mm_nn(merged, W_o, out_dtype=F32, residual=xs, name="mm_o",
                            tasks=[_gather_task(W_up, fwd=(7 * k8, k8), pair_n=n_up)])
    h2 = _rms_fwd(x1, g_ffn, "rms_ffn")
    uu, ((W_dn8,),) = _mm_nn(h2, W_up, out_dtype=BF16, name="mm_up", tn=wide,
                             tasks=[_gather_task(W_dn8, ici=ALL_ROWS)])
    act, ((W_dn8,),) = _ffn_act_fwd(uu, cw_ffn, name="ffn_act_fwd", tasks=[_gather_task(W_dn8, fwd=ALL_ROWS)])
    W_dn = W_dn8.reshape(1, F, D)
    x2 = _mm_nn(act, W_dn, out_dtype=F32, residual=x1, name="mm_down", tn=1024, tk=F)
    dx2, dx2b, loss_part, dg_final = _loss_head(x2, g_final, tgt, "loss_head")

    def pack_rows(arrs):
        flat = jnp.concatenate([a.reshape(-1) for a in arrs])
        rows = -(-flat.shape[0] // (SUB * LANES)) * SUB
        return jnp.pad(flat, (0, rows * LANES - flat.shape[0])).reshape(rows, LANES)

    def unpack_rows(pack, arrs):
        flat, out, o = pack.reshape(-1), [], 0
        for a in arrs:
            out.append(flat[o:o + a.size].reshape(a.shape))
            o += a.size
        return out

    dact = _mm_nt(dx2b, W_dn, out_dtype=BF16, name="mm_down_dx", tm=512, tko=F // 2, tn=D)
    gW_dn = _mm_tn(act, dx2b, 1, out_dtype=BF16, name="mm_down_dw", tk=1408, tt=2048).reshape(N_DEV, F // N_DEV, D)
    duu, dcw_ffn_g, dcw_ffn_v = _ffn_act_bwd(uu, dact, cw_ffn, name="ffn_act_bwd")
    dh2, ((land_dn,),) = _mm_nt(duu, W_up, out_dtype=BF16, name="mm_up_dx", tn=wide, tasks=[_swap_task(gW_dn)])
    parts_dn = _add_halves(gW_dn, land_dn, place, "rs_add_ffn_w_down")
    gW_up, ((mine_dn,),) = _mm_tn(h2, duu, N_CHIP, out_dtype=BF16, name="mm_up_dw", tk=512, tn=wide, tt=2048,
                                  tasks=[_exchange_task(*parts_dn)])
    dx1, dx1b, dg_ffn = _rms_bwd(x1, g_ffn, dh2, dx2, "rms_ffn_bwd")
    dmerged, ((land_up,),) = _mm_nt(dx1b, W_o, out_dtype=BF16, name="mm_o_dx", tn=D,
                                    tasks=[_swap_task(gW_up, pair_n=n_up)])
    parts_up = _add_halves(gW_up, land_up, place, "rs_add_ffn_w_up", pair_n=n_up)
    gW_o = _mm_tn(merged, dx1b, 1, out_dtype=BF16, name="mm_o_dw", tt=2048).reshape(N_DEV, D // N_DEV, D)
    dp, dy_lru, dy_sc = _merge_bwd(p, y_lru, y_sc, dmerged, col0=gate0, name="merge_bwd")
    dyl_pre, ((land_o,),) = _mm_small("nt", None, dy_lru, W_lo, name="mm_lru_out_dx", tasks=[_swap_task(gW_o)])
    parts_o = _add_halves(gW_o, land_o, place, "rs_add_w_o")
    gW_lo = _mm_small("tn", yl_pre, dy_lru, W_lo, name="mm_lru_out_dw")
    dys_pre, ((land_lo,),) = _mm_small("nt", None, dy_sc, W_so, name="mm_sc_out_dx", tasks=[_swap_task(gW_lo)])
    parts_lo = _add_halves(gW_lo, land_lo, place, "rs_add_lru_w_out")
    gW_so = _mm_small("tn", ys_pre, dy_sc, W_so, name="mm_sc_out_dw")
    dp, dcw_sc = _sc_bwd(p, dys_pre, dp, cw_sc, d=d_sc, name="sc_bwd")
    r_up = parts_up[0].shape[1] // 2
    dp, dcw_lru, dcb, dwa_bd, dba, dwx_bd, dbx, dlam, ((land_up,), (mine_o,), (mine_lo,), (land_so,)) = _lru_bwd(
        p, hs, dyl_pre, dp, cw_lru, cb, wa_bd, ba, wx_bd, bx, lam, name="lru_bwd",
        tasks=[_exchange_task(*parts_up, rows=(0, r_up)), _exchange_task(*parts_o), _exchange_task(*parts_lo),
               _swap_task(gW_so)])
    parts_so = _add_halves(gW_so, land_so, place, "rs_add_sc_w_out")

    dwa = _diag_blocks(dwa_bd, hpb, HEAD_DIM)
    dwx = _diag_blocks(dwx_bd, hpb, HEAD_DIM)
    dcw_ffn = jnp.concatenate([dcw_ffn_g, dcw_ffn_v], axis=1)
    small_full = [dcw_lru, dcb, dwa, dba, dwx, dbx, dlam, dcw_sc, dg_ffn, dcw_ffn, dg_final]
    gW_in, ((mine_up,), (mine_so,), (packs,)) = _mm_tn(
        h1, dp, N_CHIP, out_dtype=BF16, name="mm_in_dw", tk=512, tn=wide, tt=2048,
        tasks=[_exchange_task(parts_up[0], land_up, rows=(r_up, r_up)), _exchange_task(*parts_so),
               _bcast_task(pack_rows(small_full))])
    land_in = _swap_halves(gW_in, "rs_swap_w_in", pair_n=n_in)
    parts_in = _add_halves(gW_in, land_in, place, "rs_add_w_in", pair_n=n_in)
    dh1, ((mine_in,),) = _mm_nt(dp, W_in, out_dtype=BF16, name="mm_in_dx", tn=wide,
                                tasks=[_exchange_task(*parts_in)])
    grad_x, _, dg_mix = _rms_bwd(xs, g_mix, dh1, dx1, "rms_mix_bwd")

    mine = [mine_in, mine_lo, mine_so, mine_o, mine_up, mine_dn]
    big_m = [m_w_in, m_lru_w_out, m_sc_w_out, m_w_o, m_ffn_w_up, m_ffn_w_down]
    big_v = [v_w_in, v_lru_w_out, v_sc_w_out, v_w_o, v_ffn_w_up, v_ffn_w_down]
    big_out = {nm: _adamw_big(pt, w, m, v, "adamw_" + nm)
               for nm, pt, w, m, v in zip(big_names, mine, big, big_m, big_v)}

    (scw_lru, scb, swa, sba, swx, sbx, slam, scw_sc, sg_ffn, scw_ffn, sg_final) = unpack_rows(
        _sum_packs(packs, "sum_small"), small_full)
    (sg_mix,) = unpack_rows(_all_reduce_small(pack_rows([dg_mix]), "all_reduce_g_mix"), [dg_mix])

    def my_cols(a):
        n = a.shape[1] // N_DEV
        return lax.dynamic_slice_in_dim(a, me * n, n, axis=1)

    small_names = ["g_mix", "lru_conv_w", "lru_conv_b", "lru_wa", "lru_ba", "lru_wx", "lru_bx", "lru_lambda",
                   "sc_conv_w", "g_ffn", "ffn_conv_w", "g_final"]
    small_w = [g_mix, lru_conv_w, lru_conv_b, lru_wa, lru_ba, lru_wx, lru_bx, lru_lambda, sc_conv_w, g_ffn,
               ffn_conv_w, g_final]
    small_m = [m_g_mix, m_lru_conv_w, m_lru_conv_b, m_lru_wa, m_lru_ba, m_lru_wx, m_lru_bx, m_lru_lambda,
               m_sc_conv_w, m_g_ffn, m_ffn_conv_w, m_g_final]
    small_v = [v_g_mix, v_lru_conv_w, v_lru_conv_b, v_lru_wa, v_lru_ba, v_lru_wx, v_lru_bx, v_lru_lambda,
               v_sc_conv_w, v_g_ffn, v_ffn_conv_w, v_g_final]
    small_g = [sg_mix.reshape(D), my_cols(scw_lru), scb.reshape(d_lru), swa, sba.reshape(d_lru), swx,
               sbx.reshape(d_lru), slam.reshape(d_lru), my_cols(scw_sc), sg_ffn.reshape(D), my_cols(scw_ffn),
               sg_final.reshape(D)]
    sd, snm, snv = _adamw_small([_as_rows(a) for a in small_w], [_as_rows(a) for a in small_g],
                                [_as_rows(a) for a in small_m], [_as_rows(a) for a in small_v], "adamw_small")
    small_out = {nm: (g, d.reshape(w.shape), nm_.reshape(w.shape), nv_.reshape(w.shape))
                 for nm, w, g, d, nm_, nv_ in zip(small_names, small_w, small_g, sd, snm, snv)}

    loss = lax.psum(loss_part[0, 0], AXES)
    order = ["g_mix", "w_in", "lru_conv_w", "lru_conv_b", "lru_wa", "lru_ba", "lru_wx", "lru_bx", "lru_lambda",
             "lru_w_out", "sc_conv_w", "sc_w_out", "w_o", "g_ffn", "ffn_w_up", "ffn_conv_w", "ffn_w_down", "g_final"]
    res = {**big_out, **small_out}
    return (loss, grad_x.reshape(x.shape),
            *[res[nm][0] for nm in order], *[res[nm][1] for nm in order],
            *[res[nm][2] for nm in order], *[res[nm][3] for nm in order])
```

```python
import functools
import math

import jax
import jax.numpy as jnp
from jax import lax
from jax.experimental import pallas as pl
from jax.experimental.pallas import tpu as pltpu

F32, BF16 = jnp.float32, jnp.bfloat16
MESH = pl.DeviceIdType.MESH
N_DEV = 8
N_CHIP = 4
AXES = ("x", "y", "c")

EPS = 1e-6
LRU_C = 8.0
HEAD_DIM = 64
ADAM_LR, ADAM_B1, ADAM_B2, ADAM_EPS, ADAM_WD, ADAM_STEP = 0.001, 0.9, 0.999, 1e-08, 0.01, 10

VMEM_LIMIT = 48 * 1024 * 1024
LANES = 128
SUB = 8
HALO = 16
TB = 512
C_LRU = 256
C_EW = 512
TM, TN, TK = 512, 1536, 2048


def _tile(n, pref, align=LANES):
    best = None
    for d in range(align, min(n, pref) + 1, align):
        if n % d == 0:
            best = d
    return best or n


def _cparams(sem=None, vmem=VMEM_LIMIT):
    kw = dict(vmem_limit_bytes=vmem)
    if sem is not None:
        kw["dimension_semantics"] = sem
    return pltpu.CompilerParams(**kw)


def _S(shape, dtype):
    return jax.ShapeDtypeStruct(shape, dtype)


ANY = pl.BlockSpec(memory_space=pl.ANY)
VMEM_SPEC = pl.BlockSpec(memory_space=pltpu.VMEM)


class _Task:
    def __init__(self, arrays, aliased, start, wait, fresh=(), nsem=3):
        self.arrays, self.aliased, self.start, self.wait = arrays, aliased, start, wait
        self.fresh, self.nsem = list(fresh), nsem


def _call(name, grid, compute, in_specs, args, out_shape, out_specs, scratch, tasks=(), own_aliases=None):
    n_in, n_out, n_scr = len(args), len(out_shape), len(scratch)
    x_in, x_out, aliases, where = [], [], dict(own_aliases or {}), []
    for t in tasks:
        places = []
        for k, arr in enumerate(t.arrays):
            if k in t.aliased:
                aliases[n_in + len(x_in)] = n_out + len(x_out)
                places.append(("out", len(x_out)))
                x_out.append(_S(arr.shape, arr.dtype))
            else:
                places.append(("in", len(x_in)))
            x_in.append(arr)
        for shp in t.fresh:
            places.append(("out", len(x_out)))
            x_out.append(shp)
        where.append(places)
    n_xi, n_xo = len(x_in), len(x_out)

    def body(*refs):
        ins, xi = refs[:n_in], refs[n_in:n_in + n_xi]
        o0 = n_in + n_xi
        outs, xo = refs[o0:o0 + n_out], refs[o0 + n_out:o0 + n_out + n_xo]
        s0 = o0 + n_out + n_xo
        scr, sems = refs[s0:s0 + n_scr], refs[s0 + n_scr:]
        ids = [pl.program_id(a) for a in range(len(grid))]

        def task_refs(ti):
            return [xo[i] if kind == "out" else xi[i] for kind, i in where[ti]]

        if tasks:
            first = functools.reduce(jnp.logical_and, [i == 0 for i in ids])

            @pl.when(first)
            def _():
                for ti, t in enumerate(tasks):
                    t.start(task_refs(ti), *sems[3 * ti:3 * ti + 3])

        compute(*ins, *outs, *scr)
        if tasks:
            last = functools.reduce(jnp.logical_and, [i == g - 1 for i, g in zip(ids, grid)])

            @pl.when(last)
            def _():
                for ti, t in enumerate(tasks):
                    t.wait(task_refs(ti), *sems[3 * ti:3 * ti + 3])

    sem_shapes = []
    for t in tasks:
        sem_shapes += [pltpu.SemaphoreType.DMA((t.nsem,)), pltpu.SemaphoreType.DMA((t.nsem,)),
                       pltpu.SemaphoreType.DMA((1,))]
    res = pl.pallas_call(
        body, name=name, grid=grid,
        in_specs=list(in_specs) + [ANY] * n_xi,
        out_specs=tuple(out_specs) + (ANY,) * n_xo,
        out_shape=tuple(out_shape) + tuple(x_out),
        scratch_shapes=list(scratch) + sem_shapes,
        input_output_aliases=aliases,
        compiler_params=_cparams(("arbitrary",) * len(grid)),
    )(*args, *x_in)
    outs, passed, o = res[:n_out], [], n_out
    for places in where:
        k = sum(1 for kind, _ in places if kind == "out")
        passed.append(list(res[o:o + k]))
        o += k
    return outs, passed


def _mm_nn(a, w3, *, out_dtype, name, residual=None, tm=TM, tn=TN, tk=TK, tasks=()):
    M, K = a.shape
    G, _, n = w3.shape
    tm, tn, tk = _tile(M, tm, SUB), _tile(n, tn), _tile(K, tk)
    nj, nk = n // tn, K // tk

    def compute(*refs):
        if residual is None:
            a_ref, w_ref, o_ref = refs[:3]
            r_ref = None
        else:
            a_ref, w_ref, r_ref, o_ref = refs[:4]

        def finish(r):
            if r_ref is not None:
                r = r + r_ref[...]
            o_ref[...] = r.astype(o_ref.dtype)

        if nk == 1:
            finish(jnp.dot(a_ref[...], w_ref[...], preferred_element_type=F32))
            return
        acc = refs[-1]
        k = pl.program_id(3)

        @pl.when(k == 0)
        def _():
            acc[...] = jnp.zeros_like(acc)

        acc[...] += jnp.dot(a_ref[...], w_ref[...], preferred_element_type=F32)

        @pl.when(k == nk - 1)
        def _():
            finish(acc[...])

    in_specs = [pl.BlockSpec((tm, tk), lambda g, j, i, k: (i, k)),
                pl.BlockSpec((None, tk, tn), lambda g, j, i, k: (g, k, j))]
    args = [a, w3]
    if residual is not None:
        in_specs.append(pl.BlockSpec((tm, tn), lambda g, j, i, k: (i, g * nj + j)))
        args.append(residual)
    outs, passed = _call(
        name, (G, nj, M // tm, nk), compute, in_specs, args, [_S((M, G * n), out_dtype)],
        [pl.BlockSpec((tm, tn), lambda g, j, i, k: (i, g * nj + j))],
        [] if nk == 1 else [pltpu.VMEM((tm, tn), F32)], tasks)
    return (outs[0], passed) if tasks else outs[0]


def _mm_nt(dy, w3, *, out_dtype, name, tm=1024, tko=1024, tn=TN, tasks=()):
    M, _ = dy.shape
    G, K, n = w3.shape
    tm, tko, tn = _tile(M, tm, SUB), _tile(K, tko), _tile(n, tn)
    nj = n // tn
    nr = G * nj

    def compute(dy_ref, w_ref, o_ref, *scr):
        part = lax.dot_general(dy_ref[...], w_ref[...], (((1,), (1,)), ((), ())), preferred_element_type=F32)
        if nr == 1:
            o_ref[...] = part.astype(o_ref.dtype)
            return
        (acc,) = scr
        r = pl.program_id(2)

        @pl.when(r == 0)
        def _():
            acc[...] = jnp.zeros_like(acc)

        acc[...] += part

        @pl.when(r == nr - 1)
        def _():
            o_ref[...] = acc[...].astype(o_ref.dtype)

    outs, passed = _call(
        name, (K // tko, M // tm, nr), compute,
        [pl.BlockSpec((tm, tn), lambda ko, i, r: (i, r)),
         pl.BlockSpec((None, tko, tn), lambda ko, i, r: (r // nj, ko, r % nj))],
        [dy, w3], [_S((M, K), out_dtype)], [pl.BlockSpec((tm, tko), lambda ko, i, r: (i, ko))],
        [] if nr == 1 else [pltpu.VMEM((tm, tko), F32)], tasks)
    return (outs[0], passed) if tasks else outs[0]


def _mm_tn(a, dy, G, *, out_dtype, name, tk=1024, tn=TN, tt=1024, tasks=()):
    M, K = a.shape
    n = dy.shape[1] // G
    tk, tn, tt = _tile(K, tk), _tile(n, tn), _tile(M, tt, SUB)
    nj, nt = n // tn, M // tt

    def compute(a_ref, dy_ref, o_ref, acc):
        t = pl.program_id(3)

        @pl.when(t == 0)
        def _():
            acc[...] = jnp.zeros_like(acc)

        acc[...] += lax.dot_general(a_ref[...], dy_ref[...], (((0,), (0,)), ((), ())),
                                    preferred_element_type=F32)

        @pl.when(t == nt - 1)
        def _():
            o_ref[...] = acc[...].astype(o_ref.dtype)

    outs, passed = _call(
        name, (G, nj, K // tk, nt), compute,
        [pl.BlockSpec((tt, tk), lambda g, j, k, t: (t, k)),
         pl.BlockSpec((tt, tn), lambda g, j, k, t: (t, g * nj + j))],
        [a, dy], [_S((G, K, n), out_dtype)], [pl.BlockSpec((None, tk, tn), lambda g, j, k, t: (g, k, j))],
        [pltpu.VMEM((tk, tn), F32)], tasks)
    return (outs[0], passed) if tasks else outs[0]


def _mm_small(kind, a, b, w3, *, name, tm=1024, tasks=()):
    G, K, n = w3.shape
    M = (a if a is not None else b).shape[0]
    tm = _tile(M, tm, HALO)
    nt = M // tm
    w_spec = pl.BlockSpec((G, K, n), lambda i: (0, 0, 0))
    a_spec = pl.BlockSpec((tm, K), lambda i: (i, 0))
    b_spec = pl.BlockSpec((tm, G * n), lambda i: (i, 0))
    cols = lambda g: slice(g * n, (g + 1) * n)
    if kind == "nn":
        def compute(a_ref, w_ref, o_ref):
            av = a_ref[...]
            for g in range(G):
                o_ref[:, cols(g)] = jnp.dot(av, w_ref[g], preferred_element_type=F32).astype(o_ref.dtype)

        outs, passed = _call(name, (nt,), compute, [a_spec, w_spec], [a, w3], [_S((M, G * n), BF16)], [b_spec], [], tasks)
    elif kind == "nt":
        def compute(b_ref, w_ref, o_ref):
            acc = None
            for g in range(G):
                part = lax.dot_general(b_ref[:, cols(g)], w_ref[g], (((1,), (1,)), ((), ())),
                                       preferred_element_type=F32)
                acc = part if acc is None else acc + part
            o_ref[...] = acc.astype(o_ref.dtype)

        outs, passed = _call(name, (nt,), compute, [b_spec, w_spec], [b, w3], [_S((M, K), BF16)], [a_spec], [], tasks)
    else:
        def compute(a_ref, b_ref, o_ref, acc):
            i = pl.program_id(0)

            @pl.when(i == 0)
            def _():
                acc[...] = jnp.zeros_like(acc)

            at = a_ref[...].T
            for g in range(G):
                acc[g] += jnp.dot(at, b_ref[:, cols(g)], preferred_element_type=F32)

            @pl.when(i == nt - 1)
            def _():
                o_ref[...] = acc[...].astype(o_ref.dtype)

        outs, passed = _call(name, (nt,), compute, [a_spec, b_spec], [a, b], [_S((G, K, n), BF16)], [w_spec],
                             [pltpu.VMEM((G, K, n), F32)], tasks)
    return (outs[0], passed) if tasks else outs[0]


def _cast_into_slot(w, place, name, paired=False):
    R, C = w.shape
    tr = _tile(R, 512, HALO)

    def body(s_ref, w_ref, o_ref):
        del s_ref
        o_ref[...] = w_ref[...].astype(BF16)

    if paired:
        shape, out_map = (N_CHIP, R, 2 * C), lambda i, s: (s[1], i, s[0])
    else:
        shape, out_map = (N_DEV, R, C), lambda i, s: (s[2], i, 0)
    return pl.pallas_call(
        body, name=name, out_shape=_S(shape, BF16),
        grid_spec=pltpu.PrefetchScalarGridSpec(
            num_scalar_prefetch=1, grid=(R // tr,),
            in_specs=[pl.BlockSpec((tr, C), lambda i, s: (i, 0))],
            out_specs=pl.BlockSpec((None, tr, C), out_map)),
        compiler_params=_cparams(("parallel",)),
    )(place, w)


def _down(cur, prev8, j):
    return pltpu.roll(jnp.concatenate([prev8, cur], axis=0), j, 0)[SUB:, :]


def _up(cur, next8, j):
    n = cur.shape[0] + SUB
    return pltpu.roll(jnp.concatenate([cur, next8], axis=0), n - j, 0)[:cur.shape[0], :]


def _shifted_down(x, prev8, n):
    full = jnp.concatenate([prev8, x], axis=0)
    return [x] + [pltpu.roll(full, s, 0)[SUB:, :] for s in range(1, n)]


def _shifted_up(x, next8, n):
    m = x.shape[0] + SUB
    full = jnp.concatenate([x, next8], axis=0)
    return [x] + [pltpu.roll(full, m - s, 0)[:x.shape[0], :] for s in range(1, n)]


def _taps(sh, w_ref):
    kw = w_ref.shape[0]
    y = sh[0] * w_ref[pl.ds(kw - 1, 1), :]
    for k in range(kw - 1):
        y = y + sh[kw - 1 - k] * w_ref[pl.ds(k, 1), :]
    return y


def _conv(x, prev8, w_ref):
    return _taps(_shifted_down(x, prev8, w_ref.shape[0]), w_ref)


def _conv_t(dy, next8, w_ref):
    return _taps(_shifted_up(dy, next8, w_ref.shape[0]), w_ref)


def _conv_dw(dw_ref, dy, x, prev8, first):
    kw = dw_ref.shape[0]

    @pl.when(first)
    def _():
        dw_ref[...] = jnp.zeros_like(dw_ref)

    for k in range(kw):
        xs = x if k == kw - 1 else _down(x, prev8, kw - 1 - k)
        dw_ref[pl.ds(k, 1), :] += jnp.sum(dy * xs, axis=0, keepdims=True)


def _acc(ref, val, first):
    @pl.when(first)
    def _():
        ref[...] = jnp.zeros_like(ref)

    ref[...] += val


def _acc_row(ref, val, first):
    _acc(ref, jnp.sum(val, axis=0, keepdims=True), first)


def _prev8(h_ref, t):
    return jnp.where(t > 0, h_ref[...].astype(F32)[HALO - SUB:, :], 0.0)


def _next8(h_ref, is_last):
    return jnp.where(is_last, 0.0, h_ref[...].astype(F32)[:SUB, :])


_GELU_K0 = math.sqrt(2.0 / math.pi)
_GELU_K1 = 0.044715


def _gelu_and_grad(x):
    x2 = x * x
    th = jnp.tanh(_GELU_K0 * x * (1.0 + _GELU_K1 * x2))
    g = 0.5 * x * (1.0 + th)
    dg = 0.5 * (1.0 + th) + 0.5 * x * (1.0 - th * th) * (_GELU_K0 * (1.0 + 3.0 * _GELU_K1 * x2))
    return g, dg


def _neg_expm1(z):
    series = -z * (1.0 + z * (0.5 + z * (1.0 / 6.0 + z * (1.0 / 24.0))))
    return jnp.where(z > -0.03, series, 1.0 - jnp.exp(z))


def _store_staged(stages, dst_hbm, sems, step, n_steps, where):
    def copies(s, slot):
        return [pltpu.make_async_copy(
            st.at[slot], dst_hbm.at[pl.ds(r0, st.shape[1]), pl.ds(c0, st.shape[2])], sems.at[slot, k])
            for k, (st, (r0, c0)) in enumerate(zip(stages, where(s)))]

    slot = step % 2

    @pl.when(step > 0)
    def _():
        for cp in copies(step - 1, 1 - slot):
            cp.wait()

    for cp in copies(step, slot):
        cp.start()

    @pl.when(step == n_steps - 1)
    def _():
        for cp in copies(step, slot):
            cp.wait()


def _halo_prev_map(hb, col_fn):
    return lambda c, t: (jnp.maximum(t * hb - 1, 0), col_fn(c))


def _rms_fwd(x, g, name):
    T, D = x.shape
    tb = _tile(T, TB, SUB)

    def body(x_ref, g_ref, o_ref):
        xv = x_ref[...]
        rstd = lax.rsqrt(jnp.mean(xv * xv, axis=-1, keepdims=True) + EPS)
        o_ref[...] = (xv * rstd * g_ref[...]).astype(BF16)

    return pl.pallas_call(
        body, name=name, out_shape=_S((T, D), BF16), grid=(T // tb,),
        in_specs=[pl.BlockSpec((tb, D), lambda i: (i, 0)), pl.BlockSpec((1, D), lambda i: (0, 0))],
        out_specs=pl.BlockSpec((tb, D), lambda i: (i, 0)),
        compiler_params=_cparams(("parallel",)),
    )(x, g.reshape(1, D))


def _rms_bwd(x, g, dh, dres, name):
    T, D = x.shape
    tb = _tile(T, 256, SUB)

    def body(x_ref, g_ref, dh_ref, dr_ref, dx_ref, dxb_ref, dg_ref):
        i = pl.program_id(0)
        xv = x_ref[...]
        rstd = lax.rsqrt(jnp.mean(xv * xv, axis=-1, keepdims=True) + EPS)
        xn = xv * rstd
        dhv = dh_ref[...].astype(F32)
        _acc_row(dg_ref, dhv * xn, i == 0)
        dxn = dhv * g_ref[...]
        dx = dr_ref[...] + rstd * (dxn - xn * jnp.mean(dxn * xn, axis=-1, keepdims=True))
        dx_ref[...] = dx
        dxb_ref[...] = dx.astype(BF16)

    blk = pl.BlockSpec((tb, D), lambda i: (i, 0))
    vec = pl.BlockSpec((1, D), lambda i: (0, 0))
    return pl.pallas_call(
        body, name=name, out_shape=(_S((T, D), F32), _S((T, D), BF16), _S((1, D), F32)),
        grid=(T // tb,), in_specs=[blk, vec, blk, blk], out_specs=(blk, blk, vec),
        compiler_params=_cparams(("arbitrary",)),
    )(x, g.reshape(1, D), dh, dres)


def _loss_head(x2, g, target, name):
    T, D = x2.shape
    tb = _tile(T, 256, SUB)

    def body(x_ref, g_ref, t_ref, dx_ref, dxb_ref, loss_ref, dg_ref):
        i = pl.program_id(0)
        xv = x_ref[...]
        rstd = lax.rsqrt(jnp.mean(xv * xv, axis=-1, keepdims=True) + EPS)
        xn = xv * rstd
        err = xn * g_ref[...] - t_ref[...]
        part = 0.5 * jnp.sum(jnp.mean(err * err, axis=-1, keepdims=True), axis=0, keepdims=True)
        part = jnp.broadcast_to(part, (1, LANES))
        _acc(loss_ref, part, i == 0)
        dy = err * (1.0 / D)
        _acc_row(dg_ref, dy * xn, i == 0)
        dxn = dy * g_ref[...]
        dx = rstd * (dxn - xn * jnp.mean(dxn * xn, axis=-1, keepdims=True))
        dx_ref[...] = dx
        dxb_ref[...] = dx.astype(BF16)

    blk = pl.BlockSpec((tb, D), lambda i: (i, 0))
    vec = pl.BlockSpec((1, D), lambda i: (0, 0))
    return pl.pallas_call(
        body, name=name,
        out_shape=(_S((T, D), F32), _S((T, D), BF16), _S((1, LANES), F32), _S((1, D), F32)),
        grid=(T // tb,), in_specs=[blk, vec, blk],
        out_specs=(blk, blk, pl.BlockSpec((1, LANES), lambda i: (0, 0)), vec),
        compiler_params=_cparams(("arbitrary",)),
    )(x2, g.reshape(1, D), target)


def _lru_gates(xc, wa_ref, ba_ref, wx_ref, bx_ref, lam_ref):
    xcb = xc.astype(BF16)
    r = jax.nn.sigmoid(jnp.dot(xcb, wa_ref[...], preferred_element_type=F32) + ba_ref[...])
    i = jax.nn.sigmoid(jnp.dot(xcb, wx_ref[...], preferred_element_type=F32) + bx_ref[...])
    sp = jax.nn.softplus(-lam_ref[...])
    log_a = (-LRU_C * sp) * r
    a = jnp.exp(log_a)
    s = jnp.sqrt(_neg_expm1(2.0 * log_a))
    return xcb, r, i, a, s


def _lru_fwd(p, conv_w, conv_b, wa_bd, ba, wx_bd, bx, lam, *, name, tasks=()):
    T = p.shape[0]
    d = lam.shape[-1]
    C = _tile(d, C_LRU)
    nC = d // C
    tb = _tile(T, TB, HALO)
    nT, hb, nt = T // tb, tb // HALO, tb // SUB

    def body(x_ref, xh_ref, g_ref, cw_ref, cb_ref, wa_ref, ba_ref, wx_ref, bx_ref, lam_ref,
             hs_ref, y_ref, a_s, u_s, h_s):
        t = pl.program_id(1)

        @pl.when(t == 0)
        def _():
            h_s[...] = jnp.zeros_like(h_s)

        x = x_ref[...].astype(F32)
        xc = _conv(x, _prev8(xh_ref, t), cw_ref) + cb_ref[...]
        _, r, i, a, s = _lru_gates(xc, wa_ref, ba_ref, wx_ref, bx_ref, lam_ref)
        a_s[...] = a
        u_s[...] = s * (i * xc)
        row = lax.broadcasted_iota(jnp.int32, (SUB, C), 0)

        def step(k, h):
            o = pl.multiple_of(k * SUB, SUB)
            A = a_s[pl.ds(o, SUB), :]
            B = u_s[pl.ds(o, SUB), :]
            for sh in (1, 2, 4):
                m = row >= sh
                Ap = pltpu.roll(A, sh, 0)
                Bp = pltpu.roll(B, sh, 0)
                B = jnp.where(m, A * Bp + B, B)
                A = jnp.where(m, A * Ap, A)
            hs = A * h + B
            hs_ref[pl.ds(o, SUB), :] = hs
            return jnp.broadcast_to(hs[SUB - 1:SUB, :], (SUB, C))

        h_s[...] = lax.fori_loop(0, nt, step, h_s[...])
        gel, _ = _gelu_and_grad(g_ref[...].astype(F32))
        y_ref[...] = (gel * hs_ref[...]).astype(BF16)

    vec = pl.BlockSpec((1, C), lambda c, t: (0, c))
    sq = pl.BlockSpec((None, C, C), lambda c, t: (c, 0, 0))
    outs, passed = _call(
        name, (nC, nT), body,
        [pl.BlockSpec((tb, C), lambda c, t: (t, c)),
         pl.BlockSpec((HALO, C), _halo_prev_map(hb, lambda c: c)),
         pl.BlockSpec((tb, C), lambda c, t: (t, nC + c)),
         pl.BlockSpec((conv_w.shape[0], C), lambda c, t: (0, c)),
         vec, sq, vec, sq, vec, vec],
        [p, p, p, conv_w, conv_b, wa_bd, ba, wx_bd, bx, lam],
        [_S((T, d), F32), _S((T, d), BF16)],
        [pl.BlockSpec((tb, C), lambda c, t: (t, c)), pl.BlockSpec((tb, C), lambda c, t: (t, c))],
        [pltpu.VMEM((tb, C), F32), pltpu.VMEM((tb, C), F32), pltpu.VMEM((SUB, C), F32)], tasks)
    return (*outs, passed) if tasks else outs


def _lru_bwd(p, hs, dyl, dp, conv_w, conv_b, wa_bd, ba, wx_bd, bx, lam, *, name, tasks=()):
    T = p.shape[0]
    d = lam.shape[-1]
    C = _tile(d, C_LRU)
    nC = d // C
    tb = _tile(T, TB, HALO)
    nT, hb, nt = T // tb, tb // HALO, tb // SUB
    kw = conv_w.shape[0]

    def body(x_ref, xh_ref, g_ref, hs_ref, hh_ref, dy_ref, cw_ref, cb_ref, wa_ref, ba_ref, wx_ref, bx_ref,
             lam_ref, dp_in, dp_ref, dcw_ref, dcb_ref, dwa_ref, dba_ref, dwx_ref, dbx_ref, dlam_ref,
             b_s, g_s, dh_s, an_s, dhn_s, dxn_s, st_x, st_g, sems):
        del dp_in
        c = pl.program_id(0)
        tr = pl.program_id(1)
        t = nT - 1 - tr
        first = tr == 0

        @pl.when(first)
        def _():
            an_s[...] = jnp.zeros_like(an_s)
            dhn_s[...] = jnp.zeros_like(dhn_s)
            dxn_s[...] = jnp.zeros_like(dxn_s)

        x = x_ref[...].astype(F32)
        xprev = _prev8(xh_ref, t)
        xc = _conv(x, xprev, cw_ref) + cb_ref[...]
        xcb, r, i, a, s = _lru_gates(xc, wa_ref, ba_ref, wx_ref, bx_ref, lam_ref)
        hsv = hs_ref[...]
        dy = dy_ref[...].astype(F32)
        gel, dgel = _gelu_and_grad(g_ref[...].astype(F32))
        step_no = c * nT + tr
        slot = step_no % 2
        st_g[slot] = (dy * hsv * dgel).astype(BF16)

        b_s[...] = _up(a, an_s[...], 1)
        g_s[...] = dy * gel
        row = lax.broadcasted_iota(jnp.int32, (SUB, C), 0)

        def step(k, carry):
            o = pl.multiple_of((nt - 1 - k) * SUB, SUB)
            B = b_s[pl.ds(o, SUB), :]
            G = g_s[pl.ds(o, SUB), :]
            for sh in (1, 2, 4):
                m = row < SUB - sh
                Bn = pltpu.roll(B, SUB - sh, 0)
                Gn = pltpu.roll(G, SUB - sh, 0)
                G = jnp.where(m, B * Gn + G, G)
                B = jnp.where(m, B * Bn, B)
            dh = B * carry + G
            dh_s[pl.ds(o, SUB), :] = dh
            return jnp.broadcast_to(dh[0:1, :], (SUB, C))

        dhn_s[...] = lax.fori_loop(0, nt, step, dhn_s[...])
        an_s[...] = a[:SUB, :]
        dh = dh_s[...]

        hprev = _down(hsv, jnp.where(t > 0, hh_ref[...][HALO - SUB:, :], 0.0), 1)
        d_a = dh * hprev
        ixc = i * xc
        d_s = dh * ixc
        d_i = dh * s * xc
        d_xc = dh * s * i
        d_l = d_a * a - d_s * (a * a) / s
        sp = jax.nn.softplus(-lam_ref[...])
        _acc_row(dlam_ref, d_l * r * (LRU_C * jax.nn.sigmoid(-lam_ref[...])), first)
        d_zr = (d_l * (-LRU_C * sp)) * r * (1.0 - r)
        d_zi = d_i * i * (1.0 - i)
        _acc_row(dba_ref, d_zr, first)
        _acc_row(dbx_ref, d_zi, first)
        d_zrb = d_zr.astype(BF16)
        d_zib = d_zi.astype(BF16)
        tn_dims = (((0,), (0,)), ((), ()))
        nt_dims = (((1,), (1,)), ((), ()))
        gwa = lax.dot_general(xcb, d_zrb, tn_dims, preferred_element_type=F32)
        gwx = lax.dot_general(xcb, d_zib, tn_dims, preferred_element_type=F32)
        _acc(dwa_ref, gwa, first)
        _acc(dwx_ref, gwx, first)
        d_xc = (d_xc + lax.dot_general(d_zrb, wa_ref[...], nt_dims, preferred_element_type=F32)
                + lax.dot_general(d_zib, wx_ref[...], nt_dims, preferred_element_type=F32))
        _acc_row(dcb_ref, d_xc, first)
        _conv_dw(dcw_ref, d_xc, x, xprev, first)
        st_x[slot] = _conv_t(d_xc, dxn_s[...], cw_ref).astype(BF16)
        dxn_s[...] = d_xc[:SUB, :]

        def where(s):
            row0, col0 = (nT - 1 - s % nT) * tb, (s // nT) * C
            return [(row0, col0), (row0, d + col0)]

        _store_staged([st_x, st_g], dp_ref, sems, step_no, nC * nT, where)

    rev = lambda c, tr: (nT - 1 - tr, c)
    vec = pl.BlockSpec((1, C), lambda c, tr: (0, c))
    sq = pl.BlockSpec((None, C, C), lambda c, tr: (c, 0, 0))
    cwb = pl.BlockSpec((kw, C), lambda c, tr: (0, c))
    halo_prev = lambda c, tr: (jnp.maximum((nT - 1 - tr) * hb - 1, 0), c)
    outs, passed = _call(
        name, (nC, nT), body,
        [pl.BlockSpec((tb, C), rev),
         pl.BlockSpec((HALO, C), halo_prev),
         pl.BlockSpec((tb, C), lambda c, tr: (nT - 1 - tr, nC + c)),
         pl.BlockSpec((tb, C), rev),
         pl.BlockSpec((HALO, C), halo_prev),
         pl.BlockSpec((tb, C), rev),
         cwb, vec, sq, vec, sq, vec, vec, ANY],
        [p, p, p, hs, hs, dyl, conv_w, conv_b, wa_bd, ba, wx_bd, bx, lam, dp],
        [_S(dp.shape, dp.dtype), _S((kw, d), F32), _S((1, d), F32), _S((nC, C, C), F32), _S((1, d), F32),
         _S((nC, C, C), F32), _S((1, d), F32), _S((1, d), F32)],
        [ANY, cwb, vec, sq, vec, sq, vec, vec],
        [pltpu.VMEM((tb, C), F32), pltpu.VMEM((tb, C), F32), pltpu.VMEM((tb, C), F32),
         pltpu.VMEM((SUB, C), F32), pltpu.VMEM((SUB, C), F32), pltpu.VMEM((SUB, C), F32),
         pltpu.VMEM((2, tb, C), BF16), pltpu.VMEM((2, tb, C), BF16), pltpu.SemaphoreType.DMA((2, 2))],
        tasks, own_aliases={13: 0})
    return (*outs, passed) if tasks else outs


def _sc_fwd(p, conv_w, *, d, name):
    T = p.shape[0]
    C = _tile(d, C_EW)
    nC = d // C
    tb = _tile(T, TB, HALO)
    nT, hb = T // tb, tb // HALO

    def body(b_ref, c_ref, ch_ref, v_ref, vh_ref, w_ref, y_ref):
        t = pl.program_id(1)
        cv = c_ref[...].astype(F32) * v_ref[...].astype(F32)
        cvp = _prev8(ch_ref, t) * _prev8(vh_ref, t)
        y_ref[...] = (b_ref[...].astype(F32) * _conv(cv, cvp, w_ref)).astype(BF16)

    seg = lambda k: pl.BlockSpec((tb, C), lambda c, t: (t, k * nC + c))
    hseg = lambda k: pl.BlockSpec((HALO, C), _halo_prev_map(hb, lambda c: k * nC + c))
    return pl.pallas_call(
        body, name=name, out_shape=_S((T, d), BF16), grid=(nC, nT),
        in_specs=[seg(2), seg(3), hseg(3), seg(4), hseg(4), pl.BlockSpec((conv_w.shape[0], C), lambda c, t: (0, c))],
        out_specs=pl.BlockSpec((tb, C), lambda c, t: (t, c)),
        compiler_params=_cparams(("parallel", "parallel")),
    )(p, p, p, p, p, conv_w)


def _sc_bwd(p, dys, dp, conv_w, *, d, name):
    T = p.shape[0]
    C = _tile(d, C_EW)
    nC = d // C
    tb = _tile(T, TB, HALO)
    nT, hb = T // tb, tb // HALO
    kw = conv_w.shape[0]

    def body(b_ref, bn_ref, c_ref, ch_ref, v_ref, vh_ref, dy_ref, dyn_ref, w_ref, dp_in, dp_ref, dw_ref,
             st_b, st_c, st_v, sems):
        del dp_in
        c = pl.program_id(0)
        t = pl.program_id(1)
        last = t == nT - 1
        bv = b_ref[...].astype(F32)
        cvv = c_ref[...].astype(F32)
        vv = v_ref[...].astype(F32)
        dy = dy_ref[...].astype(F32)
        cv = cvv * vv
        cvp = _prev8(ch_ref, t) * _prev8(vh_ref, t)
        step_no = c * nT + t
        slot = step_no % 2
        st_b[slot] = (dy * _conv(cv, cvp, w_ref)).astype(BF16)
        dz = dy * bv
        dzn = _next8(dyn_ref, last) * _next8(bn_ref, last)
        _conv_dw(dw_ref, dz, cv, cvp, t == 0)
        dcv = _conv_t(dz, dzn, w_ref)
        st_c[slot] = (dcv * vv).astype(BF16)
        st_v[slot] = (dcv * cvv).astype(BF16)

        def where(s):
            return [((s % nT) * tb, (2 + k) * d + (s // nT) * C) for k in range(3)]

        _store_staged([st_b, st_c, st_v], dp_ref, sems, step_no, nC * nT, where)

    seg = lambda k: pl.BlockSpec((tb, C), lambda c, t: (t, k * nC + c))
    hseg = lambda k: pl.BlockSpec((HALO, C), _halo_prev_map(hb, lambda c: k * nC + c))
    last_h = T // HALO - 1
    nseg = lambda k: pl.BlockSpec((HALO, C), lambda c, t: (jnp.minimum((t + 1) * hb, last_h), k * nC + c))
    return pl.pallas_call(
        body, name=name, out_shape=(_S(dp.shape, dp.dtype), _S((kw, d), F32)), grid=(nC, nT),
        in_specs=[seg(2), nseg(2), seg(3), hseg(3), seg(4), hseg(4),
                  pl.BlockSpec((tb, C), lambda c, t: (t, c)), nseg(0),
                  pl.BlockSpec((kw, C), lambda c, t: (0, c)), ANY],
        out_specs=(ANY, pl.BlockSpec((kw, C), lambda c, t: (0, c))),
        scratch_shapes=[pltpu.VMEM((2, tb, C), BF16)] * 3 + [pltpu.SemaphoreType.DMA((2, 3))],
        input_output_aliases={9: 0},
        compiler_params=_cparams(("arbitrary", "arbitrary")),
    )(p, p, p, p, p, p, dys, dys, conv_w, dp)


def _merge_fwd(p, y_lru, y_sc, *, col0, name):
    T, D = y_lru.shape
    C = _tile(math.gcd(D, col0), 1024)
    nC = D // C
    k0 = col0 // C
    tb = _tile(T, 256, HALO)

    def body(gl_ref, gs_ref, yl_ref, ys_ref, o_ref):
        @pl.loop(0, tb // HALO)
        def _(k):
            rows = pl.ds(pl.multiple_of(k * HALO, HALO), HALO)
            for l0 in range(0, C, min(C, C_EW)):
                at = (rows, pl.ds(l0, min(C, C_EW)))
                o_ref[at] = (jax.nn.sigmoid(gl_ref[at].astype(F32)) * yl_ref[at].astype(F32)
                             + jax.nn.sigmoid(gs_ref[at].astype(F32)) * ys_ref[at].astype(F32)).astype(BF16)

    blk = pl.BlockSpec((tb, C), lambda c, t: (t, c))
    return pl.pallas_call(
        body, name=name, out_shape=_S((T, D), BF16), grid=(nC, T // tb),
        in_specs=[pl.BlockSpec((tb, C), lambda c, t: (t, k0 + c)),
                  pl.BlockSpec((tb, C), lambda c, t: (t, k0 + nC + c)), blk, blk],
        out_specs=blk, compiler_params=_cparams(("parallel", "parallel")),
    )(p, p, y_lru, y_sc)


def _merge_bwd(p, y_lru, y_sc, dm, *, col0, name):
    T, D = y_lru.shape
    C = _tile(math.gcd(D, col0), 1024)
    nC = D // C
    k0 = col0 // C
    tb = _tile(T, 256, HALO)
    nT = T // tb

    def body(gl_ref, gs_ref, yl_ref, ys_ref, dm_ref, dp_ref, dyl_ref, dys_ref, st_l, st_s, sems):
        step_no = pl.program_id(0) * nT + pl.program_id(1)
        slot = step_no % 2

        @pl.loop(0, tb // HALO)
        def _(k):
            rows = pl.ds(pl.multiple_of(k * HALO, HALO), HALO)
            for l0 in range(0, C, min(C, C_EW)):
                at = (rows, pl.ds(l0, min(C, C_EW)))
                dmv = dm_ref[at].astype(F32)
                sl = jax.nn.sigmoid(gl_ref[at].astype(F32))
                ss = jax.nn.sigmoid(gs_ref[at].astype(F32))
                dyl_ref[at] = (dmv * sl).astype(BF16)
                dys_ref[at] = (dmv * ss).astype(BF16)
                st_l[(slot,) + at] = (dmv * yl_ref[at].astype(F32) * sl * (1.0 - sl)).astype(BF16)
                st_s[(slot,) + at] = (dmv * ys_ref[at].astype(F32) * ss * (1.0 - ss)).astype(BF16)

        def where(s):
            row0, colc = (s % nT) * tb, (s // nT) * C
            return [(row0, col0 + colc), (row0, col0 + D + colc)]

        _store_staged([st_l, st_s], dp_ref, sems, step_no, nC * nT, where)

    blk = pl.BlockSpec((tb, C), lambda c, t: (t, c))
    return pl.pallas_call(
        body, name=name, out_shape=(_S(p.shape, BF16), _S((T, D), BF16), _S((T, D), BF16)),
        grid=(nC, nT),
        in_specs=[pl.BlockSpec((tb, C), lambda c, t: (t, k0 + c)),
                  pl.BlockSpec((tb, C), lambda c, t: (t, k0 + nC + c)), blk, blk, blk],
        out_specs=(ANY, blk, blk),
        scratch_shapes=[pltpu.VMEM((2, tb, C), BF16), pltpu.VMEM((2, tb, C), BF16), pltpu.SemaphoreType.DMA((2, 2))],
        compiler_params=_cparams(("arbitrary", "arbitrary")),
    )(p, p, y_lru, y_sc, dm)


def _ffn_act_fwd(uu, conv_w, *, name, tasks=()):
    T = uu.shape[0]
    F = uu.shape[1] // 2
    C = _tile(F, C_EW)
    nC = F // C
    tb = _tile(T, TB, HALO)
    nT, hb = T // tb, tb // HALO
    kw = conv_w.shape[0]
    R = HALO

    def body(g_ref, gh_ref, v_ref, vh_ref, wg_ref, wv_ref, o_ref):
        t = pl.program_id(1)

        def chunk(k, carry):
            gp, vp = carry
            r0 = pl.multiple_of(k * R, R)
            ug = g_ref[pl.ds(r0, R), :].astype(F32)
            uv = v_ref[pl.ds(r0, R), :].astype(F32)
            cg = _conv(ug, gp, wg_ref)
            cv = _conv(uv, vp, wv_ref)
            o_ref[pl.ds(r0, R), :] = (cg * jax.nn.sigmoid(cg) * cv).astype(BF16)
            return ug[R - SUB:, :], uv[R - SUB:, :]

        lax.fori_loop(0, tb // R, chunk, (_prev8(gh_ref, t), _prev8(vh_ref, t)))

    seg = lambda k: pl.BlockSpec((tb, C), lambda c, t: (t, k * nC + c))
    hseg = lambda k: pl.BlockSpec((HALO, C), _halo_prev_map(hb, lambda c: k * nC + c))
    wseg = lambda k: pl.BlockSpec((kw, C), lambda c, t: (0, k * nC + c))
    outs, passed = _call(
        name, (nC, nT), body, [seg(0), hseg(0), seg(1), hseg(1), wseg(0), wseg(1)],
        [uu, uu, uu, uu, conv_w, conv_w], [_S((T, F), BF16)], [pl.BlockSpec((tb, C), lambda c, t: (t, c))], [], tasks)
    return (outs[0], passed) if tasks else outs[0]


def _ffn_act_bwd(uu, dact, conv_w, *, name):
    T = uu.shape[0]
    F = uu.shape[1] // 2
    C = _tile(F, C_EW)
    nC = F // C
    tb = _tile(T, TB, HALO)
    nT, hb = T // tb, tb // HALO
    kw = conv_w.shape[0]
    R = HALO
    nk = tb // R

    def body(g_ref, gh_ref, v_ref, vh_ref, da_ref, wg_ref, wv_ref, du_ref, dwg_ref, dwv_ref,
             gn_s, vn_s, accg_s, accv_s, st_g, st_v, sems):
        c = pl.program_id(0)
        tr = pl.program_id(1)
        t = nT - 1 - tr
        first = tr == 0

        @pl.when(first)
        def _():
            gn_s[...] = jnp.zeros_like(gn_s)
            vn_s[...] = jnp.zeros_like(vn_s)
            dwg_ref[...] = jnp.zeros_like(dwg_ref)
            dwv_ref[...] = jnp.zeros_like(dwv_ref)

        accg_s[...] = jnp.zeros_like(accg_s)
        accv_s[...] = jnp.zeros_like(accv_s)
        step_no = c * nT + tr
        slot = step_no % 2

        def chunk(i, carry):
            gn, vn = carry
            k = nk - 1 - i
            r0 = pl.multiple_of(k * R, R)
            rp = pl.multiple_of(jnp.maximum(r0 - R, 0), R)
            ug = g_ref[pl.ds(r0, R), :].astype(F32)
            uv = v_ref[pl.ds(r0, R), :].astype(F32)
            gp = jnp.where(k > 0, g_ref[pl.ds(rp, R), :].astype(F32)[R - SUB:, :], _prev8(gh_ref, t))
            vp = jnp.where(k > 0, v_ref[pl.ds(rp, R), :].astype(F32)[R - SUB:, :], _prev8(vh_ref, t))
            sh_g = _shifted_down(ug, gp, kw)
            sh_v = _shifted_down(uv, vp, kw)
            cg = _taps(sh_g, wg_ref)
            cv = _taps(sh_v, wv_ref)
            da = da_ref[pl.ds(r0, R), :].astype(F32)
            sg = jax.nn.sigmoid(cg)
            d_cg = da * cv * (sg * (1.0 + cg * (1.0 - sg)))
            d_cv = da * (cg * sg)
            for j in range(kw):
                accg_s[j] += d_cg * sh_g[kw - 1 - j]
                accv_s[j] += d_cv * sh_v[kw - 1 - j]
            st_g[slot, pl.ds(r0, R), :] = _conv_t(d_cg, gn, wg_ref).astype(BF16)
            st_v[slot, pl.ds(r0, R), :] = _conv_t(d_cv, vn, wv_ref).astype(BF16)
            return d_cg[:SUB, :], d_cv[:SUB, :]

        gn, vn = lax.fori_loop(0, nk, chunk, (gn_s[...], vn_s[...]))
        gn_s[...] = gn
        vn_s[...] = vn
        for j in range(kw):
            dwg_ref[pl.ds(j, 1), :] += jnp.sum(accg_s[j], axis=0, keepdims=True)
            dwv_ref[pl.ds(j, 1), :] += jnp.sum(accv_s[j], axis=0, keepdims=True)
        def where(s):
            row0, col0 = (nT - 1 - s % nT) * tb, (s // nT) * C
            return [(row0, col0), (row0, F + col0)]

        _store_staged([st_g, st_v], du_ref, sems, step_no, nC * nT, where)

    seg = lambda k: pl.BlockSpec((tb, C), lambda c, tr: (nT - 1 - tr, k * nC + c))
    hseg = lambda k: pl.BlockSpec((HALO, C), lambda c, tr: (jnp.maximum((nT - 1 - tr) * hb - 1, 0), k * nC + c))
    wseg = lambda k: pl.BlockSpec((kw, C), lambda c, tr: (0, k * nC + c))
    dwb = pl.BlockSpec((kw, C), lambda c, tr: (0, c))
    return pl.pallas_call(
        body, name=name, out_shape=(_S(uu.shape, BF16), _S((kw, F), F32), _S((kw, F), F32)), grid=(nC, nT),
        in_specs=[seg(0), hseg(0), seg(1), hseg(1), pl.BlockSpec((tb, C), lambda c, tr: (nT - 1 - tr, c)),
                  wseg(0), wseg(1)],
        out_specs=(ANY, dwb, dwb),
        scratch_shapes=[pltpu.VMEM((SUB, C), F32), pltpu.VMEM((SUB, C), F32),
                        pltpu.VMEM((kw, R, C), F32), pltpu.VMEM((kw, R, C), F32),
                        pltpu.VMEM((2, tb, C), BF16), pltpu.VMEM((2, tb, C), BF16), pltpu.SemaphoreType.DMA((2, 2))],
        compiler_params=_cparams(("arbitrary", "arbitrary")),
    )(uu, uu, uu, uu, dact, conv_w, conv_w)


def _place():
    x, y, c = lax.axis_index("x"), lax.axis_index("y"), lax.axis_index("c")
    return x, y, c


def _chips(x, y):
    return [(1 - x, y), (x, 1 - y), (1 - x, 1 - y)]


def _all_gather(arrays, placed, over_ici, pair_n, name):
    n = len(arrays)

    def body(*refs):
        ins, outs = refs[:n], refs[n:2 * n]
        send_sems, recv_sems, local_sems = refs[2 * n:]
        x, y, c = _place()
        me, sibling = (x, y, c), (x, y, 1 - c)
        chips = _chips(x, y)
        full = [a for a in range(n) if over_ici[a]]

        def idx(px, py, pc):
            return 4 * px + 2 * py + pc

        def copy(a, k, block, to):
            dst = _dev_block(outs[a], idx(*block), pair_n[a])
            src = ins[a] if (block is me and not placed[a]) else dst
            return pltpu.make_async_remote_copy(
                src_ref=src, dst_ref=dst, send_sem=send_sems.at[a, k], recv_sem=recv_sems.at[a, k],
                device_id=to, device_id_type=MESH)

        def half(a, k, block, to, lo):
            r = rows_of[a] // 2
            blk = _rows_of(outs[a], idx(*block), (0 if lo else r, r), pair_n[a])
            return pltpu.make_async_remote_copy(
                src_ref=blk, dst_ref=blk, send_sem=send_sems.at[a, k], recv_sem=recv_sems.at[a, k],
                device_id=to, device_id_type=MESH)

        mine = [pltpu.make_async_copy(ins[a], outs[a].at[idx(*me)], local_sems.at[a])
                for a in range(n) if not placed[a]]
        for cp in mine:
            cp.start()
        chip_x, chip_y, chip_d = chips
        sent = []
        for a in full:
            sent += [copy(a, 1, me, (*chip_x, c)), copy(a, 2, me, (*chip_y, c))]
            if not relay[a]:
                sent.append(copy(a, 3, me, (*chip_d, c)))
        for a in range(n):
            sent.append(copy(a, 0, me, sibling))
        for cp in sent:
            cp.start()

        def then(cp):
            cp.start()
            sent.append(cp)

        for a in full:
            copy(a, 2, (*chip_y, c), me).wait_recv()
            if relay[a]:
                then(half(a, 3, (*chip_y, c), (*chip_x, c), True))
            then(copy(a, 6, (*chip_y, c), sibling))
            copy(a, 1, (*chip_x, c), me).wait_recv()
            if relay[a]:
                then(half(a, 4, (*chip_x, c), (*chip_y, c), False))
            then(copy(a, 5, (*chip_x, c), sibling))
        for a in full:
            if relay[a]:
                half(a, 3, (*chip_d, c), me, True).wait_recv()
                half(a, 4, (*chip_d, c), me, False).wait_recv()
            else:
                copy(a, 3, (*chip_d, c), me).wait_recv()
            then(copy(a, 7, (*chip_d, c), sibling))
        for a in range(n):
            copy(a, 0, sibling, me).wait_recv()
        for a in full:
            for j, chip in enumerate(chips):
                copy(a, 5 + j, (*chip, 1 - c), me).wait_recv()
        for cp in sent:
            cp.wait_send()
        for cp in mine:
            cp.wait()

    rows_of = [(s.shape[1] if placed[a] else s.shape[0]) for a, s in enumerate(arrays)]
    relay = [r % (2 * HALO) == 0 for r in rows_of]
    return pl.pallas_call(
        body, name=name,
        out_shape=tuple(_S(s.shape if placed[a] else (N_DEV,) + s.shape, s.dtype) for a, s in enumerate(arrays)),
        in_specs=[ANY] * n, out_specs=tuple([ANY] * n),
        scratch_shapes=[pltpu.SemaphoreType.DMA((n, 8)), pltpu.SemaphoreType.DMA((n, 8)),
                        pltpu.SemaphoreType.DMA((n,))],
        input_output_aliases={a: a for a in range(n) if placed[a]},
    )(*arrays)


def _dev_block(ref, dev, pair_n=None):
    if pair_n is None:
        return ref.at[dev]
    return ref.at[dev // 2, :, pl.ds(pl.multiple_of((dev % 2) * pair_n, LANES), pair_n)]


def _rows_of(ref, blk, rows, pair_n=None):
    v = _dev_block(ref, blk, pair_n)
    return v if rows is None else v.at[pl.ds(rows[0], rows[1])]


ALL_ROWS = "all"


def _gather_task(buf, ici=None, fwd=None, pair_n=None):
    rows = lambda r: None if r == ALL_ROWS else r
    blk_of = functools.partial(_rows_of, pair_n=pair_n)

    def copies(refs, ss, rs):
        x, y, c = _place()
        me = 4 * x + 2 * y + c
        cps = []
        for j, (px, py) in enumerate(_chips(x, y)):
            if ici is not None:
                blk = blk_of(refs[0], me, rows(ici))
                cps.append(pltpu.make_async_remote_copy(
                    src_ref=blk, dst_ref=blk, send_sem=ss.at[j], recv_sem=rs.at[j],
                    device_id=(px, py, c), device_id_type=MESH))
            if fwd is not None:
                blk = blk_of(refs[0], 4 * px + 2 * py + c, rows(fwd))
                cps.append(pltpu.make_async_remote_copy(
                    src_ref=blk, dst_ref=blk, send_sem=ss.at[3 + j], recv_sem=rs.at[3 + j],
                    device_id=(x, y, 1 - c), device_id_type=MESH))
        return cps

    def start(refs, ss, rs, ls):
        for cp in copies(refs, ss, rs):
            cp.start()

    def wait(refs, ss, rs, ls):
        x, y, c = _place()
        for j, (px, py) in enumerate(_chips(x, y)):
            if ici is not None:
                blk = blk_of(refs[0], 4 * px + 2 * py + c, rows(ici))
                pltpu.make_async_remote_copy(
                    src_ref=blk, dst_ref=blk, send_sem=ss.at[j], recv_sem=rs.at[j],
                    device_id=(px, py, c), device_id_type=MESH).wait_recv()
            if fwd is not None:
                blk = blk_of(refs[0], 4 * px + 2 * py + 1 - c, rows(fwd))
                pltpu.make_async_remote_copy(
                    src_ref=blk, dst_ref=blk, send_sem=ss.at[3 + j], recv_sem=rs.at[3 + j],
                    device_id=(x, y, 1 - c), device_id_type=MESH).wait_recv()
        for cp in copies(refs, ss, rs):
            cp.wait_send()

    return _Task([buf], [0], start, wait, nsem=6)


def _exchange_task(parts, landing, rows=None):
    def copies(refs, ss, rs):
        x, y, c = _place()
        myq = 2 * x + y
        return [pltpu.make_async_remote_copy(
            src_ref=_rows_of(refs[0], 2 * px + py, rows), dst_ref=_rows_of(refs[1], myq, rows),
            send_sem=ss.at[k], recv_sem=rs.at[k], device_id=(px, py, c), device_id_type=MESH)
            for k, (px, py) in enumerate(_chips(x, y))]

    def start(refs, ss, rs, ls):
        for cp in copies(refs, ss, rs):
            cp.start()

    def wait(refs, ss, rs, ls):
        x, y, c = _place()
        for k, (px, py) in enumerate(_chips(x, y)):
            pltpu.make_async_remote_copy(
                src_ref=_rows_of(refs[0], 2 * x + y, rows), dst_ref=_rows_of(refs[1], 2 * px + py, rows),
                send_sem=ss.at[k], recv_sem=rs.at[k], device_id=(px, py, c), device_id_type=MESH).wait_recv()
        for cp in copies(refs, ss, rs):
            cp.wait_send()

    return _Task([parts, landing], [1], start, wait)


def _core_blocks(g, pair_n):
    if pair_n is None:
        g4 = g.reshape((N_CHIP, 2) + g.shape[1:])
        return g4, (N_CHIP,) + g.shape[1:], lambda ref, c: ref.at[:, c]
    view = lambda ref, c: ref.at[:, :, pl.ds(pl.multiple_of(c * pair_n, LANES), pair_n)]
    return g, (N_CHIP, g.shape[1], pair_n), view


def _swap_task(g, pair_n=None):
    g4, shape, view = _core_blocks(g, pair_n)

    def copy(refs, ss, rs):
        x, y, c = _place()
        return pltpu.make_async_remote_copy(
            src_ref=view(refs[0], 1 - c), dst_ref=refs[1], send_sem=ss.at[0], recv_sem=rs.at[0],
            device_id=(x, y, 1 - c), device_id_type=MESH)

    def start(refs, ss, rs, ls):
        copy(refs, ss, rs).start()

    def wait(refs, ss, rs, ls):
        copy(refs, ss, rs).wait()

    return _Task([g4], [], start, wait, fresh=[_S(shape, g.dtype)], nsem=1)


def _peer(x, y, c, m):
    return x ^ (m >> 2), y ^ ((m >> 1) & 1), c ^ (m & 1)


def _bcast_task(pack):
    def copies(refs, ss, rs):
        x, y, c = _place()
        me = 4 * x + 2 * y + c
        return [pltpu.make_async_remote_copy(
            src_ref=refs[0], dst_ref=refs[1].at[me], send_sem=ss.at[m - 1], recv_sem=rs.at[m - 1],
            device_id=_peer(x, y, c, m), device_id_type=MESH) for m in range(1, N_DEV)]

    def local(refs, ls):
        x, y, c = _place()
        return pltpu.make_async_copy(refs[0], refs[1].at[4 * x + 2 * y + c], ls.at[0])

    def start(refs, ss, rs, ls):
        local(refs, ls).start()
        for cp in copies(refs, ss, rs):
            cp.start()

    def wait(refs, ss, rs, ls):
        x, y, c = _place()
        for m in range(1, N_DEV):
            px, py, pc = _peer(x, y, c, m)
            pltpu.make_async_remote_copy(
                src_ref=refs[0], dst_ref=refs[1].at[4 * px + 2 * py + pc], send_sem=ss.at[m - 1],
                recv_sem=rs.at[m - 1], device_id=(px, py, pc), device_id_type=MESH).wait_recv()
        for cp in copies(refs, ss, rs):
            cp.wait_send()
        local(refs, ls).wait()

    return _Task([pack], [], start, wait, fresh=[_S((N_DEV,) + pack.shape, pack.dtype)], nsem=N_DEV - 1)


def _sum_packs(packs, name):
    _, R, L = packs.shape

    def body(p_ref, o_ref):
        acc = p_ref[0]
        for k in range(1, N_DEV):
            acc = acc + p_ref[k]
        o_ref[...] = acc

    return pl.pallas_call(body, name=name, out_shape=_S((R, L), packs.dtype), in_specs=[VMEM_SPEC],
                          out_specs=VMEM_SPEC, compiler_params=_cparams())(packs)


def _swap_halves(g, name, pair_n=None):
    g4, shape, view = _core_blocks(g, pair_n)

    def body(g_ref, o_ref, send_sem, recv_sem):
        x, y, c = _place()
        cp = pltpu.make_async_remote_copy(
            src_ref=view(g_ref, 1 - c), dst_ref=o_ref, send_sem=send_sem, recv_sem=recv_sem,
            device_id=(x, y, 1 - c), device_id_type=MESH)
        cp.start()
        cp.wait()

    return pl.pallas_call(
        body, name=name, out_shape=_S(shape, g.dtype), in_specs=[ANY], out_specs=ANY,
        scratch_shapes=[pltpu.SemaphoreType.DMA, pltpu.SemaphoreType.DMA],
    )(g4)


def _add_halves(g, landed, place, name, pair_n=None):
    _, r, cc = landed.shape
    tr = _tile(r, 512, HALO)
    if pair_n is None:
        g4 = g.reshape(N_CHIP, 2, r, cc)
        g_spec = pl.BlockSpec((None, None, tr, cc), lambda i, q, s: (q, s[0], i, 0))
    else:
        g4 = g
        g_spec = pl.BlockSpec((None, tr, cc), lambda i, q, s: (q, i, s[0]))

    def body(s_ref, g_ref, l_ref, o_ref, land_ref):
        q = pl.program_id(1)
        v = (g_ref[...].astype(F32) + l_ref[...].astype(F32)).astype(BF16)
        o_ref[...] = v

        @pl.when(q == s_ref[1])
        def _():
            land_ref[...] = v

    return pl.pallas_call(
        body, name=name, out_shape=(_S((N_CHIP, r, cc), BF16), _S((N_CHIP, r, cc), BF16)),
        grid_spec=pltpu.PrefetchScalarGridSpec(
            num_scalar_prefetch=1, grid=(r // tr, N_CHIP),
            in_specs=[g_spec,
                      pl.BlockSpec((None, tr, cc), lambda i, q, s: (q, i, 0))],
            out_specs=(pl.BlockSpec((None, tr, cc), lambda i, q, s: (q, i, 0)),
                       pl.BlockSpec((None, tr, cc), lambda i, q, s: (s[1], i, 0)))),
        compiler_params=_cparams(("arbitrary", "arbitrary")),
    )(place, g4, landed)


def _all_reduce_small(pack, name):
    R = pack.shape[0]

    def body(p_ref, o_ref, buf, send_sems, recv_sems):
        x, y, c = _place()
        me = 4 * x + 2 * y + c
        buf[me] = p_ref[...]
        cps = []
        for k in range(N_DEV - 1):
            m = k + 1
            peer = (x ^ (m >> 2), y ^ ((m >> 1) & 1), c ^ (m & 1))
            cps.append(pltpu.make_async_remote_copy(
                src_ref=p_ref, dst_ref=buf.at[me], send_sem=send_sems.at[k], recv_sem=recv_sems.at[k],
                device_id=peer, device_id_type=MESH))
        for cp in cps:
            cp.start()
        for k in range(N_DEV - 1):
            m = k + 1
            peer_idx = 4 * (x ^ (m >> 2)) + 2 * (y ^ ((m >> 1) & 1)) + (c ^ (m & 1))
            pltpu.make_async_remote_copy(
                src_ref=p_ref, dst_ref=buf.at[peer_idx], send_sem=send_sems.at[k], recv_sem=recv_sems.at[k],
                device_id=(x, y, c), device_id_type=MESH).wait_recv()
        for cp in cps:
            cp.wait_send()
        acc = buf[0]
        for k in range(1, N_DEV):
            acc = acc + buf[k]
        o_ref[...] = acc

    return pl.pallas_call(
        body, name=name, out_shape=_S((R, LANES), F32),
        in_specs=[VMEM_SPEC], out_specs=VMEM_SPEC,
        scratch_shapes=[pltpu.VMEM((N_DEV, R, LANES), F32), pltpu.SemaphoreType.DMA((N_DEV - 1,)),
                        pltpu.SemaphoreType.DMA((N_DEV - 1,))],
        compiler_params=_cparams(),
    )(pack)


def _adamw_math(w, g, m, v):
    m = ADAM_B1 * m + (1.0 - ADAM_B1) * g
    v = ADAM_B2 * v + (1.0 - ADAM_B2) * (g * g)
    m_hat = m / (1.0 - ADAM_B1 ** ADAM_STEP)
    v_hat = v / (1.0 - ADAM_B2 ** ADAM_STEP)
    delta = -ADAM_LR * (m_hat / (jnp.sqrt(v_hat) + ADAM_EPS) + ADAM_WD * w)
    return delta, m, v


def _adamw_big(parts, w, m, v, name):
    r, cc = w.shape
    tr = _tile(r, 128, HALO)

    def body(p_ref, w_ref, m_ref, v_ref, g_ref, d_ref, nm_ref, nv_ref):
        g = p_ref[0].astype(F32)
        for q in range(1, N_CHIP):
            g = g + p_ref[q].astype(F32)
        g_ref[...] = g
        d_ref[...], nm_ref[...], nv_ref[...] = _adamw_math(w_ref[...], g, m_ref[...], v_ref[...])

    blk = pl.BlockSpec((tr, cc), lambda i: (i, 0))
    return pl.pallas_call(
        body, name=name, out_shape=tuple(_S((r, cc), F32) for _ in range(4)), grid=(r // tr,),
        in_specs=[pl.BlockSpec((N_CHIP, tr, cc), lambda i: (0, i, 0)), blk, blk, blk],
        out_specs=(blk, blk, blk, blk), compiler_params=_cparams(("parallel",)),
    )(parts, w, m, v)


def _adamw_small(ws, gs, ms, vs, name):
    n = len(ws)

    def body(*refs):
        w_r, g_r, m_r, v_r = refs[:n], refs[n:2 * n], refs[2 * n:3 * n], refs[3 * n:4 * n]
        d_r, nm_r, nv_r = refs[4 * n:5 * n], refs[5 * n:6 * n], refs[6 * n:7 * n]
        for k in range(n):
            d_r[k][...], nm_r[k][...], nv_r[k][...] = _adamw_math(w_r[k][...], g_r[k][...], m_r[k][...], v_r[k][...])

    shapes = tuple(_S(w.shape, F32) for w in ws)
    outs = pl.pallas_call(
        body, name=name, out_shape=shapes * 3,
        in_specs=[VMEM_SPEC] * (4 * n), out_specs=tuple([VMEM_SPEC] * (3 * n)),
        compiler_params=_cparams(),
    )(*ws, *gs, *ms, *vs)
    return outs[:n], outs[n:2 * n], outs[2 * n:]


def _block_diag(w, heads_per_block):
    H, hd, _ = w.shape
    nb = H // heads_per_block
    eye = jnp.eye(heads_per_block, dtype=w.dtype)
    w4 = w.reshape(nb, heads_per_block, hd, hd)
    return jnp.einsum("nhab,hg->nhagb", w4, eye).reshape(nb, heads_per_block * hd, heads_per_block * hd)


def _diag_blocks(bd, heads_per_block, hd):
    nb = bd.shape[0]
    b5 = bd.reshape(nb, heads_per_block, hd, heads_per_block, hd)
    return jnp.stack([b5[:, h, :, h, :] for h in range(heads_per_block)], axis=1).reshape(nb * heads_per_block, hd, hd)


def _as_rows(a):
    if a.ndim == 1:
        return a.reshape(-1, LANES) if a.shape[0] % LANES == 0 else a.reshape(1, -1)
    if a.ndim == 3:
        return a.reshape(-1, LANES) if (a.size % LANES == 0) else a.reshape(a.shape[0] * a.shape[1], a.shape[2])
    return a


def kernel(x, g_mix, w_in, lru_conv_w, lru_conv_b, lru_wa, lru_ba, lru_wx, lru_bx, lru_lambda, lru_w_out, sc_conv_w, sc_w_out, w_o, g_ffn, ffn_w_up, ffn_conv_w, ffn_w_down, g_final, loss_target, m_g_mix, m_w_in, m_lru_conv_w, m_lru_conv_b, m_lru_wa, m_lru_ba, m_lru_wx, m_lru_bx, m_lru_lambda, m_lru_w_out, m_sc_conv_w, m_sc_w_out, m_w_o, m_g_ffn, m_ffn_w_up, m_ffn_conv_w, m_ffn_w_down, m_g_final, v_g_mix, v_w_in, v_lru_conv_w, v_lru_conv_b, v_lru_wa, v_lru_ba, v_lru_wx, v_lru_bx, v_lru_lambda, v_lru_w_out, v_sc_conv_w, v_sc_w_out, v_w_o, v_g_ffn, v_ffn_w_up, v_ffn_conv_w, v_ffn_w_down, v_g_final):
    T, D = x.shape[1], x.shape[2]
    d_lru = lru_lambda.shape[0]
    d_sc = sc_conv_w.shape[1] * N_DEV
    F = ffn_w_down.shape[0] * N_DEV
    H = lru_wa.shape[0]
    assert d_lru == d_sc and H * HEAD_DIM == d_lru
    xs = x.reshape(T, D)
    tgt = loss_target.reshape(T, D)
    my_x, my_y, my_c = _place()
    me = 4 * my_x + 2 * my_y + my_c

    big = [w_in, lru_w_out, sc_w_out, w_o, ffn_w_up, ffn_w_down]
    big_names = ["w_in", "lru_w_out", "sc_w_out", "w_o", "ffn_w_up", "ffn_w_down"]
    place = jnp.stack([my_c, 2 * my_x + my_y, me]).astype(jnp.int32)
    n_in, n_up = w_in.shape[1], ffn_w_up.shape[1]
    paired = [n_in, None, None, None, n_up, None]
    big_bf = [_cast_into_slot(w, place, "cast_" + nm, paired=pn is not None)
              for w, nm, pn in zip(big, big_names, paired)]
    pad_rows = lambda a: jnp.pad(a, ((0, SUB - a.shape[0]), (0, 0)))
    gathered = _all_gather(big_bf + [pad_rows(lru_conv_w), pad_rows(sc_conv_w), pad_rows(ffn_conv_w)],
                           [True] * 6 + [False] * 3,
                           [True, False, False, False, False, False, True, True, True],
                           paired + [None] * 3, "all_gather_first")
    W_in, W_lo, W_so, W_o8, W_up, W_dn8 = gathered[:6]
    full_cols = lambda g, kw: g[:, :kw, :].transpose(1, 0, 2).reshape(kw, -1)
    cw_lru = full_cols(gathered[6], lru_conv_w.shape[0])
    cw_sc = full_cols(gathered[7], sc_conv_w.shape[0])
    cw_ffn = full_cols(gathered[8], ffn_conv_w.shape[0])

    C = _tile(d_lru, C_LRU)
    hpb = C // HEAD_DIM
    wa_bd = _block_diag(lru_wa, hpb).astype(BF16)
    wx_bd = _block_diag(lru_wx, hpb).astype(BF16)
    cb, ba, bx, lam = (a.reshape(1, d_lru) for a in (lru_conv_b, lru_ba, lru_bx, lru_lambda))

    h1 = _rms_fwd(xs, g_mix, "rms_mix")
    k8 = W_up.shape[1] // 8
    wide = 2 * max(n_in, n_up)
    p, ((W_o8,), (W_lo,), (W_so,), (W_up,)) = _mm_nn(
        h1, W_in, out_dtype=BF16, name="mm_in", tn=wide,
        tasks=[_gather_task(W_o8, ici=ALL_ROWS), _gather_task(W_lo, ici=ALL_ROWS), _gather_task(W_so, ici=ALL_ROWS),
               _gather_task(W_up, ici=(0, 3 * k8), pair_n=n_up)])
    hs, yl_pre, ((W_o8,), (W_lo,), (W_so,), (W_up,)) = _lru_fwd(
        p, cw_lru, cb, wa_bd, ba, wx_bd, bx, lam, name="lru_fwd",
        tasks=[_gather_task(W_o8, fwd=ALL_ROWS), _gather_task(W_lo, fwd=ALL_ROWS), _gather_task(W_so, fwd=ALL_ROWS),
               _gather_task(W_up, ici=(3 * k8, 3 * k8), fwd=(0, 3 * k8), pair_n=n_up)])
    ys_pre = _sc_fwd(p, cw_sc, d=d_sc, name="sc_fwd")
    y_lru, ((W_up,),) = _mm_small(
        "nn", yl_pre, None, W_lo, name="mm_lru_out",
        tasks=[_gather_task(W_up, ici=(6 * k8, k8), fwd=(3 * k8, 3 * k8), pair_n=n_up)])
    y_sc, ((W_up,),) = _mm_small(
        "nn", ys_pre, None, W_so, name="mm_sc_out",
        tasks=[_gather_task(W_up, ici=(7 * k8, k8), fwd=(6 * k8, k8), pair_n=n_up)])
    gate0 = 2 * d_lru + 3 * d_sc
    merged = _merge_fwd(p, y_lru, y_sc, col0=gate0, name="merge_fwd")
    W_o = W_o8.reshape(1, D, D)
    x1, ((W_up,),) = _mm_nn(merged, W_o, out_dtype=F32, residual=xs, name="mm_o",
                            tasks=[_gather_task(W_up, fwd=(7 * k8, k8), pair_n=n_up)])
    h2 = _rms_fwd(x1, g_ffn, "rms_ffn")
    uu, ((W_dn8,),) = _mm_nn(h2, W_up, out_dtype=BF16, name="mm_up", tn=wide,
                             tasks=[_gather_task(W_dn8, ici=ALL_ROWS)])
    act, ((W_dn8,),) = _ffn_act_fwd(uu, cw_ffn, name="ffn_act_fwd", tasks=[_gather_task(W_dn8, fwd=ALL_ROWS)])
    W_dn = W_dn8.reshape(1, F, D)
    x2 = _mm_nn(act, W_dn, out_dtype=F32, residual=x1, name="mm_down", tn=1024, tk=F)
    dx2, dx2b, loss_part, dg_final = _loss_head(x2, g_final, tgt, "loss_head")

    def pack_rows(arrs):
        flat = jnp.concatenate([a.reshape(-1) for a in arrs])
        rows = -(-flat.shape[0] // (SUB * LANES)) * SUB
        return jnp.pad(flat, (0, rows * LANES - flat.shape[0])).reshape(rows, LANES)

    def unpack_rows(pack, arrs):
        flat, out, o = pack.reshape(-1), [], 0
        for a in arrs:
            out.append(flat[o:o + a.size].reshape(a.shape))
            o += a.size
        return out

    dact = _mm_nt(dx2b, W_dn, out_dtype=BF16, name="mm_down_dx", tm=512, tko=F // 2, tn=D)
    gW_dn = _mm_tn(act, dx2b, 1, out_dtype=BF16, name="mm_down_dw", tk=1408, tt=2048).reshape(N_DEV, F // N_DEV, D)
    duu, dcw_ffn_g, dcw_ffn_v = _ffn_act_bwd(uu, dact, cw_ffn, name="ffn_act_bwd")
    dh2, ((land_dn,),) = _mm_nt(duu, W_up, out_dtype=BF16, name="mm_up_dx", tn=wide, tasks=[_swap_task(gW_dn)])
    parts_dn = _add_halves(gW_dn, land_dn, place, "rs_add_ffn_w_down")
    gW_up, ((mine_dn,),) = _mm_tn(h2, duu, N_CHIP, out_dtype=BF16, name="mm_up_dw", tk=512, tn=wide, tt=2048,
                                  tasks=[_exchange_task(*parts_dn)])
    dx1, dx1b, dg_ffn = _rms_bwd(x1, g_ffn, dh2, dx2, "rms_ffn_bwd")
    dmerged, ((land_up,),) = _mm_nt(dx1b, W_o, out_dtype=BF16, name="mm_o_dx", tn=D,
                                    tasks=[_swap_task(gW_up, pair_n=n_up)])
    parts_up = _add_halves(gW_up, land_up, place, "rs_add_ffn_w_up", pair_n=n_up)
    gW_o = _mm_tn(merged, dx1b, 1, out_dtype=BF16, name="mm_o_dw", tt=2048).reshape(N_DEV, D // N_DEV, D)
    dp, dy_lru, dy_sc = _merge_bwd(p, y_lru, y_sc, dmerged, col0=gate0, name="merge_bwd")
    dyl_pre, ((land_o,),) = _mm_small("nt", None, dy_lru, W_lo, name="mm_lru_out_dx", tasks=[_swap_task(gW_o)])
    parts_o = _add_halves(gW_o, land_o, place, "rs_add_w_o")
    gW_lo = _mm_small("tn", yl_pre, dy_lru, W_lo, name="mm_lru_out_dw")
    dys_pre, ((land_lo,),) = _mm_small("nt", None, dy_sc, W_so, name="mm_sc_out_dx", tasks=[_swap_task(gW_lo)])
    parts_lo = _add_halves(gW_lo, land_lo, place, "rs_add_lru_w_out")
    gW_so = _mm_small("tn", ys_pre, dy_sc, W_so, name="mm_sc_out_dw")
    dp, dcw_sc = _sc_bwd(p, dys_pre, dp, cw_sc, d=d_sc, name="sc_bwd")
    r_up = parts_up[0].shape[1] // 2
    dp, dcw_lru, dcb, dwa_bd, dba, dwx_bd, dbx, dlam, ((land_up,), (mine_o,), (mine_lo,), (land_so,)) = _lru_bwd(
        p, hs, dyl_pre, dp, cw_lru, cb, wa_bd, ba, wx_bd, bx, lam, name="lru_bwd",
        tasks=[_exchange_task(*parts_up, rows=(0, r_up)), _exchange_task(*parts_o), _exchange_task(*parts_lo),
               _swap_task(gW_so)])
    parts_so = _add_halves(gW_so, land_so, place, "rs_add_sc_w_out")

    dwa = _diag_blocks(dwa_bd, hpb, HEAD_DIM)
    dwx = _diag_blocks(dwx_bd, hpb, HEAD_DIM)
    dcw_ffn = jnp.concatenate([dcw_ffn_g, dcw_ffn_v], axis=1)
    small_full = [dcw_lru, dcb, dwa, dba, dwx, dbx, dlam, dcw_sc, dg_ffn, dcw_ffn, dg_final]
    gW_in, ((mine_up,), (mine_so,), (packs,)) = _mm_tn(
        h1, dp, N_CHIP, out_dtype=BF16, name="mm_in_dw", tk=512, tn=wide, tt=2048,
        tasks=[_exchange_task(parts_up[0], land_up, rows=(r_up, r_up)), _exchange_task(*parts_so),
               _bcast_task(pack_rows(small_full))])
    land_in = _swap_halves(gW_in, "rs_swap_w_in", pair_n=n_in)
    parts_in = _add_halves(gW_in, land_in, place, "rs_add_w_in", pair_n=n_in)
    dh1, ((mine_in,),) = _mm_nt(dp, W_in, out_dtype=BF16, name="mm_in_dx", tn=wide,
                                tasks=[_exchange_task(*parts_in)])
    grad_x, _, dg_mix = _rms_bwd(xs, g_mix, dh1, dx1, "rms_mix_bwd")

    mine = [mine_in, mine_lo, mine_so, mine_o, mine_up, mine_dn]
    big_m = [m_w_in, m_lru_w_out, m_sc_w_out, m_w_o, m_ffn_w_up, m_ffn_w_down]
    big_v = [v_w_in, v_lru_w_out, v_sc_w_out, v_w_o, v_ffn_w_up, v_ffn_w_down]
    big_out = {nm: _adamw_big(pt, w, m, v, "adamw_" + nm)
               for nm, pt, w, m, v in zip(big_names, mine, big, big_m, big_v)}

    (scw_lru, scb, swa, sba, swx, sbx, slam, scw_sc, sg_ffn, scw_ffn, sg_final) = unpack_rows(
        _sum_packs(packs, "sum_small"), small_full)
    (sg_mix,) = unpack_rows(_all_reduce_small(pack_rows([dg_mix]), "all_reduce_g_mix"), [dg_mix])

    def my_cols(a):
        n = a.shape[1] // N_DEV
        return lax.dynamic_slice_in_dim(a, me * n, n, axis=1)

    small_names = ["g_mix", "lru_conv_w", "lru_conv_b", "lru_wa", "lru_ba", "lru_wx", "lru_bx", "lru_lambda",
                   "sc_conv_w", "g_ffn", "ffn_conv_w", "g_final"]
    small_w = [g_mix, lru_conv_w, lru_conv_b, lru_wa, lru_ba, lru_wx, lru_bx, lru_lambda, sc_conv_w, g_ffn,
               ffn_conv_w, g_final]
    small_m = [m_g_mix, m_lru_conv_w, m_lru_conv_b, m_lru_wa, m_lru_ba, m_lru_wx, m_lru_bx, m_lru_lambda,
               m_sc_conv_w, m_g_ffn, m_ffn_conv_w, m_g_final]
    small_v = [v_g_mix, v_lru_conv_w, v_lru_conv_b, v_lru_wa, v_lru_ba, v_lru_wx, v_lru_bx, v_lru_lambda,
               v_sc_conv_w, v_g_ffn, v_ffn_conv_w, v_g_final]
    small_g = [sg_mix.reshape(D), my_cols(scw_lru), scb.reshape(d_lru), swa, sba.reshape(d_lru), swx,
               sbx.reshape(d_lru), slam.reshape(d_lru), my_cols(scw_sc), sg_ffn.reshape(D), my_cols(scw_ffn),
               sg_final.reshape(D)]
    sd, snm, snv = _adamw_small([_as_rows(a) for a in small_w], [_as_rows(a) for a in small_g],
                                [_as_rows(a) for a in small_m], [_as_rows(a) for a in small_v], "adamw_small")
    small_out = {nm: (g, d.reshape(w.shape), nm_.reshape(w.shape), nv_.reshape(w.shape))
                 for nm, w, g, d, nm_, nv_ in zip(small_names, small_w, small_g, sd, snm, snv)}

    loss = lax.psum(loss_part[0, 0], AXES)
    order = ["g_mix", "w_in", "lru_conv_w", "lru_conv_b", "lru_wa", "lru_ba", "lru_wx", "lru_bx", "lru_lambda",
             "lru_w_out", "sc_conv_w", "sc_w_out", "w_o", "g_ffn", "ffn_w_up", "ffn_conv_w", "ffn_w_down", "g_final"]
    res = {**big_out, **small_out}
    return (loss, grad_x.reshape(x.shape),
            *[res[nm][0] for nm in order], *[res[nm][1] for nm in order],
            *[res[nm][2] for nm in order], *[res[nm][3] for nm in order])
```

```python
import functools
import math

import jax
import jax.numpy as jnp
from jax import lax
from jax.experimental import pallas as pl
from jax.experimental.pallas import tpu as pltpu

F32, BF16 = jnp.float32, jnp.bfloat16
MESH = pl.DeviceIdType.MESH
N_DEV = 8
N_CHIP = 4
AXES = ("x", "y", "c")

EPS = 1e-6
LRU_C = 8.0
HEAD_DIM = 64
ADAM_LR, ADAM_B1, ADAM_B2, ADAM_EPS, ADAM_WD, ADAM_STEP = 0.001, 0.9, 0.999, 1e-08, 0.01, 10

VMEM_LIMIT = 48 * 1024 * 1024
LANES = 128
SUB = 8
HALO = 16
TB = 512
C_LRU = 256
C_EW = 512
TM, TN, TK = 512, 1536, 2048


def _tile(n, pref, align=LANES):
    best = None
    for d in range(align, min(n, pref) + 1, align):
        if n % d == 0:
            best = d
    return best or n


def _cparams(sem=None, vmem=VMEM_LIMIT):
    kw = dict(vmem_limit_bytes=vmem)
    if sem is not None:
        kw["dimension_semantics"] = sem
    return pltpu.CompilerParams(**kw)


def _S(shape, dtype):
    return jax.ShapeDtypeStruct(shape, dtype)


ANY = pl.BlockSpec(memory_space=pl.ANY)
VMEM_SPEC = pl.BlockSpec(memory_space=pltpu.VMEM)


class _Task:
    def __init__(self, arrays, aliased, start, wait, fresh=(), nsem=3):
        self.arrays, self.aliased, self.start, self.wait = arrays, aliased, start, wait
        self.fresh, self.nsem = list(fresh), nsem


def _call(name, grid, compute, in_specs, args, out_shape, out_specs, scratch, tasks=(), own_aliases=None):
    n_in, n_out, n_scr = len(args), len(out_shape), len(scratch)
    x_in, x_out, aliases, where = [], [], dict(own_aliases or {}), []
    for t in tasks:
        places = []
        for k, arr in enumerate(t.arrays):
            if k in t.aliased:
                aliases[n_in + len(x_in)] = n_out + len(x_out)
                places.append(("out", len(x_out)))
                x_out.append(_S(arr.shape, arr.dtype))
            else:
                places.append(("in", len(x_in)))
            x_in.append(arr)
        for shp in t.fresh:
            places.append(("out", len(x_out)))
            x_out.append(shp)
        where.append(places)
    n_xi, n_xo = len(x_in), len(x_out)

    def body(*refs):
        ins, xi = refs[:n_in], refs[n_in:n_in + n_xi]
        o0 = n_in + n_xi
        outs, xo = refs[o0:o0 + n_out], refs[o0 + n_out:o0 + n_out + n_xo]
        s0 = o0 + n_out + n_xo
        scr, sems = refs[s0:s0 + n_scr], refs[s0 + n_scr:]
        ids = [pl.program_id(a) for a in range(len(grid))]

        def task_refs(ti):
            return [xo[i] if kind == "out" else xi[i] for kind, i in where[ti]]

        if tasks:
            first = functools.reduce(jnp.logical_and, [i == 0 for i in ids])

            @pl.when(first)
            def _():
                for ti, t in enumerate(tasks):
                    t.start(task_refs(ti), *sems[3 * ti:3 * ti + 3])

        compute(*ins, *outs, *scr)
        if tasks:
            last = functools.reduce(jnp.logical_and, [i == g - 1 for i, g in zip(ids, grid)])

            @pl.when(last)
            def _():
                for ti, t in enumerate(tasks):
                    t.wait(task_refs(ti), *sems[3 * ti:3 * ti + 3])

    sem_shapes = []
    for t in tasks:
        sem_shapes += [pltpu.SemaphoreType.DMA((t.nsem,)), pltpu.SemaphoreType.DMA((t.nsem,)),
                       pltpu.SemaphoreType.DMA((1,))]
    res = pl.pallas_call(
        body, name=name, grid=grid,
        in_specs=list(in_specs) + [ANY] * n_xi,
        out_specs=tuple(out_specs) + (ANY,) * n_xo,
        out_shape=tuple(out_shape) + tuple(x_out),
        scratch_shapes=list(scratch) + sem_shapes,
        input_output_aliases=aliases,
        compiler_params=_cparams(("arbitrary",) * len(grid)),
    )(*args, *x_in)
    outs, passed, o = res[:n_out], [], n_out
    for places in where:
        k = sum(1 for kind, _ in places if kind == "out")
        passed.append(list(res[o:o + k]))
        o += k
    return outs, passed


def _mm_nn(a, w3, *, out_dtype, name, residual=None, tm=TM, tn=TN, tk=TK, tasks=()):
    M, K = a.shape
    G, _, n = w3.shape
    tm, tn, tk = _tile(M, tm, SUB), _tile(n, tn), _tile(K, tk)
    nj, nk = n // tn, K // tk

    def compute(*refs):
        if residual is None:
            a_ref, w_ref, o_ref = refs[:3]
            r_ref = None
        else:
            a_ref, w_ref, r_ref, o_ref = refs[:4]

        def finish(r):
            if r_ref is not None:
                r = r + r_ref[...]
            o_ref[...] = r.astype(o_ref.dtype)

        if nk == 1:
            finish(jnp.dot(a_ref[...], w_ref[...], preferred_element_type=F32))
            return
        acc = refs[-1]
        k = pl.program_id(3)

        @pl.when(k == 0)
        def _():
            acc[...] = jnp.zeros_like(acc)

        acc[...] += jnp.dot(a_ref[...], w_ref[...], preferred_element_type=F32)

        @pl.when(k == nk - 1)
        def _():
            finish(acc[...])

    in_specs = [pl.BlockSpec((tm, tk), lambda g, j, i, k: (i, k)),
                pl.BlockSpec((None, tk, tn), lambda g, j, i, k: (g, k, j))]
    args = [a, w3]
    if residual is not None:
        in_specs.append(pl.BlockSpec((tm, tn), lambda g, j, i, k: (i, g * nj + j)))
        args.append(residual)
    outs, passed = _call(
        name, (G, nj, M // tm, nk), compute, in_specs, args, [_S((M, G * n), out_dtype)],
        [pl.BlockSpec((tm, tn), lambda g, j, i, k: (i, g * nj + j))],
        [] if nk == 1 else [pltpu.VMEM((tm, tn), F32)], tasks)
    return (outs[0], passed) if tasks else outs[0]


def _mm_nt(dy, w3, *, out_dtype, name, tm=1024, tko=1024, tn=TN, tasks=()):
    M, _ = dy.shape
    G, K, n = w3.shape
    tm, tko, tn = _tile(M, tm, SUB), _tile(K, tko), _tile(n, tn)
    nj = n // tn
    nr = G * nj

    def compute(dy_ref, w_ref, o_ref, *scr):
        part = lax.dot_general(dy_ref[...], w_ref[...], (((1,), (1,)), ((), ())), preferred_element_type=F32)
        if nr == 1:
            o_ref[...] = part.astype(o_ref.dtype)
            return
        (acc,) = scr
        r = pl.program_id(2)

        @pl.when(r == 0)
        def _():
            acc[...] = jnp.zeros_like(acc)

        acc[...] += part

        @pl.when(r == nr - 1)
        def _():
            o_ref[...] = acc[...].astype(o_ref.dtype)

    outs, passed = _call(
        name, (K // tko, M // tm, nr), compute,
        [pl.BlockSpec((tm, tn), lambda ko, i, r: (i, r)),
         pl.BlockSpec((None, tko, tn), lambda ko, i, r: (r // nj, ko, r % nj))],
        [dy, w3], [_S((M, K), out_dtype)], [pl.BlockSpec((tm, tko), lambda ko, i, r: (i, ko))],
        [] if nr == 1 else [pltpu.VMEM((tm, tko), F32)], tasks)
    return (outs[0], passed) if tasks else outs[0]


def _mm_tn(a, dy, G, *, out_dtype, name, tk=1024, tn=TN, tt=1024, tasks=()):
    M, K = a.shape
    n = dy.shape[1] // G
    tk, tn, tt = _tile(K, tk), _tile(n, tn), _tile(M, tt, SUB)
    nj, nt = n // tn, M // tt

    def compute(a_ref, dy_ref, o_ref, acc):
        t = pl.program_id(3)

        @pl.when(t == 0)
        def _():
            acc[...] = jnp.zeros_like(acc)

        acc[...] += lax.dot_general(a_ref[...], dy_ref[...], (((0,), (0,)), ((), ())),
                                    preferred_element_type=F32)

        @pl.when(t == nt - 1)
        def _():
            o_ref[...] = acc[...].astype(o_ref.dtype)

    outs, passed = _call(
        name, (G, nj, K // tk, nt), compute,
        [pl.BlockSpec((tt, tk), lambda g, j, k, t: (t, k)),
         pl.BlockSpec((tt, tn), lambda g, j, k, t: (t, g * nj + j))],
        [a, dy], [_S((G, K, n), out_dtype)], [pl.BlockSpec((None, tk, tn), lambda g, j, k, t: (g, k, j))],
        [pltpu.VMEM((tk, tn), F32)], tasks)
    return (outs[0], passed) if tasks else outs[0]


def _mm_small(kind, a, b, w3, *, name, tm=1024, tasks=()):
    G, K, n = w3.shape
    M = (a if a is not None else b).shape[0]
    tm = _tile(M, tm, HALO)
    nt = M // tm
    w_spec = pl.BlockSpec((G, K, n), lambda i: (0, 0, 0))
    a_spec = pl.BlockSpec((tm, K), lambda i: (i, 0))
    b_spec = pl.BlockSpec((tm, G * n), lambda i: (i, 0))
    cols = lambda g: slice(g * n, (g + 1) * n)
    if kind == "nn":
        def compute(a_ref, w_ref, o_ref):
            av = a_ref[...]
            for g in range(G):
                o_ref[:, cols(g)] = jnp.dot(av, w_ref[g], preferred_element_type=F32).astype(o_ref.dtype)

        outs, passed = _call(name, (nt,), compute, [a_spec, w_spec], [a, w3], [_S((M, G * n), BF16)], [b_spec], [], tasks)
    elif kind == "nt":
        def compute(b_ref, w_ref, o_ref):
            acc = None
            for g in range(G):
                part = lax.dot_general(b_ref[:, cols(g)], w_ref[g], (((1,), (1,)), ((), ())),
                                       preferred_element_type=F32)
                acc = part if acc is None else acc + part
            o_ref[...] = acc.astype(o_ref.dtype)

        outs, passed = _call(name, (nt,), compute, [b_spec, w_spec], [b, w3], [_S((M, K), BF16)], [a_spec], [], tasks)
    else:
        def compute(a_ref, b_ref, o_ref, acc):
            i = pl.program_id(0)

            @pl.when(i == 0)
            def _():
                acc[...] = jnp.zeros_like(acc)

            at = a_ref[...].T
            for g in range(G):
                acc[g] += jnp.dot(at, b_ref[:, cols(g)], preferred_element_type=F32)

            @pl.when(i == nt - 1)
            def _():
                o_ref[...] = acc[...].astype(o_ref.dtype)

        outs, passed = _call(name, (nt,), compute, [a_spec, b_spec], [a, b], [_S((G, K, n), BF16)], [w_spec],
                             [pltpu.VMEM((G, K, n), F32)], tasks)
    return (outs[0], passed) if tasks else outs[0]


def _cast_into_slot(w, place, name, paired=False):
    R, C = w.shape
    tr = _tile(R, 512, HALO)

    def body(s_ref, w_ref, o_ref):
        del s_ref
        o_ref[...] = w_ref[...].astype(BF16)

    if paired:
        shape, out_map = (N_CHIP, R, 2 * C), lambda i, s: (s[1], i, s[0])
    else:
        shape, out_map = (N_DEV, R, C), lambda i, s: (s[2], i, 0)
    return pl.pallas_call(
        body, name=name, out_shape=_S(shape, BF16),
        grid_spec=pltpu.PrefetchScalarGridSpec(
            num_scalar_prefetch=1, grid=(R // tr,),
            in_specs=[pl.BlockSpec((tr, C), lambda i, s: (i, 0))],
            out_specs=pl.BlockSpec((None, tr, C), out_map)),
        compiler_params=_cparams(("parallel",)),
    )(place, w)


def _down(cur, prev8, j):
    return pltpu.roll(jnp.concatenate([prev8, cur], axis=0), j, 0)[SUB:, :]


def _up(cur, next8, j):
    n = cur.shape[0] + SUB
    return pltpu.roll(jnp.concatenate([cur, next8], axis=0), n - j, 0)[:cur.shape[0], :]


def _shifted_down(x, prev8, n):
    full = jnp.concatenate([prev8, x], axis=0)
    return [x] + [pltpu.roll(full, s, 0)[SUB:, :] for s in range(1, n)]


def _shifted_up(x, next8, n):
    m = x.shape[0] + SUB
    full = jnp.concatenate([x, next8], axis=0)
    return [x] + [pltpu.roll(full, m - s, 0)[:x.shape[0], :] for s in range(1, n)]


def _taps(sh, w_ref):
    kw = w_ref.shape[0]
    y = sh[0] * w_ref[pl.ds(kw - 1, 1), :]
    for k in range(kw - 1):
        y = y + sh[kw - 1 - k] * w_ref[pl.ds(k, 1), :]
    return y


def _conv(x, prev8, w_ref):
    return _taps(_shifted_down(x, prev8, w_ref.shape[0]), w_ref)


def _conv_t(dy, next8, w_ref):
    return _taps(_shifted_up(dy, next8, w_ref.shape[0]), w_ref)


def _conv_dw(dw_ref, dy, x, prev8, first):
    kw = dw_ref.shape[0]

    @pl.when(first)
    def _():
        dw_ref[...] = jnp.zeros_like(dw_ref)

    for k in range(kw):
        xs = x if k == kw - 1 else _down(x, prev8, kw - 1 - k)
        dw_ref[pl.ds(k, 1), :] += jnp.sum(dy * xs, axis=0, keepdims=True)


def _acc(ref, val, first):
    @pl.when(first)
    def _():
        ref[...] = jnp.zeros_like(ref)

    ref[...] += val


def _acc_row(ref, val, first):
    _acc(ref, jnp.sum(val, axis=0, keepdims=True), first)


def _prev8(h_ref, t):
    return jnp.where(t > 0, h_ref[...].astype(F32)[HALO - SUB:, :], 0.0)


def _next8(h_ref, is_last):
    return jnp.where(is_last, 0.0, h_ref[...].astype(F32)[:SUB, :])


_GELU_K0 = math.sqrt(2.0 / math.pi)
_GELU_K1 = 0.044715


def _gelu_and_grad(x):
    x2 = x * x
    th = jnp.tanh(_GELU_K0 * x * (1.0 + _GELU_K1 * x2))
    g = 0.5 * x * (1.0 + th)
    dg = 0.5 * (1.0 + th) + 0.5 * x * (1.0 - th * th) * (_GELU_K0 * (1.0 + 3.0 * _GELU_K1 * x2))
    return g, dg


def _neg_expm1(z):
    series = -z * (1.0 + z * (0.5 + z * (1.0 / 6.0 + z * (1.0 / 24.0))))
    return jnp.where(z > -0.03, series, 1.0 - jnp.exp(z))


def _store_staged(stages, dst_hbm, sems, step, n_steps, where):
    def copies(s, slot):
        return [pltpu.make_async_copy(
            st.at[slot], dst_hbm.at[pl.ds(r0, st.shape[1]), pl.ds(c0, st.shape[2])], sems.at[slot, k])
            for k, (st, (r0, c0)) in enumerate(zip(stages, where(s)))]

    slot = step % 2

    @pl.when(step > 0)
    def _():
        for cp in copies(step - 1, 1 - slot):
            cp.wait()

    for cp in copies(step, slot):
        cp.start()

    @pl.when(step == n_steps - 1)
    def _():
        for cp in copies(step, slot):
            cp.wait()


def _halo_prev_map(hb, col_fn):
    return lambda c, t: (jnp.maximum(t * hb - 1, 0), col_fn(c))


def _rms_fwd(x, g, name):
    T, D = x.shape
    tb = _tile(T, TB, SUB)

    def body(x_ref, g_ref, o_ref):
        xv = x_ref[...]
        rstd = lax.rsqrt(jnp.mean(xv * xv, axis=-1, keepdims=True) + EPS)
        o_ref[...] = (xv * rstd * g_ref[...]).astype(BF16)

    return pl.pallas_call(
        body, name=name, out_shape=_S((T, D), BF16), grid=(T // tb,),
        in_specs=[pl.BlockSpec((tb, D), lambda i: (i, 0)), pl.BlockSpec((1, D), lambda i: (0, 0))],
        out_specs=pl.BlockSpec((tb, D), lambda i: (i, 0)),
        compiler_params=_cparams(("parallel",)),
    )(x, g.reshape(1, D))


def _rms_bwd(x, g, dh, dres, name):
    T, D = x.shape
    tb = _tile(T, 256, SUB)

    def body(x_ref, g_ref, dh_ref, dr_ref, dx_ref, dxb_ref, dg_ref):
        i = pl.program_id(0)
        xv = x_ref[...]
        rstd = lax.rsqrt(jnp.mean(xv * xv, axis=-1, keepdims=True) + EPS)
        xn = xv * rstd
        dhv = dh_ref[...].astype(F32)
        _acc_row(dg_ref, dhv * xn, i == 0)
        dxn = dhv * g_ref[...]
        dx = dr_ref[...] + rstd * (dxn - xn * jnp.mean(dxn * xn, axis=-1, keepdims=True))
        dx_ref[...] = dx
        dxb_ref[...] = dx.astype(BF16)

    blk = pl.BlockSpec((tb, D), lambda i: (i, 0))
    vec = pl.BlockSpec((1, D), lambda i: (0, 0))
    return pl.pallas_call(
        body, name=name, out_shape=(_S((T, D), F32), _S((T, D), BF16), _S((1, D), F32)),
        grid=(T // tb,), in_specs=[blk, vec, blk, blk], out_specs=(blk, blk, vec),
        compiler_params=_cparams(("arbitrary",)),
    )(x, g.reshape(1, D), dh, dres)


def _loss_head(x2, g, target, name):
    T, D = x2.shape
    tb = _tile(T, 256, SUB)

    def body(x_ref, g_ref, t_ref, dx_ref, dxb_ref, loss_ref, dg_ref):
        i = pl.program_id(0)
        xv = x_ref[...]
        rstd = lax.rsqrt(jnp.mean(xv * xv, axis=-1, keepdims=True) + EPS)
        xn = xv * rstd
        err = xn * g_ref[...] - t_ref[...]
        part = 0.5 * jnp.sum(jnp.mean(err * err, axis=-1, keepdims=True), axis=0, keepdims=True)
        part = jnp.broadcast_to(part, (1, LANES))
        _acc(loss_ref, part, i == 0)
        dy = err * (1.0 / D)
        _acc_row(dg_ref, dy * xn, i == 0)
        dxn = dy * g_ref[...]
        dx = rstd * (dxn - xn * jnp.mean(dxn * xn, axis=-1, keepdims=True))
        dx_ref[...] = dx
        dxb_ref[...] = dx.astype(BF16)

    blk = pl.BlockSpec((tb, D), lambda i: (i, 0))
    vec = pl.BlockSpec((1, D), lambda i: (0, 0))
    return pl.pallas_call(
        body, name=name,
        out_shape=(_S((T, D), F32), _S((T, D), BF16), _S((1, LANES), F32), _S((1, D), F32)),
        grid=(T // tb,), in_specs=[blk, vec, blk],
        out_specs=(blk, blk, pl.BlockSpec((1, LANES), lambda i: (0, 0)), vec),
        compiler_params=_cparams(("arbitrary",)),
    )(x2, g.reshape(1, D), target)


def _lru_gates(xc, wa_ref, ba_ref, wx_ref, bx_ref, lam_ref):
    xcb = xc.astype(BF16)
    r = jax.nn.sigmoid(jnp.dot(xcb, wa_ref[...], preferred_element_type=F32) + ba_ref[...])
    i = jax.nn.sigmoid(jnp.dot(xcb, wx_ref[...], preferred_element_type=F32) + bx_ref[...])
    sp = jax.nn.softplus(-lam_ref[...])
    log_a = (-LRU_C * sp) * r
    a = jnp.exp(log_a)
    s = jnp.sqrt(_neg_expm1(2.0 * log_a))
    return xcb, r, i, a, s


def _lru_fwd(p, conv_w, conv_b, wa_bd, ba, wx_bd, bx, lam, *, name, tasks=()):
    T = p.shape[0]
    d = lam.shape[-1]
    C = _tile(d, C_LRU)
    nC = d // C
    tb = _tile(T, TB, HALO)
    nT, hb, nt = T // tb, tb // HALO, tb // SUB

    def body(x_ref, xh_ref, g_ref, cw_ref, cb_ref, wa_ref, ba_ref, wx_ref, bx_ref, lam_ref,
             hs_ref, y_ref, a_s, u_s, h_s):
        t = pl.program_id(1)

        @pl.when(t == 0)
        def _():
            h_s[...] = jnp.zeros_like(h_s)

        x = x_ref[...].astype(F32)
        xc = _conv(x, _prev8(xh_ref, t), cw_ref) + cb_ref[...]
        _, r, i, a, s = _lru_gates(xc, wa_ref, ba_ref, wx_ref, bx_ref, lam_ref)
        a_s[...] = a
        u_s[...] = s * (i * xc)
        row = lax.broadcasted_iota(jnp.int32, (SUB, C), 0)

        def step(k, h):
            o = pl.multiple_of(k * SUB, SUB)
            A = a_s[pl.ds(o, SUB), :]
            B = u_s[pl.ds(o, SUB), :]
            for sh in (1, 2, 4):
                m = row >= sh
                Ap = pltpu.roll(A, sh, 0)
                Bp = pltpu.roll(B, sh, 0)
                B = jnp.where(m, A * Bp + B, B)
                A = jnp.where(m, A * Ap, A)
            hs = A * h + B
            hs_ref[pl.ds(o, SUB), :] = hs
            return jnp.broadcast_to(hs[SUB - 1:SUB, :], (SUB, C))

        h_s[...] = lax.fori_loop(0, nt, step, h_s[...])
        gel, _ = _gelu_and_grad(g_ref[...].astype(F32))
        y_ref[...] = (gel * hs_ref[...]).astype(BF16)

    vec = pl.BlockSpec((1, C), lambda c, t: (0, c))
    sq = pl.BlockSpec((None, C, C), lambda c, t: (c, 0, 0))
    outs, passed = _call(
        name, (nC, nT), body,
        [pl.BlockSpec((tb, C), lambda c, t: (t, c)),
         pl.BlockSpec((HALO, C), _halo_prev_map(hb, lambda c: c)),
         pl.BlockSpec((tb, C), lambda c, t: (t, nC + c)),
         pl.BlockSpec((conv_w.shape[0], C), lambda c, t: (0, c)),
         vec, sq, vec, sq, vec, vec],
        [p, p, p, conv_w, conv_b, wa_bd, ba, wx_bd, bx, lam],
        [_S((T, d), F32), _S((T, d), BF16)],
        [pl.BlockSpec((tb, C), lambda c, t: (t, c)), pl.BlockSpec((tb, C), lambda c, t: (t, c))],
        [pltpu.VMEM((tb, C), F32), pltpu.VMEM((tb, C), F32), pltpu.VMEM((SUB, C), F32)], tasks)
    return (*outs, passed) if tasks else outs


def _lru_bwd(p, hs, dyl, dp, conv_w, conv_b, wa_bd, ba, wx_bd, bx, lam, *, name, tasks=()):
    T = p.shape[0]
    d = lam.shape[-1]
    C = _tile(d, C_LRU)
    nC = d // C
    tb = _tile(T, TB, HALO)
    nT, hb, nt = T // tb, tb // HALO, tb // SUB
    kw = conv_w.shape[0]

    def body(x_ref, xh_ref, g_ref, hs_ref, hh_ref, dy_ref, cw_ref, cb_ref, wa_ref, ba_ref, wx_ref, bx_ref,
             lam_ref, dp_in, dp_ref, dcw_ref, dcb_ref, dwa_ref, dba_ref, dwx_ref, dbx_ref, dlam_ref,
             b_s, g_s, dh_s, an_s, dhn_s, dxn_s, st_x, st_g, sems):
        del dp_in
        c = pl.program_id(0)
        tr = pl.program_id(1)
        t = nT - 1 - tr
        first = tr == 0

        @pl.when(first)
        def _():
            an_s[...] = jnp.zeros_like(an_s)
            dhn_s[...] = jnp.zeros_like(dhn_s)
            dxn_s[...] = jnp.zeros_like(dxn_s)

        x = x_ref[...].astype(F32)
        xprev = _prev8(xh_ref, t)
        xc = _conv(x, xprev, cw_ref) + cb_ref[...]
        xcb, r, i, a, s = _lru_gates(xc, wa_ref, ba_ref, wx_ref, bx_ref, lam_ref)
        hsv = hs_ref[...]
        dy = dy_ref[...].astype(F32)
        gel, dgel = _gelu_and_grad(g_ref[...].astype(F32))
        step_no = c * nT + tr
        slot = step_no % 2
        st_g[slot] = (dy * hsv * dgel).astype(BF16)

        b_s[...] = _up(a, an_s[...], 1)
        g_s[...] = dy * gel
        row = lax.broadcasted_iota(jnp.int32, (SUB, C), 0)

        def step(k, carry):
            o = pl.multiple_of((nt - 1 - k) * SUB, SUB)
            B = b_s[pl.ds(o, SUB), :]
            G = g_s[pl.ds(o, SUB), :]
            for sh in (1, 2, 4):
                m = row < SUB - sh
                Bn = pltpu.roll(B, SUB - sh, 0)
                Gn = pltpu.roll(G, SUB - sh, 0)
                G = jnp.where(m, B * Gn + G, G)
                B = jnp.where(m, B * Bn, B)
            dh = B * carry + G
            dh_s[pl.ds(o, SUB), :] = dh
            return jnp.broadcast_to(dh[0:1, :], (SUB, C))

        dhn_s[...] = lax.fori_loop(0, nt, step, dhn_s[...])
        an_s[...] = a[:SUB, :]
        dh = dh_s[...]

        hprev = _down(hsv, jnp.where(t > 0, hh_ref[...][HALO - SUB:, :], 0.0), 1)
        d_a = dh * hprev
        ixc = i * xc
        d_s = dh * ixc
        d_i = dh * s * xc
        d_xc = dh * s * i
        d_l = d_a * a - d_s * (a * a) / s
        sp = jax.nn.softplus(-lam_ref[...])
        _acc_row(dlam_ref, d_l * r * (LRU_C * jax.nn.sigmoid(-lam_ref[...])), first)
        d_zr = (d_l * (-LRU_C * sp)) * r * (1.0 - r)
        d_zi = d_i * i * (1.0 - i)
        _acc_row(dba_ref, d_zr, first)
        _acc_row(dbx_ref, d_zi, first)
        d_zrb = d_zr.astype(BF16)
        d_zib = d_zi.astype(BF16)
        tn_dims = (((0,), (0,)), ((), ()))
        nt_dims = (((1,), (1,)), ((), ()))
        gwa = lax.dot_general(xcb, d_zrb, tn_dims, preferred_element_type=F32)
        gwx = lax.dot_general(xcb, d_zib, tn_dims, preferred_element_type=F32)
        _acc(dwa_ref, gwa, first)
        _acc(dwx_ref, gwx, first)
        d_xc = (d_xc + lax.dot_general(d_zrb, wa_ref[...], nt_dims, preferred_element_type=F32)
                + lax.dot_general(d_zib, wx_ref[...], nt_dims, preferred_element_type=F32))
        _acc_row(dcb_ref, d_xc, first)
        _conv_dw(dcw_ref, d_xc, x, xprev, first)
        st_x[slot] = _conv_t(d_xc, dxn_s[...], cw_ref).astype(BF16)
        dxn_s[...] = d_xc[:SUB, :]

        def where(s):
            row0, col0 = (nT - 1 - s % nT) * tb, (s // nT) * C
            return [(row0, col0), (row0, d + col0)]

        _store_staged([st_x, st_g], dp_ref, sems, step_no, nC * nT, where)

    rev = lambda c, tr: (nT - 1 - tr, c)
    vec = pl.BlockSpec((1, C), lambda c, tr: (0, c))
    sq = pl.BlockSpec((None, C, C), lambda c, tr: (c, 0, 0))
    cwb = pl.BlockSpec((kw, C), lambda c, tr: (0, c))
    halo_prev = lambda c, tr: (jnp.maximum((nT - 1 - tr) * hb - 1, 0), c)
    outs, passed = _call(
        name, (nC, nT), body,
        [pl.BlockSpec((tb, C), rev),
         pl.BlockSpec((HALO, C), halo_prev),
         pl.BlockSpec((tb, C), lambda c, tr: (nT - 1 - tr, nC + c)),
         pl.BlockSpec((tb, C), rev),
         pl.BlockSpec((HALO, C), halo_prev),
         pl.BlockSpec((tb, C), rev),
         cwb, vec, sq, vec, sq, vec, vec, ANY],
        [p, p, p, hs, hs, dyl, conv_w, conv_b, wa_bd, ba, wx_bd, bx, lam, dp],
        [_S(dp.shape, dp.dtype), _S((kw, d), F32), _S((1, d), F32), _S((nC, C, C), F32), _S((1, d), F32),
         _S((nC, C, C), F32), _S((1, d), F32), _S((1, d), F32)],
        [ANY, cwb, vec, sq, vec, sq, vec, vec],
        [pltpu.VMEM((tb, C), F32), pltpu.VMEM((tb, C), F32), pltpu.VMEM((tb, C), F32),
         pltpu.VMEM((SUB, C), F32), pltpu.VMEM((SUB, C), F32), pltpu.VMEM((SUB, C), F32),
         pltpu.VMEM((2, tb, C), BF16), pltpu.VMEM((2, tb, C), BF16), pltpu.SemaphoreType.DMA((2, 2))],
        tasks, own_aliases={13: 0})
    return (*outs, passed) if tasks else outs


def _sc_fwd(p, conv_w, *, d, name):
    T = p.shape[0]
    C = _tile(d, C_EW)
    nC = d // C
    tb = _tile(T, TB, HALO)
    nT, hb = T // tb, tb // HALO

    def body(b_ref, c_ref, ch_ref, v_ref, vh_ref, w_ref, y_ref):
        t = pl.program_id(1)
        cv = c_ref[...].astype(F32) * v_ref[...].astype(F32)
        cvp = _prev8(ch_ref, t) * _prev8(vh_ref, t)
        y_ref[...] = (b_ref[...].astype(F32) * _conv(cv, cvp, w_ref)).astype(BF16)

    seg = lambda k: pl.BlockSpec((tb, C), lambda c, t: (t, k * nC + c))
    hseg = lambda k: pl.BlockSpec((HALO, C), _halo_prev_map(hb, lambda c: k * nC + c))
    return pl.pallas_call(
        body, name=name, out_shape=_S((T, d), BF16), grid=(nC, nT),
        in_specs=[seg(2), seg(3), hseg(3), seg(4), hseg(4), pl.BlockSpec((conv_w.shape[0], C), lambda c, t: (0, c))],
        out_specs=pl.BlockSpec((tb, C), lambda c, t: (t, c)),
        compiler_params=_cparams(("parallel", "parallel")),
    )(p, p, p, p, p, conv_w)


def _sc_bwd(p, dys, dp, conv_w, *, d, name, tasks=()):
    T = p.shape[0]
    C = _tile(d, C_EW)
    nC = d // C
    tb = _tile(T, TB, HALO)
    nT, hb = T // tb, tb // HALO
    kw = conv_w.shape[0]

    def body(b_ref, bn_ref, c_ref, ch_ref, v_ref, vh_ref, dy_ref, dyn_ref, w_ref, dp_in, dp_ref, dw_ref,
             st_b, st_c, st_v, sems):
        del dp_in
        c = pl.program_id(0)
        t = pl.program_id(1)
        last = t == nT - 1
        bv = b_ref[...].astype(F32)
        cvv = c_ref[...].astype(F32)
        vv = v_ref[...].astype(F32)
        dy = dy_ref[...].astype(F32)
        cv = cvv * vv
        cvp = _prev8(ch_ref, t) * _prev8(vh_ref, t)
        step_no = c * nT + t
        slot = step_no % 2
        st_b[slot] = (dy * _conv(cv, cvp, w_ref)).astype(BF16)
        dz = dy * bv
        dzn = _next8(dyn_ref, last) * _next8(bn_ref, last)
        _conv_dw(dw_ref, dz, cv, cvp, t == 0)
        dcv = _conv_t(dz, dzn, w_ref)
        st_c[slot] = (dcv * vv).astype(BF16)
        st_v[slot] = (dcv * cvv).astype(BF16)

        def where(s):
            return [((s % nT) * tb, (2 + k) * d + (s // nT) * C) for k in range(3)]

        _store_staged([st_b, st_c, st_v], dp_ref, sems, step_no, nC * nT, where)

    seg = lambda k: pl.BlockSpec((tb, C), lambda c, t: (t, k * nC + c))
    hseg = lambda k: pl.BlockSpec((HALO, C), _halo_prev_map(hb, lambda c: k * nC + c))
    last_h = T // HALO - 1
    nseg = lambda k: pl.BlockSpec((HALO, C), lambda c, t: (jnp.minimum((t + 1) * hb, last_h), k * nC + c))
    outs, passed = _call(
        name, (nC, nT), body,
        [seg(2), nseg(2), seg(3), hseg(3), seg(4), hseg(4),
         pl.BlockSpec((tb, C), lambda c, t: (t, c)), nseg(0),
         pl.BlockSpec((kw, C), lambda c, t: (0, c)), ANY],
        [p, p, p, p, p, p, dys, dys, conv_w, dp],
        [_S(dp.shape, dp.dtype), _S((kw, d), F32)], [ANY, pl.BlockSpec((kw, C), lambda c, t: (0, c))],
        [pltpu.VMEM((2, tb, C), BF16)] * 3 + [pltpu.SemaphoreType.DMA((2, 3))], tasks, own_aliases={9: 0})
    return (*outs, passed) if tasks else outs


def _merge_fwd(p, y_lru, y_sc, *, col0, name, tasks=()):
    T, D = y_lru.shape
    C = _tile(math.gcd(D, col0), 1024)
    nC = D // C
    k0 = col0 // C
    tb = _tile(T, 256, HALO)

    def body(gl_ref, gs_ref, yl_ref, ys_ref, o_ref):
        @pl.loop(0, tb // HALO)
        def _(k):
            rows = pl.ds(pl.multiple_of(k * HALO, HALO), HALO)
            for l0 in range(0, C, min(C, C_EW)):
                at = (rows, pl.ds(l0, min(C, C_EW)))
                o_ref[at] = (jax.nn.sigmoid(gl_ref[at].astype(F32)) * yl_ref[at].astype(F32)
                             + jax.nn.sigmoid(gs_ref[at].astype(F32)) * ys_ref[at].astype(F32)).astype(BF16)

    blk = pl.BlockSpec((tb, C), lambda c, t: (t, c))
    outs, passed = _call(
        name, (nC, T // tb), body,
        [pl.BlockSpec((tb, C), lambda c, t: (t, k0 + c)), pl.BlockSpec((tb, C), lambda c, t: (t, k0 + nC + c)),
         blk, blk], [p, p, y_lru, y_sc], [_S((T, D), BF16)], [blk], [], tasks)
    return (outs[0], passed) if tasks else outs[0]


def _merge_bwd(p, y_lru, y_sc, dm, *, col0, name, tasks=()):
    T, D = y_lru.shape
    C = _tile(math.gcd(D, col0), 1024)
    nC = D // C
    k0 = col0 // C
    tb = _tile(T, 256, HALO)
    nT = T // tb

    def body(gl_ref, gs_ref, yl_ref, ys_ref, dm_ref, dp_ref, dyl_ref, dys_ref, st_l, st_s, sems):
        step_no = pl.program_id(0) * nT + pl.program_id(1)
        slot = step_no % 2

        @pl.loop(0, tb // HALO)
        def _(k):
            rows = pl.ds(pl.multiple_of(k * HALO, HALO), HALO)
            for l0 in range(0, C, min(C, C_EW)):
                at = (rows, pl.ds(l0, min(C, C_EW)))
                dmv = dm_ref[at].astype(F32)
                sl = jax.nn.sigmoid(gl_ref[at].astype(F32))
                ss = jax.nn.sigmoid(gs_ref[at].astype(F32))
                dyl_ref[at] = (dmv * sl).astype(BF16)
                dys_ref[at] = (dmv * ss).astype(BF16)
                st_l[(slot,) + at] = (dmv * yl_ref[at].astype(F32) * sl * (1.0 - sl)).astype(BF16)
                st_s[(slot,) + at] = (dmv * ys_ref[at].astype(F32) * ss * (1.0 - ss)).astype(BF16)

        def where(s):
            row0, colc = (s % nT) * tb, (s // nT) * C
            return [(row0, col0 + colc), (row0, col0 + D + colc)]

        _store_staged([st_l, st_s], dp_ref, sems, step_no, nC * nT, where)

    blk = pl.BlockSpec((tb, C), lambda c, t: (t, c))
    outs, passed = _call(
        name, (nC, nT), body,
        [pl.BlockSpec((tb, C), lambda c, t: (t, k0 + c)), pl.BlockSpec((tb, C), lambda c, t: (t, k0 + nC + c)),
         blk, blk, blk], [p, p, y_lru, y_sc, dm],
        [_S(p.shape, BF16), _S((T, D), BF16), _S((T, D), BF16)], [ANY, blk, blk],
        [pltpu.VMEM((2, tb, C), BF16), pltpu.VMEM((2, tb, C), BF16), pltpu.SemaphoreType.DMA((2, 2))], tasks)
    return (*outs, passed) if tasks else outs


def _ffn_act_fwd(uu, conv_w, *, name, tasks=()):
    T = uu.shape[0]
    F = uu.shape[1] // 2
    C = _tile(F, C_EW)
    nC = F // C
    tb = _tile(T, TB, HALO)
    nT, hb = T // tb, tb // HALO
    kw = conv_w.shape[0]
    R = HALO

    def body(g_ref, gh_ref, v_ref, vh_ref, wg_ref, wv_ref, o_ref):
        t = pl.program_id(1)

        def chunk(k, carry):
            gp, vp = carry
            r0 = pl.multiple_of(k * R, R)
            ug = g_ref[pl.ds(r0, R), :].astype(F32)
            uv = v_ref[pl.ds(r0, R), :].astype(F32)
            cg = _conv(ug, gp, wg_ref)
            cv = _conv(uv, vp, wv_ref)
            o_ref[pl.ds(r0, R), :] = (cg * jax.nn.sigmoid(cg) * cv).astype(BF16)
            return ug[R - SUB:, :], uv[R - SUB:, :]

        lax.fori_loop(0, tb // R, chunk, (_prev8(gh_ref, t), _prev8(vh_ref, t)))

    seg = lambda k: pl.BlockSpec((tb, C), lambda c, t: (t, k * nC + c))
    hseg = lambda k: pl.BlockSpec((HALO, C), _halo_prev_map(hb, lambda c: k * nC + c))
    wseg = lambda k: pl.BlockSpec((kw, C), lambda c, t: (0, k * nC + c))
    outs, passed = _call(
        name, (nC, nT), body, [seg(0), hseg(0), seg(1), hseg(1), wseg(0), wseg(1)],
        [uu, uu, uu, uu, conv_w, conv_w], [_S((T, F), BF16)], [pl.BlockSpec((tb, C), lambda c, t: (t, c))], [], tasks)
    return (outs[0], passed) if tasks else outs[0]


def _ffn_act_bwd(uu, dact, conv_w, *, name):
    T = uu.shape[0]
    F = uu.shape[1] // 2
    C = _tile(F, C_EW)
    nC = F // C
    tb = _tile(T, TB, HALO)
    nT, hb = T // tb, tb // HALO
    kw = conv_w.shape[0]
    R = HALO
    nk = tb // R

    def body(g_ref, gh_ref, v_ref, vh_ref, da_ref, wg_ref, wv_ref, du_ref, dwg_ref, dwv_ref,
             gn_s, vn_s, accg_s, accv_s, st_g, st_v, sems):
        c = pl.program_id(0)
        tr = pl.program_id(1)
        t = nT - 1 - tr
        first = tr == 0

        @pl.when(first)
        def _():
            gn_s[...] = jnp.zeros_like(gn_s)
            vn_s[...] = jnp.zeros_like(vn_s)
            dwg_ref[...] = jnp.zeros_like(dwg_ref)
            dwv_ref[...] = jnp.zeros_like(dwv_ref)

        accg_s[...] = jnp.zeros_like(accg_s)
        accv_s[...] = jnp.zeros_like(accv_s)
        step_no = c * nT + tr
        slot = step_no % 2

        def chunk(i, carry):
            gn, vn = carry
            k = nk - 1 - i
            r0 = pl.multiple_of(k * R, R)
            rp = pl.multiple_of(jnp.maximum(r0 - R, 0), R)
            ug = g_ref[pl.ds(r0, R), :].astype(F32)
            uv = v_ref[pl.ds(r0, R), :].astype(F32)
            gp = jnp.where(k > 0, g_ref[pl.ds(rp, R), :].astype(F32)[R - SUB:, :], _prev8(gh_ref, t))
            vp = jnp.where(k > 0, v_ref[pl.ds(rp, R), :].astype(F32)[R - SUB:, :], _prev8(vh_ref, t))
            sh_g = _shifted_down(ug, gp, kw)
            sh_v = _shifted_down(uv, vp, kw)
            cg = _taps(sh_g, wg_ref)
            cv = _taps(sh_v, wv_ref)
            da = da_ref[pl.ds(r0, R), :].astype(F32)
            sg = jax.nn.sigmoid(cg)
            d_cg = da * cv * (sg * (1.0 + cg * (1.0 - sg)))
            d_cv = da * (cg * sg)
            for j in range(kw):
                accg_s[j] += d_cg * sh_g[kw - 1 - j]
                accv_s[j] += d_cv * sh_v[kw - 1 - j]
            st_g[slot, pl.ds(r0, R), :] = _conv_t(d_cg, gn, wg_ref).astype(BF16)
            st_v[slot, pl.ds(r0, R), :] = _conv_t(d_cv, vn, wv_ref).astype(BF16)
            return d_cg[:SUB, :], d_cv[:SUB, :]

        gn, vn = lax.fori_loop(0, nk, chunk, (gn_s[...], vn_s[...]))
        gn_s[...] = gn
        vn_s[...] = vn
        for j in range(kw):
            dwg_ref[pl.ds(j, 1), :] += jnp.sum(accg_s[j], axis=0, keepdims=True)
            dwv_ref[pl.ds(j, 1), :] += jnp.sum(accv_s[j], axis=0, keepdims=True)
        def where(s):
            row0, col0 = (nT - 1 - s % nT) * tb, (s // nT) * C
            return [(row0, col0), (row0, F + col0)]

        _store_staged([st_g, st_v], du_ref, sems, step_no, nC * nT, where)

    seg = lambda k: pl.BlockSpec((tb, C), lambda c, tr: (nT - 1 - tr, k * nC + c))
    hseg = lambda k: pl.BlockSpec((HALO, C), lambda c, tr: (jnp.maximum((nT - 1 - tr) * hb - 1, 0), k * nC + c))
    wseg = lambda k: pl.BlockSpec((kw, C), lambda c, tr: (0, k * nC + c))
    dwb = pl.BlockSpec((kw, C), lambda c, tr: (0, c))
    return pl.pallas_call(
        body, name=name, out_shape=(_S(uu.shape, BF16), _S((kw, F), F32), _S((kw, F), F32)), grid=(nC, nT),
        in_specs=[seg(0), hseg(0), seg(1), hseg(1), pl.BlockSpec((tb, C), lambda c, tr: (nT - 1 - tr, c)),
                  wseg(0), wseg(1)],
        out_specs=(ANY, dwb, dwb),
        scratch_shapes=[pltpu.VMEM((SUB, C), F32), pltpu.VMEM((SUB, C), F32),
                        pltpu.VMEM((kw, R, C), F32), pltpu.VMEM((kw, R, C), F32),
                        pltpu.VMEM((2, tb, C), BF16), pltpu.VMEM((2, tb, C), BF16), pltpu.SemaphoreType.DMA((2, 2))],
        compiler_params=_cparams(("arbitrary", "arbitrary")),
    )(uu, uu, uu, uu, dact, conv_w, conv_w)


def _place():
    x, y, c = lax.axis_index("x"), lax.axis_index("y"), lax.axis_index("c")
    return x, y, c


def _chips(x, y):
    return [(1 - x, y), (x, 1 - y), (1 - x, 1 - y)]


def _all_gather(arrays, placed, over_ici, pair_n, name):
    n = len(arrays)

    def body(*refs):
        ins, outs = refs[:n], refs[n:2 * n]
        send_sems, recv_sems, local_sems = refs[2 * n:]
        x, y, c = _place()
        me, sibling = (x, y, c), (x, y, 1 - c)
        chips = _chips(x, y)
        full = [a for a in range(n) if over_ici[a]]

        def idx(px, py, pc):
            return 4 * px + 2 * py + pc

        def copy(a, k, block, to):
            dst = _dev_block(outs[a], idx(*block), pair_n[a])
            src = ins[a] if (block is me and not placed[a]) else dst
            return pltpu.make_async_remote_copy(
                src_ref=src, dst_ref=dst, send_sem=send_sems.at[a, k], recv_sem=recv_sems.at[a, k],
                device_id=to, device_id_type=MESH)

        def half(a, k, block, to, lo):
            r = rows_of[a] // 2
            blk = _rows_of(outs[a], idx(*block), (0 if lo else r, r), pair_n[a])
            return pltpu.make_async_remote_copy(
                src_ref=blk, dst_ref=blk, send_sem=send_sems.at[a, k], recv_sem=recv_sems.at[a, k],
                device_id=to, device_id_type=MESH)

        mine = [pltpu.make_async_copy(ins[a], outs[a].at[idx(*me)], local_sems.at[a])
                for a in range(n) if not placed[a]]
        for cp in mine:
            cp.start()
        chip_x, chip_y, chip_d = chips
        sent = []
        for a in full:
            sent += [copy(a, 1, me, (*chip_x, c)), copy(a, 2, me, (*chip_y, c))]
            if not relay[a]:
                sent.append(copy(a, 3, me, (*chip_d, c)))
        for a in range(n):
            sent.append(copy(a, 0, me, sibling))
        for cp in sent:
            cp.start()

        def then(cp):
            cp.start()
            sent.append(cp)

        for a in full:
            copy(a, 2, (*chip_y, c), me).wait_recv()
            if relay[a]:
                then(half(a, 3, (*chip_y, c), (*chip_x, c), True))
            then(copy(a, 6, (*chip_y, c), sibling))
            copy(a, 1, (*chip_x, c), me).wait_recv()
            if relay[a]:
                then(half(a, 4, (*chip_x, c), (*chip_y, c), False))
            then(copy(a, 5, (*chip_x, c), sibling))
        for a in full:
            if relay[a]:
                half(a, 3, (*chip_d, c), me, True).wait_recv()
                half(a, 4, (*chip_d, c), me, False).wait_recv()
            else:
                copy(a, 3, (*chip_d, c), me).wait_recv()
            then(copy(a, 7, (*chip_d, c), sibling))
        for a in range(n):
            copy(a, 0, sibling, me).wait_recv()
        for a in full:
            for j, chip in enumerate(chips):
                copy(a, 5 + j, (*chip, 1 - c), me).wait_recv()
        for cp in sent:
            cp.wait_send()
        for cp in mine:
            cp.wait()

    rows_of = [(s.shape[1] if placed[a] else s.shape[0]) for a, s in enumerate(arrays)]
    relay = [r % (2 * HALO) == 0 for r in rows_of]
    return pl.pallas_call(
        body, name=name,
        out_shape=tuple(_S(s.shape if placed[a] else (N_DEV,) + s.shape, s.dtype) for a, s in enumerate(arrays)),
        in_specs=[ANY] * n, out_specs=tuple([ANY] * n),
        scratch_shapes=[pltpu.SemaphoreType.DMA((n, 8)), pltpu.SemaphoreType.DMA((n, 8)),
                        pltpu.SemaphoreType.DMA((n,))],
        input_output_aliases={a: a for a in range(n) if placed[a]},
    )(*arrays)


def _dev_block(ref, dev, pair_n=None):
    if pair_n is None:
        return ref.at[dev]
    return ref.at[dev // 2, :, pl.ds(pl.multiple_of((dev % 2) * pair_n, LANES), pair_n)]


def _rows_of(ref, blk, rows, pair_n=None):
    v = _dev_block(ref, blk, pair_n)
    return v if rows is None else v.at[pl.ds(rows[0], rows[1])]


ALL_ROWS = "all"


def _gather_task(buf, ici=None, fwd=None, pair_n=None):
    rows = lambda r: None if r == ALL_ROWS else r
    blk_of = functools.partial(_rows_of, pair_n=pair_n)

    def copies(refs, ss, rs):
        x, y, c = _place()
        me = 4 * x + 2 * y + c
        cps = []
        for j, (px, py) in enumerate(_chips(x, y)):
            if ici is not None:
                blk = blk_of(refs[0], me, rows(ici))
                cps.append(pltpu.make_async_remote_copy(
                    src_ref=blk, dst_ref=blk, send_sem=ss.at[j], recv_sem=rs.at[j],
                    device_id=(px, py, c), device_id_type=MESH))
            if fwd is not None:
                blk = blk_of(refs[0], 4 * px + 2 * py + c, rows(fwd))
                cps.append(pltpu.make_async_remote_copy(
                    src_ref=blk, dst_ref=blk, send_sem=ss.at[3 + j], recv_sem=rs.at[3 + j],
                    device_id=(x, y, 1 - c), device_id_type=MESH))
        return cps

    def start(refs, ss, rs, ls):
        for cp in copies(refs, ss, rs):
            cp.start()

    def wait(refs, ss, rs, ls):
        x, y, c = _place()
        for j, (px, py) in enumerate(_chips(x, y)):
            if ici is not None:
                blk = blk_of(refs[0], 4 * px + 2 * py + c, rows(ici))
                pltpu.make_async_remote_copy(
                    src_ref=blk, dst_ref=blk, send_sem=ss.at[j], recv_sem=rs.at[j],
                    device_id=(px, py, c), device_id_type=MESH).wait_recv()
            if fwd is not None:
                blk = blk_of(refs[0], 4 * px + 2 * py + 1 - c, rows(fwd))
                pltpu.make_async_remote_copy(
                    src_ref=blk, dst_ref=blk, send_sem=ss.at[3 + j], recv_sem=rs.at[3 + j],
                    device_id=(x, y, 1 - c), device_id_type=MESH).wait_recv()
        for cp in copies(refs, ss, rs):
            cp.wait_send()

    return _Task([buf], [0], start, wait, nsem=6)


def _exchange_task(parts, landing, rows=None):
    def copies(refs, ss, rs):
        x, y, c = _place()
        myq = 2 * x + y
        return [pltpu.make_async_remote_copy(
            src_ref=_rows_of(refs[0], 2 * px + py, rows), dst_ref=_rows_of(refs[1], myq, rows),
            send_sem=ss.at[k], recv_sem=rs.at[k], device_id=(px, py, c), device_id_type=MESH)
            for k, (px, py) in enumerate(_chips(x, y))]

    def start(refs, ss, rs, ls):
        for cp in copies(refs, ss, rs):
            cp.start()

    def wait(refs, ss, rs, ls):
        x, y, c = _place()
        for k, (px, py) in enumerate(_chips(x, y)):
            pltpu.make_async_remote_copy(
                src_ref=_rows_of(refs[0], 2 * x + y, rows), dst_ref=_rows_of(refs[1], 2 * px + py, rows),
                send_sem=ss.at[k], recv_sem=rs.at[k], device_id=(px, py, c), device_id_type=MESH).wait_recv()
        for cp in copies(refs, ss, rs):
            cp.wait_send()

    return _Task([parts, landing], [1], start, wait)


def _core_blocks(g, pair_n):
    if pair_n is None:
        g4 = g.reshape((N_CHIP, 2) + g.shape[1:])
        return g4, (N_CHIP,) + g.shape[1:], lambda ref, c: ref.at[:, c]
    view = lambda ref, c: ref.at[:, :, pl.ds(pl.multiple_of(c * pair_n, LANES), pair_n)]
    return g, (N_CHIP, g.shape[1], pair_n), view


def _swap_task(g, pair_n=None):
    g4, shape, view = _core_blocks(g, pair_n)

    def copy(refs, ss, rs):
        x, y, c = _place()
        return pltpu.make_async_remote_copy(
            src_ref=view(refs[0], 1 - c), dst_ref=refs[1], send_sem=ss.at[0], recv_sem=rs.at[0],
            device_id=(x, y, 1 - c), device_id_type=MESH)

    def start(refs, ss, rs, ls):
        copy(refs, ss, rs).start()

    def wait(refs, ss, rs, ls):
        copy(refs, ss, rs).wait()

    return _Task([g4], [], start, wait, fresh=[_S(shape, g.dtype)], nsem=1)


def _peer(x, y, c, m):
    return x ^ (m >> 2), y ^ ((m >> 1) & 1), c ^ (m & 1)


def _bcast_task(pack):
    def copies(refs, ss, rs):
        x, y, c = _place()
        me = 4 * x + 2 * y + c
        return [pltpu.make_async_remote_copy(
            src_ref=refs[0], dst_ref=refs[1].at[me], send_sem=ss.at[m - 1], recv_sem=rs.at[m - 1],
            device_id=_peer(x, y, c, m), device_id_type=MESH) for m in range(1, N_DEV)]

    def local(refs, ls):
        x, y, c = _place()
        return pltpu.make_async_copy(refs[0], refs[1].at[4 * x + 2 * y + c], ls.at[0])

    def start(refs, ss, rs, ls):
        local(refs, ls).start()
        for cp in copies(refs, ss, rs):
            cp.start()

    def wait(refs, ss, rs, ls):
        x, y, c = _place()
        for m in range(1, N_DEV):
            px, py, pc = _peer(x, y, c, m)
            pltpu.make_async_remote_copy(
                src_ref=refs[0], dst_ref=refs[1].at[4 * px + 2 * py + pc], send_sem=ss.at[m - 1],
                recv_sem=rs.at[m - 1], device_id=(px, py, pc), device_id_type=MESH).wait_recv()
        for cp in copies(refs, ss, rs):
            cp.wait_send()
        local(refs, ls).wait()

    return _Task([pack], [], start, wait, fresh=[_S((N_DEV,) + pack.shape, pack.dtype)], nsem=N_DEV - 1)


def _sum_packs(packs, name):
    _, R, L = packs.shape

    def body(p_ref, o_ref):
        acc = p_ref[0]
        for k in range(1, N_DEV):
            acc = acc + p_ref[k]
        o_ref[...] = acc

    return pl.pallas_call(body, name=name, out_shape=_S((R, L), packs.dtype), in_specs=[VMEM_SPEC],
                          out_specs=VMEM_SPEC, compiler_params=_cparams())(packs)


def _swap_halves(g, name, pair_n=None):
    g4, shape, view = _core_blocks(g, pair_n)

    def body(g_ref, o_ref, send_sem, recv_sem):
        x, y, c = _place()
        cp = pltpu.make_async_remote_copy(
            src_ref=view(g_ref, 1 - c), dst_ref=o_ref, send_sem=send_sem, recv_sem=recv_sem,
            device_id=(x, y, 1 - c), device_id_type=MESH)
        cp.start()
        cp.wait()

    return pl.pallas_call(
        body, name=name, out_shape=_S(shape, g.dtype), in_specs=[ANY], out_specs=ANY,
        scratch_shapes=[pltpu.SemaphoreType.DMA, pltpu.SemaphoreType.DMA],
    )(g4)


def _add_halves(g, landed, place, name, pair_n=None):
    _, r, cc = landed.shape
    tr = _tile(r, 512, HALO)
    if pair_n is None:
        g4 = g.reshape(N_CHIP, 2, r, cc)
        g_spec = pl.BlockSpec((None, None, tr, cc), lambda i, q, s: (q, s[0], i, 0))
    else:
        g4 = g
        g_spec = pl.BlockSpec((None, tr, cc), lambda i, q, s: (q, i, s[0]))

    def body(s_ref, g_ref, l_ref, o_ref, land_ref):
        q = pl.program_id(1)
        v = (g_ref[...].astype(F32) + l_ref[...].astype(F32)).astype(BF16)
        o_ref[...] = v

        @pl.when(q == s_ref[1])
        def _():
            land_ref[...] = v

    return pl.pallas_call(
        body, name=name, out_shape=(_S((N_CHIP, r, cc), BF16), _S((N_CHIP, r, cc), BF16)),
        grid_spec=pltpu.PrefetchScalarGridSpec(
            num_scalar_prefetch=1, grid=(r // tr, N_CHIP),
            in_specs=[g_spec,
                      pl.BlockSpec((None, tr, cc), lambda i, q, s: (q, i, 0))],
            out_specs=(pl.BlockSpec((None, tr, cc), lambda i, q, s: (q, i, 0)),
                       pl.BlockSpec((None, tr, cc), lambda i, q, s: (s[1], i, 0)))),
        compiler_params=_cparams(("arbitrary", "arbitrary")),
    )(place, g4, landed)


def _all_reduce_small(pack, name):
    R = pack.shape[0]

    def body(p_ref, o_ref, buf, send_sems, recv_sems):
        x, y, c = _place()
        me = 4 * x + 2 * y + c
        buf[me] = p_ref[...]
        cps = []
        for k in range(N_DEV - 1):
            m = k + 1
            peer = (x ^ (m >> 2), y ^ ((m >> 1) & 1), c ^ (m & 1))
            cps.append(pltpu.make_async_remote_copy(
                src_ref=p_ref, dst_ref=buf.at[me], send_sem=send_sems.at[k], recv_sem=recv_sems.at[k],
                device_id=peer, device_id_type=MESH))
        for cp in cps:
            cp.start()
        for k in range(N_DEV - 1):
            m = k + 1
            peer_idx = 4 * (x ^ (m >> 2)) + 2 * (y ^ ((m >> 1) & 1)) + (c ^ (m & 1))
            pltpu.make_async_remote_copy(
                src_ref=p_ref, dst_ref=buf.at[peer_idx], send_sem=send_sems.at[k], recv_sem=recv_sems.at[k],
                device_id=(x, y, c), device_id_type=MESH).wait_recv()
        for cp in cps:
            cp.wait_send()
        acc = buf[0]
        for k in range(1, N_DEV):
            acc = acc + buf[k]
        o_ref[...] = acc

    return pl.pallas_call(
        body, name=name, out_shape=_S((R, LANES), F32),
        in_specs=[VMEM_SPEC], out_specs=VMEM_SPEC,
        scratch_shapes=[pltpu.VMEM((N_DEV, R, LANES), F32), pltpu.SemaphoreType.DMA((N_DEV - 1,)),
                        pltpu.SemaphoreType.DMA((N_DEV - 1,))],
        compiler_params=_cparams(),
    )(pack)


def _adamw_math(w, g, m, v):
    m = ADAM_B1 * m + (1.0 - ADAM_B1) * g
    v = ADAM_B2 * v + (1.0 - ADAM_B2) * (g * g)
    m_hat = m / (1.0 - ADAM_B1 ** ADAM_STEP)
    v_hat = v / (1.0 - ADAM_B2 ** ADAM_STEP)
    delta = -ADAM_LR * (m_hat / (jnp.sqrt(v_hat) + ADAM_EPS) + ADAM_WD * w)
    return delta, m, v


def _adamw_big(parts, w, m, v, name):
    r, cc = w.shape
    tr = _tile(r, 128, HALO)

    def body(p_ref, w_ref, m_ref, v_ref, g_ref, d_ref, nm_ref, nv_ref):
        g = p_ref[0].astype(F32)
        for q in range(1, N_CHIP):
            g = g + p_ref[q].astype(F32)
        g_ref[...] = g
        d_ref[...], nm_ref[...], nv_ref[...] = _adamw_math(w_ref[...], g, m_ref[...], v_ref[...])

    blk = pl.BlockSpec((tr, cc), lambda i: (i, 0))
    return pl.pallas_call(
        body, name=name, out_shape=tuple(_S((r, cc), F32) for _ in range(4)), grid=(r // tr,),
        in_specs=[pl.BlockSpec((N_CHIP, tr, cc), lambda i: (0, i, 0)), blk, blk, blk],
        out_specs=(blk, blk, blk, blk), compiler_params=_cparams(("parallel",)),
    )(parts, w, m, v)


def _adamw_small(ws, gs, ms, vs, name):
    n = len(ws)

    def body(*refs):
        w_r, g_r, m_r, v_r = refs[:n], refs[n:2 * n], refs[2 * n:3 * n], refs[3 * n:4 * n]
        d_r, nm_r, nv_r = refs[4 * n:5 * n], refs[5 * n:6 * n], refs[6 * n:7 * n]
        for k in range(n):
            d_r[k][...], nm_r[k][...], nv_r[k][...] = _adamw_math(w_r[k][...], g_r[k][...], m_r[k][...], v_r[k][...])

    shapes = tuple(_S(w.shape, F32) for w in ws)
    outs = pl.pallas_call(
        body, name=name, out_shape=shapes * 3,
        in_specs=[VMEM_SPEC] * (4 * n), out_specs=tuple([VMEM_SPEC] * (3 * n)),
        compiler_params=_cparams(),
    )(*ws, *gs, *ms, *vs)
    return outs[:n], outs[n:2 * n], outs[2 * n:]


def _block_diag(w, heads_per_block):
    H, hd, _ = w.shape
    nb = H // heads_per_block
    eye = jnp.eye(heads_per_block, dtype=w.dtype)
    w4 = w.reshape(nb, heads_per_block, hd, hd)
    return jnp.einsum("nhab,hg->nhagb", w4, eye).reshape(nb, heads_per_block * hd, heads_per_block * hd)


def _diag_blocks(bd, heads_per_block, hd):
    nb = bd.shape[0]
    b5 = bd.reshape(nb, heads_per_block, hd, heads_per_block, hd)
    return jnp.stack([b5[:, h, :, h, :] for h in range(heads_per_block)], axis=1).reshape(nb * heads_per_block, hd, hd)


def _as_rows(a):
    if a.ndim == 1:
        return a.reshape(-1, LANES) if a.shape[0] % LANES == 0 else a.reshape(1, -1)
    if a.ndim == 3:
        return a.reshape(-1, LANES) if (a.size % LANES == 0) else a.reshape(a.shape[0] * a.shape[1], a.shape[2])
    return a


def kernel(x, g_mix, w_in, lru_conv_w, lru_conv_b, lru_wa, lru_ba, lru_wx, lru_bx, lru_lambda, lru_w_out, sc_conv_w, sc_w_out, w_o, g_ffn, ffn_w_up, ffn_conv_w, ffn_w_down, g_final, loss_target, m_g_mix, m_w_in, m_lru_conv_w, m_lru_conv_b, m_lru_wa, m_lru_ba, m_lru_wx, m_lru_bx, m_lru_lambda, m_lru_w_out, m_sc_conv_w, m_sc_w_out, m_w_o, m_g_ffn, m_ffn_w_up, m_ffn_conv_w, m_ffn_w_down, m_g_final, v_g_mix, v_w_in, v_lru_conv_w, v_lru_conv_b, v_lru_wa, v_lru_ba, v_lru_wx, v_lru_bx, v_lru_lambda, v_lru_w_out, v_sc_conv_w, v_sc_w_out, v_w_o, v_g_ffn, v_ffn_w_up, v_ffn_conv_w, v_ffn_w_down, v_g_final):
    T, D = x.shape[1], x.shape[2]
    d_lru = lru_lambda.shape[0]
    d_sc = sc_conv_w.shape[1] * N_DEV
    F = ffn_w_down.shape[0] * N_DEV
    H = lru_wa.shape[0]
    assert d_lru == d_sc and H * HEAD_DIM == d_lru
    xs = x.reshape(T, D)
    tgt = loss_target.reshape(T, D)
    my_x, my_y, my_c = _place()
    me = 4 * my_x + 2 * my_y + my_c

    big = [w_in, lru_w_out, sc_w_out, w_o, ffn_w_up, ffn_w_down]
    big_names = ["w_in", "lru_w_out", "sc_w_out", "w_o", "ffn_w_up", "ffn_w_down"]
    place = jnp.stack([my_c, 2 * my_x + my_y, me]).astype(jnp.int32)
    n_in, n_up = w_in.shape[1], ffn_w_up.shape[1]
    paired = [n_in, None, None, None, n_up, None]
    big_bf = [_cast_into_slot(w, place, "cast_" + nm, paired=pn is not None)
              for w, nm, pn in zip(big, big_names, paired)]
    pad_rows = lambda a: jnp.pad(a, ((0, SUB - a.shape[0]), (0, 0)))
    gathered = _all_gather(big_bf + [pad_rows(lru_conv_w), pad_rows(sc_conv_w), pad_rows(ffn_conv_w)],
                           [True] * 6 + [False] * 3,
                           [True, False, False, False, False, False, True, True, True],
                           paired + [None] * 3, "all_gather_first")
    W_in, W_lo, W_so, W_o8, W_up, W_dn8 = gathered[:6]
    full_cols = lambda g, kw: g[:, :kw, :].transpose(1, 0, 2).reshape(kw, -1)
    cw_lru = full_cols(gathered[6], lru_conv_w.shape[0])
    cw_sc = full_cols(gathered[7], sc_conv_w.shape[0])
    cw_ffn = full_cols(gathered[8], ffn_conv_w.shape[0])

    C = _tile(d_lru, C_LRU)
    hpb = C // HEAD_DIM
    wa_bd = _block_diag(lru_wa, hpb).astype(BF16)
    wx_bd = _block_diag(lru_wx, hpb).astype(BF16)
    cb, ba, bx, lam = (a.reshape(1, d_lru) for a in (lru_conv_b, lru_ba, lru_bx, lru_lambda))

    h1 = _rms_fwd(xs, g_mix, "rms_mix")
    k8 = W_up.shape[1] // 8
    wide = 2 * max(n_in, n_up)
    p, ((W_o8,), (W_lo,), (W_so,), (W_up,)) = _mm_nn(
        h1, W_in, out_dtype=BF16, name="mm_in", tn=wide,
        tasks=[_gather_task(W_o8, ici=ALL_ROWS), _gather_task(W_lo, ici=ALL_ROWS), _gather_task(W_so, ici=ALL_ROWS),
               _gather_task(W_up, ici=(0, 3 * k8), pair_n=n_up)])
    hs, yl_pre, ((W_o8,), (W_lo,), (W_so,), (W_up,)) = _lru_fwd(
        p, cw_lru, cb, wa_bd, ba, wx_bd, bx, lam, name="lru_fwd",
        tasks=[_gather_task(W_o8, fwd=ALL_ROWS), _gather_task(W_lo, fwd=ALL_ROWS), _gather_task(W_so, fwd=ALL_ROWS),
               _gather_task(W_up, ici=(3 * k8, 2 * k8), fwd=(0, 3 * k8), pair_n=n_up)])
    ys_pre = _sc_fwd(p, cw_sc, d=d_sc, name="sc_fwd")
    y_lru, ((W_up,),) = _mm_small(
        "nn", yl_pre, None, W_lo, name="mm_lru_out",
        tasks=[_gather_task(W_up, ici=(5 * k8, k8), fwd=(3 * k8, 2 * k8), pair_n=n_up)])
    y_sc, ((W_up,),) = _mm_small(
        "nn", ys_pre, None, W_so, name="mm_sc_out",
        tasks=[_gather_task(W_up, ici=(6 * k8, k8), fwd=(5 * k8, k8), pair_n=n_up)])
    gate0 = 2 * d_lru + 3 * d_sc
    merged, ((W_up,),) = _merge_fwd(p, y_lru, y_sc, col0=gate0, name="merge_fwd",
                                    tasks=[_gather_task(W_up, ici=(7 * k8, k8), fwd=(6 * k8, k8), pair_n=n_up)])
    W_o = W_o8.reshape(1, D, D)
    x1, ((W_up,),) = _mm_nn(merged, W_o, out_dtype=F32, residual=xs, name="mm_o",
                            tasks=[_gather_task(W_up, fwd=(7 * k8, k8), pair_n=n_up)])
    h2 = _rms_fwd(x1, g_ffn, "rms_ffn")
    uu, ((W_dn8,),) = _mm_nn(h2, W_up, out_dtype=BF16, name="mm_up", tn=wide,
                             tasks=[_gather_task(W_dn8, ici=ALL_ROWS)])
    act, ((W_dn8,),) = _ffn_act_fwd(uu, cw_ffn, name="ffn_act_fwd", tasks=[_gather_task(W_dn8, fwd=ALL_ROWS)])
    W_dn = W_dn8.reshape(1, F, D)
    x2 = _mm_nn(act, W_dn, out_dtype=F32, residual=x1, name="mm_down", tn=1024, tk=F)
    dx2, dx2b, loss_part, dg_final = _loss_head(x2, g_final, tgt, "loss_head")

    def pack_rows(arrs):
        flat = jnp.concatenate([a.reshape(-1) for a in arrs])
        rows = -(-flat.shape[0] // (SUB * LANES)) * SUB
        return jnp.pad(flat, (0, rows * LANES - flat.shape[0])).reshape(rows, LANES)

    def unpack_rows(pack, arrs):
        flat, out, o = pack.reshape(-1), [], 0
        for a in arrs:
            out.append(flat[o:o + a.size].reshape(a.shape))
            o += a.size
        return out

    dact = _mm_nt(dx2b, W_dn, out_dtype=BF16, name="mm_down_dx", tm=512, tko=F // 2, tn=D)
    gW_dn = _mm_tn(act, dx2b, 1, out_dtype=BF16, name="mm_down_dw", tk=1408, tt=2048).reshape(N_DEV, F // N_DEV, D)
    duu, dcw_ffn_g, dcw_ffn_v = _ffn_act_bwd(uu, dact, cw_ffn, name="ffn_act_bwd")
    dh2, ((land_dn,),) = _mm_nt(duu, W_up, out_dtype=BF16, name="mm_up_dx", tn=wide, tasks=[_swap_task(gW_dn)])
    parts_dn = _add_halves(gW_dn, land_dn, place, "rs_add_ffn_w_down")
    gW_up, ((mine_dn,),) = _mm_tn(h2, duu, N_CHIP, out_dtype=BF16, name="mm_up_dw", tk=512, tn=wide, tt=2048,
                                  tasks=[_exchange_task(*parts_dn)])
    dx1, dx1b, dg_ffn = _rms_bwd(x1, g_ffn, dh2, dx2, "rms_ffn_bwd")
    dmerged, ((land_up,),) = _mm_nt(dx1b, W_o, out_dtype=BF16, name="mm_o_dx", tn=D,
                                    tasks=[_swap_task(gW_up, pair_n=n_up)])
    parts_up, land_up = _add_halves(gW_up, land_up, place, "rs_add_ffn_w_up", pair_n=n_up)
    r8 = parts_up.shape[1] // 8
    gW_o, ((land_up,),) = _mm_tn(merged, dx1b, 1, out_dtype=BF16, name="mm_o_dw", tt=2048,
                                 tasks=[_exchange_task(parts_up, land_up, rows=(0, r8))])
    gW_o = gW_o.reshape(N_DEV, D // N_DEV, D)
    dp, dy_lru, dy_sc, ((land_up,),) = _merge_bwd(
        p, y_lru, y_sc, dmerged, col0=gate0, name="merge_bwd",
        tasks=[_exchange_task(parts_up, land_up, rows=(r8, 2 * r8))])
    dyl_pre, ((land_o,),) = _mm_small("nt", None, dy_lru, W_lo, name="mm_lru_out_dx", tasks=[_swap_task(gW_o)])
    parts_o = _add_halves(gW_o, land_o, place, "rs_add_w_o")
    gW_lo = _mm_small("tn", yl_pre, dy_lru, W_lo, name="mm_lru_out_dw")
    dys_pre, ((land_lo,),) = _mm_small("nt", None, dy_sc, W_so, name="mm_sc_out_dx", tasks=[_swap_task(gW_lo)])
    parts_lo = _add_halves(gW_lo, land_lo, place, "rs_add_lru_w_out")
    gW_so = _mm_small("tn", ys_pre, dy_sc, W_so, name="mm_sc_out_dw")
    dp, dcw_sc, ((land_up,),) = _sc_bwd(p, dys_pre, dp, cw_sc, d=d_sc, name="sc_bwd",
                                        tasks=[_exchange_task(parts_up, land_up, rows=(3 * r8, r8))])
    dp, dcw_lru, dcb, dwa_bd, dba, dwx_bd, dbx, dlam, ((land_up,), (mine_o,), (mine_lo,), (land_so,)) = _lru_bwd(
        p, hs, dyl_pre, dp, cw_lru, cb, wa_bd, ba, wx_bd, bx, lam, name="lru_bwd",
        tasks=[_exchange_task(parts_up, land_up, rows=(4 * r8, 2 * r8)), _exchange_task(*parts_o),
               _exchange_task(*parts_lo), _swap_task(gW_so)])
    parts_so = _add_halves(gW_so, land_so, place, "rs_add_sc_w_out")

    dwa = _diag_blocks(dwa_bd, hpb, HEAD_DIM)
    dwx = _diag_blocks(dwx_bd, hpb, HEAD_DIM)
    dcw_ffn = jnp.concatenate([dcw_ffn_g, dcw_ffn_v], axis=1)
    small_full = [dcw_lru, dcb, dwa, dba, dwx, dbx, dlam, dcw_sc, dg_ffn, dcw_ffn, dg_final]
    gW_in, ((mine_up,), (mine_so,), (packs,)) = _mm_tn(
        h1, dp, N_CHIP, out_dtype=BF16, name="mm_in_dw", tk=512, tn=wide, tt=2048,
        tasks=[_exchange_task(parts_up, land_up, rows=(6 * r8, 2 * r8)), _exchange_task(*parts_so),
               _bcast_task(pack_rows(small_full))])
    land_in = _swap_halves(gW_in, "rs_swap_w_in", pair_n=n_in)
    parts_in = _add_halves(gW_in, land_in, place, "rs_add_w_in", pair_n=n_in)
    dh1, ((mine_in,),) = _mm_nt(dp, W_in, out_dtype=BF16, name="mm_in_dx", tn=wide,
                                tasks=[_exchange_task(*parts_in)])
    grad_x, _, dg_mix = _rms_bwd(xs, g_mix, dh1, dx1, "rms_mix_bwd")

    mine = [mine_in, mine_lo, mine_so, mine_o, mine_up, mine_dn]
    big_m = [m_w_in, m_lru_w_out, m_sc_w_out, m_w_o, m_ffn_w_up, m_ffn_w_down]
    big_v = [v_w_in, v_lru_w_out, v_sc_w_out, v_w_o, v_ffn_w_up, v_ffn_w_down]
    big_out = {nm: _adamw_big(pt, w, m, v, "adamw_" + nm)
               for nm, pt, w, m, v in zip(big_names, mine, big, big_m, big_v)}

    (scw_lru, scb, swa, sba, swx, sbx, slam, scw_sc, sg_ffn, scw_ffn, sg_final) = unpack_rows(
        _sum_packs(packs, "sum_small"), small_full)
    (sg_mix,) = unpack_rows(_all_reduce_small(pack_rows([dg_mix]), "all_reduce_g_mix"), [dg_mix])

    def my_cols(a):
        n = a.shape[1] // N_DEV
        return lax.dynamic_slice_in_dim(a, me * n, n, axis=1)

    small_names = ["g_mix", "lru_conv_w", "lru_conv_b", "lru_wa", "lru_ba", "lru_wx", "lru_bx", "lru_lambda",
                   "sc_conv_w", "g_ffn", "ffn_conv_w", "g_final"]
    small_w = [g_mix, lru_conv_w, lru_conv_b, lru_wa, lru_ba, lru_wx, lru_bx, lru_lambda, sc_conv_w, g_ffn,
               ffn_conv_w, g_final]
    small_m = [m_g_mix, m_lru_conv_w, m_lru_conv_b, m_lru_wa, m_lru_ba, m_lru_wx, m_lru_bx, m_lru_lambda,
               m_sc_conv_w, m_g_ffn, m_ffn_conv_w, m_g_final]
    small_v = [v_g_mix, v_lru_conv_w, v_lru_conv_b, v_lru_wa, v_lru_ba, v_lru_wx, v_lru_bx, v_lru_lambda,
               v_sc_conv_w, v_g_ffn, v_ffn_conv_w, v_g_final]
    small_g = [sg_mix.reshape(D), my_cols(scw_lru), scb.reshape(d_lru), swa, sba.reshape(d_lru), swx,
               sbx.reshape(d_lru), slam.reshape(d_lru), my_cols(scw_sc), sg_ffn.reshape(D), my_cols(scw_ffn),
               sg_final.reshape(D)]
    sd, snm, snv = _adamw_small([_as_rows(a) for a in small_w], [_as_rows(a) for a in small_g],
                                [_as_rows(a) for a in small_m], [_as_rows(a) for a in small_v], "adamw_small")
    small_out = {nm: (g, d.reshape(w.shape), nm_.reshape(w.shape), nv_.reshape(w.shape))
                 for nm, w, g, d, nm_, nv_ in zip(small_names, small_w, small_g, sd, snm, snv)}

    loss = lax.psum(loss_part[0, 0], AXES)
    order = ["g_mix", "w_in", "lru_conv_w", "lru_conv_b", "lru_wa", "lru_ba", "lru_wx", "lru_bx", "lru_lambda",
             "lru_w_out", "sc_conv_w", "sc_w_out", "w_o", "g_ffn", "ffn_w_up", "ffn_conv_w", "ffn_w_down", "g_final"]
    res = {**big_out, **small_out}
    return (loss, grad_x.reshape(x.shape),
            *[res[nm][0] for nm in order], *[res[nm][1] for nm in order],
            *[res[nm][2] for nm in order], *[res[nm][3] for nm in order])
```

```python
import functools
import math

import jax
import jax.numpy as jnp
from jax import lax
from jax.experimental import pallas as pl
from jax.experimental.pallas import tpu as pltpu

F32, BF16 = jnp.float32, jnp.bfloat16
MESH = pl.DeviceIdType.MESH
N_DEV = 8
N_CHIP = 4
AXES = ("x", "y", "c")

EPS = 1e-6
LRU_C = 8.0
HEAD_DIM = 64
ADAM_LR, ADAM_B1, ADAM_B2, ADAM_EPS, ADAM_WD, ADAM_STEP = 0.001, 0.9, 0.999, 1e-08, 0.01, 10

VMEM_LIMIT = 48 * 1024 * 1024
LANES = 128
SUB = 8
HALO = 16
TB = 512
C_LRU = 256
C_EW = 512
TM, TN, TK = 512, 1536, 2048


def _tile(n, pref, align=LANES):
    best = None
    for d in range(align, min(n, pref) + 1, align):
        if n % d == 0:
            best = d
    return best or n


def _cparams(sem=None, vmem=VMEM_LIMIT):
    kw = dict(vmem_limit_bytes=vmem)
    if sem is not None:
        kw["dimension_semantics"] = sem
    return pltpu.CompilerParams(**kw)


def _S(shape, dtype):
    return jax.ShapeDtypeStruct(shape, dtype)


ANY = pl.BlockSpec(memory_space=pl.ANY)
VMEM_SPEC = pl.BlockSpec(memory_space=pltpu.VMEM)


class _Task:
    def __init__(self, arrays, aliased, start, wait, fresh=(), nsem=3):
        self.arrays, self.aliased, self.start, self.wait = arrays, aliased, start, wait
        self.fresh, self.nsem = list(fresh), nsem


def _call(name, grid, compute, in_specs, args, out_shape, out_specs, scratch, tasks=(), own_aliases=None):
    n_in, n_out, n_scr = len(args), len(out_shape), len(scratch)
    x_in, x_out, aliases, where = [], [], dict(own_aliases or {}), []
    for t in tasks:
        places = []
        for k, arr in enumerate(t.arrays):
            if k in t.aliased:
                aliases[n_in + len(x_in)] = n_out + len(x_out)
                places.append(("out", len(x_out)))
                x_out.append(_S(arr.shape, arr.dtype))
            else:
                places.append(("in", len(x_in)))
            x_in.append(arr)
        for shp in t.fresh:
            places.append(("out", len(x_out)))
            x_out.append(shp)
        where.append(places)
    n_xi, n_xo = len(x_in), len(x_out)

    def body(*refs):
        ins, xi = refs[:n_in], refs[n_in:n_in + n_xi]
        o0 = n_in + n_xi
        outs, xo = refs[o0:o0 + n_out], refs[o0 + n_out:o0 + n_out + n_xo]
        s0 = o0 + n_out + n_xo
        scr, sems = refs[s0:s0 + n_scr], refs[s0 + n_scr:]
        ids = [pl.program_id(a) for a in range(len(grid))]

        def task_refs(ti):
            return [xo[i] if kind == "out" else xi[i] for kind, i in where[ti]]

        if tasks:
            first = functools.reduce(jnp.logical_and, [i == 0 for i in ids])

            @pl.when(first)
            def _():
                for ti, t in enumerate(tasks):
                    t.start(task_refs(ti), *sems[3 * ti:3 * ti + 3])

        compute(*ins, *outs, *scr)
        if tasks:
            last = functools.reduce(jnp.logical_and, [i == g - 1 for i, g in zip(ids, grid)])

            @pl.when(last)
            def _():
                for ti, t in enumerate(tasks):
                    t.wait(task_refs(ti), *sems[3 * ti:3 * ti + 3])

    sem_shapes = []
    for t in tasks:
        sem_shapes += [pltpu.SemaphoreType.DMA((t.nsem,)), pltpu.SemaphoreType.DMA((t.nsem,)),
                       pltpu.SemaphoreType.DMA((1,))]
    res = pl.pallas_call(
        body, name=name, grid=grid,
        in_specs=list(in_specs) + [ANY] * n_xi,
        out_specs=tuple(out_specs) + (ANY,) * n_xo,
        out_shape=tuple(out_shape) + tuple(x_out),
        scratch_shapes=list(scratch) + sem_shapes,
        input_output_aliases=aliases,
        compiler_params=_cparams(("arbitrary",) * len(grid)),
    )(*args, *x_in)
    outs, passed, o = res[:n_out], [], n_out
    for places in where:
        k = sum(1 for kind, _ in places if kind == "out")
        passed.append(list(res[o:o + k]))
        o += k
    return outs, passed


class _Side:
    def __init__(self, in_specs, args, out_shape, out_specs, fn):
        self.in_specs, self.args, self.out_shape, self.out_specs, self.fn = in_specs, args, out_shape, out_specs, fn


def _attach_side(make_side, grid, compute, in_specs, args, out_shape, out_specs):
    s = make_side(grid) if make_side is not None else None
    if s is None:
        return compute, in_specs, args, out_shape, out_specs
    n_in, n_out, n_si, n_so = len(args), len(out_shape), len(s.args), len(s.out_shape)

    def both(*refs):
        ins, si = refs[:n_in], refs[n_in:n_in + n_si]
        o0 = n_in + n_si
        outs, so, scr = refs[o0:o0 + n_out], refs[o0 + n_out:o0 + n_out + n_so], refs[o0 + n_out + n_so:]
        compute(*ins, *outs, *scr)
        s.fn(*si, *so)

    return (both, list(in_specs) + s.in_specs, list(args) + s.args, list(out_shape) + s.out_shape,
            list(out_specs) + s.out_specs)


def _mm_nn(a, w3, *, out_dtype, name, residual=None, tm=TM, tn=TN, tk=TK, tasks=()):
    M, K = a.shape
    G, _, n = w3.shape
    tm, tn, tk = _tile(M, tm, SUB), _tile(n, tn), _tile(K, tk)
    nj, nk = n // tn, K // tk

    def compute(*refs):
        if residual is None:
            a_ref, w_ref, o_ref = refs[:3]
            r_ref = None
        else:
            a_ref, w_ref, r_ref, o_ref = refs[:4]

        def finish(r):
            if r_ref is not None:
                r = r + r_ref[...]
            o_ref[...] = r.astype(o_ref.dtype)

        if nk == 1:
            finish(jnp.dot(a_ref[...], w_ref[...], preferred_element_type=F32))
            return
        acc = refs[-1]
        k = pl.program_id(3)

        @pl.when(k == 0)
        def _():
            acc[...] = jnp.zeros_like(acc)

        acc[...] += jnp.dot(a_ref[...], w_ref[...], preferred_element_type=F32)

        @pl.when(k == nk - 1)
        def _():
            finish(acc[...])

    in_specs = [pl.BlockSpec((tm, tk), lambda g, j, i, k: (i, k)),
                pl.BlockSpec((None, tk, tn), lambda g, j, i, k: (g, k, j))]
    args = [a, w3]
    if residual is not None:
        in_specs.append(pl.BlockSpec((tm, tn), lambda g, j, i, k: (i, g * nj + j)))
        args.append(residual)
    outs, passed = _call(
        name, (G, nj, M // tm, nk), compute, in_specs, args, [_S((M, G * n), out_dtype)],
        [pl.BlockSpec((tm, tn), lambda g, j, i, k: (i, g * nj + j))],
        [] if nk == 1 else [pltpu.VMEM((tm, tn), F32)], tasks)
    return (outs[0], passed) if tasks else outs[0]


def _mm_nt(dy, w3, *, out_dtype, name, tm=1024, tko=1024, tn=TN, tasks=(), side=None):
    M, _ = dy.shape
    G, K, n = w3.shape
    tm, tko, tn = _tile(M, tm, SUB), _tile(K, tko), _tile(n, tn)
    nj = n // tn
    nr = G * nj

    def compute(dy_ref, w_ref, o_ref, *scr):
        part = lax.dot_general(dy_ref[...], w_ref[...], (((1,), (1,)), ((), ())), preferred_element_type=F32)
        if nr == 1:
            o_ref[...] = part.astype(o_ref.dtype)
            return
        (acc,) = scr
        r = pl.program_id(2)

        @pl.when(r == 0)
        def _():
            acc[...] = jnp.zeros_like(acc)

        acc[...] += part

        @pl.when(r == nr - 1)
        def _():
            o_ref[...] = acc[...].astype(o_ref.dtype)

    grid = (K // tko, M // tm, nr)
    compute, in_specs, args, out_shape, out_specs = _attach_side(
        side, grid, compute,
        [pl.BlockSpec((tm, tn), lambda ko, i, r: (i, r)),
         pl.BlockSpec((None, tko, tn), lambda ko, i, r: (r // nj, ko, r % nj))],
        [dy, w3], [_S((M, K), out_dtype)], [pl.BlockSpec((tm, tko), lambda ko, i, r: (i, ko))])
    outs, passed = _call(name, grid, compute, in_specs, args, out_shape, out_specs,
                         [] if nr == 1 else [pltpu.VMEM((tm, tko), F32)], tasks)
    if side is not None:
        return outs[0], passed, (tuple(outs[1:]) or None)
    return (outs[0], passed) if tasks else outs[0]


def _mm_tn(a, dy, G, *, out_dtype, name, tk=1024, tn=TN, tt=1024, tasks=(), side=None):
    M, K = a.shape
    n = dy.shape[1] // G
    tk, tn, tt = _tile(K, tk), _tile(n, tn), _tile(M, tt, SUB)
    nj, nt = n // tn, M // tt

    def compute(a_ref, dy_ref, o_ref, acc):
        t = pl.program_id(3)

        @pl.when(t == 0)
        def _():
            acc[...] = jnp.zeros_like(acc)

        acc[...] += lax.dot_general(a_ref[...], dy_ref[...], (((0,), (0,)), ((), ())),
                                    preferred_element_type=F32)

        @pl.when(t == nt - 1)
        def _():
            o_ref[...] = acc[...].astype(o_ref.dtype)

    grid = (G, nj, K // tk, nt)
    compute, in_specs, args, out_shape, out_specs = _attach_side(
        side, grid, compute,
        [pl.BlockSpec((tt, tk), lambda g, j, k, t: (t, k)),
         pl.BlockSpec((tt, tn), lambda g, j, k, t: (t, g * nj + j))],
        [a, dy], [_S((G, K, n), out_dtype)], [pl.BlockSpec((None, tk, tn), lambda g, j, k, t: (g, k, j))])
    outs, passed = _call(name, grid, compute, in_specs, args, out_shape, out_specs,
                         [pltpu.VMEM((tk, tn), F32)], tasks)
    if side is not None:
        return outs[0], passed, (tuple(outs[1:]) or None)
    return (outs[0], passed) if tasks else outs[0]


def _mm_small(kind, a, b, w3, *, name, tm=1024, tasks=()):
    G, K, n = w3.shape
    M = (a if a is not None else b).shape[0]
    tm = _tile(M, tm, HALO)
    nt = M // tm
    w_spec = pl.BlockSpec((G, K, n), lambda i: (0, 0, 0))
    a_spec = pl.BlockSpec((tm, K), lambda i: (i, 0))
    b_spec = pl.BlockSpec((tm, G * n), lambda i: (i, 0))
    cols = lambda g: slice(g * n, (g + 1) * n)
    if kind == "nn":
        def compute(a_ref, w_ref, o_ref):
            av = a_ref[...]
            for g in range(G):
                o_ref[:, cols(g)] = jnp.dot(av, w_ref[g], preferred_element_type=F32).astype(o_ref.dtype)

        outs, passed = _call(name, (nt,), compute, [a_spec, w_spec], [a, w3], [_S((M, G * n), BF16)], [b_spec], [], tasks)
    elif kind == "nt":
        def compute(b_ref, w_ref, o_ref):
            acc = None
            for g in range(G):
                part = lax.dot_general(b_ref[:, cols(g)], w_ref[g], (((1,), (1,)), ((), ())),
                                       preferred_element_type=F32)
                acc = part if acc is None else acc + part
            o_ref[...] = acc.astype(o_ref.dtype)

        outs, passed = _call(name, (nt,), compute, [b_spec, w_spec], [b, w3], [_S((M, K), BF16)], [a_spec], [], tasks)
    else:
        def compute(a_ref, b_ref, o_ref, acc):
            i = pl.program_id(0)

            @pl.when(i == 0)
            def _():
                acc[...] = jnp.zeros_like(acc)

            at = a_ref[...].T
            for g in range(G):
                acc[g] += jnp.dot(at, b_ref[:, cols(g)], preferred_element_type=F32)

            @pl.when(i == nt - 1)
            def _():
                o_ref[...] = acc[...].astype(o_ref.dtype)

        outs, passed = _call(name, (nt,), compute, [a_spec, b_spec], [a, b], [_S((G, K, n), BF16)], [w_spec],
                             [pltpu.VMEM((G, K, n), F32)], tasks)
    return (outs[0], passed) if tasks else outs[0]


def _cast_into_slot(w, place, name, paired=False):
    R, C = w.shape
    tr = _tile(R, 512, HALO)

    def body(s_ref, w_ref, o_ref):
        del s_ref
        o_ref[...] = w_ref[...].astype(BF16)

    if paired:
        shape, out_map = (N_CHIP, R, 2 * C), lambda i, s: (s[1], i, s[0])
    else:
        shape, out_map = (N_DEV, R, C), lambda i, s: (s[2], i, 0)
    return pl.pallas_call(
        body, name=name, out_shape=_S(shape, BF16),
        grid_spec=pltpu.PrefetchScalarGridSpec(
            num_scalar_prefetch=1, grid=(R // tr,),
            in_specs=[pl.BlockSpec((tr, C), lambda i, s: (i, 0))],
            out_specs=pl.BlockSpec((None, tr, C), out_map)),
        compiler_params=_cparams(("parallel",)),
    )(place, w)


def _down(cur, prev8, j):
    return pltpu.roll(jnp.concatenate([prev8, cur], axis=0), j, 0)[SUB:, :]


def _up(cur, next8, j):
    n = cur.shape[0] + SUB
    return pltpu.roll(jnp.concatenate([cur, next8], axis=0), n - j, 0)[:cur.shape[0], :]


def _shifted_down(x, prev8, n):
    full = jnp.concatenate([prev8, x], axis=0)
    return [x] + [pltpu.roll(full, s, 0)[SUB:, :] for s in range(1, n)]


def _shifted_up(x, next8, n):
    m = x.shape[0] + SUB
    full = jnp.concatenate([x, next8], axis=0)
    return [x] + [pltpu.roll(full, m - s, 0)[:x.shape[0], :] for s in range(1, n)]


def _taps(sh, w_ref):
    kw = w_ref.shape[0]
    y = sh[0] * w_ref[pl.ds(kw - 1, 1), :]
    for k in range(kw - 1):
        y = y + sh[kw - 1 - k] * w_ref[pl.ds(k, 1), :]
    return y


def _conv(x, prev8, w_ref):
    return _taps(_shifted_down(x, prev8, w_ref.shape[0]), w_ref)


def _conv_t(dy, next8, w_ref):
    return _taps(_shifted_up(dy, next8, w_ref.shape[0]), w_ref)


def _conv_dw(dw_ref, dy, x, prev8, first):
    kw = dw_ref.shape[0]

    @pl.when(first)
    def _():
        dw_ref[...] = jnp.zeros_like(dw_ref)

    for k in range(kw):
        xs = x if k == kw - 1 else _down(x, prev8, kw - 1 - k)
        dw_ref[pl.ds(k, 1), :] += jnp.sum(dy * xs, axis=0, keepdims=True)


def _acc(ref, val, first):
    @pl.when(first)
    def _():
        ref[...] = jnp.zeros_like(ref)

    ref[...] += val


def _acc_row(ref, val, first):
    _acc(ref, jnp.sum(val, axis=0, keepdims=True), first)


def _prev8(h_ref, t):
    return jnp.where(t > 0, h_ref[...].astype(F32)[HALO - SUB:, :], 0.0)


def _next8(h_ref, is_last):
    return jnp.where(is_last, 0.0, h_ref[...].astype(F32)[:SUB, :])


_GELU_K0 = math.sqrt(2.0 / math.pi)
_GELU_K1 = 0.044715


def _gelu_and_grad(x):
    x2 = x * x
    th = jnp.tanh(_GELU_K0 * x * (1.0 + _GELU_K1 * x2))
    g = 0.5 * x * (1.0 + th)
    dg = 0.5 * (1.0 + th) + 0.5 * x * (1.0 - th * th) * (_GELU_K0 * (1.0 + 3.0 * _GELU_K1 * x2))
    return g, dg


def _neg_expm1(z):
    series = -z * (1.0 + z * (0.5 + z * (1.0 / 6.0 + z * (1.0 / 24.0))))
    return jnp.where(z > -0.03, series, 1.0 - jnp.exp(z))


def _store_staged(stages, dst_hbm, sems, step, n_steps, where):
    def copies(s, slot):
        return [pltpu.make_async_copy(
            st.at[slot], dst_hbm.at[pl.ds(r0, st.shape[1]), pl.ds(c0, st.shape[2])], sems.at[slot, k])
            for k, (st, (r0, c0)) in enumerate(zip(stages, where(s)))]

    slot = step % 2

    @pl.when(step > 0)
    def _():
        for cp in copies(step - 1, 1 - slot):
            cp.wait()

    for cp in copies(step, slot):
        cp.start()

    @pl.when(step == n_steps - 1)
    def _():
        for cp in copies(step, slot):
            cp.wait()


def _halo_prev_map(hb, col_fn):
    return lambda c, t: (jnp.maximum(t * hb - 1, 0), col_fn(c))


def _rms_fwd(x, g, name):
    T, D = x.shape
    tb = _tile(T, TB, SUB)

    def body(x_ref, g_ref, o_ref):
        xv = x_ref[...]
        rstd = lax.rsqrt(jnp.mean(xv * xv, axis=-1, keepdims=True) + EPS)
        o_ref[...] = (xv * rstd * g_ref[...]).astype(BF16)

    return pl.pallas_call(
        body, name=name, out_shape=_S((T, D), BF16), grid=(T // tb,),
        in_specs=[pl.BlockSpec((tb, D), lambda i: (i, 0)), pl.BlockSpec((1, D), lambda i: (0, 0))],
        out_specs=pl.BlockSpec((tb, D), lambda i: (i, 0)),
        compiler_params=_cparams(("parallel",)),
    )(x, g.reshape(1, D))


def _rms_bwd(x, g, dh, dres, name):
    T, D = x.shape
    tb = _tile(T, 256, SUB)

    def body(x_ref, g_ref, dh_ref, dr_ref, dx_ref, dxb_ref, dg_ref):
        i = pl.program_id(0)
        xv = x_ref[...]
        rstd = lax.rsqrt(jnp.mean(xv * xv, axis=-1, keepdims=True) + EPS)
        xn = xv * rstd
        dhv = dh_ref[...].astype(F32)
        _acc_row(dg_ref, dhv * xn, i == 0)
        dxn = dhv * g_ref[...]
        dx = dr_ref[...] + rstd * (dxn - xn * jnp.mean(dxn * xn, axis=-1, keepdims=True))
        dx_ref[...] = dx
        dxb_ref[...] = dx.astype(BF16)

    blk = pl.BlockSpec((tb, D), lambda i: (i, 0))
    vec = pl.BlockSpec((1, D), lambda i: (0, 0))
    return pl.pallas_call(
        body, name=name, out_shape=(_S((T, D), F32), _S((T, D), BF16), _S((1, D), F32)),
        grid=(T // tb,), in_specs=[blk, vec, blk, blk], out_specs=(blk, blk, vec),
        compiler_params=_cparams(("arbitrary",)),
    )(x, g.reshape(1, D), dh, dres)


def _loss_head(x2, g, target, name):
    T, D = x2.shape
    tb = _tile(T, 256, SUB)

    def body(x_ref, g_ref, t_ref, dx_ref, dxb_ref, loss_ref, dg_ref):
        i = pl.program_id(0)
        xv = x_ref[...]
        rstd = lax.rsqrt(jnp.mean(xv * xv, axis=-1, keepdims=True) + EPS)
        xn = xv * rstd
        err = xn * g_ref[...] - t_ref[...]
        part = 0.5 * jnp.sum(jnp.mean(err * err, axis=-1, keepdims=True), axis=0, keepdims=True)
        part = jnp.broadcast_to(part, (1, LANES))
        _acc(loss_ref, part, i == 0)
        dy = err * (1.0 / D)
        _acc_row(dg_ref, dy * xn, i == 0)
        dxn = dy * g_ref[...]
        dx = rstd * (dxn - xn * jnp.mean(dxn * xn, axis=-1, keepdims=True))
        dx_ref[...] = dx
        dxb_ref[...] = dx.astype(BF16)

    blk = pl.BlockSpec((tb, D), lambda i: (i, 0))
    vec = pl.BlockSpec((1, D), lambda i: (0, 0))
    return pl.pallas_call(
        body, name=name,
        out_shape=(_S((T, D), F32), _S((T, D), BF16), _S((1, LANES), F32), _S((1, D), F32)),
        grid=(T // tb,), in_specs=[blk, vec, blk],
        out_specs=(blk, blk, pl.BlockSpec((1, LANES), lambda i: (0, 0)), vec),
        compiler_params=_cparams(("arbitrary",)),
    )(x2, g.reshape(1, D), target)


def _lru_gates(xc, wa_ref, ba_ref, wx_ref, bx_ref, lam_ref):
    xcb = xc.astype(BF16)
    r = jax.nn.sigmoid(jnp.dot(xcb, wa_ref[...], preferred_element_type=F32) + ba_ref[...])
    i = jax.nn.sigmoid(jnp.dot(xcb, wx_ref[...], preferred_element_type=F32) + bx_ref[...])
    sp = jax.nn.softplus(-lam_ref[...])
    log_a = (-LRU_C * sp) * r
    a = jnp.exp(log_a)
    s = jnp.sqrt(_neg_expm1(2.0 * log_a))
    return xcb, r, i, a, s


def _lru_fwd(p, conv_w, conv_b, wa_bd, ba, wx_bd, bx, lam, *, name, tasks=()):
    T = p.shape[0]
    d = lam.shape[-1]
    C = _tile(d, C_LRU)
    nC = d // C
    tb = _tile(T, TB, HALO)
    nT, hb, nt = T // tb, tb // HALO, tb // SUB

    def body(x_ref, xh_ref, g_ref, cw_ref, cb_ref, wa_ref, ba_ref, wx_ref, bx_ref, lam_ref,
             hs_ref, y_ref, a_s, u_s, h_s):
        t = pl.program_id(1)

        @pl.when(t == 0)
        def _():
            h_s[...] = jnp.zeros_like(h_s)

        x = x_ref[...].astype(F32)
        xc = _conv(x, _prev8(xh_ref, t), cw_ref) + cb_ref[...]
        _, r, i, a, s = _lru_gates(xc, wa_ref, ba_ref, wx_ref, bx_ref, lam_ref)
        a_s[...] = a
        u_s[...] = s * (i * xc)
        row = lax.broadcasted_iota(jnp.int32, (SUB, C), 0)

        def step(k, h):
            o = pl.multiple_of(k * SUB, SUB)
            A = a_s[pl.ds(o, SUB), :]
            B = u_s[pl.ds(o, SUB), :]
            for sh in (1, 2, 4):
                m = row >= sh
                Ap = pltpu.roll(A, sh, 0)
                Bp = pltpu.roll(B, sh, 0)
                B = jnp.where(m, A * Bp + B, B)
                A = jnp.where(m, A * Ap, A)
            hs = A * h + B
            hs_ref[pl.ds(o, SUB), :] = hs
            return jnp.broadcast_to(hs[SUB - 1:SUB, :], (SUB, C))

        h_s[...] = lax.fori_loop(0, nt, step, h_s[...])
        gel, _ = _gelu_and_grad(g_ref[...].astype(F32))
        y_ref[...] = (gel * hs_ref[...]).astype(BF16)

    vec = pl.BlockSpec((1, C), lambda c, t: (0, c))
    sq = pl.BlockSpec((None, C, C), lambda c, t: (c, 0, 0))
    outs, passed = _call(
        name, (nC, nT), body,
        [pl.BlockSpec((tb, C), lambda c, t: (t, c)),
         pl.BlockSpec((HALO, C), _halo_prev_map(hb, lambda c: c)),
         pl.BlockSpec((tb, C), lambda c, t: (t, nC + c)),
         pl.BlockSpec((conv_w.shape[0], C), lambda c, t: (0, c)),
         vec, sq, vec, sq, vec, vec],
        [p, p, p, conv_w, conv_b, wa_bd, ba, wx_bd, bx, lam],
        [_S((T, d), F32), _S((T, d), BF16)],
        [pl.BlockSpec((tb, C), lambda c, t: (t, c)), pl.BlockSpec((tb, C), lambda c, t: (t, c))],
        [pltpu.VMEM((tb, C), F32), pltpu.VMEM((tb, C), F32), pltpu.VMEM((SUB, C), F32)], tasks)
    return (*outs, passed) if tasks else outs


def _lru_bwd(p, hs, dyl, dp, conv_w, conv_b, wa_bd, ba, wx_bd, bx, lam, *, name, tasks=()):
    T = p.shape[0]
    d = lam.shape[-1]
    C = _tile(d, C_LRU)
    nC = d // C
    tb = _tile(T, TB, HALO)
    nT, hb, nt = T // tb, tb // HALO, tb // SUB
    kw = conv_w.shape[0]

    def body(x_ref, xh_ref, g_ref, hs_ref, hh_ref, dy_ref, cw_ref, cb_ref, wa_ref, ba_ref, wx_ref, bx_ref,
             lam_ref, dp_in, dp_ref, dcw_ref, dcb_ref, dwa_ref, dba_ref, dwx_ref, dbx_ref, dlam_ref,
             b_s, g_s, dh_s, an_s, dhn_s, dxn_s, st_x, st_g, sems):
        del dp_in
        c = pl.program_id(0)
        tr = pl.program_id(1)
        t = nT - 1 - tr
        first = tr == 0

        @pl.when(first)
        def _():
            an_s[...] = jnp.zeros_like(an_s)
            dhn_s[...] = jnp.zeros_like(dhn_s)
            dxn_s[...] = jnp.zeros_like(dxn_s)

        x = x_ref[...].astype(F32)
        xprev = _prev8(xh_ref, t)
        xc = _conv(x, xprev, cw_ref) + cb_ref[...]
        xcb, r, i, a, s = _lru_gates(xc, wa_ref, ba_ref, wx_ref, bx_ref, lam_ref)
        hsv = hs_ref[...]
        dy = dy_ref[...].astype(F32)
        gel, dgel = _gelu_and_grad(g_ref[...].astype(F32))
        step_no = c * nT + tr
        slot = step_no % 2
        st_g[slot] = (dy * hsv * dgel).astype(BF16)

        b_s[...] = _up(a, an_s[...], 1)
        g_s[...] = dy * gel
        row = lax.broadcasted_iota(jnp.int32, (SUB, C), 0)

        def step(k, carry):
            o = pl.multiple_of((nt - 1 - k) * SUB, SUB)
            B = b_s[pl.ds(o, SUB), :]
            G = g_s[pl.ds(o, SUB), :]
            for sh in (1, 2, 4):
                m = row < SUB - sh
                Bn = pltpu.roll(B, SUB - sh, 0)
                Gn = pltpu.roll(G, SUB - sh, 0)
                G = jnp.where(m, B * Gn + G, G)
                B = jnp.where(m, B * Bn, B)
            dh = B * carry + G
            dh_s[pl.ds(o, SUB), :] = dh
            return jnp.broadcast_to(dh[0:1, :], (SUB, C))

        dhn_s[...] = lax.fori_loop(0, nt, step, dhn_s[...])
        an_s[...] = a[:SUB, :]
        dh = dh_s[...]

        hprev = _down(hsv, jnp.where(t > 0, hh_ref[...][HALO - SUB:, :], 0.0), 1)
        d_a = dh * hprev
        ixc = i * xc
        d_s = dh * ixc
        d_i = dh * s * xc
        d_xc = dh * s * i
        d_l = d_a * a - d_s * (a * a) / s
        sp = jax.nn.softplus(-lam_ref[...])
        _acc_row(dlam_ref, d_l * r * (LRU_C * jax.nn.sigmoid(-lam_ref[...])), first)
        d_zr = (d_l * (-LRU_C * sp)) * r * (1.0 - r)
        d_zi = d_i * i * (1.0 - i)
        _acc_row(dba_ref, d_zr, first)
        _acc_row(dbx_ref, d_zi, first)
        d_zrb = d_zr.astype(BF16)
        d_zib = d_zi.astype(BF16)
        tn_dims = (((0,), (0,)), ((), ()))
        nt_dims = (((1,), (1,)), ((), ()))
        gwa = lax.dot_general(xcb, d_zrb, tn_dims, preferred_element_type=F32)
        gwx = lax.dot_general(xcb, d_zib, tn_dims, preferred_element_type=F32)
        _acc(dwa_ref, gwa, first)
        _acc(dwx_ref, gwx, first)
        d_xc = (d_xc + lax.dot_general(d_zrb, wa_ref[...], nt_dims, preferred_element_type=F32)
                + lax.dot_general(d_zib, wx_ref[...], nt_dims, preferred_element_type=F32))
        _acc_row(dcb_ref, d_xc, first)
        _conv_dw(dcw_ref, d_xc, x, xprev, first)
        st_x[slot] = _conv_t(d_xc, dxn_s[...], cw_ref).astype(BF16)
        dxn_s[...] = d_xc[:SUB, :]

        def where(s):
            row0, col0 = (nT - 1 - s % nT) * tb, (s // nT) * C
            return [(row0, col0), (row0, d + col0)]

        _store_staged([st_x, st_g], dp_ref, sems, step_no, nC * nT, where)

    rev = lambda c, tr: (nT - 1 - tr, c)
    vec = pl.BlockSpec((1, C), lambda c, tr: (0, c))
    sq = pl.BlockSpec((None, C, C), lambda c, tr: (c, 0, 0))
    cwb = pl.BlockSpec((kw, C), lambda c, tr: (0, c))
    halo_prev = lambda c, tr: (jnp.maximum((nT - 1 - tr) * hb - 1, 0), c)
    outs, passed = _call(
        name, (nC, nT), body,
        [pl.BlockSpec((tb, C), rev),
         pl.BlockSpec((HALO, C), halo_prev),
         pl.BlockSpec((tb, C), lambda c, tr: (nT - 1 - tr, nC + c)),
         pl.BlockSpec((tb, C), rev),
         pl.BlockSpec((HALO, C), halo_prev),
         pl.BlockSpec((tb, C), rev),
         cwb, vec, sq, vec, sq, vec, vec, ANY],
        [p, p, p, hs, hs, dyl, conv_w, conv_b, wa_bd, ba, wx_bd, bx, lam, dp],
        [_S(dp.shape, dp.dtype), _S((kw, d), F32), _S((1, d), F32), _S((nC, C, C), F32), _S((1, d), F32),
         _S((nC, C, C), F32), _S((1, d), F32), _S((1, d), F32)],
        [ANY, cwb, vec, sq, vec, sq, vec, vec],
        [pltpu.VMEM((tb, C), F32), pltpu.VMEM((tb, C), F32), pltpu.VMEM((tb, C), F32),
         pltpu.VMEM((SUB, C), F32), pltpu.VMEM((SUB, C), F32), pltpu.VMEM((SUB, C), F32),
         pltpu.VMEM((2, tb, C), BF16), pltpu.VMEM((2, tb, C), BF16), pltpu.SemaphoreType.DMA((2, 2))],
        tasks, own_aliases={13: 0})
    return (*outs, passed) if tasks else outs


def _sc_fwd(p, conv_w, *, d, name):
    T = p.shape[0]
    C = _tile(d, C_EW)
    nC = d // C
    tb = _tile(T, TB, HALO)
    nT, hb = T // tb, tb // HALO

    def body(b_ref, c_ref, ch_ref, v_ref, vh_ref, w_ref, y_ref):
        t = pl.program_id(1)
        cv = c_ref[...].astype(F32) * v_ref[...].astype(F32)
        cvp = _prev8(ch_ref, t) * _prev8(vh_ref, t)
        y_ref[...] = (b_ref[...].astype(F32) * _conv(cv, cvp, w_ref)).astype(BF16)

    seg = lambda k: pl.BlockSpec((tb, C), lambda c, t: (t, k * nC + c))
    hseg = lambda k: pl.BlockSpec((HALO, C), _halo_prev_map(hb, lambda c: k * nC + c))
    return pl.pallas_call(
        body, name=name, out_shape=_S((T, d), BF16), grid=(nC, nT),
        in_specs=[seg(2), seg(3), hseg(3), seg(4), hseg(4), pl.BlockSpec((conv_w.shape[0], C), lambda c, t: (0, c))],
        out_specs=pl.BlockSpec((tb, C), lambda c, t: (t, c)),
        compiler_params=_cparams(("parallel", "parallel")),
    )(p, p, p, p, p, conv_w)


def _sc_bwd(p, dys, dp, conv_w, *, d, name, tasks=()):
    T = p.shape[0]
    C = _tile(d, C_EW)
    nC = d // C
    tb = _tile(T, TB, HALO)
    nT, hb = T // tb, tb // HALO
    kw = conv_w.shape[0]

    def body(b_ref, bn_ref, c_ref, ch_ref, v_ref, vh_ref, dy_ref, dyn_ref, w_ref, dp_in, dp_ref, dw_ref,
             st_b, st_c, st_v, sems):
        del dp_in
        c = pl.program_id(0)
        t = pl.program_id(1)
        last = t == nT - 1
        bv = b_ref[...].astype(F32)
        cvv = c_ref[...].astype(F32)
        vv = v_ref[...].astype(F32)
        dy = dy_ref[...].astype(F32)
        cv = cvv * vv
        cvp = _prev8(ch_ref, t) * _prev8(vh_ref, t)
        step_no = c * nT + t
        slot = step_no % 2
        st_b[slot] = (dy * _conv(cv, cvp, w_ref)).astype(BF16)
        dz = dy * bv
        dzn = _next8(dyn_ref, last) * _next8(bn_ref, last)
        _conv_dw(dw_ref, dz, cv, cvp, t == 0)
        dcv = _conv_t(dz, dzn, w_ref)
        st_c[slot] = (dcv * vv).astype(BF16)
        st_v[slot] = (dcv * cvv).astype(BF16)

        def where(s):
            return [((s % nT) * tb, (2 + k) * d + (s // nT) * C) for k in range(3)]

        _store_staged([st_b, st_c, st_v], dp_ref, sems, step_no, nC * nT, where)

    seg = lambda k: pl.BlockSpec((tb, C), lambda c, t: (t, k * nC + c))
    hseg = lambda k: pl.BlockSpec((HALO, C), _halo_prev_map(hb, lambda c: k * nC + c))
    last_h = T // HALO - 1
    nseg = lambda k: pl.BlockSpec((HALO, C), lambda c, t: (jnp.minimum((t + 1) * hb, last_h), k * nC + c))
    outs, passed = _call(
        name, (nC, nT), body,
        [seg(2), nseg(2), seg(3), hseg(3), seg(4), hseg(4),
         pl.BlockSpec((tb, C), lambda c, t: (t, c)), nseg(0),
         pl.BlockSpec((kw, C), lambda c, t: (0, c)), ANY],
        [p, p, p, p, p, p, dys, dys, conv_w, dp],
        [_S(dp.shape, dp.dtype), _S((kw, d), F32)], [ANY, pl.BlockSpec((kw, C), lambda c, t: (0, c))],
        [pltpu.VMEM((2, tb, C), BF16)] * 3 + [pltpu.SemaphoreType.DMA((2, 3))], tasks, own_aliases={9: 0})
    return (*outs, passed) if tasks else outs


def _merge_fwd(p, y_lru, y_sc, *, col0, name, tasks=()):
    T, D = y_lru.shape
    C = _tile(math.gcd(D, col0), 1024)
    nC = D // C
    k0 = col0 // C
    tb = _tile(T, 256, HALO)

    def body(gl_ref, gs_ref, yl_ref, ys_ref, o_ref):
        @pl.loop(0, tb // HALO)
        def _(k):
            rows = pl.ds(pl.multiple_of(k * HALO, HALO), HALO)
            for l0 in range(0, C, min(C, C_EW)):
                at = (rows, pl.ds(l0, min(C, C_EW)))
                o_ref[at] = (jax.nn.sigmoid(gl_ref[at].astype(F32)) * yl_ref[at].astype(F32)
                             + jax.nn.sigmoid(gs_ref[at].astype(F32)) * ys_ref[at].astype(F32)).astype(BF16)

    blk = pl.BlockSpec((tb, C), lambda c, t: (t, c))
    outs, passed = _call(
        name, (nC, T // tb), body,
        [pl.BlockSpec((tb, C), lambda c, t: (t, k0 + c)), pl.BlockSpec((tb, C), lambda c, t: (t, k0 + nC + c)),
         blk, blk], [p, p, y_lru, y_sc], [_S((T, D), BF16)], [blk], [], tasks)
    return (outs[0], passed) if tasks else outs[0]


def _merge_bwd(p, y_lru, y_sc, dm, *, col0, name, tasks=()):
    T, D = y_lru.shape
    C = _tile(math.gcd(D, col0), 1024)
    nC = D // C
    k0 = col0 // C
    tb = _tile(T, 256, HALO)
    nT = T // tb

    def body(gl_ref, gs_ref, yl_ref, ys_ref, dm_ref, dp_ref, dyl_ref, dys_ref, st_l, st_s, sems):
        step_no = pl.program_id(0) * nT + pl.program_id(1)
        slot = step_no % 2

        @pl.loop(0, tb // HALO)
        def _(k):
            rows = pl.ds(pl.multiple_of(k * HALO, HALO), HALO)
            for l0 in range(0, C, min(C, C_EW)):
                at = (rows, pl.ds(l0, min(C, C_EW)))
                dmv = dm_ref[at].astype(F32)
                sl = jax.nn.sigmoid(gl_ref[at].astype(F32))
                ss = jax.nn.sigmoid(gs_ref[at].astype(F32))
                dyl_ref[at] = (dmv * sl).astype(BF16)
                dys_ref[at] = (dmv * ss).astype(BF16)
                st_l[(slot,) + at] = (dmv * yl_ref[at].astype(F32) * sl * (1.0 - sl)).astype(BF16)
                st_s[(slot,) + at] = (dmv * ys_ref[at].astype(F32) * ss * (1.0 - ss)).astype(BF16)

        def where(s):
            row0, colc = (s % nT) * tb, (s // nT) * C
            return [(row0, col0 + colc), (row0, col0 + D + colc)]

        _store_staged([st_l, st_s], dp_ref, sems, step_no, nC * nT, where)

    blk = pl.BlockSpec((tb, C), lambda c, t: (t, c))
    outs, passed = _call(
        name, (nC, nT), body,
        [pl.BlockSpec((tb, C), lambda c, t: (t, k0 + c)), pl.BlockSpec((tb, C), lambda c, t: (t, k0 + nC + c)),
         blk, blk, blk], [p, p, y_lru, y_sc, dm],
        [_S(p.shape, BF16), _S((T, D), BF16), _S((T, D), BF16)], [ANY, blk, blk],
        [pltpu.VMEM((2, tb, C), BF16), pltpu.VMEM((2, tb, C), BF16), pltpu.SemaphoreType.DMA((2, 2))], tasks)
    return (*outs, passed) if tasks else outs


def _ffn_act_fwd(uu, conv_w, *, name, tasks=()):
    T = uu.shape[0]
    F = uu.shape[1] // 2
    C = _tile(F, C_EW)
    nC = F // C
    tb = _tile(T, TB, HALO)
    nT, hb = T // tb, tb // HALO
    kw = conv_w.shape[0]
    R = HALO

    def body(g_ref, gh_ref, v_ref, vh_ref, wg_ref, wv_ref, o_ref):
        t = pl.program_id(1)

        def chunk(k, carry):
            gp, vp = carry
            r0 = pl.multiple_of(k * R, R)
            ug = g_ref[pl.ds(r0, R), :].astype(F32)
            uv = v_ref[pl.ds(r0, R), :].astype(F32)
            cg = _conv(ug, gp, wg_ref)
            cv = _conv(uv, vp, wv_ref)
            o_ref[pl.ds(r0, R), :] = (cg * jax.nn.sigmoid(cg) * cv).astype(BF16)
            return ug[R - SUB:, :], uv[R - SUB:, :]

        lax.fori_loop(0, tb // R, chunk, (_prev8(gh_ref, t), _prev8(vh_ref, t)))

    seg = lambda k: pl.BlockSpec((tb, C), lambda c, t: (t, k * nC + c))
    hseg = lambda k: pl.BlockSpec((HALO, C), _halo_prev_map(hb, lambda c: k * nC + c))
    wseg = lambda k: pl.BlockSpec((kw, C), lambda c, t: (0, k * nC + c))
    outs, passed = _call(
        name, (nC, nT), body, [seg(0), hseg(0), seg(1), hseg(1), wseg(0), wseg(1)],
        [uu, uu, uu, uu, conv_w, conv_w], [_S((T, F), BF16)], [pl.BlockSpec((tb, C), lambda c, t: (t, c))], [], tasks)
    return (outs[0], passed) if tasks else outs[0]


def _ffn_act_bwd(uu, dact, conv_w, *, name):
    T = uu.shape[0]
    F = uu.shape[1] // 2
    C = _tile(F, C_EW)
    nC = F // C
    tb = _tile(T, TB, HALO)
    nT, hb = T // tb, tb // HALO
    kw = conv_w.shape[0]
    R = HALO
    nk = tb // R

    def body(g_ref, gh_ref, v_ref, vh_ref, da_ref, wg_ref, wv_ref, du_ref, dwg_ref, dwv_ref,
             gn_s, vn_s, accg_s, accv_s, st_g, st_v, sems):
        c = pl.program_id(0)
        tr = pl.program_id(1)
        t = nT - 1 - tr
        first = tr == 0

        @pl.when(first)
        def _():
            gn_s[...] = jnp.zeros_like(gn_s)
            vn_s[...] = jnp.zeros_like(vn_s)
            dwg_ref[...] = jnp.zeros_like(dwg_ref)
            dwv_ref[...] = jnp.zeros_like(dwv_ref)

        accg_s[...] = jnp.zeros_like(accg_s)
        accv_s[...] = jnp.zeros_like(accv_s)
        step_no = c * nT + tr
        slot = step_no % 2

        def chunk(i, carry):
            gn, vn = carry
            k = nk - 1 - i
            r0 = pl.multiple_of(k * R, R)
            rp = pl.multiple_of(jnp.maximum(r0 - R, 0), R)
            ug = g_ref[pl.ds(r0, R), :].astype(F32)
            uv = v_ref[pl.ds(r0, R), :].astype(F32)
            gp = jnp.where(k > 0, g_ref[pl.ds(rp, R), :].astype(F32)[R - SUB:, :], _prev8(gh_ref, t))
            vp = jnp.where(k > 0, v_ref[pl.ds(rp, R), :].astype(F32)[R - SUB:, :], _prev8(vh_ref, t))
            sh_g = _shifted_down(ug, gp, kw)
            sh_v = _shifted_down(uv, vp, kw)
            cg = _taps(sh_g, wg_ref)
            cv = _taps(sh_v, wv_ref)
            da = da_ref[pl.ds(r0, R), :].astype(F32)
            sg = jax.nn.sigmoid(cg)
            d_cg = da * cv * (sg * (1.0 + cg * (1.0 - sg)))
            d_cv = da * (cg * sg)
            for j in range(kw):
                accg_s[j] += d_cg * sh_g[kw - 1 - j]
                accv_s[j] += d_cv * sh_v[kw - 1 - j]
            st_g[slot, pl.ds(r0, R), :] = _conv_t(d_cg, gn, wg_ref).astype(BF16)
            st_v[slot, pl.ds(r0, R), :] = _conv_t(d_cv, vn, wv_ref).astype(BF16)
            return d_cg[:SUB, :], d_cv[:SUB, :]

        gn, vn = lax.fori_loop(0, nk, chunk, (gn_s[...], vn_s[...]))
        gn_s[...] = gn
        vn_s[...] = vn
        for j in range(kw):
            dwg_ref[pl.ds(j, 1), :] += jnp.sum(accg_s[j], axis=0, keepdims=True)
            dwv_ref[pl.ds(j, 1), :] += jnp.sum(accv_s[j], axis=0, keepdims=True)
        def where(s):
            row0, col0 = (nT - 1 - s % nT) * tb, (s // nT) * C
            return [(row0, col0), (row0, F + col0)]

        _store_staged([st_g, st_v], du_ref, sems, step_no, nC * nT, where)

    seg = lambda k: pl.BlockSpec((tb, C), lambda c, tr: (nT - 1 - tr, k * nC + c))
    hseg = lambda k: pl.BlockSpec((HALO, C), lambda c, tr: (jnp.maximum((nT - 1 - tr) * hb - 1, 0), k * nC + c))
    wseg = lambda k: pl.BlockSpec((kw, C), lambda c, tr: (0, k * nC + c))
    dwb = pl.BlockSpec((kw, C), lambda c, tr: (0, c))
    return pl.pallas_call(
        body, name=name, out_shape=(_S(uu.shape, BF16), _S((kw, F), F32), _S((kw, F), F32)), grid=(nC, nT),
        in_specs=[seg(0), hseg(0), seg(1), hseg(1), pl.BlockSpec((tb, C), lambda c, tr: (nT - 1 - tr, c)),
                  wseg(0), wseg(1)],
        out_specs=(ANY, dwb, dwb),
        scratch_shapes=[pltpu.VMEM((SUB, C), F32), pltpu.VMEM((SUB, C), F32),
                        pltpu.VMEM((kw, R, C), F32), pltpu.VMEM((kw, R, C), F32),
                        pltpu.VMEM((2, tb, C), BF16), pltpu.VMEM((2, tb, C), BF16), pltpu.SemaphoreType.DMA((2, 2))],
        compiler_params=_cparams(("arbitrary", "arbitrary")),
    )(uu, uu, uu, uu, dact, conv_w, conv_w)


def _place():
    x, y, c = lax.axis_index("x"), lax.axis_index("y"), lax.axis_index("c")
    return x, y, c


def _chips(x, y):
    return [(1 - x, y), (x, 1 - y), (1 - x, 1 - y)]


def _all_gather(arrays, placed, over_ici, pair_n, name):
    n = len(arrays)

    def body(*refs):
        ins, outs = refs[:n], refs[n:2 * n]
        send_sems, recv_sems, local_sems = refs[2 * n:]
        x, y, c = _place()
        me, sibling = (x, y, c), (x, y, 1 - c)
        chips = _chips(x, y)
        full = [a for a in range(n) if over_ici[a]]

        def idx(px, py, pc):
            return 4 * px + 2 * py + pc

        def copy(a, k, block, to):
            dst = _dev_block(outs[a], idx(*block), pair_n[a])
            src = ins[a] if (block is me and not placed[a]) else dst
            return pltpu.make_async_remote_copy(
                src_ref=src, dst_ref=dst, send_sem=send_sems.at[a, k], recv_sem=recv_sems.at[a, k],
                device_id=to, device_id_type=MESH)

        def half(a, k, block, to, lo):
            r = rows_of[a] // 2
            blk = _rows_of(outs[a], idx(*block), (0 if lo else r, r), pair_n[a])
            return pltpu.make_async_remote_copy(
                src_ref=blk, dst_ref=blk, send_sem=send_sems.at[a, k], recv_sem=recv_sems.at[a, k],
                device_id=to, device_id_type=MESH)

        mine = [pltpu.make_async_copy(ins[a], outs[a].at[idx(*me)], local_sems.at[a])
                for a in range(n) if not placed[a]]
        for cp in mine:
            cp.start()
        chip_x, chip_y, chip_d = chips
        sent = []
        for a in full:
            sent += [copy(a, 1, me, (*chip_x, c)), copy(a, 2, me, (*chip_y, c))]
            if not relay[a]:
                sent.append(copy(a, 3, me, (*chip_d, c)))
        for a in range(n):
            sent.append(copy(a, 0, me, sibling))
        for cp in sent:
            cp.start()

        def then(cp):
            cp.start()
            sent.append(cp)

        for a in full:
            copy(a, 2, (*chip_y, c), me).wait_recv()
            if relay[a]:
                then(half(a, 3, (*chip_y, c), (*chip_x, c), True))
            then(copy(a, 6, (*chip_y, c), sibling))
            copy(a, 1, (*chip_x, c), me).wait_recv()
            if relay[a]:
                then(half(a, 4, (*chip_x, c), (*chip_y, c), False))
            then(copy(a, 5, (*chip_x, c), sibling))
        for a in full:
            if relay[a]:
                half(a, 3, (*chip_d, c), me, True).wait_recv()
                half(a, 4, (*chip_d, c), me, False).wait_recv()
            else:
                copy(a, 3, (*chip_d, c), me).wait_recv()
            then(copy(a, 7, (*chip_d, c), sibling))
        for a in range(n):
            copy(a, 0, sibling, me).wait_recv()
        for a in full:
            for j, chip in enumerate(chips):
                copy(a, 5 + j, (*chip, 1 - c), me).wait_recv()
        for cp in sent:
            cp.wait_send()
        for cp in mine:
            cp.wait()

    rows_of = [(s.shape[1] if placed[a] else s.shape[0]) for a, s in enumerate(arrays)]
    relay = [r % (2 * HALO) == 0 for r in rows_of]
    return pl.pallas_call(
        body, name=name,
        out_shape=tuple(_S(s.shape if placed[a] else (N_DEV,) + s.shape, s.dtype) for a, s in enumerate(arrays)),
        in_specs=[ANY] * n, out_specs=tuple([ANY] * n),
        scratch_shapes=[pltpu.SemaphoreType.DMA((n, 8)), pltpu.SemaphoreType.DMA((n, 8)),
                        pltpu.SemaphoreType.DMA((n,))],
        input_output_aliases={a: a for a in range(n) if placed[a]},
    )(*arrays)


def _dev_block(ref, dev, pair_n=None):
    if pair_n is None:
        return ref.at[dev]
    return ref.at[dev // 2, :, pl.ds(pl.multiple_of((dev % 2) * pair_n, LANES), pair_n)]


def _rows_of(ref, blk, rows, pair_n=None):
    v = _dev_block(ref, blk, pair_n)
    return v if rows is None else v.at[pl.ds(rows[0], rows[1])]


ALL_ROWS = "all"


def _gather_task(buf, ici=None, fwd=None, pair_n=None):
    rows = lambda r: None if r == ALL_ROWS else r
    blk_of = functools.partial(_rows_of, pair_n=pair_n)

    def copies(refs, ss, rs):
        x, y, c = _place()
        me = 4 * x + 2 * y + c
        cps = []
        for j, (px, py) in enumerate(_chips(x, y)):
            if ici is not None:
                blk = blk_of(refs[0], me, rows(ici))
                cps.append(pltpu.make_async_remote_copy(
                    src_ref=blk, dst_ref=blk, send_sem=ss.at[j], recv_sem=rs.at[j],
                    device_id=(px, py, c), device_id_type=MESH))
            if fwd is not None:
                blk = blk_of(refs[0], 4 * px + 2 * py + c, rows(fwd))
                cps.append(pltpu.make_async_remote_copy(
                    src_ref=blk, dst_ref=blk, send_sem=ss.at[3 + j], recv_sem=rs.at[3 + j],
                    device_id=(x, y, 1 - c), device_id_type=MESH))
        return cps

    def start(refs, ss, rs, ls):
        for cp in copies(refs, ss, rs):
            cp.start()

    def wait(refs, ss, rs, ls):
        x, y, c = _place()
        for j, (px, py) in enumerate(_chips(x, y)):
            if ici is not None:
                blk = blk_of(refs[0], 4 * px + 2 * py + c, rows(ici))
                pltpu.make_async_remote_copy(
                    src_ref=blk, dst_ref=blk, send_sem=ss.at[j], recv_sem=rs.at[j],
                    device_id=(px, py, c), device_id_type=MESH).wait_recv()
            if fwd is not None:
                blk = blk_of(refs[0], 4 * px + 2 * py + 1 - c, rows(fwd))
                pltpu.make_async_remote_copy(
                    src_ref=blk, dst_ref=blk, send_sem=ss.at[3 + j], recv_sem=rs.at[3 + j],
                    device_id=(x, y, 1 - c), device_id_type=MESH).wait_recv()
        for cp in copies(refs, ss, rs):
            cp.wait_send()

    return _Task([buf], [0], start, wait, nsem=6)


def _exchange_task(parts, landing, rows=None):
    def copies(refs, ss, rs):
        x, y, c = _place()
        myq = 2 * x + y
        return [pltpu.make_async_remote_copy(
            src_ref=_rows_of(refs[0], 2 * px + py, rows), dst_ref=_rows_of(refs[1], myq, rows),
            send_sem=ss.at[k], recv_sem=rs.at[k], device_id=(px, py, c), device_id_type=MESH)
            for k, (px, py) in enumerate(_chips(x, y))]

    def start(refs, ss, rs, ls):
        for cp in copies(refs, ss, rs):
            cp.start()

    def wait(refs, ss, rs, ls):
        x, y, c = _place()
        for k, (px, py) in enumerate(_chips(x, y)):
            pltpu.make_async_remote_copy(
                src_ref=_rows_of(refs[0], 2 * x + y, rows), dst_ref=_rows_of(refs[1], 2 * px + py, rows),
                send_sem=ss.at[k], recv_sem=rs.at[k], device_id=(px, py, c), device_id_type=MESH).wait_recv()
        for cp in copies(refs, ss, rs):
            cp.wait_send()

    return _Task([parts, landing], [1], start, wait)


def _core_blocks(g, pair_n):
    if pair_n is None:
        g4 = g.reshape((N_CHIP, 2) + g.shape[1:])
        return g4, (N_CHIP,) + g.shape[1:], lambda ref, c: ref.at[:, c]
    view = lambda ref, c: ref.at[:, :, pl.ds(pl.multiple_of(c * pair_n, LANES), pair_n)]
    return g, (N_CHIP, g.shape[1], pair_n), view


def _swap_task(g, pair_n=None):
    g4, shape, view = _core_blocks(g, pair_n)

    def copy(refs, ss, rs):
        x, y, c = _place()
        return pltpu.make_async_remote_copy(
            src_ref=view(refs[0], 1 - c), dst_ref=refs[1], send_sem=ss.at[0], recv_sem=rs.at[0],
            device_id=(x, y, 1 - c), device_id_type=MESH)

    def start(refs, ss, rs, ls):
        copy(refs, ss, rs).start()

    def wait(refs, ss, rs, ls):
        copy(refs, ss, rs).wait()

    return _Task([g4], [], start, wait, fresh=[_S(shape, g.dtype)], nsem=1)


def _peer(x, y, c, m):
    return x ^ (m >> 2), y ^ ((m >> 1) & 1), c ^ (m & 1)


def _bcast_task(pack):
    def copies(refs, ss, rs):
        x, y, c = _place()
        me = 4 * x + 2 * y + c
        return [pltpu.make_async_remote_copy(
            src_ref=refs[0], dst_ref=refs[1].at[me], send_sem=ss.at[m - 1], recv_sem=rs.at[m - 1],
            device_id=_peer(x, y, c, m), device_id_type=MESH) for m in range(1, N_DEV)]

    def local(refs, ls):
        x, y, c = _place()
        return pltpu.make_async_copy(refs[0], refs[1].at[4 * x + 2 * y + c], ls.at[0])

    def start(refs, ss, rs, ls):
        local(refs, ls).start()
        for cp in copies(refs, ss, rs):
            cp.start()

    def wait(refs, ss, rs, ls):
        x, y, c = _place()
        for m in range(1, N_DEV):
            px, py, pc = _peer(x, y, c, m)
            pltpu.make_async_remote_copy(
                src_ref=refs[0], dst_ref=refs[1].at[4 * px + 2 * py + pc], send_sem=ss.at[m - 1],
                recv_sem=rs.at[m - 1], device_id=(px, py, pc), device_id_type=MESH).wait_recv()
        for cp in copies(refs, ss, rs):
            cp.wait_send()
        local(refs, ls).wait()

    return _Task([pack], [], start, wait, fresh=[_S((N_DEV,) + pack.shape, pack.dtype)], nsem=N_DEV - 1)


def _sum_packs(packs, name):
    _, R, L = packs.shape

    def body(p_ref, o_ref):
        acc = p_ref[0]
        for k in range(1, N_DEV):
            acc = acc + p_ref[k]
        o_ref[...] = acc

    return pl.pallas_call(body, name=name, out_shape=_S((R, L), packs.dtype), in_specs=[VMEM_SPEC],
                          out_specs=VMEM_SPEC, compiler_params=_cparams())(packs)


def _swap_halves(g, name, pair_n=None):
    g4, shape, view = _core_blocks(g, pair_n)

    def body(g_ref, o_ref, send_sem, recv_sem):
        x, y, c = _place()
        cp = pltpu.make_async_remote_copy(
            src_ref=view(g_ref, 1 - c), dst_ref=o_ref, send_sem=send_sem, recv_sem=recv_sem,
            device_id=(x, y, 1 - c), device_id_type=MESH)
        cp.start()
        cp.wait()

    return pl.pallas_call(
        body, name=name, out_shape=_S(shape, g.dtype), in_specs=[ANY], out_specs=ANY,
        scratch_shapes=[pltpu.SemaphoreType.DMA, pltpu.SemaphoreType.DMA],
    )(g4)


def _add_halves(g, landed, place, name, pair_n=None):
    _, r, cc = landed.shape
    tr = _tile(r, 512, HALO)
    if pair_n is None:
        g4 = g.reshape(N_CHIP, 2, r, cc)
        g_spec = pl.BlockSpec((None, None, tr, cc), lambda i, q, s: (q, s[0], i, 0))
    else:
        g4 = g
        g_spec = pl.BlockSpec((None, tr, cc), lambda i, q, s: (q, i, s[0]))

    def body(s_ref, g_ref, l_ref, o_ref, land_ref):
        q = pl.program_id(1)
        v = (g_ref[...].astype(F32) + l_ref[...].astype(F32)).astype(BF16)
        o_ref[...] = v

        @pl.when(q == s_ref[1])
        def _():
            land_ref[...] = v

    return pl.pallas_call(
        body, name=name, out_shape=(_S((N_CHIP, r, cc), BF16), _S((N_CHIP, r, cc), BF16)),
        grid_spec=pltpu.PrefetchScalarGridSpec(
            num_scalar_prefetch=1, grid=(r // tr, N_CHIP),
            in_specs=[g_spec,
                      pl.BlockSpec((None, tr, cc), lambda i, q, s: (q, i, 0))],
            out_specs=(pl.BlockSpec((None, tr, cc), lambda i, q, s: (q, i, 0)),
                       pl.BlockSpec((None, tr, cc), lambda i, q, s: (s[1], i, 0)))),
        compiler_params=_cparams(("arbitrary", "arbitrary")),
    )(place, g4, landed)


def _all_reduce_small(pack, name):
    R = pack.shape[0]

    def body(p_ref, o_ref, buf, send_sems, recv_sems):
        x, y, c = _place()
        me = 4 * x + 2 * y + c
        buf[me] = p_ref[...]
        cps = []
        for k in range(N_DEV - 1):
            m = k + 1
            peer = (x ^ (m >> 2), y ^ ((m >> 1) & 1), c ^ (m & 1))
            cps.append(pltpu.make_async_remote_copy(
                src_ref=p_ref, dst_ref=buf.at[me], send_sem=send_sems.at[k], recv_sem=recv_sems.at[k],
                device_id=peer, device_id_type=MESH))
        for cp in cps:
            cp.start()
        for k in range(N_DEV - 1):
            m = k + 1
            peer_idx = 4 * (x ^ (m >> 2)) + 2 * (y ^ ((m >> 1) & 1)) + (c ^ (m & 1))
            pltpu.make_async_remote_copy(
                src_ref=p_ref, dst_ref=buf.at[peer_idx], send_sem=send_sems.at[k], recv_sem=recv_sems.at[k],
                device_id=(x, y, c), device_id_type=MESH).wait_recv()
        for cp in cps:
            cp.wait_send()
        acc = buf[0]
        for k in range(1, N_DEV):
            acc = acc + buf[k]
        o_ref[...] = acc

    return pl.pallas_call(
        body, name=name, out_shape=_S((R, LANES), F32),
        in_specs=[VMEM_SPEC], out_specs=VMEM_SPEC,
        scratch_shapes=[pltpu.VMEM((N_DEV, R, LANES), F32), pltpu.SemaphoreType.DMA((N_DEV - 1,)),
                        pltpu.SemaphoreType.DMA((N_DEV - 1,))],
        compiler_params=_cparams(),
    )(pack)


def _adamw_math(w, g, m, v):
    m = ADAM_B1 * m + (1.0 - ADAM_B1) * g
    v = ADAM_B2 * v + (1.0 - ADAM_B2) * (g * g)
    m_hat = m / (1.0 - ADAM_B1 ** ADAM_STEP)
    v_hat = v / (1.0 - ADAM_B2 ** ADAM_STEP)
    delta = -ADAM_LR * (m_hat / (jnp.sqrt(v_hat) + ADAM_EPS) + ADAM_WD * w)
    return delta, m, v


def _adamw_block(p_ref, w_ref, m_ref, v_ref, g_ref, d_ref, nm_ref, nv_ref):
    g = p_ref[0].astype(F32)
    for q in range(1, N_CHIP):
        g = g + p_ref[q].astype(F32)
    g_ref[...] = g
    d_ref[...], nm_ref[...], nv_ref[...] = _adamw_math(w_ref[...], g, m_ref[...], v_ref[...])


def _adamw_side(parts, w, m, v):
    r, cc = w.shape

    def make(grid):
        steps = math.prod(grid)
        tr = HALO * -(-r // (HALO * steps))
        if r % tr:
            return None
        n_blocks = r // tr

        def lin(*ids):
            s = 0
            for i, g in zip(ids, grid):
                s = s * g + i
            return s

        blk = pl.BlockSpec((tr, cc), lambda *ids: (jnp.minimum(lin(*ids), n_blocks - 1), 0))
        pblk = pl.BlockSpec((N_CHIP, tr, cc), lambda *ids: (0, jnp.minimum(lin(*ids), n_blocks - 1), 0))

        def fn(*refs):
            s = lin(*[pl.program_id(a) for a in range(len(grid))])

            @pl.when(s < n_blocks)
            def _():
                _adamw_block(*refs)

        return _Side([pblk, blk, blk, blk], [parts, w, m, v], [_S((r, cc), F32)] * 4, [blk] * 4, fn)

    return make


def _adamw_big(parts, w, m, v, name):
    r, cc = w.shape
    tr = _tile(r, 128, HALO)
    body = functools.partial(_adamw_block)

    blk = pl.BlockSpec((tr, cc), lambda i: (i, 0))
    return pl.pallas_call(
        body, name=name, out_shape=tuple(_S((r, cc), F32) for _ in range(4)), grid=(r // tr,),
        in_specs=[pl.BlockSpec((N_CHIP, tr, cc), lambda i: (0, i, 0)), blk, blk, blk],
        out_specs=(blk, blk, blk, blk), compiler_params=_cparams(("parallel",)),
    )(parts, w, m, v)


def _adamw_small(ws, gs, ms, vs, name):
    n = len(ws)

    def body(*refs):
        w_r, g_r, m_r, v_r = refs[:n], refs[n:2 * n], refs[2 * n:3 * n], refs[3 * n:4 * n]
        d_r, nm_r, nv_r = refs[4 * n:5 * n], refs[5 * n:6 * n], refs[6 * n:7 * n]
        for k in range(n):
            d_r[k][...], nm_r[k][...], nv_r[k][...] = _adamw_math(w_r[k][...], g_r[k][...], m_r[k][...], v_r[k][...])

    shapes = tuple(_S(w.shape, F32) for w in ws)
    outs = pl.pallas_call(
        body, name=name, out_shape=shapes * 3,
        in_specs=[VMEM_SPEC] * (4 * n), out_specs=tuple([VMEM_SPEC] * (3 * n)),
        compiler_params=_cparams(),
    )(*ws, *gs, *ms, *vs)
    return outs[:n], outs[n:2 * n], outs[2 * n:]


def _block_diag(w, heads_per_block):
    H, hd, _ = w.shape
    nb = H // heads_per_block
    eye = jnp.eye(heads_per_block, dtype=w.dtype)
    w4 = w.reshape(nb, heads_per_block, hd, hd)
    return jnp.einsum("nhab,hg->nhagb", w4, eye).reshape(nb, heads_per_block * hd, heads_per_block * hd)


def _diag_blocks(bd, heads_per_block, hd):
    nb = bd.shape[0]
    b5 = bd.reshape(nb, heads_per_block, hd, heads_per_block, hd)
    return jnp.stack([b5[:, h, :, h, :] for h in range(heads_per_block)], axis=1).reshape(nb * heads_per_block, hd, hd)


def kernel(x, g_mix, w_in, lru_conv_w, lru_conv_b, lru_wa, lru_ba, lru_wx, lru_bx, lru_lambda, lru_w_out, sc_conv_w, sc_w_out, w_o, g_ffn, ffn_w_up, ffn_conv_w, ffn_w_down, g_final, loss_target, m_g_mix, m_w_in, m_lru_conv_w, m_lru_conv_b, m_lru_wa, m_lru_ba, m_lru_wx, m_lru_bx, m_lru_lambda, m_lru_w_out, m_sc_conv_w, m_sc_w_out, m_w_o, m_g_ffn, m_ffn_w_up, m_ffn_conv_w, m_ffn_w_down, m_g_final, v_g_mix, v_w_in, v_lru_conv_w, v_lru_conv_b, v_lru_wa, v_lru_ba, v_lru_wx, v_lru_bx, v_lru_lambda, v_lru_w_out, v_sc_conv_w, v_sc_w_out, v_w_o, v_g_ffn, v_ffn_w_up, v_ffn_conv_w, v_ffn_w_down, v_g_final):
    T, D = x.shape[1], x.shape[2]
    d_lru = lru_lambda.shape[0]
    d_sc = sc_conv_w.shape[1] * N_DEV
    F = ffn_w_down.shape[0] * N_DEV
    H = lru_wa.shape[0]
    assert d_lru == d_sc and H * HEAD_DIM == d_lru
    xs = x.reshape(T, D)
    tgt = loss_target.reshape(T, D)
    my_x, my_y, my_c = _place()
    me = 4 * my_x + 2 * my_y + my_c

    big = [w_in, lru_w_out, sc_w_out, w_o, ffn_w_up, ffn_w_down]
    big_names = ["w_in", "lru_w_out", "sc_w_out", "w_o", "ffn_w_up", "ffn_w_down"]
    place = jnp.stack([my_c, 2 * my_x + my_y, me]).astype(jnp.int32)
    n_in, n_up = w_in.shape[1], ffn_w_up.shape[1]
    paired = [n_in, None, None, None, n_up, None]
    big_bf = [_cast_into_slot(w, place, "cast_" + nm, paired=pn is not None)
              for w, nm, pn in zip(big, big_names, paired)]
    pad_rows = lambda a: jnp.pad(a, ((0, SUB - a.shape[0]), (0, 0)))
    gathered = _all_gather(big_bf + [pad_rows(lru_conv_w), pad_rows(sc_conv_w), pad_rows(ffn_conv_w)],
                           [True] * 6 + [False] * 3,
                           [True, False, False, False, False, False, True, True, True],
                           paired + [None] * 3, "all_gather_first")
    W_in, W_lo, W_so, W_o8, W_up, W_dn8 = gathered[:6]
    full_cols = lambda g, kw: g[:, :kw, :].transpose(1, 0, 2).reshape(kw, -1)
    cw_lru = full_cols(gathered[6], lru_conv_w.shape[0])
    cw_sc = full_cols(gathered[7], sc_conv_w.shape[0])
    cw_ffn = full_cols(gathered[8], ffn_conv_w.shape[0])

    C = _tile(d_lru, C_LRU)
    hpb = C // HEAD_DIM
    wa_bd = _block_diag(lru_wa, hpb).astype(BF16)
    wx_bd = _block_diag(lru_wx, hpb).astype(BF16)
    cb, ba, bx, lam = (a.reshape(1, d_lru) for a in (lru_conv_b, lru_ba, lru_bx, lru_lambda))

    h1 = _rms_fwd(xs, g_mix, "rms_mix")
    k8 = W_up.shape[1] // 8
    wide = 2 * max(n_in, n_up)
    p, ((W_o8,), (W_lo,), (W_so,), (W_up,)) = _mm_nn(
        h1, W_in, out_dtype=BF16, name="mm_in", tn=wide,
        tasks=[_gather_task(W_o8, ici=ALL_ROWS), _gather_task(W_lo, ici=ALL_ROWS), _gather_task(W_so, ici=ALL_ROWS),
               _gather_task(W_up, ici=(0, 3 * k8), pair_n=n_up)])
    hs, yl_pre, ((W_o8,), (W_lo,), (W_so,), (W_up,)) = _lru_fwd(
        p, cw_lru, cb, wa_bd, ba, wx_bd, bx, lam, name="lru_fwd",
        tasks=[_gather_task(W_o8, fwd=ALL_ROWS), _gather_task(W_lo, fwd=ALL_ROWS), _gather_task(W_so, fwd=ALL_ROWS),
               _gather_task(W_up, ici=(3 * k8, 2 * k8), fwd=(0, 3 * k8), pair_n=n_up)])
    ys_pre = _sc_fwd(p, cw_sc, d=d_sc, name="sc_fwd")
    y_lru, ((W_up,),) = _mm_small(
        "nn", yl_pre, None, W_lo, name="mm_lru_out",
        tasks=[_gather_task(W_up, ici=(5 * k8, k8), fwd=(3 * k8, 2 * k8), pair_n=n_up)])
    y_sc, ((W_up,),) = _mm_small(
        "nn", ys_pre, None, W_so, name="mm_sc_out",
        tasks=[_gather_task(W_up, ici=(6 * k8, k8), fwd=(5 * k8, k8), pair_n=n_up)])
    gate0 = 2 * d_lru + 3 * d_sc
    merged, ((W_up,),) = _merge_fwd(p, y_lru, y_sc, col0=gate0, name="merge_fwd",
                                    tasks=[_gather_task(W_up, ici=(7 * k8, k8), fwd=(6 * k8, k8), pair_n=n_up)])
    W_o = W_o8.reshape(1, D, D)
    x1, ((W_up,),) = _mm_nn(merged, W_o, out_dtype=F32, residual=xs, name="mm_o",
                            tasks=[_gather_task(W_up, fwd=(7 * k8, k8), pair_n=n_up)])
    h2 = _rms_fwd(x1, g_ffn, "rms_ffn")
    uu, ((W_dn8,),) = _mm_nn(h2, W_up, out_dtype=BF16, name="mm_up", tn=wide,
                             tasks=[_gather_task(W_dn8, ici=ALL_ROWS)])
    act, ((W_dn8,),) = _ffn_act_fwd(uu, cw_ffn, name="ffn_act_fwd", tasks=[_gather_task(W_dn8, fwd=ALL_ROWS)])
    W_dn = W_dn8.reshape(1, F, D)
    x2 = _mm_nn(act, W_dn, out_dtype=F32, residual=x1, name="mm_down", tn=1024, tk=F)
    dx2, dx2b, loss_part, dg_final = _loss_head(x2, g_final, tgt, "loss_head")

    def pack_rows(arrs):
        flat = jnp.concatenate([a.reshape(-1) for a in arrs])
        rows = -(-flat.shape[0] // (SUB * LANES)) * SUB
        return jnp.pad(flat, (0, rows * LANES - flat.shape[0])).reshape(rows, LANES)

    def unpack_rows(pack, arrs):
        flat, out, o = pack.reshape(-1), [], 0
        for a in arrs:
            out.append(flat[o:o + a.size].reshape(a.shape))
            o += a.size
        return out

    dact = _mm_nt(dx2b, W_dn, out_dtype=BF16, name="mm_down_dx", tm=512, tko=F // 2, tn=D)
    gW_dn = _mm_tn(act, dx2b, 1, out_dtype=BF16, name="mm_down_dw", tk=1408, tt=2048).reshape(N_DEV, F // N_DEV, D)
    duu, dcw_ffn_g, dcw_ffn_v = _ffn_act_bwd(uu, dact, cw_ffn, name="ffn_act_bwd")
    dh2, ((land_dn,),) = _mm_nt(duu, W_up, out_dtype=BF16, name="mm_up_dx", tn=wide, tasks=[_swap_task(gW_dn)])
    parts_dn = _add_halves(gW_dn, land_dn, place, "rs_add_ffn_w_down")
    gW_up, ((mine_dn,),) = _mm_tn(h2, duu, N_CHIP, out_dtype=BF16, name="mm_up_dw", tk=512, tn=wide, tt=2048,
                                  tasks=[_exchange_task(*parts_dn)])
    dx1, dx1b, dg_ffn = _rms_bwd(x1, g_ffn, dh2, dx2, "rms_ffn_bwd")
    dmerged, ((land_up,),) = _mm_nt(dx1b, W_o, out_dtype=BF16, name="mm_o_dx", tn=D,
                                    tasks=[_swap_task(gW_up, pair_n=n_up)])
    parts_up, land_up = _add_halves(gW_up, land_up, place, "rs_add_ffn_w_up", pair_n=n_up)
    r8 = parts_up.shape[1] // 8
    gW_o, ((land_up,),) = _mm_tn(merged, dx1b, 1, out_dtype=BF16, name="mm_o_dw", tt=2048,
                                 tasks=[_exchange_task(parts_up, land_up, rows=(0, r8))])
    gW_o = gW_o.reshape(N_DEV, D // N_DEV, D)
    dp, dy_lru, dy_sc, ((land_up,),) = _merge_bwd(
        p, y_lru, y_sc, dmerged, col0=gate0, name="merge_bwd",
        tasks=[_exchange_task(parts_up, land_up, rows=(r8, 2 * r8))])
    dyl_pre, ((land_o,),) = _mm_small("nt", None, dy_lru, W_lo, name="mm_lru_out_dx", tasks=[_swap_task(gW_o)])
    parts_o = _add_halves(gW_o, land_o, place, "rs_add_w_o")
    gW_lo = _mm_small("tn", yl_pre, dy_lru, W_lo, name="mm_lru_out_dw")
    dys_pre, ((land_lo,),) = _mm_small("nt", None, dy_sc, W_so, name="mm_sc_out_dx", tasks=[_swap_task(gW_lo)])
    parts_lo = _add_halves(gW_lo, land_lo, place, "rs_add_lru_w_out")
    gW_so = _mm_small("tn", ys_pre, dy_sc, W_so, name="mm_sc_out_dw")
    dp, dcw_sc, ((land_up,),) = _sc_bwd(p, dys_pre, dp, cw_sc, d=d_sc, name="sc_bwd",
                                        tasks=[_exchange_task(parts_up, land_up, rows=(3 * r8, r8))])
    dp, dcw_lru, dcb, dwa_bd, dba, dwx_bd, dbx, dlam, ((land_up,), (mine_o,), (mine_lo,), (land_so,)) = _lru_bwd(
        p, hs, dyl_pre, dp, cw_lru, cb, wa_bd, ba, wx_bd, bx, lam, name="lru_bwd",
        tasks=[_exchange_task(parts_up, land_up, rows=(4 * r8, 2 * r8)), _exchange_task(*parts_o),
               _exchange_task(*parts_lo), _swap_task(gW_so)])
    parts_so = _add_halves(gW_so, land_so, place, "rs_add_sc_w_out")

    dwa = _diag_blocks(dwa_bd, hpb, HEAD_DIM)
    dwx = _diag_blocks(dwx_bd, hpb, HEAD_DIM)
    dcw_ffn = jnp.concatenate([dcw_ffn_g, dcw_ffn_v], axis=1)
    rep_grads = [dcb, dwa, dba, dwx, dbx, dlam, dg_ffn, dg_final]
    small_full = rep_grads + [dcw_lru, dcw_sc, dcw_ffn]
    gW_in, ((mine_up,), (mine_so,), (packs,)), adam_dn = _mm_tn(
        h1, dp, N_CHIP, out_dtype=BF16, name="mm_in_dw", tk=512, tn=wide, tt=2048,
        tasks=[_exchange_task(parts_up, land_up, rows=(6 * r8, 2 * r8)), _exchange_task(*parts_so),
               _bcast_task(pack_rows(small_full))],
        side=_adamw_side(mine_dn, ffn_w_down, m_ffn_w_down, v_ffn_w_down))
    land_in = _swap_halves(gW_in, "rs_swap_w_in", pair_n=n_in)
    parts_in = _add_halves(gW_in, land_in, place, "rs_add_w_in", pair_n=n_in)
    dh1, ((mine_in,),), adam_up = _mm_nt(
        dp, W_in, out_dtype=BF16, name="mm_in_dx", tn=wide, tasks=[_exchange_task(*parts_in)],
        side=_adamw_side(mine_up, ffn_w_up, m_ffn_w_up, v_ffn_w_up))
    grad_x, _, dg_mix = _rms_bwd(xs, g_mix, dh1, dx1, "rms_mix_bwd")

    mine = [mine_in, mine_lo, mine_so, mine_o, mine_up, mine_dn]
    big_m = [m_w_in, m_lru_w_out, m_sc_w_out, m_w_o, m_ffn_w_up, m_ffn_w_down]
    big_v = [v_w_in, v_lru_w_out, v_sc_w_out, v_w_o, v_ffn_w_up, v_ffn_w_down]
    done = {"ffn_w_down": adam_dn, "ffn_w_up": adam_up}
    big_out = {nm: done.get(nm) or _adamw_big(pt, w, m, v, "adamw_" + nm)
               for nm, pt, w, m, v in zip(big_names, mine, big, big_m, big_v)}

    total = _sum_packs(packs, "sum_small").reshape(-1)
    n_rep = sum(a.size for a in rep_grads)
    scw_lru, scw_sc, scw_ffn = unpack_rows(total[n_rep:], small_full[len(rep_grads):])
    sg_mix = _all_reduce_small(pack_rows([dg_mix]), "all_reduce_g_mix").reshape(-1)[:D]
    rep_names = ["g_mix", "lru_conv_b", "lru_wa", "lru_ba", "lru_wx", "lru_bx", "lru_lambda", "g_ffn", "g_final"]
    rep_w = [g_mix, lru_conv_b, lru_wa, lru_ba, lru_wx, lru_bx, lru_lambda, g_ffn, g_final]
    rep_m = [m_g_mix, m_lru_conv_b, m_lru_wa, m_lru_ba, m_lru_wx, m_lru_bx, m_lru_lambda, m_g_ffn, m_g_final]
    rep_v = [v_g_mix, v_lru_conv_b, v_lru_wa, v_lru_ba, v_lru_wx, v_lru_bx, v_lru_lambda, v_g_ffn, v_g_final]
    g_pack = pack_rows([sg_mix, total[:n_rep]])

    def my_cols(a):
        n = a.shape[1] // N_DEV
        return lax.dynamic_slice_in_dim(a, me * n, n, axis=1)

    conv_names = ["lru_conv_w", "sc_conv_w", "ffn_conv_w"]
    conv_w = [lru_conv_w, sc_conv_w, ffn_conv_w]
    conv_g = [my_cols(scw_lru), my_cols(scw_sc), my_cols(scw_ffn)]
    sd, snm, snv = _adamw_small(
        [pack_rows(rep_w)] + conv_w, [g_pack] + conv_g,
        [pack_rows(rep_m), m_lru_conv_w, m_sc_conv_w, m_ffn_conv_w],
        [pack_rows(rep_v), v_lru_conv_w, v_sc_conv_w, v_ffn_conv_w], "adamw_small")
    small_out = dict(zip(rep_names, zip(*(unpack_rows(pk, rep_w) for pk in (g_pack, sd[0], snm[0], snv[0])))))
    small_out.update(zip(conv_names, zip(conv_g, sd[1:], snm[1:], snv[1:])))

    loss = lax.psum(loss_part[0, 0], AXES)
    order = ["g_mix", "w_in", "lru_conv_w", "lru_conv_b", "lru_wa", "lru_ba", "lru_wx", "lru_bx", "lru_lambda",
             "lru_w_out", "sc_conv_w", "sc_w_out", "w_o", "g_ffn", "ffn_w_up", "ffn_conv_w", "ffn_w_down", "g_final"]
    res = {**big_out, **small_out}
    return (loss, grad_x.reshape(x.shape),
            *[res[nm][0] for nm in order], *[res[nm][1] for nm in order],
            *[res[nm][2] for nm in order], *[res[nm][3] for nm in order])
```

```python
import functools
import math

import jax
import jax.numpy as jnp
from jax import lax
from jax.experimental import pallas as pl
from jax.experimental.pallas import tpu as pltpu

F32, BF16 = jnp.float32, jnp.bfloat16
MESH = pl.DeviceIdType.MESH
N_DEV = 8
N_CHIP = 4
AXES = ("x", "y", "c")

EPS = 1e-6
LRU_C = 8.0
HEAD_DIM = 64
ADAM_LR, ADAM_B1, ADAM_B2, ADAM_EPS, ADAM_WD, ADAM_STEP = 0.001, 0.9, 0.999, 1e-08, 0.01, 10

VMEM_LIMIT = 48 * 1024 * 1024
LANES = 128
SUB = 8
HALO = 16
TB = 512
C_LRU = 256
C_EW = 512
TM, TN, TK = 512, 1536, 2048


def _tile(n, pref, align=LANES):
    best = None
    for d in range(align, min(n, pref) + 1, align):
        if n % d == 0:
            best = d
    return best or n


def _cparams(sem=None, vmem=VMEM_LIMIT):
    kw = dict(vmem_limit_bytes=vmem)
    if sem is not None:
        kw["dimension_semantics"] = sem
    return pltpu.CompilerParams(**kw)


def _S(shape, dtype):
    return jax.ShapeDtypeStruct(shape, dtype)


ANY = pl.BlockSpec(memory_space=pl.ANY)
VMEM_SPEC = pl.BlockSpec(memory_space=pltpu.VMEM)


class _Task:
    def __init__(self, arrays, aliased, start, wait, fresh=(), nsem=3, mid=None):
        self.arrays, self.aliased, self.start, self.wait = arrays, aliased, start, wait
        self.fresh, self.nsem, self.mid = list(fresh), nsem, mid


def _call(name, grid, compute, in_specs, args, out_shape, out_specs, scratch, tasks=(), own_aliases=None):
    n_in, n_out, n_scr = len(args), len(out_shape), len(scratch)
    x_in, x_out, aliases, where = [], [], dict(own_aliases or {}), []
    for t in tasks:
        places = []
        for k, arr in enumerate(t.arrays):
            if k in t.aliased:
                aliases[n_in + len(x_in)] = n_out + len(x_out)
                places.append(("out", len(x_out)))
                x_out.append(_S(arr.shape, arr.dtype))
            else:
                places.append(("in", len(x_in)))
            x_in.append(arr)
        for shp in t.fresh:
            places.append(("out", len(x_out)))
            x_out.append(shp)
        where.append(places)
    n_xi, n_xo = len(x_in), len(x_out)

    def body(*refs):
        ins, xi = refs[:n_in], refs[n_in:n_in + n_xi]
        o0 = n_in + n_xi
        outs, xo = refs[o0:o0 + n_out], refs[o0 + n_out:o0 + n_out + n_xo]
        s0 = o0 + n_out + n_xo
        scr, sems = refs[s0:s0 + n_scr], refs[s0 + n_scr:]
        ids = [pl.program_id(a) for a in range(len(grid))]

        def task_refs(ti):
            return [xo[i] if kind == "out" else xi[i] for kind, i in where[ti]]

        if tasks:
            first = functools.reduce(jnp.logical_and, [i == 0 for i in ids])

            @pl.when(first)
            def _():
                for ti, t in enumerate(tasks):
                    t.start(task_refs(ti), *sems[3 * ti:3 * ti + 3])

        compute(*ins, *outs, *scr)
        if any(t.mid is not None for t in tasks):
            n_steps = math.prod(grid)
            step = functools.reduce(lambda s, ig: s * ig[1] + ig[0], zip(ids, grid), 0)

            @pl.when(step == (5 * n_steps) // 8)
            def _():
                for ti, t in enumerate(tasks):
                    if t.mid is not None:
                        t.mid(task_refs(ti), *sems[3 * ti:3 * ti + 3])

        if tasks:
            last = functools.reduce(jnp.logical_and, [i == g - 1 for i, g in zip(ids, grid)])

            @pl.when(last)
            def _():
                for ti, t in enumerate(tasks):
                    t.wait(task_refs(ti), *sems[3 * ti:3 * ti + 3])

    sem_shapes = []
    for t in tasks:
        sem_shapes += [pltpu.SemaphoreType.DMA((t.nsem,)), pltpu.SemaphoreType.DMA((t.nsem,)),
                       pltpu.SemaphoreType.DMA((1,))]
    res = pl.pallas_call(
        body, name=name, grid=grid,
        in_specs=list(in_specs) + [ANY] * n_xi,
        out_specs=tuple(out_specs) + (ANY,) * n_xo,
        out_shape=tuple(out_shape) + tuple(x_out),
        scratch_shapes=list(scratch) + sem_shapes,
        input_output_aliases=aliases,
        compiler_params=_cparams(("arbitrary",) * len(grid)),
    )(*args, *x_in)
    outs, passed, o = res[:n_out], [], n_out
    for places in where:
        k = sum(1 for kind, _ in places if kind == "out")
        passed.append(list(res[o:o + k]))
        o += k
    return outs, passed


def _mm_nn(a, w3, *, out_dtype, name, residual=None, tm=TM, tn=TN, tk=TK, tasks=()):
    M, K = a.shape
    G, _, n = w3.shape
    tm, tn, tk = _tile(M, tm, SUB), _tile(n, tn), _tile(K, tk)
    nj, nk = n // tn, K // tk

    def compute(*refs):
        if residual is None:
            a_ref, w_ref, o_ref = refs[:3]
            r_ref = None
        else:
            a_ref, w_ref, r_ref, o_ref = refs[:4]

        def finish(r):
            if r_ref is not None:
                r = r + r_ref[...]
            o_ref[...] = r.astype(o_ref.dtype)

        if nk == 1:
            finish(jnp.dot(a_ref[...], w_ref[...], preferred_element_type=F32))
            return
        acc = refs[-1]
        k = pl.program_id(3)

        @pl.when(k == 0)
        def _():
            acc[...] = jnp.zeros_like(acc)

        acc[...] += jnp.dot(a_ref[...], w_ref[...], preferred_element_type=F32)

        @pl.when(k == nk - 1)
        def _():
            finish(acc[...])

    in_specs = [pl.BlockSpec((tm, tk), lambda g, j, i, k: (i, k)),
                pl.BlockSpec((None, tk, tn), lambda g, j, i, k: (g, k, j))]
    args = [a, w3]
    if residual is not None:
        in_specs.append(pl.BlockSpec((tm, tn), lambda g, j, i, k: (i, g * nj + j)))
        args.append(residual)
    outs, passed = _call(
        name, (G, nj, M // tm, nk), compute, in_specs, args, [_S((M, G * n), out_dtype)],
        [pl.BlockSpec((tm, tn), lambda g, j, i, k: (i, g * nj + j))],
        [] if nk == 1 else [pltpu.VMEM((tm, tn), F32)], tasks)
    return (outs[0], passed) if tasks else outs[0]


def _mm_nt(dy, w3, *, out_dtype, name, tm=1024, tko=1024, tn=TN, tasks=()):
    M, _ = dy.shape
    G, K, n = w3.shape
    tm, tko, tn = _tile(M, tm, SUB), _tile(K, tko), _tile(n, tn)
    nj = n // tn
    nr = G * nj

    def compute(dy_ref, w_ref, o_ref, *scr):
        part = lax.dot_general(dy_ref[...], w_ref[...], (((1,), (1,)), ((), ())), preferred_element_type=F32)
        if nr == 1:
            o_ref[...] = part.astype(o_ref.dtype)
            return
        (acc,) = scr
        r = pl.program_id(2)

        @pl.when(r == 0)
        def _():
            acc[...] = jnp.zeros_like(acc)

        acc[...] += part

        @pl.when(r == nr - 1)
        def _():
            o_ref[...] = acc[...].astype(o_ref.dtype)

    outs, passed = _call(
        name, (K // tko, M // tm, nr), compute,
        [pl.BlockSpec((tm, tn), lambda ko, i, r: (i, r)),
         pl.BlockSpec((None, tko, tn), lambda ko, i, r: (r // nj, ko, r % nj))],
        [dy, w3], [_S((M, K), out_dtype)], [pl.BlockSpec((tm, tko), lambda ko, i, r: (i, ko))],
        [] if nr == 1 else [pltpu.VMEM((tm, tko), F32)], tasks)
    return (outs[0], passed) if tasks else outs[0]


def _mm_tn(a, dy, G, *, out_dtype, name, tk=1024, tn=TN, tt=1024, tasks=()):
    M, K = a.shape
    n = dy.shape[1] // G
    tk, tn, tt = _tile(K, tk), _tile(n, tn), _tile(M, tt, SUB)
    nj, nt = n // tn, M // tt

    def compute(a_ref, dy_ref, o_ref, acc):
        t = pl.program_id(3)

        @pl.when(t == 0)
        def _():
            acc[...] = jnp.zeros_like(acc)

        acc[...] += lax.dot_general(a_ref[...], dy_ref[...], (((0,), (0,)), ((), ())),
                                    preferred_element_type=F32)

        @pl.when(t == nt - 1)
        def _():
            o_ref[...] = acc[...].astype(o_ref.dtype)

    outs, passed = _call(
        name, (G, nj, K // tk, nt), compute,
        [pl.BlockSpec((tt, tk), lambda g, j, k, t: (t, k)),
         pl.BlockSpec((tt, tn), lambda g, j, k, t: (t, g * nj + j))],
        [a, dy], [_S((G, K, n), out_dtype)], [pl.BlockSpec((None, tk, tn), lambda g, j, k, t: (g, k, j))],
        [pltpu.VMEM((tk, tn), F32)], tasks)
    return (outs[0], passed) if tasks else outs[0]


def _mm_small(kind, a, b, w3, *, name, tm=1024, tasks=()):
    G, K, n = w3.shape
    M = (a if a is not None else b).shape[0]
    tm = _tile(M, tm, HALO)
    nt = M // tm
    w_spec = pl.BlockSpec((G, K, n), lambda i: (0, 0, 0))
    a_spec = pl.BlockSpec((tm, K), lambda i: (i, 0))
    b_spec = pl.BlockSpec((tm, G * n), lambda i: (i, 0))
    cols = lambda g: slice(g * n, (g + 1) * n)
    if kind == "nn":
        def compute(a_ref, w_ref, o_ref):
            av = a_ref[...]
            for g in range(G):
                o_ref[:, cols(g)] = jnp.dot(av, w_ref[g], preferred_element_type=F32).astype(o_ref.dtype)

        outs, passed = _call(name, (nt,), compute, [a_spec, w_spec], [a, w3], [_S((M, G * n), BF16)], [b_spec], [], tasks)
    elif kind == "nt":
        def compute(b_ref, w_ref, o_ref):
            acc = None
            for g in range(G):
                part = lax.dot_general(b_ref[:, cols(g)], w_ref[g], (((1,), (1,)), ((), ())),
                                       preferred_element_type=F32)
                acc = part if acc is None else acc + part
            o_ref[...] = acc.astype(o_ref.dtype)

        outs, passed = _call(name, (nt,), compute, [b_spec, w_spec], [b, w3], [_S((M, K), BF16)], [a_spec], [], tasks)
    else:
        def compute(a_ref, b_ref, o_ref, acc):
            i = pl.program_id(0)

            @pl.when(i == 0)
            def _():
                acc[...] = jnp.zeros_like(acc)

            at = a_ref[...].T
            for g in range(G):
                acc[g] += jnp.dot(at, b_ref[:, cols(g)], preferred_element_type=F32)

            @pl.when(i == nt - 1)
            def _():
                o_ref[...] = acc[...].astype(o_ref.dtype)

        outs, passed = _call(name, (nt,), compute, [a_spec, b_spec], [a, b], [_S((G, K, n), BF16)], [w_spec],
                             [pltpu.VMEM((G, K, n), F32)], tasks)
    return (outs[0], passed) if tasks else outs[0]


def _cast_into_slot(w, place, name, paired=False):
    R, C = w.shape
    tr = _tile(R, 512, HALO)

    def body(s_ref, w_ref, o_ref):
        del s_ref
        o_ref[...] = w_ref[...].astype(BF16)

    if paired:
        shape, out_map = (N_CHIP, R, 2 * C), lambda i, s: (s[1], i, s[0])
    else:
        shape, out_map = (N_DEV, R, C), lambda i, s: (s[2], i, 0)
    return pl.pallas_call(
        body, name=name, out_shape=_S(shape, BF16),
        grid_spec=pltpu.PrefetchScalarGridSpec(
            num_scalar_prefetch=1, grid=(R // tr,),
            in_specs=[pl.BlockSpec((tr, C), lambda i, s: (i, 0))],
            out_specs=pl.BlockSpec((None, tr, C), out_map)),
        compiler_params=_cparams(("parallel",)),
    )(place, w)


def _down(cur, prev8, j):
    return pltpu.roll(jnp.concatenate([prev8, cur], axis=0), j, 0)[SUB:, :]


def _up(cur, next8, j):
    n = cur.shape[0] + SUB
    return pltpu.roll(jnp.concatenate([cur, next8], axis=0), n - j, 0)[:cur.shape[0], :]


def _shifted_down(x, prev8, n):
    full = jnp.concatenate([prev8, x], axis=0)
    return [x] + [pltpu.roll(full, s, 0)[SUB:, :] for s in range(1, n)]


def _shifted_up(x, next8, n):
    m = x.shape[0] + SUB
    full = jnp.concatenate([x, next8], axis=0)
    return [x] + [pltpu.roll(full, m - s, 0)[:x.shape[0], :] for s in range(1, n)]


def _taps(sh, w_ref):
    kw = w_ref.shape[0]
    y = sh[0] * w_ref[pl.ds(kw - 1, 1), :]
    for k in range(kw - 1):
        y = y + sh[kw - 1 - k] * w_ref[pl.ds(k, 1), :]
    return y


def _conv(x, prev8, w_ref):
    return _taps(_shifted_down(x, prev8, w_ref.shape[0]), w_ref)


def _conv_t(dy, next8, w_ref):
    return _taps(_shifted_up(dy, next8, w_ref.shape[0]), w_ref)


def _conv_dw(dw_ref, dy, x, prev8, first):
    kw = dw_ref.shape[0]

    @pl.when(first)
    def _():
        dw_ref[...] = jnp.zeros_like(dw_ref)

    for k in range(kw):
        xs = x if k == kw - 1 else _down(x, prev8, kw - 1 - k)
        dw_ref[pl.ds(k, 1), :] += jnp.sum(dy * xs, axis=0, keepdims=True)


def _acc(ref, val, first):
    @pl.when(first)
    def _():
        ref[...] = jnp.zeros_like(ref)

    ref[...] += val


def _acc_row(ref, val, first):
    _acc(ref, jnp.sum(val, axis=0, keepdims=True), first)


def _prev8(h_ref, t):
    return jnp.where(t > 0, h_ref[...].astype(F32)[HALO - SUB:, :], 0.0)


def _next8(h_ref, is_last):
    return jnp.where(is_last, 0.0, h_ref[...].astype(F32)[:SUB, :])


_GELU_K0 = math.sqrt(2.0 / math.pi)
_GELU_K1 = 0.044715


def _gelu_and_grad(x):
    x2 = x * x
    th = jnp.tanh(_GELU_K0 * x * (1.0 + _GELU_K1 * x2))
    g = 0.5 * x * (1.0 + th)
    dg = 0.5 * (1.0 + th) + 0.5 * x * (1.0 - th * th) * (_GELU_K0 * (1.0 + 3.0 * _GELU_K1 * x2))
    return g, dg


def _neg_expm1(z):
    series = -z * (1.0 + z * (0.5 + z * (1.0 / 6.0 + z * (1.0 / 24.0))))
    return jnp.where(z > -0.03, series, 1.0 - jnp.exp(z))


def _store_staged(stages, dst_hbm, sems, step, n_steps, where):
    def copies(s, slot):
        return [pltpu.make_async_copy(
            st.at[slot], dst_hbm.at[pl.ds(r0, st.shape[1]), pl.ds(c0, st.shape[2])], sems.at[slot, k])
            for k, (st, (r0, c0)) in enumerate(zip(stages, where(s)))]

    slot = step % 2

    @pl.when(step > 0)
    def _():
        for cp in copies(step - 1, 1 - slot):
            cp.wait()

    for cp in copies(step, slot):
        cp.start()

    @pl.when(step == n_steps - 1)
    def _():
        for cp in copies(step, slot):
            cp.wait()


def _halo_prev_map(hb, col_fn):
    return lambda c, t: (jnp.maximum(t * hb - 1, 0), col_fn(c))


def _rms_fwd(x, g, name):
    T, D = x.shape
    tb = _tile(T, TB, SUB)

    def body(x_ref, g_ref, o_ref):
        xv = x_ref[...]
        rstd = lax.rsqrt(jnp.mean(xv * xv, axis=-1, keepdims=True) + EPS)
        o_ref[...] = (xv * rstd * g_ref[...]).astype(BF16)

    return pl.pallas_call(
        body, name=name, out_shape=_S((T, D), BF16), grid=(T // tb,),
        in_specs=[pl.BlockSpec((tb, D), lambda i: (i, 0)), pl.BlockSpec((1, D), lambda i: (0, 0))],
        out_specs=pl.BlockSpec((tb, D), lambda i: (i, 0)),
        compiler_params=_cparams(("parallel",)),
    )(x, g.reshape(1, D))


def _rms_bwd(x, g, dh, dres, name, want_bf16=True):
    T, D = x.shape
    tb = _tile(T, 256, SUB)

    def body(x_ref, g_ref, dh_ref, dr_ref, dx_ref, *rest):
        dg_ref = rest[-1]
        i = pl.program_id(0)
        xv = x_ref[...]
        rstd = lax.rsqrt(jnp.mean(xv * xv, axis=-1, keepdims=True) + EPS)
        xn = xv * rstd
        dhv = dh_ref[...].astype(F32)
        _acc_row(dg_ref, dhv * xn, i == 0)
        dxn = dhv * g_ref[...]
        dx = dr_ref[...] + rstd * (dxn - xn * jnp.mean(dxn * xn, axis=-1, keepdims=True))
        dx_ref[...] = dx
        if want_bf16:
            rest[0][...] = dx.astype(BF16)

    blk = pl.BlockSpec((tb, D), lambda i: (i, 0))
    vec = pl.BlockSpec((1, D), lambda i: (0, 0))
    extra = [(_S((T, D), BF16), blk)] if want_bf16 else []
    return pl.pallas_call(
        body, name=name, out_shape=(_S((T, D), F32), *[s for s, _ in extra], _S((1, D), F32)),
        grid=(T // tb,), in_specs=[blk, vec, blk, blk], out_specs=(blk, *[b for _, b in extra], vec),
        compiler_params=_cparams(("arbitrary",)),
    )(x, g.reshape(1, D), dh, dres)


def _loss_head(x2, g, target, name):
    T, D = x2.shape
    tb = _tile(T, 256, SUB)

    def body(x_ref, g_ref, t_ref, dx_ref, dxb_ref, loss_ref, dg_ref):
        i = pl.program_id(0)
        xv = x_ref[...]
        rstd = lax.rsqrt(jnp.mean(xv * xv, axis=-1, keepdims=True) + EPS)
        xn = xv * rstd
        err = xn * g_ref[...] - t_ref[...]
        part = 0.5 * jnp.sum(jnp.mean(err * err, axis=-1, keepdims=True), axis=0, keepdims=True)
        part = jnp.broadcast_to(part, (1, LANES))
        _acc(loss_ref, part, i == 0)
        dy = err * (1.0 / D)
        _acc_row(dg_ref, dy * xn, i == 0)
        dxn = dy * g_ref[...]
        dx = rstd * (dxn - xn * jnp.mean(dxn * xn, axis=-1, keepdims=True))
        dx_ref[...] = dx
        dxb_ref[...] = dx.astype(BF16)

    blk = pl.BlockSpec((tb, D), lambda i: (i, 0))
    vec = pl.BlockSpec((1, D), lambda i: (0, 0))
    return pl.pallas_call(
        body, name=name,
        out_shape=(_S((T, D), F32), _S((T, D), BF16), _S((1, LANES), F32), _S((1, D), F32)),
        grid=(T // tb,), in_specs=[blk, vec, blk],
        out_specs=(blk, blk, pl.BlockSpec((1, LANES), lambda i: (0, 0)), vec),
        compiler_params=_cparams(("arbitrary",)),
    )(x2, g.reshape(1, D), target)


def _lru_gates(xc, wa_ref, ba_ref, wx_ref, bx_ref, lam_ref):
    xcb = xc.astype(BF16)
    r = jax.nn.sigmoid(jnp.dot(xcb, wa_ref[...], preferred_element_type=F32) + ba_ref[...])
    i = jax.nn.sigmoid(jnp.dot(xcb, wx_ref[...], preferred_element_type=F32) + bx_ref[...])
    sp = jax.nn.softplus(-lam_ref[...])
    log_a = (-LRU_C * sp) * r
    a = jnp.exp(log_a)
    s = jnp.sqrt(_neg_expm1(2.0 * log_a))
    return xcb, r, i, a, s


def _lru_fwd(p, conv_w, conv_b, wa_bd, ba, wx_bd, bx, lam, *, name, tasks=()):
    T = p.shape[0]
    d = lam.shape[-1]
    C = _tile(d, C_LRU)
    nC = d // C
    tb = _tile(T, TB, HALO)
    nT, hb, nt = T // tb, tb // HALO, tb // SUB

    def body(x_ref, xh_ref, g_ref, cw_ref, cb_ref, wa_ref, ba_ref, wx_ref, bx_ref, lam_ref,
             hs_ref, y_ref, a_s, u_s, h_s):
        t = pl.program_id(1)

        @pl.when(t == 0)
        def _():
            h_s[...] = jnp.zeros_like(h_s)

        x = x_ref[...].astype(F32)
        xc = _conv(x, _prev8(xh_ref, t), cw_ref) + cb_ref[...]
        _, r, i, a, s = _lru_gates(xc, wa_ref, ba_ref, wx_ref, bx_ref, lam_ref)
        a_s[...] = a
        u_s[...] = s * (i * xc)
        row = lax.broadcasted_iota(jnp.int32, (SUB, C), 0)

        def step(k, h):
            o = pl.multiple_of(k * SUB, SUB)
            A = a_s[pl.ds(o, SUB), :]
            B = u_s[pl.ds(o, SUB), :]
            for sh in (1, 2, 4):
                m = row >= sh
                Ap = pltpu.roll(A, sh, 0)
                Bp = pltpu.roll(B, sh, 0)
                B = jnp.where(m, A * Bp + B, B)
                A = jnp.where(m, A * Ap, A)
            hs = A * h + B
            hs_ref[pl.ds(o, SUB), :] = hs
            return jnp.broadcast_to(hs[SUB - 1:SUB, :], (SUB, C))

        h_s[...] = lax.fori_loop(0, nt, step, h_s[...])
        gel, _ = _gelu_and_grad(g_ref[...].astype(F32))
        y_ref[...] = (gel * hs_ref[...]).astype(BF16)

    vec = pl.BlockSpec((1, C), lambda c, t: (0, c))
    sq = pl.BlockSpec((None, C, C), lambda c, t: (c, 0, 0))
    outs, passed = _call(
        name, (nC, nT), body,
        [pl.BlockSpec((tb, C), lambda c, t: (t, c)),
         pl.BlockSpec((HALO, C), _halo_prev_map(hb, lambda c: c)),
         pl.BlockSpec((tb, C), lambda c, t: (t, nC + c)),
         pl.BlockSpec((conv_w.shape[0], C), lambda c, t: (0, c)),
         vec, sq, vec, sq, vec, vec],
        [p, p, p, conv_w, conv_b, wa_bd, ba, wx_bd, bx, lam],
        [_S((T, d), F32), _S((T, d), BF16)],
        [pl.BlockSpec((tb, C), lambda c, t: (t, c)), pl.BlockSpec((tb, C), lambda c, t: (t, c))],
        [pltpu.VMEM((tb, C), F32), pltpu.VMEM((tb, C), F32), pltpu.VMEM((SUB, C), F32)], tasks)
    return (*outs, passed) if tasks else outs


def _lru_bwd(p, hs, dyl, dp, conv_w, conv_b, wa_bd, ba, wx_bd, bx, lam, *, name, tasks=()):
    T = p.shape[0]
    d = lam.shape[-1]
    C = _tile(d, C_LRU)
    nC = d // C
    tb = _tile(T, TB, HALO)
    nT, hb, nt = T // tb, tb // HALO, tb // SUB
    kw = conv_w.shape[0]

    def body(x_ref, xh_ref, g_ref, hs_ref, hh_ref, dy_ref, cw_ref, cb_ref, wa_ref, ba_ref, wx_ref, bx_ref,
             lam_ref, dp_in, dp_ref, dcw_ref, dcb_ref, dwa_ref, dba_ref, dwx_ref, dbx_ref, dlam_ref,
             b_s, g_s, dh_s, an_s, dhn_s, dxn_s, st_x, st_g, sems):
        del dp_in
        c = pl.program_id(0)
        tr = pl.program_id(1)
        t = nT - 1 - tr
        first = tr == 0

        @pl.when(first)
        def _():
            an_s[...] = jnp.zeros_like(an_s)
            dhn_s[...] = jnp.zeros_like(dhn_s)
            dxn_s[...] = jnp.zeros_like(dxn_s)

        x = x_ref[...].astype(F32)
        xprev = _prev8(xh_ref, t)
        xc = _conv(x, xprev, cw_ref) + cb_ref[...]
        xcb, r, i, a, s = _lru_gates(xc, wa_ref, ba_ref, wx_ref, bx_ref, lam_ref)
        hsv = hs_ref[...]
        dy = dy_ref[...].astype(F32)
        gel, dgel = _gelu_and_grad(g_ref[...].astype(F32))
        step_no = c * nT + tr
        slot = step_no % 2
        st_g[slot] = (dy * hsv * dgel).astype(BF16)

        b_s[...] = _up(a, an_s[...], 1)
        g_s[...] = dy * gel
        row = lax.broadcasted_iota(jnp.int32, (SUB, C), 0)

        def step(k, carry):
            o = pl.multiple_of((nt - 1 - k) * SUB, SUB)
            B = b_s[pl.ds(o, SUB), :]
            G = g_s[pl.ds(o, SUB), :]
            for sh in (1, 2, 4):
                m = row < SUB - sh
                Bn = pltpu.roll(B, SUB - sh, 0)
                Gn = pltpu.roll(G, SUB - sh, 0)
                G = jnp.where(m, B * Gn + G, G)
                B = jnp.where(m, B * Bn, B)
            dh = B * carry + G
            dh_s[pl.ds(o, SUB), :] = dh
            return jnp.broadcast_to(dh[0:1, :], (SUB, C))

        dhn_s[...] = lax.fori_loop(0, nt, step, dhn_s[...])
        an_s[...] = a[:SUB, :]
        dh = dh_s[...]

        hprev = _down(hsv, jnp.where(t > 0, hh_ref[...][HALO - SUB:, :], 0.0), 1)
        d_a = dh * hprev
        ixc = i * xc
        d_s = dh * ixc
        d_i = dh * s * xc
        d_xc = dh * s * i
        d_l = d_a * a - d_s * (a * a) / s
        sp = jax.nn.softplus(-lam_ref[...])
        _acc_row(dlam_ref, d_l * r * (LRU_C * jax.nn.sigmoid(-lam_ref[...])), first)
        d_zr = (d_l * (-LRU_C * sp)) * r * (1.0 - r)
        d_zi = d_i * i * (1.0 - i)
        _acc_row(dba_ref, d_zr, first)
        _acc_row(dbx_ref, d_zi, first)
        d_zrb = d_zr.astype(BF16)
        d_zib = d_zi.astype(BF16)
        tn_dims = (((0,), (0,)), ((), ()))
        nt_dims = (((1,), (1,)), ((), ()))
        gwa = lax.dot_general(xcb, d_zrb, tn_dims, preferred_element_type=F32)
        gwx = lax.dot_general(xcb, d_zib, tn_dims, preferred_element_type=F32)
        _acc(dwa_ref, gwa, first)
        _acc(dwx_ref, gwx, first)
        d_xc = (d_xc + lax.dot_general(d_zrb, wa_ref[...], nt_dims, preferred_element_type=F32)
                + lax.dot_general(d_zib, wx_ref[...], nt_dims, preferred_element_type=F32))
        _acc_row(dcb_ref, d_xc, first)
        _conv_dw(dcw_ref, d_xc, x, xprev, first)
        st_x[slot] = _conv_t(d_xc, dxn_s[...], cw_ref).astype(BF16)
        dxn_s[...] = d_xc[:SUB, :]

        def where(s):
            row0, col0 = (nT - 1 - s % nT) * tb, (s // nT) * C
            return [(row0, col0), (row0, d + col0)]

        _store_staged([st_x, st_g], dp_ref, sems, step_no, nC * nT, where)

    rev = lambda c, tr: (nT - 1 - tr, c)
    vec = pl.BlockSpec((1, C), lambda c, tr: (0, c))
    sq = pl.BlockSpec((None, C, C), lambda c, tr: (c, 0, 0))
    cwb = pl.BlockSpec((kw, C), lambda c, tr: (0, c))
    halo_prev = lambda c, tr: (jnp.maximum((nT - 1 - tr) * hb - 1, 0), c)
    outs, passed = _call(
        name, (nC, nT), body,
        [pl.BlockSpec((tb, C), rev),
         pl.BlockSpec((HALO, C), halo_prev),
         pl.BlockSpec((tb, C), lambda c, tr: (nT - 1 - tr, nC + c)),
         pl.BlockSpec((tb, C), rev),
         pl.BlockSpec((HALO, C), halo_prev),
         pl.BlockSpec((tb, C), rev),
         cwb, vec, sq, vec, sq, vec, vec, ANY],
        [p, p, p, hs, hs, dyl, conv_w, conv_b, wa_bd, ba, wx_bd, bx, lam, dp],
        [_S(dp.shape, dp.dtype), _S((kw, d), F32), _S((1, d), F32), _S((nC, C, C), F32), _S((1, d), F32),
         _S((nC, C, C), F32), _S((1, d), F32), _S((1, d), F32)],
        [ANY, cwb, vec, sq, vec, sq, vec, vec],
        [pltpu.VMEM((tb, C), F32), pltpu.VMEM((tb, C), F32), pltpu.VMEM((tb, C), F32),
         pltpu.VMEM((SUB, C), F32), pltpu.VMEM((SUB, C), F32), pltpu.VMEM((SUB, C), F32),
         pltpu.VMEM((2, tb, C), BF16), pltpu.VMEM((2, tb, C), BF16), pltpu.SemaphoreType.DMA((2, 2))],
        tasks, own_aliases={13: 0})
    return (*outs, passed) if tasks else outs


def _sc_fwd(p, conv_w, *, d, name):
    T = p.shape[0]
    C = _tile(d, C_EW)
    nC = d // C
    tb = _tile(T, TB, HALO)
    nT, hb = T // tb, tb // HALO

    def body(b_ref, c_ref, ch_ref, v_ref, vh_ref, w_ref, y_ref):
        t = pl.program_id(1)
        cv = c_ref[...].astype(F32) * v_ref[...].astype(F32)
        cvp = _prev8(ch_ref, t) * _prev8(vh_ref, t)
        y_ref[...] = (b_ref[...].astype(F32) * _conv(cv, cvp, w_ref)).astype(BF16)

    seg = lambda k: pl.BlockSpec((tb, C), lambda c, t: (t, k * nC + c))
    hseg = lambda k: pl.BlockSpec((HALO, C), _halo_prev_map(hb, lambda c: k * nC + c))
    return pl.pallas_call(
        body, name=name, out_shape=_S((T, d), BF16), grid=(nC, nT),
        in_specs=[seg(2), seg(3), hseg(3), seg(4), hseg(4), pl.BlockSpec((conv_w.shape[0], C), lambda c, t: (0, c))],
        out_specs=pl.BlockSpec((tb, C), lambda c, t: (t, c)),
        compiler_params=_cparams(("parallel", "parallel")),
    )(p, p, p, p, p, conv_w)


def _sc_bwd(p, dys, dp, conv_w, *, d, name, tasks=()):
    T = p.shape[0]
    C = _tile(d, C_EW)
    nC = d // C
    tb = _tile(T, TB, HALO)
    nT, hb = T // tb, tb // HALO
    kw = conv_w.shape[0]

    def body(b_ref, bn_ref, c_ref, ch_ref, v_ref, vh_ref, dy_ref, dyn_ref, w_ref, dp_in, dp_ref, dw_ref,
             st_b, st_c, st_v, sems):
        del dp_in
        c = pl.program_id(0)
        t = pl.program_id(1)
        last = t == nT - 1
        bv = b_ref[...].astype(F32)
        cvv = c_ref[...].astype(F32)
        vv = v_ref[...].astype(F32)
        dy = dy_ref[...].astype(F32)
        cv = cvv * vv
        cvp = _prev8(ch_ref, t) * _prev8(vh_ref, t)
        step_no = c * nT + t
        slot = step_no % 2
        st_b[slot] = (dy * _conv(cv, cvp, w_ref)).astype(BF16)
        dz = dy * bv
        dzn = _next8(dyn_ref, last) * _next8(bn_ref, last)
        _conv_dw(dw_ref, dz, cv, cvp, t == 0)
        dcv = _conv_t(dz, dzn, w_ref)
        st_c[slot] = (dcv * vv).astype(BF16)
        st_v[slot] = (dcv * cvv).astype(BF16)

        def where(s):
            return [((s % nT) * tb, (2 + k) * d + (s // nT) * C) for k in range(3)]

        _store_staged([st_b, st_c, st_v], dp_ref, sems, step_no, nC * nT, where)

    seg = lambda k: pl.BlockSpec((tb, C), lambda c, t: (t, k * nC + c))
    hseg = lambda k: pl.BlockSpec((HALO, C), _halo_prev_map(hb, lambda c: k * nC + c))
    last_h = T // HALO - 1
    nseg = lambda k: pl.BlockSpec((HALO, C), lambda c, t: (jnp.minimum((t + 1) * hb, last_h), k * nC + c))
    outs, passed = _call(
        name, (nC, nT), body,
        [seg(2), nseg(2), seg(3), hseg(3), seg(4), hseg(4),
         pl.BlockSpec((tb, C), lambda c, t: (t, c)), nseg(0),
         pl.BlockSpec((kw, C), lambda c, t: (0, c)), ANY],
        [p, p, p, p, p, p, dys, dys, conv_w, dp],
        [_S(dp.shape, dp.dtype), _S((kw, d), F32)], [ANY, pl.BlockSpec((kw, C), lambda c, t: (0, c))],
        [pltpu.VMEM((2, tb, C), BF16)] * 3 + [pltpu.SemaphoreType.DMA((2, 3))], tasks, own_aliases={9: 0})
    return (*outs, passed) if tasks else outs


def _merge_fwd(p, y_lru, y_sc, *, col0, name, tasks=()):
    T, D = y_lru.shape
    C = _tile(math.gcd(D, col0), 1024)
    nC = D // C
    k0 = col0 // C
    tb = _tile(T, 256, HALO)

    def body(gl_ref, gs_ref, yl_ref, ys_ref, o_ref):
        @pl.loop(0, tb // HALO)
        def _(k):
            rows = pl.ds(pl.multiple_of(k * HALO, HALO), HALO)
            for l0 in range(0, C, min(C, C_EW)):
                at = (rows, pl.ds(l0, min(C, C_EW)))
                o_ref[at] = (jax.nn.sigmoid(gl_ref[at].astype(F32)) * yl_ref[at].astype(F32)
                             + jax.nn.sigmoid(gs_ref[at].astype(F32)) * ys_ref[at].astype(F32)).astype(BF16)

    blk = pl.BlockSpec((tb, C), lambda c, t: (t, c))
    outs, passed = _call(
        name, (nC, T // tb), body,
        [pl.BlockSpec((tb, C), lambda c, t: (t, k0 + c)), pl.BlockSpec((tb, C), lambda c, t: (t, k0 + nC + c)),
         blk, blk], [p, p, y_lru, y_sc], [_S((T, D), BF16)], [blk], [], tasks)
    return (outs[0], passed) if tasks else outs[0]


def _merge_bwd(p, y_lru, y_sc, dm, *, col0, name, tasks=()):
    T, D = y_lru.shape
    C = _tile(math.gcd(D, col0), 1024)
    nC = D // C
    k0 = col0 // C
    tb = _tile(T, 256, HALO)
    nT = T // tb

    def body(gl_ref, gs_ref, yl_ref, ys_ref, dm_ref, dp_ref, dyl_ref, dys_ref, st_l, st_s, sems):
        step_no = pl.program_id(0) * nT + pl.program_id(1)
        slot = step_no % 2

        @pl.loop(0, tb // HALO)
        def _(k):
            rows = pl.ds(pl.multiple_of(k * HALO, HALO), HALO)
            for l0 in range(0, C, min(C, C_EW)):
                at = (rows, pl.ds(l0, min(C, C_EW)))
                dmv = dm_ref[at].astype(F32)
                sl = jax.nn.sigmoid(gl_ref[at].astype(F32))
                ss = jax.nn.sigmoid(gs_ref[at].astype(F32))
                dyl_ref[at] = (dmv * sl).astype(BF16)
                dys_ref[at] = (dmv * ss).astype(BF16)
                st_l[(slot,) + at] = (dmv * yl_ref[at].astype(F32) * sl * (1.0 - sl)).astype(BF16)
                st_s[(slot,) + at] = (dmv * ys_ref[at].astype(F32) * ss * (1.0 - ss)).astype(BF16)

        def where(s):
            row0, colc = (s % nT) * tb, (s // nT) * C
            return [(row0, col0 + colc), (row0, col0 + D + colc)]

        _store_staged([st_l, st_s], dp_ref, sems, step_no, nC * nT, where)

    blk = pl.BlockSpec((tb, C), lambda c, t: (t, c))
    outs, passed = _call(
        name, (nC, nT), body,
        [pl.BlockSpec((tb, C), lambda c, t: (t, k0 + c)), pl.BlockSpec((tb, C), lambda c, t: (t, k0 + nC + c)),
         blk, blk, blk], [p, p, y_lru, y_sc, dm],
        [_S(p.shape, BF16), _S((T, D), BF16), _S((T, D), BF16)], [ANY, blk, blk],
        [pltpu.VMEM((2, tb, C), BF16), pltpu.VMEM((2, tb, C), BF16), pltpu.SemaphoreType.DMA((2, 2))], tasks)
    return (*outs, passed) if tasks else outs


def _ffn_act_fwd(uu, conv_w, *, name, tasks=()):
    T = uu.shape[0]
    F = uu.shape[1] // 2
    C = _tile(F, C_EW)
    nC = F // C
    tb = _tile(T, TB, HALO)
    nT, hb = T // tb, tb // HALO
    kw = conv_w.shape[0]
    R = HALO

    def body(g_ref, gh_ref, v_ref, vh_ref, wg_ref, wv_ref, o_ref):
        t = pl.program_id(1)

        def chunk(k, carry):
            gp, vp = carry
            r0 = pl.multiple_of(k * R, R)
            ug = g_ref[pl.ds(r0, R), :].astype(F32)
            uv = v_ref[pl.ds(r0, R), :].astype(F32)
            cg = _conv(ug, gp, wg_ref)
            cv = _conv(uv, vp, wv_ref)
            o_ref[pl.ds(r0, R), :] = (cg * jax.nn.sigmoid(cg) * cv).astype(BF16)
            return ug[R - SUB:, :], uv[R - SUB:, :]

        lax.fori_loop(0, tb // R, chunk, (_prev8(gh_ref, t), _prev8(vh_ref, t)))

    seg = lambda k: pl.BlockSpec((tb, C), lambda c, t: (t, k * nC + c))
    hseg = lambda k: pl.BlockSpec((HALO, C), _halo_prev_map(hb, lambda c: k * nC + c))
    wseg = lambda k: pl.BlockSpec((kw, C), lambda c, t: (0, k * nC + c))
    outs, passed = _call(
        name, (nC, nT), body, [seg(0), hseg(0), seg(1), hseg(1), wseg(0), wseg(1)],
        [uu, uu, uu, uu, conv_w, conv_w], [_S((T, F), BF16)], [pl.BlockSpec((tb, C), lambda c, t: (t, c))], [], tasks)
    return (outs[0], passed) if tasks else outs[0]


def _ffn_act_bwd(uu, dact, conv_w, *, name):
    T = uu.shape[0]
    F = uu.shape[1] // 2
    C = _tile(F, C_EW)
    nC = F // C
    tb = _tile(T, TB, HALO)
    nT, hb = T // tb, tb // HALO
    kw = conv_w.shape[0]
    R = HALO
    nk = tb // R

    def body(g_ref, gh_ref, v_ref, vh_ref, da_ref, wg_ref, wv_ref, du_ref, dwg_ref, dwv_ref,
             gn_s, vn_s, accg_s, accv_s, st_g, st_v, sems):
        c = pl.program_id(0)
        tr = pl.program_id(1)
        t = nT - 1 - tr
        first = tr == 0

        @pl.when(first)
        def _():
            gn_s[...] = jnp.zeros_like(gn_s)
            vn_s[...] = jnp.zeros_like(vn_s)
            dwg_ref[...] = jnp.zeros_like(dwg_ref)
            dwv_ref[...] = jnp.zeros_like(dwv_ref)

        accg_s[...] = jnp.zeros_like(accg_s)
        accv_s[...] = jnp.zeros_like(accv_s)
        step_no = c * nT + tr
        slot = step_no % 2

        def chunk(i, carry):
            gn, vn = carry
            k = nk - 1 - i
            r0 = pl.multiple_of(k * R, R)
            rp = pl.multiple_of(jnp.maximum(r0 - R, 0), R)
            ug = g_ref[pl.ds(r0, R), :].astype(F32)
            uv = v_ref[pl.ds(r0, R), :].astype(F32)
            gp = jnp.where(k > 0, g_ref[pl.ds(rp, R), :].astype(F32)[R - SUB:, :], _prev8(gh_ref, t))
            vp = jnp.where(k > 0, v_ref[pl.ds(rp, R), :].astype(F32)[R - SUB:, :], _prev8(vh_ref, t))
            sh_g = _shifted_down(ug, gp, kw)
            sh_v = _shifted_down(uv, vp, kw)
            cg = _taps(sh_g, wg_ref)
            cv = _taps(sh_v, wv_ref)
            da = da_ref[pl.ds(r0, R), :].astype(F32)
            sg = jax.nn.sigmoid(cg)
            d_cg = da * cv * (sg * (1.0 + cg * (1.0 - sg)))
            d_cv = da * (cg * sg)
            for j in range(kw):
                accg_s[j] += d_cg * sh_g[kw - 1 - j]
                accv_s[j] += d_cv * sh_v[kw - 1 - j]
            st_g[slot, pl.ds(r0, R), :] = _conv_t(d_cg, gn, wg_ref).astype(BF16)
            st_v[slot, pl.ds(r0, R), :] = _conv_t(d_cv, vn, wv_ref).astype(BF16)
            return d_cg[:SUB, :], d_cv[:SUB, :]

        gn, vn = lax.fori_loop(0, nk, chunk, (gn_s[...], vn_s[...]))
        gn_s[...] = gn
        vn_s[...] = vn
        for j in range(kw):
            dwg_ref[pl.ds(j, 1), :] += jnp.sum(accg_s[j], axis=0, keepdims=True)
            dwv_ref[pl.ds(j, 1), :] += jnp.sum(accv_s[j], axis=0, keepdims=True)
        def where(s):
            row0, col0 = (nT - 1 - s % nT) * tb, (s // nT) * C
            return [(row0, col0), (row0, F + col0)]

        _store_staged([st_g, st_v], du_ref, sems, step_no, nC * nT, where)

    seg = lambda k: pl.BlockSpec((tb, C), lambda c, tr: (nT - 1 - tr, k * nC + c))
    hseg = lambda k: pl.BlockSpec((HALO, C), lambda c, tr: (jnp.maximum((nT - 1 - tr) * hb - 1, 0), k * nC + c))
    wseg = lambda k: pl.BlockSpec((kw, C), lambda c, tr: (0, k * nC + c))
    dwb = pl.BlockSpec((kw, C), lambda c, tr: (0, c))
    return pl.pallas_call(
        body, name=name, out_shape=(_S(uu.shape, BF16), _S((kw, F), F32), _S((kw, F), F32)), grid=(nC, nT),
        in_specs=[seg(0), hseg(0), seg(1), hseg(1), pl.BlockSpec((tb, C), lambda c, tr: (nT - 1 - tr, c)),
                  wseg(0), wseg(1)],
        out_specs=(ANY, dwb, dwb),
        scratch_shapes=[pltpu.VMEM((SUB, C), F32), pltpu.VMEM((SUB, C), F32),
                        pltpu.VMEM((kw, R, C), F32), pltpu.VMEM((kw, R, C), F32),
                        pltpu.VMEM((2, tb, C), BF16), pltpu.VMEM((2, tb, C), BF16), pltpu.SemaphoreType.DMA((2, 2))],
        compiler_params=_cparams(("arbitrary", "arbitrary")),
    )(uu, uu, uu, uu, dact, conv_w, conv_w)


def _place():
    x, y, c = lax.axis_index("x"), lax.axis_index("y"), lax.axis_index("c")
    return x, y, c


def _chips(x, y):
    return [(1 - x, y), (x, 1 - y), (1 - x, 1 - y)]


def _all_gather(arrays, placed, over_ici, pair_n, name):
    n = len(arrays)

    def body(*refs):
        ins, outs = refs[:n], refs[n:2 * n]
        send_sems, recv_sems, local_sems = refs[2 * n:]
        x, y, c = _place()
        me, sibling = (x, y, c), (x, y, 1 - c)
        chips = _chips(x, y)
        full = [a for a in range(n) if over_ici[a]]

        def idx(px, py, pc):
            return 4 * px + 2 * py + pc

        def copy(a, k, block, to):
            dst = _dev_block(outs[a], idx(*block), pair_n[a])
            src = ins[a] if (block is me and not placed[a]) else dst
            return pltpu.make_async_remote_copy(
                src_ref=src, dst_ref=dst, send_sem=send_sems.at[a, k], recv_sem=recv_sems.at[a, k],
                device_id=to, device_id_type=MESH)

        def half(a, k, block, to, lo):
            r = rows_of[a] // 2
            blk = _rows_of(outs[a], idx(*block), (0 if lo else r, r), pair_n[a])
            return pltpu.make_async_remote_copy(
                src_ref=blk, dst_ref=blk, send_sem=send_sems.at[a, k], recv_sem=recv_sems.at[a, k],
                device_id=to, device_id_type=MESH)

        mine = [pltpu.make_async_copy(ins[a], outs[a].at[idx(*me)], local_sems.at[a])
                for a in range(n) if not placed[a]]
        for cp in mine:
            cp.start()
        chip_x, chip_y, chip_d = chips
        sent = []
        for a in full:
            sent += [copy(a, 1, me, (*chip_x, c)), copy(a, 2, me, (*chip_y, c))]
            if not relay[a]:
                sent.append(copy(a, 3, me, (*chip_d, c)))
        for a in range(n):
            sent.append(copy(a, 0, me, sibling))
        for cp in sent:
            cp.start()

        def then(cp):
            cp.start()
            sent.append(cp)

        for a in full:
            copy(a, 2, (*chip_y, c), me).wait_recv()
            if relay[a]:
                then(half(a, 3, (*chip_y, c), (*chip_x, c), True))
            then(copy(a, 6, (*chip_y, c), sibling))
            copy(a, 1, (*chip_x, c), me).wait_recv()
            if relay[a]:
                then(half(a, 4, (*chip_x, c), (*chip_y, c), False))
            then(copy(a, 5, (*chip_x, c), sibling))
        for a in full:
            if relay[a]:
                half(a, 3, (*chip_d, c), me, True).wait_recv()
                half(a, 4, (*chip_d, c), me, False).wait_recv()
            else:
                copy(a, 3, (*chip_d, c), me).wait_recv()
            then(copy(a, 7, (*chip_d, c), sibling))
        for a in range(n):
            copy(a, 0, sibling, me).wait_recv()
        for a in full:
            for j, chip in enumerate(chips):
                copy(a, 5 + j, (*chip, 1 - c), me).wait_recv()
        for cp in sent:
            cp.wait_send()
        for cp in mine:
            cp.wait()

    rows_of = [(s.shape[1] if placed[a] else s.shape[0]) for a, s in enumerate(arrays)]
    relay = [r % (2 * HALO) == 0 for r in rows_of]
    return pl.pallas_call(
        body, name=name,
        out_shape=tuple(_S(s.shape if placed[a] else (N_DEV,) + s.shape, s.dtype) for a, s in enumerate(arrays)),
        in_specs=[ANY] * n, out_specs=tuple([ANY] * n),
        scratch_shapes=[pltpu.SemaphoreType.DMA((n, 8)), pltpu.SemaphoreType.DMA((n, 8)),
                        pltpu.SemaphoreType.DMA((n,))],
        input_output_aliases={a: a for a in range(n) if placed[a]},
    )(*arrays)


def _dev_block(ref, dev, pair_n=None):
    if pair_n is None:
        return ref.at[dev]
    return ref.at[dev // 2, :, pl.ds(pl.multiple_of((dev % 2) * pair_n, LANES), pair_n)]


def _rows_of(ref, blk, rows, pair_n=None):
    v = _dev_block(ref, blk, pair_n)
    return v if rows is None else v.at[pl.ds(rows[0], rows[1])]


ALL_ROWS = "all"


def _gather_task(buf, ici=None, fwd=None, pair_n=None):
    blk_of = functools.partial(_rows_of, pair_n=pair_n)
    r0, nr = (0, buf.shape[1]) if ici == ALL_ROWS else (ici or (0, 0))
    assert nr % (2 * HALO) == 0
    lo, hi = (r0, nr // 2), (r0 + nr // 2, nr // 2)
    both = (r0, nr)
    fwd_rows = None if fwd == ALL_ROWS else fwd

    def remote(refs, ss, rs, k, dev, rows, to):
        blk = blk_of(refs[0], dev, rows)
        return pltpu.make_async_remote_copy(src_ref=blk, dst_ref=blk, send_sem=ss.at[k], recv_sem=rs.at[k],
                                            device_id=to, device_id_type=MESH)

    def waves(refs, ss, rs):
        x, y, c = _place()
        me = 4 * x + 2 * y + c
        (xx, xy), (yx, yy), _ = _chips(x, y)
        dev_x, dev_y = 4 * xx + 2 * xy + c, 4 * yx + 2 * yy + c
        first, second = [], []
        if ici is not None:
            first += [remote(refs, ss, rs, 0, me, both, (xx, xy, c)), remote(refs, ss, rs, 1, me, both, (yx, yy, c))]
            second += [remote(refs, ss, rs, 2, dev_y, lo, (xx, xy, c)), remote(refs, ss, rs, 3, dev_x, hi, (yx, yy, c))]
        if fwd is not None:
            first += [remote(refs, ss, rs, 4 + j, 4 * px + 2 * py + c, fwd_rows, (x, y, 1 - c))
                      for j, (px, py) in enumerate(_chips(x, y))]
        return first, second

    def start(refs, ss, rs, ls):
        for cp in waves(refs, ss, rs)[0]:
            cp.start()

    def mid(refs, ss, rs, ls):
        x, y, c = _place()
        (xx, xy), (yx, yy), _ = _chips(x, y)
        remote(refs, ss, rs, 0, 4 * xx + 2 * xy + c, both, (x, y, c)).wait_recv()
        remote(refs, ss, rs, 1, 4 * yx + 2 * yy + c, both, (x, y, c)).wait_recv()
        for cp in waves(refs, ss, rs)[1]:
            cp.start()

    def wait(refs, ss, rs, ls):
        x, y, c = _place()
        chips = _chips(x, y)
        if ici is not None:
            dev_d = 4 * chips[2][0] + 2 * chips[2][1] + c
            remote(refs, ss, rs, 2, dev_d, lo, (x, y, c)).wait_recv()
            remote(refs, ss, rs, 3, dev_d, hi, (x, y, c)).wait_recv()
        if fwd is not None:
            for j, (px, py) in enumerate(chips):
                remote(refs, ss, rs, 4 + j, 4 * px + 2 * py + 1 - c, fwd_rows, (x, y, c)).wait_recv()
        first, second = waves(refs, ss, rs)
        for cp in first + second:
            cp.wait_send()

    return _Task([buf], [0], start, wait, nsem=7, mid=mid if ici is not None else None)


def _exchange_task(parts, landing, rows=None):
    def copies(refs, ss, rs):
        x, y, c = _place()
        myq = 2 * x + y
        return [pltpu.make_async_remote_copy(
            src_ref=_rows_of(refs[0], 2 * px + py, rows), dst_ref=_rows_of(refs[1], myq, rows),
            send_sem=ss.at[k], recv_sem=rs.at[k], device_id=(px, py, c), device_id_type=MESH)
            for k, (px, py) in enumerate(_chips(x, y))]

    def start(refs, ss, rs, ls):
        for cp in copies(refs, ss, rs):
            cp.start()

    def wait(refs, ss, rs, ls):
        x, y, c = _place()
        for k, (px, py) in enumerate(_chips(x, y)):
            pltpu.make_async_remote_copy(
                src_ref=_rows_of(refs[0], 2 * x + y, rows), dst_ref=_rows_of(refs[1], 2 * px + py, rows),
                send_sem=ss.at[k], recv_sem=rs.at[k], device_id=(px, py, c), device_id_type=MESH).wait_recv()
        for cp in copies(refs, ss, rs):
            cp.wait_send()

    return _Task([parts, landing], [1], start, wait)


def _core_blocks(g, pair_n):
    if pair_n is None:
        g4 = g.reshape((N_CHIP, 2) + g.shape[1:])
        return g4, (N_CHIP,) + g.shape[1:], lambda ref, c: ref.at[:, c]
    view = lambda ref, c: ref.at[:, :, pl.ds(pl.multiple_of(c * pair_n, LANES), pair_n)]
    return g, (N_CHIP, g.shape[1], pair_n), view


def _swap_task(g, pair_n=None):
    g4, shape, view = _core_blocks(g, pair_n)

    def copy(refs, ss, rs):
        x, y, c = _place()
        return pltpu.make_async_remote_copy(
            src_ref=view(refs[0], 1 - c), dst_ref=refs[1], send_sem=ss.at[0], recv_sem=rs.at[0],
            device_id=(x, y, 1 - c), device_id_type=MESH)

    def start(refs, ss, rs, ls):
        copy(refs, ss, rs).start()

    def wait(refs, ss, rs, ls):
        copy(refs, ss, rs).wait()

    return _Task([g4], [], start, wait, fresh=[_S(shape, g.dtype)], nsem=1)


def _peer(x, y, c, m):
    return x ^ (m >> 2), y ^ ((m >> 1) & 1), c ^ (m & 1)


def _bcast_task(pack):
    def copies(refs, ss, rs):
        x, y, c = _place()
        me = 4 * x + 2 * y + c
        return [pltpu.make_async_remote_copy(
            src_ref=refs[0], dst_ref=refs[1].at[me], send_sem=ss.at[m - 1], recv_sem=rs.at[m - 1],
            device_id=_peer(x, y, c, m), device_id_type=MESH) for m in range(1, N_DEV)]

    def local(refs, ls):
        x, y, c = _place()
        return pltpu.make_async_copy(refs[0], refs[1].at[4 * x + 2 * y + c], ls.at[0])

    def start(refs, ss, rs, ls):
        local(refs, ls).start()
        for cp in copies(refs, ss, rs):
            cp.start()

    def wait(refs, ss, rs, ls):
        x, y, c = _place()
        for m in range(1, N_DEV):
            px, py, pc = _peer(x, y, c, m)
            pltpu.make_async_remote_copy(
                src_ref=refs[0], dst_ref=refs[1].at[4 * px + 2 * py + pc], send_sem=ss.at[m - 1],
                recv_sem=rs.at[m - 1], device_id=(px, py, pc), device_id_type=MESH).wait_recv()
        for cp in copies(refs, ss, rs):
            cp.wait_send()
        local(refs, ls).wait()

    return _Task([pack], [], start, wait, fresh=[_S((N_DEV,) + pack.shape, pack.dtype)], nsem=N_DEV - 1)


def _sum_packs(packs, name):
    _, R, L = packs.shape

    def body(p_ref, o_ref):
        acc = p_ref[0]
        for k in range(1, N_DEV):
            acc = acc + p_ref[k]
        o_ref[...] = acc

    return pl.pallas_call(body, name=name, out_shape=_S((R, L), packs.dtype), in_specs=[VMEM_SPEC],
                          out_specs=VMEM_SPEC, compiler_params=_cparams())(packs)


def _swap_halves(g, name, pair_n=None):
    g4, shape, view = _core_blocks(g, pair_n)

    def body(g_ref, o_ref, send_sem, recv_sem):
        x, y, c = _place()
        cp = pltpu.make_async_remote_copy(
            src_ref=view(g_ref, 1 - c), dst_ref=o_ref, send_sem=send_sem, recv_sem=recv_sem,
            device_id=(x, y, 1 - c), device_id_type=MESH)
        cp.start()
        cp.wait()

    return pl.pallas_call(
        body, name=name, out_shape=_S(shape, g.dtype), in_specs=[ANY], out_specs=ANY,
        scratch_shapes=[pltpu.SemaphoreType.DMA, pltpu.SemaphoreType.DMA],
    )(g4)


def _add_halves(g, landed, place, name, pair_n=None):
    _, r, cc = landed.shape
    tr = _tile(r, 512, HALO)
    if pair_n is None:
        g4 = g.reshape(N_CHIP, 2, r, cc)
        g_spec = pl.BlockSpec((None, None, tr, cc), lambda i, q, s: (q, s[0], i, 0))
    else:
        g4 = g
        g_spec = pl.BlockSpec((None, tr, cc), lambda i, q, s: (q, i, s[0]))

    def body(s_ref, g_ref, l_ref, o_ref, land_ref):
        q = pl.program_id(1)
        v = (g_ref[...].astype(F32) + l_ref[...].astype(F32)).astype(BF16)
        o_ref[...] = v

        @pl.when(q == s_ref[1])
        def _():
            land_ref[...] = v

    return pl.pallas_call(
        body, name=name, out_shape=(_S((N_CHIP, r, cc), BF16), _S((N_CHIP, r, cc), BF16)),
        grid_spec=pltpu.PrefetchScalarGridSpec(
            num_scalar_prefetch=1, grid=(r // tr, N_CHIP),
            in_specs=[g_spec,
                      pl.BlockSpec((None, tr, cc), lambda i, q, s: (q, i, 0))],
            out_specs=(pl.BlockSpec((None, tr, cc), lambda i, q, s: (q, i, 0)),
                       pl.BlockSpec((None, tr, cc), lambda i, q, s: (s[1], i, 0)))),
        compiler_params=_cparams(("arbitrary", "arbitrary")),
    )(place, g4, landed)


def _all_reduce_small(pack, name):
    R = pack.shape[0]

    def body(p_ref, o_ref, buf, send_sems, recv_sems):
        x, y, c = _place()
        me = 4 * x + 2 * y + c
        buf[me] = p_ref[...]
        cps = []
        for k in range(N_DEV - 1):
            m = k + 1
            peer = (x ^ (m >> 2), y ^ ((m >> 1) & 1), c ^ (m & 1))
            cps.append(pltpu.make_async_remote_copy(
                src_ref=p_ref, dst_ref=buf.at[me], send_sem=send_sems.at[k], recv_sem=recv_sems.at[k],
                device_id=peer, device_id_type=MESH))
        for cp in cps:
            cp.start()
        for k in range(N_DEV - 1):
            m = k + 1
            peer_idx = 4 * (x ^ (m >> 2)) + 2 * (y ^ ((m >> 1) & 1)) + (c ^ (m & 1))
            pltpu.make_async_remote_copy(
                src_ref=p_ref, dst_ref=buf.at[peer_idx], send_sem=send_sems.at[k], recv_sem=recv_sems.at[k],
                device_id=(x, y, c), device_id_type=MESH).wait_recv()
        for cp in cps:
            cp.wait_send()
        acc = buf[0]
        for k in range(1, N_DEV):
            acc = acc + buf[k]
        o_ref[...] = acc

    return pl.pallas_call(
        body, name=name, out_shape=_S((R, LANES), F32),
        in_specs=[VMEM_SPEC], out_specs=VMEM_SPEC,
        scratch_shapes=[pltpu.VMEM((N_DEV, R, LANES), F32), pltpu.SemaphoreType.DMA((N_DEV - 1,)),
                        pltpu.SemaphoreType.DMA((N_DEV - 1,))],
        compiler_params=_cparams(),
    )(pack)


def _adamw_math(w, g, m, v):
    m = ADAM_B1 * m + (1.0 - ADAM_B1) * g
    v = ADAM_B2 * v + (1.0 - ADAM_B2) * (g * g)
    m_hat = m / (1.0 - ADAM_B1 ** ADAM_STEP)
    v_hat = v / (1.0 - ADAM_B2 ** ADAM_STEP)
    delta = -ADAM_LR * (m_hat / (jnp.sqrt(v_hat) + ADAM_EPS) + ADAM_WD * w)
    return delta, m, v


def _adamw_block(p_ref, w_ref, m_ref, v_ref, g_ref, d_ref, nm_ref, nv_ref):
    g = p_ref[0].astype(F32)
    for q in range(1, N_CHIP):
        g = g + p_ref[q].astype(F32)
    g_ref[...] = g
    d_ref[...], nm_ref[...], nv_ref[...] = _adamw_math(w_ref[...], g, m_ref[...], v_ref[...])


def _adamw_big(parts, w, m, v, name):
    r, cc = w.shape
    tr = _tile(r, 128, HALO)
    body = functools.partial(_adamw_block)

    blk = pl.BlockSpec((tr, cc), lambda i: (i, 0))
    return pl.pallas_call(
        body, name=name, out_shape=tuple(_S((r, cc), F32) for _ in range(4)), grid=(r // tr,),
        in_specs=[pl.BlockSpec((N_CHIP, tr, cc), lambda i: (0, i, 0)), blk, blk, blk],
        out_specs=(blk, blk, blk, blk), compiler_params=_cparams(("parallel",)),
    )(parts, w, m, v)


def _adamw_small(ws, gs, ms, vs, name):
    n = len(ws)

    def body(*refs):
        w_r, g_r, m_r, v_r = refs[:n], refs[n:2 * n], refs[2 * n:3 * n], refs[3 * n:4 * n]
        d_r, nm_r, nv_r = refs[4 * n:5 * n], refs[5 * n:6 * n], refs[6 * n:7 * n]
        for k in range(n):
            d_r[k][...], nm_r[k][...], nv_r[k][...] = _adamw_math(w_r[k][...], g_r[k][...], m_r[k][...], v_r[k][...])

    shapes = tuple(_S(w.shape, F32) for w in ws)
    outs = pl.pallas_call(
        body, name=name, out_shape=shapes * 3,
        in_specs=[VMEM_SPEC] * (4 * n), out_specs=tuple([VMEM_SPEC] * (3 * n)),
        compiler_params=_cparams(),
    )(*ws, *gs, *ms, *vs)
    return outs[:n], outs[n:2 * n], outs[2 * n:]


def _block_diag(w, heads_per_block):
    H, hd, _ = w.shape
    nb = H // heads_per_block
    eye = jnp.eye(heads_per_block, dtype=w.dtype)
    w4 = w.reshape(nb, heads_per_block, hd, hd)
    return jnp.einsum("nhab,hg->nhagb", w4, eye).reshape(nb, heads_per_block * hd, heads_per_block * hd)


def _diag_blocks(bd, heads_per_block, hd):
    nb = bd.shape[0]
    b5 = bd.reshape(nb, heads_per_block, hd, heads_per_block, hd)
    return jnp.stack([b5[:, h, :, h, :] for h in range(heads_per_block)], axis=1).reshape(nb * heads_per_block, hd, hd)


def _as_rows(a):
    if a.ndim == 1:
        return a.reshape(-1, LANES) if a.shape[0] % LANES == 0 else a.reshape(1, -1)
    if a.ndim == 3:
        return a.reshape(-1, LANES) if (a.size % LANES == 0) else a.reshape(a.shape[0] * a.shape[1], a.shape[2])
    return a


def kernel(x, g_mix, w_in, lru_conv_w, lru_conv_b, lru_wa, lru_ba, lru_wx, lru_bx, lru_lambda, lru_w_out, sc_conv_w, sc_w_out, w_o, g_ffn, ffn_w_up, ffn_conv_w, ffn_w_down, g_final, loss_target, m_g_mix, m_w_in, m_lru_conv_w, m_lru_conv_b, m_lru_wa, m_lru_ba, m_lru_wx, m_lru_bx, m_lru_lambda, m_lru_w_out, m_sc_conv_w, m_sc_w_out, m_w_o, m_g_ffn, m_ffn_w_up, m_ffn_conv_w, m_ffn_w_down, m_g_final, v_g_mix, v_w_in, v_lru_conv_w, v_lru_conv_b, v_lru_wa, v_lru_ba, v_lru_wx, v_lru_bx, v_lru_lambda, v_lru_w_out, v_sc_conv_w, v_sc_w_out, v_w_o, v_g_ffn, v_ffn_w_up, v_ffn_conv_w, v_ffn_w_down, v_g_final):
    T, D = x.shape[1], x.shape[2]
    d_lru = lru_lambda.shape[0]
    d_sc = sc_conv_w.shape[1] * N_DEV
    F = ffn_w_down.shape[0] * N_DEV
    H = lru_wa.shape[0]
    assert d_lru == d_sc and H * HEAD_DIM == d_lru
    xs = x.reshape(T, D)
    tgt = loss_target.reshape(T, D)
    my_x, my_y, my_c = _place()
    me = 4 * my_x + 2 * my_y + my_c

    big = [w_in, lru_w_out, sc_w_out, w_o, ffn_w_up, ffn_w_down]
    big_names = ["w_in", "lru_w_out", "sc_w_out", "w_o", "ffn_w_up", "ffn_w_down"]
    place = jnp.stack([my_c, 2 * my_x + my_y, me]).astype(jnp.int32)
    n_in, n_up = w_in.shape[1], ffn_w_up.shape[1]
    paired = [n_in, None, None, None, n_up, None]
    big_bf = [_cast_into_slot(w, place, "cast_" + nm, paired=pn is not None)
              for w, nm, pn in zip(big, big_names, paired)]
    pad_rows = lambda a: jnp.pad(a, ((0, SUB - a.shape[0]), (0, 0)))
    gathered = _all_gather(big_bf + [pad_rows(lru_conv_w), pad_rows(sc_conv_w), pad_rows(ffn_conv_w)],
                           [True] * 6 + [False] * 3,
                           [True, False, False, False, False, False, True, True, True],
                           paired + [None] * 3, "all_gather_first")
    W_in, W_lo, W_so, W_o8, W_up, W_dn8 = gathered[:6]
    full_cols = lambda g, kw: g[:, :kw, :].transpose(1, 0, 2).reshape(kw, -1)
    cw_lru = full_cols(gathered[6], lru_conv_w.shape[0])
    cw_sc = full_cols(gathered[7], sc_conv_w.shape[0])
    cw_ffn = full_cols(gathered[8], ffn_conv_w.shape[0])

    C = _tile(d_lru, C_LRU)
    hpb = C // HEAD_DIM
    wa_bd = _block_diag(lru_wa, hpb).astype(BF16)
    wx_bd = _block_diag(lru_wx, hpb).astype(BF16)
    cb, ba, bx, lam = (a.reshape(1, d_lru) for a in (lru_conv_b, lru_ba, lru_bx, lru_lambda))

    h1 = _rms_fwd(xs, g_mix, "rms_mix")
    k8 = W_up.shape[1] // 8
    wide = 2 * max(n_in, n_up)
    p, ((W_o8,), (W_lo,), (W_so,), (W_up,)) = _mm_nn(
        h1, W_in, out_dtype=BF16, name="mm_in", tn=wide,
        tasks=[_gather_task(W_o8, ici=ALL_ROWS), _gather_task(W_lo, ici=ALL_ROWS), _gather_task(W_so, ici=ALL_ROWS),
               _gather_task(W_up, ici=(0, 4 * k8), pair_n=n_up)])
    hs, yl_pre, ((W_o8,), (W_lo,), (W_so,), (W_up,)) = _lru_fwd(
        p, cw_lru, cb, wa_bd, ba, wx_bd, bx, lam, name="lru_fwd",
        tasks=[_gather_task(W_o8, fwd=ALL_ROWS), _gather_task(W_lo, fwd=ALL_ROWS), _gather_task(W_so, fwd=ALL_ROWS),
               _gather_task(W_up, ici=(4 * k8, 3 * k8), fwd=(0, 4 * k8), pair_n=n_up)])
    ys_pre = _sc_fwd(p, cw_sc, d=d_sc, name="sc_fwd")
    y_lru, ((W_up,),) = _mm_small(
        "nn", yl_pre, None, W_lo, name="mm_lru_out",
        tasks=[_gather_task(W_up, ici=(7 * k8, k8), fwd=(4 * k8, 3 * k8), pair_n=n_up)])
    y_sc, ((W_up,),) = _mm_small("nn", ys_pre, None, W_so, name="mm_sc_out",
                                 tasks=[_gather_task(W_up, fwd=(7 * k8, k8), pair_n=n_up)])
    gate0 = 2 * d_lru + 3 * d_sc
    merged = _merge_fwd(p, y_lru, y_sc, col0=gate0, name="merge_fwd")
    W_o = W_o8.reshape(1, D, D)
    x1 = _mm_nn(merged, W_o, out_dtype=F32, residual=xs, name="mm_o")
    h2 = _rms_fwd(x1, g_ffn, "rms_ffn")
    uu, ((W_dn8,),) = _mm_nn(h2, W_up, out_dtype=BF16, name="mm_up", tn=wide,
                             tasks=[_gather_task(W_dn8, ici=ALL_ROWS)])
    act, ((W_dn8,),) = _ffn_act_fwd(uu, cw_ffn, name="ffn_act_fwd", tasks=[_gather_task(W_dn8, fwd=ALL_ROWS)])
    W_dn = W_dn8.reshape(1, F, D)
    x2 = _mm_nn(act, W_dn, out_dtype=F32, residual=x1, name="mm_down", tn=1024, tk=F)
    dx2, dx2b, loss_part, dg_final = _loss_head(x2, g_final, tgt, "loss_head")

    def pack_rows(arrs):
        flat = jnp.concatenate([a.reshape(-1) for a in arrs])
        rows = -(-flat.shape[0] // (SUB * LANES)) * SUB
        return jnp.pad(flat, (0, rows * LANES - flat.shape[0])).reshape(rows, LANES)

    def unpack_rows(pack, arrs):
        flat, out, o = pack.reshape(-1), [], 0
        for a in arrs:
            out.append(flat[o:o + a.size].reshape(a.shape))
            o += a.size
        return out

    dact = _mm_nt(dx2b, W_dn, out_dtype=BF16, name="mm_down_dx", tm=512, tko=F // 2, tn=D)
    gW_dn = _mm_tn(act, dx2b, 1, out_dtype=BF16, name="mm_down_dw", tk=1408, tt=2048).reshape(N_DEV, F // N_DEV, D)
    duu, dcw_ffn_g, dcw_ffn_v = _ffn_act_bwd(uu, dact, cw_ffn, name="ffn_act_bwd")
    dh2, ((land_dn,),) = _mm_nt(duu, W_up, out_dtype=BF16, name="mm_up_dx", tn=wide, tasks=[_swap_task(gW_dn)])
    parts_dn = _add_halves(gW_dn, land_dn, place, "rs_add_ffn_w_down")
    gW_up, ((mine_dn,),) = _mm_tn(h2, duu, N_CHIP, out_dtype=BF16, name="mm_up_dw", tk=512, tn=wide, tt=2048,
                                  tasks=[_exchange_task(*parts_dn)])
    dx1, dx1b, dg_ffn = _rms_bwd(x1, g_ffn, dh2, dx2, "rms_ffn_bwd")
    dmerged, ((land_up,),) = _mm_nt(dx1b, W_o, out_dtype=BF16, name="mm_o_dx", tn=D,
                                    tasks=[_swap_task(gW_up, pair_n=n_up)])
    parts_up, land_up = _add_halves(gW_up, land_up, place, "rs_add_ffn_w_up", pair_n=n_up)
    r8 = parts_up.shape[1] // 8
    gW_o, ((land_up,),) = _mm_tn(merged, dx1b, 1, out_dtype=BF16, name="mm_o_dw", tt=2048,
                                 tasks=[_exchange_task(parts_up, land_up, rows=(0, r8))])
    gW_o = gW_o.reshape(N_DEV, D // N_DEV, D)
    dp, dy_lru, dy_sc, ((land_up,),) = _merge_bwd(
        p, y_lru, y_sc, dmerged, col0=gate0, name="merge_bwd",
        tasks=[_exchange_task(parts_up, land_up, rows=(r8, 2 * r8))])
    dyl_pre, ((land_o,),) = _mm_small("nt", None, dy_lru, W_lo, name="mm_lru_out_dx", tasks=[_swap_task(gW_o)])
    parts_o = _add_halves(gW_o, land_o, place, "rs_add_w_o")
    gW_lo = _mm_small("tn", yl_pre, dy_lru, W_lo, name="mm_lru_out_dw")
    dys_pre, ((land_lo,),) = _mm_small("nt", None, dy_sc, W_so, name="mm_sc_out_dx", tasks=[_swap_task(gW_lo)])
    parts_lo = _add_halves(gW_lo, land_lo, place, "rs_add_lru_w_out")
    gW_so = _mm_small("tn", ys_pre, dy_sc, W_so, name="mm_sc_out_dw")
    dp, dcw_sc, ((land_up,),) = _sc_bwd(p, dys_pre, dp, cw_sc, d=d_sc, name="sc_bwd",
                                        tasks=[_exchange_task(parts_up, land_up, rows=(3 * r8, r8))])
    dp, dcw_lru, dcb, dwa_bd, dba, dwx_bd, dbx, dlam, ((land_up,), (mine_o,), (mine_lo,), (land_so,)) = _lru_bwd(
        p, hs, dyl_pre, dp, cw_lru, cb, wa_bd, ba, wx_bd, bx, lam, name="lru_bwd",
        tasks=[_exchange_task(parts_up, land_up, rows=(4 * r8, 2 * r8)), _exchange_task(*parts_o),
               _exchange_task(*parts_lo), _swap_task(gW_so)])
    parts_so = _add_halves(gW_so, land_so, place, "rs_add_sc_w_out")

    dwa = _diag_blocks(dwa_bd, hpb, HEAD_DIM)
    dwx = _diag_blocks(dwx_bd, hpb, HEAD_DIM)
    dcw_ffn = jnp.concatenate([dcw_ffn_g, dcw_ffn_v], axis=1)
    rep_grads = [dcb, dwa, dba, dwx, dbx, dlam, dg_ffn, dg_final]
    small_full = rep_grads + [dcw_lru, dcw_sc, dcw_ffn]
    gW_in, ((mine_up,), (mine_so,), (packs,)) = _mm_tn(
        h1, dp, N_CHIP, out_dtype=BF16, name="mm_in_dw", tk=512, tn=wide, tt=2048,
        tasks=[_exchange_task(parts_up, land_up, rows=(6 * r8, 2 * r8)), _exchange_task(*parts_so),
               _bcast_task(pack_rows(small_full))])
    land_in = _swap_halves(gW_in, "rs_swap_w_in", pair_n=n_in)
    parts_in = _add_halves(gW_in, land_in, place, "rs_add_w_in", pair_n=n_in)
    dh1, ((mine_in,),) = _mm_nt(dp, W_in, out_dtype=BF16, name="mm_in_dx", tn=wide,
                                tasks=[_exchange_task(*parts_in)])
    grad_x, dg_mix = _rms_bwd(xs, g_mix, dh1, dx1, "rms_mix_bwd", want_bf16=False)

    mine = [mine_in, mine_lo, mine_so, mine_o, mine_up, mine_dn]
    big_m = [m_w_in, m_lru_w_out, m_sc_w_out, m_w_o, m_ffn_w_up, m_ffn_w_down]
    big_v = [v_w_in, v_lru_w_out, v_sc_w_out, v_w_o, v_ffn_w_up, v_ffn_w_down]
    big_out = {nm: _adamw_big(pt, w, m, v, "adamw_" + nm)
               for nm, pt, w, m, v in zip(big_names, mine, big, big_m, big_v)}

    (scb, swa, sba, swx, sbx, slam, sg_ffn, sg_final, scw_lru, scw_sc, scw_ffn) = unpack_rows(
        _sum_packs(packs, "sum_small"), small_full)
    (sg_mix,) = unpack_rows(_all_reduce_small(pack_rows([dg_mix]), "all_reduce_g_mix"), [dg_mix])

    def my_cols(a):
        n = a.shape[1] // N_DEV
        return lax.dynamic_slice_in_dim(a, me * n, n, axis=1)

    small_names = ["g_mix", "lru_conv_w", "lru_conv_b", "lru_wa", "lru_ba", "lru_wx", "lru_bx", "lru_lambda",
                   "sc_conv_w", "g_ffn", "ffn_conv_w", "g_final"]
    small_w = [g_mix, lru_conv_w, lru_conv_b, lru_wa, lru_ba, lru_wx, lru_bx, lru_lambda, sc_conv_w, g_ffn,
               ffn_conv_w, g_final]
    small_m = [m_g_mix, m_lru_conv_w, m_lru_conv_b, m_lru_wa, m_lru_ba, m_lru_wx, m_lru_bx, m_lru_lambda,
               m_sc_conv_w, m_g_ffn, m_ffn_conv_w, m_g_final]
    small_v = [v_g_mix, v_lru_conv_w, v_lru_conv_b, v_lru_wa, v_lru_ba, v_lru_wx, v_lru_bx, v_lru_lambda,
               v_sc_conv_w, v_g_ffn, v_ffn_conv_w, v_g_final]
    small_g = [sg_mix.reshape(D), my_cols(scw_lru), scb.reshape(d_lru), swa, sba.reshape(d_lru), swx,
               sbx.reshape(d_lru), slam.reshape(d_lru), my_cols(scw_sc), sg_ffn.reshape(D), my_cols(scw_ffn),
               sg_final.reshape(D)]
    sd, snm, snv = _adamw_small([_as_rows(a) for a in small_w], [_as_rows(a) for a in small_g],
                                [_as_rows(a) for a in small_m], [_as_rows(a) for a in small_v], "adamw_small")
    small_out = {nm: (g, d.reshape(w.shape), nm_.reshape(w.shape), nv_.reshape(w.shape))
                 for nm, w, g, d, nm_, nv_ in zip(small_names, small_w, small_g, sd, snm, snv)}

    loss = lax.psum(loss_part[0, 0], AXES)
    order = ["g_mix", "w_in", "lru_conv_w", "lru_conv_b", "lru_wa", "lru_ba", "lru_wx", "lru_bx", "lru_lambda",
             "lru_w_out", "sc_conv_w", "sc_w_out", "w_o", "g_ffn", "ffn_w_up", "ffn_conv_w", "ffn_w_down", "g_final"]
    res = {**big_out, **small_out}
    return (loss, grad_x.reshape(x.shape),
            *[res[nm][0] for nm in order], *[res[nm][1] for nm in order],
            *[res[nm][2] for nm in order], *[res[nm][3] for nm in order])
```

```python
import functools
import math

import jax
import jax.numpy as jnp
from jax import lax
from jax.experimental import pallas as pl
from jax.experimental.pallas import tpu as pltpu

F32, BF16 = jnp.float32, jnp.bfloat16
MESH = pl.DeviceIdType.MESH
N_DEV = 8
N_CHIP = 4
AXES = ("x", "y", "c")

EPS = 1e-6
LRU_C = 8.0
HEAD_DIM = 64
ADAM_LR, ADAM_B1, ADAM_B2, ADAM_EPS, ADAM_WD, ADAM_STEP = 0.001, 0.9, 0.999, 1e-08, 0.01, 10

VMEM_LIMIT = 48 * 1024 * 1024
LANES = 128
SUB = 8
HALO = 16
TB = 512
TB_CHUNKED = 1024
C_LRU = 256
C_EW = 512
TM, TN, TK = 512, 1536, 2048


def _tile(n, pref, align=LANES):
    best = None
    for d in range(align, min(n, pref) + 1, align):
        if n % d == 0:
            best = d
    return best or n


def _cparams(sem=None, vmem=VMEM_LIMIT):
    kw = dict(vmem_limit_bytes=vmem)
    if sem is not None:
        kw["dimension_semantics"] = sem
    return pltpu.CompilerParams(**kw)


def _S(shape, dtype):
    return jax.ShapeDtypeStruct(shape, dtype)


ANY = pl.BlockSpec(memory_space=pl.ANY)
VMEM_SPEC = pl.BlockSpec(memory_space=pltpu.VMEM)


class _Task:
    def __init__(self, arrays, aliased, start, wait, fresh=(), nsem=3, mid=None):
        self.arrays, self.aliased, self.start, self.wait = arrays, aliased, start, wait
        self.fresh, self.nsem, self.mid = list(fresh), nsem, mid


def _call(name, grid, compute, in_specs, args, out_shape, out_specs, scratch, tasks=(), own_aliases=None):
    n_in, n_out, n_scr = len(args), len(out_shape), len(scratch)
    x_in, x_out, aliases, where = [], [], dict(own_aliases or {}), []
    for t in tasks:
        places = []
        for k, arr in enumerate(t.arrays):
            if k in t.aliased:
                aliases[n_in + len(x_in)] = n_out + len(x_out)
                places.append(("out", len(x_out)))
                x_out.append(_S(arr.shape, arr.dtype))
            else:
                places.append(("in", len(x_in)))
            x_in.append(arr)
        for shp in t.fresh:
            places.append(("out", len(x_out)))
            x_out.append(shp)
        where.append(places)
    n_xi, n_xo = len(x_in), len(x_out)

    def body(*refs):
        ins, xi = refs[:n_in], refs[n_in:n_in + n_xi]
        o0 = n_in + n_xi
        outs, xo = refs[o0:o0 + n_out], refs[o0 + n_out:o0 + n_out + n_xo]
        s0 = o0 + n_out + n_xo
        scr, sems = refs[s0:s0 + n_scr], refs[s0 + n_scr:]
        ids = [pl.program_id(a) for a in range(len(grid))]

        def task_refs(ti):
            return [xo[i] if kind == "out" else xi[i] for kind, i in where[ti]]

        if tasks:
            first = functools.reduce(jnp.logical_and, [i == 0 for i in ids])

            @pl.when(first)
            def _():
                for ti, t in enumerate(tasks):
                    t.start(task_refs(ti), *sems[3 * ti:3 * ti + 3])

        compute(*ins, *outs, *scr)
        if any(t.mid is not None for t in tasks):
            n_steps = math.prod(grid)
            step = functools.reduce(lambda s, ig: s * ig[1] + ig[0], zip(ids, grid), 0)

            @pl.when(step == (5 * n_steps) // 8)
            def _():
                for ti, t in enumerate(tasks):
                    if t.mid is not None:
                        t.mid(task_refs(ti), *sems[3 * ti:3 * ti + 3])

        if tasks:
            last = functools.reduce(jnp.logical_and, [i == g - 1 for i, g in zip(ids, grid)])

            @pl.when(last)
            def _():
                for ti, t in enumerate(tasks):
                    t.wait(task_refs(ti), *sems[3 * ti:3 * ti + 3])

    sem_shapes = []
    for t in tasks:
        sem_shapes += [pltpu.SemaphoreType.DMA((t.nsem,)), pltpu.SemaphoreType.DMA((t.nsem,)),
                       pltpu.SemaphoreType.DMA((1,))]
    res = pl.pallas_call(
        body, name=name, grid=grid,
        in_specs=list(in_specs) + [ANY] * n_xi,
        out_specs=tuple(out_specs) + (ANY,) * n_xo,
        out_shape=tuple(out_shape) + tuple(x_out),
        scratch_shapes=list(scratch) + sem_shapes,
        input_output_aliases=aliases,
        compiler_params=_cparams(("arbitrary",) * len(grid)),
    )(*args, *x_in)
    outs, passed, o = res[:n_out], [], n_out
    for places in where:
        k = sum(1 for kind, _ in places if kind == "out")
        passed.append(list(res[o:o + k]))
        o += k
    return outs, passed


def _mm_nn(a, w3, *, out_dtype, name, residual=None, tm=TM, tn=TN, tk=TK, tasks=()):
    M, K = a.shape
    G, _, n = w3.shape
    tm, tn, tk = _tile(M, tm, SUB), _tile(n, tn), _tile(K, tk)
    nj, nk = n // tn, K // tk

    def compute(*refs):
        if residual is None:
            a_ref, w_ref, o_ref = refs[:3]
            r_ref = None
        else:
            a_ref, w_ref, r_ref, o_ref = refs[:4]

        def finish(r):
            if r_ref is not None:
                r = r + r_ref[...]
            o_ref[...] = r.astype(o_ref.dtype)

        if nk == 1:
            finish(jnp.dot(a_ref[...], w_ref[...], preferred_element_type=F32))
            return
        acc = refs[-1]
        k = pl.program_id(3)

        @pl.when(k == 0)
        def _():
            acc[...] = jnp.zeros_like(acc)

        acc[...] += jnp.dot(a_ref[...], w_ref[...], preferred_element_type=F32)

        @pl.when(k == nk - 1)
        def _():
            finish(acc[...])

    in_specs = [pl.BlockSpec((tm, tk), lambda g, j, i, k: (i, k)),
                pl.BlockSpec((None, tk, tn), lambda g, j, i, k: (g, k, j))]
    args = [a, w3]
    if residual is not None:
        in_specs.append(pl.BlockSpec((tm, tn), lambda g, j, i, k: (i, g * nj + j)))
        args.append(residual)
    outs, passed = _call(
        name, (G, nj, M // tm, nk), compute, in_specs, args, [_S((M, G * n), out_dtype)],
        [pl.BlockSpec((tm, tn), lambda g, j, i, k: (i, g * nj + j))],
        [] if nk == 1 else [pltpu.VMEM((tm, tn), F32)], tasks)
    return (outs[0], passed) if tasks else outs[0]


def _mm_nt(dy, w3, *, out_dtype, name, tm=1024, tko=1024, tn=TN, tasks=()):
    M, _ = dy.shape
    G, K, n = w3.shape
    tm, tko, tn = _tile(M, tm, SUB), _tile(K, tko), _tile(n, tn)
    nj = n // tn
    nr = G * nj

    def compute(dy_ref, w_ref, o_ref, *scr):
        part = lax.dot_general(dy_ref[...], w_ref[...], (((1,), (1,)), ((), ())), preferred_element_type=F32)
        if nr == 1:
            o_ref[...] = part.astype(o_ref.dtype)
            return
        (acc,) = scr
        r = pl.program_id(2)

        @pl.when(r == 0)
        def _():
            acc[...] = jnp.zeros_like(acc)

        acc[...] += part

        @pl.when(r == nr - 1)
        def _():
            o_ref[...] = acc[...].astype(o_ref.dtype)

    outs, passed = _call(
        name, (K // tko, M // tm, nr), compute,
        [pl.BlockSpec((tm, tn), lambda ko, i, r: (i, r)),
         pl.BlockSpec((None, tko, tn), lambda ko, i, r: (r // nj, ko, r % nj))],
        [dy, w3], [_S((M, K), out_dtype)], [pl.BlockSpec((tm, tko), lambda ko, i, r: (i, ko))],
        [] if nr == 1 else [pltpu.VMEM((tm, tko), F32)], tasks)
    return (outs[0], passed) if tasks else outs[0]


def _mm_tn(a, dy, G, *, out_dtype, name, tk=1024, tn=TN, tt=1024, tasks=()):
    M, K = a.shape
    n = dy.shape[1] // G
    tk, tn, tt = _tile(K, tk), _tile(n, tn), _tile(M, tt, SUB)
    nj, nt = n // tn, M // tt

    def compute(a_ref, dy_ref, o_ref, acc):
        t = pl.program_id(3)

        @pl.when(t == 0)
        def _():
            acc[...] = jnp.zeros_like(acc)

        acc[...] += lax.dot_general(a_ref[...], dy_ref[...], (((0,), (0,)), ((), ())),
                                    preferred_element_type=F32)

        @pl.when(t == nt - 1)
        def _():
            o_ref[...] = acc[...].astype(o_ref.dtype)

    outs, passed = _call(
        name, (G, nj, K // tk, nt), compute,
        [pl.BlockSpec((tt, tk), lambda g, j, k, t: (t, k)),
         pl.BlockSpec((tt, tn), lambda g, j, k, t: (t, g * nj + j))],
        [a, dy], [_S((G, K, n), out_dtype)], [pl.BlockSpec((None, tk, tn), lambda g, j, k, t: (g, k, j))],
        [pltpu.VMEM((tk, tn), F32)], tasks)
    return (outs[0], passed) if tasks else outs[0]


def _mm_small(kind, a, b, w3, *, name, tm=1024, tasks=()):
    G, K, n = w3.shape
    M = (a if a is not None else b).shape[0]
    tm = _tile(M, tm, HALO)
    nt = M // tm
    w_spec = pl.BlockSpec((G, K, n), lambda i: (0, 0, 0))
    a_spec = pl.BlockSpec((tm, K), lambda i: (i, 0))
    b_spec = pl.BlockSpec((tm, G * n), lambda i: (i, 0))
    cols = lambda g: slice(g * n, (g + 1) * n)
    if kind == "nn":
        def compute(a_ref, w_ref, o_ref):
            av = a_ref[...]
            for g in range(G):
                o_ref[:, cols(g)] = jnp.dot(av, w_ref[g], preferred_element_type=F32).astype(o_ref.dtype)

        outs, passed = _call(name, (nt,), compute, [a_spec, w_spec], [a, w3], [_S((M, G * n), BF16)], [b_spec], [], tasks)
    elif kind == "nt":
        def compute(b_ref, w_ref, o_ref):
            acc = None
            for g in range(G):
                part = lax.dot_general(b_ref[:, cols(g)], w_ref[g], (((1,), (1,)), ((), ())),
                                       preferred_element_type=F32)
                acc = part if acc is None else acc + part
            o_ref[...] = acc.astype(o_ref.dtype)

        outs, passed = _call(name, (nt,), compute, [b_spec, w_spec], [b, w3], [_S((M, K), BF16)], [a_spec], [], tasks)
    else:
        def compute(a_ref, b_ref, o_ref, acc):
            i = pl.program_id(0)

            @pl.when(i == 0)
            def _():
                acc[...] = jnp.zeros_like(acc)

            at = a_ref[...].T
            for g in range(G):
                acc[g] += jnp.dot(at, b_ref[:, cols(g)], preferred_element_type=F32)

            @pl.when(i == nt - 1)
            def _():
                o_ref[...] = acc[...].astype(o_ref.dtype)

        outs, passed = _call(name, (nt,), compute, [a_spec, b_spec], [a, b], [_S((G, K, n), BF16)], [w_spec],
                             [pltpu.VMEM((G, K, n), F32)], tasks)
    return (outs[0], passed) if tasks else outs[0]


def _cast_into_slot(w, place, name, paired=False):
    R, C = w.shape
    tr = _tile(R, 512, HALO)

    def body(s_ref, w_ref, o_ref):
        del s_ref
        o_ref[...] = w_ref[...].astype(BF16)

    if paired:
        shape, out_map = (N_CHIP, R, 2 * C), lambda i, s: (s[1], i, s[0])
    else:
        shape, out_map = (N_DEV, R, C), lambda i, s: (s[2], i, 0)
    return pl.pallas_call(
        body, name=name, out_shape=_S(shape, BF16),
        grid_spec=pltpu.PrefetchScalarGridSpec(
            num_scalar_prefetch=1, grid=(R // tr,),
            in_specs=[pl.BlockSpec((tr, C), lambda i, s: (i, 0))],
            out_specs=pl.BlockSpec((None, tr, C), out_map)),
        compiler_params=_cparams(("parallel",)),
    )(place, w)


def _down(cur, prev8, j):
    return pltpu.roll(jnp.concatenate([prev8, cur], axis=0), j, 0)[SUB:, :]


def _up(cur, next8, j):
    n = cur.shape[0] + SUB
    return pltpu.roll(jnp.concatenate([cur, next8], axis=0), n - j, 0)[:cur.shape[0], :]


def _shifted_down(x, prev8, n):
    full = jnp.concatenate([prev8, x], axis=0)
    return [x] + [pltpu.roll(full, s, 0)[SUB:, :] for s in range(1, n)]


def _shifted_up(x, next8, n):
    m = x.shape[0] + SUB
    full = jnp.concatenate([x, next8], axis=0)
    return [x] + [pltpu.roll(full, m - s, 0)[:x.shape[0], :] for s in range(1, n)]


def _taps(sh, w_ref):
    kw = w_ref.shape[0]
    y = sh[0] * w_ref[pl.ds(kw - 1, 1), :]
    for k in range(kw - 1):
        y = y + sh[kw - 1 - k] * w_ref[pl.ds(k, 1), :]
    return y


def _conv(x, prev8, w_ref):
    return _taps(_shifted_down(x, prev8, w_ref.shape[0]), w_ref)


def _conv_t(dy, next8, w_ref):
    return _taps(_shifted_up(dy, next8, w_ref.shape[0]), w_ref)


def _conv_dw(dw_ref, dy, x, prev8, first):
    kw = dw_ref.shape[0]

    @pl.when(first)
    def _():
        dw_ref[...] = jnp.zeros_like(dw_ref)

    for k in range(kw):
        xs = x if k == kw - 1 else _down(x, prev8, kw - 1 - k)
        dw_ref[pl.ds(k, 1), :] += jnp.sum(dy * xs, axis=0, keepdims=True)


def _acc(ref, val, first):
    @pl.when(first)
    def _():
        ref[...] = jnp.zeros_like(ref)

    ref[...] += val


def _acc_row(ref, val, first):
    _acc(ref, jnp.sum(val, axis=0, keepdims=True), first)


def _prev8(h_ref, t):
    return jnp.where(t > 0, h_ref[...].astype(F32)[HALO - SUB:, :], 0.0)


def _next8(h_ref, is_last):
    return jnp.where(is_last, 0.0, h_ref[...].astype(F32)[:SUB, :])


_GELU_K0 = math.sqrt(2.0 / math.pi)
_GELU_K1 = 0.044715


def _gelu_and_grad(x):
    x2 = x * x
    th = jnp.tanh(_GELU_K0 * x * (1.0 + _GELU_K1 * x2))
    g = 0.5 * x * (1.0 + th)
    dg = 0.5 * (1.0 + th) + 0.5 * x * (1.0 - th * th) * (_GELU_K0 * (1.0 + 3.0 * _GELU_K1 * x2))
    return g, dg


def _neg_expm1(z):
    series = -z * (1.0 + z * (0.5 + z * (1.0 / 6.0 + z * (1.0 / 24.0))))
    return jnp.where(z > -0.03, series, 1.0 - jnp.exp(z))


def _store_staged(stages, dst_hbm, sems, step, n_steps, where):
    def copies(s, slot):
        return [pltpu.make_async_copy(
            st.at[slot], dst_hbm.at[pl.ds(r0, st.shape[1]), pl.ds(c0, st.shape[2])], sems.at[slot, k])
            for k, (st, (r0, c0)) in enumerate(zip(stages, where(s)))]

    slot = step % 2

    @pl.when(step > 0)
    def _():
        for cp in copies(step - 1, 1 - slot):
            cp.wait()

    for cp in copies(step, slot):
        cp.start()

    @pl.when(step == n_steps - 1)
    def _():
        for cp in copies(step, slot):
            cp.wait()


def _halo_prev_map(hb, col_fn):
    return lambda c, t: (jnp.maximum(t * hb - 1, 0), col_fn(c))


def _rms_fwd(x, g, name):
    T, D = x.shape
    tb = _tile(T, TB, SUB)

    def body(x_ref, g_ref, o_ref):
        xv = x_ref[...]
        rstd = lax.rsqrt(jnp.mean(xv * xv, axis=-1, keepdims=True) + EPS)
        o_ref[...] = (xv * rstd * g_ref[...]).astype(BF16)

    return pl.pallas_call(
        body, name=name, out_shape=_S((T, D), BF16), grid=(T // tb,),
        in_specs=[pl.BlockSpec((tb, D), lambda i: (i, 0)), pl.BlockSpec((1, D), lambda i: (0, 0))],
        out_specs=pl.BlockSpec((tb, D), lambda i: (i, 0)),
        compiler_params=_cparams(("parallel",)),
    )(x, g.reshape(1, D))


def _rms_bwd(x, g, dh, dres, name, want_bf16=True):
    T, D = x.shape
    tb = _tile(T, 256, SUB)

    def body(x_ref, g_ref, dh_ref, dr_ref, dx_ref, *rest):
        dg_ref = rest[-1]
        i = pl.program_id(0)
        xv = x_ref[...]
        rstd = lax.rsqrt(jnp.mean(xv * xv, axis=-1, keepdims=True) + EPS)
        xn = xv * rstd
        dhv = dh_ref[...].astype(F32)
        _acc_row(dg_ref, dhv * xn, i == 0)
        dxn = dhv * g_ref[...]
        dx = dr_ref[...] + rstd * (dxn - xn * jnp.mean(dxn * xn, axis=-1, keepdims=True))
        dx_ref[...] = dx
        if want_bf16:
            rest[0][...] = dx.astype(BF16)

    blk = pl.BlockSpec((tb, D), lambda i: (i, 0))
    vec = pl.BlockSpec((1, D), lambda i: (0, 0))
    extra = [(_S((T, D), BF16), blk)] if want_bf16 else []
    return pl.pallas_call(
        body, name=name, out_shape=(_S((T, D), F32), *[s for s, _ in extra], _S((1, D), F32)),
        grid=(T // tb,), in_specs=[blk, vec, blk, blk], out_specs=(blk, *[b for _, b in extra], vec),
        compiler_params=_cparams(("arbitrary",)),
    )(x, g.reshape(1, D), dh, dres)


def _loss_head(x2, g, target, name):
    T, D = x2.shape
    tb = _tile(T, 256, SUB)

    def body(x_ref, g_ref, t_ref, dx_ref, dxb_ref, loss_ref, dg_ref):
        i = pl.program_id(0)
        xv = x_ref[...]
        rstd = lax.rsqrt(jnp.mean(xv * xv, axis=-1, keepdims=True) + EPS)
        xn = xv * rstd
        err = xn * g_ref[...] - t_ref[...]
        part = 0.5 * jnp.sum(jnp.mean(err * err, axis=-1, keepdims=True), axis=0, keepdims=True)
        part = jnp.broadcast_to(part, (1, LANES))
        _acc(loss_ref, part, i == 0)
        dy = err * (1.0 / D)
        _acc_row(dg_ref, dy * xn, i == 0)
        dxn = dy * g_ref[...]
        dx = rstd * (dxn - xn * jnp.mean(dxn * xn, axis=-1, keepdims=True))
        dx_ref[...] = dx
        dxb_ref[...] = dx.astype(BF16)

    blk = pl.BlockSpec((tb, D), lambda i: (i, 0))
    vec = pl.BlockSpec((1, D), lambda i: (0, 0))
    return pl.pallas_call(
        body, name=name,
        out_shape=(_S((T, D), F32), _S((T, D), BF16), _S((1, LANES), F32), _S((1, D), F32)),
        grid=(T // tb,), in_specs=[blk, vec, blk],
        out_specs=(blk, blk, pl.BlockSpec((1, LANES), lambda i: (0, 0)), vec),
        compiler_params=_cparams(("arbitrary",)),
    )(x2, g.reshape(1, D), target)


def _lru_gates(xc, wa_ref, ba_ref, wx_ref, bx_ref, lam_ref):
    xcb = xc.astype(BF16)
    r = jax.nn.sigmoid(jnp.dot(xcb, wa_ref[...], preferred_element_type=F32) + ba_ref[...])
    i = jax.nn.sigmoid(jnp.dot(xcb, wx_ref[...], preferred_element_type=F32) + bx_ref[...])
    sp = jax.nn.softplus(-lam_ref[...])
    log_a = (-LRU_C * sp) * r
    a = jnp.exp(log_a)
    s = jnp.sqrt(_neg_expm1(2.0 * log_a))
    return xcb, r, i, a, s


def _lru_fwd(p, conv_w, conv_b, wa_bd, ba, wx_bd, bx, lam, *, name, tasks=()):
    T = p.shape[0]
    d = lam.shape[-1]
    C = _tile(d, C_LRU)
    nC = d // C
    tb = _tile(T, TB, HALO)
    nT, hb, nt = T // tb, tb // HALO, tb // SUB

    def body(x_ref, xh_ref, g_ref, cw_ref, cb_ref, wa_ref, ba_ref, wx_ref, bx_ref, lam_ref,
             hs_ref, y_ref, a_s, u_s, h_s):
        t = pl.program_id(1)

        @pl.when(t == 0)
        def _():
            h_s[...] = jnp.zeros_like(h_s)

        x = x_ref[...].astype(F32)
        xc = _conv(x, _prev8(xh_ref, t), cw_ref) + cb_ref[...]
        _, r, i, a, s = _lru_gates(xc, wa_ref, ba_ref, wx_ref, bx_ref, lam_ref)
        a_s[...] = a
        u_s[...] = s * (i * xc)
        row = lax.broadcasted_iota(jnp.int32, (SUB, C), 0)

        def step(k, h):
            o = pl.multiple_of(k * SUB, SUB)
            A = a_s[pl.ds(o, SUB), :]
            B = u_s[pl.ds(o, SUB), :]
            for sh in (1, 2, 4):
                m = row >= sh
                Ap = pltpu.roll(A, sh, 0)
                Bp = pltpu.roll(B, sh, 0)
                B = jnp.where(m, A * Bp + B, B)
                A = jnp.where(m, A * Ap, A)
            hs = A * h + B
            hs_ref[pl.ds(o, SUB), :] = hs
            return jnp.broadcast_to(hs[SUB - 1:SUB, :], (SUB, C))

        h_s[...] = lax.fori_loop(0, nt, step, h_s[...])
        gel, _ = _gelu_and_grad(g_ref[...].astype(F32))
        y_ref[...] = (gel * hs_ref[...]).astype(BF16)

    vec = pl.BlockSpec((1, C), lambda c, t: (0, c))
    sq = pl.BlockSpec((None, C, C), lambda c, t: (c, 0, 0))
    outs, passed = _call(
        name, (nC, nT), body,
        [pl.BlockSpec((tb, C), lambda c, t: (t, c)),
         pl.BlockSpec((HALO, C), _halo_prev_map(hb, lambda c: c)),
         pl.BlockSpec((tb, C), lambda c, t: (t, nC + c)),
         pl.BlockSpec((conv_w.shape[0], C), lambda c, t: (0, c)),
         vec, sq, vec, sq, vec, vec],
        [p, p, p, conv_w, conv_b, wa_bd, ba, wx_bd, bx, lam],
        [_S((T, d), F32), _S((T, d), BF16)],
        [pl.BlockSpec((tb, C), lambda c, t: (t, c)), pl.BlockSpec((tb, C), lambda c, t: (t, c))],
        [pltpu.VMEM((tb, C), F32), pltpu.VMEM((tb, C), F32), pltpu.VMEM((SUB, C), F32)], tasks)
    return (*outs, passed) if tasks else outs


def _lru_bwd(p, hs, dyl, dp, conv_w, conv_b, wa_bd, ba, wx_bd, bx, lam, *, name, tasks=()):
    T = p.shape[0]
    d = lam.shape[-1]
    C = _tile(d, C_LRU)
    nC = d // C
    tb = _tile(T, TB, HALO)
    nT, hb, nt = T // tb, tb // HALO, tb // SUB
    kw = conv_w.shape[0]

    def body(x_ref, xh_ref, g_ref, hs_ref, hh_ref, dy_ref, cw_ref, cb_ref, wa_ref, ba_ref, wx_ref, bx_ref,
             lam_ref, dp_in, dp_ref, dcw_ref, dcb_ref, dwa_ref, dba_ref, dwx_ref, dbx_ref, dlam_ref,
             b_s, g_s, dh_s, an_s, dhn_s, dxn_s, st_x, st_g, sems):
        del dp_in
        c = pl.program_id(0)
        tr = pl.program_id(1)
        t = nT - 1 - tr
        first = tr == 0

        @pl.when(first)
        def _():
            an_s[...] = jnp.zeros_like(an_s)
            dhn_s[...] = jnp.zeros_like(dhn_s)
            dxn_s[...] = jnp.zeros_like(dxn_s)

        x = x_ref[...].astype(F32)
        xprev = _prev8(xh_ref, t)
        xc = _conv(x, xprev, cw_ref) + cb_ref[...]
        xcb, r, i, a, s = _lru_gates(xc, wa_ref, ba_ref, wx_ref, bx_ref, lam_ref)
        hsv = hs_ref[...]
        dy = dy_ref[...].astype(F32)
        gel, dgel = _gelu_and_grad(g_ref[...].astype(F32))
        step_no = c * nT + tr
        slot = step_no % 2
        st_g[slot] = (dy * hsv * dgel).astype(BF16)

        b_s[...] = _up(a, an_s[...], 1)
        g_s[...] = dy * gel
        row = lax.broadcasted_iota(jnp.int32, (SUB, C), 0)

        def step(k, carry):
            o = pl.multiple_of((nt - 1 - k) * SUB, SUB)
            B = b_s[pl.ds(o, SUB), :]
            G = g_s[pl.ds(o, SUB), :]
            for sh in (1, 2, 4):
                m = row < SUB - sh
                Bn = pltpu.roll(B, SUB - sh, 0)
                Gn = pltpu.roll(G, SUB - sh, 0)
                G = jnp.where(m, B * Gn + G, G)
                B = jnp.where(m, B * Bn, B)
            dh = B * carry + G
            dh_s[pl.ds(o, SUB), :] = dh
            return jnp.broadcast_to(dh[0:1, :], (SUB, C))

        dhn_s[...] = lax.fori_loop(0, nt, step, dhn_s[...])
        an_s[...] = a[:SUB, :]
        dh = dh_s[...]

        hprev = _down(hsv, jnp.where(t > 0, hh_ref[...][HALO - SUB:, :], 0.0), 1)
        d_a = dh * hprev
        ixc = i * xc
        d_s = dh * ixc
        d_i = dh * s * xc
        d_xc = dh * s * i
        d_l = d_a * a - d_s * (a * a) / s
        sp = jax.nn.softplus(-lam_ref[...])
        _acc_row(dlam_ref, d_l * r * (LRU_C * jax.nn.sigmoid(-lam_ref[...])), first)
        d_zr = (d_l * (-LRU_C * sp)) * r * (1.0 - r)
        d_zi = d_i * i * (1.0 - i)
        _acc_row(dba_ref, d_zr, first)
        _acc_row(dbx_ref, d_zi, first)
        d_zrb = d_zr.astype(BF16)
        d_zib = d_zi.astype(BF16)
        tn_dims = (((0,), (0,)), ((), ()))
        nt_dims = (((1,), (1,)), ((), ()))
        gwa = lax.dot_general(xcb, d_zrb, tn_dims, preferred_element_type=F32)
        gwx = lax.dot_general(xcb, d_zib, tn_dims, preferred_element_type=F32)
        _acc(dwa_ref, gwa, first)
        _acc(dwx_ref, gwx, first)
        d_xc = (d_xc + lax.dot_general(d_zrb, wa_ref[...], nt_dims, preferred_element_type=F32)
                + lax.dot_general(d_zib, wx_ref[...], nt_dims, preferred_element_type=F32))
        _acc_row(dcb_ref, d_xc, first)
        _conv_dw(dcw_ref, d_xc, x, xprev, first)
        st_x[slot] = _conv_t(d_xc, dxn_s[...], cw_ref).astype(BF16)
        dxn_s[...] = d_xc[:SUB, :]

        def where(s):
            row0, col0 = (nT - 1 - s % nT) * tb, (s // nT) * C
            return [(row0, col0), (row0, d + col0)]

        _store_staged([st_x, st_g], dp_ref, sems, step_no, nC * nT, where)

    rev = lambda c, tr: (nT - 1 - tr, c)
    vec = pl.BlockSpec((1, C), lambda c, tr: (0, c))
    sq = pl.BlockSpec((None, C, C), lambda c, tr: (c, 0, 0))
    cwb = pl.BlockSpec((kw, C), lambda c, tr: (0, c))
    halo_prev = lambda c, tr: (jnp.maximum((nT - 1 - tr) * hb - 1, 0), c)
    outs, passed = _call(
        name, (nC, nT), body,
        [pl.BlockSpec((tb, C), rev),
         pl.BlockSpec((HALO, C), halo_prev),
         pl.BlockSpec((tb, C), lambda c, tr: (nT - 1 - tr, nC + c)),
         pl.BlockSpec((tb, C), rev),
         pl.BlockSpec((HALO, C), halo_prev),
         pl.BlockSpec((tb, C), rev),
         cwb, vec, sq, vec, sq, vec, vec, ANY],
        [p, p, p, hs, hs, dyl, conv_w, conv_b, wa_bd, ba, wx_bd, bx, lam, dp],
        [_S(dp.shape, dp.dtype), _S((kw, d), F32), _S((1, d), F32), _S((nC, C, C), F32), _S((1, d), F32),
         _S((nC, C, C), F32), _S((1, d), F32), _S((1, d), F32)],
        [ANY, cwb, vec, sq, vec, sq, vec, vec],
        [pltpu.VMEM((tb, C), F32), pltpu.VMEM((tb, C), F32), pltpu.VMEM((tb, C), F32),
         pltpu.VMEM((SUB, C), F32), pltpu.VMEM((SUB, C), F32), pltpu.VMEM((SUB, C), F32),
         pltpu.VMEM((2, tb, C), BF16), pltpu.VMEM((2, tb, C), BF16), pltpu.SemaphoreType.DMA((2, 2))],
        tasks, own_aliases={13: 0})
    return (*outs, passed) if tasks else outs


def _sc_fwd(p, conv_w, *, d, name):
    T = p.shape[0]
    C = _tile(d, C_EW)
    nC = d // C
    tb = _tile(T, TB, HALO)
    nT, hb = T // tb, tb // HALO

    def body(b_ref, c_ref, ch_ref, v_ref, vh_ref, w_ref, y_ref):
        t = pl.program_id(1)
        cv = c_ref[...].astype(F32) * v_ref[...].astype(F32)
        cvp = _prev8(ch_ref, t) * _prev8(vh_ref, t)
        y_ref[...] = (b_ref[...].astype(F32) * _conv(cv, cvp, w_ref)).astype(BF16)

    seg = lambda k: pl.BlockSpec((tb, C), lambda c, t: (t, k * nC + c))
    hseg = lambda k: pl.BlockSpec((HALO, C), _halo_prev_map(hb, lambda c: k * nC + c))
    return pl.pallas_call(
        body, name=name, out_shape=_S((T, d), BF16), grid=(nC, nT),
        in_specs=[seg(2), seg(3), hseg(3), seg(4), hseg(4), pl.BlockSpec((conv_w.shape[0], C), lambda c, t: (0, c))],
        out_specs=pl.BlockSpec((tb, C), lambda c, t: (t, c)),
        compiler_params=_cparams(("parallel", "parallel")),
    )(p, p, p, p, p, conv_w)


def _sc_bwd(p, dys, dp, conv_w, *, d, name, tasks=()):
    T = p.shape[0]
    C = _tile(d, C_EW)
    nC = d // C
    tb = _tile(T, TB, HALO)
    nT, hb = T // tb, tb // HALO
    kw = conv_w.shape[0]

    def body(b_ref, bn_ref, c_ref, ch_ref, v_ref, vh_ref, dy_ref, dyn_ref, w_ref, dp_in, dp_ref, dw_ref,
             st_b, st_c, st_v, sems):
        del dp_in
        c = pl.program_id(0)
        t = pl.program_id(1)
        last = t == nT - 1
        bv = b_ref[...].astype(F32)
        cvv = c_ref[...].astype(F32)
        vv = v_ref[...].astype(F32)
        dy = dy_ref[...].astype(F32)
        cv = cvv * vv
        cvp = _prev8(ch_ref, t) * _prev8(vh_ref, t)
        step_no = c * nT + t
        slot = step_no % 2
        st_b[slot] = (dy * _conv(cv, cvp, w_ref)).astype(BF16)
        dz = dy * bv
        dzn = _next8(dyn_ref, last) * _next8(bn_ref, last)
        _conv_dw(dw_ref, dz, cv, cvp, t == 0)
        dcv = _conv_t(dz, dzn, w_ref)
        st_c[slot] = (dcv * vv).astype(BF16)
        st_v[slot] = (dcv * cvv).astype(BF16)

        def where(s):
            return [((s % nT) * tb, (2 + k) * d + (s // nT) * C) for k in range(3)]

        _store_staged([st_b, st_c, st_v], dp_ref, sems, step_no, nC * nT, where)

    seg = lambda k: pl.BlockSpec((tb, C), lambda c, t: (t, k * nC + c))
    hseg = lambda k: pl.BlockSpec((HALO, C), _halo_prev_map(hb, lambda c: k * nC + c))
    last_h = T // HALO - 1
    nseg = lambda k: pl.BlockSpec((HALO, C), lambda c, t: (jnp.minimum((t + 1) * hb, last_h), k * nC + c))
    outs, passed = _call(
        name, (nC, nT), body,
        [seg(2), nseg(2), seg(3), hseg(3), seg(4), hseg(4),
         pl.BlockSpec((tb, C), lambda c, t: (t, c)), nseg(0),
         pl.BlockSpec((kw, C), lambda c, t: (0, c)), ANY],
        [p, p, p, p, p, p, dys, dys, conv_w, dp],
        [_S(dp.shape, dp.dtype), _S((kw, d), F32)], [ANY, pl.BlockSpec((kw, C), lambda c, t: (0, c))],
        [pltpu.VMEM((2, tb, C), BF16)] * 3 + [pltpu.SemaphoreType.DMA((2, 3))], tasks, own_aliases={9: 0})
    return (*outs, passed) if tasks else outs


def _merge_fwd(p, y_lru, y_sc, *, col0, name, tasks=()):
    T, D = y_lru.shape
    C = _tile(math.gcd(D, col0), 1024)
    nC = D // C
    k0 = col0 // C
    tb = _tile(T, TB, HALO)

    def body(gl_ref, gs_ref, yl_ref, ys_ref, o_ref):
        @pl.loop(0, tb // HALO)
        def _(k):
            rows = pl.ds(pl.multiple_of(k * HALO, HALO), HALO)
            for l0 in range(0, C, min(C, C_EW)):
                at = (rows, pl.ds(l0, min(C, C_EW)))
                o_ref[at] = (jax.nn.sigmoid(gl_ref[at].astype(F32)) * yl_ref[at].astype(F32)
                             + jax.nn.sigmoid(gs_ref[at].astype(F32)) * ys_ref[at].astype(F32)).astype(BF16)

    blk = pl.BlockSpec((tb, C), lambda c, t: (t, c))
    outs, passed = _call(
        name, (nC, T // tb), body,
        [pl.BlockSpec((tb, C), lambda c, t: (t, k0 + c)), pl.BlockSpec((tb, C), lambda c, t: (t, k0 + nC + c)),
         blk, blk], [p, p, y_lru, y_sc], [_S((T, D), BF16)], [blk], [], tasks)
    return (outs[0], passed) if tasks else outs[0]


def _merge_bwd(p, y_lru, y_sc, dm, *, col0, name, tasks=()):
    T, D = y_lru.shape
    C = _tile(math.gcd(D, col0), 1024)
    nC = D // C
    k0 = col0 // C
    tb = _tile(T, TB, HALO)
    nT = T // tb

    def body(gl_ref, gs_ref, yl_ref, ys_ref, dm_ref, dp_ref, dyl_ref, dys_ref, st_l, st_s, sems):
        step_no = pl.program_id(0) * nT + pl.program_id(1)
        slot = step_no % 2

        @pl.loop(0, tb // HALO)
        def _(k):
            rows = pl.ds(pl.multiple_of(k * HALO, HALO), HALO)
            for l0 in range(0, C, min(C, C_EW)):
                at = (rows, pl.ds(l0, min(C, C_EW)))
                dmv = dm_ref[at].astype(F32)
                sl = jax.nn.sigmoid(gl_ref[at].astype(F32))
                ss = jax.nn.sigmoid(gs_ref[at].astype(F32))
                dyl_ref[at] = (dmv * sl).astype(BF16)
                dys_ref[at] = (dmv * ss).astype(BF16)
                st_l[(slot,) + at] = (dmv * yl_ref[at].astype(F32) * sl * (1.0 - sl)).astype(BF16)
                st_s[(slot,) + at] = (dmv * ys_ref[at].astype(F32) * ss * (1.0 - ss)).astype(BF16)

        def where(s):
            row0, colc = (s % nT) * tb, (s // nT) * C
            return [(row0, col0 + colc), (row0, col0 + D + colc)]

        _store_staged([st_l, st_s], dp_ref, sems, step_no, nC * nT, where)

    blk = pl.BlockSpec((tb, C), lambda c, t: (t, c))
    outs, passed = _call(
        name, (nC, nT), body,
        [pl.BlockSpec((tb, C), lambda c, t: (t, k0 + c)), pl.BlockSpec((tb, C), lambda c, t: (t, k0 + nC + c)),
         blk, blk, blk], [p, p, y_lru, y_sc, dm],
        [_S(p.shape, BF16), _S((T, D), BF16), _S((T, D), BF16)], [ANY, blk, blk],
        [pltpu.VMEM((2, tb, C), BF16), pltpu.VMEM((2, tb, C), BF16), pltpu.SemaphoreType.DMA((2, 2))], tasks)
    return (*outs, passed) if tasks else outs


def _ffn_act_fwd(uu, conv_w, *, name, tasks=()):
    T = uu.shape[0]
    F = uu.shape[1] // 2
    C = _tile(F, C_EW)
    nC = F // C
    tb = _tile(T, TB_CHUNKED, HALO)
    nT, hb = T // tb, tb // HALO
    kw = conv_w.shape[0]
    R = HALO

    def body(g_ref, gh_ref, v_ref, vh_ref, wg_ref, wv_ref, o_ref):
        t = pl.program_id(1)

        def chunk(k, carry):
            gp, vp = carry
            r0 = pl.multiple_of(k * R, R)
            ug = g_ref[pl.ds(r0, R), :].astype(F32)
            uv = v_ref[pl.ds(r0, R), :].astype(F32)
            cg = _conv(ug, gp, wg_ref)
            cv = _conv(uv, vp, wv_ref)
            o_ref[pl.ds(r0, R), :] = (cg * jax.nn.sigmoid(cg) * cv).astype(BF16)
            return ug[R - SUB:, :], uv[R - SUB:, :]

        lax.fori_loop(0, tb // R, chunk, (_prev8(gh_ref, t), _prev8(vh_ref, t)))

    seg = lambda k: pl.BlockSpec((tb, C), lambda c, t: (t, k * nC + c))
    hseg = lambda k: pl.BlockSpec((HALO, C), _halo_prev_map(hb, lambda c: k * nC + c))
    wseg = lambda k: pl.BlockSpec((kw, C), lambda c, t: (0, k * nC + c))
    outs, passed = _call(
        name, (nC, nT), body, [seg(0), hseg(0), seg(1), hseg(1), wseg(0), wseg(1)],
        [uu, uu, uu, uu, conv_w, conv_w], [_S((T, F), BF16)], [pl.BlockSpec((tb, C), lambda c, t: (t, c))], [], tasks)
    return (outs[0], passed) if tasks else outs[0]


def _ffn_act_bwd(uu, dact, conv_w, *, name):
    T = uu.shape[0]
    F = uu.shape[1] // 2
    C = _tile(F, C_EW)
    nC = F // C
    tb = _tile(T, TB_CHUNKED, HALO)
    nT, hb = T // tb, tb // HALO
    kw = conv_w.shape[0]
    R = HALO
    nk = tb // R

    def body(g_ref, gh_ref, v_ref, vh_ref, da_ref, wg_ref, wv_ref, du_ref, dwg_ref, dwv_ref,
             gn_s, vn_s, accg_s, accv_s, st_g, st_v, sems):
        c = pl.program_id(0)
        tr = pl.program_id(1)
        t = nT - 1 - tr
        first = tr == 0

        @pl.when(first)
        def _():
            gn_s[...] = jnp.zeros_like(gn_s)
            vn_s[...] = jnp.zeros_like(vn_s)
            dwg_ref[...] = jnp.zeros_like(dwg_ref)
            dwv_ref[...] = jnp.zeros_like(dwv_ref)

        accg_s[...] = jnp.zeros_like(accg_s)
        accv_s[...] = jnp.zeros_like(accv_s)
        step_no = c * nT + tr
        slot = step_no % 2

        def chunk(i, carry):
            gn, vn = carry
            k = nk - 1 - i
            r0 = pl.multiple_of(k * R, R)
            rp = pl.multiple_of(jnp.maximum(r0 - R, 0), R)
            ug = g_ref[pl.ds(r0, R), :].astype(F32)
            uv = v_ref[pl.ds(r0, R), :].astype(F32)
            gp = jnp.where(k > 0, g_ref[pl.ds(rp, R), :].astype(F32)[R - SUB:, :], _prev8(gh_ref, t))
            vp = jnp.where(k > 0, v_ref[pl.ds(rp, R), :].astype(F32)[R - SUB:, :], _prev8(vh_ref, t))
            sh_g = _shifted_down(ug, gp, kw)
            sh_v = _shifted_down(uv, vp, kw)
            cg = _taps(sh_g, wg_ref)
            cv = _taps(sh_v, wv_ref)
            da = da_ref[pl.ds(r0, R), :].astype(F32)
            sg = jax.nn.sigmoid(cg)
            d_cg = da * cv * (sg * (1.0 + cg * (1.0 - sg)))
            d_cv = da * (cg * sg)
            for j in range(kw):
                accg_s[j] += d_cg * sh_g[kw - 1 - j]
                accv_s[j] += d_cv * sh_v[kw - 1 - j]
            st_g[slot, pl.ds(r0, R), :] = _conv_t(d_cg, gn, wg_ref).astype(BF16)
            st_v[slot, pl.ds(r0, R), :] = _conv_t(d_cv, vn, wv_ref).astype(BF16)
            return d_cg[:SUB, :], d_cv[:SUB, :]

        gn, vn = lax.fori_loop(0, nk, chunk, (gn_s[...], vn_s[...]))
        gn_s[...] = gn
        vn_s[...] = vn
        for j in range(kw):
            dwg_ref[pl.ds(j, 1), :] += jnp.sum(accg_s[j], axis=0, keepdims=True)
            dwv_ref[pl.ds(j, 1), :] += jnp.sum(accv_s[j], axis=0, keepdims=True)
        def where(s):
            row0, col0 = (nT - 1 - s % nT) * tb, (s // nT) * C
            return [(row0, col0), (row0, F + col0)]

        _store_staged([st_g, st_v], du_ref, sems, step_no, nC * nT, where)

    seg = lambda k: pl.BlockSpec((tb, C), lambda c, tr: (nT - 1 - tr, k * nC + c))
    hseg = lambda k: pl.BlockSpec((HALO, C), lambda c, tr: (jnp.maximum((nT - 1 - tr) * hb - 1, 0), k * nC + c))
    wseg = lambda k: pl.BlockSpec((kw, C), lambda c, tr: (0, k * nC + c))
    dwb = pl.BlockSpec((kw, C), lambda c, tr: (0, c))
    return pl.pallas_call(
        body, name=name, out_shape=(_S(uu.shape, BF16), _S((kw, F), F32), _S((kw, F), F32)), grid=(nC, nT),
        in_specs=[seg(0), hseg(0), seg(1), hseg(1), pl.BlockSpec((tb, C), lambda c, tr: (nT - 1 - tr, c)),
                  wseg(0), wseg(1)],
        out_specs=(ANY, dwb, dwb),
        scratch_shapes=[pltpu.VMEM((SUB, C), F32), pltpu.VMEM((SUB, C), F32),
                        pltpu.VMEM((kw, R, C), F32), pltpu.VMEM((kw, R, C), F32),
                        pltpu.VMEM((2, tb, C), BF16), pltpu.VMEM((2, tb, C), BF16), pltpu.SemaphoreType.DMA((2, 2))],
        compiler_params=_cparams(("arbitrary", "arbitrary")),
    )(uu, uu, uu, uu, dact, conv_w, conv_w)


def _place():
    x, y, c = lax.axis_index("x"), lax.axis_index("y"), lax.axis_index("c")
    return x, y, c


def _chips(x, y):
    return [(1 - x, y), (x, 1 - y), (1 - x, 1 - y)]


def _all_gather(arrays, placed, over_ici, pair_n, name):
    n = len(arrays)

    def body(*refs):
        ins, outs = refs[:n], refs[n:2 * n]
        send_sems, recv_sems, local_sems = refs[2 * n:]
        x, y, c = _place()
        me, sibling = (x, y, c), (x, y, 1 - c)
        chips = _chips(x, y)
        full = [a for a in range(n) if over_ici[a]]

        def idx(px, py, pc):
            return 4 * px + 2 * py + pc

        def copy(a, k, block, to):
            dst = _dev_block(outs[a], idx(*block), pair_n[a])
            src = ins[a] if (block is me and not placed[a]) else dst
            return pltpu.make_async_remote_copy(
                src_ref=src, dst_ref=dst, send_sem=send_sems.at[a, k], recv_sem=recv_sems.at[a, k],
                device_id=to, device_id_type=MESH)

        def half(a, k, block, to, lo):
            r = rows_of[a] // 2
            blk = _rows_of(outs[a], idx(*block), (0 if lo else r, r), pair_n[a])
            return pltpu.make_async_remote_copy(
                src_ref=blk, dst_ref=blk, send_sem=send_sems.at[a, k], recv_sem=recv_sems.at[a, k],
                device_id=to, device_id_type=MESH)

        mine = [pltpu.make_async_copy(ins[a], outs[a].at[idx(*me)], local_sems.at[a])
                for a in range(n) if not placed[a]]
        for cp in mine:
            cp.start()
        chip_x, chip_y, chip_d = chips
        sent = []
        for a in full:
            sent += [copy(a, 1, me, (*chip_x, c)), copy(a, 2, me, (*chip_y, c))]
            if not relay[a]:
                sent.append(copy(a, 3, me, (*chip_d, c)))
        for a in range(n):
            sent.append(copy(a, 0, me, sibling))
        for cp in sent:
            cp.start()

        def then(cp):
            cp.start()
            sent.append(cp)

        for a in full:
            copy(a, 2, (*chip_y, c), me).wait_recv()
            if relay[a]:
                then(half(a, 3, (*chip_y, c), (*chip_x, c), True))
            then(copy(a, 6, (*chip_y, c), sibling))
            copy(a, 1, (*chip_x, c), me).wait_recv()
            if relay[a]:
                then(half(a, 4, (*chip_x, c), (*chip_y, c), False))
            then(copy(a, 5, (*chip_x, c), sibling))
        for a in full:
            if relay[a]:
                half(a, 3, (*chip_d, c), me, True).wait_recv()
                half(a, 4, (*chip_d, c), me, False).wait_recv()
            else:
                copy(a, 3, (*chip_d, c), me).wait_recv()
            then(copy(a, 7, (*chip_d, c), sibling))
        for a in range(n):
            copy(a, 0, sibling, me).wait_recv()
        for a in full:
            for j, chip in enumerate(chips):
                copy(a, 5 + j, (*chip, 1 - c), me).wait_recv()
        for cp in sent:
            cp.wait_send()
        for cp in mine:
            cp.wait()

    rows_of = [(s.shape[1] if placed[a] else s.shape[0]) for a, s in enumerate(arrays)]
    relay = [r % (2 * HALO) == 0 for r in rows_of]
    return pl.pallas_call(
        body, name=name,
        out_shape=tuple(_S(s.shape if placed[a] else (N_DEV,) + s.shape, s.dtype) for a, s in enumerate(arrays)),
        in_specs=[ANY] * n, out_specs=tuple([ANY] * n),
        scratch_shapes=[pltpu.SemaphoreType.DMA((n, 8)), pltpu.SemaphoreType.DMA((n, 8)),
                        pltpu.SemaphoreType.DMA((n,))],
        input_output_aliases={a: a for a in range(n) if placed[a]},
    )(*arrays)


def _dev_block(ref, dev, pair_n=None):
    if pair_n is None:
        return ref.at[dev]
    return ref.at[dev // 2, :, pl.ds(pl.multiple_of((dev % 2) * pair_n, LANES), pair_n)]


def _rows_of(ref, blk, rows, pair_n=None):
    v = _dev_block(ref, blk, pair_n)
    return v if rows is None else v.at[pl.ds(rows[0], rows[1])]


ALL_ROWS = "all"


def _gather_task(buf, ici=None, fwd=None, pair_n=None):
    blk_of = functools.partial(_rows_of, pair_n=pair_n)
    r0, nr = (0, buf.shape[1]) if ici == ALL_ROWS else (ici or (0, 0))
    assert nr % (2 * HALO) == 0
    lo, hi = (r0, nr // 2), (r0 + nr // 2, nr // 2)
    both = (r0, nr)
    fwd_rows = None if fwd == ALL_ROWS else fwd

    def remote(refs, ss, rs, k, dev, rows, to):
        blk = blk_of(refs[0], dev, rows)
        return pltpu.make_async_remote_copy(src_ref=blk, dst_ref=blk, send_sem=ss.at[k], recv_sem=rs.at[k],
                                            device_id=to, device_id_type=MESH)

    def waves(refs, ss, rs):
        x, y, c = _place()
        me = 4 * x + 2 * y + c
        (xx, xy), (yx, yy), _ = _chips(x, y)
        dev_x, dev_y = 4 * xx + 2 * xy + c, 4 * yx + 2 * yy + c
        first, second = [], []
        if ici is not None:
            first += [remote(refs, ss, rs, 0, me, both, (xx, xy, c)), remote(refs, ss, rs, 1, me, both, (yx, yy, c))]
            second += [remote(refs, ss, rs, 2, dev_y, lo, (xx, xy, c)), remote(refs, ss, rs, 3, dev_x, hi, (yx, yy, c))]
        if fwd is not None:
            first += [remote(refs, ss, rs, 4 + j, 4 * px + 2 * py + c, fwd_rows, (x, y, 1 - c))
                      for j, (px, py) in enumerate(_chips(x, y))]
        return first, second

    def start(refs, ss, rs, ls):
        for cp in waves(refs, ss, rs)[0]:
            cp.start()

    def mid(refs, ss, rs, ls):
        x, y, c = _place()
        (xx, xy), (yx, yy), _ = _chips(x, y)
        remote(refs, ss, rs, 0, 4 * xx + 2 * xy + c, both, (x, y, c)).wait_recv()
        remote(refs, ss, rs, 1, 4 * yx + 2 * yy + c, both, (x, y, c)).wait_recv()
        for cp in waves(refs, ss, rs)[1]:
            cp.start()

    def wait(refs, ss, rs, ls):
        x, y, c = _place()
        chips = _chips(x, y)
        if ici is not None:
            dev_d = 4 * chips[2][0] + 2 * chips[2][1] + c
            remote(refs, ss, rs, 2, dev_d, lo, (x, y, c)).wait_recv()
            remote(refs, ss, rs, 3, dev_d, hi, (x, y, c)).wait_recv()
        if fwd is not None:
            for j, (px, py) in enumerate(chips):
                remote(refs, ss, rs, 4 + j, 4 * px + 2 * py + 1 - c, fwd_rows, (x, y, c)).wait_recv()
        first, second = waves(refs, ss, rs)
        for cp in first + second:
            cp.wait_send()

    return _Task([buf], [0], start, wait, nsem=7, mid=mid if ici is not None else None)


def _exchange_task(parts, landing, rows=None):
    def copies(refs, ss, rs):
        x, y, c = _place()
        myq = 2 * x + y
        return [pltpu.make_async_remote_copy(
            src_ref=_rows_of(refs[0], 2 * px + py, rows), dst_ref=_rows_of(refs[1], myq, rows),
            send_sem=ss.at[k], recv_sem=rs.at[k], device_id=(px, py, c), device_id_type=MESH)
            for k, (px, py) in enumerate(_chips(x, y))]

    def start(refs, ss, rs, ls):
        for cp in copies(refs, ss, rs):
            cp.start()

    def wait(refs, ss, rs, ls):
        x, y, c = _place()
        for k, (px, py) in enumerate(_chips(x, y)):
            pltpu.make_async_remote_copy(
                src_ref=_rows_of(refs[0], 2 * x + y, rows), dst_ref=_rows_of(refs[1], 2 * px + py, rows),
                send_sem=ss.at[k], recv_sem=rs.at[k], device_id=(px, py, c), device_id_type=MESH).wait_recv()
        for cp in copies(refs, ss, rs):
            cp.wait_send()

    return _Task([parts, landing], [1], start, wait)


def _core_blocks(g, pair_n):
    if pair_n is None:
        g4 = g.reshape((N_CHIP, 2) + g.shape[1:])
        return g4, (N_CHIP,) + g.shape[1:], lambda ref, c: ref.at[:, c]
    view = lambda ref, c: ref.at[:, :, pl.ds(pl.multiple_of(c * pair_n, LANES), pair_n)]
    return g, (N_CHIP, g.shape[1], pair_n), view


def _swap_task(g, pair_n=None):
    g4, shape, view = _core_blocks(g, pair_n)

    def copy(refs, ss, rs):
        x, y, c = _place()
        return pltpu.make_async_remote_copy(
            src_ref=view(refs[0], 1 - c), dst_ref=refs[1], send_sem=ss.at[0], recv_sem=rs.at[0],
            device_id=(x, y, 1 - c), device_id_type=MESH)

    def start(refs, ss, rs, ls):
        copy(refs, ss, rs).start()

    def wait(refs, ss, rs, ls):
        copy(refs, ss, rs).wait()

    return _Task([g4], [], start, wait, fresh=[_S(shape, g.dtype)], nsem=1)


def _peer(x, y, c, m):
    return x ^ (m >> 2), y ^ ((m >> 1) & 1), c ^ (m & 1)


def _bcast_task(pack):
    def copies(refs, ss, rs):
        x, y, c = _place()
        me = 4 * x + 2 * y + c
        return [pltpu.make_async_remote_copy(
            src_ref=refs[0], dst_ref=refs[1].at[me], send_sem=ss.at[m - 1], recv_sem=rs.at[m - 1],
            device_id=_peer(x, y, c, m), device_id_type=MESH) for m in range(1, N_DEV)]

    def local(refs, ls):
        x, y, c = _place()
        return pltpu.make_async_copy(refs[0], refs[1].at[4 * x + 2 * y + c], ls.at[0])

    def start(refs, ss, rs, ls):
        local(refs, ls).start()
        for cp in copies(refs, ss, rs):
            cp.start()

    def wait(refs, ss, rs, ls):
        x, y, c = _place()
        for m in range(1, N_DEV):
            px, py, pc = _peer(x, y, c, m)
            pltpu.make_async_remote_copy(
                src_ref=refs[0], dst_ref=refs[1].at[4 * px + 2 * py + pc], send_sem=ss.at[m - 1],
                recv_sem=rs.at[m - 1], device_id=(px, py, pc), device_id_type=MESH).wait_recv()
        for cp in copies(refs, ss, rs):
            cp.wait_send()
        local(refs, ls).wait()

    return _Task([pack], [], start, wait, fresh=[_S((N_DEV,) + pack.shape, pack.dtype)], nsem=N_DEV - 1)


def _sum_packs(packs, name):
    _, R, L = packs.shape

    def body(p_ref, o_ref):
        acc = p_ref[0]
        for k in range(1, N_DEV):
            acc = acc + p_ref[k]
        o_ref[...] = acc

    return pl.pallas_call(body, name=name, out_shape=_S((R, L), packs.dtype), in_specs=[VMEM_SPEC],
                          out_specs=VMEM_SPEC, compiler_params=_cparams())(packs)


def _swap_halves(g, name, pair_n=None):
    g4, shape, view = _core_blocks(g, pair_n)

    def body(g_ref, o_ref, send_sem, recv_sem):
        x, y, c = _place()
        cp = pltpu.make_async_remote_copy(
            src_ref=view(g_ref, 1 - c), dst_ref=o_ref, send_sem=send_sem, recv_sem=recv_sem,
            device_id=(x, y, 1 - c), device_id_type=MESH)
        cp.start()
        cp.wait()

    return pl.pallas_call(
        body, name=name, out_shape=_S(shape, g.dtype), in_specs=[ANY], out_specs=ANY,
        scratch_shapes=[pltpu.SemaphoreType.DMA, pltpu.SemaphoreType.DMA],
    )(g4)


def _add_halves(g, landed, place, name, pair_n=None):
    _, r, cc = landed.shape
    tr = _tile(r, 512, HALO)
    if pair_n is None:
        g4 = g.reshape(N_CHIP, 2, r, cc)
        g_spec = pl.BlockSpec((None, None, tr, cc), lambda i, q, s: (q, s[0], i, 0))
    else:
        g4 = g
        g_spec = pl.BlockSpec((None, tr, cc), lambda i, q, s: (q, i, s[0]))

    def body(s_ref, g_ref, l_ref, o_ref, land_ref):
        q = pl.program_id(1)
        v = (g_ref[...].astype(F32) + l_ref[...].astype(F32)).astype(BF16)
        o_ref[...] = v

        @pl.when(q == s_ref[1])
        def _():
            land_ref[...] = v

    return pl.pallas_call(
        body, name=name, out_shape=(_S((N_CHIP, r, cc), BF16), _S((N_CHIP, r, cc), BF16)),
        grid_spec=pltpu.PrefetchScalarGridSpec(
            num_scalar_prefetch=1, grid=(r // tr, N_CHIP),
            in_specs=[g_spec,
                      pl.BlockSpec((None, tr, cc), lambda i, q, s: (q, i, 0))],
            out_specs=(pl.BlockSpec((None, tr, cc), lambda i, q, s: (q, i, 0)),
                       pl.BlockSpec((None, tr, cc), lambda i, q, s: (s[1], i, 0)))),
        compiler_params=_cparams(("arbitrary", "arbitrary")),
    )(place, g4, landed)


def _all_reduce_small(pack, name):
    R = pack.shape[0]

    def body(p_ref, o_ref, buf, send_sems, recv_sems):
        x, y, c = _place()
        me = 4 * x + 2 * y + c
        buf[me] = p_ref[...]
        cps = []
        for k in range(N_DEV - 1):
            m = k + 1
            peer = (x ^ (m >> 2), y ^ ((m >> 1) & 1), c ^ (m & 1))
            cps.append(pltpu.make_async_remote_copy(
                src_ref=p_ref, dst_ref=buf.at[me], send_sem=send_sems.at[k], recv_sem=recv_sems.at[k],
                device_id=peer, device_id_type=MESH))
        for cp in cps:
            cp.start()
        for k in range(N_DEV - 1):
            m = k + 1
            peer_idx = 4 * (x ^ (m >> 2)) + 2 * (y ^ ((m >> 1) & 1)) + (c ^ (m & 1))
            pltpu.make_async_remote_copy(
                src_ref=p_ref, dst_ref=buf.at[peer_idx], send_sem=send_sems.at[k], recv_sem=recv_sems.at[k],
                device_id=(x, y, c), device_id_type=MESH).wait_recv()
        for cp in cps:
            cp.wait_send()
        acc = buf[0]
        for k in range(1, N_DEV):
            acc = acc + buf[k]
        o_ref[...] = acc

    return pl.pallas_call(
        body, name=name, out_shape=_S((R, LANES), F32),
        in_specs=[VMEM_SPEC], out_specs=VMEM_SPEC,
        scratch_shapes=[pltpu.VMEM((N_DEV, R, LANES), F32), pltpu.SemaphoreType.DMA((N_DEV - 1,)),
                        pltpu.SemaphoreType.DMA((N_DEV - 1,))],
        compiler_params=_cparams(),
    )(pack)


def _adamw_math(w, g, m, v):
    m = ADAM_B1 * m + (1.0 - ADAM_B1) * g
    v = ADAM_B2 * v + (1.0 - ADAM_B2) * (g * g)
    m_hat = m / (1.0 - ADAM_B1 ** ADAM_STEP)
    v_hat = v / (1.0 - ADAM_B2 ** ADAM_STEP)
    delta = -ADAM_LR * (m_hat / (jnp.sqrt(v_hat) + ADAM_EPS) + ADAM_WD * w)
    return delta, m, v


def _adamw_block(p_ref, w_ref, m_ref, v_ref, g_ref, d_ref, nm_ref, nv_ref):
    g = p_ref[0].astype(F32)
    for q in range(1, N_CHIP):
        g = g + p_ref[q].astype(F32)
    g_ref[...] = g
    d_ref[...], nm_ref[...], nv_ref[...] = _adamw_math(w_ref[...], g, m_ref[...], v_ref[...])


def _adamw_big(parts, w, m, v, name):
    r, cc = w.shape
    tr = _tile(r, 128, HALO)
    body = functools.partial(_adamw_block)

    blk = pl.BlockSpec((tr, cc), lambda i: (i, 0))
    return pl.pallas_call(
        body, name=name, out_shape=tuple(_S((r, cc), F32) for _ in range(4)), grid=(r // tr,),
        in_specs=[pl.BlockSpec((N_CHIP, tr, cc), lambda i: (0, i, 0)), blk, blk, blk],
        out_specs=(blk, blk, blk, blk), compiler_params=_cparams(("parallel",)),
    )(parts, w, m, v)


def _adamw_small(ws, gs, ms, vs, name):
    n = len(ws)

    def body(*refs):
        w_r, g_r, m_r, v_r = refs[:n], refs[n:2 * n], refs[2 * n:3 * n], refs[3 * n:4 * n]
        d_r, nm_r, nv_r = refs[4 * n:5 * n], refs[5 * n:6 * n], refs[6 * n:7 * n]
        for k in range(n):
            d_r[k][...], nm_r[k][...], nv_r[k][...] = _adamw_math(w_r[k][...], g_r[k][...], m_r[k][...], v_r[k][...])

    shapes = tuple(_S(w.shape, F32) for w in ws)
    outs = pl.pallas_call(
        body, name=name, out_shape=shapes * 3,
        in_specs=[VMEM_SPEC] * (4 * n), out_specs=tuple([VMEM_SPEC] * (3 * n)),
        compiler_params=_cparams(),
    )(*ws, *gs, *ms, *vs)
    return outs[:n], outs[n:2 * n], outs[2 * n:]


def _block_diag(w, heads_per_block):
    H, hd, _ = w.shape
    nb = H // heads_per_block
    eye = jnp.eye(heads_per_block, dtype=w.dtype)
    w4 = w.reshape(nb, heads_per_block, hd, hd)
    return jnp.einsum("nhab,hg->nhagb", w4, eye).reshape(nb, heads_per_block * hd, heads_per_block * hd)


def _diag_blocks(bd, heads_per_block, hd):
    nb = bd.shape[0]
    b5 = bd.reshape(nb, heads_per_block, hd, heads_per_block, hd)
    return jnp.stack([b5[:, h, :, h, :] for h in range(heads_per_block)], axis=1).reshape(nb * heads_per_block, hd, hd)


def _as_rows(a):
    if a.ndim == 1:
        return a.reshape(-1, LANES) if a.shape[0] % LANES == 0 else a.reshape(1, -1)
    if a.ndim == 3:
        return a.reshape(-1, LANES) if (a.size % LANES == 0) else a.reshape(a.shape[0] * a.shape[1], a.shape[2])
    return a


def kernel(x, g_mix, w_in, lru_conv_w, lru_conv_b, lru_wa, lru_ba, lru_wx, lru_bx, lru_lambda, lru_w_out, sc_conv_w, sc_w_out, w_o, g_ffn, ffn_w_up, ffn_conv_w, ffn_w_down, g_final, loss_target, m_g_mix, m_w_in, m_lru_conv_w, m_lru_conv_b, m_lru_wa, m_lru_ba, m_lru_wx, m_lru_bx, m_lru_lambda, m_lru_w_out, m_sc_conv_w, m_sc_w_out, m_w_o, m_g_ffn, m_ffn_w_up, m_ffn_conv_w, m_ffn_w_down, m_g_final, v_g_mix, v_w_in, v_lru_conv_w, v_lru_conv_b, v_lru_wa, v_lru_ba, v_lru_wx, v_lru_bx, v_lru_lambda, v_lru_w_out, v_sc_conv_w, v_sc_w_out, v_w_o, v_g_ffn, v_ffn_w_up, v_ffn_conv_w, v_ffn_w_down, v_g_final):
    T, D = x.shape[1], x.shape[2]
    d_lru = lru_lambda.shape[0]
    d_sc = sc_conv_w.shape[1] * N_DEV
    F = ffn_w_down.shape[0] * N_DEV
    H = lru_wa.shape[0]
    assert d_lru == d_sc and H * HEAD_DIM == d_lru
    xs = x.reshape(T, D)
    tgt = loss_target.reshape(T, D)
    my_x, my_y, my_c = _place()
    me = 4 * my_x + 2 * my_y + my_c

    big = [w_in, lru_w_out, sc_w_out, w_o, ffn_w_up, ffn_w_down]
    big_names = ["w_in", "lru_w_out", "sc_w_out", "w_o", "ffn_w_up", "ffn_w_down"]
    place = jnp.stack([my_c, 2 * my_x + my_y, me]).astype(jnp.int32)
    n_in, n_up = w_in.shape[1], ffn_w_up.shape[1]
    paired = [n_in, None, None, None, n_up, None]
    big_bf = [_cast_into_slot(w, place, "cast_" + nm, paired=pn is not None)
              for w, nm, pn in zip(big, big_names, paired)]
    pad_rows = lambda a: jnp.pad(a, ((0, SUB - a.shape[0]), (0, 0)))
    gathered = _all_gather(big_bf + [pad_rows(lru_conv_w), pad_rows(sc_conv_w), pad_rows(ffn_conv_w)],
                           [True] * 6 + [False] * 3,
                           [True, False, False, False, False, False, True, True, True],
                           paired + [None] * 3, "all_gather_first")
    W_in, W_lo, W_so, W_o8, W_up, W_dn8 = gathered[:6]
    full_cols = lambda g, kw: g[:, :kw, :].transpose(1, 0, 2).reshape(kw, -1)
    cw_lru = full_cols(gathered[6], lru_conv_w.shape[0])
    cw_sc = full_cols(gathered[7], sc_conv_w.shape[0])
    cw_ffn = full_cols(gathered[8], ffn_conv_w.shape[0])

    C = _tile(d_lru, C_LRU)
    hpb = C // HEAD_DIM
    wa_bd = _block_diag(lru_wa, hpb).astype(BF16)
    wx_bd = _block_diag(lru_wx, hpb).astype(BF16)
    cb, ba, bx, lam = (a.reshape(1, d_lru) for a in (lru_conv_b, lru_ba, lru_bx, lru_lambda))

    h1 = _rms_fwd(xs, g_mix, "rms_mix")
    k8 = W_up.shape[1] // 8
    wide = 2 * max(n_in, n_up)
    p, ((W_o8,), (W_lo,), (W_so,), (W_up,)) = _mm_nn(
        h1, W_in, out_dtype=BF16, name="mm_in", tn=wide,
        tasks=[_gather_task(W_o8, ici=ALL_ROWS), _gather_task(W_lo, ici=ALL_ROWS), _gather_task(W_so, ici=ALL_ROWS),
               _gather_task(W_up, ici=(0, 4 * k8), pair_n=n_up)])
    hs, yl_pre, ((W_o8,), (W_lo,), (W_so,), (W_up,)) = _lru_fwd(
        p, cw_lru, cb, wa_bd, ba, wx_bd, bx, lam, name="lru_fwd",
        tasks=[_gather_task(W_o8, fwd=ALL_ROWS), _gather_task(W_lo, fwd=ALL_ROWS), _gather_task(W_so, fwd=ALL_ROWS),
               _gather_task(W_up, ici=(4 * k8, 3 * k8), fwd=(0, 4 * k8), pair_n=n_up)])
    ys_pre = _sc_fwd(p, cw_sc, d=d_sc, name="sc_fwd")
    y_lru, ((W_up,),) = _mm_small(
        "nn", yl_pre, None, W_lo, name="mm_lru_out",
        tasks=[_gather_task(W_up, ici=(7 * k8, k8), fwd=(4 * k8, 3 * k8), pair_n=n_up)])
    y_sc, ((W_up,),) = _mm_small("nn", ys_pre, None, W_so, name="mm_sc_out",
                                 tasks=[_gather_task(W_up, fwd=(7 * k8, k8), pair_n=n_up)])
    gate0 = 2 * d_lru + 3 * d_sc
    merged = _merge_fwd(p, y_lru, y_sc, col0=gate0, name="merge_fwd")
    W_o = W_o8.reshape(1, D, D)
    x1 = _mm_nn(merged, W_o, out_dtype=F32, residual=xs, name="mm_o")
    h2 = _rms_fwd(x1, g_ffn, "rms_ffn")
    uu, ((W_dn8,),) = _mm_nn(h2, W_up, out_dtype=BF16, name="mm_up", tn=wide,
                             tasks=[_gather_task(W_dn8, ici=ALL_ROWS)])
    act, ((W_dn8,),) = _ffn_act_fwd(uu, cw_ffn, name="ffn_act_fwd", tasks=[_gather_task(W_dn8, fwd=ALL_ROWS)])
    W_dn = W_dn8.reshape(1, F, D)
    x2 = _mm_nn(act, W_dn, out_dtype=F32, residual=x1, name="mm_down", tn=1024, tk=F)
    dx2, dx2b, loss_part, dg_final = _loss_head(x2, g_final, tgt, "loss_head")

    def pack_rows(arrs):
        flat = jnp.concatenate([a.reshape(-1) for a in arrs])
        rows = -(-flat.shape[0] // (SUB * LANES)) * SUB
        return jnp.pad(flat, (0, rows * LANES - flat.shape[0])).reshape(rows, LANES)

    def unpack_rows(pack, arrs):
        flat, out, o = pack.reshape(-1), [], 0
        for a in arrs:
            out.append(flat[o:o + a.size].reshape(a.shape))
            o += a.size
        return out

    dact = _mm_nt(dx2b, W_dn, out_dtype=BF16, name="mm_down_dx", tm=512, tko=F // 2, tn=D)
    gW_dn = _mm_tn(act, dx2b, 1, out_dtype=BF16, name="mm_down_dw", tk=1408, tt=2048).reshape(N_DEV, F // N_DEV, D)
    duu, dcw_ffn_g, dcw_ffn_v = _ffn_act_bwd(uu, dact, cw_ffn, name="ffn_act_bwd")
    dh2, ((land_dn,),) = _mm_nt(duu, W_up, out_dtype=BF16, name="mm_up_dx", tn=wide, tasks=[_swap_task(gW_dn)])
    parts_dn = _add_halves(gW_dn, land_dn, place, "rs_add_ffn_w_down")
    gW_up, ((mine_dn,),) = _mm_tn(h2, duu, N_CHIP, out_dtype=BF16, name="mm_up_dw", tk=512, tn=wide, tt=2048,
                                  tasks=[_exchange_task(*parts_dn)])
    dx1, dx1b, dg_ffn = _rms_bwd(x1, g_ffn, dh2, dx2, "rms_ffn_bwd")
    dmerged, ((land_up,),) = _mm_nt(dx1b, W_o, out_dtype=BF16, name="mm_o_dx", tn=D,
                                    tasks=[_swap_task(gW_up, pair_n=n_up)])
    parts_up, land_up = _add_halves(gW_up, land_up, place, "rs_add_ffn_w_up", pair_n=n_up)
    r8 = parts_up.shape[1] // 8
    gW_o, ((land_up,),) = _mm_tn(merged, dx1b, 1, out_dtype=BF16, name="mm_o_dw", tt=2048,
                                 tasks=[_exchange_task(parts_up, land_up, rows=(0, r8))])
    gW_o = gW_o.reshape(N_DEV, D // N_DEV, D)
    dp, dy_lru, dy_sc, ((land_up,),) = _merge_bwd(
        p, y_lru, y_sc, dmerged, col0=gate0, name="merge_bwd",
        tasks=[_exchange_task(parts_up, land_up, rows=(r8, 2 * r8))])
    dyl_pre, ((land_o,),) = _mm_small("nt", None, dy_lru, W_lo, name="mm_lru_out_dx", tasks=[_swap_task(gW_o)])
    parts_o = _add_halves(gW_o, land_o, place, "rs_add_w_o")
    gW_lo = _mm_small("tn", yl_pre, dy_lru, W_lo, name="mm_lru_out_dw")
    dys_pre, ((land_lo,),) = _mm_small("nt", None, dy_sc, W_so, name="mm_sc_out_dx", tasks=[_swap_task(gW_lo)])
    parts_lo = _add_halves(gW_lo, land_lo, place, "rs_add_lru_w_out")
    gW_so = _mm_small("tn", ys_pre, dy_sc, W_so, name="mm_sc_out_dw")
    dp, dcw_sc, ((land_up,),) = _sc_bwd(p, dys_pre, dp, cw_sc, d=d_sc, name="sc_bwd",
                                        tasks=[_exchange_task(parts_up, land_up, rows=(3 * r8, r8))])
    dp, dcw_lru, dcb, dwa_bd, dba, dwx_bd, dbx, dlam, ((land_up,), (mine_o,), (mine_lo,), (land_so,)) = _lru_bwd(
        p, hs, dyl_pre, dp, cw_lru, cb, wa_bd, ba, wx_bd, bx, lam, name="lru_bwd",
        tasks=[_exchange_task(parts_up, land_up, rows=(4 * r8, 2 * r8)), _exchange_task(*parts_o),
               _exchange_task(*parts_lo), _swap_task(gW_so)])
    parts_so = _add_halves(gW_so, land_so, place, "rs_add_sc_w_out")

    dwa = _diag_blocks(dwa_bd, hpb, HEAD_DIM)
    dwx = _diag_blocks(dwx_bd, hpb, HEAD_DIM)
    dcw_ffn = jnp.concatenate([dcw_ffn_g, dcw_ffn_v], axis=1)
    rep_grads = [dcb, dwa, dba, dwx, dbx, dlam, dg_ffn, dg_final]
    small_full = rep_grads + [dcw_lru, dcw_sc, dcw_ffn]
    gW_in, ((mine_up,), (mine_so,), (packs,)) = _mm_tn(
        h1, dp, N_CHIP, out_dtype=BF16, name="mm_in_dw", tk=512, tn=wide, tt=2048,
        tasks=[_exchange_task(parts_up, land_up, rows=(6 * r8, 2 * r8)), _exchange_task(*parts_so),
               _bcast_task(pack_rows(small_full))])
    land_in = _swap_halves(gW_in, "rs_swap_w_in", pair_n=n_in)
    parts_in = _add_halves(gW_in, land_in, place, "rs_add_w_in", pair_n=n_in)
    dh1, ((mine_in,),) = _mm_nt(dp, W_in, out_dtype=BF16, name="mm_in_dx", tn=wide,
                                tasks=[_exchange_task(*parts_in)])
    grad_x, dg_mix = _rms_bwd(xs, g_mix, dh1, dx1, "rms_mix_bwd", want_bf16=False)

    mine = [mine_in, mine_lo, mine_so, mine_o, mine_up, mine_dn]
    big_m = [m_w_in, m_lru_w_out, m_sc_w_out, m_w_o, m_ffn_w_up, m_ffn_w_down]
    big_v = [v_w_in, v_lru_w_out, v_sc_w_out, v_w_o, v_ffn_w_up, v_ffn_w_down]
    big_out = {nm: _adamw_big(pt, w, m, v, "adamw_" + nm)
               for nm, pt, w, m, v in zip(big_names, mine, big, big_m, big_v)}

    (scb, swa, sba, swx, sbx, slam, sg_ffn, sg_final, scw_lru, scw_sc, scw_ffn) = unpack_rows(
        _sum_packs(packs, "sum_small"), small_full)
    (sg_mix,) = unpack_rows(_all_reduce_small(pack_rows([dg_mix]), "all_reduce_g_mix"), [dg_mix])

    def my_cols(a):
        n = a.shape[1] // N_DEV
        return lax.dynamic_slice_in_dim(a, me * n, n, axis=1)

    small_names = ["g_mix", "lru_conv_w", "lru_conv_b", "lru_wa", "lru_ba", "lru_wx", "lru_bx", "lru_lambda",
                   "sc_conv_w", "g_ffn", "ffn_conv_w", "g_final"]
    small_w = [g_mix, lru_conv_w, lru_conv_b, lru_wa, lru_ba, lru_wx, lru_bx, lru_lambda, sc_conv_w, g_ffn,
               ffn_conv_w, g_final]
    small_m = [m_g_mix, m_lru_conv_w, m_lru_conv_b, m_lru_wa, m_lru_ba, m_lru_wx, m_lru_bx, m_lru_lambda,
               m_sc_conv_w, m_g_ffn, m_ffn_conv_w, m_g_final]
    small_v = [v_g_mix, v_lru_conv_w, v_lru_conv_b, v_lru_wa, v_lru_ba, v_lru_wx, v_lru_bx, v_lru_lambda,
               v_sc_conv_w, v_g_ffn, v_ffn_conv_w, v_g_final]
    small_g = [sg_mix.reshape(D), my_cols(scw_lru), scb.reshape(d_lru), swa, sba.reshape(d_lru), swx,
               sbx.reshape(d_lru), slam.reshape(d_lru), my_cols(scw_sc), sg_ffn.reshape(D), my_cols(scw_ffn),
               sg_final.reshape(D)]
    sd, snm, snv = _adamw_small([_as_rows(a) for a in small_w], [_as_rows(a) for a in small_g],
                                [_as_rows(a) for a in small_m], [_as_rows(a) for a in small_v], "adamw_small")
    small_out = {nm: (g, d.reshape(w.shape), nm_.reshape(w.shape), nv_.reshape(w.shape))
                 for nm, w, g, d, nm_, nv_ in zip(small_names, small_w, small_g, sd, snm, snv)}

    loss = lax.psum(loss_part[0, 0], AXES)
    order = ["g_mix", "w_in", "lru_conv_w", "lru_conv_b", "lru_wa", "lru_ba", "lru_wx", "lru_bx", "lru_lambda",
             "lru_w_out", "sc_conv_w", "sc_w_out", "w_o", "g_ffn", "ffn_w_up", "ffn_conv_w", "ffn_w_down", "g_final"]
    res = {**big_out, **small_out}
    return (loss, grad_x.reshape(x.shape),
            *[res[nm][0] for nm in order], *[res[nm][1] for nm in order],
            *[res[nm][2] for nm in order], *[res[nm][3] for nm in order])
```

```python
import functools
import math

import jax
import jax.numpy as jnp
from jax import lax
from jax.experimental import pallas as pl
from jax.experimental.pallas import tpu as pltpu

F32, BF16 = jnp.float32, jnp.bfloat16
MESH = pl.DeviceIdType.MESH
N_DEV = 8
N_CHIP = 4
AXES = ("x", "y", "c")

EPS = 1e-6
LRU_C = 8.0
HEAD_DIM = 64
ADAM_LR, ADAM_B1, ADAM_B2, ADAM_EPS, ADAM_WD, ADAM_STEP = 0.001, 0.9, 0.999, 1e-08, 0.01, 10

VMEM_LIMIT = 48 * 1024 * 1024
LANES = 128
SUB = 8
HALO = 16
TB = 512
TB_CHUNKED = 1024
C_LRU = 256
C_EW = 512
TM, TN, TK = 512, 1536, 2048


def _tile(n, pref, align=LANES):
    best = None
    for d in range(align, min(n, pref) + 1, align):
        if n % d == 0:
            best = d
    return best or n


def _cparams(sem=None, vmem=VMEM_LIMIT):
    kw = dict(vmem_limit_bytes=vmem)
    if sem is not None:
        kw["dimension_semantics"] = sem
    return pltpu.CompilerParams(**kw)


def _S(shape, dtype):
    return jax.ShapeDtypeStruct(shape, dtype)


ANY = pl.BlockSpec(memory_space=pl.ANY)
VMEM_SPEC = pl.BlockSpec(memory_space=pltpu.VMEM)


class _Task:
    def __init__(self, arrays, aliased, start, wait, fresh=(), nsem=3, mid=None):
        self.arrays, self.aliased, self.start, self.wait = arrays, aliased, start, wait
        self.fresh, self.nsem, self.mid = list(fresh), nsem, mid


def _call(name, grid, compute, in_specs, args, out_shape, out_specs, scratch, tasks=(), own_aliases=None):
    n_in, n_out, n_scr = len(args), len(out_shape), len(scratch)
    x_in, x_out, aliases, where = [], [], dict(own_aliases or {}), []
    for t in tasks:
        places = []
        for k, arr in enumerate(t.arrays):
            if k in t.aliased:
                aliases[n_in + len(x_in)] = n_out + len(x_out)
                places.append(("out", len(x_out)))
                x_out.append(_S(arr.shape, arr.dtype))
            else:
                places.append(("in", len(x_in)))
            x_in.append(arr)
        for shp in t.fresh:
            places.append(("out", len(x_out)))
            x_out.append(shp)
        where.append(places)
    n_xi, n_xo = len(x_in), len(x_out)

    def body(*refs):
        ins, xi = refs[:n_in], refs[n_in:n_in + n_xi]
        o0 = n_in + n_xi
        outs, xo = refs[o0:o0 + n_out], refs[o0 + n_out:o0 + n_out + n_xo]
        s0 = o0 + n_out + n_xo
        scr, sems = refs[s0:s0 + n_scr], refs[s0 + n_scr:]
        ids = [pl.program_id(a) for a in range(len(grid))]

        def task_refs(ti):
            return [xo[i] if kind == "out" else xi[i] for kind, i in where[ti]]

        if tasks:
            first = functools.reduce(jnp.logical_and, [i == 0 for i in ids])

            @pl.when(first)
            def _():
                for ti, t in enumerate(tasks):
                    t.start(task_refs(ti), *sems[3 * ti:3 * ti + 3])

        compute(*ins, *outs, *scr)
        if any(t.mid is not None for t in tasks):
            n_steps = math.prod(grid)
            step = functools.reduce(lambda s, ig: s * ig[1] + ig[0], zip(ids, grid), 0)

            @pl.when(step == (5 * n_steps) // 8)
            def _():
                for ti, t in enumerate(tasks):
                    if t.mid is not None:
                        t.mid(task_refs(ti), *sems[3 * ti:3 * ti + 3])

        if tasks:
            last = functools.reduce(jnp.logical_and, [i == g - 1 for i, g in zip(ids, grid)])

            @pl.when(last)
            def _():
                for ti, t in enumerate(tasks):
                    t.wait(task_refs(ti), *sems[3 * ti:3 * ti + 3])

    sem_shapes = []
    for t in tasks:
        sem_shapes += [pltpu.SemaphoreType.DMA((t.nsem,)), pltpu.SemaphoreType.DMA((t.nsem,)),
                       pltpu.SemaphoreType.DMA((1,))]
    res = pl.pallas_call(
        body, name=name, grid=grid,
        in_specs=list(in_specs) + [ANY] * n_xi,
        out_specs=tuple(out_specs) + (ANY,) * n_xo,
        out_shape=tuple(out_shape) + tuple(x_out),
        scratch_shapes=list(scratch) + sem_shapes,
        input_output_aliases=aliases,
        compiler_params=_cparams(("arbitrary",) * len(grid)),
    )(*args, *x_in)
    outs, passed, o = res[:n_out], [], n_out
    for places in where:
        k = sum(1 for kind, _ in places if kind == "out")
        passed.append(list(res[o:o + k]))
        o += k
    return outs, passed


def _mm_nn(a, w3, *, out_dtype, name, residual=None, tm=TM, tn=TN, tk=TK, tasks=()):
    M, K = a.shape
    G, _, n = w3.shape
    tm, tn, tk = _tile(M, tm, SUB), _tile(n, tn), _tile(K, tk)
    nj, nk = n // tn, K // tk

    def compute(*refs):
        if residual is None:
            a_ref, w_ref, o_ref = refs[:3]
            r_ref = None
        else:
            a_ref, w_ref, r_ref, o_ref = refs[:4]

        def finish(r):
            if r_ref is not None:
                r = r + r_ref[...]
            o_ref[...] = r.astype(o_ref.dtype)

        if nk == 1:
            finish(jnp.dot(a_ref[...], w_ref[...], preferred_element_type=F32))
            return
        acc = refs[-1]
        k = pl.program_id(3)

        @pl.when(k == 0)
        def _():
            acc[...] = jnp.zeros_like(acc)

        acc[...] += jnp.dot(a_ref[...], w_ref[...], preferred_element_type=F32)

        @pl.when(k == nk - 1)
        def _():
            finish(acc[...])

    in_specs = [pl.BlockSpec((tm, tk), lambda g, j, i, k: (i, k)),
                pl.BlockSpec((None, tk, tn), lambda g, j, i, k: (g, k, j))]
    args = [a, w3]
    if residual is not None:
        in_specs.append(pl.BlockSpec((tm, tn), lambda g, j, i, k: (i, g * nj + j)))
        args.append(residual)
    outs, passed = _call(
        name, (G, nj, M // tm, nk), compute, in_specs, args, [_S((M, G * n), out_dtype)],
        [pl.BlockSpec((tm, tn), lambda g, j, i, k: (i, g * nj + j))],
        [] if nk == 1 else [pltpu.VMEM((tm, tn), F32)], tasks)
    return (outs[0], passed) if tasks else outs[0]


def _mm_nt(dy, w3, *, out_dtype, name, tm=1024, tko=1024, tn=TN, tasks=()):
    M, _ = dy.shape
    G, K, n = w3.shape
    tm, tko, tn = _tile(M, tm, SUB), _tile(K, tko), _tile(n, tn)
    nj = n // tn
    nr = G * nj

    def compute(dy_ref, w_ref, o_ref, *scr):
        part = lax.dot_general(dy_ref[...], w_ref[...], (((1,), (1,)), ((), ())), preferred_element_type=F32)
        if nr == 1:
            o_ref[...] = part.astype(o_ref.dtype)
            return
        (acc,) = scr
        r = pl.program_id(2)

        @pl.when(r == 0)
        def _():
            acc[...] = jnp.zeros_like(acc)

        acc[...] += part

        @pl.when(r == nr - 1)
        def _():
            o_ref[...] = acc[...].astype(o_ref.dtype)

    outs, passed = _call(
        name, (K // tko, M // tm, nr), compute,
        [pl.BlockSpec((tm, tn), lambda ko, i, r: (i, r)),
         pl.BlockSpec((None, tko, tn), lambda ko, i, r: (r // nj, ko, r % nj))],
        [dy, w3], [_S((M, K), out_dtype)], [pl.BlockSpec((tm, tko), lambda ko, i, r: (i, ko))],
        [] if nr == 1 else [pltpu.VMEM((tm, tko), F32)], tasks)
    return (outs[0], passed) if tasks else outs[0]


def _mm_tn(a, dy, G, *, out_dtype, name, tk=1024, tn=TN, tt=1024, tasks=()):
    M, K = a.shape
    n = dy.shape[1] // G
    tk, tn, tt = _tile(K, tk), _tile(n, tn), _tile(M, tt, SUB)
    nj, nt = n // tn, M // tt

    def compute(a_ref, dy_ref, o_ref, acc):
        t = pl.program_id(3)

        @pl.when(t == 0)
        def _():
            acc[...] = jnp.zeros_like(acc)

        acc[...] += lax.dot_general(a_ref[...], dy_ref[...], (((0,), (0,)), ((), ())),
                                    preferred_element_type=F32)

        @pl.when(t == nt - 1)
        def _():
            o_ref[...] = acc[...].astype(o_ref.dtype)

    outs, passed = _call(
        name, (G, nj, K // tk, nt), compute,
        [pl.BlockSpec((tt, tk), lambda g, j, k, t: (t, k)),
         pl.BlockSpec((tt, tn), lambda g, j, k, t: (t, g * nj + j))],
        [a, dy], [_S((G, K, n), out_dtype)], [pl.BlockSpec((None, tk, tn), lambda g, j, k, t: (g, k, j))],
        [pltpu.VMEM((tk, tn), F32)], tasks)
    return (outs[0], passed) if tasks else outs[0]


def _mm_small(kind, a, b, w3, *, name, tm=1024, tasks=()):
    G, K, n = w3.shape
    M = (a if a is not None else b).shape[0]
    tm = _tile(M, tm, HALO)
    nt = M // tm
    w_spec = pl.BlockSpec((G, K, n), lambda i: (0, 0, 0))
    a_spec = pl.BlockSpec((tm, K), lambda i: (i, 0))
    b_spec = pl.BlockSpec((tm, G * n), lambda i: (i, 0))
    cols = lambda g: slice(g * n, (g + 1) * n)
    if kind == "nn":
        def compute(a_ref, w_ref, o_ref):
            av = a_ref[...]
            for g in range(G):
                o_ref[:, cols(g)] = jnp.dot(av, w_ref[g], preferred_element_type=F32).astype(o_ref.dtype)

        outs, passed = _call(name, (nt,), compute, [a_spec, w_spec], [a, w3], [_S((M, G * n), BF16)], [b_spec], [], tasks)
    elif kind == "nt":
        def compute(b_ref, w_ref, o_ref):
            acc = None
            for g in range(G):
                part = lax.dot_general(b_ref[:, cols(g)], w_ref[g], (((1,), (1,)), ((), ())),
                                       preferred_element_type=F32)
                acc = part if acc is None else acc + part
            o_ref[...] = acc.astype(o_ref.dtype)

        outs, passed = _call(name, (nt,), compute, [b_spec, w_spec], [b, w3], [_S((M, K), BF16)], [a_spec], [], tasks)
    else:
        def compute(a_ref, b_ref, o_ref, acc):
            i = pl.program_id(0)

            @pl.when(i == 0)
            def _():
                acc[...] = jnp.zeros_like(acc)

            at = a_ref[...].T
            for g in range(G):
                acc[g] += jnp.dot(at, b_ref[:, cols(g)], preferred_element_type=F32)

            @pl.when(i == nt - 1)
            def _():
                o_ref[...] = acc[...].astype(o_ref.dtype)

        outs, passed = _call(name, (nt,), compute, [a_spec, b_spec], [a, b], [_S((G, K, n), BF16)], [w_spec],
                             [pltpu.VMEM((G, K, n), F32)], tasks)
    return (outs[0], passed) if tasks else outs[0]


def _cast_into_slot(w, place, name, paired=False):
    R, C = w.shape
    tr = _tile(R, 512, HALO)

    def body(s_ref, w_ref, o_ref):
        del s_ref
        o_ref[...] = w_ref[...].astype(BF16)

    if paired:
        shape, out_map = (N_CHIP, R, 2 * C), lambda i, s: (s[1], i, s[0])
    else:
        shape, out_map = (N_DEV, R, C), lambda i, s: (s[2], i, 0)
    return pl.pallas_call(
        body, name=name, out_shape=_S(shape, BF16),
        grid_spec=pltpu.PrefetchScalarGridSpec(
            num_scalar_prefetch=1, grid=(R // tr,),
            in_specs=[pl.BlockSpec((tr, C), lambda i, s: (i, 0))],
            out_specs=pl.BlockSpec((None, tr, C), out_map)),
        compiler_params=_cparams(("parallel",)),
    )(place, w)


def _down(cur, prev8, j):
    return pltpu.roll(jnp.concatenate([prev8, cur], axis=0), j, 0)[SUB:, :]


def _up(cur, next8, j):
    n = cur.shape[0] + SUB
    return pltpu.roll(jnp.concatenate([cur, next8], axis=0), n - j, 0)[:cur.shape[0], :]


def _shifted_down(x, prev8, n):
    full = jnp.concatenate([prev8, x], axis=0)
    return [x] + [pltpu.roll(full, s, 0)[SUB:, :] for s in range(1, n)]


def _shifted_up(x, next8, n):
    m = x.shape[0] + SUB
    full = jnp.concatenate([x, next8], axis=0)
    return [x] + [pltpu.roll(full, m - s, 0)[:x.shape[0], :] for s in range(1, n)]


def _taps(sh, w_ref):
    kw = w_ref.shape[0]
    y = sh[0] * w_ref[pl.ds(kw - 1, 1), :]
    for k in range(kw - 1):
        y = y + sh[kw - 1 - k] * w_ref[pl.ds(k, 1), :]
    return y


def _conv(x, prev8, w_ref):
    return _taps(_shifted_down(x, prev8, w_ref.shape[0]), w_ref)


def _conv_t(dy, next8, w_ref):
    return _taps(_shifted_up(dy, next8, w_ref.shape[0]), w_ref)


def _conv_dw(dw_ref, dy, x, prev8, first):
    kw = dw_ref.shape[0]

    @pl.when(first)
    def _():
        dw_ref[...] = jnp.zeros_like(dw_ref)

    for k in range(kw):
        xs = x if k == kw - 1 else _down(x, prev8, kw - 1 - k)
        dw_ref[pl.ds(k, 1), :] += jnp.sum(dy * xs, axis=0, keepdims=True)


def _acc(ref, val, first):
    @pl.when(first)
    def _():
        ref[...] = jnp.zeros_like(ref)

    ref[...] += val


def _acc_row(ref, val, first):
    _acc(ref, jnp.sum(val, axis=0, keepdims=True), first)


def _prev8(h_ref, t):
    return jnp.where(t > 0, h_ref[...].astype(F32)[HALO - SUB:, :], 0.0)


def _next8(h_ref, is_last):
    return jnp.where(is_last, 0.0, h_ref[...].astype(F32)[:SUB, :])


_GELU_K0 = math.sqrt(2.0 / math.pi)
_GELU_K1 = 0.044715


def _gelu_and_grad(x):
    x2 = x * x
    th = jnp.tanh(_GELU_K0 * x * (1.0 + _GELU_K1 * x2))
    g = 0.5 * x * (1.0 + th)
    dg = 0.5 * (1.0 + th) + 0.5 * x * (1.0 - th * th) * (_GELU_K0 * (1.0 + 3.0 * _GELU_K1 * x2))
    return g, dg


def _neg_expm1(z):
    series = -z * (1.0 + z * (0.5 + z * (1.0 / 6.0 + z * (1.0 / 24.0))))
    return jnp.where(z > -0.03, series, 1.0 - jnp.exp(z))


def _store_staged(stages, dst_hbm, sems, step, n_steps, where):
    def copies(s, slot):
        return [pltpu.make_async_copy(
            st.at[slot], dst_hbm.at[pl.ds(r0, st.shape[1]), pl.ds(c0, st.shape[2])], sems.at[slot, k])
            for k, (st, (r0, c0)) in enumerate(zip(stages, where(s)))]

    slot = step % 2

    @pl.when(step > 0)
    def _():
        for cp in copies(step - 1, 1 - slot):
            cp.wait()

    for cp in copies(step, slot):
        cp.start()

    @pl.when(step == n_steps - 1)
    def _():
        for cp in copies(step, slot):
            cp.wait()


def _halo_prev_map(hb, col_fn):
    return lambda c, t: (jnp.maximum(t * hb - 1, 0), col_fn(c))


def _rms_fwd(x, g, name):
    T, D = x.shape
    tb = _tile(T, TB, SUB)

    def body(x_ref, g_ref, o_ref):
        xv = x_ref[...]
        rstd = lax.rsqrt(jnp.mean(xv * xv, axis=-1, keepdims=True) + EPS)
        o_ref[...] = (xv * rstd * g_ref[...]).astype(BF16)

    return pl.pallas_call(
        body, name=name, out_shape=_S((T, D), BF16), grid=(T // tb,),
        in_specs=[pl.BlockSpec((tb, D), lambda i: (i, 0)), pl.BlockSpec((1, D), lambda i: (0, 0))],
        out_specs=pl.BlockSpec((tb, D), lambda i: (i, 0)),
        compiler_params=_cparams(("parallel",)),
    )(x, g.reshape(1, D))


def _rms_bwd(x, g, dh, dres, name, want_bf16=True):
    T, D = x.shape
    tb = _tile(T, 256, SUB)

    def body(x_ref, g_ref, dh_ref, dr_ref, dx_ref, *rest):
        dg_ref = rest[-1]
        i = pl.program_id(0)
        xv = x_ref[...]
        rstd = lax.rsqrt(jnp.mean(xv * xv, axis=-1, keepdims=True) + EPS)
        xn = xv * rstd
        dhv = dh_ref[...].astype(F32)
        _acc_row(dg_ref, dhv * xn, i == 0)
        dxn = dhv * g_ref[...]
        dx = dr_ref[...] + rstd * (dxn - xn * jnp.mean(dxn * xn, axis=-1, keepdims=True))
        dx_ref[...] = dx
        if want_bf16:
            rest[0][...] = dx.astype(BF16)

    blk = pl.BlockSpec((tb, D), lambda i: (i, 0))
    vec = pl.BlockSpec((1, D), lambda i: (0, 0))
    extra = [(_S((T, D), BF16), blk)] if want_bf16 else []
    return pl.pallas_call(
        body, name=name, out_shape=(_S((T, D), F32), *[s for s, _ in extra], _S((1, D), F32)),
        grid=(T // tb,), in_specs=[blk, vec, blk, blk], out_specs=(blk, *[b for _, b in extra], vec),
        compiler_params=_cparams(("arbitrary",)),
    )(x, g.reshape(1, D), dh, dres)


def _loss_head(x2, g, target, name):
    T, D = x2.shape
    tb = _tile(T, 256, SUB)

    def body(x_ref, g_ref, t_ref, dx_ref, dxb_ref, loss_ref, dg_ref):
        i = pl.program_id(0)
        xv = x_ref[...]
        rstd = lax.rsqrt(jnp.mean(xv * xv, axis=-1, keepdims=True) + EPS)
        xn = xv * rstd
        err = xn * g_ref[...] - t_ref[...]
        part = 0.5 * jnp.sum(jnp.mean(err * err, axis=-1, keepdims=True), axis=0, keepdims=True)
        part = jnp.broadcast_to(part, (1, LANES))
        _acc(loss_ref, part, i == 0)
        dy = err * (1.0 / D)
        _acc_row(dg_ref, dy * xn, i == 0)
        dxn = dy * g_ref[...]
        dx = rstd * (dxn - xn * jnp.mean(dxn * xn, axis=-1, keepdims=True))
        dx_ref[...] = dx
        dxb_ref[...] = dx.astype(BF16)

    blk = pl.BlockSpec((tb, D), lambda i: (i, 0))
    vec = pl.BlockSpec((1, D), lambda i: (0, 0))
    return pl.pallas_call(
        body, name=name,
        out_shape=(_S((T, D), F32), _S((T, D), BF16), _S((1, LANES), F32), _S((1, D), F32)),
        grid=(T // tb,), in_specs=[blk, vec, blk],
        out_specs=(blk, blk, pl.BlockSpec((1, LANES), lambda i: (0, 0)), vec),
        compiler_params=_cparams(("arbitrary",)),
    )(x2, g.reshape(1, D), target)


def _lru_gates(xc, wa_ref, ba_ref, wx_ref, bx_ref, lam_ref):
    xcb = xc.astype(BF16)
    r = jax.nn.sigmoid(jnp.dot(xcb, wa_ref[...], preferred_element_type=F32) + ba_ref[...])
    i = jax.nn.sigmoid(jnp.dot(xcb, wx_ref[...], preferred_element_type=F32) + bx_ref[...])
    sp = jax.nn.softplus(-lam_ref[...])
    log_a = (-LRU_C * sp) * r
    a = jnp.exp(log_a)
    s = jnp.sqrt(_neg_expm1(2.0 * log_a))
    return xcb, r, i, a, s


def _lru_fwd(p, conv_w, conv_b, wa_bd, ba, wx_bd, bx, lam, *, name, tasks=()):
    T = p.shape[0]
    d = lam.shape[-1]
    C = _tile(d, C_LRU)
    nC = d // C
    tb = _tile(T, TB, HALO)
    nT, hb, nt = T // tb, tb // HALO, tb // SUB

    def body(x_ref, xh_ref, g_ref, cw_ref, cb_ref, wa_ref, ba_ref, wx_ref, bx_ref, lam_ref,
             hs_ref, y_ref, a_s, u_s, h_s):
        t = pl.program_id(1)

        @pl.when(t == 0)
        def _():
            h_s[...] = jnp.zeros_like(h_s)

        x = x_ref[...].astype(F32)
        xc = _conv(x, _prev8(xh_ref, t), cw_ref) + cb_ref[...]
        _, r, i, a, s = _lru_gates(xc, wa_ref, ba_ref, wx_ref, bx_ref, lam_ref)
        a_s[...] = a
        u_s[...] = s * (i * xc)
        row = lax.broadcasted_iota(jnp.int32, (SUB, C), 0)

        def step(k, h):
            o = pl.multiple_of(k * SUB, SUB)
            A = a_s[pl.ds(o, SUB), :]
            B = u_s[pl.ds(o, SUB), :]
            for sh in (1, 2, 4):
                m = row >= sh
                Ap = pltpu.roll(A, sh, 0)
                Bp = pltpu.roll(B, sh, 0)
                B = jnp.where(m, A * Bp + B, B)
                A = jnp.where(m, A * Ap, A)
            hs = A * h + B
            hs_ref[pl.ds(o, SUB), :] = hs
            return jnp.broadcast_to(hs[SUB - 1:SUB, :], (SUB, C))

        h_s[...] = lax.fori_loop(0, nt, step, h_s[...])
        gel, _ = _gelu_and_grad(g_ref[...].astype(F32))
        y_ref[...] = (gel * hs_ref[...]).astype(BF16)

    vec = pl.BlockSpec((1, C), lambda c, t: (0, c))
    sq = pl.BlockSpec((None, C, C), lambda c, t: (c, 0, 0))
    outs, passed = _call(
        name, (nC, nT), body,
        [pl.BlockSpec((tb, C), lambda c, t: (t, c)),
         pl.BlockSpec((HALO, C), _halo_prev_map(hb, lambda c: c)),
         pl.BlockSpec((tb, C), lambda c, t: (t, nC + c)),
         pl.BlockSpec((conv_w.shape[0], C), lambda c, t: (0, c)),
         vec, sq, vec, sq, vec, vec],
        [p, p, p, conv_w, conv_b, wa_bd, ba, wx_bd, bx, lam],
        [_S((T, d), F32), _S((T, d), BF16)],
        [pl.BlockSpec((tb, C), lambda c, t: (t, c)), pl.BlockSpec((tb, C), lambda c, t: (t, c))],
        [pltpu.VMEM((tb, C), F32), pltpu.VMEM((tb, C), F32), pltpu.VMEM((SUB, C), F32)], tasks)
    return (*outs, passed) if tasks else outs


def _lru_bwd(p, hs, dyl, dp, conv_w, conv_b, wa_bd, ba, wx_bd, bx, lam, *, name, tasks=()):
    T = p.shape[0]
    d = lam.shape[-1]
    C = _tile(d, C_LRU)
    nC = d // C
    tb = _tile(T, TB, HALO)
    nT, hb, nt = T // tb, tb // HALO, tb // SUB
    kw = conv_w.shape[0]

    def body(x_ref, xh_ref, g_ref, hs_ref, hh_ref, dy_ref, cw_ref, cb_ref, wa_ref, ba_ref, wx_ref, bx_ref,
             lam_ref, dp_in, dp_ref, dcw_ref, dcb_ref, dwa_ref, dba_ref, dwx_ref, dbx_ref, dlam_ref,
             b_s, g_s, dh_s, an_s, dhn_s, dxn_s, st_x, st_g, sems):
        del dp_in
        c = pl.program_id(0)
        tr = pl.program_id(1)
        t = nT - 1 - tr
        first = tr == 0

        @pl.when(first)
        def _():
            an_s[...] = jnp.zeros_like(an_s)
            dhn_s[...] = jnp.zeros_like(dhn_s)
            dxn_s[...] = jnp.zeros_like(dxn_s)

        x = x_ref[...].astype(F32)
        xprev = _prev8(xh_ref, t)
        xc = _conv(x, xprev, cw_ref) + cb_ref[...]
        xcb, r, i, a, s = _lru_gates(xc, wa_ref, ba_ref, wx_ref, bx_ref, lam_ref)
        hsv = hs_ref[...]
        dy = dy_ref[...].astype(F32)
        gel, dgel = _gelu_and_grad(g_ref[...].astype(F32))
        step_no = c * nT + tr
        slot = step_no % 2
        st_g[slot] = (dy * hsv * dgel).astype(BF16)

        b_s[...] = _up(a, an_s[...], 1)
        g_s[...] = dy * gel
        row = lax.broadcasted_iota(jnp.int32, (SUB, C), 0)

        def step(k, carry):
            o = pl.multiple_of((nt - 1 - k) * SUB, SUB)
            B = b_s[pl.ds(o, SUB), :]
            G = g_s[pl.ds(o, SUB), :]
            for sh in (1, 2, 4):
                m = row < SUB - sh
                Bn = pltpu.roll(B, SUB - sh, 0)
                Gn = pltpu.roll(G, SUB - sh, 0)
                G = jnp.where(m, B * Gn + G, G)
                B = jnp.where(m, B * Bn, B)
            dh = B * carry + G
            dh_s[pl.ds(o, SUB), :] = dh
            return jnp.broadcast_to(dh[0:1, :], (SUB, C))

        dhn_s[...] = lax.fori_loop(0, nt, step, dhn_s[...])
        an_s[...] = a[:SUB, :]
        dh = dh_s[...]

        hprev = _down(hsv, jnp.where(t > 0, hh_ref[...][HALO - SUB:, :], 0.0), 1)
        d_a = dh * hprev
        ixc = i * xc
        d_s = dh * ixc
        d_i = dh * s * xc
        d_xc = dh * s * i
        d_l = d_a * a - d_s * (a * a) / s
        sp = jax.nn.softplus(-lam_ref[...])
        _acc_row(dlam_ref, d_l * r * (LRU_C * jax.nn.sigmoid(-lam_ref[...])), first)
        d_zr = (d_l * (-LRU_C * sp)) * r * (1.0 - r)
        d_zi = d_i * i * (1.0 - i)
        _acc_row(dba_ref, d_zr, first)
        _acc_row(dbx_ref, d_zi, first)
        d_zrb = d_zr.astype(BF16)
        d_zib = d_zi.astype(BF16)
        tn_dims = (((0,), (0,)), ((), ()))
        nt_dims = (((1,), (1,)), ((), ()))
        gwa = lax.dot_general(xcb, d_zrb, tn_dims, preferred_element_type=F32)
        gwx = lax.dot_general(xcb, d_zib, tn_dims, preferred_element_type=F32)
        _acc(dwa_ref, gwa, first)
        _acc(dwx_ref, gwx, first)
        d_xc = (d_xc + lax.dot_general(d_zrb, wa_ref[...], nt_dims, preferred_element_type=F32)
                + lax.dot_general(d_zib, wx_ref[...], nt_dims, preferred_element_type=F32))
        _acc_row(dcb_ref, d_xc, first)
        _conv_dw(dcw_ref, d_xc, x, xprev, first)
        st_x[slot] = _conv_t(d_xc, dxn_s[...], cw_ref).astype(BF16)
        dxn_s[...] = d_xc[:SUB, :]

        def where(s):
            row0, col0 = (nT - 1 - s % nT) * tb, (s // nT) * C
            return [(row0, col0), (row0, d + col0)]

        _store_staged([st_x, st_g], dp_ref, sems, step_no, nC * nT, where)

    rev = lambda c, tr: (nT - 1 - tr, c)
    vec = pl.BlockSpec((1, C), lambda c, tr: (0, c))
    sq = pl.BlockSpec((None, C, C), lambda c, tr: (c, 0, 0))
    cwb = pl.BlockSpec((kw, C), lambda c, tr: (0, c))
    halo_prev = lambda c, tr: (jnp.maximum((nT - 1 - tr) * hb - 1, 0), c)
    outs, passed = _call(
        name, (nC, nT), body,
        [pl.BlockSpec((tb, C), rev),
         pl.BlockSpec((HALO, C), halo_prev),
         pl.BlockSpec((tb, C), lambda c, tr: (nT - 1 - tr, nC + c)),
         pl.BlockSpec((tb, C), rev),
         pl.BlockSpec((HALO, C), halo_prev),
         pl.BlockSpec((tb, C), rev),
         cwb, vec, sq, vec, sq, vec, vec, ANY],
        [p, p, p, hs, hs, dyl, conv_w, conv_b, wa_bd, ba, wx_bd, bx, lam, dp],
        [_S(dp.shape, dp.dtype), _S((kw, d), F32), _S((1, d), F32), _S((nC, C, C), F32), _S((1, d), F32),
         _S((nC, C, C), F32), _S((1, d), F32), _S((1, d), F32)],
        [ANY, cwb, vec, sq, vec, sq, vec, vec],
        [pltpu.VMEM((tb, C), F32), pltpu.VMEM((tb, C), F32), pltpu.VMEM((tb, C), F32),
         pltpu.VMEM((SUB, C), F32), pltpu.VMEM((SUB, C), F32), pltpu.VMEM((SUB, C), F32),
         pltpu.VMEM((2, tb, C), BF16), pltpu.VMEM((2, tb, C), BF16), pltpu.SemaphoreType.DMA((2, 2))],
        tasks, own_aliases={13: 0})
    return (*outs, passed) if tasks else outs


def _sc_fwd(p, conv_w, *, d, name):
    T = p.shape[0]
    C = _tile(d, C_EW)
    nC = d // C
    tb = _tile(T, TB, HALO)
    nT, hb = T // tb, tb // HALO

    def body(b_ref, c_ref, ch_ref, v_ref, vh_ref, w_ref, y_ref):
        t = pl.program_id(1)
        cv = c_ref[...].astype(F32) * v_ref[...].astype(F32)
        cvp = _prev8(ch_ref, t) * _prev8(vh_ref, t)
        y_ref[...] = (b_ref[...].astype(F32) * _conv(cv, cvp, w_ref)).astype(BF16)

    seg = lambda k: pl.BlockSpec((tb, C), lambda c, t: (t, k * nC + c))
    hseg = lambda k: pl.BlockSpec((HALO, C), _halo_prev_map(hb, lambda c: k * nC + c))
    return pl.pallas_call(
        body, name=name, out_shape=_S((T, d), BF16), grid=(nC, nT),
        in_specs=[seg(2), seg(3), hseg(3), seg(4), hseg(4), pl.BlockSpec((conv_w.shape[0], C), lambda c, t: (0, c))],
        out_specs=pl.BlockSpec((tb, C), lambda c, t: (t, c)),
        compiler_params=_cparams(("parallel", "parallel")),
    )(p, p, p, p, p, conv_w)


def _sc_bwd(p, dys, dp, conv_w, *, d, name, tasks=()):
    T = p.shape[0]
    C = _tile(d, C_EW)
    nC = d // C
    tb = _tile(T, TB, HALO)
    nT, hb = T // tb, tb // HALO
    kw = conv_w.shape[0]

    def body(b_ref, bn_ref, c_ref, ch_ref, v_ref, vh_ref, dy_ref, dyn_ref, w_ref, dp_in, dp_ref, dw_ref,
             st_b, st_c, st_v, sems):
        del dp_in
        c = pl.program_id(0)
        t = pl.program_id(1)
        last = t == nT - 1
        bv = b_ref[...].astype(F32)
        cvv = c_ref[...].astype(F32)
        vv = v_ref[...].astype(F32)
        dy = dy_ref[...].astype(F32)
        cv = cvv * vv
        cvp = _prev8(ch_ref, t) * _prev8(vh_ref, t)
        step_no = c * nT + t
        slot = step_no % 2
        st_b[slot] = (dy * _conv(cv, cvp, w_ref)).astype(BF16)
        dz = dy * bv
        dzn = _next8(dyn_ref, last) * _next8(bn_ref, last)
        _conv_dw(dw_ref, dz, cv, cvp, t == 0)
        dcv = _conv_t(dz, dzn, w_ref)
        st_c[slot] = (dcv * vv).astype(BF16)
        st_v[slot] = (dcv * cvv).astype(BF16)

        def where(s):
            return [((s % nT) * tb, (2 + k) * d + (s // nT) * C) for k in range(3)]

        _store_staged([st_b, st_c, st_v], dp_ref, sems, step_no, nC * nT, where)

    seg = lambda k: pl.BlockSpec((tb, C), lambda c, t: (t, k * nC + c))
    hseg = lambda k: pl.BlockSpec((HALO, C), _halo_prev_map(hb, lambda c: k * nC + c))
    last_h = T // HALO - 1
    nseg = lambda k: pl.BlockSpec((HALO, C), lambda c, t: (jnp.minimum((t + 1) * hb, last_h), k * nC + c))
    outs, passed = _call(
        name, (nC, nT), body,
        [seg(2), nseg(2), seg(3), hseg(3), seg(4), hseg(4),
         pl.BlockSpec((tb, C), lambda c, t: (t, c)), nseg(0),
         pl.BlockSpec((kw, C), lambda c, t: (0, c)), ANY],
        [p, p, p, p, p, p, dys, dys, conv_w, dp],
        [_S(dp.shape, dp.dtype), _S((kw, d), F32)], [ANY, pl.BlockSpec((kw, C), lambda c, t: (0, c))],
        [pltpu.VMEM((2, tb, C), BF16)] * 3 + [pltpu.SemaphoreType.DMA((2, 3))], tasks, own_aliases={9: 0})
    return (*outs, passed) if tasks else outs


def _merge_fwd(p, y_lru, y_sc, *, col0, name, tasks=()):
    T, D = y_lru.shape
    C = _tile(math.gcd(D, col0), 1024)
    nC = D // C
    k0 = col0 // C
    tb = _tile(T, TB, HALO)

    def body(gl_ref, gs_ref, yl_ref, ys_ref, o_ref):
        @pl.loop(0, tb // HALO)
        def _(k):
            rows = pl.ds(pl.multiple_of(k * HALO, HALO), HALO)
            for l0 in range(0, C, min(C, C_EW)):
                at = (rows, pl.ds(l0, min(C, C_EW)))
                o_ref[at] = (jax.nn.sigmoid(gl_ref[at].astype(F32)) * yl_ref[at].astype(F32)
                             + jax.nn.sigmoid(gs_ref[at].astype(F32)) * ys_ref[at].astype(F32)).astype(BF16)

    blk = pl.BlockSpec((tb, C), lambda c, t: (t, c))
    outs, passed = _call(
        name, (nC, T // tb), body,
        [pl.BlockSpec((tb, C), lambda c, t: (t, k0 + c)), pl.BlockSpec((tb, C), lambda c, t: (t, k0 + nC + c)),
         blk, blk], [p, p, y_lru, y_sc], [_S((T, D), BF16)], [blk], [], tasks)
    return (outs[0], passed) if tasks else outs[0]


def _merge_bwd(p, y_lru, y_sc, dm, *, col0, name, tasks=()):
    T, D = y_lru.shape
    C = _tile(math.gcd(D, col0), 1024)
    nC = D // C
    k0 = col0 // C
    tb = _tile(T, TB, HALO)
    nT = T // tb

    def body(gl_ref, gs_ref, yl_ref, ys_ref, dm_ref, dp_ref, dyl_ref, dys_ref, st_l, st_s, sems):
        step_no = pl.program_id(0) * nT + pl.program_id(1)
        slot = step_no % 2

        @pl.loop(0, tb // HALO)
        def _(k):
            rows = pl.ds(pl.multiple_of(k * HALO, HALO), HALO)
            for l0 in range(0, C, min(C, C_EW)):
                at = (rows, pl.ds(l0, min(C, C_EW)))
                dmv = dm_ref[at].astype(F32)
                sl = jax.nn.sigmoid(gl_ref[at].astype(F32))
                ss = jax.nn.sigmoid(gs_ref[at].astype(F32))
                dyl_ref[at] = (dmv * sl).astype(BF16)
                dys_ref[at] = (dmv * ss).astype(BF16)
                st_l[(slot,) + at] = (dmv * yl_ref[at].astype(F32) * sl * (1.0 - sl)).astype(BF16)
                st_s[(slot,) + at] = (dmv * ys_ref[at].astype(F32) * ss * (1.0 - ss)).astype(BF16)

        def where(s):
            row0, colc = (s % nT) * tb, (s // nT) * C
            return [(row0, col0 + colc), (row0, col0 + D + colc)]

        _store_staged([st_l, st_s], dp_ref, sems, step_no, nC * nT, where)

    blk = pl.BlockSpec((tb, C), lambda c, t: (t, c))
    outs, passed = _call(
        name, (nC, nT), body,
        [pl.BlockSpec((tb, C), lambda c, t: (t, k0 + c)), pl.BlockSpec((tb, C), lambda c, t: (t, k0 + nC + c)),
         blk, blk, blk], [p, p, y_lru, y_sc, dm],
        [_S(p.shape, BF16), _S((T, D), BF16), _S((T, D), BF16)], [ANY, blk, blk],
        [pltpu.VMEM((2, tb, C), BF16), pltpu.VMEM((2, tb, C), BF16), pltpu.SemaphoreType.DMA((2, 2))], tasks)
    return (*outs, passed) if tasks else outs


def _ffn_act_fwd(uu, conv_w, *, name, tasks=()):
    T = uu.shape[0]
    F = uu.shape[1] // 2
    C = _tile(F, C_EW)
    nC = F // C
    tb = _tile(T, TB_CHUNKED, HALO)
    nT, hb = T // tb, tb // HALO
    kw = conv_w.shape[0]
    R = HALO

    def body(g_ref, gh_ref, v_ref, vh_ref, wg_ref, wv_ref, o_ref):
        t = pl.program_id(1)

        def chunk(k, carry):
            gp, vp = carry
            r0 = pl.multiple_of(k * R, R)
            ug = g_ref[pl.ds(r0, R), :].astype(F32)
            uv = v_ref[pl.ds(r0, R), :].astype(F32)
            cg = _conv(ug, gp, wg_ref)
            cv = _conv(uv, vp, wv_ref)
            o_ref[pl.ds(r0, R), :] = (cg * jax.nn.sigmoid(cg) * cv).astype(BF16)
            return ug[R - SUB:, :], uv[R - SUB:, :]

        lax.fori_loop(0, tb // R, chunk, (_prev8(gh_ref, t), _prev8(vh_ref, t)))

    seg = lambda k: pl.BlockSpec((tb, C), lambda c, t: (t, k * nC + c))
    hseg = lambda k: pl.BlockSpec((HALO, C), _halo_prev_map(hb, lambda c: k * nC + c))
    wseg = lambda k: pl.BlockSpec((kw, C), lambda c, t: (0, k * nC + c))
    outs, passed = _call(
        name, (nC, nT), body, [seg(0), hseg(0), seg(1), hseg(1), wseg(0), wseg(1)],
        [uu, uu, uu, uu, conv_w, conv_w], [_S((T, F), BF16)], [pl.BlockSpec((tb, C), lambda c, t: (t, c))], [], tasks)
    return (outs[0], passed) if tasks else outs[0]


def _ffn_act_bwd(uu, dact, conv_w, *, name):
    T = uu.shape[0]
    F = uu.shape[1] // 2
    C = _tile(F, C_EW)
    nC = F // C
    tb = _tile(T, TB_CHUNKED, HALO)
    nT, hb = T // tb, tb // HALO
    kw = conv_w.shape[0]
    R = HALO
    nk = tb // R

    def body(g_ref, gh_ref, v_ref, vh_ref, da_ref, wg_ref, wv_ref, du_ref, dwg_ref, dwv_ref,
             gn_s, vn_s, accg_s, accv_s, st_g, st_v, sems):
        c = pl.program_id(0)
        tr = pl.program_id(1)
        t = nT - 1 - tr
        first = tr == 0

        @pl.when(first)
        def _():
            gn_s[...] = jnp.zeros_like(gn_s)
            vn_s[...] = jnp.zeros_like(vn_s)
            dwg_ref[...] = jnp.zeros_like(dwg_ref)
            dwv_ref[...] = jnp.zeros_like(dwv_ref)

        accg_s[...] = jnp.zeros_like(accg_s)
        accv_s[...] = jnp.zeros_like(accv_s)
        step_no = c * nT + tr
        slot = step_no % 2

        def chunk(i, carry):
            gn, vn = carry
            k = nk - 1 - i
            r0 = pl.multiple_of(k * R, R)
            rp = pl.multiple_of(jnp.maximum(r0 - R, 0), R)
            ug = g_ref[pl.ds(r0, R), :].astype(F32)
            uv = v_ref[pl.ds(r0, R), :].astype(F32)
            gp = jnp.where(k > 0, g_ref[pl.ds(rp, R), :].astype(F32)[R - SUB:, :], _prev8(gh_ref, t))
            vp = jnp.where(k > 0, v_ref[pl.ds(rp, R), :].astype(F32)[R - SUB:, :], _prev8(vh_ref, t))
            sh_g = _shifted_down(ug, gp, kw)
            sh_v = _shifted_down(uv, vp, kw)
            cg = _taps(sh_g, wg_ref)
            cv = _taps(sh_v, wv_ref)
            da = da_ref[pl.ds(r0, R), :].astype(F32)
            sg = jax.nn.sigmoid(cg)
            d_cg = da * cv * (sg * (1.0 + cg * (1.0 - sg)))
            d_cv = da * (cg * sg)
            for j in range(kw):
                accg_s[j] += d_cg * sh_g[kw - 1 - j]
                accv_s[j] += d_cv * sh_v[kw - 1 - j]
            st_g[slot, pl.ds(r0, R), :] = _conv_t(d_cg, gn, wg_ref).astype(BF16)
            st_v[slot, pl.ds(r0, R), :] = _conv_t(d_cv, vn, wv_ref).astype(BF16)
            return d_cg[:SUB, :], d_cv[:SUB, :]

        gn, vn = lax.fori_loop(0, nk, chunk, (gn_s[...], vn_s[...]))
        gn_s[...] = gn
        vn_s[...] = vn
        for j in range(kw):
            dwg_ref[pl.ds(j, 1), :] += jnp.sum(accg_s[j], axis=0, keepdims=True)
            dwv_ref[pl.ds(j, 1), :] += jnp.sum(accv_s[j], axis=0, keepdims=True)
        def where(s):
            row0, col0 = (nT - 1 - s % nT) * tb, (s // nT) * C
            return [(row0, col0), (row0, F + col0)]

        _store_staged([st_g, st_v], du_ref, sems, step_no, nC * nT, where)

    seg = lambda k: pl.BlockSpec((tb, C), lambda c, tr: (nT - 1 - tr, k * nC + c))
    hseg = lambda k: pl.BlockSpec((HALO, C), lambda c, tr: (jnp.maximum((nT - 1 - tr) * hb - 1, 0), k * nC + c))
    wseg = lambda k: pl.BlockSpec((kw, C), lambda c, tr: (0, k * nC + c))
    dwb = pl.BlockSpec((kw, C), lambda c, tr: (0, c))
    return pl.pallas_call(
        body, name=name, out_shape=(_S(uu.shape, BF16), _S((kw, F), F32), _S((kw, F), F32)), grid=(nC, nT),
        in_specs=[seg(0), hseg(0), seg(1), hseg(1), pl.BlockSpec((tb, C), lambda c, tr: (nT - 1 - tr, c)),
                  wseg(0), wseg(1)],
        out_specs=(ANY, dwb, dwb),
        scratch_shapes=[pltpu.VMEM((SUB, C), F32), pltpu.VMEM((SUB, C), F32),
                        pltpu.VMEM((kw, R, C), F32), pltpu.VMEM((kw, R, C), F32),
                        pltpu.VMEM((2, tb, C), BF16), pltpu.VMEM((2, tb, C), BF16), pltpu.SemaphoreType.DMA((2, 2))],
        compiler_params=_cparams(("arbitrary", "arbitrary")),
    )(uu, uu, uu, uu, dact, conv_w, conv_w)


def _place():
    x, y, c = lax.axis_index("x"), lax.axis_index("y"), lax.axis_index("c")
    return x, y, c


def _chips(x, y):
    return [(1 - x, y), (x, 1 - y), (1 - x, 1 - y)]


def _all_gather(arrays, placed, over_ici, pair_n, name):
    n = len(arrays)

    def body(*refs):
        ins, outs = refs[:n], refs[n:2 * n]
        send_sems, recv_sems, local_sems = refs[2 * n:]
        x, y, c = _place()
        me, sibling = (x, y, c), (x, y, 1 - c)
        chips = _chips(x, y)
        full = [a for a in range(n) if over_ici[a]]

        def idx(px, py, pc):
            return 4 * px + 2 * py + pc

        def copy(a, k, block, to):
            dst = _dev_block(outs[a], idx(*block), pair_n[a])
            src = ins[a] if (block is me and not placed[a]) else dst
            return pltpu.make_async_remote_copy(
                src_ref=src, dst_ref=dst, send_sem=send_sems.at[a, k], recv_sem=recv_sems.at[a, k],
                device_id=to, device_id_type=MESH)

        def half(a, k, block, to, lo):
            r = rows_of[a] // 2
            blk = _rows_of(outs[a], idx(*block), (0 if lo else r, r), pair_n[a])
            return pltpu.make_async_remote_copy(
                src_ref=blk, dst_ref=blk, send_sem=send_sems.at[a, k], recv_sem=recv_sems.at[a, k],
                device_id=to, device_id_type=MESH)

        mine = [pltpu.make_async_copy(ins[a], outs[a].at[idx(*me)], local_sems.at[a])
                for a in range(n) if not placed[a]]
        for cp in mine:
            cp.start()
        chip_x, chip_y, chip_d = chips
        sent = []
        for a in full:
            sent += [copy(a, 1, me, (*chip_x, c)), copy(a, 2, me, (*chip_y, c))]
            if not relay[a]:
                sent.append(copy(a, 3, me, (*chip_d, c)))
        for a in range(n):
            sent.append(copy(a, 0, me, sibling))
        for cp in sent:
            cp.start()

        def then(cp):
            cp.start()
            sent.append(cp)

        for a in full:
            copy(a, 2, (*chip_y, c), me).wait_recv()
            if relay[a]:
                then(half(a, 3, (*chip_y, c), (*chip_x, c), True))
            then(copy(a, 6, (*chip_y, c), sibling))
            copy(a, 1, (*chip_x, c), me).wait_recv()
            if relay[a]:
                then(half(a, 4, (*chip_x, c), (*chip_y, c), False))
            then(copy(a, 5, (*chip_x, c), sibling))
        for a in full:
            if relay[a]:
                half(a, 3, (*chip_d, c), me, True).wait_recv()
                half(a, 4, (*chip_d, c), me, False).wait_recv()
            else:
                copy(a, 3, (*chip_d, c), me).wait_recv()
            then(copy(a, 7, (*chip_d, c), sibling))
        for a in range(n):
            copy(a, 0, sibling, me).wait_recv()
        for a in full:
            for j, chip in enumerate(chips):
                copy(a, 5 + j, (*chip, 1 - c), me).wait_recv()
        for cp in sent:
            cp.wait_send()
        for cp in mine:
            cp.wait()

    rows_of = [(s.shape[1] if placed[a] else s.shape[0]) for a, s in enumerate(arrays)]
    relay = [r % (2 * HALO) == 0 for r in rows_of]
    return pl.pallas_call(
        body, name=name,
        out_shape=tuple(_S(s.shape if placed[a] else (N_DEV,) + s.shape, s.dtype) for a, s in enumerate(arrays)),
        in_specs=[ANY] * n, out_specs=tuple([ANY] * n),
        scratch_shapes=[pltpu.SemaphoreType.DMA((n, 8)), pltpu.SemaphoreType.DMA((n, 8)),
                        pltpu.SemaphoreType.DMA((n,))],
        input_output_aliases={a: a for a in range(n) if placed[a]},
    )(*arrays)


def _dev_block(ref, dev, pair_n=None):
    if pair_n is None:
        return ref.at[dev]
    return ref.at[dev // 2, :, pl.ds(pl.multiple_of((dev % 2) * pair_n, LANES), pair_n)]


def _rows_of(ref, blk, rows, pair_n=None):
    v = _dev_block(ref, blk, pair_n)
    return v if rows is None else v.at[pl.ds(rows[0], rows[1])]


ALL_ROWS = "all"


def _gather_task(buf, ici=None, fwd=None, pair_n=None):
    blk_of = functools.partial(_rows_of, pair_n=pair_n)
    r0, nr = (0, buf.shape[1]) if ici == ALL_ROWS else (ici or (0, 0))
    assert nr % (2 * HALO) == 0
    lo, hi = (r0, nr // 2), (r0 + nr // 2, nr // 2)
    both = (r0, nr)
    fwd_rows = None if fwd == ALL_ROWS else fwd

    def remote(refs, ss, rs, k, dev, rows, to):
        blk = blk_of(refs[0], dev, rows)
        return pltpu.make_async_remote_copy(src_ref=blk, dst_ref=blk, send_sem=ss.at[k], recv_sem=rs.at[k],
                                            device_id=to, device_id_type=MESH)

    def waves(refs, ss, rs):
        x, y, c = _place()
        me = 4 * x + 2 * y + c
        (xx, xy), (yx, yy), _ = _chips(x, y)
        dev_x, dev_y = 4 * xx + 2 * xy + c, 4 * yx + 2 * yy + c
        first, second = [], []
        if ici is not None:
            first += [remote(refs, ss, rs, 0, me, both, (xx, xy, c)), remote(refs, ss, rs, 1, me, both, (yx, yy, c))]
            second += [remote(refs, ss, rs, 2, dev_y, lo, (xx, xy, c)), remote(refs, ss, rs, 3, dev_x, hi, (yx, yy, c))]
        if fwd is not None:
            first += [remote(refs, ss, rs, 4 + j, 4 * px + 2 * py + c, fwd_rows, (x, y, 1 - c))
                      for j, (px, py) in enumerate(_chips(x, y))]
        return first, second

    def start(refs, ss, rs, ls):
        for cp in waves(refs, ss, rs)[0]:
            cp.start()

    def mid(refs, ss, rs, ls):
        x, y, c = _place()
        (xx, xy), (yx, yy), _ = _chips(x, y)
        remote(refs, ss, rs, 0, 4 * xx + 2 * xy + c, both, (x, y, c)).wait_recv()
        remote(refs, ss, rs, 1, 4 * yx + 2 * yy + c, both, (x, y, c)).wait_recv()
        for cp in waves(refs, ss, rs)[1]:
            cp.start()

    def wait(refs, ss, rs, ls):
        x, y, c = _place()
        chips = _chips(x, y)
        if ici is not None:
            dev_d = 4 * chips[2][0] + 2 * chips[2][1] + c
            remote(refs, ss, rs, 2, dev_d, lo, (x, y, c)).wait_recv()
            remote(refs, ss, rs, 3, dev_d, hi, (x, y, c)).wait_recv()
        if fwd is not None:
            for j, (px, py) in enumerate(chips):
                remote(refs, ss, rs, 4 + j, 4 * px + 2 * py + 1 - c, fwd_rows, (x, y, c)).wait_recv()
        first, second = waves(refs, ss, rs)
        for cp in first + second:
            cp.wait_send()

    return _Task([buf], [0], start, wait, nsem=7, mid=mid if ici is not None else None)


def _exchange_task(parts, landing, rows=None):
    def copies(refs, ss, rs):
        x, y, c = _place()
        myq = 2 * x + y
        return [pltpu.make_async_remote_copy(
            src_ref=_rows_of(refs[0], 2 * px + py, rows), dst_ref=_rows_of(refs[1], myq, rows),
            send_sem=ss.at[k], recv_sem=rs.at[k], device_id=(px, py, c), device_id_type=MESH)
            for k, (px, py) in enumerate(_chips(x, y))]

    def start(refs, ss, rs, ls):
        for cp in copies(refs, ss, rs):
            cp.start()

    def wait(refs, ss, rs, ls):
        x, y, c = _place()
        for k, (px, py) in enumerate(_chips(x, y)):
            pltpu.make_async_remote_copy(
                src_ref=_rows_of(refs[0], 2 * x + y, rows), dst_ref=_rows_of(refs[1], 2 * px + py, rows),
                send_sem=ss.at[k], recv_sem=rs.at[k], device_id=(px, py, c), device_id_type=MESH).wait_recv()
        for cp in copies(refs, ss, rs):
            cp.wait_send()

    return _Task([parts, landing], [1], start, wait)


def _core_blocks(g, pair_n):
    if pair_n is None:
        g4 = g.reshape((N_CHIP, 2) + g.shape[1:])
        return g4, (N_CHIP,) + g.shape[1:], lambda ref, c: ref.at[:, c]
    view = lambda ref, c: ref.at[:, :, pl.ds(pl.multiple_of(c * pair_n, LANES), pair_n)]
    return g, (N_CHIP, g.shape[1], pair_n), view


def _swap_task(g, pair_n=None):
    g4, shape, view = _core_blocks(g, pair_n)

    def copy(refs, ss, rs):
        x, y, c = _place()
        return pltpu.make_async_remote_copy(
            src_ref=view(refs[0], 1 - c), dst_ref=refs[1], send_sem=ss.at[0], recv_sem=rs.at[0],
            device_id=(x, y, 1 - c), device_id_type=MESH)

    def start(refs, ss, rs, ls):
        copy(refs, ss, rs).start()

    def wait(refs, ss, rs, ls):
        copy(refs, ss, rs).wait()

    return _Task([g4], [], start, wait, fresh=[_S(shape, g.dtype)], nsem=1)


def _peer(x, y, c, m):
    return x ^ (m >> 2), y ^ ((m >> 1) & 1), c ^ (m & 1)


def _bcast_task(pack):
    def copies(refs, ss, rs):
        x, y, c = _place()
        me = 4 * x + 2 * y + c
        return [pltpu.make_async_remote_copy(
            src_ref=refs[0], dst_ref=refs[1].at[me], send_sem=ss.at[m - 1], recv_sem=rs.at[m - 1],
            device_id=_peer(x, y, c, m), device_id_type=MESH) for m in range(1, N_DEV)]

    def local(refs, ls):
        x, y, c = _place()
        return pltpu.make_async_copy(refs[0], refs[1].at[4 * x + 2 * y + c], ls.at[0])

    def start(refs, ss, rs, ls):
        local(refs, ls).start()
        for cp in copies(refs, ss, rs):
            cp.start()

    def wait(refs, ss, rs, ls):
        x, y, c = _place()
        for m in range(1, N_DEV):
            px, py, pc = _peer(x, y, c, m)
            pltpu.make_async_remote_copy(
                src_ref=refs[0], dst_ref=refs[1].at[4 * px + 2 * py + pc], send_sem=ss.at[m - 1],
                recv_sem=rs.at[m - 1], device_id=(px, py, pc), device_id_type=MESH).wait_recv()
        for cp in copies(refs, ss, rs):
            cp.wait_send()
        local(refs, ls).wait()

    return _Task([pack], [], start, wait, fresh=[_S((N_DEV,) + pack.shape, pack.dtype)], nsem=N_DEV - 1)


def _sum_packs(packs, name):
    _, R, L = packs.shape

    def body(p_ref, o_ref):
        acc = p_ref[0]
        for k in range(1, N_DEV):
            acc = acc + p_ref[k]
        o_ref[...] = acc

    return pl.pallas_call(body, name=name, out_shape=_S((R, L), packs.dtype), in_specs=[VMEM_SPEC],
                          out_specs=VMEM_SPEC, compiler_params=_cparams())(packs)


def _swap_halves(g, name, pair_n=None):
    g4, shape, view = _core_blocks(g, pair_n)

    def body(g_ref, o_ref, send_sem, recv_sem):
        x, y, c = _place()
        cp = pltpu.make_async_remote_copy(
            src_ref=view(g_ref, 1 - c), dst_ref=o_ref, send_sem=send_sem, recv_sem=recv_sem,
            device_id=(x, y, 1 - c), device_id_type=MESH)
        cp.start()
        cp.wait()

    return pl.pallas_call(
        body, name=name, out_shape=_S(shape, g.dtype), in_specs=[ANY], out_specs=ANY,
        scratch_shapes=[pltpu.SemaphoreType.DMA, pltpu.SemaphoreType.DMA],
    )(g4)


def _add_halves(g, landed, place, name, pair_n=None):
    _, r, cc = landed.shape
    tr = _tile(r, 512, HALO)
    if pair_n is None:
        g4 = g.reshape(N_CHIP, 2, r, cc)
        g_spec = pl.BlockSpec((None, None, tr, cc), lambda i, q, s: (q, s[0], i, 0))
    else:
        g4 = g
        g_spec = pl.BlockSpec((None, tr, cc), lambda i, q, s: (q, i, s[0]))

    def body(s_ref, g_ref, l_ref, o_ref, land_ref):
        q = pl.program_id(1)
        v = (g_ref[...].astype(F32) + l_ref[...].astype(F32)).astype(BF16)
        o_ref[...] = v

        @pl.when(q == s_ref[1])
        def _():
            land_ref[...] = v

    return pl.pallas_call(
        body, name=name, out_shape=(_S((N_CHIP, r, cc), BF16), _S((N_CHIP, r, cc), BF16)),
        grid_spec=pltpu.PrefetchScalarGridSpec(
            num_scalar_prefetch=1, grid=(r // tr, N_CHIP),
            in_specs=[g_spec,
                      pl.BlockSpec((None, tr, cc), lambda i, q, s: (q, i, 0))],
            out_specs=(pl.BlockSpec((None, tr, cc), lambda i, q, s: (q, i, 0)),
                       pl.BlockSpec((None, tr, cc), lambda i, q, s: (s[1], i, 0)))),
        compiler_params=_cparams(("arbitrary", "arbitrary")),
    )(place, g4, landed)


def _all_reduce_small(pack, name):
    R = pack.shape[0]

    def body(p_ref, o_ref, buf, send_sems, recv_sems):
        x, y, c = _place()
        me = 4 * x + 2 * y + c
        buf[me] = p_ref[...]
        cps = []
        for k in range(N_DEV - 1):
            m = k + 1
            peer = (x ^ (m >> 2), y ^ ((m >> 1) & 1), c ^ (m & 1))
            cps.append(pltpu.make_async_remote_copy(
                src_ref=p_ref, dst_ref=buf.at[me], send_sem=send_sems.at[k], recv_sem=recv_sems.at[k],
                device_id=peer, device_id_type=MESH))
        for cp in cps:
            cp.start()
        for k in range(N_DEV - 1):
            m = k + 1
            peer_idx = 4 * (x ^ (m >> 2)) + 2 * (y ^ ((m >> 1) & 1)) + (c ^ (m & 1))
            pltpu.make_async_remote_copy(
                src_ref=p_ref, dst_ref=buf.at[peer_idx], send_sem=send_sems.at[k], recv_sem=recv_sems.at[k],
                device_id=(x, y, c), device_id_type=MESH).wait_recv()
        for cp in cps:
            cp.wait_send()
        acc = buf[0]
        for k in range(1, N_DEV):
            acc = acc + buf[k]
        o_ref[...] = acc

    return pl.pallas_call(
        body, name=name, out_shape=_S((R, LANES), F32),
        in_specs=[VMEM_SPEC], out_specs=VMEM_SPEC,
        scratch_shapes=[pltpu.VMEM((N_DEV, R, LANES), F32), pltpu.SemaphoreType.DMA((N_DEV - 1,)),
                        pltpu.SemaphoreType.DMA((N_DEV - 1,))],
        compiler_params=_cparams(),
    )(pack)


def _adamw_math(w, g, m, v):
    m = ADAM_B1 * m + (1.0 - ADAM_B1) * g
    v = ADAM_B2 * v + (1.0 - ADAM_B2) * (g * g)
    m_hat = m / (1.0 - ADAM_B1 ** ADAM_STEP)
    v_hat = v / (1.0 - ADAM_B2 ** ADAM_STEP)
    delta = -ADAM_LR * (m_hat / (jnp.sqrt(v_hat) + ADAM_EPS) + ADAM_WD * w)
    return delta, m, v


def _adamw_block(p_ref, w_ref, m_ref, v_ref, g_ref, d_ref, nm_ref, nv_ref):
    g = p_ref[0].astype(F32)
    for q in range(1, N_CHIP):
        g = g + p_ref[q].astype(F32)
    g_ref[...] = g
    d_ref[...], nm_ref[...], nv_ref[...] = _adamw_math(w_ref[...], g, m_ref[...], v_ref[...])


def _adamw_big(parts, w, m, v, name):
    r, cc = w.shape
    tr = _tile(r, 128, HALO)
    body = functools.partial(_adamw_block)

    blk = pl.BlockSpec((tr, cc), lambda i: (i, 0))
    return pl.pallas_call(
        body, name=name, out_shape=tuple(_S((r, cc), F32) for _ in range(4)), grid=(r // tr,),
        in_specs=[pl.BlockSpec((N_CHIP, tr, cc), lambda i: (0, i, 0)), blk, blk, blk],
        out_specs=(blk, blk, blk, blk), compiler_params=_cparams(("parallel",)),
    )(parts, w, m, v)


def _adamw_small(ws, gs, ms, vs, name):
    n = len(ws)

    def body(*refs):
        w_r, g_r, m_r, v_r = refs[:n], refs[n:2 * n], refs[2 * n:3 * n], refs[3 * n:4 * n]
        d_r, nm_r, nv_r = refs[4 * n:5 * n], refs[5 * n:6 * n], refs[6 * n:7 * n]
        for k in range(n):
            d_r[k][...], nm_r[k][...], nv_r[k][...] = _adamw_math(w_r[k][...], g_r[k][...], m_r[k][...], v_r[k][...])

    shapes = tuple(_S(w.shape, F32) for w in ws)
    outs = pl.pallas_call(
        body, name=name, out_shape=shapes * 3,
        in_specs=[VMEM_SPEC] * (4 * n), out_specs=tuple([VMEM_SPEC] * (3 * n)),
        compiler_params=_cparams(),
    )(*ws, *gs, *ms, *vs)
    return outs[:n], outs[n:2 * n], outs[2 * n:]


def _block_diag(w, heads_per_block):
    H, hd, _ = w.shape
    nb = H // heads_per_block
    eye = jnp.eye(heads_per_block, dtype=w.dtype)
    w4 = w.reshape(nb, heads_per_block, hd, hd)
    return jnp.einsum("nhab,hg->nhagb", w4, eye).reshape(nb, heads_per_block * hd, heads_per_block * hd)


def _diag_blocks(bd, heads_per_block, hd):
    nb = bd.shape[0]
    b5 = bd.reshape(nb, heads_per_block, hd, heads_per_block, hd)
    return jnp.stack([b5[:, h, :, h, :] for h in range(heads_per_block)], axis=1).reshape(nb * heads_per_block, hd, hd)


def _as_rows(a):
    if a.ndim == 1:
        return a.reshape(-1, LANES) if a.shape[0] % LANES == 0 else a.reshape(1, -1)
    if a.ndim == 3:
        return a.reshape(-1, LANES) if (a.size % LANES == 0) else a.reshape(a.shape[0] * a.shape[1], a.shape[2])
    return a


def kernel(x, g_mix, w_in, lru_conv_w, lru_conv_b, lru_wa, lru_ba, lru_wx, lru_bx, lru_lambda, lru_w_out, sc_conv_w, sc_w_out, w_o, g_ffn, ffn_w_up, ffn_conv_w, ffn_w_down, g_final, loss_target, m_g_mix, m_w_in, m_lru_conv_w, m_lru_conv_b, m_lru_wa, m_lru_ba, m_lru_wx, m_lru_bx, m_lru_lambda, m_lru_w_out, m_sc_conv_w, m_sc_w_out, m_w_o, m_g_ffn, m_ffn_w_up, m_ffn_conv_w, m_ffn_w_down, m_g_final, v_g_mix, v_w_in, v_lru_conv_w, v_lru_conv_b, v_lru_wa, v_lru_ba, v_lru_wx, v_lru_bx, v_lru_lambda, v_lru_w_out, v_sc_conv_w, v_sc_w_out, v_w_o, v_g_ffn, v_ffn_w_up, v_ffn_conv_w, v_ffn_w_down, v_g_final):
    T, D = x.shape[1], x.shape[2]
    d_lru = lru_lambda.shape[0]
    d_sc = sc_conv_w.shape[1] * N_DEV
    F = ffn_w_down.shape[0] * N_DEV
    H = lru_wa.shape[0]
    assert d_lru == d_sc and H * HEAD_DIM == d_lru
    xs = x.reshape(T, D)
    tgt = loss_target.reshape(T, D)
    my_x, my_y, my_c = _place()
    me = 4 * my_x + 2 * my_y + my_c

    big = [w_in, lru_w_out, sc_w_out, w_o, ffn_w_up, ffn_w_down]
    big_names = ["w_in", "lru_w_out", "sc_w_out", "w_o", "ffn_w_up", "ffn_w_down"]
    place = jnp.stack([my_c, 2 * my_x + my_y, me]).astype(jnp.int32)
    n_in, n_up = w_in.shape[1], ffn_w_up.shape[1]
    paired = [n_in, None, None, None, n_up, None]
    big_bf = [_cast_into_slot(w, place, "cast_" + nm, paired=pn is not None)
              for w, nm, pn in zip(big, big_names, paired)]
    pad_rows = lambda a: jnp.pad(a, ((0, SUB - a.shape[0]), (0, 0)))
    gathered = _all_gather(big_bf + [pad_rows(lru_conv_w), pad_rows(sc_conv_w), pad_rows(ffn_conv_w)],
                           [True] * 6 + [False] * 3,
                           [True, False, False, False, False, False, True, True, True],
                           paired + [None] * 3, "all_gather_first")
    W_in, W_lo, W_so, W_o8, W_up, W_dn8 = gathered[:6]
    full_cols = lambda g, kw: g[:, :kw, :].transpose(1, 0, 2).reshape(kw, -1)
    cw_lru = full_cols(gathered[6], lru_conv_w.shape[0])
    cw_sc = full_cols(gathered[7], sc_conv_w.shape[0])
    cw_ffn = full_cols(gathered[8], ffn_conv_w.shape[0])

    C = _tile(d_lru, C_LRU)
    hpb = C // HEAD_DIM
    wa_bd = _block_diag(lru_wa, hpb).astype(BF16)
    wx_bd = _block_diag(lru_wx, hpb).astype(BF16)
    cb, ba, bx, lam = (a.reshape(1, d_lru) for a in (lru_conv_b, lru_ba, lru_bx, lru_lambda))

    h1 = _rms_fwd(xs, g_mix, "rms_mix")
    k8 = W_up.shape[1] // 8
    wide = 2 * max(n_in, n_up)
    p, ((W_o8,), (W_lo,), (W_so,), (W_up,)) = _mm_nn(
        h1, W_in, out_dtype=BF16, name="mm_in", tn=wide,
        tasks=[_gather_task(W_o8, ici=ALL_ROWS), _gather_task(W_lo, ici=ALL_ROWS), _gather_task(W_so, ici=ALL_ROWS),
               _gather_task(W_up, ici=(0, 4 * k8), pair_n=n_up)])
    hs, yl_pre, ((W_o8,), (W_lo,), (W_so,), (W_up,)) = _lru_fwd(
        p, cw_lru, cb, wa_bd, ba, wx_bd, bx, lam, name="lru_fwd",
        tasks=[_gather_task(W_o8, fwd=ALL_ROWS), _gather_task(W_lo, fwd=ALL_ROWS), _gather_task(W_so, fwd=ALL_ROWS),
               _gather_task(W_up, ici=(4 * k8, 3 * k8), fwd=(0, 4 * k8), pair_n=n_up)])
    ys_pre = _sc_fwd(p, cw_sc, d=d_sc, name="sc_fwd")
    y_lru, ((W_up,),) = _mm_small(
        "nn", yl_pre, None, W_lo, name="mm_lru_out",
        tasks=[_gather_task(W_up, ici=(7 * k8, k8), fwd=(4 * k8, 3 * k8), pair_n=n_up)])
    y_sc, ((W_up,),) = _mm_small("nn", ys_pre, None, W_so, name="mm_sc_out",
                                 tasks=[_gather_task(W_up, fwd=(7 * k8, k8), pair_n=n_up)])
    gate0 = 2 * d_lru + 3 * d_sc
    merged = _merge_fwd(p, y_lru, y_sc, col0=gate0, name="merge_fwd")
    W_o = W_o8.reshape(1, D, D)
    x1 = _mm_nn(merged, W_o, out_dtype=F32, residual=xs, name="mm_o")
    h2 = _rms_fwd(x1, g_ffn, "rms_ffn")
    uu, ((W_dn8,),) = _mm_nn(h2, W_up, out_dtype=BF16, name="mm_up", tn=wide,
                             tasks=[_gather_task(W_dn8, ici=ALL_ROWS)])
    act, ((W_dn8,),) = _ffn_act_fwd(uu, cw_ffn, name="ffn_act_fwd", tasks=[_gather_task(W_dn8, fwd=ALL_ROWS)])
    W_dn = W_dn8.reshape(1, F, D)
    x2 = _mm_nn(act, W_dn, out_dtype=F32, residual=x1, name="mm_down", tn=1024, tk=F)
    dx2, dx2b, loss_part, dg_final = _loss_head(x2, g_final, tgt, "loss_head")

    def pack_rows(arrs):
        flat = jnp.concatenate([a.reshape(-1) for a in arrs])
        rows = -(-flat.shape[0] // (SUB * LANES)) * SUB
        return jnp.pad(flat, (0, rows * LANES - flat.shape[0])).reshape(rows, LANES)

    def unpack_rows(pack, arrs):
        flat, out, o = pack.reshape(-1), [], 0
        for a in arrs:
            out.append(flat[o:o + a.size].reshape(a.shape))
            o += a.size
        return out

    dact = _mm_nt(dx2b, W_dn, out_dtype=BF16, name="mm_down_dx", tm=512, tko=F // 2, tn=D)
    gW_dn = _mm_tn(act, dx2b, 1, out_dtype=BF16, name="mm_down_dw", tk=1408, tt=2048).reshape(N_DEV, F // N_DEV, D)
    duu, dcw_ffn_g, dcw_ffn_v = _ffn_act_bwd(uu, dact, cw_ffn, name="ffn_act_bwd")
    dh2, ((land_dn,),) = _mm_nt(duu, W_up, out_dtype=BF16, name="mm_up_dx", tn=wide, tasks=[_swap_task(gW_dn)])
    parts_dn = _add_halves(gW_dn, land_dn, place, "rs_add_ffn_w_down")
    gW_up, ((mine_dn,),) = _mm_tn(h2, duu, N_CHIP, out_dtype=BF16, name="mm_up_dw", tk=512, tn=wide, tt=2048,
                                  tasks=[_exchange_task(*parts_dn)])
    dx1, dx1b, dg_ffn = _rms_bwd(x1, g_ffn, dh2, dx2, "rms_ffn_bwd")
    dmerged, ((land_up,),) = _mm_nt(dx1b, W_o, out_dtype=BF16, name="mm_o_dx", tn=D,
                                    tasks=[_swap_task(gW_up, pair_n=n_up)])
    parts_up, land_up = _add_halves(gW_up, land_up, place, "rs_add_ffn_w_up", pair_n=n_up)
    r8 = parts_up.shape[1] // 8
    gW_o, ((land_up,),) = _mm_tn(merged, dx1b, 1, out_dtype=BF16, name="mm_o_dw", tt=2048,
                                 tasks=[_exchange_task(parts_up, land_up, rows=(0, r8))])
    gW_o = gW_o.reshape(N_DEV, D // N_DEV, D)
    dp, dy_lru, dy_sc, ((land_up,),) = _merge_bwd(
        p, y_lru, y_sc, dmerged, col0=gate0, name="merge_bwd",
        tasks=[_exchange_task(parts_up, land_up, rows=(r8, r8))])
    dyl_pre, ((land_o,),) = _mm_small("nt", None, dy_lru, W_lo, name="mm_lru_out_dx", tasks=[_swap_task(gW_o)])
    parts_o = _add_halves(gW_o, land_o, place, "rs_add_w_o")
    gW_lo = _mm_small("tn", yl_pre, dy_lru, W_lo, name="mm_lru_out_dw")
    dys_pre, ((land_lo,),) = _mm_small("nt", None, dy_sc, W_so, name="mm_sc_out_dx", tasks=[_swap_task(gW_lo)])
    parts_lo = _add_halves(gW_lo, land_lo, place, "rs_add_lru_w_out")
    gW_so = _mm_small("tn", ys_pre, dy_sc, W_so, name="mm_sc_out_dw")
    dp, dcw_sc, ((land_up,),) = _sc_bwd(p, dys_pre, dp, cw_sc, d=d_sc, name="sc_bwd",
                                        tasks=[_exchange_task(parts_up, land_up, rows=(2 * r8, r8))])
    dp, dcw_lru, dcb, dwa_bd, dba, dwx_bd, dbx, dlam, ((land_up,), (mine_o,), (mine_lo,), (land_so,)) = _lru_bwd(
        p, hs, dyl_pre, dp, cw_lru, cb, wa_bd, ba, wx_bd, bx, lam, name="lru_bwd",
        tasks=[_exchange_task(parts_up, land_up, rows=(3 * r8, 3 * r8)), _exchange_task(*parts_o),
               _exchange_task(*parts_lo), _swap_task(gW_so)])
    parts_so = _add_halves(gW_so, land_so, place, "rs_add_sc_w_out")

    dwa = _diag_blocks(dwa_bd, hpb, HEAD_DIM)
    dwx = _diag_blocks(dwx_bd, hpb, HEAD_DIM)
    dcw_ffn = jnp.concatenate([dcw_ffn_g, dcw_ffn_v], axis=1)
    rep_grads = [dcb, dwa, dba, dwx, dbx, dlam, dg_ffn, dg_final]
    small_full = rep_grads + [dcw_lru, dcw_sc, dcw_ffn]
    gW_in, ((mine_up,), (mine_so,), (packs,)) = _mm_tn(
        h1, dp, N_CHIP, out_dtype=BF16, name="mm_in_dw", tk=512, tn=wide, tt=2048,
        tasks=[_exchange_task(parts_up, land_up, rows=(6 * r8, 2 * r8)), _exchange_task(*parts_so),
               _bcast_task(pack_rows(small_full))])
    land_in = _swap_halves(gW_in, "rs_swap_w_in", pair_n=n_in)
    parts_in = _add_halves(gW_in, land_in, place, "rs_add_w_in", pair_n=n_in)
    dh1, ((mine_in,),) = _mm_nt(dp, W_in, out_dtype=BF16, name="mm_in_dx", tn=wide,
                                tasks=[_exchange_task(*parts_in)])
    grad_x, dg_mix = _rms_bwd(xs, g_mix, dh1, dx1, "rms_mix_bwd", want_bf16=False)

    mine = [mine_in, mine_lo, mine_so, mine_o, mine_up, mine_dn]
    big_m = [m_w_in, m_lru_w_out, m_sc_w_out, m_w_o, m_ffn_w_up, m_ffn_w_down]
    big_v = [v_w_in, v_lru_w_out, v_sc_w_out, v_w_o, v_ffn_w_up, v_ffn_w_down]
    big_out = {nm: _adamw_big(pt, w, m, v, "adamw_" + nm)
               for nm, pt, w, m, v in zip(big_names, mine, big, big_m, big_v)}

    (scb, swa, sba, swx, sbx, slam, sg_ffn, sg_final, scw_lru, scw_sc, scw_ffn) = unpack_rows(
        _sum_packs(packs, "sum_small"), small_full)
    (sg_mix,) = unpack_rows(_all_reduce_small(pack_rows([dg_mix]), "all_reduce_g_mix"), [dg_mix])

    def my_cols(a):
        n = a.shape[1] // N_DEV
        return lax.dynamic_slice_in_dim(a, me * n, n, axis=1)

    small_names = ["g_mix", "lru_conv_w", "lru_conv_b", "lru_wa", "lru_ba", "lru_wx", "lru_bx", "lru_lambda",
                   "sc_conv_w", "g_ffn", "ffn_conv_w", "g_final"]
    small_w = [g_mix, lru_conv_w, lru_conv_b, lru_wa, lru_ba, lru_wx, lru_bx, lru_lambda, sc_conv_w, g_ffn,
               ffn_conv_w, g_final]
    small_m = [m_g_mix, m_lru_conv_w, m_lru_conv_b, m_lru_wa, m_lru_ba, m_lru_wx, m_lru_bx, m_lru_lambda,
               m_sc_conv_w, m_g_ffn, m_ffn_conv_w, m_g_final]
    small_v = [v_g_mix, v_lru_conv_w, v_lru_conv_b, v_lru_wa, v_lru_ba, v_lru_wx, v_lru_bx, v_lru_lambda,
               v_sc_conv_w, v_g_ffn, v_ffn_conv_w, v_g_final]
    small_g = [sg_mix.reshape(D), my_cols(scw_lru), scb.reshape(d_lru), swa, sba.reshape(d_lru), swx,
               sbx.reshape(d_lru), slam.reshape(d_lru), my_cols(scw_sc), sg_ffn.reshape(D), my_cols(scw_ffn),
               sg_final.reshape(D)]
    sd, snm, snv = _adamw_small([_as_rows(a) for a in small_w], [_as_rows(a) for a in small_g],
                                [_as_rows(a) for a in small_m], [_as_rows(a) for a in small_v], "adamw_small")
    small_out = {nm: (g, d.reshape(w.shape), nm_.reshape(w.shape), nv_.reshape(w.shape))
                 for nm, w, g, d, nm_, nv_ in zip(small_names, small_w, small_g, sd, snm, snv)}

    loss = lax.psum(loss_part[0, 0], AXES)
    order = ["g_mix", "w_in", "lru_conv_w", "lru_conv_b", "lru_wa", "lru_ba", "lru_wx", "lru_bx", "lru_lambda",
             "lru_w_out", "sc_conv_w", "sc_w_out", "w_o", "g_ffn", "ffn_w_up", "ffn_conv_w", "ffn_w_down", "g_final"]
    res = {**big_out, **small_out}
    return (loss, grad_x.reshape(x.shape),
            *[res[nm][0] for nm in order], *[res[nm][1] for nm in order],
            *[res[nm][2] for nm in order], *[res[nm][3] for nm in order])
```

```python
import functools
import math

import jax
import jax.numpy as jnp
from jax import lax
from jax.experimental import pallas as pl
from jax.experimental.pallas import tpu as pltpu

F32, BF16 = jnp.float32, jnp.bfloat16
MESH = pl.DeviceIdType.MESH
N_DEV = 8
N_CHIP = 4
AXES = ("x", "y", "c")

EPS = 1e-6
LRU_C = 8.0
HEAD_DIM = 64
ADAM_LR, ADAM_B1, ADAM_B2, ADAM_EPS, ADAM_WD, ADAM_STEP = 0.001, 0.9, 0.999, 1e-08, 0.01, 10

VMEM_LIMIT = 48 * 1024 * 1024
LANES = 128
SUB = 8
HALO = 16
TB = 512
TB_CHUNKED = 1024
C_LRU = 256
C_EW = 512
TM, TN, TK = 512, 1536, 2048


def _tile(n, pref, align=LANES):
    best = None
    for d in range(align, min(n, pref) + 1, align):
        if n % d == 0:
            best = d
    return best or n


def _cparams(sem=None, vmem=VMEM_LIMIT):
    kw = dict(vmem_limit_bytes=vmem)
    if sem is not None:
        kw["dimension_semantics"] = sem
    return pltpu.CompilerParams(**kw)


def _S(shape, dtype):
    return jax.ShapeDtypeStruct(shape, dtype)


ANY = pl.BlockSpec(memory_space=pl.ANY)
VMEM_SPEC = pl.BlockSpec(memory_space=pltpu.VMEM)


class _Task:
    def __init__(self, arrays, aliased, start, wait, fresh=(), nsem=3, mid=None):
        self.arrays, self.aliased, self.start, self.wait = arrays, aliased, start, wait
        self.fresh, self.nsem, self.mid = list(fresh), nsem, mid


def _call(name, grid, compute, in_specs, args, out_shape, out_specs, scratch, tasks=(), own_aliases=None):
    n_in, n_out, n_scr = len(args), len(out_shape), len(scratch)
    x_in, x_out, aliases, where = [], [], dict(own_aliases or {}), []
    for t in tasks:
        places = []
        for k, arr in enumerate(t.arrays):
            if k in t.aliased:
                aliases[n_in + len(x_in)] = n_out + len(x_out)
                places.append(("out", len(x_out)))
                x_out.append(_S(arr.shape, arr.dtype))
            else:
                places.append(("in", len(x_in)))
            x_in.append(arr)
        for shp in t.fresh:
            places.append(("out", len(x_out)))
            x_out.append(shp)
        where.append(places)
    n_xi, n_xo = len(x_in), len(x_out)

    def body(*refs):
        ins, xi = refs[:n_in], refs[n_in:n_in + n_xi]
        o0 = n_in + n_xi
        outs, xo = refs[o0:o0 + n_out], refs[o0 + n_out:o0 + n_out + n_xo]
        s0 = o0 + n_out + n_xo
        scr, sems = refs[s0:s0 + n_scr], refs[s0 + n_scr:]
        ids = [pl.program_id(a) for a in range(len(grid))]

        def task_refs(ti):
            return [xo[i] if kind == "out" else xi[i] for kind, i in where[ti]]

        if tasks:
            first = functools.reduce(jnp.logical_and, [i == 0 for i in ids])

            @pl.when(first)
            def _():
                for ti, t in enumerate(tasks):
                    t.start(task_refs(ti), *sems[3 * ti:3 * ti + 3])

        compute(*ins, *outs, *scr)
        if any(t.mid is not None for t in tasks):
            n_steps = math.prod(grid)
            step = functools.reduce(lambda s, ig: s * ig[1] + ig[0], zip(ids, grid), 0)

            @pl.when(step == (5 * n_steps) // 8)
            def _():
                for ti, t in enumerate(tasks):
                    if t.mid is not None:
                        t.mid(task_refs(ti), *sems[3 * ti:3 * ti + 3])

        if tasks:
            last = functools.reduce(jnp.logical_and, [i == g - 1 for i, g in zip(ids, grid)])

            @pl.when(last)
            def _():
                for ti, t in enumerate(tasks):
                    t.wait(task_refs(ti), *sems[3 * ti:3 * ti + 3])

    sem_shapes = []
    for t in tasks:
        sem_shapes += [pltpu.SemaphoreType.DMA((t.nsem,)), pltpu.SemaphoreType.DMA((t.nsem,)),
                       pltpu.SemaphoreType.DMA((1,))]
    res = pl.pallas_call(
        body, name=name, grid=grid,
        in_specs=list(in_specs) + [ANY] * n_xi,
        out_specs=tuple(out_specs) + (ANY,) * n_xo,
        out_shape=tuple(out_shape) + tuple(x_out),
        scratch_shapes=list(scratch) + sem_shapes,
        input_output_aliases=aliases,
        compiler_params=_cparams(("arbitrary",) * len(grid)),
    )(*args, *x_in)
    outs, passed, o = res[:n_out], [], n_out
    for places in where:
        k = sum(1 for kind, _ in places if kind == "out")
        passed.append(list(res[o:o + k]))
        o += k
    return outs, passed


def _mm_nn(a, w3, *, out_dtype, name, residual=None, tm=TM, tn=TN, tk=TK, tasks=()):
    M, K = a.shape
    G, _, n = w3.shape
    tm, tn, tk = _tile(M, tm, SUB), _tile(n, tn), _tile(K, tk)
    nj, nk = n // tn, K // tk

    def compute(*refs):
        if residual is None:
            a_ref, w_ref, o_ref = refs[:3]
            r_ref = None
        else:
            a_ref, w_ref, r_ref, o_ref = refs[:4]

        def finish(r):
            if r_ref is not None:
                r = r + r_ref[...]
            o_ref[...] = r.astype(o_ref.dtype)

        if nk == 1:
            finish(jnp.dot(a_ref[...], w_ref[...], preferred_element_type=F32))
            return
        acc = refs[-1]
        k = pl.program_id(3)

        @pl.when(k == 0)
        def _():
            acc[...] = jnp.zeros_like(acc)

        acc[...] += jnp.dot(a_ref[...], w_ref[...], preferred_element_type=F32)

        @pl.when(k == nk - 1)
        def _():
            finish(acc[...])

    in_specs = [pl.BlockSpec((tm, tk), lambda g, j, i, k: (i, k)),
                pl.BlockSpec((None, tk, tn), lambda g, j, i, k: (g, k, j))]
    args = [a, w3]
    if residual is not None:
        in_specs.append(pl.BlockSpec((tm, tn), lambda g, j, i, k: (i, g * nj + j)))
        args.append(residual)
    outs, passed = _call(
        name, (G, nj, M // tm, nk), compute, in_specs, args, [_S((M, G * n), out_dtype)],
        [pl.BlockSpec((tm, tn), lambda g, j, i, k: (i, g * nj + j))],
        [] if nk == 1 else [pltpu.VMEM((tm, tn), F32)], tasks)
    return (outs[0], passed) if tasks else outs[0]


def _mm_nt(dy, w3, *, out_dtype, name, tm=1024, tko=1024, tn=TN, tasks=()):
    M, _ = dy.shape
    G, K, n = w3.shape
    tm, tko, tn = _tile(M, tm, SUB), _tile(K, tko), _tile(n, tn)
    nj = n // tn
    nr = G * nj

    def compute(dy_ref, w_ref, o_ref, *scr):
        part = lax.dot_general(dy_ref[...], w_ref[...], (((1,), (1,)), ((), ())), preferred_element_type=F32)
        if nr == 1:
            o_ref[...] = part.astype(o_ref.dtype)
            return
        (acc,) = scr
        r = pl.program_id(2)

        @pl.when(r == 0)
        def _():
            acc[...] = jnp.zeros_like(acc)

        acc[...] += part

        @pl.when(r == nr - 1)
        def _():
            o_ref[...] = acc[...].astype(o_ref.dtype)

    outs, passed = _call(
        name, (K // tko, M // tm, nr), compute,
        [pl.BlockSpec((tm, tn), lambda ko, i, r: (i, r)),
         pl.BlockSpec((None, tko, tn), lambda ko, i, r: (r // nj, ko, r % nj))],
        [dy, w3], [_S((M, K), out_dtype)], [pl.BlockSpec((tm, tko), lambda ko, i, r: (i, ko))],
        [] if nr == 1 else [pltpu.VMEM((tm, tko), F32)], tasks)
    return (outs[0], passed) if tasks else outs[0]


def _mm_tn(a, dy, G, *, out_dtype, name, tk=1024, tn=TN, tt=1024, tasks=()):
    M, K = a.shape
    n = dy.shape[1] // G
    tk, tn, tt = _tile(K, tk), _tile(n, tn), _tile(M, tt, SUB)
    nj, nt = n // tn, M // tt

    def compute(a_ref, dy_ref, o_ref, acc):
        t = pl.program_id(3)

        @pl.when(t == 0)
        def _():
            acc[...] = jnp.zeros_like(acc)

        acc[...] += lax.dot_general(a_ref[...], dy_ref[...], (((0,), (0,)), ((), ())),
                                    preferred_element_type=F32)

        @pl.when(t == nt - 1)
        def _():
            o_ref[...] = acc[...].astype(o_ref.dtype)

    outs, passed = _call(
        name, (G, nj, K // tk, nt), compute,
        [pl.BlockSpec((tt, tk), lambda g, j, k, t: (t, k)),
         pl.BlockSpec((tt, tn), lambda g, j, k, t: (t, g * nj + j))],
        [a, dy], [_S((G, K, n), out_dtype)], [pl.BlockSpec((None, tk, tn), lambda g, j, k, t: (g, k, j))],
        [pltpu.VMEM((tk, tn), F32)], tasks)
    return (outs[0], passed) if tasks else outs[0]


def _mm_small(kind, a, b, w3, *, name, tm=1024, tasks=()):
    G, K, n = w3.shape
    M = (a if a is not None else b).shape[0]
    tm = _tile(M, tm, HALO)
    nt = M // tm
    w_spec = pl.BlockSpec((G, K, n), lambda i: (0, 0, 0))
    a_spec = pl.BlockSpec((tm, K), lambda i: (i, 0))
    b_spec = pl.BlockSpec((tm, G * n), lambda i: (i, 0))
    cols = lambda g: slice(g * n, (g + 1) * n)
    if kind == "nn":
        def compute(a_ref, w_ref, o_ref):
            av = a_ref[...]
            for g in range(G):
                o_ref[:, cols(g)] = jnp.dot(av, w_ref[g], preferred_element_type=F32).astype(o_ref.dtype)

        outs, passed = _call(name, (nt,), compute, [a_spec, w_spec], [a, w3], [_S((M, G * n), BF16)], [b_spec], [], tasks)
    elif kind == "nt":
        def compute(b_ref, w_ref, o_ref):
            acc = None
            for g in range(G):
                part = lax.dot_general(b_ref[:, cols(g)], w_ref[g], (((1,), (1,)), ((), ())),
                                       preferred_element_type=F32)
                acc = part if acc is None else acc + part
            o_ref[...] = acc.astype(o_ref.dtype)

        outs, passed = _call(name, (nt,), compute, [b_spec, w_spec], [b, w3], [_S((M, K), BF16)], [a_spec], [], tasks)
    else:
        def compute(a_ref, b_ref, o_ref, acc):
            i = pl.program_id(0)

            @pl.when(i == 0)
            def _():
                acc[...] = jnp.zeros_like(acc)

            at = a_ref[...].T
            for g in range(G):
                acc[g] += jnp.dot(at, b_ref[:, cols(g)], preferred_element_type=F32)

            @pl.when(i == nt - 1)
            def _():
                o_ref[...] = acc[...].astype(o_ref.dtype)

        outs, passed = _call(name, (nt,), compute, [a_spec, b_spec], [a, b], [_S((G, K, n), BF16)], [w_spec],
                             [pltpu.VMEM((G, K, n), F32)], tasks)
    return (outs[0], passed) if tasks else outs[0]


def _cast_into_slot(w, place, name, paired=False):
    R, C = w.shape
    tr = _tile(R, 512, HALO)

    def body(s_ref, w_ref, o_ref):
        del s_ref
        o_ref[...] = w_ref[...].astype(BF16)

    if paired:
        shape, out_map = (N_CHIP, R, 2 * C), lambda i, s: (s[1], i, s[0])
    else:
        shape, out_map = (N_DEV, R, C), lambda i, s: (s[2], i, 0)
    return pl.pallas_call(
        body, name=name, out_shape=_S(shape, BF16),
        grid_spec=pltpu.PrefetchScalarGridSpec(
            num_scalar_prefetch=1, grid=(R // tr,),
            in_specs=[pl.BlockSpec((tr, C), lambda i, s: (i, 0))],
            out_specs=pl.BlockSpec((None, tr, C), out_map)),
        compiler_params=_cparams(("parallel",)),
    )(place, w)


def _down(cur, prev8, j):
    return pltpu.roll(jnp.concatenate([prev8, cur], axis=0), j, 0)[SUB:, :]


def _up(cur, next8, j):
    n = cur.shape[0] + SUB
    return pltpu.roll(jnp.concatenate([cur, next8], axis=0), n - j, 0)[:cur.shape[0], :]


def _shifted_down(x, prev8, n):
    full = jnp.concatenate([prev8, x], axis=0)
    return [x] + [pltpu.roll(full, s, 0)[SUB:, :] for s in range(1, n)]


def _shifted_up(x, next8, n):
    m = x.shape[0] + SUB
    full = jnp.concatenate([x, next8], axis=0)
    return [x] + [pltpu.roll(full, m - s, 0)[:x.shape[0], :] for s in range(1, n)]


def _taps(sh, w_ref):
    kw = w_ref.shape[0]
    y = sh[0] * w_ref[pl.ds(kw - 1, 1), :]
    for k in range(kw - 1):
        y = y + sh[kw - 1 - k] * w_ref[pl.ds(k, 1), :]
    return y


def _conv(x, prev8, w_ref):
    return _taps(_shifted_down(x, prev8, w_ref.shape[0]), w_ref)


def _conv_t(dy, next8, w_ref):
    return _taps(_shifted_up(dy, next8, w_ref.shape[0]), w_ref)


def _conv_dw(dw_ref, dy, x, prev8, first):
    kw = dw_ref.shape[0]

    @pl.when(first)
    def _():
        dw_ref[...] = jnp.zeros_like(dw_ref)

    for k in range(kw):
        xs = x if k == kw - 1 else _down(x, prev8, kw - 1 - k)
        dw_ref[pl.ds(k, 1), :] += jnp.sum(dy * xs, axis=0, keepdims=True)


def _acc(ref, val, first):
    @pl.when(first)
    def _():
        ref[...] = jnp.zeros_like(ref)

    ref[...] += val


def _acc_row(ref, val, first):
    _acc(ref, jnp.sum(val, axis=0, keepdims=True), first)


def _prev8(h_ref, t):
    return jnp.where(t > 0, h_ref[...].astype(F32)[HALO - SUB:, :], 0.0)


def _next8(h_ref, is_last):
    return jnp.where(is_last, 0.0, h_ref[...].astype(F32)[:SUB, :])


_GELU_K0 = math.sqrt(2.0 / math.pi)
_GELU_K1 = 0.044715


def _gelu_and_grad(x):
    x2 = x * x
    th = jnp.tanh(_GELU_K0 * x * (1.0 + _GELU_K1 * x2))
    g = 0.5 * x * (1.0 + th)
    dg = 0.5 * (1.0 + th) + 0.5 * x * (1.0 - th * th) * (_GELU_K0 * (1.0 + 3.0 * _GELU_K1 * x2))
    return g, dg


def _neg_expm1(z):
    series = -z * (1.0 + z * (0.5 + z * (1.0 / 6.0 + z * (1.0 / 24.0))))
    return jnp.where(z > -0.03, series, 1.0 - jnp.exp(z))


def _store_staged(stages, dst_hbm, sems, step, n_steps, where):
    def copies(s, slot):
        return [pltpu.make_async_copy(
            st.at[slot], dst_hbm.at[pl.ds(r0, st.shape[1]), pl.ds(c0, st.shape[2])], sems.at[slot, k])
            for k, (st, (r0, c0)) in enumerate(zip(stages, where(s)))]

    slot = step % 2

    @pl.when(step > 0)
    def _():
        for cp in copies(step - 1, 1 - slot):
            cp.wait()

    for cp in copies(step, slot):
        cp.start()

    @pl.when(step == n_steps - 1)
    def _():
        for cp in copies(step, slot):
            cp.wait()


def _halo_prev_map(hb, col_fn):
    return lambda c, t: (jnp.maximum(t * hb - 1, 0), col_fn(c))


def _rms_fwd(x, g, name):
    T, D = x.shape
    tb = _tile(T, TB, SUB)

    def body(x_ref, g_ref, o_ref):
        xv = x_ref[...]
        rstd = lax.rsqrt(jnp.mean(xv * xv, axis=-1, keepdims=True) + EPS)
        o_ref[...] = (xv * rstd * g_ref[...]).astype(BF16)

    return pl.pallas_call(
        body, name=name, out_shape=_S((T, D), BF16), grid=(T // tb,),
        in_specs=[pl.BlockSpec((tb, D), lambda i: (i, 0)), pl.BlockSpec((1, D), lambda i: (0, 0))],
        out_specs=pl.BlockSpec((tb, D), lambda i: (i, 0)),
        compiler_params=_cparams(("parallel",)),
    )(x, g.reshape(1, D))


def _rms_bwd(x, g, dh, dres, name, out_dtype):
    T, D = x.shape
    tb = _tile(T, 256, HALO)

    def body(x_ref, g_ref, dh_ref, dr_ref, dx_ref, dg_ref):
        i = pl.program_id(0)
        xv = x_ref[...]
        rstd = lax.rsqrt(jnp.mean(xv * xv, axis=-1, keepdims=True) + EPS)
        xn = xv * rstd
        dhv = dh_ref[...].astype(F32)
        _acc_row(dg_ref, dhv * xn, i == 0)
        dxn = dhv * g_ref[...]
        dx = dr_ref[...].astype(F32) + rstd * (dxn - xn * jnp.mean(dxn * xn, axis=-1, keepdims=True))
        dx_ref[...] = dx.astype(dx_ref.dtype)

    blk = pl.BlockSpec((tb, D), lambda i: (i, 0))
    vec = pl.BlockSpec((1, D), lambda i: (0, 0))
    return pl.pallas_call(
        body, name=name, out_shape=(_S((T, D), out_dtype), _S((1, D), F32)),
        grid=(T // tb,), in_specs=[blk, vec, blk, blk], out_specs=(blk, vec),
        compiler_params=_cparams(("arbitrary",)),
    )(x, g.reshape(1, D), dh, dres)


def _loss_head(x2, g, target, name):
    T, D = x2.shape
    tb = _tile(T, 256, HALO)

    def body(x_ref, g_ref, t_ref, dxb_ref, loss_ref, dg_ref):
        i = pl.program_id(0)
        xv = x_ref[...]
        rstd = lax.rsqrt(jnp.mean(xv * xv, axis=-1, keepdims=True) + EPS)
        xn = xv * rstd
        err = xn * g_ref[...] - t_ref[...]
        part = 0.5 * jnp.sum(jnp.mean(err * err, axis=-1, keepdims=True), axis=0, keepdims=True)
        part = jnp.broadcast_to(part, (1, LANES))
        _acc(loss_ref, part, i == 0)
        dy = err * (1.0 / D)
        _acc_row(dg_ref, dy * xn, i == 0)
        dxn = dy * g_ref[...]
        dxb_ref[...] = (rstd * (dxn - xn * jnp.mean(dxn * xn, axis=-1, keepdims=True))).astype(BF16)

    blk = pl.BlockSpec((tb, D), lambda i: (i, 0))
    vec = pl.BlockSpec((1, D), lambda i: (0, 0))
    return pl.pallas_call(
        body, name=name,
        out_shape=(_S((T, D), BF16), _S((1, LANES), F32), _S((1, D), F32)),
        grid=(T // tb,), in_specs=[blk, vec, blk],
        out_specs=(blk, pl.BlockSpec((1, LANES), lambda i: (0, 0)), vec),
        compiler_params=_cparams(("arbitrary",)),
    )(x2, g.reshape(1, D), target)


def _lru_gates(xc, wa_ref, ba_ref, wx_ref, bx_ref, lam_ref):
    xcb = xc.astype(BF16)
    r = jax.nn.sigmoid(jnp.dot(xcb, wa_ref[...], preferred_element_type=F32) + ba_ref[...])
    i = jax.nn.sigmoid(jnp.dot(xcb, wx_ref[...], preferred_element_type=F32) + bx_ref[...])
    sp = jax.nn.softplus(-lam_ref[...])
    log_a = (-LRU_C * sp) * r
    a = jnp.exp(log_a)
    s = jnp.sqrt(_neg_expm1(2.0 * log_a))
    return xcb, r, i, a, s


def _lru_fwd(p, conv_w, conv_b, wa_bd, ba, wx_bd, bx, lam, *, name, tasks=()):
    T = p.shape[0]
    d = lam.shape[-1]
    C = _tile(d, C_LRU)
    nC = d // C
    tb = _tile(T, TB, HALO)
    nT, hb, nt = T // tb, tb // HALO, tb // SUB

    def body(x_ref, xh_ref, g_ref, cw_ref, cb_ref, wa_ref, ba_ref, wx_ref, bx_ref, lam_ref,
             hs_ref, y_ref, a_s, u_s, h_s):
        t = pl.program_id(1)

        @pl.when(t == 0)
        def _():
            h_s[...] = jnp.zeros_like(h_s)

        x = x_ref[...].astype(F32)
        xc = _conv(x, _prev8(xh_ref, t), cw_ref) + cb_ref[...]
        _, r, i, a, s = _lru_gates(xc, wa_ref, ba_ref, wx_ref, bx_ref, lam_ref)
        a_s[...] = a
        u_s[...] = s * (i * xc)
        row = lax.broadcasted_iota(jnp.int32, (SUB, C), 0)

        def step(k, h):
            o = pl.multiple_of(k * SUB, SUB)
            A = a_s[pl.ds(o, SUB), :]
            B = u_s[pl.ds(o, SUB), :]
            for sh in (1, 2, 4):
                m = row >= sh
                Ap = pltpu.roll(A, sh, 0)
                Bp = pltpu.roll(B, sh, 0)
                B = jnp.where(m, A * Bp + B, B)
                A = jnp.where(m, A * Ap, A)
            hs = A * h + B
            hs_ref[pl.ds(o, SUB), :] = hs
            return jnp.broadcast_to(hs[SUB - 1:SUB, :], (SUB, C))

        h_s[...] = lax.fori_loop(0, nt, step, h_s[...])
        gel, _ = _gelu_and_grad(g_ref[...].astype(F32))
        y_ref[...] = (gel * hs_ref[...]).astype(BF16)

    vec = pl.BlockSpec((1, C), lambda c, t: (0, c))
    sq = pl.BlockSpec((None, C, C), lambda c, t: (c, 0, 0))
    outs, passed = _call(
        name, (nC, nT), body,
        [pl.BlockSpec((tb, C), lambda c, t: (t, c)),
         pl.BlockSpec((HALO, C), _halo_prev_map(hb, lambda c: c)),
         pl.BlockSpec((tb, C), lambda c, t: (t, nC + c)),
         pl.BlockSpec((conv_w.shape[0], C), lambda c, t: (0, c)),
         vec, sq, vec, sq, vec, vec],
        [p, p, p, conv_w, conv_b, wa_bd, ba, wx_bd, bx, lam],
        [_S((T, d), F32), _S((T, d), BF16)],
        [pl.BlockSpec((tb, C), lambda c, t: (t, c)), pl.BlockSpec((tb, C), lambda c, t: (t, c))],
        [pltpu.VMEM((tb, C), F32), pltpu.VMEM((tb, C), F32), pltpu.VMEM((SUB, C), F32)], tasks)
    return (*outs, passed) if tasks else outs


def _lru_bwd(p, hs, dyl, dp, conv_w, conv_b, wa_bd, ba, wx_bd, bx, lam, *, name, tasks=()):
    T = p.shape[0]
    d = lam.shape[-1]
    C = _tile(d, C_LRU)
    nC = d // C
    tb = _tile(T, TB, HALO)
    nT, hb, nt = T // tb, tb // HALO, tb // SUB
    kw = conv_w.shape[0]

    def body(x_ref, xh_ref, g_ref, hs_ref, hh_ref, dy_ref, cw_ref, cb_ref, wa_ref, ba_ref, wx_ref, bx_ref,
             lam_ref, dp_in, dp_ref, dcw_ref, dcb_ref, dwa_ref, dba_ref, dwx_ref, dbx_ref, dlam_ref,
             b_s, g_s, dh_s, an_s, dhn_s, dxn_s, st_x, st_g, sems):
        del dp_in
        c = pl.program_id(0)
        tr = pl.program_id(1)
        t = nT - 1 - tr
        first = tr == 0

        @pl.when(first)
        def _():
            an_s[...] = jnp.zeros_like(an_s)
            dhn_s[...] = jnp.zeros_like(dhn_s)
            dxn_s[...] = jnp.zeros_like(dxn_s)

        x = x_ref[...].astype(F32)
        xprev = _prev8(xh_ref, t)
        xc = _conv(x, xprev, cw_ref) + cb_ref[...]
        xcb, r, i, a, s = _lru_gates(xc, wa_ref, ba_ref, wx_ref, bx_ref, lam_ref)
        hsv = hs_ref[...]
        dy = dy_ref[...].astype(F32)
        gel, dgel = _gelu_and_grad(g_ref[...].astype(F32))
        step_no = c * nT + tr
        slot = step_no % 2
        st_g[slot] = (dy * hsv * dgel).astype(BF16)

        b_s[...] = _up(a, an_s[...], 1)
        g_s[...] = dy * gel
        row = lax.broadcasted_iota(jnp.int32, (SUB, C), 0)

        def step(k, carry):
            o = pl.multiple_of((nt - 1 - k) * SUB, SUB)
            B = b_s[pl.ds(o, SUB), :]
            G = g_s[pl.ds(o, SUB), :]
            for sh in (1, 2, 4):
                m = row < SUB - sh
                Bn = pltpu.roll(B, SUB - sh, 0)
                Gn = pltpu.roll(G, SUB - sh, 0)
                G = jnp.where(m, B * Gn + G, G)
                B = jnp.where(m, B * Bn, B)
            dh = B * carry + G
            dh_s[pl.ds(o, SUB), :] = dh
            return jnp.broadcast_to(dh[0:1, :], (SUB, C))

        dhn_s[...] = lax.fori_loop(0, nt, step, dhn_s[...])
        an_s[...] = a[:SUB, :]
        dh = dh_s[...]

        hprev = _down(hsv, jnp.where(t > 0, hh_ref[...][HALO - SUB:, :], 0.0), 1)
        d_a = dh * hprev
        ixc = i * xc
        d_s = dh * ixc
        d_i = dh * s * xc
        d_xc = dh * s * i
        d_l = d_a * a - d_s * (a * a) / s
        sp = jax.nn.softplus(-lam_ref[...])
        _acc_row(dlam_ref, d_l * r * (LRU_C * jax.nn.sigmoid(-lam_ref[...])), first)
        d_zr = (d_l * (-LRU_C * sp)) * r * (1.0 - r)
        d_zi = d_i * i * (1.0 - i)
        _acc_row(dba_ref, d_zr, first)
        _acc_row(dbx_ref, d_zi, first)
        d_zrb = d_zr.astype(BF16)
        d_zib = d_zi.astype(BF16)
        tn_dims = (((0,), (0,)), ((), ()))
        nt_dims = (((1,), (1,)), ((), ()))
        gwa = lax.dot_general(xcb, d_zrb, tn_dims, preferred_element_type=F32)
        gwx = lax.dot_general(xcb, d_zib, tn_dims, preferred_element_type=F32)
        _acc(dwa_ref, gwa, first)
        _acc(dwx_ref, gwx, first)
        d_xc = (d_xc + lax.dot_general(d_zrb, wa_ref[...], nt_dims, preferred_element_type=F32)
                + lax.dot_general(d_zib, wx_ref[...], nt_dims, preferred_element_type=F32))
        _acc_row(dcb_ref, d_xc, first)
        _conv_dw(dcw_ref, d_xc, x, xprev, first)
        st_x[slot] = _conv_t(d_xc, dxn_s[...], cw_ref).astype(BF16)
        dxn_s[...] = d_xc[:SUB, :]

        def where(s):
            row0, col0 = (nT - 1 - s % nT) * tb, (s // nT) * C
            return [(row0, col0), (row0, d + col0)]

        _store_staged([st_x, st_g], dp_ref, sems, step_no, nC * nT, where)

    rev = lambda c, tr: (nT - 1 - tr, c)
    vec = pl.BlockSpec((1, C), lambda c, tr: (0, c))
    sq = pl.BlockSpec((None, C, C), lambda c, tr: (c, 0, 0))
    cwb = pl.BlockSpec((kw, C), lambda c, tr: (0, c))
    halo_prev = lambda c, tr: (jnp.maximum((nT - 1 - tr) * hb - 1, 0), c)
    outs, passed = _call(
        name, (nC, nT), body,
        [pl.BlockSpec((tb, C), rev),
         pl.BlockSpec((HALO, C), halo_prev),
         pl.BlockSpec((tb, C), lambda c, tr: (nT - 1 - tr, nC + c)),
         pl.BlockSpec((tb, C), rev),
         pl.BlockSpec((HALO, C), halo_prev),
         pl.BlockSpec((tb, C), rev),
         cwb, vec, sq, vec, sq, vec, vec, ANY],
        [p, p, p, hs, hs, dyl, conv_w, conv_b, wa_bd, ba, wx_bd, bx, lam, dp],
        [_S(dp.shape, dp.dtype), _S((kw, d), F32), _S((1, d), F32), _S((nC, C, C), F32), _S((1, d), F32),
         _S((nC, C, C), F32), _S((1, d), F32), _S((1, d), F32)],
        [ANY, cwb, vec, sq, vec, sq, vec, vec],
        [pltpu.VMEM((tb, C), F32), pltpu.VMEM((tb, C), F32), pltpu.VMEM((tb, C), F32),
         pltpu.VMEM((SUB, C), F32), pltpu.VMEM((SUB, C), F32), pltpu.VMEM((SUB, C), F32),
         pltpu.VMEM((2, tb, C), BF16), pltpu.VMEM((2, tb, C), BF16), pltpu.SemaphoreType.DMA((2, 2))],
        tasks, own_aliases={13: 0})
    return (*outs, passed) if tasks else outs


def _sc_fwd(p, conv_w, *, d, name):
    T = p.shape[0]
    C = _tile(d, C_EW)
    nC = d // C
    tb = _tile(T, TB, HALO)
    nT, hb = T // tb, tb // HALO

    def body(b_ref, c_ref, ch_ref, v_ref, vh_ref, w_ref, y_ref):
        t = pl.program_id(1)
        cv = c_ref[...].astype(F32) * v_ref[...].astype(F32)
        cvp = _prev8(ch_ref, t) * _prev8(vh_ref, t)
        y_ref[...] = (b_ref[...].astype(F32) * _conv(cv, cvp, w_ref)).astype(BF16)

    seg = lambda k: pl.BlockSpec((tb, C), lambda c, t: (t, k * nC + c))
    hseg = lambda k: pl.BlockSpec((HALO, C), _halo_prev_map(hb, lambda c: k * nC + c))
    return pl.pallas_call(
        body, name=name, out_shape=_S((T, d), BF16), grid=(nC, nT),
        in_specs=[seg(2), seg(3), hseg(3), seg(4), hseg(4), pl.BlockSpec((conv_w.shape[0], C), lambda c, t: (0, c))],
        out_specs=pl.BlockSpec((tb, C), lambda c, t: (t, c)),
        compiler_params=_cparams(("parallel", "parallel")),
    )(p, p, p, p, p, conv_w)


def _sc_bwd(p, dys, dp, conv_w, *, d, name, tasks=()):
    T = p.shape[0]
    C = _tile(d, C_EW)
    nC = d // C
    tb = _tile(T, TB, HALO)
    nT, hb = T // tb, tb // HALO
    kw = conv_w.shape[0]

    def body(b_ref, bn_ref, c_ref, ch_ref, v_ref, vh_ref, dy_ref, dyn_ref, w_ref, dp_in, dp_ref, dw_ref,
             st_b, st_c, st_v, sems):
        del dp_in
        c = pl.program_id(0)
        t = pl.program_id(1)
        last = t == nT - 1
        bv = b_ref[...].astype(F32)
        cvv = c_ref[...].astype(F32)
        vv = v_ref[...].astype(F32)
        dy = dy_ref[...].astype(F32)
        cv = cvv * vv
        cvp = _prev8(ch_ref, t) * _prev8(vh_ref, t)
        step_no = c * nT + t
        slot = step_no % 2
        st_b[slot] = (dy * _conv(cv, cvp, w_ref)).astype(BF16)
        dz = dy * bv
        dzn = _next8(dyn_ref, last) * _next8(bn_ref, last)
        _conv_dw(dw_ref, dz, cv, cvp, t == 0)
        dcv = _conv_t(dz, dzn, w_ref)
        st_c[slot] = (dcv * vv).astype(BF16)
        st_v[slot] = (dcv * cvv).astype(BF16)

        def where(s):
            return [((s % nT) * tb, (2 + k) * d + (s // nT) * C) for k in range(3)]

        _store_staged([st_b, st_c, st_v], dp_ref, sems, step_no, nC * nT, where)

    seg = lambda k: pl.BlockSpec((tb, C), lambda c, t: (t, k * nC + c))
    hseg = lambda k: pl.BlockSpec((HALO, C), _halo_prev_map(hb, lambda c: k * nC + c))
    last_h = T // HALO - 1
    nseg = lambda k: pl.BlockSpec((HALO, C), lambda c, t: (jnp.minimum((t + 1) * hb, last_h), k * nC + c))
    outs, passed = _call(
        name, (nC, nT), body,
        [seg(2), nseg(2), seg(3), hseg(3), seg(4), hseg(4),
         pl.BlockSpec((tb, C), lambda c, t: (t, c)), nseg(0),
         pl.BlockSpec((kw, C), lambda c, t: (0, c)), ANY],
        [p, p, p, p, p, p, dys, dys, conv_w, dp],
        [_S(dp.shape, dp.dtype), _S((kw, d), F32)], [ANY, pl.BlockSpec((kw, C), lambda c, t: (0, c))],
        [pltpu.VMEM((2, tb, C), BF16)] * 3 + [pltpu.SemaphoreType.DMA((2, 3))], tasks, own_aliases={9: 0})
    return (*outs, passed) if tasks else outs


def _merge_fwd(p, y_lru, y_sc, *, col0, name, tasks=()):
    T, D = y_lru.shape
    C = _tile(math.gcd(D, col0), 1024)
    nC = D // C
    k0 = col0 // C
    tb = _tile(T, TB, HALO)

    def body(gl_ref, gs_ref, yl_ref, ys_ref, o_ref):
        @pl.loop(0, tb // HALO)
        def _(k):
            rows = pl.ds(pl.multiple_of(k * HALO, HALO), HALO)
            for l0 in range(0, C, min(C, C_EW)):
                at = (rows, pl.ds(l0, min(C, C_EW)))
                o_ref[at] = (jax.nn.sigmoid(gl_ref[at].astype(F32)) * yl_ref[at].astype(F32)
                             + jax.nn.sigmoid(gs_ref[at].astype(F32)) * ys_ref[at].astype(F32)).astype(BF16)

    blk = pl.BlockSpec((tb, C), lambda c, t: (t, c))
    outs, passed = _call(
        name, (nC, T // tb), body,
        [pl.BlockSpec((tb, C), lambda c, t: (t, k0 + c)), pl.BlockSpec((tb, C), lambda c, t: (t, k0 + nC + c)),
         blk, blk], [p, p, y_lru, y_sc], [_S((T, D), BF16)], [blk], [], tasks)
    return (outs[0], passed) if tasks else outs[0]


def _merge_bwd(p, y_lru, y_sc, dm, *, col0, name, tasks=()):
    T, D = y_lru.shape
    C = _tile(math.gcd(D, col0), 1024)
    nC = D // C
    k0 = col0 // C
    tb = _tile(T, TB, HALO)
    nT = T // tb

    def body(gl_ref, gs_ref, yl_ref, ys_ref, dm_ref, dp_ref, dyl_ref, dys_ref, st_l, st_s, sems):
        step_no = pl.program_id(0) * nT + pl.program_id(1)
        slot = step_no % 2

        @pl.loop(0, tb // HALO)
        def _(k):
            rows = pl.ds(pl.multiple_of(k * HALO, HALO), HALO)
            for l0 in range(0, C, min(C, C_EW)):
                at = (rows, pl.ds(l0, min(C, C_EW)))
                dmv = dm_ref[at].astype(F32)
                sl = jax.nn.sigmoid(gl_ref[at].astype(F32))
                ss = jax.nn.sigmoid(gs_ref[at].astype(F32))
                dyl_ref[at] = (dmv * sl).astype(BF16)
                dys_ref[at] = (dmv * ss).astype(BF16)
                st_l[(slot,) + at] = (dmv * yl_ref[at].astype(F32) * sl * (1.0 - sl)).astype(BF16)
                st_s[(slot,) + at] = (dmv * ys_ref[at].astype(F32) * ss * (1.0 - ss)).astype(BF16)

        def where(s):
            row0, colc = (s % nT) * tb, (s // nT) * C
            return [(row0, col0 + colc), (row0, col0 + D + colc)]

        _store_staged([st_l, st_s], dp_ref, sems, step_no, nC * nT, where)

    blk = pl.BlockSpec((tb, C), lambda c, t: (t, c))
    outs, passed = _call(
        name, (nC, nT), body,
        [pl.BlockSpec((tb, C), lambda c, t: (t, k0 + c)), pl.BlockSpec((tb, C), lambda c, t: (t, k0 + nC + c)),
         blk, blk, blk], [p, p, y_lru, y_sc, dm],
        [_S(p.shape, BF16), _S((T, D), BF16), _S((T, D), BF16)], [ANY, blk, blk],
        [pltpu.VMEM((2, tb, C), BF16), pltpu.VMEM((2, tb, C), BF16), pltpu.SemaphoreType.DMA((2, 2))], tasks)
    return (*outs, passed) if tasks else outs


def _ffn_act_fwd(uu, conv_w, *, name, tasks=()):
    T = uu.shape[0]
    F = uu.shape[1] // 2
    C = _tile(F, C_EW)
    nC = F // C
    tb = _tile(T, TB_CHUNKED, HALO)
    nT, hb = T // tb, tb // HALO
    kw = conv_w.shape[0]
    R = HALO

    def body(g_ref, gh_ref, v_ref, vh_ref, wg_ref, wv_ref, o_ref):
        t = pl.program_id(1)

        def chunk(k, carry):
            gp, vp = carry
            r0 = pl.multiple_of(k * R, R)
            ug = g_ref[pl.ds(r0, R), :].astype(F32)
            uv = v_ref[pl.ds(r0, R), :].astype(F32)
            cg = _conv(ug, gp, wg_ref)
            cv = _conv(uv, vp, wv_ref)
            o_ref[pl.ds(r0, R), :] = (cg * jax.nn.sigmoid(cg) * cv).astype(BF16)
            return ug[R - SUB:, :], uv[R - SUB:, :]

        lax.fori_loop(0, tb // R, chunk, (_prev8(gh_ref, t), _prev8(vh_ref, t)))

    seg = lambda k: pl.BlockSpec((tb, C), lambda c, t: (t, k * nC + c))
    hseg = lambda k: pl.BlockSpec((HALO, C), _halo_prev_map(hb, lambda c: k * nC + c))
    wseg = lambda k: pl.BlockSpec((kw, C), lambda c, t: (0, k * nC + c))
    outs, passed = _call(
        name, (nC, nT), body, [seg(0), hseg(0), seg(1), hseg(1), wseg(0), wseg(1)],
        [uu, uu, uu, uu, conv_w, conv_w], [_S((T, F), BF16)], [pl.BlockSpec((tb, C), lambda c, t: (t, c))], [], tasks)
    return (outs[0], passed) if tasks else outs[0]


def _ffn_act_bwd(uu, dact, conv_w, *, name):
    T = uu.shape[0]
    F = uu.shape[1] // 2
    C = _tile(F, C_EW)
    nC = F // C
    tb = _tile(T, TB_CHUNKED, HALO)
    nT, hb = T // tb, tb // HALO
    kw = conv_w.shape[0]
    R = HALO
    nk = tb // R

    def body(g_ref, gh_ref, v_ref, vh_ref, da_ref, wg_ref, wv_ref, du_ref, dwg_ref, dwv_ref,
             gn_s, vn_s, accg_s, accv_s, st_g, st_v, sems):
        c = pl.program_id(0)
        tr = pl.program_id(1)
        t = nT - 1 - tr
        first = tr == 0

        @pl.when(first)
        def _():
            gn_s[...] = jnp.zeros_like(gn_s)
            vn_s[...] = jnp.zeros_like(vn_s)
            dwg_ref[...] = jnp.zeros_like(dwg_ref)
            dwv_ref[...] = jnp.zeros_like(dwv_ref)

        accg_s[...] = jnp.zeros_like(accg_s)
        accv_s[...] = jnp.zeros_like(accv_s)
        step_no = c * nT + tr
        slot = step_no % 2

        def chunk(i, carry):
            gn, vn = carry
            k = nk - 1 - i
            r0 = pl.multiple_of(k * R, R)
            rp = pl.multiple_of(jnp.maximum(r0 - R, 0), R)
            ug = g_ref[pl.ds(r0, R), :].astype(F32)
            uv = v_ref[pl.ds(r0, R), :].astype(F32)
            gp = jnp.where(k > 0, g_ref[pl.ds(rp, R), :].astype(F32)[R - SUB:, :], _prev8(gh_ref, t))
            vp = jnp.where(k > 0, v_ref[pl.ds(rp, R), :].astype(F32)[R - SUB:, :], _prev8(vh_ref, t))
            sh_g = _shifted_down(ug, gp, kw)
            sh_v = _shifted_down(uv, vp, kw)
            cg = _taps(sh_g, wg_ref)
            cv = _taps(sh_v, wv_ref)
            da = da_ref[pl.ds(r0, R), :].astype(F32)
            sg = jax.nn.sigmoid(cg)
            d_cg = da * cv * (sg * (1.0 + cg * (1.0 - sg)))
            d_cv = da * (cg * sg)
            for j in range(kw):
                accg_s[j] += d_cg * sh_g[kw - 1 - j]
                accv_s[j] += d_cv * sh_v[kw - 1 - j]
            st_g[slot, pl.ds(r0, R), :] = _conv_t(d_cg, gn, wg_ref).astype(BF16)
            st_v[slot, pl.ds(r0, R), :] = _conv_t(d_cv, vn, wv_ref).astype(BF16)
            return d_cg[:SUB, :], d_cv[:SUB, :]

        gn, vn = lax.fori_loop(0, nk, chunk, (gn_s[...], vn_s[...]))
        gn_s[...] = gn
        vn_s[...] = vn
        for j in range(kw):
            dwg_ref[pl.ds(j, 1), :] += jnp.sum(accg_s[j], axis=0, keepdims=True)
            dwv_ref[pl.ds(j, 1), :] += jnp.sum(accv_s[j], axis=0, keepdims=True)
        def where(s):
            row0, col0 = (nT - 1 - s % nT) * tb, (s // nT) * C
            return [(row0, col0), (row0, F + col0)]

        _store_staged([st_g, st_v], du_ref, sems, step_no, nC * nT, where)

    seg = lambda k: pl.BlockSpec((tb, C), lambda c, tr: (nT - 1 - tr, k * nC + c))
    hseg = lambda k: pl.BlockSpec((HALO, C), lambda c, tr: (jnp.maximum((nT - 1 - tr) * hb - 1, 0), k * nC + c))
    wseg = lambda k: pl.BlockSpec((kw, C), lambda c, tr: (0, k * nC + c))
    dwb = pl.BlockSpec((kw, C), lambda c, tr: (0, c))
    return pl.pallas_call(
        body, name=name, out_shape=(_S(uu.shape, BF16), _S((kw, F), F32), _S((kw, F), F32)), grid=(nC, nT),
        in_specs=[seg(0), hseg(0), seg(1), hseg(1), pl.BlockSpec((tb, C), lambda c, tr: (nT - 1 - tr, c)),
                  wseg(0), wseg(1)],
        out_specs=(ANY, dwb, dwb),
        scratch_shapes=[pltpu.VMEM((SUB, C), F32), pltpu.VMEM((SUB, C), F32),
                        pltpu.VMEM((kw, R, C), F32), pltpu.VMEM((kw, R, C), F32),
                        pltpu.VMEM((2, tb, C), BF16), pltpu.VMEM((2, tb, C), BF16), pltpu.SemaphoreType.DMA((2, 2))],
        compiler_params=_cparams(("arbitrary", "arbitrary")),
    )(uu, uu, uu, uu, dact, conv_w, conv_w)


def _place():
    x, y, c = lax.axis_index("x"), lax.axis_index("y"), lax.axis_index("c")
    return x, y, c


def _chips(x, y):
    return [(1 - x, y), (x, 1 - y), (1 - x, 1 - y)]


def _all_gather(arrays, placed, over_ici, pair_n, name):
    n = len(arrays)

    def body(*refs):
        ins, outs = refs[:n], refs[n:2 * n]
        send_sems, recv_sems, local_sems = refs[2 * n:]
        x, y, c = _place()
        me, sibling = (x, y, c), (x, y, 1 - c)
        chips = _chips(x, y)
        full = [a for a in range(n) if over_ici[a]]

        def idx(px, py, pc):
            return 4 * px + 2 * py + pc

        def copy(a, k, block, to):
            dst = _dev_block(outs[a], idx(*block), pair_n[a])
            src = ins[a] if (block is me and not placed[a]) else dst
            return pltpu.make_async_remote_copy(
                src_ref=src, dst_ref=dst, send_sem=send_sems.at[a, k], recv_sem=recv_sems.at[a, k],
                device_id=to, device_id_type=MESH)

        def half(a, k, block, to, lo):
            r = rows_of[a] // 2
            blk = _rows_of(outs[a], idx(*block), (0 if lo else r, r), pair_n[a])
            return pltpu.make_async_remote_copy(
                src_ref=blk, dst_ref=blk, send_sem=send_sems.at[a, k], recv_sem=recv_sems.at[a, k],
                device_id=to, device_id_type=MESH)

        mine = [pltpu.make_async_copy(ins[a], outs[a].at[idx(*me)], local_sems.at[a])
                for a in range(n) if not placed[a]]
        for cp in mine:
            cp.start()
        chip_x, chip_y, chip_d = chips
        sent = []
        for a in full:
            sent += [copy(a, 1, me, (*chip_x, c)), copy(a, 2, me, (*chip_y, c))]
            if not relay[a]:
                sent.append(copy(a, 3, me, (*chip_d, c)))
        for a in range(n):
            sent.append(copy(a, 0, me, sibling))
        for cp in sent:
            cp.start()

        def then(cp):
            cp.start()
            sent.append(cp)

        for a in full:
            copy(a, 2, (*chip_y, c), me).wait_recv()
            if relay[a]:
                then(half(a, 3, (*chip_y, c), (*chip_x, c), True))
            then(copy(a, 6, (*chip_y, c), sibling))
            copy(a, 1, (*chip_x, c), me).wait_recv()
            if relay[a]:
                then(half(a, 4, (*chip_x, c), (*chip_y, c), False))
            then(copy(a, 5, (*chip_x, c), sibling))
        for a in full:
            if relay[a]:
                half(a, 3, (*chip_d, c), me, True).wait_recv()
                half(a, 4, (*chip_d, c), me, False).wait_recv()
            else:
                copy(a, 3, (*chip_d, c), me).wait_recv()
            then(copy(a, 7, (*chip_d, c), sibling))
        for a in range(n):
            copy(a, 0, sibling, me).wait_recv()
        for a in full:
            for j, chip in enumerate(chips):
                copy(a, 5 + j, (*chip, 1 - c), me).wait_recv()
        for cp in sent:
            cp.wait_send()
        for cp in mine:
            cp.wait()

    rows_of = [(s.shape[1] if placed[a] else s.shape[0]) for a, s in enumerate(arrays)]
    relay = [r % (2 * HALO) == 0 for r in rows_of]
    return pl.pallas_call(
        body, name=name,
        out_shape=tuple(_S(s.shape if placed[a] else (N_DEV,) + s.shape, s.dtype) for a, s in enumerate(arrays)),
        in_specs=[ANY] * n, out_specs=tuple([ANY] * n),
        scratch_shapes=[pltpu.SemaphoreType.DMA((n, 8)), pltpu.SemaphoreType.DMA((n, 8)),
                        pltpu.SemaphoreType.DMA((n,))],
        input_output_aliases={a: a for a in range(n) if placed[a]},
    )(*arrays)


def _dev_block(ref, dev, pair_n=None):
    if pair_n is None:
        return ref.at[dev]
    return ref.at[dev // 2, :, pl.ds(pl.multiple_of((dev % 2) * pair_n, LANES), pair_n)]


def _rows_of(ref, blk, rows, pair_n=None):
    v = _dev_block(ref, blk, pair_n)
    return v if rows is None else v.at[pl.ds(rows[0], rows[1])]


ALL_ROWS = "all"


def _gather_task(buf, ici=None, fwd=None, pair_n=None):
    blk_of = functools.partial(_rows_of, pair_n=pair_n)
    r0, nr = (0, buf.shape[1]) if ici == ALL_ROWS else (ici or (0, 0))
    assert nr % (2 * HALO) == 0
    lo, hi = (r0, nr // 2), (r0 + nr // 2, nr // 2)
    both = (r0, nr)
    fwd_rows = None if fwd == ALL_ROWS else fwd

    def remote(refs, ss, rs, k, dev, rows, to):
        blk = blk_of(refs[0], dev, rows)
        return pltpu.make_async_remote_copy(src_ref=blk, dst_ref=blk, send_sem=ss.at[k], recv_sem=rs.at[k],
                                            device_id=to, device_id_type=MESH)

    def waves(refs, ss, rs):
        x, y, c = _place()
        me = 4 * x + 2 * y + c
        (xx, xy), (yx, yy), _ = _chips(x, y)
        dev_x, dev_y = 4 * xx + 2 * xy + c, 4 * yx + 2 * yy + c
        first, second = [], []
        if ici is not None:
            first += [remote(refs, ss, rs, 0, me, both, (xx, xy, c)), remote(refs, ss, rs, 1, me, both, (yx, yy, c))]
            second += [remote(refs, ss, rs, 2, dev_y, lo, (xx, xy, c)), remote(refs, ss, rs, 3, dev_x, hi, (yx, yy, c))]
        if fwd is not None:
            first += [remote(refs, ss, rs, 4 + j, 4 * px + 2 * py + c, fwd_rows, (x, y, 1 - c))
                      for j, (px, py) in enumerate(_chips(x, y))]
        return first, second

    def start(refs, ss, rs, ls):
        for cp in waves(refs, ss, rs)[0]:
            cp.start()

    def mid(refs, ss, rs, ls):
        x, y, c = _place()
        (xx, xy), (yx, yy), _ = _chips(x, y)
        remote(refs, ss, rs, 0, 4 * xx + 2 * xy + c, both, (x, y, c)).wait_recv()
        remote(refs, ss, rs, 1, 4 * yx + 2 * yy + c, both, (x, y, c)).wait_recv()
        for cp in waves(refs, ss, rs)[1]:
            cp.start()

    def wait(refs, ss, rs, ls):
        x, y, c = _place()
        chips = _chips(x, y)
        if ici is not None:
            dev_d = 4 * chips[2][0] + 2 * chips[2][1] + c
            remote(refs, ss, rs, 2, dev_d, lo, (x, y, c)).wait_recv()
            remote(refs, ss, rs, 3, dev_d, hi, (x, y, c)).wait_recv()
        if fwd is not None:
            for j, (px, py) in enumerate(chips):
                remote(refs, ss, rs, 4 + j, 4 * px + 2 * py + 1 - c, fwd_rows, (x, y, c)).wait_recv()
        first, second = waves(refs, ss, rs)
        for cp in first + second:
            cp.wait_send()

    return _Task([buf], [0], start, wait, nsem=7, mid=mid if ici is not None else None)


def _exchange_task(parts, landing, rows=None):
    def copies(refs, ss, rs):
        x, y, c = _place()
        myq = 2 * x + y
        return [pltpu.make_async_remote_copy(
            src_ref=_rows_of(refs[0], 2 * px + py, rows), dst_ref=_rows_of(refs[1], myq, rows),
            send_sem=ss.at[k], recv_sem=rs.at[k], device_id=(px, py, c), device_id_type=MESH)
            for k, (px, py) in enumerate(_chips(x, y))]

    def start(refs, ss, rs, ls):
        for cp in copies(refs, ss, rs):
            cp.start()

    def wait(refs, ss, rs, ls):
        x, y, c = _place()
        for k, (px, py) in enumerate(_chips(x, y)):
            pltpu.make_async_remote_copy(
                src_ref=_rows_of(refs[0], 2 * x + y, rows), dst_ref=_rows_of(refs[1], 2 * px + py, rows),
                send_sem=ss.at[k], recv_sem=rs.at[k], device_id=(px, py, c), device_id_type=MESH).wait_recv()
        for cp in copies(refs, ss, rs):
            cp.wait_send()

    return _Task([parts, landing], [1], start, wait)


def _core_blocks(g, pair_n):
    if pair_n is None:
        g4 = g.reshape((N_CHIP, 2) + g.shape[1:])
        return g4, (N_CHIP,) + g.shape[1:], lambda ref, c: ref.at[:, c]
    view = lambda ref, c: ref.at[:, :, pl.ds(pl.multiple_of(c * pair_n, LANES), pair_n)]
    return g, (N_CHIP, g.shape[1], pair_n), view


def _swap_task(g, pair_n=None):
    g4, shape, view = _core_blocks(g, pair_n)

    def copy(refs, ss, rs):
        x, y, c = _place()
        return pltpu.make_async_remote_copy(
            src_ref=view(refs[0], 1 - c), dst_ref=refs[1], send_sem=ss.at[0], recv_sem=rs.at[0],
            device_id=(x, y, 1 - c), device_id_type=MESH)

    def start(refs, ss, rs, ls):
        copy(refs, ss, rs).start()

    def wait(refs, ss, rs, ls):
        copy(refs, ss, rs).wait()

    return _Task([g4], [], start, wait, fresh=[_S(shape, g.dtype)], nsem=1)


def _peer(x, y, c, m):
    return x ^ (m >> 2), y ^ ((m >> 1) & 1), c ^ (m & 1)


def _bcast_task(pack):
    def copies(refs, ss, rs):
        x, y, c = _place()
        me = 4 * x + 2 * y + c
        return [pltpu.make_async_remote_copy(
            src_ref=refs[0], dst_ref=refs[1].at[me], send_sem=ss.at[m - 1], recv_sem=rs.at[m - 1],
            device_id=_peer(x, y, c, m), device_id_type=MESH) for m in range(1, N_DEV)]

    def local(refs, ls):
        x, y, c = _place()
        return pltpu.make_async_copy(refs[0], refs[1].at[4 * x + 2 * y + c], ls.at[0])

    def start(refs, ss, rs, ls):
        local(refs, ls).start()
        for cp in copies(refs, ss, rs):
            cp.start()

    def wait(refs, ss, rs, ls):
        x, y, c = _place()
        for m in range(1, N_DEV):
            px, py, pc = _peer(x, y, c, m)
            pltpu.make_async_remote_copy(
                src_ref=refs[0], dst_ref=refs[1].at[4 * px + 2 * py + pc], send_sem=ss.at[m - 1],
                recv_sem=rs.at[m - 1], device_id=(px, py, pc), device_id_type=MESH).wait_recv()
        for cp in copies(refs, ss, rs):
            cp.wait_send()
        local(refs, ls).wait()

    return _Task([pack], [], start, wait, fresh=[_S((N_DEV,) + pack.shape, pack.dtype)], nsem=N_DEV - 1)


def _sum_packs(packs, name):
    _, R, L = packs.shape

    def body(p_ref, o_ref):
        acc = p_ref[0]
        for k in range(1, N_DEV):
            acc = acc + p_ref[k]
        o_ref[...] = acc

    return pl.pallas_call(body, name=name, out_shape=_S((R, L), packs.dtype), in_specs=[VMEM_SPEC],
                          out_specs=VMEM_SPEC, compiler_params=_cparams())(packs)


def _swap_halves(g, name, pair_n=None):
    g4, shape, view = _core_blocks(g, pair_n)

    def body(g_ref, o_ref, send_sem, recv_sem):
        x, y, c = _place()
        cp = pltpu.make_async_remote_copy(
            src_ref=view(g_ref, 1 - c), dst_ref=o_ref, send_sem=send_sem, recv_sem=recv_sem,
            device_id=(x, y, 1 - c), device_id_type=MESH)
        cp.start()
        cp.wait()

    return pl.pallas_call(
        body, name=name, out_shape=_S(shape, g.dtype), in_specs=[ANY], out_specs=ANY,
        scratch_shapes=[pltpu.SemaphoreType.DMA, pltpu.SemaphoreType.DMA],
    )(g4)


def _add_halves(g, landed, place, name, pair_n=None):
    _, r, cc = landed.shape
    tr = _tile(r, 512, HALO)
    if pair_n is None:
        g4 = g.reshape(N_CHIP, 2, r, cc)
        g_spec = pl.BlockSpec((None, None, tr, cc), lambda i, q, s: (q, s[0], i, 0))
    else:
        g4 = g
        g_spec = pl.BlockSpec((None, tr, cc), lambda i, q, s: (q, i, s[0]))

    def body(s_ref, g_ref, l_ref, o_ref, land_ref):
        q = pl.program_id(1)
        v = (g_ref[...].astype(F32) + l_ref[...].astype(F32)).astype(BF16)
        o_ref[...] = v

        @pl.when(q == s_ref[1])
        def _():
            land_ref[...] = v

    return pl.pallas_call(
        body, name=name, out_shape=(_S((N_CHIP, r, cc), BF16), _S((N_CHIP, r, cc), BF16)),
        grid_spec=pltpu.PrefetchScalarGridSpec(
            num_scalar_prefetch=1, grid=(r // tr, N_CHIP),
            in_specs=[g_spec,
                      pl.BlockSpec((None, tr, cc), lambda i, q, s: (q, i, 0))],
            out_specs=(pl.BlockSpec((None, tr, cc), lambda i, q, s: (q, i, 0)),
                       pl.BlockSpec((None, tr, cc), lambda i, q, s: (s[1], i, 0)))),
        compiler_params=_cparams(("arbitrary", "arbitrary")),
    )(place, g4, landed)


def _all_reduce_small(pack, name):
    R = pack.shape[0]

    def body(p_ref, o_ref, buf, send_sems, recv_sems):
        x, y, c = _place()
        me = 4 * x + 2 * y + c
        buf[me] = p_ref[...]
        cps = []
        for k in range(N_DEV - 1):
            m = k + 1
            peer = (x ^ (m >> 2), y ^ ((m >> 1) & 1), c ^ (m & 1))
            cps.append(pltpu.make_async_remote_copy(
                src_ref=p_ref, dst_ref=buf.at[me], send_sem=send_sems.at[k], recv_sem=recv_sems.at[k],
                device_id=peer, device_id_type=MESH))
        for cp in cps:
            cp.start()
        for k in range(N_DEV - 1):
            m = k + 1
            peer_idx = 4 * (x ^ (m >> 2)) + 2 * (y ^ ((m >> 1) & 1)) + (c ^ (m & 1))
            pltpu.make_async_remote_copy(
                src_ref=p_ref, dst_ref=buf.at[peer_idx], send_sem=send_sems.at[k], recv_sem=recv_sems.at[k],
                device_id=(x, y, c), device_id_type=MESH).wait_recv()
        for cp in cps:
            cp.wait_send()
        acc = buf[0]
        for k in range(1, N_DEV):
            acc = acc + buf[k]
        o_ref[...] = acc

    return pl.pallas_call(
        body, name=name, out_shape=_S((R, LANES), F32),
        in_specs=[VMEM_SPEC], out_specs=VMEM_SPEC,
        scratch_shapes=[pltpu.VMEM((N_DEV, R, LANES), F32), pltpu.SemaphoreType.DMA((N_DEV - 1,)),
                        pltpu.SemaphoreType.DMA((N_DEV - 1,))],
        compiler_params=_cparams(),
    )(pack)


def _adamw_math(w, g, m, v):
    m = ADAM_B1 * m + (1.0 - ADAM_B1) * g
    v = ADAM_B2 * v + (1.0 - ADAM_B2) * (g * g)
    m_hat = m / (1.0 - ADAM_B1 ** ADAM_STEP)
    v_hat = v / (1.0 - ADAM_B2 ** ADAM_STEP)
    delta = -ADAM_LR * (m_hat / (jnp.sqrt(v_hat) + ADAM_EPS) + ADAM_WD * w)
    return delta, m, v


def _adamw_block(p_ref, w_ref, m_ref, v_ref, g_ref, d_ref, nm_ref, nv_ref):
    g = p_ref[0].astype(F32)
    for q in range(1, N_CHIP):
        g = g + p_ref[q].astype(F32)
    g_ref[...] = g
    d_ref[...], nm_ref[...], nv_ref[...] = _adamw_math(w_ref[...], g, m_ref[...], v_ref[...])


def _adamw_big(parts, w, m, v, name):
    r, cc = w.shape
    tr = _tile(r, 128, HALO)
    body = functools.partial(_adamw_block)

    blk = pl.BlockSpec((tr, cc), lambda i: (i, 0))
    return pl.pallas_call(
        body, name=name, out_shape=tuple(_S((r, cc), F32) for _ in range(4)), grid=(r // tr,),
        in_specs=[pl.BlockSpec((N_CHIP, tr, cc), lambda i: (0, i, 0)), blk, blk, blk],
        out_specs=(blk, blk, blk, blk), compiler_params=_cparams(("parallel",)),
    )(parts, w, m, v)


def _adamw_small(ws, gs, ms, vs, name):
    n = len(ws)

    def body(*refs):
        w_r, g_r, m_r, v_r = refs[:n], refs[n:2 * n], refs[2 * n:3 * n], refs[3 * n:4 * n]
        d_r, nm_r, nv_r = refs[4 * n:5 * n], refs[5 * n:6 * n], refs[6 * n:7 * n]
        for k in range(n):
            d_r[k][...], nm_r[k][...], nv_r[k][...] = _adamw_math(w_r[k][...], g_r[k][...], m_r[k][...], v_r[k][...])

    shapes = tuple(_S(w.shape, F32) for w in ws)
    outs = pl.pallas_call(
        body, name=name, out_shape=shapes * 3,
        in_specs=[VMEM_SPEC] * (4 * n), out_specs=tuple([VMEM_SPEC] * (3 * n)),
        compiler_params=_cparams(),
    )(*ws, *gs, *ms, *vs)
    return outs[:n], outs[n:2 * n], outs[2 * n:]


def _block_diag(w, heads_per_block):
    H, hd, _ = w.shape
    nb = H // heads_per_block
    eye = jnp.eye(heads_per_block, dtype=w.dtype)
    w4 = w.reshape(nb, heads_per_block, hd, hd)
    return jnp.einsum("nhab,hg->nhagb", w4, eye).reshape(nb, heads_per_block * hd, heads_per_block * hd)


def _diag_blocks(bd, heads_per_block, hd):
    nb = bd.shape[0]
    b5 = bd.reshape(nb, heads_per_block, hd, heads_per_block, hd)
    return jnp.stack([b5[:, h, :, h, :] for h in range(heads_per_block)], axis=1).reshape(nb * heads_per_block, hd, hd)


def _as_rows(a):
    if a.ndim == 1:
        return a.reshape(-1, LANES) if a.shape[0] % LANES == 0 else a.reshape(1, -1)
    if a.ndim == 3:
        return a.reshape(-1, LANES) if (a.size % LANES == 0) else a.reshape(a.shape[0] * a.shape[1], a.shape[2])
    return a


def kernel(x, g_mix, w_in, lru_conv_w, lru_conv_b, lru_wa, lru_ba, lru_wx, lru_bx, lru_lambda, lru_w_out, sc_conv_w, sc_w_out, w_o, g_ffn, ffn_w_up, ffn_conv_w, ffn_w_down, g_final, loss_target, m_g_mix, m_w_in, m_lru_conv_w, m_lru_conv_b, m_lru_wa, m_lru_ba, m_lru_wx, m_lru_bx, m_lru_lambda, m_lru_w_out, m_sc_conv_w, m_sc_w_out, m_w_o, m_g_ffn, m_ffn_w_up, m_ffn_conv_w, m_ffn_w_down, m_g_final, v_g_mix, v_w_in, v_lru_conv_w, v_lru_conv_b, v_lru_wa, v_lru_ba, v_lru_wx, v_lru_bx, v_lru_lambda, v_lru_w_out, v_sc_conv_w, v_sc_w_out, v_w_o, v_g_ffn, v_ffn_w_up, v_ffn_conv_w, v_ffn_w_down, v_g_final):
    T, D = x.shape[1], x.shape[2]
    d_lru = lru_lambda.shape[0]
    d_sc = sc_conv_w.shape[1] * N_DEV
    F = ffn_w_down.shape[0] * N_DEV
    H = lru_wa.shape[0]
    assert d_lru == d_sc and H * HEAD_DIM == d_lru
    xs = x.reshape(T, D)
    tgt = loss_target.reshape(T, D)
    my_x, my_y, my_c = _place()
    me = 4 * my_x + 2 * my_y + my_c

    big = [w_in, lru_w_out, sc_w_out, w_o, ffn_w_up, ffn_w_down]
    big_names = ["w_in", "lru_w_out", "sc_w_out", "w_o", "ffn_w_up", "ffn_w_down"]
    place = jnp.stack([my_c, 2 * my_x + my_y, me]).astype(jnp.int32)
    n_in, n_up = w_in.shape[1], ffn_w_up.shape[1]
    paired = [n_in, None, None, None, n_up, None]
    big_bf = [_cast_into_slot(w, place, "cast_" + nm, paired=pn is not None)
              for w, nm, pn in zip(big, big_names, paired)]
    pad_rows = lambda a: jnp.pad(a, ((0, SUB - a.shape[0]), (0, 0)))
    gathered = _all_gather(big_bf + [pad_rows(lru_conv_w), pad_rows(sc_conv_w), pad_rows(ffn_conv_w)],
                           [True] * 6 + [False] * 3,
                           [True, False, False, False, False, False, True, True, True],
                           paired + [None] * 3, "all_gather_first")
    W_in, W_lo, W_so, W_o8, W_up, W_dn8 = gathered[:6]
    full_cols = lambda g, kw: g[:, :kw, :].transpose(1, 0, 2).reshape(kw, -1)
    cw_lru = full_cols(gathered[6], lru_conv_w.shape[0])
    cw_sc = full_cols(gathered[7], sc_conv_w.shape[0])
    cw_ffn = full_cols(gathered[8], ffn_conv_w.shape[0])

    C = _tile(d_lru, C_LRU)
    hpb = C // HEAD_DIM
    wa_bd = _block_diag(lru_wa, hpb).astype(BF16)
    wx_bd = _block_diag(lru_wx, hpb).astype(BF16)
    cb, ba, bx, lam = (a.reshape(1, d_lru) for a in (lru_conv_b, lru_ba, lru_bx, lru_lambda))

    h1 = _rms_fwd(xs, g_mix, "rms_mix")
    k8 = W_up.shape[1] // 8
    wide = 2 * max(n_in, n_up)
    p, ((W_o8,), (W_lo,), (W_so,), (W_up,)) = _mm_nn(
        h1, W_in, out_dtype=BF16, name="mm_in", tn=wide,
        tasks=[_gather_task(W_o8, ici=ALL_ROWS), _gather_task(W_lo, ici=ALL_ROWS), _gather_task(W_so, ici=ALL_ROWS),
               _gather_task(W_up, ici=(0, 4 * k8), pair_n=n_up)])
    hs, yl_pre, ((W_o8,), (W_lo,), (W_so,), (W_up,)) = _lru_fwd(
        p, cw_lru, cb, wa_bd, ba, wx_bd, bx, lam, name="lru_fwd",
        tasks=[_gather_task(W_o8, fwd=ALL_ROWS), _gather_task(W_lo, fwd=ALL_ROWS), _gather_task(W_so, fwd=ALL_ROWS),
               _gather_task(W_up, ici=(4 * k8, 3 * k8), fwd=(0, 4 * k8), pair_n=n_up)])
    ys_pre = _sc_fwd(p, cw_sc, d=d_sc, name="sc_fwd")
    y_lru, ((W_up,),) = _mm_small(
        "nn", yl_pre, None, W_lo, name="mm_lru_out",
        tasks=[_gather_task(W_up, ici=(7 * k8, k8), fwd=(4 * k8, 3 * k8), pair_n=n_up)])
    y_sc, ((W_up,),) = _mm_small("nn", ys_pre, None, W_so, name="mm_sc_out",
                                 tasks=[_gather_task(W_up, fwd=(7 * k8, k8), pair_n=n_up)])
    gate0 = 2 * d_lru + 3 * d_sc
    merged = _merge_fwd(p, y_lru, y_sc, col0=gate0, name="merge_fwd")
    W_o = W_o8.reshape(1, D, D)
    x1 = _mm_nn(merged, W_o, out_dtype=F32, residual=xs, name="mm_o")
    h2 = _rms_fwd(x1, g_ffn, "rms_ffn")
    uu, ((W_dn8,),) = _mm_nn(h2, W_up, out_dtype=BF16, name="mm_up", tn=wide,
                             tasks=[_gather_task(W_dn8, ici=ALL_ROWS)])
    act, ((W_dn8,),) = _ffn_act_fwd(uu, cw_ffn, name="ffn_act_fwd", tasks=[_gather_task(W_dn8, fwd=ALL_ROWS)])
    W_dn = W_dn8.reshape(1, F, D)
    x2 = _mm_nn(act, W_dn, out_dtype=F32, residual=x1, name="mm_down", tn=1024, tk=F)
    dx2b, loss_part, dg_final = _loss_head(x2, g_final, tgt, "loss_head")

    def pack_rows(arrs):
        flat = jnp.concatenate([a.reshape(-1) for a in arrs])
        rows = -(-flat.shape[0] // (SUB * LANES)) * SUB
        return jnp.pad(flat, (0, rows * LANES - flat.shape[0])).reshape(rows, LANES)

    def unpack_rows(pack, arrs):
        flat, out, o = pack.reshape(-1), [], 0
        for a in arrs:
            out.append(flat[o:o + a.size].reshape(a.shape))
            o += a.size
        return out

    dact = _mm_nt(dx2b, W_dn, out_dtype=BF16, name="mm_down_dx", tm=512, tko=F // 2, tn=D)
    gW_dn = _mm_tn(act, dx2b, 1, out_dtype=BF16, name="mm_down_dw", tk=1408, tt=2048).reshape(N_DEV, F // N_DEV, D)
    duu, dcw_ffn_g, dcw_ffn_v = _ffn_act_bwd(uu, dact, cw_ffn, name="ffn_act_bwd")
    dh2, ((land_dn,),) = _mm_nt(duu, W_up, out_dtype=BF16, name="mm_up_dx", tn=wide, tasks=[_swap_task(gW_dn)])
    parts_dn = _add_halves(gW_dn, land_dn, place, "rs_add_ffn_w_down")
    gW_up, ((mine_dn,),) = _mm_tn(h2, duu, N_CHIP, out_dtype=BF16, name="mm_up_dw", tk=512, tn=wide, tt=2048,
                                  tasks=[_exchange_task(*parts_dn)])
    dx1b, dg_ffn = _rms_bwd(x1, g_ffn, dh2, dx2b, "rms_ffn_bwd", BF16)
    dmerged, ((land_up,),) = _mm_nt(dx1b, W_o, out_dtype=BF16, name="mm_o_dx", tn=D,
                                    tasks=[_swap_task(gW_up, pair_n=n_up)])
    parts_up, land_up = _add_halves(gW_up, land_up, place, "rs_add_ffn_w_up", pair_n=n_up)
    r8 = parts_up.shape[1] // 8
    gW_o, ((land_up,),) = _mm_tn(merged, dx1b, 1, out_dtype=BF16, name="mm_o_dw", tt=2048,
                                 tasks=[_exchange_task(parts_up, land_up, rows=(0, r8))])
    gW_o = gW_o.reshape(N_DEV, D // N_DEV, D)
    dp, dy_lru, dy_sc, ((land_up,),) = _merge_bwd(
        p, y_lru, y_sc, dmerged, col0=gate0, name="merge_bwd",
        tasks=[_exchange_task(parts_up, land_up, rows=(r8, r8))])
    dyl_pre, ((land_o,),) = _mm_small("nt", None, dy_lru, W_lo, name="mm_lru_out_dx", tasks=[_swap_task(gW_o)])
    parts_o = _add_halves(gW_o, land_o, place, "rs_add_w_o")
    gW_lo = _mm_small("tn", yl_pre, dy_lru, W_lo, name="mm_lru_out_dw")
    dys_pre, ((land_lo,),) = _mm_small("nt", None, dy_sc, W_so, name="mm_sc_out_dx", tasks=[_swap_task(gW_lo)])
    parts_lo = _add_halves(gW_lo, land_lo, place, "rs_add_lru_w_out")
    gW_so = _mm_small("tn", ys_pre, dy_sc, W_so, name="mm_sc_out_dw")
    dp, dcw_sc, ((land_up,),) = _sc_bwd(p, dys_pre, dp, cw_sc, d=d_sc, name="sc_bwd",
                                        tasks=[_exchange_task(parts_up, land_up, rows=(2 * r8, r8))])
    dp, dcw_lru, dcb, dwa_bd, dba, dwx_bd, dbx, dlam, ((land_up,), (mine_o,), (mine_lo,), (land_so,)) = _lru_bwd(
        p, hs, dyl_pre, dp, cw_lru, cb, wa_bd, ba, wx_bd, bx, lam, name="lru_bwd",
        tasks=[_exchange_task(parts_up, land_up, rows=(3 * r8, 3 * r8)), _exchange_task(*parts_o),
               _exchange_task(*parts_lo), _swap_task(gW_so)])
    parts_so = _add_halves(gW_so, land_so, place, "rs_add_sc_w_out")

    dwa = _diag_blocks(dwa_bd, hpb, HEAD_DIM)
    dwx = _diag_blocks(dwx_bd, hpb, HEAD_DIM)
    dcw_ffn = jnp.concatenate([dcw_ffn_g, dcw_ffn_v], axis=1)
    rep_grads = [dcb, dwa, dba, dwx, dbx, dlam, dg_ffn, dg_final]
    small_full = rep_grads + [dcw_lru, dcw_sc, dcw_ffn]
    gW_in, ((mine_up,), (mine_so,), (packs,)) = _mm_tn(
        h1, dp, N_CHIP, out_dtype=BF16, name="mm_in_dw", tk=512, tn=wide, tt=2048,
        tasks=[_exchange_task(parts_up, land_up, rows=(6 * r8, 2 * r8)), _exchange_task(*parts_so),
               _bcast_task(pack_rows(small_full))])
    land_in = _swap_halves(gW_in, "rs_swap_w_in", pair_n=n_in)
    parts_in = _add_halves(gW_in, land_in, place, "rs_add_w_in", pair_n=n_in)
    dh1, ((mine_in,),) = _mm_nt(dp, W_in, out_dtype=BF16, name="mm_in_dx", tn=wide,
                                tasks=[_exchange_task(*parts_in)])
    grad_x, dg_mix = _rms_bwd(xs, g_mix, dh1, dx1b, "rms_mix_bwd", F32)

    mine = [mine_in, mine_lo, mine_so, mine_o, mine_up, mine_dn]
    big_m = [m_w_in, m_lru_w_out, m_sc_w_out, m_w_o, m_ffn_w_up, m_ffn_w_down]
    big_v = [v_w_in, v_lru_w_out, v_sc_w_out, v_w_o, v_ffn_w_up, v_ffn_w_down]
    big_out = {nm: _adamw_big(pt, w, m, v, "adamw_" + nm)
               for nm, pt, w, m, v in zip(big_names, mine, big, big_m, big_v)}

    (scb, swa, sba, swx, sbx, slam, sg_ffn, sg_final, scw_lru, scw_sc, scw_ffn) = unpack_rows(
        _sum_packs(packs, "sum_small"), small_full)
    (sg_mix,) = unpack_rows(_all_reduce_small(pack_rows([dg_mix]), "all_reduce_g_mix"), [dg_mix])

    def my_cols(a):
        n = a.shape[1] // N_DEV
        return lax.dynamic_slice_in_dim(a, me * n, n, axis=1)

    small_names = ["g_mix", "lru_conv_w", "lru_conv_b", "lru_wa", "lru_ba", "lru_wx", "lru_bx", "lru_lambda",
                   "sc_conv_w", "g_ffn", "ffn_conv_w", "g_final"]
    small_w = [g_mix, lru_conv_w, lru_conv_b, lru_wa, lru_ba, lru_wx, lru_bx, lru_lambda, sc_conv_w, g_ffn,
               ffn_conv_w, g_final]
    small_m = [m_g_mix, m_lru_conv_w, m_lru_conv_b, m_lru_wa, m_lru_ba, m_lru_wx, m_lru_bx, m_lru_lambda,
               m_sc_conv_w, m_g_ffn, m_ffn_conv_w, m_g_final]
    small_v = [v_g_mix, v_lru_conv_w, v_lru_conv_b, v_lru_wa, v_lru_ba, v_lru_wx, v_lru_bx, v_lru_lambda,
               v_sc_conv_w, v_g_ffn, v_ffn_conv_w, v_g_final]
    small_g = [sg_mix.reshape(D), my_cols(scw_lru), scb.reshape(d_lru), swa, sba.reshape(d_lru), swx,
               sbx.reshape(d_lru), slam.reshape(d_lru), my_cols(scw_sc), sg_ffn.reshape(D), my_cols(scw_ffn),
               sg_final.reshape(D)]
    sd, snm, snv = _adamw_small([_as_rows(a) for a in small_w], [_as_rows(a) for a in small_g],
                                [_as_rows(a) for a in small_m], [_as_rows(a) for a in small_v], "adamw_small")
    small_out = {nm: (g, d.reshape(w.shape), nm_.reshape(w.shape), nv_.reshape(w.shape))
                 for nm, w, g, d, nm_, nv_ in zip(small_names, small_w, small_g, sd, snm, snv)}

    loss = lax.psum(loss_part[0, 0], AXES)
    order = ["g_mix", "w_in", "lru_conv_w", "lru_conv_b", "lru_wa", "lru_ba", "lru_wx", "lru_bx", "lru_lambda",
             "lru_w_out", "sc_conv_w", "sc_w_out", "w_o", "g_ffn", "ffn_w_up", "ffn_conv_w", "ffn_w_down", "g_final"]
    res = {**big_out, **small_out}
    return (loss, grad_x.reshape(x.shape),
            *[res[nm][0] for nm in order], *[res[nm][1] for nm in order],
            *[res[nm][2] for nm in order], *[res[nm][3] for nm in order])
```

```python
import functools
import math

import jax
import jax.numpy as jnp
from jax import lax
from jax.experimental import pallas as pl
from jax.experimental.pallas import tpu as pltpu

F32, BF16 = jnp.float32, jnp.bfloat16
MESH = pl.DeviceIdType.MESH
N_DEV = 8
N_CHIP = 4
AXES = ("x", "y", "c")

EPS = 1e-6
LRU_C = 8.0
HEAD_DIM = 64
ADAM_LR, ADAM_B1, ADAM_B2, ADAM_EPS, ADAM_WD, ADAM_STEP = 0.001, 0.9, 0.999, 1e-08, 0.01, 10

VMEM_LIMIT = 48 * 1024 * 1024
LANES = 128
SUB = 8
HALO = 16
TB = 512
TB_CHUNKED = 2048
C_LRU = 256
C_EW = 512
TM, TN, TK = 512, 1536, 2048


def _tile(n, pref, align=LANES):
    best = None
    for d in range(align, min(n, pref) + 1, align):
        if n % d == 0:
            best = d
    return best or n


def _cparams(sem=None, vmem=VMEM_LIMIT):
    kw = dict(vmem_limit_bytes=vmem)
    if sem is not None:
        kw["dimension_semantics"] = sem
    return pltpu.CompilerParams(**kw)


def _S(shape, dtype):
    return jax.ShapeDtypeStruct(shape, dtype)


ANY = pl.BlockSpec(memory_space=pl.ANY)
VMEM_SPEC = pl.BlockSpec(memory_space=pltpu.VMEM)


class _Task:
    def __init__(self, arrays, aliased, start, wait, fresh=(), nsem=3, mid=None):
        self.arrays, self.aliased, self.start, self.wait = arrays, aliased, start, wait
        self.fresh, self.nsem, self.mid = list(fresh), nsem, mid


def _call(name, grid, compute, in_specs, args, out_shape, out_specs, scratch, tasks=(), own_aliases=None):
    n_in, n_out, n_scr = len(args), len(out_shape), len(scratch)
    x_in, x_out, aliases, where = [], [], dict(own_aliases or {}), []
    for t in tasks:
        places = []
        for k, arr in enumerate(t.arrays):
            if k in t.aliased:
                aliases[n_in + len(x_in)] = n_out + len(x_out)
                places.append(("out", len(x_out)))
                x_out.append(_S(arr.shape, arr.dtype))
            else:
                places.append(("in", len(x_in)))
            x_in.append(arr)
        for shp in t.fresh:
            places.append(("out", len(x_out)))
            x_out.append(shp)
        where.append(places)
    n_xi, n_xo = len(x_in), len(x_out)

    def body(*refs):
        ins, xi = refs[:n_in], refs[n_in:n_in + n_xi]
        o0 = n_in + n_xi
        outs, xo = refs[o0:o0 + n_out], refs[o0 + n_out:o0 + n_out + n_xo]
        s0 = o0 + n_out + n_xo
        scr, sems = refs[s0:s0 + n_scr], refs[s0 + n_scr:]
        ids = [pl.program_id(a) for a in range(len(grid))]

        def task_refs(ti):
            return [xo[i] if kind == "out" else xi[i] for kind, i in where[ti]]

        if tasks:
            first = functools.reduce(jnp.logical_and, [i == 0 for i in ids])

            @pl.when(first)
            def _():
                for ti, t in enumerate(tasks):
                    t.start(task_refs(ti), *sems[3 * ti:3 * ti + 3])

        compute(*ins, *outs, *scr)
        if any(t.mid is not None for t in tasks):
            n_steps = math.prod(grid)
            step = functools.reduce(lambda s, ig: s * ig[1] + ig[0], zip(ids, grid), 0)

            @pl.when(step == (5 * n_steps) // 8)
            def _():
                for ti, t in enumerate(tasks):
                    if t.mid is not None:
                        t.mid(task_refs(ti), *sems[3 * ti:3 * ti + 3])

        if tasks:
            last = functools.reduce(jnp.logical_and, [i == g - 1 for i, g in zip(ids, grid)])

            @pl.when(last)
            def _():
                for ti, t in enumerate(tasks):
                    t.wait(task_refs(ti), *sems[3 * ti:3 * ti + 3])

    sem_shapes = []
    for t in tasks:
        sem_shapes += [pltpu.SemaphoreType.DMA((t.nsem,)), pltpu.SemaphoreType.DMA((t.nsem,)),
                       pltpu.SemaphoreType.DMA((1,))]
    res = pl.pallas_call(
        body, name=name, grid=grid,
        in_specs=list(in_specs) + [ANY] * n_xi,
        out_specs=tuple(out_specs) + (ANY,) * n_xo,
        out_shape=tuple(out_shape) + tuple(x_out),
        scratch_shapes=list(scratch) + sem_shapes,
        input_output_aliases=aliases,
        compiler_params=_cparams(("arbitrary",) * len(grid)),
    )(*args, *x_in)
    outs, passed, o = res[:n_out], [], n_out
    for places in where:
        k = sum(1 for kind, _ in places if kind == "out")
        passed.append(list(res[o:o + k]))
        o += k
    return outs, passed


def _mm_nn(a, w3, *, out_dtype, name, residual=None, tm=TM, tn=TN, tk=TK, tasks=()):
    M, K = a.shape
    G, _, n = w3.shape
    tm, tn, tk = _tile(M, tm, SUB), _tile(n, tn), _tile(K, tk)
    nj, nk = n // tn, K // tk

    def compute(*refs):
        if residual is None:
            a_ref, w_ref, o_ref = refs[:3]
            r_ref = None
        else:
            a_ref, w_ref, r_ref, o_ref = refs[:4]

        def finish(r):
            if r_ref is not None:
                r = r + r_ref[...]
            o_ref[...] = r.astype(o_ref.dtype)

        if nk == 1:
            finish(jnp.dot(a_ref[...], w_ref[...], preferred_element_type=F32))
            return
        acc = refs[-1]
        k = pl.program_id(3)

        @pl.when(k == 0)
        def _():
            acc[...] = jnp.zeros_like(acc)

        acc[...] += jnp.dot(a_ref[...], w_ref[...], preferred_element_type=F32)

        @pl.when(k == nk - 1)
        def _():
            finish(acc[...])

    in_specs = [pl.BlockSpec((tm, tk), lambda g, j, i, k: (i, k)),
                pl.BlockSpec((None, tk, tn), lambda g, j, i, k: (g, k, j))]
    args = [a, w3]
    if residual is not None:
        in_specs.append(pl.BlockSpec((tm, tn), lambda g, j, i, k: (i, g * nj + j)))
        args.append(residual)
    outs, passed = _call(
        name, (G, nj, M // tm, nk), compute, in_specs, args, [_S((M, G * n), out_dtype)],
        [pl.BlockSpec((tm, tn), lambda g, j, i, k: (i, g * nj + j))],
        [] if nk == 1 else [pltpu.VMEM((tm, tn), F32)], tasks)
    return (outs[0], passed) if tasks else outs[0]


def _mm_nt(dy, w3, *, out_dtype, name, tm=1024, tko=1024, tn=TN, tasks=()):
    M, _ = dy.shape
    G, K, n = w3.shape
    tm, tko, tn = _tile(M, tm, SUB), _tile(K, tko), _tile(n, tn)
    nj = n // tn
    nr = G * nj

    def compute(dy_ref, w_ref, o_ref, *scr):
        part = lax.dot_general(dy_ref[...], w_ref[...], (((1,), (1,)), ((), ())), preferred_element_type=F32)
        if nr == 1:
            o_ref[...] = part.astype(o_ref.dtype)
            return
        (acc,) = scr
        r = pl.program_id(2)

        @pl.when(r == 0)
        def _():
            acc[...] = jnp.zeros_like(acc)

        acc[...] += part

        @pl.when(r == nr - 1)
        def _():
            o_ref[...] = acc[...].astype(o_ref.dtype)

    outs, passed = _call(
        name, (K // tko, M // tm, nr), compute,
        [pl.BlockSpec((tm, tn), lambda ko, i, r: (i, r)),
         pl.BlockSpec((None, tko, tn), lambda ko, i, r: (r // nj, ko, r % nj))],
        [dy, w3], [_S((M, K), out_dtype)], [pl.BlockSpec((tm, tko), lambda ko, i, r: (i, ko))],
        [] if nr == 1 else [pltpu.VMEM((tm, tko), F32)], tasks)
    return (outs[0], passed) if tasks else outs[0]


def _mm_tn(a, dy, G, *, out_dtype, name, tk=1024, tn=TN, tt=1024, tasks=()):
    M, K = a.shape
    n = dy.shape[1] // G
    tk, tn, tt = _tile(K, tk), _tile(n, tn), _tile(M, tt, SUB)
    nj, nt = n // tn, M // tt

    def compute(a_ref, dy_ref, o_ref, acc):
        t = pl.program_id(3)

        @pl.when(t == 0)
        def _():
            acc[...] = jnp.zeros_like(acc)

        acc[...] += lax.dot_general(a_ref[...], dy_ref[...], (((0,), (0,)), ((), ())),
                                    preferred_element_type=F32)

        @pl.when(t == nt - 1)
        def _():
            o_ref[...] = acc[...].astype(o_ref.dtype)

    outs, passed = _call(
        name, (G, nj, K // tk, nt), compute,
        [pl.BlockSpec((tt, tk), lambda g, j, k, t: (t, k)),
         pl.BlockSpec((tt, tn), lambda g, j, k, t: (t, g * nj + j))],
        [a, dy], [_S((G, K, n), out_dtype)], [pl.BlockSpec((None, tk, tn), lambda g, j, k, t: (g, k, j))],
        [pltpu.VMEM((tk, tn), F32)], tasks)
    return (outs[0], passed) if tasks else outs[0]


def _mm_small(kind, a, b, w3, *, name, tm=1024, tasks=()):
    G, K, n = w3.shape
    M = (a if a is not None else b).shape[0]
    tm = _tile(M, tm, HALO)
    nt = M // tm
    w_spec = pl.BlockSpec((G, K, n), lambda i: (0, 0, 0))
    a_spec = pl.BlockSpec((tm, K), lambda i: (i, 0))
    b_spec = pl.BlockSpec((tm, G * n), lambda i: (i, 0))
    cols = lambda g: slice(g * n, (g + 1) * n)
    if kind == "nn":
        def compute(a_ref, w_ref, o_ref):
            av = a_ref[...]
            for g in range(G):
                o_ref[:, cols(g)] = jnp.dot(av, w_ref[g], preferred_element_type=F32).astype(o_ref.dtype)

        outs, passed = _call(name, (nt,), compute, [a_spec, w_spec], [a, w3], [_S((M, G * n), BF16)], [b_spec], [], tasks)
    elif kind == "nt":
        def compute(b_ref, w_ref, o_ref):
            acc = None
            for g in range(G):
                part = lax.dot_general(b_ref[:, cols(g)], w_ref[g], (((1,), (1,)), ((), ())),
                                       preferred_element_type=F32)
                acc = part if acc is None else acc + part
            o_ref[...] = acc.astype(o_ref.dtype)

        outs, passed = _call(name, (nt,), compute, [b_spec, w_spec], [b, w3], [_S((M, K), BF16)], [a_spec], [], tasks)
    else:
        def compute(a_ref, b_ref, o_ref, acc):
            i = pl.program_id(0)

            @pl.when(i == 0)
            def _():
                acc[...] = jnp.zeros_like(acc)

            at = a_ref[...].T
            for g in range(G):
                acc[g] += jnp.dot(at, b_ref[:, cols(g)], preferred_element_type=F32)

            @pl.when(i == nt - 1)
            def _():
                o_ref[...] = acc[...].astype(o_ref.dtype)

        outs, passed = _call(name, (nt,), compute, [a_spec, b_spec], [a, b], [_S((G, K, n), BF16)], [w_spec],
                             [pltpu.VMEM((G, K, n), F32)], tasks)
    return (outs[0], passed) if tasks else outs[0]


def _cast_into_slot(w, place, name, paired=False):
    R, C = w.shape
    tr = _tile(R, 512, HALO)

    def body(s_ref, w_ref, o_ref):
        del s_ref
        o_ref[...] = w_ref[...].astype(BF16)

    if paired:
        shape, out_map = (N_CHIP, R, 2 * C), lambda i, s: (s[1], i, s[0])
    else:
        shape, out_map = (N_DEV, R, C), lambda i, s: (s[2], i, 0)
    return pl.pallas_call(
        body, name=name, out_shape=_S(shape, BF16),
        grid_spec=pltpu.PrefetchScalarGridSpec(
            num_scalar_prefetch=1, grid=(R // tr,),
            in_specs=[pl.BlockSpec((tr, C), lambda i, s: (i, 0))],
            out_specs=pl.BlockSpec((None, tr, C), out_map)),
        compiler_params=_cparams(("parallel",)),
    )(place, w)


def _down(cur, prev8, j):
    return pltpu.roll(jnp.concatenate([prev8, cur], axis=0), j, 0)[SUB:, :]


def _up(cur, next8, j):
    n = cur.shape[0] + SUB
    return pltpu.roll(jnp.concatenate([cur, next8], axis=0), n - j, 0)[:cur.shape[0], :]


def _shifted_down(x, prev8, n):
    full = jnp.concatenate([prev8, x], axis=0)
    return [x] + [pltpu.roll(full, s, 0)[SUB:, :] for s in range(1, n)]


def _shifted_up(x, next8, n):
    m = x.shape[0] + SUB
    full = jnp.concatenate([x, next8], axis=0)
    return [x] + [pltpu.roll(full, m - s, 0)[:x.shape[0], :] for s in range(1, n)]


def _taps(sh, w_ref):
    kw = w_ref.shape[0]
    y = sh[0] * w_ref[pl.ds(kw - 1, 1), :]
    for k in range(kw - 1):
        y = y + sh[kw - 1 - k] * w_ref[pl.ds(k, 1), :]
    return y


def _conv(x, prev8, w_ref):
    return _taps(_shifted_down(x, prev8, w_ref.shape[0]), w_ref)


def _conv_t(dy, next8, w_ref):
    return _taps(_shifted_up(dy, next8, w_ref.shape[0]), w_ref)


def _conv_dw(dw_ref, dy, x, prev8, first):
    kw = dw_ref.shape[0]

    @pl.when(first)
    def _():
        dw_ref[...] = jnp.zeros_like(dw_ref)

    for k in range(kw):
        xs = x if k == kw - 1 else _down(x, prev8, kw - 1 - k)
        dw_ref[pl.ds(k, 1), :] += jnp.sum(dy * xs, axis=0, keepdims=True)


def _acc(ref, val, first):
    @pl.when(first)
    def _():
        ref[...] = jnp.zeros_like(ref)

    ref[...] += val


def _acc_row(ref, val, first):
    _acc(ref, jnp.sum(val, axis=0, keepdims=True), first)


def _prev8(h_ref, t):
    return jnp.where(t > 0, h_ref[...].astype(F32)[HALO - SUB:, :], 0.0)


def _next8(h_ref, is_last):
    return jnp.where(is_last, 0.0, h_ref[...].astype(F32)[:SUB, :])


_GELU_K0 = math.sqrt(2.0 / math.pi)
_GELU_K1 = 0.044715


def _gelu_and_grad(x):
    x2 = x * x
    th = jnp.tanh(_GELU_K0 * x * (1.0 + _GELU_K1 * x2))
    g = 0.5 * x * (1.0 + th)
    dg = 0.5 * (1.0 + th) + 0.5 * x * (1.0 - th * th) * (_GELU_K0 * (1.0 + 3.0 * _GELU_K1 * x2))
    return g, dg


def _neg_expm1(z):
    series = -z * (1.0 + z * (0.5 + z * (1.0 / 6.0 + z * (1.0 / 24.0))))
    return jnp.where(z > -0.03, series, 1.0 - jnp.exp(z))


def _store_staged(stages, dst_hbm, sems, step, n_steps, where):
    def copies(s, slot):
        return [pltpu.make_async_copy(
            st.at[slot], dst_hbm.at[pl.ds(r0, st.shape[1]), pl.ds(c0, st.shape[2])], sems.at[slot, k])
            for k, (st, (r0, c0)) in enumerate(zip(stages, where(s)))]

    slot = step % 2

    @pl.when(step > 0)
    def _():
        for cp in copies(step - 1, 1 - slot):
            cp.wait()

    for cp in copies(step, slot):
        cp.start()

    @pl.when(step == n_steps - 1)
    def _():
        for cp in copies(step, slot):
            cp.wait()


def _halo_prev_map(hb, col_fn):
    return lambda c, t: (jnp.maximum(t * hb - 1, 0), col_fn(c))


def _rms_fwd(x, g, name):
    T, D = x.shape
    tb = _tile(T, TB, SUB)

    def body(x_ref, g_ref, o_ref):
        xv = x_ref[...]
        rstd = lax.rsqrt(jnp.mean(xv * xv, axis=-1, keepdims=True) + EPS)
        o_ref[...] = (xv * rstd * g_ref[...]).astype(BF16)

    return pl.pallas_call(
        body, name=name, out_shape=_S((T, D), BF16), grid=(T // tb,),
        in_specs=[pl.BlockSpec((tb, D), lambda i: (i, 0)), pl.BlockSpec((1, D), lambda i: (0, 0))],
        out_specs=pl.BlockSpec((tb, D), lambda i: (i, 0)),
        compiler_params=_cparams(("parallel",)),
    )(x, g.reshape(1, D))


def _rms_bwd(x, g, dh, dres, name, out_dtype):
    T, D = x.shape
    tb = _tile(T, 256, HALO)

    def body(x_ref, g_ref, dh_ref, dr_ref, dx_ref, dg_ref):
        i = pl.program_id(0)
        xv = x_ref[...]
        rstd = lax.rsqrt(jnp.mean(xv * xv, axis=-1, keepdims=True) + EPS)
        xn = xv * rstd
        dhv = dh_ref[...].astype(F32)
        _acc_row(dg_ref, dhv * xn, i == 0)
        dxn = dhv * g_ref[...]
        dx = dr_ref[...].astype(F32) + rstd * (dxn - xn * jnp.mean(dxn * xn, axis=-1, keepdims=True))
        dx_ref[...] = dx.astype(dx_ref.dtype)

    blk = pl.BlockSpec((tb, D), lambda i: (i, 0))
    vec = pl.BlockSpec((1, D), lambda i: (0, 0))
    return pl.pallas_call(
        body, name=name, out_shape=(_S((T, D), out_dtype), _S((1, D), F32)),
        grid=(T // tb,), in_specs=[blk, vec, blk, blk], out_specs=(blk, vec),
        compiler_params=_cparams(("arbitrary",)),
    )(x, g.reshape(1, D), dh, dres)


def _loss_head(x2, g, target, name):
    T, D = x2.shape
    tb = _tile(T, 256, HALO)

    def body(x_ref, g_ref, t_ref, dxb_ref, loss_ref, dg_ref):
        i = pl.program_id(0)
        xv = x_ref[...]
        rstd = lax.rsqrt(jnp.mean(xv * xv, axis=-1, keepdims=True) + EPS)
        xn = xv * rstd
        err = xn * g_ref[...] - t_ref[...]
        part = 0.5 * jnp.sum(jnp.mean(err * err, axis=-1, keepdims=True), axis=0, keepdims=True)
        part = jnp.broadcast_to(part, (1, LANES))
        _acc(loss_ref, part, i == 0)
        dy = err * (1.0 / D)
        _acc_row(dg_ref, dy * xn, i == 0)
        dxn = dy * g_ref[...]
        dxb_ref[...] = (rstd * (dxn - xn * jnp.mean(dxn * xn, axis=-1, keepdims=True))).astype(BF16)

    blk = pl.BlockSpec((tb, D), lambda i: (i, 0))
    vec = pl.BlockSpec((1, D), lambda i: (0, 0))
    return pl.pallas_call(
        body, name=name,
        out_shape=(_S((T, D), BF16), _S((1, LANES), F32), _S((1, D), F32)),
        grid=(T // tb,), in_specs=[blk, vec, blk],
        out_specs=(blk, pl.BlockSpec((1, LANES), lambda i: (0, 0)), vec),
        compiler_params=_cparams(("arbitrary",)),
    )(x2, g.reshape(1, D), target)


def _lru_gates(xc, wa_ref, ba_ref, wx_ref, bx_ref, lam_ref):
    xcb = xc.astype(BF16)
    r = jax.nn.sigmoid(jnp.dot(xcb, wa_ref[...], preferred_element_type=F32) + ba_ref[...])
    i = jax.nn.sigmoid(jnp.dot(xcb, wx_ref[...], preferred_element_type=F32) + bx_ref[...])
    sp = jax.nn.softplus(-lam_ref[...])
    log_a = (-LRU_C * sp) * r
    a = jnp.exp(log_a)
    s = jnp.sqrt(_neg_expm1(2.0 * log_a))
    return xcb, r, i, a, s


def _lru_fwd(p, conv_w, conv_b, wa_bd, ba, wx_bd, bx, lam, *, name, tasks=()):
    T = p.shape[0]
    d = lam.shape[-1]
    C = _tile(d, C_LRU)
    nC = d // C
    tb = _tile(T, TB, HALO)
    nT, hb, nt = T // tb, tb // HALO, tb // SUB

    def body(x_ref, xh_ref, g_ref, cw_ref, cb_ref, wa_ref, ba_ref, wx_ref, bx_ref, lam_ref,
             hs_ref, y_ref, a_s, u_s, h_s):
        t = pl.program_id(1)

        @pl.when(t == 0)
        def _():
            h_s[...] = jnp.zeros_like(h_s)

        x = x_ref[...].astype(F32)
        xc = _conv(x, _prev8(xh_ref, t), cw_ref) + cb_ref[...]
        _, r, i, a, s = _lru_gates(xc, wa_ref, ba_ref, wx_ref, bx_ref, lam_ref)
        a_s[...] = a
        u_s[...] = s * (i * xc)
        row = lax.broadcasted_iota(jnp.int32, (SUB, C), 0)

        def step(k, h):
            o = pl.multiple_of(k * SUB, SUB)
            A = a_s[pl.ds(o, SUB), :]
            B = u_s[pl.ds(o, SUB), :]
            for sh in (1, 2, 4):
                m = row >= sh
                Ap = pltpu.roll(A, sh, 0)
                Bp = pltpu.roll(B, sh, 0)
                B = jnp.where(m, A * Bp + B, B)
                A = jnp.where(m, A * Ap, A)
            hs = A * h + B
            hs_ref[pl.ds(o, SUB), :] = hs
            return jnp.broadcast_to(hs[SUB - 1:SUB, :], (SUB, C))

        h_s[...] = lax.fori_loop(0, nt, step, h_s[...])
        gel, _ = _gelu_and_grad(g_ref[...].astype(F32))
        y_ref[...] = (gel * hs_ref[...]).astype(BF16)

    vec = pl.BlockSpec((1, C), lambda c, t: (0, c))
    sq = pl.BlockSpec((None, C, C), lambda c, t: (c, 0, 0))
    outs, passed = _call(
        name, (nC, nT), body,
        [pl.BlockSpec((tb, C), lambda c, t: (t, c)),
         pl.BlockSpec((HALO, C), _halo_prev_map(hb, lambda c: c)),
         pl.BlockSpec((tb, C), lambda c, t: (t, nC + c)),
         pl.BlockSpec((conv_w.shape[0], C), lambda c, t: (0, c)),
         vec, sq, vec, sq, vec, vec],
        [p, p, p, conv_w, conv_b, wa_bd, ba, wx_bd, bx, lam],
        [_S((T, d), F32), _S((T, d), BF16)],
        [pl.BlockSpec((tb, C), lambda c, t: (t, c)), pl.BlockSpec((tb, C), lambda c, t: (t, c))],
        [pltpu.VMEM((tb, C), F32), pltpu.VMEM((tb, C), F32), pltpu.VMEM((SUB, C), F32)], tasks)
    return (*outs, passed) if tasks else outs


def _lru_bwd(p, hs, dyl, dp, conv_w, conv_b, wa_bd, ba, wx_bd, bx, lam, *, name, tasks=()):
    T = p.shape[0]
    d = lam.shape[-1]
    C = _tile(d, C_LRU)
    nC = d // C
    tb = _tile(T, TB, HALO)
    nT, hb, nt = T // tb, tb // HALO, tb // SUB
    kw = conv_w.shape[0]

    def body(x_ref, xh_ref, g_ref, hs_ref, hh_ref, dy_ref, cw_ref, cb_ref, wa_ref, ba_ref, wx_ref, bx_ref,
             lam_ref, dp_in, dp_ref, dcw_ref, dcb_ref, dwa_ref, dba_ref, dwx_ref, dbx_ref, dlam_ref,
             b_s, g_s, dh_s, an_s, dhn_s, dxn_s, st_x, st_g, sems):
        del dp_in
        c = pl.program_id(0)
        tr = pl.program_id(1)
        t = nT - 1 - tr
        first = tr == 0

        @pl.when(first)
        def _():
            an_s[...] = jnp.zeros_like(an_s)
            dhn_s[...] = jnp.zeros_like(dhn_s)
            dxn_s[...] = jnp.zeros_like(dxn_s)

        x = x_ref[...].astype(F32)
        xprev = _prev8(xh_ref, t)
        xc = _conv(x, xprev, cw_ref) + cb_ref[...]
        xcb, r, i, a, s = _lru_gates(xc, wa_ref, ba_ref, wx_ref, bx_ref, lam_ref)
        hsv = hs_ref[...]
        dy = dy_ref[...].astype(F32)
        gel, dgel = _gelu_and_grad(g_ref[...].astype(F32))
        step_no = c * nT + tr
        slot = step_no % 2
        st_g[slot] = (dy * hsv * dgel).astype(BF16)

        b_s[...] = _up(a, an_s[...], 1)
        g_s[...] = dy * gel
        row = lax.broadcasted_iota(jnp.int32, (SUB, C), 0)

        def step(k, carry):
            o = pl.multiple_of((nt - 1 - k) * SUB, SUB)
            B = b_s[pl.ds(o, SUB), :]
            G = g_s[pl.ds(o, SUB), :]
            for sh in (1, 2, 4):
                m = row < SUB - sh
                Bn = pltpu.roll(B, SUB - sh, 0)
                Gn = pltpu.roll(G, SUB - sh, 0)
                G = jnp.where(m, B * Gn + G, G)
                B = jnp.where(m, B * Bn, B)
            dh = B * carry + G
            dh_s[pl.ds(o, SUB), :] = dh
            return jnp.broadcast_to(dh[0:1, :], (SUB, C))

        dhn_s[...] = lax.fori_loop(0, nt, step, dhn_s[...])
        an_s[...] = a[:SUB, :]
        dh = dh_s[...]

        hprev = _down(hsv, jnp.where(t > 0, hh_ref[...][HALO - SUB:, :], 0.0), 1)
        d_a = dh * hprev
        ixc = i * xc
        d_s = dh * ixc
        d_i = dh * s * xc
        d_xc = dh * s * i
        d_l = d_a * a - d_s * (a * a) / s
        sp = jax.nn.softplus(-lam_ref[...])
        _acc_row(dlam_ref, d_l * r * (LRU_C * jax.nn.sigmoid(-lam_ref[...])), first)
        d_zr = (d_l * (-LRU_C * sp)) * r * (1.0 - r)
        d_zi = d_i * i * (1.0 - i)
        _acc_row(dba_ref, d_zr, first)
        _acc_row(dbx_ref, d_zi, first)
        d_zrb = d_zr.astype(BF16)
        d_zib = d_zi.astype(BF16)
        tn_dims = (((0,), (0,)), ((), ()))
        nt_dims = (((1,), (1,)), ((), ()))
        gwa = lax.dot_general(xcb, d_zrb, tn_dims, preferred_element_type=F32)
        gwx = lax.dot_general(xcb, d_zib, tn_dims, preferred_element_type=F32)
        _acc(dwa_ref, gwa, first)
        _acc(dwx_ref, gwx, first)
        d_xc = (d_xc + lax.dot_general(d_zrb, wa_ref[...], nt_dims, preferred_element_type=F32)
                + lax.dot_general(d_zib, wx_ref[...], nt_dims, preferred_element_type=F32))
        _acc_row(dcb_ref, d_xc, first)
        _conv_dw(dcw_ref, d_xc, x, xprev, first)
        st_x[slot] = _conv_t(d_xc, dxn_s[...], cw_ref).astype(BF16)
        dxn_s[...] = d_xc[:SUB, :]

        def where(s):
            row0, col0 = (nT - 1 - s % nT) * tb, (s // nT) * C
            return [(row0, col0), (row0, d + col0)]

        _store_staged([st_x, st_g], dp_ref, sems, step_no, nC * nT, where)

    rev = lambda c, tr: (nT - 1 - tr, c)
    vec = pl.BlockSpec((1, C), lambda c, tr: (0, c))
    sq = pl.BlockSpec((None, C, C), lambda c, tr: (c, 0, 0))
    cwb = pl.BlockSpec((kw, C), lambda c, tr: (0, c))
    halo_prev = lambda c, tr: (jnp.maximum((nT - 1 - tr) * hb - 1, 0), c)
    outs, passed = _call(
        name, (nC, nT), body,
        [pl.BlockSpec((tb, C), rev),
         pl.BlockSpec((HALO, C), halo_prev),
         pl.BlockSpec((tb, C), lambda c, tr: (nT - 1 - tr, nC + c)),
         pl.BlockSpec((tb, C), rev),
         pl.BlockSpec((HALO, C), halo_prev),
         pl.BlockSpec((tb, C), rev),
         cwb, vec, sq, vec, sq, vec, vec, ANY],
        [p, p, p, hs, hs, dyl, conv_w, conv_b, wa_bd, ba, wx_bd, bx, lam, dp],
        [_S(dp.shape, dp.dtype), _S((kw, d), F32), _S((1, d), F32), _S((nC, C, C), F32), _S((1, d), F32),
         _S((nC, C, C), F32), _S((1, d), F32), _S((1, d), F32)],
        [ANY, cwb, vec, sq, vec, sq, vec, vec],
        [pltpu.VMEM((tb, C), F32), pltpu.VMEM((tb, C), F32), pltpu.VMEM((tb, C), F32),
         pltpu.VMEM((SUB, C), F32), pltpu.VMEM((SUB, C), F32), pltpu.VMEM((SUB, C), F32),
         pltpu.VMEM((2, tb, C), BF16), pltpu.VMEM((2, tb, C), BF16), pltpu.SemaphoreType.DMA((2, 2))],
        tasks, own_aliases={13: 0})
    return (*outs, passed) if tasks else outs


def _sc_fwd(p, conv_w, *, d, name):
    T = p.shape[0]
    C = _tile(d, C_EW)
    nC = d // C
    tb = _tile(T, TB, HALO)
    nT, hb = T // tb, tb // HALO

    def body(b_ref, c_ref, ch_ref, v_ref, vh_ref, w_ref, y_ref):
        t = pl.program_id(1)
        cv = c_ref[...].astype(F32) * v_ref[...].astype(F32)
        cvp = _prev8(ch_ref, t) * _prev8(vh_ref, t)
        y_ref[...] = (b_ref[...].astype(F32) * _conv(cv, cvp, w_ref)).astype(BF16)

    seg = lambda k: pl.BlockSpec((tb, C), lambda c, t: (t, k * nC + c))
    hseg = lambda k: pl.BlockSpec((HALO, C), _halo_prev_map(hb, lambda c: k * nC + c))
    return pl.pallas_call(
        body, name=name, out_shape=_S((T, d), BF16), grid=(nC, nT),
        in_specs=[seg(2), seg(3), hseg(3), seg(4), hseg(4), pl.BlockSpec((conv_w.shape[0], C), lambda c, t: (0, c))],
        out_specs=pl.BlockSpec((tb, C), lambda c, t: (t, c)),
        compiler_params=_cparams(("parallel", "parallel")),
    )(p, p, p, p, p, conv_w)


def _sc_bwd(p, dys, dp, conv_w, *, d, name, tasks=()):
    T = p.shape[0]
    C = _tile(d, C_EW)
    nC = d // C
    tb = _tile(T, TB, HALO)
    nT, hb = T // tb, tb // HALO
    kw = conv_w.shape[0]

    def body(b_ref, bn_ref, c_ref, ch_ref, v_ref, vh_ref, dy_ref, dyn_ref, w_ref, dp_in, dp_ref, dw_ref,
             st_b, st_c, st_v, sems):
        del dp_in
        c = pl.program_id(0)
        t = pl.program_id(1)
        last = t == nT - 1
        bv = b_ref[...].astype(F32)
        cvv = c_ref[...].astype(F32)
        vv = v_ref[...].astype(F32)
        dy = dy_ref[...].astype(F32)
        cv = cvv * vv
        cvp = _prev8(ch_ref, t) * _prev8(vh_ref, t)
        step_no = c * nT + t
        slot = step_no % 2
        st_b[slot] = (dy * _conv(cv, cvp, w_ref)).astype(BF16)
        dz = dy * bv
        dzn = _next8(dyn_ref, last) * _next8(bn_ref, last)
        _conv_dw(dw_ref, dz, cv, cvp, t == 0)
        dcv = _conv_t(dz, dzn, w_ref)
        st_c[slot] = (dcv * vv).astype(BF16)
        st_v[slot] = (dcv * cvv).astype(BF16)

        def where(s):
            return [((s % nT) * tb, (2 + k) * d + (s // nT) * C) for k in range(3)]

        _store_staged([st_b, st_c, st_v], dp_ref, sems, step_no, nC * nT, where)

    seg = lambda k: pl.BlockSpec((tb, C), lambda c, t: (t, k * nC + c))
    hseg = lambda k: pl.BlockSpec((HALO, C), _halo_prev_map(hb, lambda c: k * nC + c))
    last_h = T // HALO - 1
    nseg = lambda k: pl.BlockSpec((HALO, C), lambda c, t: (jnp.minimum((t + 1) * hb, last_h), k * nC + c))
    outs, passed = _call(
        name, (nC, nT), body,
        [seg(2), nseg(2), seg(3), hseg(3), seg(4), hseg(4),
         pl.BlockSpec((tb, C), lambda c, t: (t, c)), nseg(0),
         pl.BlockSpec((kw, C), lambda c, t: (0, c)), ANY],
        [p, p, p, p, p, p, dys, dys, conv_w, dp],
        [_S(dp.shape, dp.dtype), _S((kw, d), F32)], [ANY, pl.BlockSpec((kw, C), lambda c, t: (0, c))],
        [pltpu.VMEM((2, tb, C), BF16)] * 3 + [pltpu.SemaphoreType.DMA((2, 3))], tasks, own_aliases={9: 0})
    return (*outs, passed) if tasks else outs


def _merge_fwd(p, y_lru, y_sc, *, col0, name, tasks=()):
    T, D = y_lru.shape
    C = _tile(math.gcd(D, col0), 1024)
    nC = D // C
    k0 = col0 // C
    tb = _tile(T, TB, HALO)

    def body(gl_ref, gs_ref, yl_ref, ys_ref, o_ref):
        @pl.loop(0, tb // HALO)
        def _(k):
            rows = pl.ds(pl.multiple_of(k * HALO, HALO), HALO)
            for l0 in range(0, C, min(C, C_EW)):
                at = (rows, pl.ds(l0, min(C, C_EW)))
                o_ref[at] = (jax.nn.sigmoid(gl_ref[at].astype(F32)) * yl_ref[at].astype(F32)
                             + jax.nn.sigmoid(gs_ref[at].astype(F32)) * ys_ref[at].astype(F32)).astype(BF16)

    blk = pl.BlockSpec((tb, C), lambda c, t: (t, c))
    outs, passed = _call(
        name, (nC, T // tb), body,
        [pl.BlockSpec((tb, C), lambda c, t: (t, k0 + c)), pl.BlockSpec((tb, C), lambda c, t: (t, k0 + nC + c)),
         blk, blk], [p, p, y_lru, y_sc], [_S((T, D), BF16)], [blk], [], tasks)
    return (outs[0], passed) if tasks else outs[0]


def _merge_bwd(p, y_lru, y_sc, dm, *, col0, name, tasks=()):
    T, D = y_lru.shape
    C = _tile(math.gcd(D, col0), 1024)
    nC = D // C
    k0 = col0 // C
    tb = _tile(T, TB, HALO)
    nT = T // tb

    def body(gl_ref, gs_ref, yl_ref, ys_ref, dm_ref, dp_ref, dyl_ref, dys_ref, st_l, st_s, sems):
        step_no = pl.program_id(0) * nT + pl.program_id(1)
        slot = step_no % 2

        @pl.loop(0, tb // HALO)
        def _(k):
            rows = pl.ds(pl.multiple_of(k * HALO, HALO), HALO)
            for l0 in range(0, C, min(C, C_EW)):
                at = (rows, pl.ds(l0, min(C, C_EW)))
                dmv = dm_ref[at].astype(F32)
                sl = jax.nn.sigmoid(gl_ref[at].astype(F32))
                ss = jax.nn.sigmoid(gs_ref[at].astype(F32))
                dyl_ref[at] = (dmv * sl).astype(BF16)
                dys_ref[at] = (dmv * ss).astype(BF16)
                st_l[(slot,) + at] = (dmv * yl_ref[at].astype(F32) * sl * (1.0 - sl)).astype(BF16)
                st_s[(slot,) + at] = (dmv * ys_ref[at].astype(F32) * ss * (1.0 - ss)).astype(BF16)

        def where(s):
            row0, colc = (s % nT) * tb, (s // nT) * C
            return [(row0, col0 + colc), (row0, col0 + D + colc)]

        _store_staged([st_l, st_s], dp_ref, sems, step_no, nC * nT, where)

    blk = pl.BlockSpec((tb, C), lambda c, t: (t, c))
    outs, passed = _call(
        name, (nC, nT), body,
        [pl.BlockSpec((tb, C), lambda c, t: (t, k0 + c)), pl.BlockSpec((tb, C), lambda c, t: (t, k0 + nC + c)),
         blk, blk, blk], [p, p, y_lru, y_sc, dm],
        [_S(p.shape, BF16), _S((T, D), BF16), _S((T, D), BF16)], [ANY, blk, blk],
        [pltpu.VMEM((2, tb, C), BF16), pltpu.VMEM((2, tb, C), BF16), pltpu.SemaphoreType.DMA((2, 2))], tasks)
    return (*outs, passed) if tasks else outs


def _ffn_act_fwd(uu, conv_w, *, name, tasks=()):
    T = uu.shape[0]
    F = uu.shape[1] // 2
    C = _tile(F, C_EW)
    nC = F // C
    tb = _tile(T, TB_CHUNKED, HALO)
    nT, hb = T // tb, tb // HALO
    kw = conv_w.shape[0]
    R = HALO

    def body(g_ref, gh_ref, v_ref, vh_ref, wg_ref, wv_ref, o_ref):
        t = pl.program_id(1)

        def chunk(k, carry):
            gp, vp = carry
            r0 = pl.multiple_of(k * R, R)
            ug = g_ref[pl.ds(r0, R), :].astype(F32)
            uv = v_ref[pl.ds(r0, R), :].astype(F32)
            cg = _conv(ug, gp, wg_ref)
            cv = _conv(uv, vp, wv_ref)
            o_ref[pl.ds(r0, R), :] = (cg * jax.nn.sigmoid(cg) * cv).astype(BF16)
            return ug[R - SUB:, :], uv[R - SUB:, :]

        lax.fori_loop(0, tb // R, chunk, (_prev8(gh_ref, t), _prev8(vh_ref, t)))

    seg = lambda k: pl.BlockSpec((tb, C), lambda c, t: (t, k * nC + c))
    hseg = lambda k: pl.BlockSpec((HALO, C), _halo_prev_map(hb, lambda c: k * nC + c))
    wseg = lambda k: pl.BlockSpec((kw, C), lambda c, t: (0, k * nC + c))
    outs, passed = _call(
        name, (nC, nT), body, [seg(0), hseg(0), seg(1), hseg(1), wseg(0), wseg(1)],
        [uu, uu, uu, uu, conv_w, conv_w], [_S((T, F), BF16)], [pl.BlockSpec((tb, C), lambda c, t: (t, c))], [], tasks)
    return (outs[0], passed) if tasks else outs[0]


def _ffn_act_bwd(uu, dact, conv_w, *, name):
    T = uu.shape[0]
    F = uu.shape[1] // 2
    C = _tile(F, C_EW)
    nC = F // C
    tb = _tile(T, TB_CHUNKED, HALO)
    nT, hb = T // tb, tb // HALO
    kw = conv_w.shape[0]
    R = HALO
    nk = tb // R

    def body(g_ref, gh_ref, v_ref, vh_ref, da_ref, wg_ref, wv_ref, du_ref, dwg_ref, dwv_ref,
             gn_s, vn_s, accg_s, accv_s, st_g, st_v, sems):
        c = pl.program_id(0)
        tr = pl.program_id(1)
        t = nT - 1 - tr
        first = tr == 0

        @pl.when(first)
        def _():
            gn_s[...] = jnp.zeros_like(gn_s)
            vn_s[...] = jnp.zeros_like(vn_s)
            dwg_ref[...] = jnp.zeros_like(dwg_ref)
            dwv_ref[...] = jnp.zeros_like(dwv_ref)

        accg_s[...] = jnp.zeros_like(accg_s)
        accv_s[...] = jnp.zeros_like(accv_s)
        step_no = c * nT + tr
        slot = step_no % 2

        def chunk(i, carry):
            gn, vn = carry
            k = nk - 1 - i
            r0 = pl.multiple_of(k * R, R)
            rp = pl.multiple_of(jnp.maximum(r0 - R, 0), R)
            ug = g_ref[pl.ds(r0, R), :].astype(F32)
            uv = v_ref[pl.ds(r0, R), :].astype(F32)
            gp = jnp.where(k > 0, g_ref[pl.ds(rp, R), :].astype(F32)[R - SUB:, :], _prev8(gh_ref, t))
            vp = jnp.where(k > 0, v_ref[pl.ds(rp, R), :].astype(F32)[R - SUB:, :], _prev8(vh_ref, t))
            sh_g = _shifted_down(ug, gp, kw)
            sh_v = _shifted_down(uv, vp, kw)
            cg = _taps(sh_g, wg_ref)
            cv = _taps(sh_v, wv_ref)
            da = da_ref[pl.ds(r0, R), :].astype(F32)
            sg = jax.nn.sigmoid(cg)
            d_cg = da * cv * (sg * (1.0 + cg * (1.0 - sg)))
            d_cv = da * (cg * sg)
            for j in range(kw):
                accg_s[j] += d_cg * sh_g[kw - 1 - j]
                accv_s[j] += d_cv * sh_v[kw - 1 - j]
            st_g[slot, pl.ds(r0, R), :] = _conv_t(d_cg, gn, wg_ref).astype(BF16)
            st_v[slot, pl.ds(r0, R), :] = _conv_t(d_cv, vn, wv_ref).astype(BF16)
            return d_cg[:SUB, :], d_cv[:SUB, :]

        gn, vn = lax.fori_loop(0, nk, chunk, (gn_s[...], vn_s[...]))
        gn_s[...] = gn
        vn_s[...] = vn
        for j in range(kw):
            dwg_ref[pl.ds(j, 1), :] += jnp.sum(accg_s[j], axis=0, keepdims=True)
            dwv_ref[pl.ds(j, 1), :] += jnp.sum(accv_s[j], axis=0, keepdims=True)
        def where(s):
            row0, col0 = (nT - 1 - s % nT) * tb, (s // nT) * C
            return [(row0, col0), (row0, F + col0)]

        _store_staged([st_g, st_v], du_ref, sems, step_no, nC * nT, where)

    seg = lambda k: pl.BlockSpec((tb, C), lambda c, tr: (nT - 1 - tr, k * nC + c))
    hseg = lambda k: pl.BlockSpec((HALO, C), lambda c, tr: (jnp.maximum((nT - 1 - tr) * hb - 1, 0), k * nC + c))
    wseg = lambda k: pl.BlockSpec((kw, C), lambda c, tr: (0, k * nC + c))
    dwb = pl.BlockSpec((kw, C), lambda c, tr: (0, c))
    return pl.pallas_call(
        body, name=name, out_shape=(_S(uu.shape, BF16), _S((kw, F), F32), _S((kw, F), F32)), grid=(nC, nT),
        in_specs=[seg(0), hseg(0), seg(1), hseg(1), pl.BlockSpec((tb, C), lambda c, tr: (nT - 1 - tr, c)),
                  wseg(0), wseg(1)],
        out_specs=(ANY, dwb, dwb),
        scratch_shapes=[pltpu.VMEM((SUB, C), F32), pltpu.VMEM((SUB, C), F32),
                        pltpu.VMEM((kw, R, C), F32), pltpu.VMEM((kw, R, C), F32),
                        pltpu.VMEM((2, tb, C), BF16), pltpu.VMEM((2, tb, C), BF16), pltpu.SemaphoreType.DMA((2, 2))],
        compiler_params=_cparams(("arbitrary", "arbitrary")),
    )(uu, uu, uu, uu, dact, conv_w, conv_w)


def _place():
    x, y, c = lax.axis_index("x"), lax.axis_index("y"), lax.axis_index("c")
    return x, y, c


def _chips(x, y):
    return [(1 - x, y), (x, 1 - y), (1 - x, 1 - y)]


def _all_gather(arrays, placed, over_ici, pair_n, name):
    n = len(arrays)

    def body(*refs):
        ins, outs = refs[:n], refs[n:2 * n]
        send_sems, recv_sems, local_sems = refs[2 * n:]
        x, y, c = _place()
        me, sibling = (x, y, c), (x, y, 1 - c)
        chips = _chips(x, y)
        full = [a for a in range(n) if over_ici[a]]

        def idx(px, py, pc):
            return 4 * px + 2 * py + pc

        def copy(a, k, block, to):
            dst = _dev_block(outs[a], idx(*block), pair_n[a])
            src = ins[a] if (block is me and not placed[a]) else dst
            return pltpu.make_async_remote_copy(
                src_ref=src, dst_ref=dst, send_sem=send_sems.at[a, k], recv_sem=recv_sems.at[a, k],
                device_id=to, device_id_type=MESH)

        def half(a, k, block, to, lo):
            r = rows_of[a] // 2
            blk = _rows_of(outs[a], idx(*block), (0 if lo else r, r), pair_n[a])
            return pltpu.make_async_remote_copy(
                src_ref=blk, dst_ref=blk, send_sem=send_sems.at[a, k], recv_sem=recv_sems.at[a, k],
                device_id=to, device_id_type=MESH)

        mine = [pltpu.make_async_copy(ins[a], outs[a].at[idx(*me)], local_sems.at[a])
                for a in range(n) if not placed[a]]
        for cp in mine:
            cp.start()
        chip_x, chip_y, chip_d = chips
        sent = []
        for a in full:
            sent += [copy(a, 1, me, (*chip_x, c)), copy(a, 2, me, (*chip_y, c))]
            if not relay[a]:
                sent.append(copy(a, 3, me, (*chip_d, c)))
        for a in range(n):
            sent.append(copy(a, 0, me, sibling))
        for cp in sent:
            cp.start()

        def then(cp):
            cp.start()
            sent.append(cp)

        for a in full:
            copy(a, 2, (*chip_y, c), me).wait_recv()
            if relay[a]:
                then(half(a, 3, (*chip_y, c), (*chip_x, c), True))
            then(copy(a, 6, (*chip_y, c), sibling))
            copy(a, 1, (*chip_x, c), me).wait_recv()
            if relay[a]:
                then(half(a, 4, (*chip_x, c), (*chip_y, c), False))
            then(copy(a, 5, (*chip_x, c), sibling))
        for a in full:
            if relay[a]:
                half(a, 3, (*chip_d, c), me, True).wait_recv()
                half(a, 4, (*chip_d, c), me, False).wait_recv()
            else:
                copy(a, 3, (*chip_d, c), me).wait_recv()
            then(copy(a, 7, (*chip_d, c), sibling))
        for a in range(n):
            copy(a, 0, sibling, me).wait_recv()
        for a in full:
            for j, chip in enumerate(chips):
                copy(a, 5 + j, (*chip, 1 - c), me).wait_recv()
        for cp in sent:
            cp.wait_send()
        for cp in mine:
            cp.wait()

    rows_of = [(s.shape[1] if placed[a] else s.shape[0]) for a, s in enumerate(arrays)]
    relay = [r % (2 * HALO) == 0 for r in rows_of]
    return pl.pallas_call(
        body, name=name,
        out_shape=tuple(_S(s.shape if placed[a] else (N_DEV,) + s.shape, s.dtype) for a, s in enumerate(arrays)),
        in_specs=[ANY] * n, out_specs=tuple([ANY] * n),
        scratch_shapes=[pltpu.SemaphoreType.DMA((n, 8)), pltpu.SemaphoreType.DMA((n, 8)),
                        pltpu.SemaphoreType.DMA((n,))],
        input_output_aliases={a: a for a in range(n) if placed[a]},
    )(*arrays)


def _dev_block(ref, dev, pair_n=None):
    if pair_n is None:
        return ref.at[dev]
    return ref.at[dev // 2, :, pl.ds(pl.multiple_of((dev % 2) * pair_n, LANES), pair_n)]


def _rows_of(ref, blk, rows, pair_n=None):
    v = _dev_block(ref, blk, pair_n)
    return v if rows is None else v.at[pl.ds(rows[0], rows[1])]


ALL_ROWS = "all"


def _gather_task(buf, ici=None, fwd=None, pair_n=None):
    blk_of = functools.partial(_rows_of, pair_n=pair_n)
    r0, nr = (0, buf.shape[1]) if ici == ALL_ROWS else (ici or (0, 0))
    assert nr % (2 * HALO) == 0
    lo, hi = (r0, nr // 2), (r0 + nr // 2, nr // 2)
    both = (r0, nr)
    fwd_rows = None if fwd == ALL_ROWS else fwd

    def remote(refs, ss, rs, k, dev, rows, to):
        blk = blk_of(refs[0], dev, rows)
        return pltpu.make_async_remote_copy(src_ref=blk, dst_ref=blk, send_sem=ss.at[k], recv_sem=rs.at[k],
                                            device_id=to, device_id_type=MESH)

    def waves(refs, ss, rs):
        x, y, c = _place()
        me = 4 * x + 2 * y + c
        (xx, xy), (yx, yy), _ = _chips(x, y)
        dev_x, dev_y = 4 * xx + 2 * xy + c, 4 * yx + 2 * yy + c
        first, second = [], []
        if ici is not None:
            first += [remote(refs, ss, rs, 0, me, both, (xx, xy, c)), remote(refs, ss, rs, 1, me, both, (yx, yy, c))]
            second += [remote(refs, ss, rs, 2, dev_y, lo, (xx, xy, c)), remote(refs, ss, rs, 3, dev_x, hi, (yx, yy, c))]
        if fwd is not None:
            first += [remote(refs, ss, rs, 4 + j, 4 * px + 2 * py + c, fwd_rows, (x, y, 1 - c))
                      for j, (px, py) in enumerate(_chips(x, y))]
        return first, second

    def start(refs, ss, rs, ls):
        for cp in waves(refs, ss, rs)[0]:
            cp.start()

    def mid(refs, ss, rs, ls):
        x, y, c = _place()
        (xx, xy), (yx, yy), _ = _chips(x, y)
        remote(refs, ss, rs, 0, 4 * xx + 2 * xy + c, both, (x, y, c)).wait_recv()
        remote(refs, ss, rs, 1, 4 * yx + 2 * yy + c, both, (x, y, c)).wait_recv()
        for cp in waves(refs, ss, rs)[1]:
            cp.start()

    def wait(refs, ss, rs, ls):
        x, y, c = _place()
        chips = _chips(x, y)
        if ici is not None:
            dev_d = 4 * chips[2][0] + 2 * chips[2][1] + c
            remote(refs, ss, rs, 2, dev_d, lo, (x, y, c)).wait_recv()
            remote(refs, ss, rs, 3, dev_d, hi, (x, y, c)).wait_recv()
        if fwd is not None:
            for j, (px, py) in enumerate(chips):
                remote(refs, ss, rs, 4 + j, 4 * px + 2 * py + 1 - c, fwd_rows, (x, y, c)).wait_recv()
        first, second = waves(refs, ss, rs)
        for cp in first + second:
            cp.wait_send()

    return _Task([buf], [0], start, wait, nsem=7, mid=mid if ici is not None else None)


def _exchange_task(parts, landing, rows=None):
    def copies(refs, ss, rs):
        x, y, c = _place()
        myq = 2 * x + y
        return [pltpu.make_async_remote_copy(
            src_ref=_rows_of(refs[0], 2 * px + py, rows), dst_ref=_rows_of(refs[1], myq, rows),
            send_sem=ss.at[k], recv_sem=rs.at[k], device_id=(px, py, c), device_id_type=MESH)
            for k, (px, py) in enumerate(_chips(x, y))]

    def start(refs, ss, rs, ls):
        for cp in copies(refs, ss, rs):
            cp.start()

    def wait(refs, ss, rs, ls):
        x, y, c = _place()
        for k, (px, py) in enumerate(_chips(x, y)):
            pltpu.make_async_remote_copy(
                src_ref=_rows_of(refs[0], 2 * x + y, rows), dst_ref=_rows_of(refs[1], 2 * px + py, rows),
                send_sem=ss.at[k], recv_sem=rs.at[k], device_id=(px, py, c), device_id_type=MESH).wait_recv()
        for cp in copies(refs, ss, rs):
            cp.wait_send()

    return _Task([parts, landing], [1], start, wait)


def _core_blocks(g, pair_n):
    if pair_n is None:
        g4 = g.reshape((N_CHIP, 2) + g.shape[1:])
        return g4, (N_CHIP,) + g.shape[1:], lambda ref, c: ref.at[:, c]
    view = lambda ref, c: ref.at[:, :, pl.ds(pl.multiple_of(c * pair_n, LANES), pair_n)]
    return g, (N_CHIP, g.shape[1], pair_n), view


def _swap_task(g, pair_n=None):
    g4, shape, view = _core_blocks(g, pair_n)

    def copy(refs, ss, rs):
        x, y, c = _place()
        return pltpu.make_async_remote_copy(
            src_ref=view(refs[0], 1 - c), dst_ref=refs[1], send_sem=ss.at[0], recv_sem=rs.at[0],
            device_id=(x, y, 1 - c), device_id_type=MESH)

    def start(refs, ss, rs, ls):
        copy(refs, ss, rs).start()

    def wait(refs, ss, rs, ls):
        copy(refs, ss, rs).wait()

    return _Task([g4], [], start, wait, fresh=[_S(shape, g.dtype)], nsem=1)


def _peer(x, y, c, m):
    return x ^ (m >> 2), y ^ ((m >> 1) & 1), c ^ (m & 1)


def _bcast_task(pack):
    def copies(refs, ss, rs):
        x, y, c = _place()
        me = 4 * x + 2 * y + c
        return [pltpu.make_async_remote_copy(
            src_ref=refs[0], dst_ref=refs[1].at[me], send_sem=ss.at[m - 1], recv_sem=rs.at[m - 1],
            device_id=_peer(x, y, c, m), device_id_type=MESH) for m in range(1, N_DEV)]

    def local(refs, ls):
        x, y, c = _place()
        return pltpu.make_async_copy(refs[0], refs[1].at[4 * x + 2 * y + c], ls.at[0])

    def start(refs, ss, rs, ls):
        local(refs, ls).start()
        for cp in copies(refs, ss, rs):
            cp.start()

    def wait(refs, ss, rs, ls):
        x, y, c = _place()
        for m in range(1, N_DEV):
            px, py, pc = _peer(x, y, c, m)
            pltpu.make_async_remote_copy(
                src_ref=refs[0], dst_ref=refs[1].at[4 * px + 2 * py + pc], send_sem=ss.at[m - 1],
                recv_sem=rs.at[m - 1], device_id=(px, py, pc), device_id_type=MESH).wait_recv()
        for cp in copies(refs, ss, rs):
            cp.wait_send()
        local(refs, ls).wait()

    return _Task([pack], [], start, wait, fresh=[_S((N_DEV,) + pack.shape, pack.dtype)], nsem=N_DEV - 1)


def _sum_packs(packs, name):
    _, R, L = packs.shape

    def body(p_ref, o_ref):
        acc = p_ref[0]
        for k in range(1, N_DEV):
            acc = acc + p_ref[k]
        o_ref[...] = acc

    return pl.pallas_call(body, name=name, out_shape=_S((R, L), packs.dtype), in_specs=[VMEM_SPEC],
                          out_specs=VMEM_SPEC, compiler_params=_cparams())(packs)


def _swap_halves(g, name, pair_n=None):
    g4, shape, view = _core_blocks(g, pair_n)

    def body(g_ref, o_ref, send_sem, recv_sem):
        x, y, c = _place()
        cp = pltpu.make_async_remote_copy(
            src_ref=view(g_ref, 1 - c), dst_ref=o_ref, send_sem=send_sem, recv_sem=recv_sem,
            device_id=(x, y, 1 - c), device_id_type=MESH)
        cp.start()
        cp.wait()

    return pl.pallas_call(
        body, name=name, out_shape=_S(shape, g.dtype), in_specs=[ANY], out_specs=ANY,
        scratch_shapes=[pltpu.SemaphoreType.DMA, pltpu.SemaphoreType.DMA],
    )(g4)


def _add_halves(g, landed, place, name, pair_n=None):
    _, r, cc = landed.shape
    tr = _tile(r, 512, HALO)
    if pair_n is None:
        g4 = g.reshape(N_CHIP, 2, r, cc)
        g_spec = pl.BlockSpec((None, None, tr, cc), lambda i, q, s: (q, s[0], i, 0))
    else:
        g4 = g
        g_spec = pl.BlockSpec((None, tr, cc), lambda i, q, s: (q, i, s[0]))

    def body(s_ref, g_ref, l_ref, o_ref, land_ref):
        q = pl.program_id(1)
        v = (g_ref[...].astype(F32) + l_ref[...].astype(F32)).astype(BF16)
        o_ref[...] = v

        @pl.when(q == s_ref[1])
        def _():
            land_ref[...] = v

    return pl.pallas_call(
        body, name=name, out_shape=(_S((N_CHIP, r, cc), BF16), _S((N_CHIP, r, cc), BF16)),
        grid_spec=pltpu.PrefetchScalarGridSpec(
            num_scalar_prefetch=1, grid=(r // tr, N_CHIP),
            in_specs=[g_spec,
                      pl.BlockSpec((None, tr, cc), lambda i, q, s: (q, i, 0))],
            out_specs=(pl.BlockSpec((None, tr, cc), lambda i, q, s: (q, i, 0)),
                       pl.BlockSpec((None, tr, cc), lambda i, q, s: (s[1], i, 0)))),
        compiler_params=_cparams(("arbitrary", "arbitrary")),
    )(place, g4, landed)


def _all_reduce_small(pack, name):
    R = pack.shape[0]

    def body(p_ref, o_ref, buf, send_sems, recv_sems):
        x, y, c = _place()
        me = 4 * x + 2 * y + c
        buf[me] = p_ref[...]
        cps = []
        for k in range(N_DEV - 1):
            m = k + 1
            peer = (x ^ (m >> 2), y ^ ((m >> 1) & 1), c ^ (m & 1))
            cps.append(pltpu.make_async_remote_copy(
                src_ref=p_ref, dst_ref=buf.at[me], send_sem=send_sems.at[k], recv_sem=recv_sems.at[k],
                device_id=peer, device_id_type=MESH))
        for cp in cps:
            cp.start()
        for k in range(N_DEV - 1):
            m = k + 1
            peer_idx = 4 * (x ^ (m >> 2)) + 2 * (y ^ ((m >> 1) & 1)) + (c ^ (m & 1))
            pltpu.make_async_remote_copy(
                src_ref=p_ref, dst_ref=buf.at[peer_idx], send_sem=send_sems.at[k], recv_sem=recv_sems.at[k],
                device_id=(x, y, c), device_id_type=MESH).wait_recv()
        for cp in cps:
            cp.wait_send()
        acc = buf[0]
        for k in range(1, N_DEV):
            acc = acc + buf[k]
        o_ref[...] = acc

    return pl.pallas_call(
        body, name=name, out_shape=_S((R, LANES), F32),
        in_specs=[VMEM_SPEC], out_specs=VMEM_SPEC,
        scratch_shapes=[pltpu.VMEM((N_DEV, R, LANES), F32), pltpu.SemaphoreType.DMA((N_DEV - 1,)),
                        pltpu.SemaphoreType.DMA((N_DEV - 1,))],
        compiler_params=_cparams(),
    )(pack)


def _adamw_math(w, g, m, v):
    m = ADAM_B1 * m + (1.0 - ADAM_B1) * g
    v = ADAM_B2 * v + (1.0 - ADAM_B2) * (g * g)
    m_hat = m / (1.0 - ADAM_B1 ** ADAM_STEP)
    v_hat = v / (1.0 - ADAM_B2 ** ADAM_STEP)
    delta = -ADAM_LR * (m_hat / (jnp.sqrt(v_hat) + ADAM_EPS) + ADAM_WD * w)
    return delta, m, v


def _adamw_block(p_ref, w_ref, m_ref, v_ref, g_ref, d_ref, nm_ref, nv_ref):
    g = p_ref[0].astype(F32)
    for q in range(1, N_CHIP):
        g = g + p_ref[q].astype(F32)
    g_ref[...] = g
    d_ref[...], nm_ref[...], nv_ref[...] = _adamw_math(w_ref[...], g, m_ref[...], v_ref[...])


def _adamw_big(parts, w, m, v, name):
    r, cc = w.shape
    tr = _tile(r, 128, HALO)
    body = functools.partial(_adamw_block)

    blk = pl.BlockSpec((tr, cc), lambda i: (i, 0))
    return pl.pallas_call(
        body, name=name, out_shape=tuple(_S((r, cc), F32) for _ in range(4)), grid=(r // tr,),
        in_specs=[pl.BlockSpec((N_CHIP, tr, cc), lambda i: (0, i, 0)), blk, blk, blk],
        out_specs=(blk, blk, blk, blk), compiler_params=_cparams(("parallel",)),
    )(parts, w, m, v)


def _adamw_small(ws, gs, ms, vs, name):
    n = len(ws)

    def body(*refs):
        w_r, g_r, m_r, v_r = refs[:n], refs[n:2 * n], refs[2 * n:3 * n], refs[3 * n:4 * n]
        d_r, nm_r, nv_r = refs[4 * n:5 * n], refs[5 * n:6 * n], refs[6 * n:7 * n]
        for k in range(n):
            d_r[k][...], nm_r[k][...], nv_r[k][...] = _adamw_math(w_r[k][...], g_r[k][...], m_r[k][...], v_r[k][...])

    shapes = tuple(_S(w.shape, F32) for w in ws)
    outs = pl.pallas_call(
        body, name=name, out_shape=shapes * 3,
        in_specs=[VMEM_SPEC] * (4 * n), out_specs=tuple([VMEM_SPEC] * (3 * n)),
        compiler_params=_cparams(),
    )(*ws, *gs, *ms, *vs)
    return outs[:n], outs[n:2 * n], outs[2 * n:]


def _block_diag(w, heads_per_block):
    H, hd, _ = w.shape
    nb = H // heads_per_block
    eye = jnp.eye(heads_per_block, dtype=w.dtype)
    w4 = w.reshape(nb, heads_per_block, hd, hd)
    return jnp.einsum("nhab,hg->nhagb", w4, eye).reshape(nb, heads_per_block * hd, heads_per_block * hd)


def _diag_blocks(bd, heads_per_block, hd):
    nb = bd.shape[0]
    b5 = bd.reshape(nb, heads_per_block, hd, heads_per_block, hd)
    return jnp.stack([b5[:, h, :, h, :] for h in range(heads_per_block)], axis=1).reshape(nb * heads_per_block, hd, hd)


def _as_rows(a):
    if a.ndim == 1:
        return a.reshape(-1, LANES) if a.shape[0] % LANES == 0 else a.reshape(1, -1)
    if a.ndim == 3:
        return a.reshape(-1, LANES) if (a.size % LANES == 0) else a.reshape(a.shape[0] * a.shape[1], a.shape[2])
    return a


def kernel(x, g_mix, w_in, lru_conv_w, lru_conv_b, lru_wa, lru_ba, lru_wx, lru_bx, lru_lambda, lru_w_out, sc_conv_w, sc_w_out, w_o, g_ffn, ffn_w_up, ffn_conv_w, ffn_w_down, g_final, loss_target, m_g_mix, m_w_in, m_lru_conv_w, m_lru_conv_b, m_lru_wa, m_lru_ba, m_lru_wx, m_lru_bx, m_lru_lambda, m_lru_w_out, m_sc_conv_w, m_sc_w_out, m_w_o, m_g_ffn, m_ffn_w_up, m_ffn_conv_w, m_ffn_w_down, m_g_final, v_g_mix, v_w_in, v_lru_conv_w, v_lru_conv_b, v_lru_wa, v_lru_ba, v_lru_wx, v_lru_bx, v_lru_lambda, v_lru_w_out, v_sc_conv_w, v_sc_w_out, v_w_o, v_g_ffn, v_ffn_w_up, v_ffn_conv_w, v_ffn_w_down, v_g_final):
    T, D = x.shape[1], x.shape[2]
    d_lru = lru_lambda.shape[0]
    d_sc = sc_conv_w.shape[1] * N_DEV
    F = ffn_w_down.shape[0] * N_DEV
    H = lru_wa.shape[0]
    assert d_lru == d_sc and H * HEAD_DIM == d_lru
    xs = x.reshape(T, D)
    tgt = loss_target.reshape(T, D)
    my_x, my_y, my_c = _place()
    me = 4 * my_x + 2 * my_y + my_c

    big = [w_in, lru_w_out, sc_w_out, w_o, ffn_w_up, ffn_w_down]
    big_names = ["w_in", "lru_w_out", "sc_w_out", "w_o", "ffn_w_up", "ffn_w_down"]
    place = jnp.stack([my_c, 2 * my_x + my_y, me]).astype(jnp.int32)
    n_in, n_up = w_in.shape[1], ffn_w_up.shape[1]
    paired = [n_in, None, None, None, n_up, None]
    big_bf = [_cast_into_slot(w, place, "cast_" + nm, paired=pn is not None)
              for w, nm, pn in zip(big, big_names, paired)]
    pad_rows = lambda a: jnp.pad(a, ((0, SUB - a.shape[0]), (0, 0)))
    gathered = _all_gather(big_bf + [pad_rows(lru_conv_w), pad_rows(sc_conv_w), pad_rows(ffn_conv_w)],
                           [True] * 6 + [False] * 3,
                           [True, False, False, False, False, False, True, True, True],
                           paired + [None] * 3, "all_gather_first")
    W_in, W_lo, W_so, W_o8, W_up, W_dn8 = gathered[:6]
    full_cols = lambda g, kw: g[:, :kw, :].transpose(1, 0, 2).reshape(kw, -1)
    cw_lru = full_cols(gathered[6], lru_conv_w.shape[0])
    cw_sc = full_cols(gathered[7], sc_conv_w.shape[0])
    cw_ffn = full_cols(gathered[8], ffn_conv_w.shape[0])

    C = _tile(d_lru, C_LRU)
    hpb = C // HEAD_DIM
    wa_bd = _block_diag(lru_wa, hpb).astype(BF16)
    wx_bd = _block_diag(lru_wx, hpb).astype(BF16)
    cb, ba, bx, lam = (a.reshape(1, d_lru) for a in (lru_conv_b, lru_ba, lru_bx, lru_lambda))

    h1 = _rms_fwd(xs, g_mix, "rms_mix")
    k8 = W_up.shape[1] // 8
    wide = 2 * max(n_in, n_up)
    p, ((W_o8,), (W_lo,), (W_so,), (W_up,)) = _mm_nn(
        h1, W_in, out_dtype=BF16, name="mm_in", tn=wide,
        tasks=[_gather_task(W_o8, ici=ALL_ROWS), _gather_task(W_lo, ici=ALL_ROWS), _gather_task(W_so, ici=ALL_ROWS),
               _gather_task(W_up, ici=(0, 4 * k8), pair_n=n_up)])
    hs, yl_pre, ((W_o8,), (W_lo,), (W_so,), (W_up,)) = _lru_fwd(
        p, cw_lru, cb, wa_bd, ba, wx_bd, bx, lam, name="lru_fwd",
        tasks=[_gather_task(W_o8, fwd=ALL_ROWS), _gather_task(W_lo, fwd=ALL_ROWS), _gather_task(W_so, fwd=ALL_ROWS),
               _gather_task(W_up, ici=(4 * k8, 3 * k8), fwd=(0, 4 * k8), pair_n=n_up)])
    ys_pre = _sc_fwd(p, cw_sc, d=d_sc, name="sc_fwd")
    y_lru, ((W_up,),) = _mm_small(
        "nn", yl_pre, None, W_lo, name="mm_lru_out",
        tasks=[_gather_task(W_up, ici=(7 * k8, k8), fwd=(4 * k8, 3 * k8), pair_n=n_up)])
    y_sc, ((W_up,),) = _mm_small("nn", ys_pre, None, W_so, name="mm_sc_out",
                                 tasks=[_gather_task(W_up, fwd=(7 * k8, k8), pair_n=n_up)])
    gate0 = 2 * d_lru + 3 * d_sc
    merged = _merge_fwd(p, y_lru, y_sc, col0=gate0, name="merge_fwd")
    W_o = W_o8.reshape(1, D, D)
    x1 = _mm_nn(merged, W_o, out_dtype=F32, residual=xs, name="mm_o")
    h2 = _rms_fwd(x1, g_ffn, "rms_ffn")
    uu, ((W_dn8,),) = _mm_nn(h2, W_up, out_dtype=BF16, name="mm_up", tn=wide,
                             tasks=[_gather_task(W_dn8, ici=ALL_ROWS)])
    act, ((W_dn8,),) = _ffn_act_fwd(uu, cw_ffn, name="ffn_act_fwd", tasks=[_gather_task(W_dn8, fwd=ALL_ROWS)])
    W_dn = W_dn8.reshape(1, F, D)
    x2 = _mm_nn(act, W_dn, out_dtype=F32, residual=x1, name="mm_down", tn=1024, tk=F)
    dx2b, loss_part, dg_final = _loss_head(x2, g_final, tgt, "loss_head")

    def pack_rows(arrs):
        flat = jnp.concatenate([a.reshape(-1) for a in arrs])
        rows = -(-flat.shape[0] // (SUB * LANES)) * SUB
        return jnp.pad(flat, (0, rows * LANES - flat.shape[0])).reshape(rows, LANES)

    def unpack_rows(pack, arrs):
        flat, out, o = pack.reshape(-1), [], 0
        for a in arrs:
            out.append(flat[o:o + a.size].reshape(a.shape))
            o += a.size
        return out

    dact = _mm_nt(dx2b, W_dn, out_dtype=BF16, name="mm_down_dx", tm=512, tko=F // 2, tn=D)
    gW_dn = _mm_tn(act, dx2b, 1, out_dtype=BF16, name="mm_down_dw", tk=1408, tt=2048).reshape(N_DEV, F // N_DEV, D)
    duu, dcw_ffn_g, dcw_ffn_v = _ffn_act_bwd(uu, dact, cw_ffn, name="ffn_act_bwd")
    dh2, ((land_dn,),) = _mm_nt(duu, W_up, out_dtype=BF16, name="mm_up_dx", tn=wide, tasks=[_swap_task(gW_dn)])
    parts_dn = _add_halves(gW_dn, land_dn, place, "rs_add_ffn_w_down")
    gW_up, ((mine_dn,),) = _mm_tn(h2, duu, N_CHIP, out_dtype=BF16, name="mm_up_dw", tk=512, tn=wide, tt=2048,
                                  tasks=[_exchange_task(*parts_dn)])
    dx1b, dg_ffn = _rms_bwd(x1, g_ffn, dh2, dx2b, "rms_ffn_bwd", BF16)
    dmerged, ((land_up,),) = _mm_nt(dx1b, W_o, out_dtype=BF16, name="mm_o_dx", tn=D,
                                    tasks=[_swap_task(gW_up, pair_n=n_up)])
    parts_up, land_up = _add_halves(gW_up, land_up, place, "rs_add_ffn_w_up", pair_n=n_up)
    r8 = parts_up.shape[1] // 8
    gW_o, ((land_up,),) = _mm_tn(merged, dx1b, 1, out_dtype=BF16, name="mm_o_dw", tt=2048,
                                 tasks=[_exchange_task(parts_up, land_up, rows=(0, r8))])
    gW_o = gW_o.reshape(N_DEV, D // N_DEV, D)
    dp, dy_lru, dy_sc, ((land_up,),) = _merge_bwd(
        p, y_lru, y_sc, dmerged, col0=gate0, name="merge_bwd",
        tasks=[_exchange_task(parts_up, land_up, rows=(r8, r8))])
    dyl_pre, ((land_o,),) = _mm_small("nt", None, dy_lru, W_lo, name="mm_lru_out_dx", tasks=[_swap_task(gW_o)])
    parts_o = _add_halves(gW_o, land_o, place, "rs_add_w_o")
    gW_lo = _mm_small("tn", yl_pre, dy_lru, W_lo, name="mm_lru_out_dw")
    dys_pre, ((land_lo,),) = _mm_small("nt", None, dy_sc, W_so, name="mm_sc_out_dx", tasks=[_swap_task(gW_lo)])
    parts_lo = _add_halves(gW_lo, land_lo, place, "rs_add_lru_w_out")
    gW_so = _mm_small("tn", ys_pre, dy_sc, W_so, name="mm_sc_out_dw")
    dp, dcw_sc, ((land_up,),) = _sc_bwd(p, dys_pre, dp, cw_sc, d=d_sc, name="sc_bwd",
                                        tasks=[_exchange_task(parts_up, land_up, rows=(2 * r8, r8))])
    dp, dcw_lru, dcb, dwa_bd, dba, dwx_bd, dbx, dlam, ((land_up,), (mine_o,), (mine_lo,), (land_so,)) = _lru_bwd(
        p, hs, dyl_pre, dp, cw_lru, cb, wa_bd, ba, wx_bd, bx, lam, name="lru_bwd",
        tasks=[_exchange_task(parts_up, land_up, rows=(3 * r8, 3 * r8)), _exchange_task(*parts_o),
               _exchange_task(*parts_lo), _swap_task(gW_so)])
    parts_so = _add_halves(gW_so, land_so, place, "rs_add_sc_w_out")

    dwa = _diag_blocks(dwa_bd, hpb, HEAD_DIM)
    dwx = _diag_blocks(dwx_bd, hpb, HEAD_DIM)
    dcw_ffn = jnp.concatenate([dcw_ffn_g, dcw_ffn_v], axis=1)
    rep_grads = [dcb, dwa, dba, dwx, dbx, dlam, dg_ffn, dg_final]
    small_full = rep_grads + [dcw_lru, dcw_sc, dcw_ffn]
    gW_in, ((mine_up,), (mine_so,), (packs,)) = _mm_tn(
        h1, dp, N_CHIP, out_dtype=BF16, name="mm_in_dw", tk=512, tn=wide, tt=2048,
        tasks=[_exchange_task(parts_up, land_up, rows=(6 * r8, 2 * r8)), _exchange_task(*parts_so),
               _bcast_task(pack_rows(small_full))])
    land_in = _swap_halves(gW_in, "rs_swap_w_in", pair_n=n_in)
    parts_in = _add_halves(gW_in, land_in, place, "rs_add_w_in", pair_n=n_in)
    dh1, ((mine_in,),) = _mm_nt(dp, W_in, out_dtype=BF16, name="mm_in_dx", tn=wide,
                                tasks=[_exchange_task(*parts_in)])
    grad_x, dg_mix = _rms_bwd(xs, g_mix, dh1, dx1b, "rms_mix_bwd", F32)

    mine = [mine_in, mine_lo, mine_so, mine_o, mine_up, mine_dn]
    big_m = [m_w_in, m_lru_w_out, m_sc_w_out, m_w_o, m_ffn_w_up, m_ffn_w_down]
    big_v = [v_w_in, v_lru_w_out, v_sc_w_out, v_w_o, v_ffn_w_up, v_ffn_w_down]
    big_out = {nm: _adamw_big(pt, w, m, v, "adamw_" + nm)
               for nm, pt, w, m, v in zip(big_names, mine, big, big_m, big_v)}

    (scb, swa, sba, swx, sbx, slam, sg_ffn, sg_final, scw_lru, scw_sc, scw_ffn) = unpack_rows(
        _sum_packs(packs, "sum_small"), small_full)
    (sg_mix,) = unpack_rows(_all_reduce_small(pack_rows([dg_mix]), "all_reduce_g_mix"), [dg_mix])

    def my_cols(a):
        n = a.shape[1] // N_DEV
        return lax.dynamic_slice_in_dim(a, me * n, n, axis=1)

    small_names = ["g_mix", "lru_conv_w", "lru_conv_b", "lru_wa", "lru_ba", "lru_wx", "lru_bx", "lru_lambda",
                   "sc_conv_w", "g_ffn", "ffn_conv_w", "g_final"]
    small_w = [g_mix, lru_conv_w, lru_conv_b, lru_wa, lru_ba, lru_wx, lru_bx, lru_lambda, sc_conv_w, g_ffn,
               ffn_conv_w, g_final]
    small_m = [m_g_mix, m_lru_conv_w, m_lru_conv_b, m_lru_wa, m_lru_ba, m_lru_wx, m_lru_bx, m_lru_lambda,
               m_sc_conv_w, m_g_ffn, m_ffn_conv_w, m_g_final]
    small_v = [v_g_mix, v_lru_conv_w, v_lru_conv_b, v_lru_wa, v_lru_ba, v_lru_wx, v_lru_bx, v_lru_lambda,
               v_sc_conv_w, v_g_ffn, v_ffn_conv_w, v_g_final]
    small_g = [sg_mix.reshape(D), my_cols(scw_lru), scb.reshape(d_lru), swa, sba.reshape(d_lru), swx,
               sbx.reshape(d_lru), slam.reshape(d_lru), my_cols(scw_sc), sg_ffn.reshape(D), my_cols(scw_ffn),
               sg_final.reshape(D)]
    sd, snm, snv = _adamw_small([_as_rows(a) for a in small_w], [_as_rows(a) for a in small_g],
                                [_as_rows(a) for a in small_m], [_as_rows(a) for a in small_v], "adamw_small")
    small_out = {nm: (g, d.reshape(w.shape), nm_.reshape(w.shape), nv_.reshape(w.shape))
                 for nm, w, g, d, nm_, nv_ in zip(small_names, small_w, small_g, sd, snm, snv)}

    loss = lax.psum(loss_part[0, 0], AXES)
    order = ["g_mix", "w_in", "lru_conv_w", "lru_conv_b", "lru_wa", "lru_ba", "lru_wx", "lru_bx", "lru_lambda",
             "lru_w_out", "sc_conv_w", "sc_w_out", "w_o", "g_ffn", "ffn_w_up", "ffn_conv_w", "ffn_w_down", "g_final"]
    res = {**big_out, **small_out}
    return (loss, grad_x.reshape(x.shape),
            *[res[nm][0] for nm in order], *[res[nm][1] for nm in order],
            *[res[nm][2] for nm in order], *[res[nm][3] for nm in order])
```

```python
import functools
import math

import jax
import jax.numpy as jnp
from jax import lax
from jax.experimental import pallas as pl
from jax.experimental.pallas import tpu as pltpu

F32, BF16 = jnp.float32, jnp.bfloat16
MESH = pl.DeviceIdType.MESH
N_DEV = 8
N_CHIP = 4
AXES = ("x", "y", "c")

EPS = 1e-6
LRU_C = 8.0
HEAD_DIM = 64
ADAM_LR, ADAM_B1, ADAM_B2, ADAM_EPS, ADAM_WD, ADAM_STEP = 0.001, 0.9, 0.999, 1e-08, 0.01, 10

VMEM_LIMIT = 48 * 1024 * 1024
LANES = 128
SUB = 8
HALO = 16
TB = 512
TB_CHUNKED = 2048
C_LRU = 256
C_EW = 512
TM, TN, TK = 512, 1536, 2048


def _tile(n, pref, align=LANES):
    best = None
    for d in range(align, min(n, pref) + 1, align):
        if n % d == 0:
            best = d
    return best or n


def _cparams(sem=None, vmem=VMEM_LIMIT):
    kw = dict(vmem_limit_bytes=vmem)
    if sem is not None:
        kw["dimension_semantics"] = sem
    return pltpu.CompilerParams(**kw)


def _S(shape, dtype):
    return jax.ShapeDtypeStruct(shape, dtype)


ANY = pl.BlockSpec(memory_space=pl.ANY)
VMEM_SPEC = pl.BlockSpec(memory_space=pltpu.VMEM)


class _Task:
    def __init__(self, arrays, aliased, start, wait, fresh=(), nsem=3, mid=None):
        self.arrays, self.aliased, self.start, self.wait = arrays, aliased, start, wait
        self.fresh, self.nsem, self.mid = list(fresh), nsem, mid


def _call(name, grid, compute, in_specs, args, out_shape, out_specs, scratch, tasks=(), own_aliases=None):
    n_in, n_out, n_scr = len(args), len(out_shape), len(scratch)
    x_in, x_out, aliases, where = [], [], dict(own_aliases or {}), []
    for t in tasks:
        places = []
        for k, arr in enumerate(t.arrays):
            if k in t.aliased:
                aliases[n_in + len(x_in)] = n_out + len(x_out)
                places.append(("out", len(x_out)))
                x_out.append(_S(arr.shape, arr.dtype))
            else:
                places.append(("in", len(x_in)))
            x_in.append(arr)
        for shp in t.fresh:
            places.append(("out", len(x_out)))
            x_out.append(shp)
        where.append(places)
    n_xi, n_xo = len(x_in), len(x_out)

    def body(*refs):
        ins, xi = refs[:n_in], refs[n_in:n_in + n_xi]
        o0 = n_in + n_xi
        outs, xo = refs[o0:o0 + n_out], refs[o0 + n_out:o0 + n_out + n_xo]
        s0 = o0 + n_out + n_xo
        scr, sems = refs[s0:s0 + n_scr], refs[s0 + n_scr:]
        ids = [pl.program_id(a) for a in range(len(grid))]

        def task_refs(ti):
            return [xo[i] if kind == "out" else xi[i] for kind, i in where[ti]]

        if tasks:
            first = functools.reduce(jnp.logical_and, [i == 0 for i in ids])

            @pl.when(first)
            def _():
                for ti, t in enumerate(tasks):
                    t.start(task_refs(ti), *sems[3 * ti:3 * ti + 3])

        compute(*ins, *outs, *scr)
        if any(t.mid is not None for t in tasks):
            n_steps = math.prod(grid)
            step = functools.reduce(lambda s, ig: s * ig[1] + ig[0], zip(ids, grid), 0)

            @pl.when(step == (5 * n_steps) // 8)
            def _():
                for ti, t in enumerate(tasks):
                    if t.mid is not None:
                        t.mid(task_refs(ti), *sems[3 * ti:3 * ti + 3])

        if tasks:
            last = functools.reduce(jnp.logical_and, [i == g - 1 for i, g in zip(ids, grid)])

            @pl.when(last)
            def _():
                for ti, t in enumerate(tasks):
                    t.wait(task_refs(ti), *sems[3 * ti:3 * ti + 3])

    sem_shapes = []
    for t in tasks:
        sem_shapes += [pltpu.SemaphoreType.DMA((t.nsem,)), pltpu.SemaphoreType.DMA((t.nsem,)),
                       pltpu.SemaphoreType.DMA((1,))]
    res = pl.pallas_call(
        body, name=name, grid=grid,
        in_specs=list(in_specs) + [ANY] * n_xi,
        out_specs=tuple(out_specs) + (ANY,) * n_xo,
        out_shape=tuple(out_shape) + tuple(x_out),
        scratch_shapes=list(scratch) + sem_shapes,
        input_output_aliases=aliases,
        compiler_params=_cparams(("arbitrary",) * len(grid)),
    )(*args, *x_in)
    outs, passed, o = res[:n_out], [], n_out
    for places in where:
        k = sum(1 for kind, _ in places if kind == "out")
        passed.append(list(res[o:o + k]))
        o += k
    return outs, passed


def _mm_nn(a, w3, *, out_dtype, name, residual=None, tm=TM, tn=TN, tk=TK, tasks=(), norm_g=None):
    M, K = a.shape
    G, _, n = w3.shape
    tm, tn, tk = _tile(M, tm, SUB), _tile(n, tn), _tile(K, tk)
    nj, nk = n // tn, K // tk
    assert norm_g is None or (G == 1 and nj == 1)

    def compute(*refs):
        refs = list(refs)
        a_ref, w_ref = refs[:2]
        rest = refs[2:]
        r_ref = rest.pop(0) if residual is not None else None
        g_ref = rest.pop(0) if norm_g is not None else None
        o_ref = rest.pop(0)
        h_ref = rest.pop(0) if norm_g is not None else None

        def finish(r):
            if r_ref is not None:
                r = r + r_ref[...]
            o_ref[...] = r.astype(o_ref.dtype)
            if h_ref is not None:
                rstd = lax.rsqrt(jnp.mean(r * r, axis=-1, keepdims=True) + EPS)
                h_ref[...] = (r * rstd * g_ref[...]).astype(BF16)

        if nk == 1:
            finish(jnp.dot(a_ref[...], w_ref[...], preferred_element_type=F32))
            return
        acc = refs[-1]
        k = pl.program_id(3)

        @pl.when(k == 0)
        def _():
            acc[...] = jnp.zeros_like(acc)

        acc[...] += jnp.dot(a_ref[...], w_ref[...], preferred_element_type=F32)

        @pl.when(k == nk - 1)
        def _():
            finish(acc[...])

    in_specs = [pl.BlockSpec((tm, tk), lambda g, j, i, k: (i, k)),
                pl.BlockSpec((None, tk, tn), lambda g, j, i, k: (g, k, j))]
    args = [a, w3]
    if residual is not None:
        in_specs.append(pl.BlockSpec((tm, tn), lambda g, j, i, k: (i, g * nj + j)))
        args.append(residual)
    o_blk = pl.BlockSpec((tm, tn), lambda g, j, i, k: (i, g * nj + j))
    out_shape, out_specs = [_S((M, G * n), out_dtype)], [o_blk]
    if norm_g is not None:
        in_specs.append(pl.BlockSpec((1, tn), lambda g, j, i, k: (0, 0)))
        args.append(norm_g.reshape(1, n))
        out_shape.append(_S((M, n), BF16))
        out_specs.append(o_blk)
    outs, passed = _call(
        name, (G, nj, M // tm, nk), compute, in_specs, args, out_shape, out_specs,
        [] if nk == 1 else [pltpu.VMEM((tm, tn), F32)], tasks)
    out = outs[0] if norm_g is None else tuple(outs)
    return (out, passed) if tasks else out


def _mm_nt(dy, w3, *, out_dtype, name, tm=1024, tko=1024, tn=TN, tasks=()):
    M, _ = dy.shape
    G, K, n = w3.shape
    tm, tko, tn = _tile(M, tm, SUB), _tile(K, tko), _tile(n, tn)
    nj = n // tn
    nr = G * nj

    def compute(dy_ref, w_ref, o_ref, *scr):
        part = lax.dot_general(dy_ref[...], w_ref[...], (((1,), (1,)), ((), ())), preferred_element_type=F32)
        if nr == 1:
            o_ref[...] = part.astype(o_ref.dtype)
            return
        (acc,) = scr
        r = pl.program_id(2)

        @pl.when(r == 0)
        def _():
            acc[...] = jnp.zeros_like(acc)

        acc[...] += part

        @pl.when(r == nr - 1)
        def _():
            o_ref[...] = acc[...].astype(o_ref.dtype)

    outs, passed = _call(
        name, (K // tko, M // tm, nr), compute,
        [pl.BlockSpec((tm, tn), lambda ko, i, r: (i, r)),
         pl.BlockSpec((None, tko, tn), lambda ko, i, r: (r // nj, ko, r % nj))],
        [dy, w3], [_S((M, K), out_dtype)], [pl.BlockSpec((tm, tko), lambda ko, i, r: (i, ko))],
        [] if nr == 1 else [pltpu.VMEM((tm, tko), F32)], tasks)
    return (outs[0], passed) if tasks else outs[0]


def _mm_tn(a, dy, G, *, out_dtype, name, tk=1024, tn=TN, tt=1024, tasks=()):
    M, K = a.shape
    n = dy.shape[1] // G
    tk, tn, tt = _tile(K, tk), _tile(n, tn), _tile(M, tt, SUB)
    nj, nt = n // tn, M // tt

    def compute(a_ref, dy_ref, o_ref, acc):
        t = pl.program_id(3)

        @pl.when(t == 0)
        def _():
            acc[...] = jnp.zeros_like(acc)

        acc[...] += lax.dot_general(a_ref[...], dy_ref[...], (((0,), (0,)), ((), ())),
                                    preferred_element_type=F32)

        @pl.when(t == nt - 1)
        def _():
            o_ref[...] = acc[...].astype(o_ref.dtype)

    outs, passed = _call(
        name, (G, nj, K // tk, nt), compute,
        [pl.BlockSpec((tt, tk), lambda g, j, k, t: (t, k)),
         pl.BlockSpec((tt, tn), lambda g, j, k, t: (t, g * nj + j))],
        [a, dy], [_S((G, K, n), out_dtype)], [pl.BlockSpec((None, tk, tn), lambda g, j, k, t: (g, k, j))],
        [pltpu.VMEM((tk, tn), F32)], tasks)
    return (outs[0], passed) if tasks else outs[0]


def _mm_small(kind, a, b, w3, *, name, tm=1024, tasks=()):
    G, K, n = w3.shape
    M = (a if a is not None else b).shape[0]
    tm = _tile(M, tm, HALO)
    nt = M // tm
    w_spec = pl.BlockSpec((G, K, n), lambda i: (0, 0, 0))
    a_spec = pl.BlockSpec((tm, K), lambda i: (i, 0))
    b_spec = pl.BlockSpec((tm, G * n), lambda i: (i, 0))
    cols = lambda g: slice(g * n, (g + 1) * n)
    if kind == "nn":
        def compute(a_ref, w_ref, o_ref):
            av = a_ref[...]
            for g in range(G):
                o_ref[:, cols(g)] = jnp.dot(av, w_ref[g], preferred_element_type=F32).astype(o_ref.dtype)

        outs, passed = _call(name, (nt,), compute, [a_spec, w_spec], [a, w3], [_S((M, G * n), BF16)], [b_spec], [], tasks)
    elif kind == "nt":
        def compute(b_ref, w_ref, o_ref):
            acc = None
            for g in range(G):
                part = lax.dot_general(b_ref[:, cols(g)], w_ref[g], (((1,), (1,)), ((), ())),
                                       preferred_element_type=F32)
                acc = part if acc is None else acc + part
            o_ref[...] = acc.astype(o_ref.dtype)

        outs, passed = _call(name, (nt,), compute, [b_spec, w_spec], [b, w3], [_S((M, K), BF16)], [a_spec], [], tasks)
    else:
        def compute(a_ref, b_ref, o_ref, acc):
            i = pl.program_id(0)

            @pl.when(i == 0)
            def _():
                acc[...] = jnp.zeros_like(acc)

            at = a_ref[...].T
            for g in range(G):
                acc[g] += jnp.dot(at, b_ref[:, cols(g)], preferred_element_type=F32)

            @pl.when(i == nt - 1)
            def _():
                o_ref[...] = acc[...].astype(o_ref.dtype)

        outs, passed = _call(name, (nt,), compute, [a_spec, b_spec], [a, b], [_S((G, K, n), BF16)], [w_spec],
                             [pltpu.VMEM((G, K, n), F32)], tasks)
    return (outs[0], passed) if tasks else outs[0]


def _cast_into_slot(w, place, name, paired=False):
    R, C = w.shape
    tr = _tile(R, 512, HALO)

    def body(s_ref, w_ref, o_ref):
        del s_ref
        o_ref[...] = w_ref[...].astype(BF16)

    if paired:
        shape, out_map = (N_CHIP, R, 2 * C), lambda i, s: (s[1], i, s[0])
    else:
        shape, out_map = (N_DEV, R, C), lambda i, s: (s[2], i, 0)
    return pl.pallas_call(
        body, name=name, out_shape=_S(shape, BF16),
        grid_spec=pltpu.PrefetchScalarGridSpec(
            num_scalar_prefetch=1, grid=(R // tr,),
            in_specs=[pl.BlockSpec((tr, C), lambda i, s: (i, 0))],
            out_specs=pl.BlockSpec((None, tr, C), out_map)),
        compiler_params=_cparams(("parallel",)),
    )(place, w)


def _down(cur, prev8, j):
    return pltpu.roll(jnp.concatenate([prev8, cur], axis=0), j, 0)[SUB:, :]


def _up(cur, next8, j):
    n = cur.shape[0] + SUB
    return pltpu.roll(jnp.concatenate([cur, next8], axis=0), n - j, 0)[:cur.shape[0], :]


def _shifted_down(x, prev8, n):
    full = jnp.concatenate([prev8, x], axis=0)
    return [x] + [pltpu.roll(full, s, 0)[SUB:, :] for s in range(1, n)]


def _shifted_up(x, next8, n):
    m = x.shape[0] + SUB
    full = jnp.concatenate([x, next8], axis=0)
    return [x] + [pltpu.roll(full, m - s, 0)[:x.shape[0], :] for s in range(1, n)]


def _taps(sh, w_ref):
    kw = w_ref.shape[0]
    y = sh[0] * w_ref[pl.ds(kw - 1, 1), :]
    for k in range(kw - 1):
        y = y + sh[kw - 1 - k] * w_ref[pl.ds(k, 1), :]
    return y


def _conv(x, prev8, w_ref):
    return _taps(_shifted_down(x, prev8, w_ref.shape[0]), w_ref)


def _conv_t(dy, next8, w_ref):
    return _taps(_shifted_up(dy, next8, w_ref.shape[0]), w_ref)


def _conv_dw(dw_ref, dy, x, prev8, first):
    kw = dw_ref.shape[0]

    @pl.when(first)
    def _():
        dw_ref[...] = jnp.zeros_like(dw_ref)

    for k in range(kw):
        xs = x if k == kw - 1 else _down(x, prev8, kw - 1 - k)
        dw_ref[pl.ds(k, 1), :] += jnp.sum(dy * xs, axis=0, keepdims=True)


def _acc(ref, val, first):
    @pl.when(first)
    def _():
        ref[...] = jnp.zeros_like(ref)

    ref[...] += val


def _acc_row(ref, val, first):
    _acc(ref, jnp.sum(val, axis=0, keepdims=True), first)


def _prev8(h_ref, t):
    return jnp.where(t > 0, h_ref[...].astype(F32)[HALO - SUB:, :], 0.0)


def _next8(h_ref, is_last):
    return jnp.where(is_last, 0.0, h_ref[...].astype(F32)[:SUB, :])


_GELU_K0 = math.sqrt(2.0 / math.pi)
_GELU_K1 = 0.044715


def _gelu_and_grad(x):
    x2 = x * x
    th = jnp.tanh(_GELU_K0 * x * (1.0 + _GELU_K1 * x2))
    g = 0.5 * x * (1.0 + th)
    dg = 0.5 * (1.0 + th) + 0.5 * x * (1.0 - th * th) * (_GELU_K0 * (1.0 + 3.0 * _GELU_K1 * x2))
    return g, dg


def _neg_expm1(z):
    series = -z * (1.0 + z * (0.5 + z * (1.0 / 6.0 + z * (1.0 / 24.0))))
    return jnp.where(z > -0.03, series, 1.0 - jnp.exp(z))


def _store_staged(stages, dst_hbm, sems, step, n_steps, where):
    def copies(s, slot):
        return [pltpu.make_async_copy(
            st.at[slot], dst_hbm.at[pl.ds(r0, st.shape[1]), pl.ds(c0, st.shape[2])], sems.at[slot, k])
            for k, (st, (r0, c0)) in enumerate(zip(stages, where(s)))]

    slot = step % 2

    @pl.when(step > 0)
    def _():
        for cp in copies(step - 1, 1 - slot):
            cp.wait()

    for cp in copies(step, slot):
        cp.start()

    @pl.when(step == n_steps - 1)
    def _():
        for cp in copies(step, slot):
            cp.wait()


def _halo_prev_map(hb, col_fn):
    return lambda c, t: (jnp.maximum(t * hb - 1, 0), col_fn(c))


def _rms_fwd(x, g, name):
    T, D = x.shape
    tb = _tile(T, TB, SUB)

    def body(x_ref, g_ref, o_ref):
        xv = x_ref[...]
        rstd = lax.rsqrt(jnp.mean(xv * xv, axis=-1, keepdims=True) + EPS)
        o_ref[...] = (xv * rstd * g_ref[...]).astype(BF16)

    return pl.pallas_call(
        body, name=name, out_shape=_S((T, D), BF16), grid=(T // tb,),
        in_specs=[pl.BlockSpec((tb, D), lambda i: (i, 0)), pl.BlockSpec((1, D), lambda i: (0, 0))],
        out_specs=pl.BlockSpec((tb, D), lambda i: (i, 0)),
        compiler_params=_cparams(("parallel",)),
    )(x, g.reshape(1, D))


def _rms_bwd(x, g, dh, dres, name, out_dtype):
    T, D = x.shape
    tb = _tile(T, 256, HALO)

    def body(x_ref, g_ref, dh_ref, dr_ref, dx_ref, dg_ref):
        i = pl.program_id(0)
        xv = x_ref[...]
        rstd = lax.rsqrt(jnp.mean(xv * xv, axis=-1, keepdims=True) + EPS)
        xn = xv * rstd
        dhv = dh_ref[...].astype(F32)
        _acc_row(dg_ref, dhv * xn, i == 0)
        dxn = dhv * g_ref[...]
        dx = dr_ref[...].astype(F32) + rstd * (dxn - xn * jnp.mean(dxn * xn, axis=-1, keepdims=True))
        dx_ref[...] = dx.astype(dx_ref.dtype)

    blk = pl.BlockSpec((tb, D), lambda i: (i, 0))
    vec = pl.BlockSpec((1, D), lambda i: (0, 0))
    return pl.pallas_call(
        body, name=name, out_shape=(_S((T, D), out_dtype), _S((1, D), F32)),
        grid=(T // tb,), in_specs=[blk, vec, blk, blk], out_specs=(blk, vec),
        compiler_params=_cparams(("arbitrary",)),
    )(x, g.reshape(1, D), dh, dres)


def _loss_head(x2, g, target, name):
    T, D = x2.shape
    tb = _tile(T, 256, HALO)

    def body(x_ref, g_ref, t_ref, dxb_ref, loss_ref, dg_ref):
        i = pl.program_id(0)
        xv = x_ref[...]
        rstd = lax.rsqrt(jnp.mean(xv * xv, axis=-1, keepdims=True) + EPS)
        xn = xv * rstd
        err = xn * g_ref[...] - t_ref[...]
        part = 0.5 * jnp.sum(jnp.mean(err * err, axis=-1, keepdims=True), axis=0, keepdims=True)
        part = jnp.broadcast_to(part, (1, LANES))
        _acc(loss_ref, part, i == 0)
        dy = err * (1.0 / D)
        _acc_row(dg_ref, dy * xn, i == 0)
        dxn = dy * g_ref[...]
        dxb_ref[...] = (rstd * (dxn - xn * jnp.mean(dxn * xn, axis=-1, keepdims=True))).astype(BF16)

    blk = pl.BlockSpec((tb, D), lambda i: (i, 0))
    vec = pl.BlockSpec((1, D), lambda i: (0, 0))
    return pl.pallas_call(
        body, name=name,
        out_shape=(_S((T, D), BF16), _S((1, LANES), F32), _S((1, D), F32)),
        grid=(T // tb,), in_specs=[blk, vec, blk],
        out_specs=(blk, pl.BlockSpec((1, LANES), lambda i: (0, 0)), vec),
        compiler_params=_cparams(("arbitrary",)),
    )(x2, g.reshape(1, D), target)


def _lru_gates(xc, wa_ref, ba_ref, wx_ref, bx_ref, lam_ref):
    xcb = xc.astype(BF16)
    r = jax.nn.sigmoid(jnp.dot(xcb, wa_ref[...], preferred_element_type=F32) + ba_ref[...])
    i = jax.nn.sigmoid(jnp.dot(xcb, wx_ref[...], preferred_element_type=F32) + bx_ref[...])
    sp = jax.nn.softplus(-lam_ref[...])
    log_a = (-LRU_C * sp) * r
    a = jnp.exp(log_a)
    s = jnp.sqrt(_neg_expm1(2.0 * log_a))
    return xcb, r, i, a, s


def _lru_fwd(p, conv_w, conv_b, wa_bd, ba, wx_bd, bx, lam, *, name, tasks=()):
    T = p.shape[0]
    d = lam.shape[-1]
    C = _tile(d, C_LRU)
    nC = d // C
    tb = _tile(T, TB, HALO)
    nT, hb, nt = T // tb, tb // HALO, tb // SUB

    def body(x_ref, xh_ref, g_ref, cw_ref, cb_ref, wa_ref, ba_ref, wx_ref, bx_ref, lam_ref,
             hs_ref, y_ref, a_s, u_s, h_s):
        t = pl.program_id(1)

        @pl.when(t == 0)
        def _():
            h_s[...] = jnp.zeros_like(h_s)

        x = x_ref[...].astype(F32)
        xc = _conv(x, _prev8(xh_ref, t), cw_ref) + cb_ref[...]
        _, r, i, a, s = _lru_gates(xc, wa_ref, ba_ref, wx_ref, bx_ref, lam_ref)
        a_s[...] = a
        u_s[...] = s * (i * xc)
        row = lax.broadcasted_iota(jnp.int32, (SUB, C), 0)

        def step(k, h):
            o = pl.multiple_of(k * SUB, SUB)
            A = a_s[pl.ds(o, SUB), :]
            B = u_s[pl.ds(o, SUB), :]
            for sh in (1, 2, 4):
                m = row >= sh
                Ap = pltpu.roll(A, sh, 0)
                Bp = pltpu.roll(B, sh, 0)
                B = jnp.where(m, A * Bp + B, B)
                A = jnp.where(m, A * Ap, A)
            hs = A * h + B
            hs_ref[pl.ds(o, SUB), :] = hs
            return jnp.broadcast_to(hs[SUB - 1:SUB, :], (SUB, C))

        h_s[...] = lax.fori_loop(0, nt, step, h_s[...])
        gel, _ = _gelu_and_grad(g_ref[...].astype(F32))
        y_ref[...] = (gel * hs_ref[...]).astype(BF16)

    vec = pl.BlockSpec((1, C), lambda c, t: (0, c))
    sq = pl.BlockSpec((None, C, C), lambda c, t: (c, 0, 0))
    outs, passed = _call(
        name, (nC, nT), body,
        [pl.BlockSpec((tb, C), lambda c, t: (t, c)),
         pl.BlockSpec((HALO, C), _halo_prev_map(hb, lambda c: c)),
         pl.BlockSpec((tb, C), lambda c, t: (t, nC + c)),
         pl.BlockSpec((conv_w.shape[0], C), lambda c, t: (0, c)),
         vec, sq, vec, sq, vec, vec],
        [p, p, p, conv_w, conv_b, wa_bd, ba, wx_bd, bx, lam],
        [_S((T, d), F32), _S((T, d), BF16)],
        [pl.BlockSpec((tb, C), lambda c, t: (t, c)), pl.BlockSpec((tb, C), lambda c, t: (t, c))],
        [pltpu.VMEM((tb, C), F32), pltpu.VMEM((tb, C), F32), pltpu.VMEM((SUB, C), F32)], tasks)
    return (*outs, passed) if tasks else outs


def _lru_bwd(p, hs, dyl, dp, conv_w, conv_b, wa_bd, ba, wx_bd, bx, lam, *, name, tasks=()):
    T = p.shape[0]
    d = lam.shape[-1]
    C = _tile(d, C_LRU)
    nC = d // C
    tb = _tile(T, TB, HALO)
    nT, hb, nt = T // tb, tb // HALO, tb // SUB
    kw = conv_w.shape[0]

    def body(x_ref, xh_ref, g_ref, hs_ref, hh_ref, dy_ref, cw_ref, cb_ref, wa_ref, ba_ref, wx_ref, bx_ref,
             lam_ref, dp_in, dp_ref, dcw_ref, dcb_ref, dwa_ref, dba_ref, dwx_ref, dbx_ref, dlam_ref,
             b_s, g_s, dh_s, an_s, dhn_s, dxn_s, st_x, st_g, sems):
        del dp_in
        c = pl.program_id(0)
        tr = pl.program_id(1)
        t = nT - 1 - tr
        first = tr == 0

        @pl.when(first)
        def _():
            an_s[...] = jnp.zeros_like(an_s)
            dhn_s[...] = jnp.zeros_like(dhn_s)
            dxn_s[...] = jnp.zeros_like(dxn_s)

        x = x_ref[...].astype(F32)
        xprev = _prev8(xh_ref, t)
        xc = _conv(x, xprev, cw_ref) + cb_ref[...]
        xcb, r, i, a, s = _lru_gates(xc, wa_ref, ba_ref, wx_ref, bx_ref, lam_ref)
        hsv = hs_ref[...]
        dy = dy_ref[...].astype(F32)
        gel, dgel = _gelu_and_grad(g_ref[...].astype(F32))
        step_no = c * nT + tr
        slot = step_no % 2
        st_g[slot] = (dy * hsv * dgel).astype(BF16)

        b_s[...] = _up(a, an_s[...], 1)
        g_s[...] = dy * gel
        row = lax.broadcasted_iota(jnp.int32, (SUB, C), 0)

        def step(k, carry):
            o = pl.multiple_of((nt - 1 - k) * SUB, SUB)
            B = b_s[pl.ds(o, SUB), :]
            G = g_s[pl.ds(o, SUB), :]
            for sh in (1, 2, 4):
                m = row < SUB - sh
                Bn = pltpu.roll(B, SUB - sh, 0)
                Gn = pltpu.roll(G, SUB - sh, 0)
                G = jnp.where(m, B * Gn + G, G)
                B = jnp.where(m, B * Bn, B)
            dh = B * carry + G
            dh_s[pl.ds(o, SUB), :] = dh
            return jnp.broadcast_to(dh[0:1, :], (SUB, C))

        dhn_s[...] = lax.fori_loop(0, nt, step, dhn_s[...])
        an_s[...] = a[:SUB, :]
        dh = dh_s[...]

        hprev = _down(hsv, jnp.where(t > 0, hh_ref[...][HALO - SUB:, :], 0.0), 1)
        d_a = dh * hprev
        ixc = i * xc
        d_s = dh * ixc
        d_i = dh * s * xc
        d_xc = dh * s * i
        d_l = d_a * a - d_s * (a * a) / s
        sp = jax.nn.softplus(-lam_ref[...])
        _acc_row(dlam_ref, d_l * r * (LRU_C * jax.nn.sigmoid(-lam_ref[...])), first)
        d_zr = (d_l * (-LRU_C * sp)) * r * (1.0 - r)
        d_zi = d_i * i * (1.0 - i)
        _acc_row(dba_ref, d_zr, first)
        _acc_row(dbx_ref, d_zi, first)
        d_zrb = d_zr.astype(BF16)
        d_zib = d_zi.astype(BF16)
        tn_dims = (((0,), (0,)), ((), ()))
        nt_dims = (((1,), (1,)), ((), ()))
        gwa = lax.dot_general(xcb, d_zrb, tn_dims, preferred_element_type=F32)
        gwx = lax.dot_general(xcb, d_zib, tn_dims, preferred_element_type=F32)
        _acc(dwa_ref, gwa, first)
        _acc(dwx_ref, gwx, first)
        d_xc = (d_xc + lax.dot_general(d_zrb, wa_ref[...], nt_dims, preferred_element_type=F32)
                + lax.dot_general(d_zib, wx_ref[...], nt_dims, preferred_element_type=F32))
        _acc_row(dcb_ref, d_xc, first)
        _conv_dw(dcw_ref, d_xc, x, xprev, first)
        st_x[slot] = _conv_t(d_xc, dxn_s[...], cw_ref).astype(BF16)
        dxn_s[...] = d_xc[:SUB, :]

        def where(s):
            row0, col0 = (nT - 1 - s % nT) * tb, (s // nT) * C
            return [(row0, col0), (row0, d + col0)]

        _store_staged([st_x, st_g], dp_ref, sems, step_no, nC * nT, where)

    rev = lambda c, tr: (nT - 1 - tr, c)
    vec = pl.BlockSpec((1, C), lambda c, tr: (0, c))
    sq = pl.BlockSpec((None, C, C), lambda c, tr: (c, 0, 0))
    cwb = pl.BlockSpec((kw, C), lambda c, tr: (0, c))
    halo_prev = lambda c, tr: (jnp.maximum((nT - 1 - tr) * hb - 1, 0), c)
    outs, passed = _call(
        name, (nC, nT), body,
        [pl.BlockSpec((tb, C), rev),
         pl.BlockSpec((HALO, C), halo_prev),
         pl.BlockSpec((tb, C), lambda c, tr: (nT - 1 - tr, nC + c)),
         pl.BlockSpec((tb, C), rev),
         pl.BlockSpec((HALO, C), halo_prev),
         pl.BlockSpec((tb, C), rev),
         cwb, vec, sq, vec, sq, vec, vec, ANY],
        [p, p, p, hs, hs, dyl, conv_w, conv_b, wa_bd, ba, wx_bd, bx, lam, dp],
        [_S(dp.shape, dp.dtype), _S((kw, d), F32), _S((1, d), F32), _S((nC, C, C), F32), _S((1, d), F32),
         _S((nC, C, C), F32), _S((1, d), F32), _S((1, d), F32)],
        [ANY, cwb, vec, sq, vec, sq, vec, vec],
        [pltpu.VMEM((tb, C), F32), pltpu.VMEM((tb, C), F32), pltpu.VMEM((tb, C), F32),
         pltpu.VMEM((SUB, C), F32), pltpu.VMEM((SUB, C), F32), pltpu.VMEM((SUB, C), F32),
         pltpu.VMEM((2, tb, C), BF16), pltpu.VMEM((2, tb, C), BF16), pltpu.SemaphoreType.DMA((2, 2))],
        tasks, own_aliases={13: 0})
    return (*outs, passed) if tasks else outs


def _sc_fwd(p, conv_w, *, d, name):
    T = p.shape[0]
    C = _tile(d, C_EW)
    nC = d // C
    tb = _tile(T, TB, HALO)
    nT, hb = T // tb, tb // HALO

    def body(b_ref, c_ref, ch_ref, v_ref, vh_ref, w_ref, y_ref):
        t = pl.program_id(1)
        cv = c_ref[...].astype(F32) * v_ref[...].astype(F32)
        cvp = _prev8(ch_ref, t) * _prev8(vh_ref, t)
        y_ref[...] = (b_ref[...].astype(F32) * _conv(cv, cvp, w_ref)).astype(BF16)

    seg = lambda k: pl.BlockSpec((tb, C), lambda c, t: (t, k * nC + c))
    hseg = lambda k: pl.BlockSpec((HALO, C), _halo_prev_map(hb, lambda c: k * nC + c))
    return pl.pallas_call(
        body, name=name, out_shape=_S((T, d), BF16), grid=(nC, nT),
        in_specs=[seg(2), seg(3), hseg(3), seg(4), hseg(4), pl.BlockSpec((conv_w.shape[0], C), lambda c, t: (0, c))],
        out_specs=pl.BlockSpec((tb, C), lambda c, t: (t, c)),
        compiler_params=_cparams(("parallel", "parallel")),
    )(p, p, p, p, p, conv_w)


def _sc_bwd(p, dys, dp, conv_w, *, d, name, tasks=()):
    T = p.shape[0]
    C = _tile(d, C_EW)
    nC = d // C
    tb = _tile(T, TB, HALO)
    nT, hb = T // tb, tb // HALO
    kw = conv_w.shape[0]

    def body(b_ref, bn_ref, c_ref, ch_ref, v_ref, vh_ref, dy_ref, dyn_ref, w_ref, dp_in, dp_ref, dw_ref,
             st_b, st_c, st_v, sems):
        del dp_in
        c = pl.program_id(0)
        t = pl.program_id(1)
        last = t == nT - 1
        bv = b_ref[...].astype(F32)
        cvv = c_ref[...].astype(F32)
        vv = v_ref[...].astype(F32)
        dy = dy_ref[...].astype(F32)
        cv = cvv * vv
        cvp = _prev8(ch_ref, t) * _prev8(vh_ref, t)
        step_no = c * nT + t
        slot = step_no % 2
        st_b[slot] = (dy * _conv(cv, cvp, w_ref)).astype(BF16)
        dz = dy * bv
        dzn = _next8(dyn_ref, last) * _next8(bn_ref, last)
        _conv_dw(dw_ref, dz, cv, cvp, t == 0)
        dcv = _conv_t(dz, dzn, w_ref)
        st_c[slot] = (dcv * vv).astype(BF16)
        st_v[slot] = (dcv * cvv).astype(BF16)

        def where(s):
            return [((s % nT) * tb, (2 + k) * d + (s // nT) * C) for k in range(3)]

        _store_staged([st_b, st_c, st_v], dp_ref, sems, step_no, nC * nT, where)

    seg = lambda k: pl.BlockSpec((tb, C), lambda c, t: (t, k * nC + c))
    hseg = lambda k: pl.BlockSpec((HALO, C), _halo_prev_map(hb, lambda c: k * nC + c))
    last_h = T // HALO - 1
    nseg = lambda k: pl.BlockSpec((HALO, C), lambda c, t: (jnp.minimum((t + 1) * hb, last_h), k * nC + c))
    outs, passed = _call(
        name, (nC, nT), body,
        [seg(2), nseg(2), seg(3), hseg(3), seg(4), hseg(4),
         pl.BlockSpec((tb, C), lambda c, t: (t, c)), nseg(0),
         pl.BlockSpec((kw, C), lambda c, t: (0, c)), ANY],
        [p, p, p, p, p, p, dys, dys, conv_w, dp],
        [_S(dp.shape, dp.dtype), _S((kw, d), F32)], [ANY, pl.BlockSpec((kw, C), lambda c, t: (0, c))],
        [pltpu.VMEM((2, tb, C), BF16)] * 3 + [pltpu.SemaphoreType.DMA((2, 3))], tasks, own_aliases={9: 0})
    return (*outs, passed) if tasks else outs


def _merge_fwd(p, y_lru, y_sc, *, col0, name, tasks=()):
    T, D = y_lru.shape
    C = _tile(math.gcd(D, col0), 1024)
    nC = D // C
    k0 = col0 // C
    tb = _tile(T, TB, HALO)

    def body(gl_ref, gs_ref, yl_ref, ys_ref, o_ref):
        @pl.loop(0, tb // HALO)
        def _(k):
            rows = pl.ds(pl.multiple_of(k * HALO, HALO), HALO)
            for l0 in range(0, C, min(C, C_EW)):
                at = (rows, pl.ds(l0, min(C, C_EW)))
                o_ref[at] = (jax.nn.sigmoid(gl_ref[at].astype(F32)) * yl_ref[at].astype(F32)
                             + jax.nn.sigmoid(gs_ref[at].astype(F32)) * ys_ref[at].astype(F32)).astype(BF16)

    blk = pl.BlockSpec((tb, C), lambda c, t: (t, c))
    outs, passed = _call(
        name, (nC, T // tb), body,
        [pl.BlockSpec((tb, C), lambda c, t: (t, k0 + c)), pl.BlockSpec((tb, C), lambda c, t: (t, k0 + nC + c)),
         blk, blk], [p, p, y_lru, y_sc], [_S((T, D), BF16)], [blk], [], tasks)
    return (outs[0], passed) if tasks else outs[0]


def _merge_bwd(p, y_lru, y_sc, dm, *, col0, name, tasks=()):
    T, D = y_lru.shape
    C = _tile(math.gcd(D, col0), 1024)
    nC = D // C
    k0 = col0 // C
    tb = _tile(T, TB, HALO)
    nT = T // tb

    def body(gl_ref, gs_ref, yl_ref, ys_ref, dm_ref, dp_ref, dyl_ref, dys_ref, st_l, st_s, sems):
        step_no = pl.program_id(0) * nT + pl.program_id(1)
        slot = step_no % 2

        @pl.loop(0, tb // HALO)
        def _(k):
            rows = pl.ds(pl.multiple_of(k * HALO, HALO), HALO)
            for l0 in range(0, C, min(C, C_EW)):
                at = (rows, pl.ds(l0, min(C, C_EW)))
                dmv = dm_ref[at].astype(F32)
                sl = jax.nn.sigmoid(gl_ref[at].astype(F32))
                ss = jax.nn.sigmoid(gs_ref[at].astype(F32))
                dyl_ref[at] = (dmv * sl).astype(BF16)
                dys_ref[at] = (dmv * ss).astype(BF16)
                st_l[(slot,) + at] = (dmv * yl_ref[at].astype(F32) * sl * (1.0 - sl)).astype(BF16)
                st_s[(slot,) + at] = (dmv * ys_ref[at].astype(F32) * ss * (1.0 - ss)).astype(BF16)

        def where(s):
            row0, colc = (s % nT) * tb, (s // nT) * C
            return [(row0, col0 + colc), (row0, col0 + D + colc)]

        _store_staged([st_l, st_s], dp_ref, sems, step_no, nC * nT, where)

    blk = pl.BlockSpec((tb, C), lambda c, t: (t, c))
    outs, passed = _call(
        name, (nC, nT), body,
        [pl.BlockSpec((tb, C), lambda c, t: (t, k0 + c)), pl.BlockSpec((tb, C), lambda c, t: (t, k0 + nC + c)),
         blk, blk, blk], [p, p, y_lru, y_sc, dm],
        [_S(p.shape, BF16), _S((T, D), BF16), _S((T, D), BF16)], [ANY, blk, blk],
        [pltpu.VMEM((2, tb, C), BF16), pltpu.VMEM((2, tb, C), BF16), pltpu.SemaphoreType.DMA((2, 2))], tasks)
    return (*outs, passed) if tasks else outs


def _ffn_act_fwd(uu, conv_w, *, name, tasks=()):
    T = uu.shape[0]
    F = uu.shape[1] // 2
    C = _tile(F, C_EW)
    nC = F // C
    tb = _tile(T, TB_CHUNKED, HALO)
    nT, hb = T // tb, tb // HALO
    kw = conv_w.shape[0]
    R = HALO

    def body(g_ref, gh_ref, v_ref, vh_ref, wg_ref, wv_ref, o_ref):
        t = pl.program_id(1)

        def chunk(k, carry):
            gp, vp = carry
            r0 = pl.multiple_of(k * R, R)
            ug = g_ref[pl.ds(r0, R), :].astype(F32)
            uv = v_ref[pl.ds(r0, R), :].astype(F32)
            cg = _conv(ug, gp, wg_ref)
            cv = _conv(uv, vp, wv_ref)
            o_ref[pl.ds(r0, R), :] = (cg * jax.nn.sigmoid(cg) * cv).astype(BF16)
            return ug[R - SUB:, :], uv[R - SUB:, :]

        lax.fori_loop(0, tb // R, chunk, (_prev8(gh_ref, t), _prev8(vh_ref, t)))

    seg = lambda k: pl.BlockSpec((tb, C), lambda c, t: (t, k * nC + c))
    hseg = lambda k: pl.BlockSpec((HALO, C), _halo_prev_map(hb, lambda c: k * nC + c))
    wseg = lambda k: pl.BlockSpec((kw, C), lambda c, t: (0, k * nC + c))
    outs, passed = _call(
        name, (nC, nT), body, [seg(0), hseg(0), seg(1), hseg(1), wseg(0), wseg(1)],
        [uu, uu, uu, uu, conv_w, conv_w], [_S((T, F), BF16)], [pl.BlockSpec((tb, C), lambda c, t: (t, c))], [], tasks)
    return (outs[0], passed) if tasks else outs[0]


def _ffn_act_bwd(uu, dact, conv_w, *, name):
    T = uu.shape[0]
    F = uu.shape[1] // 2
    C = _tile(F, C_EW)
    nC = F // C
    tb = _tile(T, TB_CHUNKED, HALO)
    nT, hb = T // tb, tb // HALO
    kw = conv_w.shape[0]
    R = HALO
    nk = tb // R

    def body(g_ref, gh_ref, v_ref, vh_ref, da_ref, wg_ref, wv_ref, du_ref, dwg_ref, dwv_ref,
             gn_s, vn_s, accg_s, accv_s, st_g, st_v, sems):
        c = pl.program_id(0)
        tr = pl.program_id(1)
        t = nT - 1 - tr
        first = tr == 0

        @pl.when(first)
        def _():
            gn_s[...] = jnp.zeros_like(gn_s)
            vn_s[...] = jnp.zeros_like(vn_s)
            dwg_ref[...] = jnp.zeros_like(dwg_ref)
            dwv_ref[...] = jnp.zeros_like(dwv_ref)

        accg_s[...] = jnp.zeros_like(accg_s)
        accv_s[...] = jnp.zeros_like(accv_s)
        step_no = c * nT + tr
        slot = step_no % 2

        def chunk(i, carry):
            gn, vn = carry
            k = nk - 1 - i
            r0 = pl.multiple_of(k * R, R)
            rp = pl.multiple_of(jnp.maximum(r0 - R, 0), R)
            ug = g_ref[pl.ds(r0, R), :].astype(F32)
            uv = v_ref[pl.ds(r0, R), :].astype(F32)
            gp = jnp.where(k > 0, g_ref[pl.ds(rp, R), :].astype(F32)[R - SUB:, :], _prev8(gh_ref, t))
            vp = jnp.where(k > 0, v_ref[pl.ds(rp, R), :].astype(F32)[R - SUB:, :], _prev8(vh_ref, t))
            sh_g = _shifted_down(ug, gp, kw)
            sh_v = _shifted_down(uv, vp, kw)
            cg = _taps(sh_g, wg_ref)
            cv = _taps(sh_v, wv_ref)
            da = da_ref[pl.ds(r0, R), :].astype(F32)
            sg = jax.nn.sigmoid(cg)
            d_cg = da * cv * (sg * (1.0 + cg * (1.0 - sg)))
            d_cv = da * (cg * sg)
            for j in range(kw):
                accg_s[j] += d_cg * sh_g[kw - 1 - j]
                accv_s[j] += d_cv * sh_v[kw - 1 - j]
            st_g[slot, pl.ds(r0, R), :] = _conv_t(d_cg, gn, wg_ref).astype(BF16)
            st_v[slot, pl.ds(r0, R), :] = _conv_t(d_cv, vn, wv_ref).astype(BF16)
            return d_cg[:SUB, :], d_cv[:SUB, :]

        gn, vn = lax.fori_loop(0, nk, chunk, (gn_s[...], vn_s[...]))
        gn_s[...] = gn
        vn_s[...] = vn
        for j in range(kw):
            dwg_ref[pl.ds(j, 1), :] += jnp.sum(accg_s[j], axis=0, keepdims=True)
            dwv_ref[pl.ds(j, 1), :] += jnp.sum(accv_s[j], axis=0, keepdims=True)
        def where(s):
            row0, col0 = (nT - 1 - s % nT) * tb, (s // nT) * C
            return [(row0, col0), (row0, F + col0)]

        _store_staged([st_g, st_v], du_ref, sems, step_no, nC * nT, where)

    seg = lambda k: pl.BlockSpec((tb, C), lambda c, tr: (nT - 1 - tr, k * nC + c))
    hseg = lambda k: pl.BlockSpec((HALO, C), lambda c, tr: (jnp.maximum((nT - 1 - tr) * hb - 1, 0), k * nC + c))
    wseg = lambda k: pl.BlockSpec((kw, C), lambda c, tr: (0, k * nC + c))
    dwb = pl.BlockSpec((kw, C), lambda c, tr: (0, c))
    return pl.pallas_call(
        body, name=name, out_shape=(_S(uu.shape, BF16), _S((kw, F), F32), _S((kw, F), F32)), grid=(nC, nT),
        in_specs=[seg(0), hseg(0), seg(1), hseg(1), pl.BlockSpec((tb, C), lambda c, tr: (nT - 1 - tr, c)),
                  wseg(0), wseg(1)],
        out_specs=(ANY, dwb, dwb),
        scratch_shapes=[pltpu.VMEM((SUB, C), F32), pltpu.VMEM((SUB, C), F32),
                        pltpu.VMEM((kw, R, C), F32), pltpu.VMEM((kw, R, C), F32),
                        pltpu.VMEM((2, tb, C), BF16), pltpu.VMEM((2, tb, C), BF16), pltpu.SemaphoreType.DMA((2, 2))],
        compiler_params=_cparams(("arbitrary", "arbitrary")),
    )(uu, uu, uu, uu, dact, conv_w, conv_w)


def _place():
    x, y, c = lax.axis_index("x"), lax.axis_index("y"), lax.axis_index("c")
    return x, y, c


def _chips(x, y):
    return [(1 - x, y), (x, 1 - y), (1 - x, 1 - y)]


def _all_gather(arrays, placed, over_ici, pair_n, name):
    n = len(arrays)

    def body(*refs):
        ins, outs = refs[:n], refs[n:2 * n]
        send_sems, recv_sems, local_sems = refs[2 * n:]
        x, y, c = _place()
        me, sibling = (x, y, c), (x, y, 1 - c)
        chips = _chips(x, y)
        full = [a for a in range(n) if over_ici[a]]

        def idx(px, py, pc):
            return 4 * px + 2 * py + pc

        def copy(a, k, block, to):
            dst = _dev_block(outs[a], idx(*block), pair_n[a])
            src = ins[a] if (block is me and not placed[a]) else dst
            return pltpu.make_async_remote_copy(
                src_ref=src, dst_ref=dst, send_sem=send_sems.at[a, k], recv_sem=recv_sems.at[a, k],
                device_id=to, device_id_type=MESH)

        def half(a, k, block, to, lo):
            r = rows_of[a] // 2
            blk = _rows_of(outs[a], idx(*block), (0 if lo else r, r), pair_n[a])
            return pltpu.make_async_remote_copy(
                src_ref=blk, dst_ref=blk, send_sem=send_sems.at[a, k], recv_sem=recv_sems.at[a, k],
                device_id=to, device_id_type=MESH)

        mine = [pltpu.make_async_copy(ins[a], outs[a].at[idx(*me)], local_sems.at[a])
                for a in range(n) if not placed[a]]
        for cp in mine:
            cp.start()
        chip_x, chip_y, chip_d = chips
        sent = []
        for a in full:
            sent += [copy(a, 1, me, (*chip_x, c)), copy(a, 2, me, (*chip_y, c))]
            if not relay[a]:
                sent.append(copy(a, 3, me, (*chip_d, c)))
        for a in range(n):
            sent.append(copy(a, 0, me, sibling))
        for cp in sent:
            cp.start()

        def then(cp):
            cp.start()
            sent.append(cp)

        for a in full:
            copy(a, 2, (*chip_y, c), me).wait_recv()
            if relay[a]:
                then(half(a, 3, (*chip_y, c), (*chip_x, c), True))
            then(copy(a, 6, (*chip_y, c), sibling))
            copy(a, 1, (*chip_x, c), me).wait_recv()
            if relay[a]:
                then(half(a, 4, (*chip_x, c), (*chip_y, c), False))
            then(copy(a, 5, (*chip_x, c), sibling))
        for a in full:
            if relay[a]:
                half(a, 3, (*chip_d, c), me, True).wait_recv()
                half(a, 4, (*chip_d, c), me, False).wait_recv()
            else:
                copy(a, 3, (*chip_d, c), me).wait_recv()
            then(copy(a, 7, (*chip_d, c), sibling))
        for a in range(n):
            copy(a, 0, sibling, me).wait_recv()
        for a in full:
            for j, chip in enumerate(chips):
                copy(a, 5 + j, (*chip, 1 - c), me).wait_recv()
        for cp in sent:
            cp.wait_send()
        for cp in mine:
            cp.wait()

    rows_of = [(s.shape[1] if placed[a] else s.shape[0]) for a, s in enumerate(arrays)]
    relay = [r % (2 * HALO) == 0 for r in rows_of]
    return pl.pallas_call(
        body, name=name,
        out_shape=tuple(_S(s.shape if placed[a] else (N_DEV,) + s.shape, s.dtype) for a, s in enumerate(arrays)),
        in_specs=[ANY] * n, out_specs=tuple([ANY] * n),
        scratch_shapes=[pltpu.SemaphoreType.DMA((n, 8)), pltpu.SemaphoreType.DMA((n, 8)),
                        pltpu.SemaphoreType.DMA((n,))],
        input_output_aliases={a: a for a in range(n) if placed[a]},
    )(*arrays)


def _dev_block(ref, dev, pair_n=None):
    if pair_n is None:
        return ref.at[dev]
    return ref.at[dev // 2, :, pl.ds(pl.multiple_of((dev % 2) * pair_n, LANES), pair_n)]


def _rows_of(ref, blk, rows, pair_n=None):
    v = _dev_block(ref, blk, pair_n)
    return v if rows is None else v.at[pl.ds(rows[0], rows[1])]


ALL_ROWS = "all"


def _gather_task(buf, ici=None, fwd=None, pair_n=None):
    blk_of = functools.partial(_rows_of, pair_n=pair_n)
    r0, nr = (0, buf.shape[1]) if ici == ALL_ROWS else (ici or (0, 0))
    assert nr % (2 * HALO) == 0
    lo, hi = (r0, nr // 2), (r0 + nr // 2, nr // 2)
    both = (r0, nr)
    fwd_rows = None if fwd == ALL_ROWS else fwd

    def remote(refs, ss, rs, k, dev, rows, to):
        blk = blk_of(refs[0], dev, rows)
        return pltpu.make_async_remote_copy(src_ref=blk, dst_ref=blk, send_sem=ss.at[k], recv_sem=rs.at[k],
                                            device_id=to, device_id_type=MESH)

    def waves(refs, ss, rs):
        x, y, c = _place()
        me = 4 * x + 2 * y + c
        (xx, xy), (yx, yy), _ = _chips(x, y)
        dev_x, dev_y = 4 * xx + 2 * xy + c, 4 * yx + 2 * yy + c
        first, second = [], []
        if ici is not None:
            first += [remote(refs, ss, rs, 0, me, both, (xx, xy, c)), remote(refs, ss, rs, 1, me, both, (yx, yy, c))]
            second += [remote(refs, ss, rs, 2, dev_y, lo, (xx, xy, c)), remote(refs, ss, rs, 3, dev_x, hi, (yx, yy, c))]
        if fwd is not None:
            first += [remote(refs, ss, rs, 4 + j, 4 * px + 2 * py + c, fwd_rows, (x, y, 1 - c))
                      for j, (px, py) in enumerate(_chips(x, y))]
        return first, second

    def start(refs, ss, rs, ls):
        for cp in waves(refs, ss, rs)[0]:
            cp.start()

    def mid(refs, ss, rs, ls):
        x, y, c = _place()
        (xx, xy), (yx, yy), _ = _chips(x, y)
        remote(refs, ss, rs, 0, 4 * xx + 2 * xy + c, both, (x, y, c)).wait_recv()
        remote(refs, ss, rs, 1, 4 * yx + 2 * yy + c, both, (x, y, c)).wait_recv()
        for cp in waves(refs, ss, rs)[1]:
            cp.start()

    def wait(refs, ss, rs, ls):
        x, y, c = _place()
        chips = _chips(x, y)
        if ici is not None:
            dev_d = 4 * chips[2][0] + 2 * chips[2][1] + c
            remote(refs, ss, rs, 2, dev_d, lo, (x, y, c)).wait_recv()
            remote(refs, ss, rs, 3, dev_d, hi, (x, y, c)).wait_recv()
        if fwd is not None:
            for j, (px, py) in enumerate(chips):
                remote(refs, ss, rs, 4 + j, 4 * px + 2 * py + 1 - c, fwd_rows, (x, y, c)).wait_recv()
        first, second = waves(refs, ss, rs)
        for cp in first + second:
            cp.wait_send()

    return _Task([buf], [0], start, wait, nsem=7, mid=mid if ici is not None else None)


def _exchange_task(parts, landing, rows=None):
    def copies(refs, ss, rs):
        x, y, c = _place()
        myq = 2 * x + y
        return [pltpu.make_async_remote_copy(
            src_ref=_rows_of(refs[0], 2 * px + py, rows), dst_ref=_rows_of(refs[1], myq, rows),
            send_sem=ss.at[k], recv_sem=rs.at[k], device_id=(px, py, c), device_id_type=MESH)
            for k, (px, py) in enumerate(_chips(x, y))]

    def start(refs, ss, rs, ls):
        for cp in copies(refs, ss, rs):
            cp.start()

    def wait(refs, ss, rs, ls):
        x, y, c = _place()
        for k, (px, py) in enumerate(_chips(x, y)):
            pltpu.make_async_remote_copy(
                src_ref=_rows_of(refs[0], 2 * x + y, rows), dst_ref=_rows_of(refs[1], 2 * px + py, rows),
                send_sem=ss.at[k], recv_sem=rs.at[k], device_id=(px, py, c), device_id_type=MESH).wait_recv()
        for cp in copies(refs, ss, rs):
            cp.wait_send()

    return _Task([parts, landing], [1], start, wait)


def _core_blocks(g, pair_n):
    if pair_n is None:
        g4 = g.reshape((N_CHIP, 2) + g.shape[1:])
        return g4, (N_CHIP,) + g.shape[1:], lambda ref, c: ref.at[:, c]
    view = lambda ref, c: ref.at[:, :, pl.ds(pl.multiple_of(c * pair_n, LANES), pair_n)]
    return g, (N_CHIP, g.shape[1], pair_n), view


def _swap_task(g, pair_n=None):
    g4, shape, view = _core_blocks(g, pair_n)

    def copy(refs, ss, rs):
        x, y, c = _place()
        return pltpu.make_async_remote_copy(
            src_ref=view(refs[0], 1 - c), dst_ref=refs[1], send_sem=ss.at[0], recv_sem=rs.at[0],
            device_id=(x, y, 1 - c), device_id_type=MESH)

    def start(refs, ss, rs, ls):
        copy(refs, ss, rs).start()

    def wait(refs, ss, rs, ls):
        copy(refs, ss, rs).wait()

    return _Task([g4], [], start, wait, fresh=[_S(shape, g.dtype)], nsem=1)


def _peer(x, y, c, m):
    return x ^ (m >> 2), y ^ ((m >> 1) & 1), c ^ (m & 1)


def _bcast_task(pack):
    def copies(refs, ss, rs):
        x, y, c = _place()
        me = 4 * x + 2 * y + c
        return [pltpu.make_async_remote_copy(
            src_ref=refs[0], dst_ref=refs[1].at[me], send_sem=ss.at[m - 1], recv_sem=rs.at[m - 1],
            device_id=_peer(x, y, c, m), device_id_type=MESH) for m in range(1, N_DEV)]

    def local(refs, ls):
        x, y, c = _place()
        return pltpu.make_async_copy(refs[0], refs[1].at[4 * x + 2 * y + c], ls.at[0])

    def start(refs, ss, rs, ls):
        local(refs, ls).start()
        for cp in copies(refs, ss, rs):
            cp.start()

    def wait(refs, ss, rs, ls):
        x, y, c = _place()
        for m in range(1, N_DEV):
            px, py, pc = _peer(x, y, c, m)
            pltpu.make_async_remote_copy(
                src_ref=refs[0], dst_ref=refs[1].at[4 * px + 2 * py + pc], send_sem=ss.at[m - 1],
                recv_sem=rs.at[m - 1], device_id=(px, py, pc), device_id_type=MESH).wait_recv()
        for cp in copies(refs, ss, rs):
            cp.wait_send()
        local(refs, ls).wait()

    return _Task([pack], [], start, wait, fresh=[_S((N_DEV,) + pack.shape, pack.dtype)], nsem=N_DEV - 1)


def _sum_packs(packs, name):
    _, R, L = packs.shape

    def body(p_ref, o_ref):
        acc = p_ref[0]
        for k in range(1, N_DEV):
            acc = acc + p_ref[k]
        o_ref[...] = acc

    return pl.pallas_call(body, name=name, out_shape=_S((R, L), packs.dtype), in_specs=[VMEM_SPEC],
                          out_specs=VMEM_SPEC, compiler_params=_cparams())(packs)


def _swap_halves(g, name, pair_n=None):
    g4, shape, view = _core_blocks(g, pair_n)

    def body(g_ref, o_ref, send_sem, recv_sem):
        x, y, c = _place()
        cp = pltpu.make_async_remote_copy(
            src_ref=view(g_ref, 1 - c), dst_ref=o_ref, send_sem=send_sem, recv_sem=recv_sem,
            device_id=(x, y, 1 - c), device_id_type=MESH)
        cp.start()
        cp.wait()

    return pl.pallas_call(
        body, name=name, out_shape=_S(shape, g.dtype), in_specs=[ANY], out_specs=ANY,
        scratch_shapes=[pltpu.SemaphoreType.DMA, pltpu.SemaphoreType.DMA],
    )(g4)


def _add_halves(g, landed, place, name, pair_n=None):
    _, r, cc = landed.shape
    tr = _tile(r, 512, HALO)
    if pair_n is None:
        g4 = g.reshape(N_CHIP, 2, r, cc)
        g_spec = pl.BlockSpec((None, None, tr, cc), lambda i, q, s: (q, s[0], i, 0))
    else:
        g4 = g
        g_spec = pl.BlockSpec((None, tr, cc), lambda i, q, s: (q, i, s[0]))

    def body(s_ref, g_ref, l_ref, o_ref, land_ref):
        q = pl.program_id(1)
        v = (g_ref[...].astype(F32) + l_ref[...].astype(F32)).astype(BF16)
        o_ref[...] = v

        @pl.when(q == s_ref[1])
        def _():
            land_ref[...] = v

    return pl.pallas_call(
        body, name=name, out_shape=(_S((N_CHIP, r, cc), BF16), _S((N_CHIP, r, cc), BF16)),
        grid_spec=pltpu.PrefetchScalarGridSpec(
            num_scalar_prefetch=1, grid=(r // tr, N_CHIP),
            in_specs=[g_spec,
                      pl.BlockSpec((None, tr, cc), lambda i, q, s: (q, i, 0))],
            out_specs=(pl.BlockSpec((None, tr, cc), lambda i, q, s: (q, i, 0)),
                       pl.BlockSpec((None, tr, cc), lambda i, q, s: (s[1], i, 0)))),
        compiler_params=_cparams(("arbitrary", "arbitrary")),
    )(place, g4, landed)


def _all_reduce_small(pack, name):
    R = pack.shape[0]

    def body(p_ref, o_ref, buf, send_sems, recv_sems):
        x, y, c = _place()
        me = 4 * x + 2 * y + c
        buf[me] = p_ref[...]
        cps = []
        for k in range(N_DEV - 1):
            m = k + 1
            peer = (x ^ (m >> 2), y ^ ((m >> 1) & 1), c ^ (m & 1))
            cps.append(pltpu.make_async_remote_copy(
                src_ref=p_ref, dst_ref=buf.at[me], send_sem=send_sems.at[k], recv_sem=recv_sems.at[k],
                device_id=peer, device_id_type=MESH))
        for cp in cps:
            cp.start()
        for k in range(N_DEV - 1):
            m = k + 1
            peer_idx = 4 * (x ^ (m >> 2)) + 2 * (y ^ ((m >> 1) & 1)) + (c ^ (m & 1))
            pltpu.make_async_remote_copy(
                src_ref=p_ref, dst_ref=buf.at[peer_idx], send_sem=send_sems.at[k], recv_sem=recv_sems.at[k],
                device_id=(x, y, c), device_id_type=MESH).wait_recv()
        for cp in cps:
            cp.wait_send()
        acc = buf[0]
        for k in range(1, N_DEV):
            acc = acc + buf[k]
        o_ref[...] = acc

    return pl.pallas_call(
        body, name=name, out_shape=_S((R, LANES), F32),
        in_specs=[VMEM_SPEC], out_specs=VMEM_SPEC,
        scratch_shapes=[pltpu.VMEM((N_DEV, R, LANES), F32), pltpu.SemaphoreType.DMA((N_DEV - 1,)),
                        pltpu.SemaphoreType.DMA((N_DEV - 1,))],
        compiler_params=_cparams(),
    )(pack)


def _adamw_math(w, g, m, v):
    m = ADAM_B1 * m + (1.0 - ADAM_B1) * g
    v = ADAM_B2 * v + (1.0 - ADAM_B2) * (g * g)
    m_hat = m / (1.0 - ADAM_B1 ** ADAM_STEP)
    v_hat = v / (1.0 - ADAM_B2 ** ADAM_STEP)
    delta = -ADAM_LR * (m_hat / (jnp.sqrt(v_hat) + ADAM_EPS) + ADAM_WD * w)
    return delta, m, v


def _adamw_block(p_ref, w_ref, m_ref, v_ref, g_ref, d_ref, nm_ref, nv_ref):
    g = p_ref[0].astype(F32)
    for q in range(1, N_CHIP):
        g = g + p_ref[q].astype(F32)
    g_ref[...] = g
    d_ref[...], nm_ref[...], nv_ref[...] = _adamw_math(w_ref[...], g, m_ref[...], v_ref[...])


def _adamw_big(parts, w, m, v, name):
    r, cc = w.shape
    tr = _tile(r, 128, HALO)
    body = functools.partial(_adamw_block)

    blk = pl.BlockSpec((tr, cc), lambda i: (i, 0))
    return pl.pallas_call(
        body, name=name, out_shape=tuple(_S((r, cc), F32) for _ in range(4)), grid=(r // tr,),
        in_specs=[pl.BlockSpec((N_CHIP, tr, cc), lambda i: (0, i, 0)), blk, blk, blk],
        out_specs=(blk, blk, blk, blk), compiler_params=_cparams(("parallel",)),
    )(parts, w, m, v)


def _adamw_small(ws, gs, ms, vs, name):
    n = len(ws)

    def body(*refs):
        w_r, g_r, m_r, v_r = refs[:n], refs[n:2 * n], refs[2 * n:3 * n], refs[3 * n:4 * n]
        d_r, nm_r, nv_r = refs[4 * n:5 * n], refs[5 * n:6 * n], refs[6 * n:7 * n]
        for k in range(n):
            d_r[k][...], nm_r[k][...], nv_r[k][...] = _adamw_math(w_r[k][...], g_r[k][...], m_r[k][...], v_r[k][...])

    shapes = tuple(_S(w.shape, F32) for w in ws)
    outs = pl.pallas_call(
        body, name=name, out_shape=shapes * 3,
        in_specs=[VMEM_SPEC] * (4 * n), out_specs=tuple([VMEM_SPEC] * (3 * n)),
        compiler_params=_cparams(),
    )(*ws, *gs, *ms, *vs)
    return outs[:n], outs[n:2 * n], outs[2 * n:]


def _block_diag(w, heads_per_block):
    H, hd, _ = w.shape
    nb = H // heads_per_block
    eye = jnp.eye(heads_per_block, dtype=w.dtype)
    w4 = w.reshape(nb, heads_per_block, hd, hd)
    return jnp.einsum("nhab,hg->nhagb", w4, eye).reshape(nb, heads_per_block * hd, heads_per_block * hd)


def _diag_blocks(bd, heads_per_block, hd):
    nb = bd.shape[0]
    b5 = bd.reshape(nb, heads_per_block, hd, heads_per_block, hd)
    return jnp.stack([b5[:, h, :, h, :] for h in range(heads_per_block)], axis=1).reshape(nb * heads_per_block, hd, hd)


def _as_rows(a):
    if a.ndim == 1:
        return a.reshape(-1, LANES) if a.shape[0] % LANES == 0 else a.reshape(1, -1)
    if a.ndim == 3:
        return a.reshape(-1, LANES) if (a.size % LANES == 0) else a.reshape(a.shape[0] * a.shape[1], a.shape[2])
    return a


def kernel(x, g_mix, w_in, lru_conv_w, lru_conv_b, lru_wa, lru_ba, lru_wx, lru_bx, lru_lambda, lru_w_out, sc_conv_w, sc_w_out, w_o, g_ffn, ffn_w_up, ffn_conv_w, ffn_w_down, g_final, loss_target, m_g_mix, m_w_in, m_lru_conv_w, m_lru_conv_b, m_lru_wa, m_lru_ba, m_lru_wx, m_lru_bx, m_lru_lambda, m_lru_w_out, m_sc_conv_w, m_sc_w_out, m_w_o, m_g_ffn, m_ffn_w_up, m_ffn_conv_w, m_ffn_w_down, m_g_final, v_g_mix, v_w_in, v_lru_conv_w, v_lru_conv_b, v_lru_wa, v_lru_ba, v_lru_wx, v_lru_bx, v_lru_lambda, v_lru_w_out, v_sc_conv_w, v_sc_w_out, v_w_o, v_g_ffn, v_ffn_w_up, v_ffn_conv_w, v_ffn_w_down, v_g_final):
    T, D = x.shape[1], x.shape[2]
    d_lru = lru_lambda.shape[0]
    d_sc = sc_conv_w.shape[1] * N_DEV
    F = ffn_w_down.shape[0] * N_DEV
    H = lru_wa.shape[0]
    assert d_lru == d_sc and H * HEAD_DIM == d_lru
    xs = x.reshape(T, D)
    tgt = loss_target.reshape(T, D)
    my_x, my_y, my_c = _place()
    me = 4 * my_x + 2 * my_y + my_c

    big = [w_in, lru_w_out, sc_w_out, w_o, ffn_w_up, ffn_w_down]
    big_names = ["w_in", "lru_w_out", "sc_w_out", "w_o", "ffn_w_up", "ffn_w_down"]
    place = jnp.stack([my_c, 2 * my_x + my_y, me]).astype(jnp.int32)
    n_in, n_up = w_in.shape[1], ffn_w_up.shape[1]
    paired = [n_in, None, None, None, n_up, None]
    big_bf = [_cast_into_slot(w, place, "cast_" + nm, paired=pn is not None)
              for w, nm, pn in zip(big, big_names, paired)]
    pad_rows = lambda a: jnp.pad(a, ((0, SUB - a.shape[0]), (0, 0)))
    gathered = _all_gather(big_bf + [pad_rows(lru_conv_w), pad_rows(sc_conv_w), pad_rows(ffn_conv_w)],
                           [True] * 6 + [False] * 3,
                           [True, False, False, False, False, False, True, True, True],
                           paired + [None] * 3, "all_gather_first")
    W_in, W_lo, W_so, W_o8, W_up, W_dn8 = gathered[:6]
    full_cols = lambda g, kw: g[:, :kw, :].transpose(1, 0, 2).reshape(kw, -1)
    cw_lru = full_cols(gathered[6], lru_conv_w.shape[0])
    cw_sc = full_cols(gathered[7], sc_conv_w.shape[0])
    cw_ffn = full_cols(gathered[8], ffn_conv_w.shape[0])

    C = _tile(d_lru, C_LRU)
    hpb = C // HEAD_DIM
    wa_bd = _block_diag(lru_wa, hpb).astype(BF16)
    wx_bd = _block_diag(lru_wx, hpb).astype(BF16)
    cb, ba, bx, lam = (a.reshape(1, d_lru) for a in (lru_conv_b, lru_ba, lru_bx, lru_lambda))

    h1 = _rms_fwd(xs, g_mix, "rms_mix")
    k8 = W_up.shape[1] // 8
    wide = 2 * max(n_in, n_up)
    p, ((W_o8,), (W_lo,), (W_so,), (W_up,)) = _mm_nn(
        h1, W_in, out_dtype=BF16, name="mm_in", tn=wide,
        tasks=[_gather_task(W_o8, ici=ALL_ROWS), _gather_task(W_lo, ici=ALL_ROWS), _gather_task(W_so, ici=ALL_ROWS),
               _gather_task(W_up, ici=(0, 4 * k8), pair_n=n_up)])
    hs, yl_pre, ((W_o8,), (W_lo,), (W_so,), (W_up,)) = _lru_fwd(
        p, cw_lru, cb, wa_bd, ba, wx_bd, bx, lam, name="lru_fwd",
        tasks=[_gather_task(W_o8, fwd=ALL_ROWS), _gather_task(W_lo, fwd=ALL_ROWS), _gather_task(W_so, fwd=ALL_ROWS),
               _gather_task(W_up, ici=(4 * k8, 3 * k8), fwd=(0, 4 * k8), pair_n=n_up)])
    ys_pre = _sc_fwd(p, cw_sc, d=d_sc, name="sc_fwd")
    y_lru, ((W_up,),) = _mm_small(
        "nn", yl_pre, None, W_lo, name="mm_lru_out",
        tasks=[_gather_task(W_up, ici=(7 * k8, k8), fwd=(4 * k8, 3 * k8), pair_n=n_up)])
    y_sc, ((W_up,),) = _mm_small("nn", ys_pre, None, W_so, name="mm_sc_out",
                                 tasks=[_gather_task(W_up, fwd=(7 * k8, k8), pair_n=n_up)])
    gate0 = 2 * d_lru + 3 * d_sc
    merged = _merge_fwd(p, y_lru, y_sc, col0=gate0, name="merge_fwd")
    W_o = W_o8.reshape(1, D, D)
    x1, h2 = _mm_nn(merged, W_o, out_dtype=F32, residual=xs, name="mm_o", tm=256, tn=D, norm_g=g_ffn)
    uu, ((W_dn8,),) = _mm_nn(h2, W_up, out_dtype=BF16, name="mm_up", tn=wide,
                             tasks=[_gather_task(W_dn8, ici=ALL_ROWS)])
    act, ((W_dn8,),) = _ffn_act_fwd(uu, cw_ffn, name="ffn_act_fwd", tasks=[_gather_task(W_dn8, fwd=ALL_ROWS)])
    W_dn = W_dn8.reshape(1, F, D)
    x2 = _mm_nn(act, W_dn, out_dtype=F32, residual=x1, name="mm_down", tn=1024, tk=F)
    dx2b, loss_part, dg_final = _loss_head(x2, g_final, tgt, "loss_head")

    def pack_rows(arrs):
        flat = jnp.concatenate([a.reshape(-1) for a in arrs])
        rows = -(-flat.shape[0] // (SUB * LANES)) * SUB
        return jnp.pad(flat, (0, rows * LANES - flat.shape[0])).reshape(rows, LANES)

    def unpack_rows(pack, arrs):
        flat, out, o = pack.reshape(-1), [], 0
        for a in arrs:
            out.append(flat[o:o + a.size].reshape(a.shape))
            o += a.size
        return out

    dact = _mm_nt(dx2b, W_dn, out_dtype=BF16, name="mm_down_dx", tm=512, tko=F // 2, tn=D)
    gW_dn = _mm_tn(act, dx2b, 1, out_dtype=BF16, name="mm_down_dw", tk=1408, tt=2048).reshape(N_DEV, F // N_DEV, D)
    duu, dcw_ffn_g, dcw_ffn_v = _ffn_act_bwd(uu, dact, cw_ffn, name="ffn_act_bwd")
    dh2, ((land_dn,),) = _mm_nt(duu, W_up, out_dtype=BF16, name="mm_up_dx", tn=wide, tasks=[_swap_task(gW_dn)])
    parts_dn = _add_halves(gW_dn, land_dn, place, "rs_add_ffn_w_down")
    gW_up, ((mine_dn,),) = _mm_tn(h2, duu, N_CHIP, out_dtype=BF16, name="mm_up_dw", tk=512, tn=wide, tt=2048,
                                  tasks=[_exchange_task(*parts_dn)])
    dx1b, dg_ffn = _rms_bwd(x1, g_ffn, dh2, dx2b, "rms_ffn_bwd", BF16)
    dmerged, ((land_up,),) = _mm_nt(dx1b, W_o, out_dtype=BF16, name="mm_o_dx", tn=D,
                                    tasks=[_swap_task(gW_up, pair_n=n_up)])
    parts_up, land_up = _add_halves(gW_up, land_up, place, "rs_add_ffn_w_up", pair_n=n_up)
    r8 = parts_up.shape[1] // 8
    gW_o, ((land_up,),) = _mm_tn(merged, dx1b, 1, out_dtype=BF16, name="mm_o_dw", tt=2048,
                                 tasks=[_exchange_task(parts_up, land_up, rows=(0, r8))])
    gW_o = gW_o.reshape(N_DEV, D // N_DEV, D)
    dp, dy_lru, dy_sc, ((land_up,),) = _merge_bwd(
        p, y_lru, y_sc, dmerged, col0=gate0, name="merge_bwd",
        tasks=[_exchange_task(parts_up, land_up, rows=(r8, r8))])
    dyl_pre, ((land_o,),) = _mm_small("nt", None, dy_lru, W_lo, name="mm_lru_out_dx", tasks=[_swap_task(gW_o)])
    parts_o = _add_halves(gW_o, land_o, place, "rs_add_w_o")
    gW_lo = _mm_small("tn", yl_pre, dy_lru, W_lo, name="mm_lru_out_dw")
    dys_pre, ((land_lo,),) = _mm_small("nt", None, dy_sc, W_so, name="mm_sc_out_dx", tasks=[_swap_task(gW_lo)])
    parts_lo = _add_halves(gW_lo, land_lo, place, "rs_add_lru_w_out")
    gW_so = _mm_small("tn", ys_pre, dy_sc, W_so, name="mm_sc_out_dw")
    dp, dcw_sc, ((land_up,),) = _sc_bwd(p, dys_pre, dp, cw_sc, d=d_sc, name="sc_bwd",
                                        tasks=[_exchange_task(parts_up, land_up, rows=(2 * r8, r8))])
    dp, dcw_lru, dcb, dwa_bd, dba, dwx_bd, dbx, dlam, ((land_up,), (mine_o,), (mine_lo,), (land_so,)) = _lru_bwd(
        p, hs, dyl_pre, dp, cw_lru, cb, wa_bd, ba, wx_bd, bx, lam, name="lru_bwd",
        tasks=[_exchange_task(parts_up, land_up, rows=(3 * r8, 3 * r8)), _exchange_task(*parts_o),
               _exchange_task(*parts_lo), _swap_task(gW_so)])
    parts_so = _add_halves(gW_so, land_so, place, "rs_add_sc_w_out")

    dwa = _diag_blocks(dwa_bd, hpb, HEAD_DIM)
    dwx = _diag_blocks(dwx_bd, hpb, HEAD_DIM)
    dcw_ffn = jnp.concatenate([dcw_ffn_g, dcw_ffn_v], axis=1)
    rep_grads = [dcb, dwa, dba, dwx, dbx, dlam, dg_ffn, dg_final]
    small_full = rep_grads + [dcw_lru, dcw_sc, dcw_ffn]
    gW_in, ((mine_up,), (mine_so,), (packs,)) = _mm_tn(
        h1, dp, N_CHIP, out_dtype=BF16, name="mm_in_dw", tk=512, tn=wide, tt=2048,
        tasks=[_exchange_task(parts_up, land_up, rows=(6 * r8, 2 * r8)), _exchange_task(*parts_so),
               _bcast_task(pack_rows(small_full))])
    land_in = _swap_halves(gW_in, "rs_swap_w_in", pair_n=n_in)
    parts_in = _add_halves(gW_in, land_in, place, "rs_add_w_in", pair_n=n_in)
    dh1, ((mine_in,),) = _mm_nt(dp, W_in, out_dtype=BF16, name="mm_in_dx", tn=wide,
                                tasks=[_exchange_task(*parts_in)])
    grad_x, dg_mix = _rms_bwd(xs, g_mix, dh1, dx1b, "rms_mix_bwd", F32)

    mine = [mine_in, mine_lo, mine_so, mine_o, mine_up, mine_dn]
    big_m = [m_w_in, m_lru_w_out, m_sc_w_out, m_w_o, m_ffn_w_up, m_ffn_w_down]
    big_v = [v_w_in, v_lru_w_out, v_sc_w_out, v_w_o, v_ffn_w_up, v_ffn_w_down]
    big_out = {nm: _adamw_big(pt, w, m, v, "adamw_" + nm)
               for nm, pt, w, m, v in zip(big_names, mine, big, big_m, big_v)}

    (scb, swa, sba, swx, sbx, slam, sg_ffn, sg_final, scw_lru, scw_sc, scw_ffn) = unpack_rows(
        _sum_packs(packs, "sum_small"), small_full)
    (sg_mix,) = unpack_rows(_all_reduce_small(pack_rows([dg_mix]), "all_reduce_g_mix"), [dg_mix])

    def my_cols(a):
        n = a.shape[1] // N_DEV
        return lax.dynamic_slice_in_dim(a, me * n, n, axis=1)

    small_names = ["g_mix", "lru_conv_w", "lru_conv_b", "lru_wa", "lru_ba", "lru_wx", "lru_bx", "lru_lambda",
                   "sc_conv_w", "g_ffn", "ffn_conv_w", "g_final"]
    small_w = [g_mix, lru_conv_w, lru_conv_b, lru_wa, lru_ba, lru_wx, lru_bx, lru_lambda, sc_conv_w, g_ffn,
               ffn_conv_w, g_final]
    small_m = [m_g_mix, m_lru_conv_w, m_lru_conv_b, m_lru_wa, m_lru_ba, m_lru_wx, m_lru_bx, m_lru_lambda,
               m_sc_conv_w, m_g_ffn, m_ffn_conv_w, m_g_final]
    small_v = [v_g_mix, v_lru_conv_w, v_lru_conv_b, v_lru_wa, v_lru_ba, v_lru_wx, v_lru_bx, v_lru_lambda,
               v_sc_conv_w, v_g_ffn, v_ffn_conv_w, v_g_final]
    small_g = [sg_mix.reshape(D), my_cols(scw_lru), scb.reshape(d_lru), swa, sba.reshape(d_lru), swx,
               sbx.reshape(d_lru), slam.reshape(d_lru), my_cols(scw_sc), sg_ffn.reshape(D), my_cols(scw_ffn),
               sg_final.reshape(D)]
    sd, snm, snv = _adamw_small([_as_rows(a) for a in small_w], [_as_rows(a) for a in small_g],
                                [_as_rows(a) for a in small_m], [_as_rows(a) for a in small_v], "adamw_small")
    small_out = {nm: (g, d.reshape(w.shape), nm_.reshape(w.shape), nv_.reshape(w.shape))
                 for nm, w, g, d, nm_, nv_ in zip(small_names, small_w, small_g, sd, snm, snv)}

    loss = lax.psum(loss_part[0, 0], AXES)
    order = ["g_mix", "w_in", "lru_conv_w", "lru_conv_b", "lru_wa", "lru_ba", "lru_wx", "lru_bx", "lru_lambda",
             "lru_w_out", "sc_conv_w", "sc_w_out", "w_o", "g_ffn", "ffn_w_up", "ffn_conv_w", "ffn_w_down", "g_final"]
    res = {**big_out, **small_out}
    return (loss, grad_x.reshape(x.shape),
            *[res[nm][0] for nm in order], *[res[nm][1] for nm in order],
            *[res[nm][2] for nm in order], *[res[nm][3] for nm in order])
```
